```python
import jax, jax.numpy as jnp
from jax import lax
import numpy as np

D_MODEL = 1024
BATCH = 32
SEQ = 256
DEPTH = 2
DEC_BATCH = 4
DEC_SEQ = 2048
PAST_LEN = 512

GRID_W = 64
ROPE_BASE = 10000.0
EPS = 1e-6
NEG_INF = -1e30
F32 = jnp.float32
N_HEADS_A = 8
N_KV_A = 2
HD_A = 64
GQA_GROUP = N_HEADS_A // N_KV_A
WINDOW = 128
BAND_BLK = 128
N_HEADS_B = 8
QK_NOPE_B = 64
QK_ROPE_B = 32
V_HD_B = 64
Q_RANK_B = 384
KV_RANK_B = 256
MLA_SCALE = (QK_NOPE_B + QK_ROPE_B) ** -0.5
N_HEADS_C = 4
DK_C = 128
DV_C = 128
CONV_K = 3
CHUNK_C = 64
Q_BLOCK = 128
W_A = N_HEADS_A * HD_A
W_B = N_HEADS_B * V_HD_B
W_C = N_HEADS_C * DV_C
QKV_C = 2 * N_HEADS_C * DK_C + W_C
IN_SIZES = (W_A, N_KV_A * HD_A, N_KV_A * HD_A, W_A, Q_RANK_B, KV_RANK_B, QK_ROPE_B, W_B, QKV_C, 2 * N_HEADS_C, 2 * N_HEADS_C, W_C, 3 * D_MODEL)
IN_WIDTH = sum(IN_SIZES)

kernel_name = 'hybrid_diffusion_swa_mla_gdn_step'


def _rmsnorm(x, g):
    xf = x.astype(F32)
    y = xf * lax.rsqrt(jnp.mean(xf * xf, axis=-1, keepdims=True) + EPS)
    return (y * g.astype(F32)).astype(x.dtype)


def _l2norm(x):
    return x * lax.rsqrt(jnp.sum(x * x, axis=-1, keepdims=True) + EPS)


def _split_inputs(u):
    points = [int(p) for p in np.cumsum(IN_SIZES)[:-1]]
    return jnp.split(u, points, axis=-1)


def _axial_rope(n_tokens, rot_dim):
    rows = n_tokens // GRID_W
    row = jnp.repeat(jnp.arange(rows), GRID_W).astype(F32)
    col = jnp.tile(jnp.arange(GRID_W), rows).astype(F32)
    n_pairs = rot_dim // 4
    inv = ROPE_BASE ** (-jnp.arange(n_pairs, dtype=F32) / n_pairs)
    ang = jnp.concatenate([row[:, None] * inv, col[:, None] * inv], axis=-1)
    return jnp.cos(ang), jnp.sin(ang)


def _apply_rope(x, cos, sin):
    xf = x.astype(F32).reshape(*x.shape[:-1], -1, 2)
    x1, x2 = xf[..., 0], xf[..., 1]
    c, s = cos[:, None, :], sin[:, None, :]
    out = jnp.stack([x1 * c - x2 * s, x1 * s + x2 * c], axis=-1).reshape(x.shape)
    return out.astype(x.dtype)


def _attend_block(q, k, v, sink, scale):
    s = jnp.einsum('bqkgd,bskd->bkgqs', q, k).astype(F32) * scale
    m = s.max(-1, keepdims=True)
    if sink is None:
        e = jnp.exp(s - m)
        p = e / e.sum(-1, keepdims=True)
    else:
        sk = sink.astype(F32).reshape(1, k.shape[2], q.shape[3], 1, 1)
        m = jnp.maximum(m, sk)
        e = jnp.exp(s - m)
        p = e / (e.sum(-1, keepdims=True) + jnp.exp(sk - m))
    return jnp.einsum('bkgqs,bskd->bqkgd', p.astype(v.dtype), v)


def _dense_attention(q, k, v, sink, scale):
    B, Q = q.shape[:2]
    nb = Q // Q_BLOCK
    qb = jnp.moveaxis(q.reshape(B, nb, Q_BLOCK, *q.shape[2:]), 1, 0)
    out = lax.map(lambda qi: _attend_block(qi, k, v, sink, scale), qb)
    return jnp.moveaxis(out, 0, 1).reshape(B, Q, *out.shape[3:])


def _band_context_attention(q, k, v, k_ctx, v_ctx, sink, scale):
    B, T, K, G, d = q.shape
    nb = T // BAND_BLK
    qb = q.reshape(B, nb, BAND_BLK, K, G, d)

    def neighbours(a):
        ap = jnp.pad(a, ((0, 0), (BAND_BLK, BAND_BLK), (0, 0), (0, 0)))
        ap = ap.reshape(B, nb + 2, BAND_BLK, K, a.shape[-1])
        return jnp.concatenate([ap[:, :-2], ap[:, 1:-1], ap[:, 2:]], axis=2)

    kb, vb = neighbours(k), neighbours(v)
    n_loc = 3 * BAND_BLK
    qi = jnp.arange(BAND_BLK)[:, None]
    kj = jnp.arange(n_loc)[None, :]
    in_window = jnp.abs(BAND_BLK + qi - kj) <= WINDOW
    key_pos = (jnp.arange(nb)[:, None] - 1) * BAND_BLK + jnp.arange(n_loc)[None, :]
    in_range = (key_pos >= 0) & (key_pos < T)
    valid = in_window[None] & in_range[:, None, :]
    s_loc = jnp.einsum('bnqkgd,bnskd->bnkgqs', qb, kb).astype(F32) * scale
    s_loc = jnp.where(valid[None, :, None, None], s_loc, NEG_INF)
    s_ctx = jnp.einsum('bnqkgd,bckd->bnkgqc', qb, k_ctx).astype(F32) * scale
    s = jnp.concatenate([s_loc, s_ctx], axis=-1)
    sk = sink.astype(F32).reshape(1, 1, K, G, 1, 1)
    m = jnp.maximum(s.max(-1, keepdims=True), sk)
    e = jnp.exp(s - m)
    p = (e / (e.sum(-1, keepdims=True) + jnp.exp(sk - m))).astype(v.dtype)
    o = (jnp.einsum('bnkgqs,bnskd->bnqkgd', p[..., :n_loc], vb)
         + jnp.einsum('bnkgqc,bckd->bnqkgd', p[..., n_loc:], v_ctx))
    return o.reshape(B, T, K, G, d)


def _short_conv(x, w):
    pad = CONV_K // 2
    return lax.conv_general_dilated(x, w[:, None, :].astype(x.dtype), window_strides=(1,),
                                    padding=[(pad, pad)], dimension_numbers=('NWC', 'WIO', 'NWC'),
                                    feature_group_count=x.shape[-1])


def _gated_delta_chunked(q, k, v, g, beta, s0):
    B, T, H, _ = q.shape
    n = T // CHUNK_C

    def chunks(a):
        return jnp.moveaxis(a.reshape(B, n, CHUNK_C, H, *a.shape[3:]), 3, 2)

    q, k, v, g, beta = chunks(q), chunks(k), chunks(v), chunks(g), chunks(beta)
    g = jnp.cumsum(g, axis=-1)
    kb = k * beta[..., None]
    vb = v * beta[..., None]
    tri = jnp.tril(jnp.ones((CHUNK_C, CHUNK_C), bool))
    strict = jnp.tril(jnp.ones((CHUNK_C, CHUNK_C), bool), -1)
    diff = g[..., :, None] - g[..., None, :]
    decay = jnp.where(tri, jnp.exp(jnp.where(tri, diff, 0.0)), 0.0)
    low = jnp.where(strict, jnp.einsum('bnhid,bnhjd->bnhij', kb, k) * decay, 0.0)
    a_mat = low + jnp.eye(CHUNK_C, dtype=F32)
    u = lax.linalg.triangular_solve(a_mat, vb, left_side=True, lower=True)
    w = lax.linalg.triangular_solve(a_mat, kb * jnp.exp(g)[..., None], left_side=True, lower=True)
    attn = jnp.einsum('bnhid,bnhjd->bnhij', q, k) * decay

    def step(s, inp):
        qc, kc, uc, wc, ac, gc = inp
        v_new = uc - jnp.einsum('bhcd,bhde->bhce', wc, s)
        o = (jnp.einsum('bhcd,bhde->bhce', qc * jnp.exp(gc)[..., None], s)
             + jnp.einsum('bhij,bhje->bhie', ac, v_new))
        g_last = gc[..., -1]
        s = (s * jnp.exp(g_last)[..., None, None]
             + jnp.einsum('bhcd,bhce->bhde', kc * jnp.exp(g_last[..., None] - gc)[..., None], v_new))
        return s, o

    xs = tuple(jnp.moveaxis(a, 1, 0) for a in (q, k, u, w, attn, g))
    s_final, o = lax.scan(step, s0.astype(F32), xs)
    o = jnp.swapaxes(jnp.moveaxis(o, 0, 1), 2, 3).reshape(B, T, H, v.shape[-1])
    return o, s_final


def _gdn_inputs(c_qkv, c_a, c_b, p):
    B, T, _ = c_qkv.shape
    x = jax.nn.silu(_short_conv(c_qkv, p['gdn_conv'])).astype(F32)
    nq = N_HEADS_C * DK_C
    q = _l2norm(x[..., :nq].reshape(B, T, N_HEADS_C, DK_C)) * (DK_C ** -0.5)
    k = _l2norm(x[..., nq:2 * nq].reshape(B, T, N_HEADS_C, DK_C))
    v = x[..., 2 * nq:].reshape(B, T, N_HEADS_C, DV_C)
    a = c_a.astype(F32).reshape(B, T, 2, N_HEADS_C)
    g = -jnp.exp(p['gdn_a_log'].astype(F32)) * jax.nn.softplus(a + p['gdn_dt_bias'].astype(F32))
    beta = jax.nn.sigmoid(c_b.astype(F32).reshape(B, T, 2, N_HEADS_C))
    return q, k, v, g, beta


def _gdn_bidirectional(q, k, v, g, beta, s_fwd, s_bwd, norm_g):
    o_f, st_f = _gated_delta_chunked(q, k, v, g[:, :, 0], beta[:, :, 0], s_fwd)
    fl = lambda a: jnp.flip(a, axis=1)
    o_b, st_b = _gated_delta_chunked(fl(q), fl(k), fl(v), fl(g[:, :, 1]), fl(beta[:, :, 1]), s_bwd)
    o = _rmsnorm(o_f + fl(o_b), norm_g)
    return o.reshape(o.shape[0], o.shape[1], W_C), jnp.stack([st_f, st_b], axis=1)


def _mla_queries(b_cq, p):
    B, T, _ = b_cq.shape
    q = (_rmsnorm(b_cq, p['mla_q_norm']) @ p['mla_w_uq']).reshape(B, T, N_HEADS_B, QK_NOPE_B + QK_ROPE_B)
    return q[..., :QK_NOPE_B], q[..., QK_NOPE_B:]


def _mla_keys_values(ckv_n, kpe, w_ukv):
    B, T, _ = ckv_n.shape
    kv = (ckv_n @ w_ukv).reshape(B, T, N_HEADS_B, QK_NOPE_B + V_HD_B)
    k_pe = jnp.broadcast_to(kpe[:, :, None, :], (B, T, N_HEADS_B, QK_ROPE_B)).astype(kv.dtype)
    return jnp.concatenate([kv[..., :QK_NOPE_B], k_pe], axis=-1), kv[..., QK_NOPE_B:]


def _modulated_inputs(x, mod, p):
    shift, scale, gate = jnp.split(mod, 3, axis=-1)
    h = _rmsnorm(x, p['norm_g']) * (1.0 + scale) + shift
    return _split_inputs(h @ p['w_in']), gate


def _merge_branches(x, o_a, o_b, o_c, z_a, z_b, z_c, gates, gate, p):
    dt = x.dtype
    p_a = (o_a.astype(dt) * jax.nn.silu(z_a)) @ p['w_branch_a']
    p_b = (o_b.astype(dt) * jax.nn.silu(z_b)) @ p['w_branch_b']
    p_c = (o_c.astype(dt) * jax.nn.silu(z_c)) @ p['w_branch_c']
    g_a, g_b, g_c = jnp.split(jax.nn.sigmoid(gates), 3, axis=-1)
    y = (g_a * p_a + g_b * p_b + g_c * p_c) @ p['w_out']
    return x + gate * y


def _layer_context(x, mod, p):
    B, T, _ = x.shape
    (a_q, a_k, a_v, z_a, b_cq, b_ckv, b_kpe, z_b, c_qkv, c_a, c_b, z_c, gates), gate = _modulated_inputs(x, mod, p)
    k_a = a_k.reshape(B, T, N_KV_A, HD_A)
    v_a = a_v.reshape(B, T, N_KV_A, HD_A)
    q_a = a_q.reshape(B, T, N_KV_A, GQA_GROUP, HD_A)
    o_a = _dense_attention(q_a, k_a, v_a, p['attn_sink'], HD_A ** -0.5).reshape(B, T, W_A)
    q_nope, q_pe = _mla_queries(b_cq, p)
    ckv_n = _rmsnorm(b_ckv, p['mla_kv_norm'])
    k_b, v_b = _mla_keys_values(ckv_n, b_kpe, p['mla_w_ukv'])
    q_b = jnp.concatenate([q_nope, q_pe], axis=-1)[:, :, :, None, :]
    o_b = _dense_attention(q_b, k_b, v_b, None, MLA_SCALE).reshape(B, T, W_B)
    q_c, k_c, v_c, g_c, beta_c = _gdn_inputs(c_qkv, c_a, c_b, p)
    s0 = jnp.zeros((B, N_HEADS_C, DK_C, DV_C), F32)
    o_c, st = _gdn_bidirectional(q_c, k_c, v_c, g_c, beta_c, s0, s0, p['gdn_norm'])
    y = _merge_branches(x, o_a, o_b, o_c, z_a, z_b, z_c, gates, gate, p)
    return y, (k_a, v_a, ckv_n, b_kpe, st)


def _layer_latent(x, mod, p, rope_a, rope_b, k_a_ctx, v_a_ctx, ckv_ctx, kpe_ctx, st_ctx):
    B, T, _ = x.shape
    (a_q, a_k, a_v, z_a, b_cq, b_ckv, b_kpe, z_b, c_qkv, c_a, c_b, z_c, gates), gate = _modulated_inputs(x, mod, p)
    q_a = _apply_rope(a_q.reshape(B, T, N_HEADS_A, HD_A), *rope_a).reshape(B, T, N_KV_A, GQA_GROUP, HD_A)
    k_a = _apply_rope(a_k.reshape(B, T, N_KV_A, HD_A), *rope_a)
    v_a = a_v.reshape(B, T, N_KV_A, HD_A)
    o_a = _band_context_attention(q_a, k_a, v_a, k_a_ctx, v_a_ctx, p['attn_sink'], HD_A ** -0.5).reshape(B, T, W_A)
    q_nope, q_pe = _mla_queries(b_cq, p)
    q_pe = _apply_rope(q_pe, *rope_b)
    ckv_n = _rmsnorm(b_ckv, p['mla_kv_norm'])
    kpe = _apply_rope(b_kpe[:, :, None, :], *rope_b)[:, :, 0, :]
    k_l, v_l = _mla_keys_values(ckv_n, kpe, p['mla_w_ukv'])
    k_c, v_c = _mla_keys_values(ckv_ctx, kpe_ctx, p['mla_w_ukv'])
    k_b = jnp.concatenate([k_l, k_c], axis=1)
    v_b = jnp.concatenate([v_l, v_c], axis=1)
    q_b = jnp.concatenate([q_nope, q_pe], axis=-1)[:, :, :, None, :]
    o_b = _dense_attention(q_b, k_b, v_b, None, MLA_SCALE).reshape(B, T, W_B)
    q_c, k_c2, v_c2, g_c, beta_c = _gdn_inputs(c_qkv, c_a, c_b, p)
    o_c, _ = _gdn_bidirectional(q_c, k_c2, v_c2, g_c, beta_c, st_ctx[:, 0], st_ctx[:, 1], p['gdn_norm'])
    return _merge_branches(x, o_a, o_b, o_c, z_a, z_b, z_c, gates, gate, p)


def setup_inputs(seed: int = 0) -> dict:
    key = jax.random.key(seed)
    ks = iter(jax.random.split(key, 40))

    def nrm(shape, s=1.0):
        return jax.random.normal(next(ks), shape, F32) * s

    def gain(shape):
        return 1.0 + nrm(shape, 0.02)

    dt = jnp.exp(jax.random.uniform(next(ks), (DEPTH, 2, N_HEADS_C), F32,
                                    minval=float(np.log(1e-3)), maxval=float(np.log(0.1))))
    return {
        'x_prompt': nrm((BATCH, SEQ, D_MODEL)),
        'x_sample': nrm((DEC_BATCH, DEC_SEQ, D_MODEL)),
        'cache_attn_k': nrm((DEC_BATCH, DEPTH, PAST_LEN, N_KV_A, HD_A)),
        'cache_attn_v': nrm((DEC_BATCH, DEPTH, PAST_LEN, N_KV_A, HD_A)),
        'cache_mla_ckv': nrm((DEC_BATCH, DEPTH, PAST_LEN, KV_RANK_B)),
        'cache_mla_kpe': nrm((DEC_BATCH, DEPTH, PAST_LEN, QK_ROPE_B)),
        'state_gdn': nrm((DEC_BATCH, DEPTH, 2, N_HEADS_C, DK_C, DV_C), 0.3),
        'c': nrm((DEC_BATCH, D_MODEL)),
        'c_ctx': nrm((D_MODEL,)),
        'norm_g': gain((DEPTH, D_MODEL)),
        'w_ada': nrm((DEPTH, D_MODEL, 3 * D_MODEL), 0.5 * D_MODEL ** -0.5),
        'b_ada': nrm((DEPTH, 3 * D_MODEL), 0.02),
        'w_in': nrm((DEPTH, D_MODEL, IN_WIDTH), D_MODEL ** -0.5),
        'attn_sink': nrm((DEPTH, N_HEADS_A)),
        'mla_q_norm': gain((DEPTH, Q_RANK_B)),
        'mla_w_uq': nrm((DEPTH, Q_RANK_B, N_HEADS_B * (QK_NOPE_B + QK_ROPE_B)), Q_RANK_B ** -0.5),
        'mla_kv_norm': gain((DEPTH, KV_RANK_B)),
        'mla_w_ukv': nrm((DEPTH, KV_RANK_B, N_HEADS_B * (QK_NOPE_B + V_HD_B)), KV_RANK_B ** -0.5),
        'gdn_conv': nrm((DEPTH, CONV_K, QKV_C), CONV_K ** -0.5),
        'gdn_a_log': jnp.log(jax.random.uniform(next(ks), (DEPTH, 2, N_HEADS_C), F32, minval=1.0, maxval=16.0)),
        'gdn_dt_bias': dt + jnp.log(-jnp.expm1(-dt)),
        'gdn_norm': gain((DEPTH, DV_C)),
        'w_branch_a': nrm((DEPTH, W_A, D_MODEL), W_A ** -0.5),
        'w_branch_b': nrm((DEPTH, W_B, D_MODEL), W_B ** -0.5),
        'w_branch_c': nrm((DEPTH, W_C, D_MODEL), W_C ** -0.5),
        'w_out': nrm((DEPTH, D_MODEL, D_MODEL), D_MODEL ** -0.5),
        'final_norm_g': gain((D_MODEL,)),
    }


def reference(x_prompt, x_sample, cache_attn_k, cache_attn_v, cache_mla_ckv, cache_mla_kpe, state_gdn,
              c, c_ctx, norm_g, w_ada, b_ada, w_in, attn_sink, mla_q_norm, mla_w_uq, mla_kv_norm,
              mla_w_ukv, gdn_conv, gdn_a_log, gdn_dt_bias, gdn_norm, w_branch_a, w_branch_b,
              w_branch_c, w_out, final_norm_g):
    t_lat = x_sample.shape[1]
    rope_a = _axial_rope(t_lat, HD_A)
    rope_b = _axial_rope(t_lat, QK_ROPE_B)
    y_p, y_s = x_prompt, x_sample
    ks, vs, ckvs, kpes, sts = [], [], [], [], []
    for l in range(DEPTH):
        p = {'norm_g': norm_g[l], 'w_in': w_in[l], 'attn_sink': attn_sink[l],
             'mla_q_norm': mla_q_norm[l], 'mla_w_uq': mla_w_uq[l], 'mla_kv_norm': mla_kv_norm[l],
             'mla_w_ukv': mla_w_ukv[l], 'gdn_conv': gdn_conv[l], 'gdn_a_log': gdn_a_log[l],
             'gdn_dt_bias': gdn_dt_bias[l], 'gdn_norm': gdn_norm[l], 'w_branch_a': w_branch_a[l],
             'w_branch_b': w_branch_b[l], 'w_branch_c': w_branch_c[l], 'w_out': w_out[l]}
        mod_ctx = (jax.nn.silu(c_ctx) @ w_ada[l] + b_ada[l])[None, None, :]
        mod_lat = (jax.nn.silu(c) @ w_ada[l] + b_ada[l])[:, None, :]
        y_p, (k_l, v_l, ckv_l, kpe_l, st_l) = _layer_context(y_p, mod_ctx, p)
        ks.append(k_l)
        vs.append(v_l)
        ckvs.append(ckv_l)
        kpes.append(kpe_l)
        sts.append(st_l)
        y_s = _layer_latent(y_s, mod_lat, p, rope_a, rope_b, cache_attn_k[:, l], cache_attn_v[:, l],
                            cache_mla_ckv[:, l], cache_mla_kpe[:, l], state_gdn[:, l])
    y_prompt = _rmsnorm(y_p, final_norm_g)
    y_sample = _rmsnorm(y_s, final_norm_g)
    new_attn_k = jnp.stack(ks, axis=1)
    new_attn_v = jnp.stack(vs, axis=1)
    new_mla_ckv = jnp.stack(ckvs, axis=1)
    new_mla_kpe = jnp.stack(kpes, axis=1)
    new_state_gdn = jnp.stack(sts, axis=1)
    return (y_prompt, y_sample, new_attn_k, new_attn_v, new_mla_ckv, new_mla_kpe, new_state_gdn)
```

```python
import functools

import jax
import jax.numpy as jnp
from jax import lax
from jax.experimental import pallas as pl
from jax.experimental.pallas import tpu as pltpu

F32 = jnp.float32
BF16 = jnp.bfloat16

D_MODEL = 1024
GRID_W = 64
ROPE_BASE = 10000.0
EPS = 1e-6
NEG_INF = -1e30
N_HEADS_A = 8
N_KV_A = 2
HD_A = 64
GQA_GROUP = N_HEADS_A // N_KV_A
BAND_BLK = 128
N_HEADS_B = 8
QK_NOPE_B = 64
QK_ROPE_B = 32
V_HD_B = 64
Q_RANK_B = 384
KV_RANK_B = 256
MLA_SCALE = (QK_NOPE_B + QK_ROPE_B) ** -0.5
N_HEADS_C = 4
DK_C = 128
DV_C = 128
CHUNK = 64
W_A = N_HEADS_A * HD_A
W_B = N_HEADS_B * V_HD_B
W_C = N_HEADS_C * DV_C
QKV_C = 2 * N_HEADS_C * DK_C + W_C
IN_SIZES = (W_A, N_KV_A * HD_A, N_KV_A * HD_A, W_A, Q_RANK_B, KV_RANK_B, QK_ROPE_B, W_B, QKV_C,
            2 * N_HEADS_C, 2 * N_HEADS_C, W_C, 3 * D_MODEL)

LANE = 128
TM = 256
CPT = TM // CHUNK
HALO = 8
MLA_HW = 128

P_QKV = 0
P_Z = P_QKV + 768
P_GATES = P_Z + 1536
P_CQ = P_GATES + 3 * D_MODEL
P_CKV = P_CQ + Q_RANK_B
P_SMALL = P_CKV + KV_RANK_B
P_CQKV = P_SMALL + LANE
P_END = P_CQKV + QKV_C
S_KPE = 64
S_A = 96
S_B = 104


def _sigmoid(x):
    return 1.0 / (1.0 + jnp.exp(-x))


def _silu(x):
    return x * _sigmoid(x)


def _softplus(x):
    return jnp.maximum(x, 0.0) + jnp.log(1.0 + jnp.exp(-jnp.abs(x)))


def _dot(a, b):
    return jnp.dot(a, b, preferred_element_type=F32)


def _dot_nt(a, b):
    return lax.dot_general(a, b, (((1,), (1,)), ((), ())), preferred_element_type=F32)


def _dot_tn(a, b):
    return lax.dot_general(a, b, (((0,), (0,)), ((), ())), preferred_element_type=F32)


def _dot_exact(a, b):
    return jnp.dot(a, b, preferred_element_type=F32, precision=lax.Precision.HIGHEST)


def _rope(x, c, s):
    n = x.shape[-1]
    lane = lax.broadcasted_iota(jnp.int32, x.shape, 1)
    swapped = jnp.where(lane % 2 == 0, pltpu.roll(x, n - 1, 1), pltpu.roll(x, 1, 1))
    return x * c + swapped * s


def _mod_kernel(cond_ref, w_ref, b_ref, out_ref):
    cnd = cond_ref[...]
    out_ref[0] = _dot(_silu(cnd).astype(BF16), w_ref[0].astype(BF16)) + b_ref[0]


def _modulation(cond8, w_ada, b_ada):
    depth = w_ada.shape[0]
    tn = 768
    return pl.pallas_call(
        _mod_kernel,
        grid=(depth, 3 * D_MODEL // tn),
        in_specs=[pl.BlockSpec((8, D_MODEL), lambda l, n: (0, 0)),
                  pl.BlockSpec((1, D_MODEL, tn), lambda l, n: (l, 0, n)),
                  pl.BlockSpec((1, 1, tn), lambda l, n: (l, 0, n))],
        out_specs=pl.BlockSpec((1, 8, tn), lambda l, n: (l, 0, n)),
        out_shape=jax.ShapeDtypeStruct((depth, 8, 3 * D_MODEL), F32),
        name="adaln_mod",
    )(cond8, w_ada, b_ada.reshape(depth, 1, 3 * D_MODEL))


def _inproj_kernel(*refs, rope, ctx):
    it = iter(refs)
    x_ref, mod_ref, ng_ref, wp_ref, wab_ref, wuq_ref, wukv_ref, qn_ref, kvn_ref = (next(it) for _ in range(9))
    if rope:
        ca_ref, sa_ref, cb_ref, sb_ref = (next(it) for _ in range(4))
    qkv_ref, z_ref, gates_ref, qb_ref, kvb_ref, small_ref, kpe_ref, cqkv_ref, abt_ref = (next(it) for _ in range(9))
    if ctx:
        kva_ref, ckvn_ref = (next(it) for _ in range(2))

    x = x_ref[...]
    mod = mod_ref[0]
    shift, scale = mod[:, :D_MODEL], mod[:, D_MODEL:2 * D_MODEL]
    xn = x * lax.rsqrt(jnp.mean(x * x, axis=-1, keepdims=True) + EPS) * ng_ref[...]
    hb = (xn * (1.0 + scale) + shift).astype(BF16)

    def mm(lo, hi):
        return _dot(hb, wp_ref[:, lo:hi])

    r = mm(P_QKV, P_QKV + 768)
    qscale = HD_A ** -0.5
    if rope:
        ca, sa = ca_ref[...], sa_ref[...]
        for t in range(5):
            seg = _rope(r[:, t * LANE:(t + 1) * LANE], ca, sa)
            if t < 4:
                seg = seg * qscale
            qkv_ref[:, t * LANE:(t + 1) * LANE] = seg.astype(BF16)
        qkv_ref[:, 640:768] = r[:, 640:768].astype(BF16)
    else:
        qkv_ref[:, :W_A] = (r[:, :W_A] * qscale).astype(BF16)
        qkv_ref[:, W_A:] = r[:, W_A:].astype(BF16)
    if ctx:
        kva_ref[...] = r[:, W_A:]

    for t in range(3):
        z_ref[:, t * 512:(t + 1) * 512] = mm(P_Z + t * 512, P_Z + (t + 1) * 512).astype(BF16)
    for t in range(6):
        gates_ref[:, t * 512:(t + 1) * 512] = mm(P_GATES + t * 512, P_GATES + (t + 1) * 512).astype(BF16)

    r = mm(P_CQ, P_CQ + Q_RANK_B)
    qn = r * lax.rsqrt(jnp.mean(r * r, axis=-1, keepdims=True) + EPS) * qn_ref[...]
    q = _dot(qn.astype(BF16), wuq_ref[...])
    if rope:
        cb, sb = cb_ref[...], sb_ref[...]
        for h in range(N_HEADS_B):
            seg = _rope(q[:, h * MLA_HW:(h + 1) * MLA_HW], cb, sb) * MLA_SCALE
            qb_ref[:, h * MLA_HW:(h + 1) * MLA_HW] = seg.astype(BF16)
    else:
        qb_ref[...] = (q * MLA_SCALE).astype(BF16)

    r = mm(P_CKV, P_CKV + KV_RANK_B)
    cn = r * lax.rsqrt(jnp.mean(r * r, axis=-1, keepdims=True) + EPS) * kvn_ref[...]
    if ctx:
        ckvn_ref[...] = cn
    kvb_ref[...] = _dot(cn.astype(BF16), wukv_ref[...]).astype(BF16)

    r = mm(P_SMALL, P_SMALL + LANE)
    small_ref[...] = r
    kp = _rope(r, cb, sb) if rope else r
    lane = lax.broadcasted_iota(jnp.int32, r.shape, 1)
    kpe_ref[...] = jnp.where((lane >= S_KPE) & (lane < S_KPE + QK_ROPE_B), kp, 0.0).astype(BF16)

    for t in range(3):
        cqkv_ref[:, t * 512:(t + 1) * 512] = mm(P_CQKV + t * 512, P_CQKV + (t + 1) * 512)

    for c in range(CPT):
        abt_ref[c] = _dot_nt(wab_ref[...], hb[c * CHUNK:(c + 1) * CHUNK])


def _inproj(x2d, mod, mod_row_fn, ng, wp, wab, wuq, wukv, qn, kvn, rope_tabs, tiles_per_seq, ctx):
    t = x2d.shape[0]
    nt = t // TM
    rope = rope_tabs is not None
    const2 = lambda i: (0, 0)
    row = lambda i: (i, 0)
    in_specs = [pl.BlockSpec((TM, D_MODEL), row),
                pl.BlockSpec((1, 1, 3 * D_MODEL), lambda i: (mod_row_fn(i), 0, 0)),
                pl.BlockSpec((1, D_MODEL), const2),
                pl.BlockSpec((D_MODEL, P_END), const2),
                pl.BlockSpec((16, D_MODEL), const2),
                pl.BlockSpec((Q_RANK_B, N_HEADS_B * MLA_HW), const2),
                pl.BlockSpec((KV_RANK_B, N_HEADS_B * MLA_HW), const2),
                pl.BlockSpec((1, Q_RANK_B), const2),
                pl.BlockSpec((1, KV_RANK_B), const2)]
    args = [x2d, mod, ng, wp, wab, wuq, wukv, qn, kvn]
    if rope:
        pos = lambda i: (i % tiles_per_seq, 0)
        in_specs += [pl.BlockSpec((TM, LANE), pos)] * 4
        args += list(rope_tabs)
    out_shape = [jax.ShapeDtypeStruct((t, 768), BF16),
                 jax.ShapeDtypeStruct((t, 1536), BF16),
                 jax.ShapeDtypeStruct((t, 3 * D_MODEL), BF16),
                 jax.ShapeDtypeStruct((t, N_HEADS_B * MLA_HW), BF16),
                 jax.ShapeDtypeStruct((t, N_HEADS_B * MLA_HW), BF16),
                 jax.ShapeDtypeStruct((t, LANE), F32),
                 jax.ShapeDtypeStruct((t, LANE), BF16),
                 jax.ShapeDtypeStruct((t, QKV_C), F32),
                 jax.ShapeDtypeStruct((t // CHUNK, 16, CHUNK), F32)]
    out_specs = [pl.BlockSpec((TM, 768), row),
                 pl.BlockSpec((TM, 1536), row),
                 pl.BlockSpec((TM, 3 * D_MODEL), row),
                 pl.BlockSpec((TM, N_HEADS_B * MLA_HW), row),
                 pl.BlockSpec((TM, N_HEADS_B * MLA_HW), row),
                 pl.BlockSpec((TM, LANE), row),
                 pl.BlockSpec((TM, LANE), row),
                 pl.BlockSpec((TM, QKV_C), row),
                 pl.BlockSpec((CPT, 16, CHUNK), lambda i: (i, 0, 0))]
    if ctx:
        out_shape += [jax.ShapeDtypeStruct((t, 2 * N_KV_A * HD_A), F32), jax.ShapeDtypeStruct((t, KV_RANK_B), F32)]
        out_specs += [pl.BlockSpec((TM, 2 * N_KV_A * HD_A), row), pl.BlockSpec((TM, KV_RANK_B), row)]
    return pl.pallas_call(
        functools.partial(_inproj_kernel, rope=rope, ctx=ctx),
        grid=(nt,),
        in_specs=in_specs,
        out_specs=out_specs,
        out_shape=out_shape,
        compiler_params=pltpu.CompilerParams(dimension_semantics=("parallel",)),
        name="inproj_ctx" if ctx else "inproj_lat",
    )(*args)


def _kvup_kernel(c_ref, w_ref, o_ref):
    o_ref[...] = _dot(c_ref[...].astype(BF16), w_ref[...]).astype(BF16)


def _kvup(ckv2d, wukv):
    t = ckv2d.shape[0]
    tm = 512
    return pl.pallas_call(
        _kvup_kernel,
        grid=(t // tm,),
        in_specs=[pl.BlockSpec((tm, KV_RANK_B), lambda i: (i, 0)),
                  pl.BlockSpec((KV_RANK_B, N_HEADS_B * MLA_HW), lambda i: (0, 0))],
        out_specs=pl.BlockSpec((tm, N_HEADS_B * MLA_HW), lambda i: (i, 0)),
        out_shape=jax.ShapeDtypeStruct((t, N_HEADS_B * MLA_HW), BF16),
        name="mla_cache_up",
    )(ckv2d, wukv)


def _softmax_parts(scores, sink):
    m = scores[0].max(axis=-1, keepdims=True)
    for s in scores[1:]:
        m = jnp.maximum(m, s.max(axis=-1, keepdims=True))
    if sink is not None:
        m = jnp.maximum(m, sink)
    es = [jnp.exp(s - m) for s in scores]
    den = es[0].sum(axis=-1, keepdims=True)
    for e in es[1:]:
        den = den + e.sum(axis=-1, keepdims=True)
    if sink is not None:
        den = den + jnp.exp(sink - m)
    return es, den


def _attn_a_ctx_kernel(qkv_ref, z_ref, sink_ref, o_ref, acc_ref):
    for h in range(N_HEADS_A):
        g = h // GQA_GROUP
        q = qkv_ref[:, h * HD_A:(h + 1) * HD_A]
        k = qkv_ref[:, W_A + g * HD_A:W_A + (g + 1) * HD_A]
        v = qkv_ref[:, W_A + 128 + g * HD_A:W_A + 128 + (g + 1) * HD_A]
        (e,), den = _softmax_parts([_dot_nt(q, k)], sink_ref[:, h:h + 1])
        acc_ref[:, h * HD_A:(h + 1) * HD_A] = _dot(e.astype(BF16), v) / den
    o_ref[...] = (acc_ref[...] * _silu(z_ref[...].astype(F32))).astype(BF16)


def _attn_a_ctx(qkv, z, sink, seq):
    t = qkv.shape[0]
    return pl.pallas_call(
        _attn_a_ctx_kernel,
        grid=(t // seq,),
        in_specs=[pl.BlockSpec((seq, 768), lambda b: (b, 0)),
                  pl.BlockSpec((seq, W_A), lambda b: (b, 0)),
                  pl.BlockSpec((1, N_HEADS_A), lambda b: (0, 0))],
        out_specs=pl.BlockSpec((seq, W_A), lambda b: (b, 0)),
        out_shape=jax.ShapeDtypeStruct((t, W_A), BF16),
        scratch_shapes=[pltpu.VMEM((seq, W_A), F32)],
        compiler_params=pltpu.CompilerParams(dimension_semantics=("parallel",)),
        name="attn_a_ctx",
    )(qkv, z, sink)


def _attn_a_lat_kernel(q_ref, kvp_ref, kvc_ref, kvn_ref, kx_ref, vx_ref, z_ref, sink_ref, o_ref, acc_ref, *, nb):
    j = pl.program_id(1)
    ri = lax.broadcasted_iota(jnp.int32, (BAND_BLK, BAND_BLK), 0)
    ci = lax.broadcasted_iota(jnp.int32, (BAND_BLK, BAND_BLK), 1)
    ok_prev = (ci >= ri) & (j > 0)
    ok_next = (ci <= ri) & (j < nb - 1)
    kx = kx_ref[0].astype(BF16)
    vx = vx_ref[0].astype(BF16)
    for h in range(N_HEADS_A):
        g = h // GQA_GROUP
        q = q_ref[:, h * HD_A:(h + 1) * HD_A]
        ks = slice(g * HD_A, (g + 1) * HD_A)
        vs = slice(128 + g * HD_A, 128 + (g + 1) * HD_A)
        sp = jnp.where(ok_prev, _dot_nt(q, kvp_ref[:, ks]), NEG_INF)
        sc = _dot_nt(q, kvc_ref[:, ks])
        sn = jnp.where(ok_next, _dot_nt(q, kvn_ref[:, ks]), NEG_INF)
        sx = _dot_nt(q, kx[:, ks])
        (ep, ec, en, ex), den = _softmax_parts([sp, sc, sn, sx], sink_ref[:, h:h + 1])
        o = (_dot(ep.astype(BF16), kvp_ref[:, vs]) + _dot(ec.astype(BF16), kvc_ref[:, vs])
             + _dot(en.astype(BF16), kvn_ref[:, vs]) + _dot(ex.astype(BF16), vx[:, ks]))
        acc_ref[:, h * HD_A:(h + 1) * HD_A] = o / den
    o_ref[...] = (acc_ref[...] * _silu(z_ref[...].astype(F32))).astype(BF16)


def _attn_a_lat(qkv, z, sink, kx, vx, seq):
    t = qkv.shape[0]
    nb = seq // BAND_BLK
    past = kx.shape[1]
    row = lambda b, j: (b * nb + j, 0)
    return pl.pallas_call(
        functools.partial(_attn_a_lat_kernel, nb=nb),
        grid=(t // seq, nb),
        in_specs=[pl.BlockSpec((BAND_BLK, W_A), row),
                  pl.BlockSpec((BAND_BLK, 256), lambda b, j: (b * nb + jnp.maximum(j - 1, 0), 2)),
                  pl.BlockSpec((BAND_BLK, 256), lambda b, j: (b * nb + j, 2)),
                  pl.BlockSpec((BAND_BLK, 256), lambda b, j: (b * nb + jnp.minimum(j + 1, nb - 1), 2)),
                  pl.BlockSpec((1, past, 128), lambda b, j: (b, 0, 0)),
                  pl.BlockSpec((1, past, 128), lambda b, j: (b, 0, 0)),
                  pl.BlockSpec((BAND_BLK, W_A), row),
                  pl.BlockSpec((1, N_HEADS_A), lambda b, j: (0, 0))],
        out_specs=pl.BlockSpec((BAND_BLK, W_A), row),
        out_shape=jax.ShapeDtypeStruct((t, W_A), BF16),
        scratch_shapes=[pltpu.VMEM((BAND_BLK, W_A), F32)],
        compiler_params=pltpu.CompilerParams(dimension_semantics=("parallel", "parallel")),
        name="attn_a_lat",
    )(qkv, qkv, qkv, qkv, kx, vx, z, sink)


def _mla_keys(kv, kpe):
    lane = lax.broadcasted_iota(jnp.int32, kv.shape, 1)
    return jnp.where(lane < QK_NOPE_B, kv.astype(F32), kpe.astype(F32)).astype(BF16)


def _attn_b_kernel(*refs, has_ctx):
    if has_ctx:
        q_ref, kv_ref, kpe_ref, kvx_ref, kpex_ref, z_ref, o_ref, acc_ref = refs
    else:
        q_ref, kv_ref, kpe_ref, z_ref, o_ref, acc_ref = refs
    kpe = kpe_ref[...]
    for h in range(N_HEADS_B):
        hs = slice(h * MLA_HW, (h + 1) * MLA_HW)
        q = q_ref[:, hs]
        kv = kv_ref[:, hs]
        scores = [_dot_nt(q, _mla_keys(kv, kpe))]
        if has_ctx:
            kvx = kvx_ref[:, hs]
            scores.append(_dot_nt(q, _mla_keys(kvx, kpex_ref[0])))
        es, den = _softmax_parts(scores, None)
        o = _dot(es[0].astype(BF16), kv)
        if has_ctx:
            o = o + _dot(es[1].astype(BF16), kvx)
        acc_ref[:, h * V_HD_B:(h + 1) * V_HD_B] = (o / den)[:, QK_NOPE_B:]
    o_ref[...] = (acc_ref[...] * _silu(z_ref[...].astype(F32))).astype(BF16)


def _attn_b(qb, kvb, kpe, z, seq, qblk, kvx=None, kpex=None):
    t = qb.shape[0]
    nq = seq // qblk
    has_ctx = kvx is not None
    hw = N_HEADS_B * MLA_HW
    in_specs = [pl.BlockSpec((qblk, hw), lambda b, j: (b * nq + j, 0)),
                pl.BlockSpec((seq, hw), lambda b, j: (b, 0)),
                pl.BlockSpec((seq, LANE), lambda b, j: (b, 0))]
    args = [qb, kvb, kpe]
    if has_ctx:
        past = kpex.shape[1]
        in_specs += [pl.BlockSpec((past, hw), lambda b, j: (b, 0)),
                     pl.BlockSpec((1, past, LANE), lambda b, j: (b, 0, 0))]
        args += [kvx, kpex]
    in_specs.append(pl.BlockSpec((qblk, W_B), lambda b, j: (b * nq + j, 1)))
    args.append(z)
    return pl.pallas_call(
        functools.partial(_attn_b_kernel, has_ctx=has_ctx),
        grid=(t // seq, nq),
        in_specs=in_specs,
        out_specs=pl.BlockSpec((qblk, W_B), lambda b, j: (b * nq + j, 0)),
        out_shape=jax.ShapeDtypeStruct((t, W_B), BF16),
        scratch_shapes=[pltpu.VMEM((qblk, W_B), F32)],
        compiler_params=pltpu.CompilerParams(dimension_semantics=("parallel", "parallel")),
        name="attn_b_lat" if has_ctx else "attn_b_ctx",
    )(*args)


def _gdn_local_kernel(cq_ref, prev_ref, next_ref, small_ref, abt_ref, cw_ref, prow_ref, pcol_ref,
                      u_ref, w_ref, qg_ref, kd_ref, attn_ref, eg_ref, qkv_scr, gb_scr, *, tiles_per_seq):
    tpos = pl.program_id(0) % tiles_per_seq
    x = cq_ref[...]
    prev_row = jnp.where(tpos > 0, prev_ref[HALO - 1:HALO, :], 0.0)
    next_row = jnp.where(tpos < tiles_per_seq - 1, next_ref[0:1, :], 0.0)
    rows = lax.broadcasted_iota(jnp.int32, (TM, 1), 0)
    xm1 = jnp.where(rows == 0, prev_row, pltpu.roll(x, 1, 0))
    xp1 = jnp.where(rows == TM - 1, next_row, pltpu.roll(x, TM - 1, 0))
    cw = cw_ref[...]
    y = _silu(xm1 * cw[0:1] + x * cw[1:2] + xp1 * cw[2:3])
    nq = N_HEADS_C * DK_C
    for h in range(N_HEADS_C):
        qh = y[:, h * DK_C:(h + 1) * DK_C]
        kh = y[:, nq + h * DK_C:nq + (h + 1) * DK_C]
        qkv_scr[:, h * DK_C:(h + 1) * DK_C] = (
            qh * lax.rsqrt(jnp.sum(qh * qh, axis=-1, keepdims=True) + EPS) * (DK_C ** -0.5))
        qkv_scr[:, nq + h * DK_C:nq + (h + 1) * DK_C] = kh * lax.rsqrt(jnp.sum(kh * kh, axis=-1, keepdims=True) + EPS)
    qkv_scr[:, 2 * nq:] = y[:, 2 * nq:]

    sm = small_ref[...]
    prow = prow_ref[...]
    gb_scr[:, 0:8] = -jnp.exp(prow[0:1]) * _softplus(sm[:, S_A:S_A + 8] + prow[1:2])
    gb_scr[:, 8:16] = _sigmoid(sm[:, S_B:S_B + 8])
    pcol = pcol_ref[...]

    ri = lax.broadcasted_iota(jnp.int32, (CHUNK, CHUNK), 0)
    ci = lax.broadcasted_iota(jnp.int32, (CHUNK, CHUNK), 1)
    tril = (ri >= ci).astype(F32)
    triu = (ri <= ci).astype(F32)
    xor = ri ^ ci
    eye = (ri == ci).astype(F32)

    def chunk_body(c, carry):
        r0 = pl.multiple_of(c * CHUNK, CHUNK)
        rs = pl.ds(r0, CHUNK)
        gcol = gb_scr[rs, 0:8]
        bcol = gb_scr[rs, 8:16]
        abt = abt_ref[c]
        grow = -jnp.exp(pcol[:, 0:1]) * _softplus(abt[0:8] + pcol[:, 1:2])
        gc_f = _dot_exact(tril, gcol)
        gc_b = _dot_exact(triu, gcol)
        gr_f = _dot_exact(grow, triu)
        gr_b = _dot_exact(grow, tril)
        for h in range(N_HEADS_C):
            q = qkv_scr[rs, h * DK_C:(h + 1) * DK_C]
            k = qkv_scr[rs, nq + h * DK_C:nq + (h + 1) * DK_C]
            v = qkv_scr[rs, 2 * nq + h * DV_C:2 * nq + (h + 1) * DV_C]
            kb16 = k.astype(BF16)
            kk = _dot_nt(kb16, kb16)
            qk = _dot_nt(q.astype(BF16), kb16)
            for d in range(2):
                dh = d * N_HEADS_C + h
                gc = (gc_f if d == 0 else gc_b)[:, dh:dh + 1]
                gr = (gr_f if d == 0 else gr_b)[dh:dh + 1, :]
                beta = bcol[:, dh:dh + 1]
                incl = (ri >= ci) if d == 0 else (ri <= ci)
                strict = (ri > ci) if d == 0 else (ri < ci)
                decay = jnp.where(incl, jnp.exp(jnp.where(incl, gc - gr, 0.0)), 0.0)
                low = jnp.where(strict, beta * kk * decay, 0.0)
                inv = eye - jnp.where(xor == 1, low, 0.0)
                b = 2
                while b < CHUNK:
                    cpl = jnp.where((xor >= b) & (xor < 2 * b), low, 0.0)
                    tmp = _dot(cpl.astype(BF16), inv.astype(BF16))
                    inv = inv - _dot(inv.astype(BF16), tmp.astype(BF16))
                    b *= 2
                eg = jnp.exp(gc)
                rhs = jnp.concatenate([v * beta, k * (beta * eg)], axis=-1).astype(BF16)
                uw = _dot(inv.astype(BF16), rhs)
                g_last = gc[CHUNK - 1:CHUNK] if d == 0 else gc[0:1]
                cs = slice(dh * DK_C, (dh + 1) * DK_C)
                u_ref[rs, cs] = uw[:, :DV_C]
                w_ref[rs, cs] = uw[:, DV_C:].astype(BF16)
                qg_ref[rs, cs] = (q * eg).astype(BF16)
                kd_ref[rs, cs] = (k * jnp.exp(g_last - gc)).astype(BF16)
                attn_ref[rs, dh * CHUNK:(dh + 1) * CHUNK] = (qk * decay).astype(BF16)
                eg_ref[c, dh:dh + 1, :] = jnp.broadcast_to(jnp.exp(g_last), (1, LANE))
        return carry

    lax.fori_loop(0, CPT, chunk_body, 0)


def _gdn_local(cqkv, small, abt, conv_w, prow, pcol, tiles_per_seq):
    t = cqkv.shape[0]
    nt = t // TM
    nh8 = t // HALO
    row = lambda i: (i, 0)
    dh = 2 * N_HEADS_C
    return pl.pallas_call(
        functools.partial(_gdn_local_kernel, tiles_per_seq=tiles_per_seq),
        grid=(nt,),
        in_specs=[pl.BlockSpec((TM, QKV_C), row),
                  pl.BlockSpec((HALO, QKV_C), lambda i: (jnp.maximum(i * (TM // HALO) - 1, 0), 0)),
                  pl.BlockSpec((HALO, QKV_C), lambda i: (jnp.minimum((i + 1) * (TM // HALO), nh8 - 1), 0)),
                  pl.BlockSpec((TM, LANE), row),
                  pl.BlockSpec((CPT, 16, CHUNK), lambda i: (i, 0, 0)),
                  pl.BlockSpec((3, QKV_C), lambda i: (0, 0)),
                  pl.BlockSpec((2, dh), lambda i: (0, 0)),
                  pl.BlockSpec((dh, 2), lambda i: (0, 0))],
        out_specs=[pl.BlockSpec((TM, dh * DV_C), row),
                   pl.BlockSpec((TM, dh * DK_C), row),
                   pl.BlockSpec((TM, dh * DK_C), row),
                   pl.BlockSpec((TM, dh * DK_C), row),
                   pl.BlockSpec((TM, dh * CHUNK), row),
                   pl.BlockSpec((CPT, dh, LANE), lambda i: (i, 0, 0))],
        out_shape=[jax.ShapeDtypeStruct((t, dh * DV_C), F32),
                   jax.ShapeDtypeStruct((t, dh * DK_C), BF16),
                   jax.ShapeDtypeStruct((t, dh * DK_C), BF16),
                   jax.ShapeDtypeStruct((t, dh * DK_C), BF16),
                   jax.ShapeDtypeStruct((t, dh * CHUNK), BF16),
                   jax.ShapeDtypeStruct((t // CHUNK, dh, LANE), F32)],
        scratch_shapes=[pltpu.VMEM((TM, QKV_C), F32), pltpu.VMEM((TM, 16), F32)],
        compiler_params=pltpu.CompilerParams(dimension_semantics=("parallel",)),
        name="gdn_local",
    )(cqkv, cqkv, cqkv, small, abt, conv_w, prow, pcol)


def _gdn_scan_kernel(*refs, nt, has_init, want_state):
    it = iter(refs)
    ins = [[next(it) for _ in range(6)] for _ in range(2)]
    s0_ref = next(it) if has_init else None
    of_ref, ob_ref = next(it), next(it)
    st_ref = next(it) if want_state else None
    s_scr = next(it)
    j = pl.program_id(1)

    @pl.when(j == 0)
    def _():
        if has_init:
            s_scr[...] = s0_ref[0]
        else:
            s_scr[...] = jnp.zeros_like(s_scr)

    for d in range(2):
        u_ref, w_ref, qg_ref, kd_ref, attn_ref, eg_ref = ins[d]
        o_ref = of_ref if d == 0 else ob_ref
        for step in range(CPT):
            c = step if d == 0 else CPT - 1 - step
            rs = slice(c * CHUNK, (c + 1) * CHUNK)
            for h in range(N_HEADS_C):
                cs = slice(h * DK_C, (h + 1) * DK_C)
                s = s_scr[d, h]
                sb = s.astype(BF16)
                v_new = u_ref[rs, cs] - _dot(w_ref[rs, cs], sb)
                vb = v_new.astype(BF16)
                o_ref[rs, cs] = _dot(qg_ref[rs, cs], sb) + _dot(attn_ref[rs, h * CHUNK:(h + 1) * CHUNK], vb)
                eg = eg_ref[c, d * N_HEADS_C + h:d * N_HEADS_C + h + 1, :]
                s_scr[d, h] = s * eg + _dot_tn(kd_ref[rs, cs], vb)

    if want_state:
        @pl.when(j == nt - 1)
        def _():
            st_ref[0] = s_scr[...]


def _gdn_scan(u, w, qg, kd, attn, eg, s0, seq, want_state):
    t = u.shape[0]
    nt = seq // TM
    nb = t // seq
    half = N_HEADS_C * DK_C
    has_init = s0 is not None
    in_specs, args = [], []
    for d in range(2):
        if d == 0:
            row = lambda b, j: (b * nt + j, 0)
            row3 = lambda b, j: (b * nt + j, 0, 0)
        else:
            row = lambda b, j: (b * nt + nt - 1 - j, 1)
            row3 = lambda b, j: (b * nt + nt - 1 - j, 0, 0)
        in_specs += [pl.BlockSpec((TM, half), row)] * 4
        in_specs += [pl.BlockSpec((TM, N_HEADS_C * CHUNK), row),
                     pl.BlockSpec((CPT, 2 * N_HEADS_C, LANE), row3)]
        args += [u, w, qg, kd, attn, eg]
    st_block = (1, 2, N_HEADS_C, DK_C, DV_C)
    if has_init:
        in_specs.append(pl.BlockSpec(st_block, lambda b, j: (b, 0, 0, 0, 0)))
        args.append(s0)
    out_specs = [pl.BlockSpec((TM, half), lambda b, j: (b * nt + j, 0)),
                 pl.BlockSpec((TM, half), lambda b, j: (b * nt + nt - 1 - j, 0))]
    out_shape = [jax.ShapeDtypeStruct((t, half), F32), jax.ShapeDtypeStruct((t, half), F32)]
    if want_state:
        out_specs.append(pl.BlockSpec(st_block, lambda b, j: (b, 0, 0, 0, 0)))
        out_shape.append(jax.ShapeDtypeStruct((nb,) + st_block[1:], F32))
    return pl.pallas_call(
        functools.partial(_gdn_scan_kernel, nt=nt, has_init=has_init, want_state=want_state),
        grid=(nb, nt),
        in_specs=in_specs,
        out_specs=out_specs,
        out_shape=out_shape,
        scratch_shapes=[pltpu.VMEM((2, N_HEADS_C, DK_C, DV_C), F32)],
        compiler_params=pltpu.CompilerParams(dimension_semantics=("parallel", "arbitrary")),
        name="gdn_scan",
    )(*args)


def _merge_kernel(x_ref, mod_ref, oa_ref, ob_ref, cf_ref, cb_ref, zc_ref, gates_ref, gn_ref,
                  wa_ref, wb_ref, wc_ref, wo_ref, fg_ref, o_ref, *, last):
    oc = cf_ref[...] + cb_ref[...]
    zc = zc_ref[...].astype(F32)
    gn = gn_ref[...]
    parts = []
    for h in range(N_HEADS_C):
        hs = slice(h * DV_C, (h + 1) * DV_C)
        och = oc[:, hs]
        och = och * lax.rsqrt(jnp.mean(och * och, axis=-1, keepdims=True) + EPS) * gn
        parts.append((och * _silu(zc[:, hs])).astype(BF16))
    ocz = jnp.concatenate(parts, axis=-1)
    pa = _dot(oa_ref[...], wa_ref[...])
    pb = _dot(ob_ref[...], wb_ref[...])
    pc = _dot(ocz, wc_ref[...])
    ga = _sigmoid(gates_ref[:, 0:D_MODEL].astype(F32))
    gb = _sigmoid(gates_ref[:, D_MODEL:2 * D_MODEL].astype(F32))
    gc = _sigmoid(gates_ref[:, 2 * D_MODEL:].astype(F32))
    y = _dot((ga * pa + gb * pb + gc * pc).astype(BF16), wo_ref[...])
    gate = mod_ref[0][:, 2 * D_MODEL:]
    xo = x_ref[...] + gate * y
    if last:
        xo = xo * lax.rsqrt(jnp.mean(xo * xo, axis=-1, keepdims=True) + EPS) * fg_ref[...]
    o_ref[...] = xo


def _merge(x2d, mod, mod_row_fn, oa, ob, cf, cb, z, gates, gn, wa, wb, wc, wo, fg, last):
    t = x2d.shape[0]
    row = lambda i: (i, 0)
    const2 = lambda i: (0, 0)
    return pl.pallas_call(
        functools.partial(_merge_kernel, last=last),
        grid=(t // TM,),
        in_specs=[pl.BlockSpec((TM, D_MODEL), row),
                  pl.BlockSpec((1, 1, 3 * D_MODEL), lambda i: (mod_row_fn(i), 0, 0)),
                  pl.BlockSpec((TM, W_A), row),
                  pl.BlockSpec((TM, W_B), row),
                  pl.BlockSpec((TM, W_C), row),
                  pl.BlockSpec((TM, W_C), row),
                  pl.BlockSpec((TM, W_C), lambda i: (i, 2)),
                  pl.BlockSpec((TM, 3 * D_MODEL), row),
                  pl.BlockSpec((1, DV_C), const2),
                  pl.BlockSpec((W_A, D_MODEL), const2),
                  pl.BlockSpec((W_B, D_MODEL), const2),
                  pl.BlockSpec((W_C, D_MODEL), const2),
                  pl.BlockSpec((D_MODEL, D_MODEL), const2),
                  pl.BlockSpec((1, D_MODEL), const2)],
        out_specs=pl.BlockSpec((TM, D_MODEL), row),
        out_shape=jax.ShapeDtypeStruct((t, D_MODEL), F32),
        compiler_params=pltpu.CompilerParams(dimension_semantics=("parallel",)),
        name="merge",
    )(x2d, mod, oa, ob, cf, cb, z, gates, gn, wa, wb, wc, wo, fg)


def _rope_tables(n_tokens, rot_dim):
    rows = n_tokens // GRID_W
    row = jnp.repeat(jnp.arange(rows), GRID_W).astype(F32)
    col = jnp.tile(jnp.arange(GRID_W), rows).astype(F32)
    n_pairs = rot_dim // 4
    inv = ROPE_BASE ** (-jnp.arange(n_pairs, dtype=F32) / n_pairs)
    ang = jnp.concatenate([row[:, None] * inv, col[:, None] * inv], axis=-1)
    c, s = jnp.cos(ang), jnp.sin(ang)
    return jnp.repeat(c, 2, axis=-1), jnp.stack([-s, s], axis=-1).reshape(n_tokens, rot_dim)


def _layer_weights(w_in_l, w_uq_l):
    o = [0]
    for n in IN_SIZES:
        o.append(o[-1] + n)
    seg = lambda i: w_in_l[:, o[i]:o[i + 1]]
    zeros = lambda n: jnp.zeros((D_MODEL, n), w_in_l.dtype)
    wp = jnp.concatenate(
        [seg(0), seg(1), seg(2), seg(3), seg(7), seg(11), seg(12), seg(4), seg(5),
         zeros(S_KPE), seg(6), seg(9), seg(10), zeros(LANE - S_B - 8), seg(8)], axis=1).astype(BF16)
    wab = jnp.concatenate([seg(9), seg(10)], axis=1).T.astype(BF16)
    hd = QK_NOPE_B + QK_ROPE_B
    wuq = jnp.pad(w_uq_l.reshape(Q_RANK_B, N_HEADS_B, hd), ((0, 0), (0, 0), (0, MLA_HW - hd)))
    return wp, wab, wuq.reshape(Q_RANK_B, N_HEADS_B * MLA_HW).astype(BF16)


def kernel(x_prompt, x_sample, cache_attn_k, cache_attn_v, cache_mla_ckv, cache_mla_kpe, state_gdn, c, c_ctx,
           norm_g, w_ada, b_ada, w_in, attn_sink, mla_q_norm, mla_w_uq, mla_kv_norm, mla_w_ukv, gdn_conv,
           gdn_a_log, gdn_dt_bias, gdn_norm, w_branch_a, w_branch_b, w_branch_c, w_out, final_norm_g):
    depth = w_in.shape[0]
    nb_c, seq_c, _ = x_prompt.shape
    nb_l, seq_l, _ = x_sample.shape
    past = cache_attn_k.shape[2]
    assert P_END == 7680 and seq_c % TM == 0 and seq_l % TM == 0 and nb_l < 8

    cond8 = jnp.zeros((8, D_MODEL), F32).at[:nb_l].set(c).at[nb_l].set(c_ctx)
    mods = _modulation(cond8, w_ada, b_ada)

    c_a, s_a = _rope_tables(seq_l, HD_A)
    c_b, s_b = _rope_tables(seq_l, QK_ROPE_B)
    rope_tabs = (jnp.tile(c_a, (1, LANE // HD_A)), jnp.tile(s_a, (1, LANE // HD_A)),
                 jnp.concatenate([jnp.ones((seq_l, S_KPE), F32), c_b, jnp.ones((seq_l, LANE - S_KPE - QK_ROPE_B), F32)], 1),
                 jnp.concatenate([jnp.zeros((seq_l, S_KPE), F32), s_b, jnp.zeros((seq_l, LANE - S_KPE - QK_ROPE_B), F32)], 1))

    tps_c, tps_l = seq_c // TM, seq_l // TM
    y_p = x_prompt.reshape(nb_c * seq_c, D_MODEL)
    y_s = x_sample.reshape(nb_l * seq_l, D_MODEL)
    ks, vs, ckvs, kpes, sts = [], [], [], [], []
    for l in range(depth):
        last = l == depth - 1
        wp, wab, wuq = _layer_weights(w_in[l], mla_w_uq[l])
        wukv = mla_w_ukv[l].astype(BF16)
        mod = mods[l].reshape(8, 1, 3 * D_MODEL)
        ng = norm_g[l].reshape(1, D_MODEL)
        qn = mla_q_norm[l].reshape(1, Q_RANK_B)
        kvn = mla_kv_norm[l].reshape(1, KV_RANK_B)
        sink = attn_sink[l].reshape(1, N_HEADS_A)
        prow = jnp.stack([gdn_a_log[l].reshape(-1), gdn_dt_bias[l].reshape(-1)], axis=0)
        pcol = prow.T
        gn = gdn_norm[l].reshape(1, DV_C)
        wa, wb, wc, wo = (w.astype(BF16) for w in (w_branch_a[l], w_branch_b[l], w_branch_c[l], w_out[l]))
        fg = final_norm_g.reshape(1, D_MODEL)

        mod_row_c = lambda i: nb_l
        (qkv, z, gates, qb, kvb, small, kpe, cqkv, abt, kva, ckvn) = _inproj(
            y_p, mod, mod_row_c, ng, wp, wab, wuq, wukv, qn, kvn, None, tps_c, True)
        oa = _attn_a_ctx(qkv, z, sink, seq_c)
        ob = _attn_b(qb, kvb, kpe, z, seq_c, seq_c)
        u, w, qg, kd, attn, eg = _gdn_local(cqkv, small, abt, gdn_conv[l], prow, pcol, tps_c)
        cf, cb, st = _gdn_scan(u, w, qg, kd, attn, eg, None, seq_c, True)
        y_p = _merge(y_p, mod, mod_row_c, oa, ob, cf, cb, z, gates, gn, wa, wb, wc, wo, fg, last)
        ks.append(kva[:, :N_KV_A * HD_A].reshape(nb_c, seq_c, N_KV_A, HD_A))
        vs.append(kva[:, N_KV_A * HD_A:].reshape(nb_c, seq_c, N_KV_A, HD_A))
        ckvs.append(ckvn.reshape(nb_c, seq_c, KV_RANK_B))
        kpes.append(small[:, S_KPE:S_KPE + QK_ROPE_B].reshape(nb_c, seq_c, QK_ROPE_B))
        sts.append(st)

        mod_row_l = lambda i: i // tps_l
        (qkv, z, gates, qb, kvb, small, kpe, cqkv, abt) = _inproj(
            y_s, mod, mod_row_l, ng, wp, wab, wuq, wukv, qn, kvn, rope_tabs, tps_l, False)
        kx = cache_attn_k[:, l].reshape(nb_l, past, N_KV_A * HD_A)
        vx = cache_attn_v[:, l].reshape(nb_l, past, N_KV_A * HD_A)
        oa = _attn_a_lat(qkv, z, sink, kx, vx, seq_l)
        kvx = _kvup(cache_mla_ckv[:, l].reshape(nb_l * past, KV_RANK_B), wukv)
        kpex = jnp.pad(cache_mla_kpe[:, l], ((0, 0), (0, 0), (S_KPE, LANE - S_KPE - QK_ROPE_B))).astype(BF16)
        ob = _attn_b(qb, kvb, kpe, z, seq_l, TM, kvx, kpex)
        u, w, qg, kd, attn, eg = _gdn_local(cqkv, small, abt, gdn_conv[l], prow, pcol, tps_l)
        cf, cb = _gdn_scan(u, w, qg, kd, attn, eg, state_gdn[:, l], seq_l, False)
        y_s = _merge(y_s, mod, mod_row_l, oa, ob, cf, cb, z, gates, gn, wa, wb, wc, wo, fg, last)

    return (y_p.reshape(nb_c, seq_c, D_MODEL), y_s.reshape(nb_l, seq_l, D_MODEL),
            jnp.stack(ks, axis=1), jnp.stack(vs, axis=1), jnp.stack(ckvs, axis=1), jnp.stack(kpes, axis=1),
            jnp.stack(sts, axis=1))
```

```python
import functools

import jax
import jax.numpy as jnp
from jax import lax
from jax.experimental import pallas as pl
from jax.experimental.pallas import tpu as pltpu

F32 = jnp.float32
BF16 = jnp.bfloat16

D_MODEL = 1024
GRID_W = 64
ROPE_BASE = 10000.0
EPS = 1e-6
NEG_INF = -1e30
N_HEADS_A = 8
N_KV_A = 2
HD_A = 64
GQA_GROUP = N_HEADS_A // N_KV_A
BAND_BLK = 128
N_HEADS_B = 8
QK_NOPE_B = 64
QK_ROPE_B = 32
V_HD_B = 64
Q_RANK_B = 384
KV_RANK_B = 256
MLA_SCALE = (QK_NOPE_B + QK_ROPE_B) ** -0.5
N_HEADS_C = 4
DK_C = 128
DV_C = 128
CHUNK = 64
W_A = N_HEADS_A * HD_A
W_B = N_HEADS_B * V_HD_B
W_C = N_HEADS_C * DV_C
QKV_C = 2 * N_HEADS_C * DK_C + W_C
IN_SIZES = (W_A, N_KV_A * HD_A, N_KV_A * HD_A, W_A, Q_RANK_B, KV_RANK_B, QK_ROPE_B, W_B, QKV_C,
            2 * N_HEADS_C, 2 * N_HEADS_C, W_C, 3 * D_MODEL)

LANE = 128
TM = 256
CPT = TM // CHUNK
HALO = 8
MLA_HW = 128

P_QKV = 0
P_Z = P_QKV + 768
P_GATES = P_Z + 1536
P_CQ = P_GATES + 3 * D_MODEL
P_CKV = P_CQ + Q_RANK_B
P_SMALL = P_CKV + KV_RANK_B
P_CQKV = P_SMALL + LANE
P_END = P_CQKV + QKV_C
S_KPE = 64
S_A = 96
S_B = 104


def _sigmoid(x):
    return 1.0 / (1.0 + jnp.exp(-x))


def _silu(x):
    return x * _sigmoid(x)


def _softplus(x):
    return jnp.maximum(x, 0.0) + jnp.log(1.0 + jnp.exp(-jnp.abs(x)))


def _dot(a, b):
    return jnp.dot(a, b, preferred_element_type=F32)


def _dot_nt(a, b):
    return lax.dot_general(a, b, (((1,), (1,)), ((), ())), preferred_element_type=F32)


def _dot_tn(a, b):
    return lax.dot_general(a, b, (((0,), (0,)), ((), ())), preferred_element_type=F32)


def _bdot(a, b):
    return lax.dot_general(a, b, (((2,), (1,)), ((0,), (0,))), preferred_element_type=F32)


def _dot_exact(a, b):
    return jnp.dot(a, b, preferred_element_type=F32, precision=lax.Precision.HIGHEST)


def _rope(x, c, s):
    n = x.shape[-1]
    lane = lax.broadcasted_iota(jnp.int32, x.shape, 1)
    swapped = jnp.where(lane % 2 == 0, pltpu.roll(x, n - 1, 1), pltpu.roll(x, 1, 1))
    return x * c + swapped * s


def _mod_kernel(cond_ref, w_ref, b_ref, out_ref):
    cnd = cond_ref[...]
    out_ref[0] = _dot(_silu(cnd).astype(BF16), w_ref[0].astype(BF16)) + b_ref[0]


def _modulation(cond8, w_ada, b_ada):
    depth = w_ada.shape[0]
    tn = 768
    return pl.pallas_call(
        _mod_kernel,
        grid=(depth, 3 * D_MODEL // tn),
        in_specs=[pl.BlockSpec((8, D_MODEL), lambda l, n: (0, 0)),
                  pl.BlockSpec((1, D_MODEL, tn), lambda l, n: (l, 0, n)),
                  pl.BlockSpec((1, 1, tn), lambda l, n: (l, 0, n))],
        out_specs=pl.BlockSpec((1, 8, tn), lambda l, n: (l, 0, n)),
        out_shape=jax.ShapeDtypeStruct((depth, 8, 3 * D_MODEL), F32),
        name="adaln_mod",
    )(cond8, w_ada, b_ada.reshape(depth, 1, 3 * D_MODEL))


def _inproj_kernel(*refs, rope, ctx):
    it = iter(refs)
    x_ref, mod_ref, ng_ref, wp_ref, wab_ref, wuq_ref, wukv_ref, qn_ref, kvn_ref = (next(it) for _ in range(9))
    if rope:
        ca_ref, sa_ref, cb_ref, sb_ref = (next(it) for _ in range(4))
    qkv_ref, z_ref, gates_ref, qb_ref, kvb_ref, small_ref, kpe_ref, cqkv_ref, abt_ref = (next(it) for _ in range(9))
    if ctx:
        kva_ref, ckvn_ref = (next(it) for _ in range(2))

    x = x_ref[...]
    mod = mod_ref[0]
    shift, scale = mod[:, :D_MODEL], mod[:, D_MODEL:2 * D_MODEL]
    xn = x * lax.rsqrt(jnp.mean(x * x, axis=-1, keepdims=True) + EPS) * ng_ref[...]
    hb = (xn * (1.0 + scale) + shift).astype(BF16)

    def mm(lo, hi):
        return _dot(hb, wp_ref[:, lo:hi])

    r = mm(P_QKV, P_QKV + 768)
    qscale = HD_A ** -0.5
    if rope:
        ca, sa = ca_ref[...], sa_ref[...]
        for t in range(5):
            seg = _rope(r[:, t * LANE:(t + 1) * LANE], ca, sa)
            if t < 4:
                seg = seg * qscale
            qkv_ref[:, t * LANE:(t + 1) * LANE] = seg.astype(BF16)
        qkv_ref[:, 640:768] = r[:, 640:768].astype(BF16)
    else:
        qkv_ref[:, :W_A] = (r[:, :W_A] * qscale).astype(BF16)
        qkv_ref[:, W_A:] = r[:, W_A:].astype(BF16)
    if ctx:
        kva_ref[...] = r[:, W_A:]

    for t in range(3):
        z_ref[:, t * 512:(t + 1) * 512] = mm(P_Z + t * 512, P_Z + (t + 1) * 512).astype(BF16)
    for t in range(6):
        gates_ref[:, t * 512:(t + 1) * 512] = mm(P_GATES + t * 512, P_GATES + (t + 1) * 512).astype(BF16)

    r = mm(P_CQ, P_CQ + Q_RANK_B)
    qn = r * lax.rsqrt(jnp.mean(r * r, axis=-1, keepdims=True) + EPS) * qn_ref[...]
    q = _dot(qn.astype(BF16), wuq_ref[...])
    if rope:
        cb, sb = cb_ref[...], sb_ref[...]
        for h in range(N_HEADS_B):
            seg = _rope(q[:, h * MLA_HW:(h + 1) * MLA_HW], cb, sb) * MLA_SCALE
            qb_ref[:, h * MLA_HW:(h + 1) * MLA_HW] = seg.astype(BF16)
    else:
        qb_ref[...] = (q * MLA_SCALE).astype(BF16)

    r = mm(P_CKV, P_CKV + KV_RANK_B)
    cn = r * lax.rsqrt(jnp.mean(r * r, axis=-1, keepdims=True) + EPS) * kvn_ref[...]
    if ctx:
        ckvn_ref[...] = cn
    kvb_ref[...] = _dot(cn.astype(BF16), wukv_ref[...]).astype(BF16)

    r = mm(P_SMALL, P_SMALL + LANE)
    small_ref[...] = r
    kp = _rope(r, cb, sb) if rope else r
    lane = lax.broadcasted_iota(jnp.int32, r.shape, 1)
    kpe_ref[...] = jnp.where((lane >= S_KPE) & (lane < S_KPE + QK_ROPE_B), kp, 0.0).astype(BF16)

    for t in range(3):
        cqkv_ref[:, t * 512:(t + 1) * 512] = mm(P_CQKV + t * 512, P_CQKV + (t + 1) * 512)

    for c in range(CPT):
        abt_ref[c] = _dot_nt(wab_ref[...], hb[c * CHUNK:(c + 1) * CHUNK])


def _inproj(x2d, mod, mod_row_fn, ng, wp, wab, wuq, wukv, qn, kvn, rope_tabs, tiles_per_seq, ctx):
    t = x2d.shape[0]
    nt = t // TM
    rope = rope_tabs is not None
    const2 = lambda i: (0, 0)
    row = lambda i: (i, 0)
    in_specs = [pl.BlockSpec((TM, D_MODEL), row),
                pl.BlockSpec((1, 1, 3 * D_MODEL), lambda i: (mod_row_fn(i), 0, 0)),
                pl.BlockSpec((1, D_MODEL), const2),
                pl.BlockSpec((D_MODEL, P_END), const2),
                pl.BlockSpec((16, D_MODEL), const2),
                pl.BlockSpec((Q_RANK_B, N_HEADS_B * MLA_HW), const2),
                pl.BlockSpec((KV_RANK_B, N_HEADS_B * MLA_HW), const2),
                pl.BlockSpec((1, Q_RANK_B), const2),
                pl.BlockSpec((1, KV_RANK_B), const2)]
    args = [x2d, mod, ng, wp, wab, wuq, wukv, qn, kvn]
    if rope:
        pos = lambda i: (i % tiles_per_seq, 0)
        in_specs += [pl.BlockSpec((TM, LANE), pos)] * 4
        args += list(rope_tabs)
    out_shape = [jax.ShapeDtypeStruct((t, 768), BF16),
                 jax.ShapeDtypeStruct((t, 1536), BF16),
                 jax.ShapeDtypeStruct((t, 3 * D_MODEL), BF16),
                 jax.ShapeDtypeStruct((t, N_HEADS_B * MLA_HW), BF16),
                 jax.ShapeDtypeStruct((t, N_HEADS_B * MLA_HW), BF16),
                 jax.ShapeDtypeStruct((t, LANE), F32),
                 jax.ShapeDtypeStruct((t, LANE), BF16),
                 jax.ShapeDtypeStruct((t, QKV_C), F32),
                 jax.ShapeDtypeStruct((t // CHUNK, 16, CHUNK), F32)]
    out_specs = [pl.BlockSpec((TM, 768), row),
                 pl.BlockSpec((TM, 1536), row),
                 pl.BlockSpec((TM, 3 * D_MODEL), row),
                 pl.BlockSpec((TM, N_HEADS_B * MLA_HW), row),
                 pl.BlockSpec((TM, N_HEADS_B * MLA_HW), row),
                 pl.BlockSpec((TM, LANE), row),
                 pl.BlockSpec((TM, LANE), row),
                 pl.BlockSpec((TM, QKV_C), row),
                 pl.BlockSpec((CPT, 16, CHUNK), lambda i: (i, 0, 0))]
    if ctx:
        out_shape += [jax.ShapeDtypeStruct((t, 2 * N_KV_A * HD_A), F32), jax.ShapeDtypeStruct((t, KV_RANK_B), F32)]
        out_specs += [pl.BlockSpec((TM, 2 * N_KV_A * HD_A), row), pl.BlockSpec((TM, KV_RANK_B), row)]
    return pl.pallas_call(
        functools.partial(_inproj_kernel, rope=rope, ctx=ctx),
        grid=(nt,),
        in_specs=in_specs,
        out_specs=out_specs,
        out_shape=out_shape,
        compiler_params=pltpu.CompilerParams(dimension_semantics=("parallel",)),
        name="inproj_ctx" if ctx else "inproj_lat",
    )(*args)


def _kvup_kernel(c_ref, w_ref, o_ref):
    o_ref[...] = _dot(c_ref[...].astype(BF16), w_ref[...]).astype(BF16)


def _kvup(ckv2d, wukv):
    t = ckv2d.shape[0]
    tm = 512
    return pl.pallas_call(
        _kvup_kernel,
        grid=(t // tm,),
        in_specs=[pl.BlockSpec((tm, KV_RANK_B), lambda i: (i, 0)),
                  pl.BlockSpec((KV_RANK_B, N_HEADS_B * MLA_HW), lambda i: (0, 0))],
        out_specs=pl.BlockSpec((tm, N_HEADS_B * MLA_HW), lambda i: (i, 0)),
        out_shape=jax.ShapeDtypeStruct((t, N_HEADS_B * MLA_HW), BF16),
        name="mla_cache_up",
    )(ckv2d, wukv)


def _softmax_parts(scores, sink):
    m = scores[0].max(axis=-1, keepdims=True)
    for s in scores[1:]:
        m = jnp.maximum(m, s.max(axis=-1, keepdims=True))
    if sink is not None:
        m = jnp.maximum(m, sink)
    es = [jnp.exp(s - m) for s in scores]
    den = es[0].sum(axis=-1, keepdims=True)
    for e in es[1:]:
        den = den + e.sum(axis=-1, keepdims=True)
    if sink is not None:
        den = den + jnp.exp(sink - m)
    return es, den


def _attn_a_ctx_kernel(qkv_ref, z_ref, sink_ref, o_ref, acc_ref):
    for h in range(N_HEADS_A):
        g = h // GQA_GROUP
        q = qkv_ref[:, h * HD_A:(h + 1) * HD_A]
        k = qkv_ref[:, W_A + g * HD_A:W_A + (g + 1) * HD_A]
        v = qkv_ref[:, W_A + 128 + g * HD_A:W_A + 128 + (g + 1) * HD_A]
        (e,), den = _softmax_parts([_dot_nt(q, k)], sink_ref[:, h:h + 1])
        acc_ref[:, h * HD_A:(h + 1) * HD_A] = _dot(e.astype(BF16), v) / den
    o_ref[...] = (acc_ref[...] * _silu(z_ref[...].astype(F32))).astype(BF16)


def _attn_a_ctx(qkv, z, sink, seq):
    t = qkv.shape[0]
    return pl.pallas_call(
        _attn_a_ctx_kernel,
        grid=(t // seq,),
        in_specs=[pl.BlockSpec((seq, 768), lambda b: (b, 0)),
                  pl.BlockSpec((seq, W_A), lambda b: (b, 0)),
                  pl.BlockSpec((1, N_HEADS_A), lambda b: (0, 0))],
        out_specs=pl.BlockSpec((seq, W_A), lambda b: (b, 0)),
        out_shape=jax.ShapeDtypeStruct((t, W_A), BF16),
        scratch_shapes=[pltpu.VMEM((seq, W_A), F32)],
        compiler_params=pltpu.CompilerParams(dimension_semantics=("parallel",)),
        name="attn_a_ctx",
    )(qkv, z, sink)


def _attn_a_lat_kernel(q_ref, kvp_ref, kvc_ref, kvn_ref, kx_ref, vx_ref, z_ref, sink_ref, o_ref, acc_ref, *, nb):
    j = pl.program_id(1)
    ri = lax.broadcasted_iota(jnp.int32, (BAND_BLK, BAND_BLK), 0)
    ci = lax.broadcasted_iota(jnp.int32, (BAND_BLK, BAND_BLK), 1)
    ok_prev = (ci >= ri) & (j > 0)
    ok_next = (ci <= ri) & (j < nb - 1)
    kx = kx_ref[0].astype(BF16)
    vx = vx_ref[0].astype(BF16)
    for h in range(N_HEADS_A):
        g = h // GQA_GROUP
        q = q_ref[:, h * HD_A:(h + 1) * HD_A]
        ks = slice(g * HD_A, (g + 1) * HD_A)
        vs = slice(128 + g * HD_A, 128 + (g + 1) * HD_A)
        sp = jnp.where(ok_prev, _dot_nt(q, kvp_ref[:, ks]), NEG_INF)
        sc = _dot_nt(q, kvc_ref[:, ks])
        sn = jnp.where(ok_next, _dot_nt(q, kvn_ref[:, ks]), NEG_INF)
        sx = _dot_nt(q, kx[:, ks])
        (ep, ec, en, ex), den = _softmax_parts([sp, sc, sn, sx], sink_ref[:, h:h + 1])
        o = (_dot(ep.astype(BF16), kvp_ref[:, vs]) + _dot(ec.astype(BF16), kvc_ref[:, vs])
             + _dot(en.astype(BF16), kvn_ref[:, vs]) + _dot(ex.astype(BF16), vx[:, ks]))
        acc_ref[:, h * HD_A:(h + 1) * HD_A] = o / den
    o_ref[...] = (acc_ref[...] * _silu(z_ref[...].astype(F32))).astype(BF16)


def _attn_a_lat(qkv, z, sink, kx, vx, seq):
    t = qkv.shape[0]
    nb = seq // BAND_BLK
    past = kx.shape[1]
    row = lambda b, j: (b * nb + j, 0)
    return pl.pallas_call(
        functools.partial(_attn_a_lat_kernel, nb=nb),
        grid=(t // seq, nb),
        in_specs=[pl.BlockSpec((BAND_BLK, W_A), row),
                  pl.BlockSpec((BAND_BLK, 256), lambda b, j: (b * nb + jnp.maximum(j - 1, 0), 2)),
                  pl.BlockSpec((BAND_BLK, 256), lambda b, j: (b * nb + j, 2)),
                  pl.BlockSpec((BAND_BLK, 256), lambda b, j: (b * nb + jnp.minimum(j + 1, nb - 1), 2)),
                  pl.BlockSpec((1, past, 128), lambda b, j: (b, 0, 0)),
                  pl.BlockSpec((1, past, 128), lambda b, j: (b, 0, 0)),
                  pl.BlockSpec((BAND_BLK, W_A), row),
                  pl.BlockSpec((1, N_HEADS_A), lambda b, j: (0, 0))],
        out_specs=pl.BlockSpec((BAND_BLK, W_A), row),
        out_shape=jax.ShapeDtypeStruct((t, W_A), BF16),
        scratch_shapes=[pltpu.VMEM((BAND_BLK, W_A), F32)],
        compiler_params=pltpu.CompilerParams(dimension_semantics=("parallel", "parallel")),
        name="attn_a_lat",
    )(qkv, qkv, qkv, qkv, kx, vx, z, sink)


def _mla_keys(kv, kpe):
    lane = lax.broadcasted_iota(jnp.int32, kv.shape, 1)
    return jnp.where(lane < QK_NOPE_B, kv.astype(F32), kpe.astype(F32)).astype(BF16)


def _attn_b_kernel(*refs, has_ctx):
    if has_ctx:
        q_ref, kv_ref, kpe_ref, kvx_ref, kpex_ref, z_ref, o_ref, acc_ref = refs
    else:
        q_ref, kv_ref, kpe_ref, z_ref, o_ref, acc_ref = refs
    kpe = kpe_ref[...]
    for h in range(N_HEADS_B):
        hs = slice(h * MLA_HW, (h + 1) * MLA_HW)
        q = q_ref[:, hs]
        kv = kv_ref[:, hs]
        scores = [_dot_nt(q, _mla_keys(kv, kpe))]
        if has_ctx:
            kvx = kvx_ref[:, hs]
            scores.append(_dot_nt(q, _mla_keys(kvx, kpex_ref[0])))
        es, den = _softmax_parts(scores, None)
        o = _dot(es[0].astype(BF16), kv)
        if has_ctx:
            o = o + _dot(es[1].astype(BF16), kvx)
        acc_ref[:, h * V_HD_B:(h + 1) * V_HD_B] = (o / den)[:, QK_NOPE_B:]
    o_ref[...] = (acc_ref[...] * _silu(z_ref[...].astype(F32))).astype(BF16)


def _attn_b(qb, kvb, kpe, z, seq, qblk, kvx=None, kpex=None):
    t = qb.shape[0]
    nq = seq // qblk
    has_ctx = kvx is not None
    hw = N_HEADS_B * MLA_HW
    in_specs = [pl.BlockSpec((qblk, hw), lambda b, j: (b * nq + j, 0)),
                pl.BlockSpec((seq, hw), lambda b, j: (b, 0)),
                pl.BlockSpec((seq, LANE), lambda b, j: (b, 0))]
    args = [qb, kvb, kpe]
    if has_ctx:
        past = kpex.shape[1]
        in_specs += [pl.BlockSpec((past, hw), lambda b, j: (b, 0)),
                     pl.BlockSpec((1, past, LANE), lambda b, j: (b, 0, 0))]
        args += [kvx, kpex]
    in_specs.append(pl.BlockSpec((qblk, W_B), lambda b, j: (b * nq + j, 1)))
    args.append(z)
    return pl.pallas_call(
        functools.partial(_attn_b_kernel, has_ctx=has_ctx),
        grid=(t // seq, nq),
        in_specs=in_specs,
        out_specs=pl.BlockSpec((qblk, W_B), lambda b, j: (b * nq + j, 0)),
        out_shape=jax.ShapeDtypeStruct((t, W_B), BF16),
        scratch_shapes=[pltpu.VMEM((qblk, W_B), F32)],
        compiler_params=pltpu.CompilerParams(dimension_semantics=("parallel", "parallel")),
        name="attn_b_lat" if has_ctx else "attn_b_ctx",
    )(*args)


def _gdn_local_kernel(cq_ref, prev_ref, next_ref, small_ref, abt_ref, cw_ref, prow_ref, pcol_ref,
                      u_ref, w_ref, qg_ref, kd_ref, attn_ref, eg_ref, qkv_scr, gb_scr, *, tiles_per_seq):
    tpos = pl.program_id(0) % tiles_per_seq
    x = cq_ref[...]
    prev_row = jnp.where(tpos > 0, prev_ref[HALO - 1:HALO, :], 0.0)
    next_row = jnp.where(tpos < tiles_per_seq - 1, next_ref[0:1, :], 0.0)
    rows = lax.broadcasted_iota(jnp.int32, (TM, 1), 0)
    xm1 = jnp.where(rows == 0, prev_row, pltpu.roll(x, 1, 0))
    xp1 = jnp.where(rows == TM - 1, next_row, pltpu.roll(x, TM - 1, 0))
    cw = cw_ref[...]
    y = _silu(xm1 * cw[0:1] + x * cw[1:2] + xp1 * cw[2:3])
    nq = N_HEADS_C * DK_C
    for h in range(N_HEADS_C):
        qh = y[:, h * DK_C:(h + 1) * DK_C]
        kh = y[:, nq + h * DK_C:nq + (h + 1) * DK_C]
        qkv_scr[:, h * DK_C:(h + 1) * DK_C] = (
            qh * lax.rsqrt(jnp.sum(qh * qh, axis=-1, keepdims=True) + EPS) * (DK_C ** -0.5))
        qkv_scr[:, nq + h * DK_C:nq + (h + 1) * DK_C] = kh * lax.rsqrt(jnp.sum(kh * kh, axis=-1, keepdims=True) + EPS)
    qkv_scr[:, 2 * nq:] = y[:, 2 * nq:]

    sm = small_ref[...]
    prow = prow_ref[...]
    gb_scr[:, 0:8] = -jnp.exp(prow[0:1]) * _softplus(sm[:, S_A:S_A + 8] + prow[1:2])
    gb_scr[:, 8:16] = _sigmoid(sm[:, S_B:S_B + 8])
    pcol = pcol_ref[...]

    ri = lax.broadcasted_iota(jnp.int32, (CHUNK, CHUNK), 0)
    ci = lax.broadcasted_iota(jnp.int32, (CHUNK, CHUNK), 1)
    tril = (ri >= ci).astype(F32)
    triu = (ri <= ci).astype(F32)
    xor = ri ^ ci
    eye = (ri == ci).astype(F32)

    def chunk_body(c, carry):
        r0 = pl.multiple_of(c * CHUNK, CHUNK)
        rs = pl.ds(r0, CHUNK)
        gcol = gb_scr[rs, 0:8]
        bcol = gb_scr[rs, 8:16]
        abt = abt_ref[c]
        grow = -jnp.exp(pcol[:, 0:1]) * _softplus(abt[0:8] + pcol[:, 1:2])
        gc_f = _dot_exact(tril, gcol)
        gc_b = _dot_exact(triu, gcol)
        gr_f = _dot_exact(grow, triu)
        gr_b = _dot_exact(grow, tril)
        lows, rhss, order = [], [], []
        for h in range(N_HEADS_C):
            q = qkv_scr[rs, h * DK_C:(h + 1) * DK_C]
            k = qkv_scr[rs, nq + h * DK_C:nq + (h + 1) * DK_C]
            v = qkv_scr[rs, 2 * nq + h * DV_C:2 * nq + (h + 1) * DV_C]
            kb16 = k.astype(BF16)
            kk = _dot_nt(kb16, kb16)
            qk = _dot_nt(q.astype(BF16), kb16)
            for d in range(2):
                dh = d * N_HEADS_C + h
                gc = (gc_f if d == 0 else gc_b)[:, dh:dh + 1]
                gr = (gr_f if d == 0 else gr_b)[dh:dh + 1, :]
                beta = bcol[:, dh:dh + 1]
                incl = (ri >= ci) if d == 0 else (ri <= ci)
                strict = (ri > ci) if d == 0 else (ri < ci)
                decay = jnp.where(incl, jnp.exp(jnp.where(incl, gc - gr, 0.0)), 0.0)
                lows.append(jnp.where(strict, beta * kk * decay, 0.0))
                eg = jnp.exp(gc)
                rhss.append(jnp.concatenate([v * beta, k * (beta * eg)], axis=-1).astype(BF16))
                order.append(dh)
                g_last = gc[CHUNK - 1:CHUNK] if d == 0 else gc[0:1]
                cs = slice(dh * DK_C, (dh + 1) * DK_C)
                qg_ref[rs, cs] = (q * eg).astype(BF16)
                kd_ref[rs, cs] = (k * jnp.exp(g_last - gc)).astype(BF16)
                attn_ref[rs, dh * CHUNK:(dh + 1) * CHUNK] = (qk * decay).astype(BF16)
                eg_ref[c, dh:dh + 1, :] = jnp.broadcast_to(jnp.exp(g_last), (1, LANE))
        low = jnp.stack(lows, axis=0)
        inv = eye[None] - jnp.where(xor[None] == 1, low, 0.0)
        b = 2
        while b < CHUNK:
            cpl = jnp.where((xor[None] >= b) & (xor[None] < 2 * b), low, 0.0)
            tmp = _bdot(cpl.astype(BF16), inv.astype(BF16))
            inv = inv - _bdot(inv.astype(BF16), tmp.astype(BF16))
            b *= 2
        uw = _bdot(inv.astype(BF16), jnp.stack(rhss, axis=0))
        for i, dh in enumerate(order):
            cs = slice(dh * DK_C, (dh + 1) * DK_C)
            u_ref[rs, cs] = uw[i, :, :DV_C]
            w_ref[rs, cs] = uw[i, :, DV_C:].astype(BF16)
        return carry

    lax.fori_loop(0, CPT, chunk_body, 0)


def _gdn_local(cqkv, small, abt, conv_w, prow, pcol, tiles_per_seq):
    t = cqkv.shape[0]
    nt = t // TM
    nh8 = t // HALO
    row = lambda i: (i, 0)
    dh = 2 * N_HEADS_C
    return pl.pallas_call(
        functools.partial(_gdn_local_kernel, tiles_per_seq=tiles_per_seq),
        grid=(nt,),
        in_specs=[pl.BlockSpec((TM, QKV_C), row),
                  pl.BlockSpec((HALO, QKV_C), lambda i: (jnp.maximum(i * (TM // HALO) - 1, 0), 0)),
                  pl.BlockSpec((HALO, QKV_C), lambda i: (jnp.minimum((i + 1) * (TM // HALO), nh8 - 1), 0)),
                  pl.BlockSpec((TM, LANE), row),
                  pl.BlockSpec((CPT, 16, CHUNK), lambda i: (i, 0, 0)),
                  pl.BlockSpec((3, QKV_C), lambda i: (0, 0)),
                  pl.BlockSpec((2, dh), lambda i: (0, 0)),
                  pl.BlockSpec((dh, 2), lambda i: (0, 0))],
        out_specs=[pl.BlockSpec((TM, dh * DV_C), row),
                   pl.BlockSpec((TM, dh * DK_C), row),
                   pl.BlockSpec((TM, dh * DK_C), row),
                   pl.BlockSpec((TM, dh * DK_C), row),
                   pl.BlockSpec((TM, dh * CHUNK), row),
                   pl.BlockSpec((CPT, dh, LANE), lambda i: (i, 0, 0))],
        out_shape=[jax.ShapeDtypeStruct((t, dh * DV_C), F32),
                   jax.ShapeDtypeStruct((t, dh * DK_C), BF16),
                   jax.ShapeDtypeStruct((t, dh * DK_C), BF16),
                   jax.ShapeDtypeStruct((t, dh * DK_C), BF16),
                   jax.ShapeDtypeStruct((t, dh * CHUNK), BF16),
                   jax.ShapeDtypeStruct((t // CHUNK, dh, LANE), F32)],
        scratch_shapes=[pltpu.VMEM((TM, QKV_C), F32), pltpu.VMEM((TM, 16), F32)],
        compiler_params=pltpu.CompilerParams(dimension_semantics=("parallel",)),
        name="gdn_local",
    )(cqkv, cqkv, cqkv, small, abt, conv_w, prow, pcol)


def _gdn_scan_kernel(*refs, nt, has_init, want_state):
    it = iter(refs)
    ins = [[next(it) for _ in range(6)] for _ in range(2)]
    s0_ref = next(it) if has_init else None
    of_ref, ob_ref = next(it), next(it)
    st_ref = next(it) if want_state else None
    s_scr = next(it)
    j = pl.program_id(1)

    @pl.when(j == 0)
    def _():
        if has_init:
            s_scr[...] = s0_ref[0]
        else:
            s_scr[...] = jnp.zeros_like(s_scr)

    for d in range(2):
        u_ref, w_ref, qg_ref, kd_ref, attn_ref, eg_ref = ins[d]
        o_ref = of_ref if d == 0 else ob_ref
        for step in range(CPT):
            c = step if d == 0 else CPT - 1 - step
            rs = slice(c * CHUNK, (c + 1) * CHUNK)
            for h in range(N_HEADS_C):
                cs = slice(h * DK_C, (h + 1) * DK_C)
                s = s_scr[d, h]
                sb = s.astype(BF16)
                v_new = u_ref[rs, cs] - _dot(w_ref[rs, cs], sb)
                vb = v_new.astype(BF16)
                o_ref[rs, cs] = _dot(qg_ref[rs, cs], sb) + _dot(attn_ref[rs, h * CHUNK:(h + 1) * CHUNK], vb)
                eg = eg_ref[c, d * N_HEADS_C + h:d * N_HEADS_C + h + 1, :]
                s_scr[d, h] = s * eg + _dot_tn(kd_ref[rs, cs], vb)

    if want_state:
        @pl.when(j == nt - 1)
        def _():
            st_ref[0] = s_scr[...]


def _gdn_scan(u, w, qg, kd, attn, eg, s0, seq, want_state):
    t = u.shape[0]
    nt = seq // TM
    nb = t // seq
    half = N_HEADS_C * DK_C
    has_init = s0 is not None
    in_specs, args = [], []
    for d in range(2):
        if d == 0:
            row = lambda b, j: (b * nt + j, 0)
            row3 = lambda b, j: (b * nt + j, 0, 0)
        else:
            row = lambda b, j: (b * nt + nt - 1 - j, 1)
            row3 = lambda b, j: (b * nt + nt - 1 - j, 0, 0)
        in_specs += [pl.BlockSpec((TM, half), row)] * 4
        in_specs += [pl.BlockSpec((TM, N_HEADS_C * CHUNK), row),
                     pl.BlockSpec((CPT, 2 * N_HEADS_C, LANE), row3)]
        args += [u, w, qg, kd, attn, eg]
    st_block = (1, 2, N_HEADS_C, DK_C, DV_C)
    if has_init:
        in_specs.append(pl.BlockSpec(st_block, lambda b, j: (b, 0, 0, 0, 0)))
        args.append(s0)
    out_specs = [pl.BlockSpec((TM, half), lambda b, j: (b * nt + j, 0)),
                 pl.BlockSpec((TM, half), lambda b, j: (b * nt + nt - 1 - j, 0))]
    out_shape = [jax.ShapeDtypeStruct((t, half), F32), jax.ShapeDtypeStruct((t, half), F32)]
    if want_state:
        out_specs.append(pl.BlockSpec(st_block, lambda b, j: (b, 0, 0, 0, 0)))
        out_shape.append(jax.ShapeDtypeStruct((nb,) + st_block[1:], F32))
    return pl.pallas_call(
        functools.partial(_gdn_scan_kernel, nt=nt, has_init=has_init, want_state=want_state),
        grid=(nb, nt),
        in_specs=in_specs,
        out_specs=out_specs,
        out_shape=out_shape,
        scratch_shapes=[pltpu.VMEM((2, N_HEADS_C, DK_C, DV_C), F32)],
        compiler_params=pltpu.CompilerParams(dimension_semantics=("parallel", "arbitrary")),
        name="gdn_scan",
    )(*args)


def _merge_kernel(x_ref, mod_ref, oa_ref, ob_ref, cf_ref, cb_ref, zc_ref, gates_ref, gn_ref,
                  wa_ref, wb_ref, wc_ref, wo_ref, fg_ref, o_ref, *, last):
    oc = cf_ref[...] + cb_ref[...]
    zc = zc_ref[...].astype(F32)
    gn = gn_ref[...]
    parts = []
    for h in range(N_HEADS_C):
        hs = slice(h * DV_C, (h + 1) * DV_C)
        och = oc[:, hs]
        och = och * lax.rsqrt(jnp.mean(och * och, axis=-1, keepdims=True) + EPS) * gn
        parts.append((och * _silu(zc[:, hs])).astype(BF16))
    ocz = jnp.concatenate(parts, axis=-1)
    pa = _dot(oa_ref[...], wa_ref[...])
    pb = _dot(ob_ref[...], wb_ref[...])
    pc = _dot(ocz, wc_ref[...])
    ga = _sigmoid(gates_ref[:, 0:D_MODEL].astype(F32))
    gb = _sigmoid(gates_ref[:, D_MODEL:2 * D_MODEL].astype(F32))
    gc = _sigmoid(gates_ref[:, 2 * D_MODEL:].astype(F32))
    y = _dot((ga * pa + gb * pb + gc * pc).astype(BF16), wo_ref[...])
    gate = mod_ref[0][:, 2 * D_MODEL:]
    xo = x_ref[...] + gate * y
    if last:
        xo = xo * lax.rsqrt(jnp.mean(xo * xo, axis=-1, keepdims=True) + EPS) * fg_ref[...]
    o_ref[...] = xo


def _merge(x2d, mod, mod_row_fn, oa, ob, cf, cb, z, gates, gn, wa, wb, wc, wo, fg, last):
    t = x2d.shape[0]
    row = lambda i: (i, 0)
    const2 = lambda i: (0, 0)
    return pl.pallas_call(
        functools.partial(_merge_kernel, last=last),
        grid=(t // TM,),
        in_specs=[pl.BlockSpec((TM, D_MODEL), row),
                  pl.BlockSpec((1, 1, 3 * D_MODEL), lambda i: (mod_row_fn(i), 0, 0)),
                  pl.BlockSpec((TM, W_A), row),
                  pl.BlockSpec((TM, W_B), row),
                  pl.BlockSpec((TM, W_C), row),
                  pl.BlockSpec((TM, W_C), row),
                  pl.BlockSpec((TM, W_C), lambda i: (i, 2)),
                  pl.BlockSpec((TM, 3 * D_MODEL), row),
                  pl.BlockSpec((1, DV_C), const2),
                  pl.BlockSpec((W_A, D_MODEL), const2),
                  pl.BlockSpec((W_B, D_MODEL), const2),
                  pl.BlockSpec((W_C, D_MODEL), const2),
                  pl.BlockSpec((D_MODEL, D_MODEL), const2),
                  pl.BlockSpec((1, D_MODEL), const2)],
        out_specs=pl.BlockSpec((TM, D_MODEL), row),
        out_shape=jax.ShapeDtypeStruct((t, D_MODEL), F32),
        compiler_params=pltpu.CompilerParams(dimension_semantics=("parallel",)),
        name="merge",
    )(x2d, mod, oa, ob, cf, cb, z, gates, gn, wa, wb, wc, wo, fg)


def _rope_tables(n_tokens, rot_dim):
    rows = n_tokens // GRID_W
    row = jnp.repeat(jnp.arange(rows), GRID_W).astype(F32)
    col = jnp.tile(jnp.arange(GRID_W), rows).astype(F32)
    n_pairs = rot_dim // 4
    inv = ROPE_BASE ** (-jnp.arange(n_pairs, dtype=F32) / n_pairs)
    ang = jnp.concatenate([row[:, None] * inv, col[:, None] * inv], axis=-1)
    c, s = jnp.cos(ang), jnp.sin(ang)
    return jnp.repeat(c, 2, axis=-1), jnp.stack([-s, s], axis=-1).reshape(n_tokens, rot_dim)


def _layer_weights(w_in_l, w_uq_l):
    o = [0]
    for n in IN_SIZES:
        o.append(o[-1] + n)
    seg = lambda i: w_in_l[:, o[i]:o[i + 1]]
    zeros = lambda n: jnp.zeros((D_MODEL, n), w_in_l.dtype)
    wp = jnp.concatenate(
        [seg(0), seg(1), seg(2), seg(3), seg(7), seg(11), seg(12), seg(4), seg(5),
         zeros(S_KPE), seg(6), seg(9), seg(10), zeros(LANE - S_B - 8), seg(8)], axis=1).astype(BF16)
    wab = jnp.concatenate([seg(9), seg(10)], axis=1).T.astype(BF16)
    hd = QK_NOPE_B + QK_ROPE_B
    wuq = jnp.pad(w_uq_l.reshape(Q_RANK_B, N_HEADS_B, hd), ((0, 0), (0, 0), (0, MLA_HW - hd)))
    return wp, wab, wuq.reshape(Q_RANK_B, N_HEADS_B * MLA_HW).astype(BF16)


def kernel(x_prompt, x_sample, cache_attn_k, cache_attn_v, cache_mla_ckv, cache_mla_kpe, state_gdn, c, c_ctx,
           norm_g, w_ada, b_ada, w_in, attn_sink, mla_q_norm, mla_w_uq, mla_kv_norm, mla_w_ukv, gdn_conv,
           gdn_a_log, gdn_dt_bias, gdn_norm, w_branch_a, w_branch_b, w_branch_c, w_out, final_norm_g):
    depth = w_in.shape[0]
    nb_c, seq_c, _ = x_prompt.shape
    nb_l, seq_l, _ = x_sample.shape
    past = cache_attn_k.shape[2]
    assert P_END == 7680 and seq_c % TM == 0 and seq_l % TM == 0 and nb_l < 8

    cond8 = jnp.zeros((8, D_MODEL), F32).at[:nb_l].set(c).at[nb_l].set(c_ctx)
    mods = _modulation(cond8, w_ada, b_ada)

    c_a, s_a = _rope_tables(seq_l, HD_A)
    c_b, s_b = _rope_tables(seq_l, QK_ROPE_B)
    rope_tabs = (jnp.tile(c_a, (1, LANE // HD_A)), jnp.tile(s_a, (1, LANE // HD_A)),
                 jnp.concatenate([jnp.ones((seq_l, S_KPE), F32), c_b, jnp.ones((seq_l, LANE - S_KPE - QK_ROPE_B), F32)], 1),
                 jnp.concatenate([jnp.zeros((seq_l, S_KPE), F32), s_b, jnp.zeros((seq_l, LANE - S_KPE - QK_ROPE_B), F32)], 1))

    tps_c, tps_l = seq_c // TM, seq_l // TM
    y_p = x_prompt.reshape(nb_c * seq_c, D_MODEL)
    y_s = x_sample.reshape(nb_l * seq_l, D_MODEL)
    ks, vs, ckvs, kpes, sts = [], [], [], [], []
    for l in range(depth):
        last = l == depth - 1
        wp, wab, wuq = _layer_weights(w_in[l], mla_w_uq[l])
        wukv = mla_w_ukv[l].astype(BF16)
        mod = mods[l].reshape(8, 1, 3 * D_MODEL)
        ng = norm_g[l].reshape(1, D_MODEL)
        qn = mla_q_norm[l].reshape(1, Q_RANK_B)
        kvn = mla_kv_norm[l].reshape(1, KV_RANK_B)
        sink = attn_sink[l].reshape(1, N_HEADS_A)
        prow = jnp.stack([gdn_a_log[l].reshape(-1), gdn_dt_bias[l].reshape(-1)], axis=0)
        pcol = prow.T
        gn = gdn_norm[l].reshape(1, DV_C)
        wa, wb, wc, wo = (w.astype(BF16) for w in (w_branch_a[l], w_branch_b[l], w_branch_c[l], w_out[l]))
        fg = final_norm_g.reshape(1, D_MODEL)

        mod_row_c = lambda i: nb_l
        (qkv, z, gates, qb, kvb, small, kpe, cqkv, abt, kva, ckvn) = _inproj(
            y_p, mod, mod_row_c, ng, wp, wab, wuq, wukv, qn, kvn, None, tps_c, True)
        oa = _attn_a_ctx(qkv, z, sink, seq_c)
        ob = _attn_b(qb, kvb, kpe, z, seq_c, seq_c)
        u, w, qg, kd, attn, eg = _gdn_local(cqkv, small, abt, gdn_conv[l], prow, pcol, tps_c)
        cf, cb, st = _gdn_scan(u, w, qg, kd, attn, eg, None, seq_c, True)
        y_p = _merge(y_p, mod, mod_row_c, oa, ob, cf, cb, z, gates, gn, wa, wb, wc, wo, fg, last)
        ks.append(kva[:, :N_KV_A * HD_A].reshape(nb_c, seq_c, N_KV_A, HD_A))
        vs.append(kva[:, N_KV_A * HD_A:].reshape(nb_c, seq_c, N_KV_A, HD_A))
        ckvs.append(ckvn.reshape(nb_c, seq_c, KV_RANK_B))
        kpes.append(small[:, S_KPE:S_KPE + QK_ROPE_B].reshape(nb_c, seq_c, QK_ROPE_B))
        sts.append(st)

        mod_row_l = lambda i: i // tps_l
        (qkv, z, gates, qb, kvb, small, kpe, cqkv, abt) = _inproj(
            y_s, mod, mod_row_l, ng, wp, wab, wuq, wukv, qn, kvn, rope_tabs, tps_l, False)
        kx = cache_attn_k[:, l].reshape(nb_l, past, N_KV_A * HD_A)
        vx = cache_attn_v[:, l].reshape(nb_l, past, N_KV_A * HD_A)
        oa = _attn_a_lat(qkv, z, sink, kx, vx, seq_l)
        kvx = _kvup(cache_mla_ckv[:, l].reshape(nb_l * past, KV_RANK_B), wukv)
        kpex = jnp.pad(cache_mla_kpe[:, l], ((0, 0), (0, 0), (S_KPE, LANE - S_KPE - QK_ROPE_B))).astype(BF16)
        ob = _attn_b(qb, kvb, kpe, z, seq_l, TM, kvx, kpex)
        u, w, qg, kd, attn, eg = _gdn_local(cqkv, small, abt, gdn_conv[l], prow, pcol, tps_l)
        cf, cb = _gdn_scan(u, w, qg, kd, attn, eg, state_gdn[:, l], seq_l, False)
        y_s = _merge(y_s, mod, mod_row_l, oa, ob, cf, cb, z, gates, gn, wa, wb, wc, wo, fg, last)

    return (y_p.reshape(nb_c, seq_c, D_MODEL), y_s.reshape(nb_l, seq_l, D_MODEL),
            jnp.stack(ks, axis=1), jnp.stack(vs, axis=1), jnp.stack(ckvs, axis=1), jnp.stack(kpes, axis=1),
            jnp.stack(sts, axis=1))
```

```python
import functools

import jax
import jax.numpy as jnp
from jax import lax
from jax.experimental import pallas as pl
from jax.experimental.pallas import tpu as pltpu

F32 = jnp.float32
BF16 = jnp.bfloat16

D_MODEL = 1024
GRID_W = 64
ROPE_BASE = 10000.0
EPS = 1e-6
NEG_INF = -1e30
N_HEADS_A = 8
N_KV_A = 2
HD_A = 64
GQA_GROUP = N_HEADS_A // N_KV_A
WINDOW = 128
N_HEADS_B = 8
QK_NOPE_B = 64
QK_ROPE_B = 32
V_HD_B = 64
Q_RANK_B = 384
KV_RANK_B = 256
MLA_SCALE = (QK_NOPE_B + QK_ROPE_B) ** -0.5
N_HEADS_C = 4
DK_C = 128
DV_C = 128
CHUNK = 64
W_A = N_HEADS_A * HD_A
W_B = N_HEADS_B * V_HD_B
W_C = N_HEADS_C * DV_C
QKV_C = 2 * N_HEADS_C * DK_C + W_C
IN_SIZES = (W_A, N_KV_A * HD_A, N_KV_A * HD_A, W_A, Q_RANK_B, KV_RANK_B, QK_ROPE_B, W_B, QKV_C,
            2 * N_HEADS_C, 2 * N_HEADS_C, W_C, 3 * D_MODEL)

LANE = 128
HALF = LANE // 2
TM = 256
CPT = TM // CHUNK
HALO = 8
MLA_HW = 128
KA_COLS = 4 * LANE
VA_COLS = 2 * LANE
KVB_COLS = N_HEADS_B * MLA_HW + W_B

P_QKV = 0
A_COLS = W_A + 4 * LANE
P_Z = P_QKV + A_COLS
P_GATES = P_Z + 1536
P_CQ = P_GATES + 3 * D_MODEL
P_CKV = P_CQ + Q_RANK_B
P_SMALL = P_CKV + KV_RANK_B
P_CQKV = P_SMALL + LANE
P_END = P_CQKV + QKV_C
S_KPE = 64
S_A = 96
S_B = 104


def _sigmoid(x):
    return 1.0 / (1.0 + jnp.exp(-x))


def _silu(x):
    return x * _sigmoid(x)


def _softplus(x):
    return jnp.maximum(x, 0.0) + jnp.log(1.0 + jnp.exp(-jnp.abs(x)))


def _dot(a, b):
    return jnp.dot(a, b, preferred_element_type=F32)


def _dot_nt(a, b):
    return lax.dot_general(a, b, (((1,), (1,)), ((), ())), preferred_element_type=F32)


def _bdot(a, b):
    return lax.dot_general(a, b, (((2,), (1,)), ((0,), (0,))), preferred_element_type=F32)


def _bdot_nt(a, b):
    return lax.dot_general(a, b, (((2,), (2,)), ((0,), (0,))), preferred_element_type=F32)


def _bdot_tn(a, b):
    return lax.dot_general(a, b, (((1,), (1,)), ((0,), (0,))), preferred_element_type=F32)


def _dot_exact(a, b):
    return jnp.dot(a, b, preferred_element_type=F32, precision=lax.Precision.HIGHEST)


def _rope(x, c, s):
    n = x.shape[-1]
    lane = lax.broadcasted_iota(jnp.int32, x.shape, 1)
    swapped = jnp.where(lane % 2 == 0, pltpu.roll(x, n - 1, 1), pltpu.roll(x, 1, 1))
    return x * c + swapped * s


def _mod_kernel(cond_ref, w_ref, b_ref, out_ref):
    cnd = cond_ref[...]
    out_ref[0] = _dot(_silu(cnd).astype(BF16), w_ref[0].astype(BF16)) + b_ref[0]


def _modulation(cond8, w_ada, b_ada):
    depth = w_ada.shape[0]
    tn = 768
    return pl.pallas_call(
        _mod_kernel,
        grid=(depth, 3 * D_MODEL // tn),
        in_specs=[pl.BlockSpec((8, D_MODEL), lambda l, n: (0, 0)),
                  pl.BlockSpec((1, D_MODEL, tn), lambda l, n: (l, 0, n)),
                  pl.BlockSpec((1, 1, tn), lambda l, n: (l, 0, n))],
        out_specs=pl.BlockSpec((1, 8, tn), lambda l, n: (l, 0, n)),
        out_shape=jax.ShapeDtypeStruct((depth, 8, 3 * D_MODEL), F32),
        name="adaln_mod",
    )(cond8, w_ada, b_ada.reshape(depth, 1, 3 * D_MODEL))


def _inproj_kernel(*refs, rope, ctx):
    it = iter(refs)
    x_ref, mod_ref, ng_ref, wp_ref, wab_ref, wuq_ref, wukv_ref, qn_ref, kvn_ref = (next(it) for _ in range(9))
    if rope:
        ca_ref, sa_ref, cb_ref, sb_ref = (next(it) for _ in range(4))
    (qa_ref, ka_ref, va_ref, z_ref, gates_ref, qb_ref, kb_ref, vb_ref, small_ref, cqkv_ref,
     abt_ref) = (next(it) for _ in range(11))
    if ctx:
        kva_ref, ckvn_ref = (next(it) for _ in range(2))

    x = x_ref[...]
    mod = mod_ref[0]
    shift, scale = mod[:, :D_MODEL], mod[:, D_MODEL:2 * D_MODEL]
    xn = x * lax.rsqrt(jnp.mean(x * x, axis=-1, keepdims=True) + EPS) * ng_ref[...]
    hb = (xn * (1.0 + scale) + shift).astype(BF16)
    lane = lax.broadcasted_iota(jnp.int32, (TM, LANE), 1)
    lo = lane < HALF

    def mm(lo_col, hi_col):
        return _dot(hb, wp_ref[:, lo_col:hi_col])

    r = mm(P_QKV, P_QKV + A_COLS)
    tiles = [r[:, t * LANE:(t + 1) * LANE] for t in range(A_COLS // LANE)]
    if ctx:
        kva_ref[:, :LANE] = tiles[4]
        kva_ref[:, LANE:] = tiles[6]
    if rope:
        ca, sa = ca_ref[...], sa_ref[...]
        tiles[:6] = [_rope(t, ca, sa) for t in tiles[:6]]
    for t in range(4):
        qa_ref[:, t * LANE:(t + 1) * LANE] = (tiles[t] * (HD_A ** -0.5)).astype(BF16)
    k01, k10 = tiles[4], tiles[5]
    ka_ref[:, 0 * LANE:1 * LANE] = jnp.where(lo, k01, 0.0).astype(BF16)
    ka_ref[:, 1 * LANE:2 * LANE] = jnp.where(lo, 0.0, k10).astype(BF16)
    ka_ref[:, 2 * LANE:3 * LANE] = jnp.where(lo, k10, 0.0).astype(BF16)
    ka_ref[:, 3 * LANE:4 * LANE] = jnp.where(lo, 0.0, k01).astype(BF16)
    va_ref[:, :LANE] = tiles[6].astype(BF16)
    va_ref[:, LANE:] = tiles[7].astype(BF16)

    for t in range(3):
        z_ref[:, t * 512:(t + 1) * 512] = mm(P_Z + t * 512, P_Z + (t + 1) * 512).astype(BF16)
    for t in range(6):
        gates_ref[:, t * 512:(t + 1) * 512] = mm(P_GATES + t * 512, P_GATES + (t + 1) * 512).astype(BF16)

    r = mm(P_CQ, P_CQ + Q_RANK_B)
    qn = r * lax.rsqrt(jnp.mean(r * r, axis=-1, keepdims=True) + EPS) * qn_ref[...]
    q = _dot(qn.astype(BF16), wuq_ref[...])
    if rope:
        cb, sb = cb_ref[...], sb_ref[...]
        for h in range(N_HEADS_B):
            seg = _rope(q[:, h * MLA_HW:(h + 1) * MLA_HW], cb, sb) * MLA_SCALE
            qb_ref[:, h * MLA_HW:(h + 1) * MLA_HW] = seg.astype(BF16)
    else:
        qb_ref[...] = (q * MLA_SCALE).astype(BF16)

    r = mm(P_SMALL, P_SMALL + LANE)
    small_ref[...] = r
    kp = _rope(r, cb, sb) if rope else r
    kp = jnp.where((lane >= S_KPE) & (lane < S_KPE + QK_ROPE_B), kp, 0.0)

    r = mm(P_CKV, P_CKV + KV_RANK_B)
    cn = r * lax.rsqrt(jnp.mean(r * r, axis=-1, keepdims=True) + EPS) * kvn_ref[...]
    if ctx:
        ckvn_ref[...] = cn
    kv = _dot(cn.astype(BF16), wukv_ref[...])
    for h in range(N_HEADS_B):
        kb_ref[:, h * MLA_HW:(h + 1) * MLA_HW] = (kv[:, h * MLA_HW:(h + 1) * MLA_HW] + kp).astype(BF16)
    vb_ref[...] = kv[:, N_HEADS_B * MLA_HW:].astype(BF16)

    for t in range(3):
        cqkv_ref[:, t * 512:(t + 1) * 512] = mm(P_CQKV + t * 512, P_CQKV + (t + 1) * 512)

    for c in range(CPT):
        abt_ref[c] = _dot_nt(wab_ref[...], hb[c * CHUNK:(c + 1) * CHUNK])


def _inproj(x2d, mod, mod_row_fn, ng, wp, wab, wuq, wukv, qn, kvn, rope_tabs, tiles_per_seq, ctx):
    t = x2d.shape[0]
    nt = t // TM
    rope = rope_tabs is not None
    const2 = lambda i: (0, 0)
    row = lambda i: (i, 0)
    in_specs = [pl.BlockSpec((TM, D_MODEL), row),
                pl.BlockSpec((1, 1, 3 * D_MODEL), lambda i: (mod_row_fn(i), 0, 0)),
                pl.BlockSpec((1, D_MODEL), const2),
                pl.BlockSpec((D_MODEL, P_END), const2),
                pl.BlockSpec((16, D_MODEL), const2),
                pl.BlockSpec((Q_RANK_B, N_HEADS_B * MLA_HW), const2),
                pl.BlockSpec((KV_RANK_B, KVB_COLS), const2),
                pl.BlockSpec((1, Q_RANK_B), const2),
                pl.BlockSpec((1, KV_RANK_B), const2)]
    args = [x2d, mod, ng, wp, wab, wuq, wukv, qn, kvn]
    if rope:
        pos = lambda i: (i % tiles_per_seq, 0)
        in_specs += [pl.BlockSpec((TM, LANE), pos)] * 4
        args += list(rope_tabs)
    widths = [(W_A, BF16), (KA_COLS, BF16), (VA_COLS, BF16), (1536, BF16), (3 * D_MODEL, BF16),
              (N_HEADS_B * MLA_HW, BF16), (N_HEADS_B * MLA_HW, BF16), (W_B, BF16), (LANE, F32), (QKV_C, F32)]
    out_shape = [jax.ShapeDtypeStruct((t, w), dt) for w, dt in widths]
    out_specs = [pl.BlockSpec((TM, w), row) for w, _ in widths]
    out_shape.append(jax.ShapeDtypeStruct((t // CHUNK, 16, CHUNK), F32))
    out_specs.append(pl.BlockSpec((CPT, 16, CHUNK), lambda i: (i, 0, 0)))
    if ctx:
        out_shape += [jax.ShapeDtypeStruct((t, 2 * N_KV_A * HD_A), F32), jax.ShapeDtypeStruct((t, KV_RANK_B), F32)]
        out_specs += [pl.BlockSpec((TM, 2 * N_KV_A * HD_A), row), pl.BlockSpec((TM, KV_RANK_B), row)]
    return pl.pallas_call(
        functools.partial(_inproj_kernel, rope=rope, ctx=ctx),
        grid=(nt,),
        in_specs=in_specs,
        out_specs=out_specs,
        out_shape=out_shape,
        compiler_params=pltpu.CompilerParams(dimension_semantics=("parallel",)),
        name="inproj_ctx" if ctx else "inproj_lat",
    )(*args)


def _kvup_kernel(c_ref, kpe_ref, w_ref, k_ref, v_ref):
    kv = _dot(c_ref[...].astype(BF16), w_ref[...])
    kp = kpe_ref[...]
    for h in range(N_HEADS_B):
        k_ref[:, h * MLA_HW:(h + 1) * MLA_HW] = (kv[:, h * MLA_HW:(h + 1) * MLA_HW] + kp).astype(BF16)
    v_ref[...] = kv[:, N_HEADS_B * MLA_HW:].astype(BF16)


def _kvup(ckv2d, kpe2d, wukv):
    t = ckv2d.shape[0]
    tm = 512
    row = lambda i: (i, 0)
    return pl.pallas_call(
        _kvup_kernel,
        grid=(t // tm,),
        in_specs=[pl.BlockSpec((tm, KV_RANK_B), row),
                  pl.BlockSpec((tm, LANE), row),
                  pl.BlockSpec((KV_RANK_B, KVB_COLS), lambda i: (0, 0))],
        out_specs=[pl.BlockSpec((tm, N_HEADS_B * MLA_HW), row), pl.BlockSpec((tm, W_B), row)],
        out_shape=[jax.ShapeDtypeStruct((t, N_HEADS_B * MLA_HW), BF16), jax.ShapeDtypeStruct((t, W_B), BF16)],
        name="mla_cache_up",
    )(ckv2d, kpe2d, wukv)


def _attend(q_tiles, k_tiles, v_tiles, bias, sink):
    s = _bdot_nt(jnp.stack(q_tiles), jnp.stack(k_tiles))
    if bias is not None:
        s = s + bias[None]
    m = jnp.max(s, axis=-1, keepdims=True)
    if sink is not None:
        m = jnp.maximum(m, sink)
    e = jnp.exp(s - m)
    den = jnp.sum(e, axis=-1, keepdims=True)
    if sink is not None:
        den = den + jnp.exp(sink - m)
    return _bdot(e.astype(BF16), jnp.stack(v_tiles)) / den


def _tile(x, t):
    return x[:, t * LANE:(t + 1) * LANE]


def _attn_a_heads(q, ka, va, bias, sink_ref):
    qs, ks, vs = [], [], []
    for t in range(N_HEADS_A // 2):
        g = (2 * t) // GQA_GROUP
        for e in range(2):
            qs.append(_tile(q, t))
            ks.append(_tile(ka, 2 * g + e))
            vs.append(_tile(va, (g + e) % 2))
    sink = jnp.stack([sink_ref[:, h:h + 1] for h in range(N_HEADS_A)])
    o = _attend(qs, ks, vs, bias, sink)
    lo = lax.broadcasted_iota(jnp.int32, o.shape[1:], 1) < HALF
    return [jnp.where(lo, o[2 * t], o[2 * t + 1]) for t in range(N_HEADS_A // 2)]


def _attn_a_ctx_kernel(q_ref, ka_ref, va_ref, z_ref, sink_ref, o_ref):
    outs = _attn_a_heads(q_ref[...], ka_ref[...], va_ref[...], None, sink_ref)
    for t, o in enumerate(outs):
        z = _tile(z_ref, t).astype(F32)
        o_ref[:, t * LANE:(t + 1) * LANE] = (o * _silu(z)).astype(BF16)


def _attn_a_ctx(qa, ka, va, z, sink, seq):
    t = qa.shape[0]
    row = lambda b: (b, 0)
    return pl.pallas_call(
        _attn_a_ctx_kernel,
        grid=(t // seq,),
        in_specs=[pl.BlockSpec((seq, W_A), row),
                  pl.BlockSpec((seq, KA_COLS), row),
                  pl.BlockSpec((seq, VA_COLS), row),
                  pl.BlockSpec((seq, W_A), row),
                  pl.BlockSpec((1, N_HEADS_A), lambda b: (0, 0))],
        out_specs=pl.BlockSpec((seq, W_A), row),
        out_shape=jax.ShapeDtypeStruct((t, W_A), BF16),
        compiler_params=pltpu.CompilerParams(dimension_semantics=("parallel",)),
        name="attn_a_ctx",
    )(qa, ka, va, z, sink)


def _attn_a_lat_kernel(q_ref, kp_ref, kc_ref, kn_ref, vp_ref, vc_ref, vn_ref, kx_ref, vx_ref, z_ref, sink_ref,
                       o_ref, *, nq):
    j = pl.program_id(1)
    ka = jnp.concatenate([kp_ref[...], kc_ref[...], kn_ref[...], kx_ref[0]], axis=0)
    va = jnp.concatenate([vp_ref[...], vc_ref[...], vn_ref[...], vx_ref[0]], axis=0)
    n_loc = TM + 2 * WINDOW
    qi = lax.broadcasted_iota(jnp.int32, (TM, ka.shape[0]), 0)
    kj = lax.broadcasted_iota(jnp.int32, (TM, ka.shape[0]), 1)
    ok = (kj >= qi) & (kj <= qi + 2 * WINDOW)
    ok = ok & ((kj >= WINDOW) | (j > 0)) & ((kj < TM + WINDOW) | (j < nq - 1))
    bias = jnp.where(ok | (kj >= n_loc), 0.0, NEG_INF)
    outs = _attn_a_heads(q_ref[...], ka, va, bias, sink_ref)
    for t, o in enumerate(outs):
        z = _tile(z_ref, t).astype(F32)
        o_ref[:, t * LANE:(t + 1) * LANE] = (o * _silu(z)).astype(BF16)


def _attn_a_lat(qa, ka, va, z, sink, kx, vx, seq):
    t = qa.shape[0]
    nq = seq // TM
    past = kx.shape[1]
    r = TM // WINDOW
    row = lambda b, j: (b * nq + j, 0)
    prev = lambda b, j: ((b * nq + j) * r - jnp.where(j > 0, 1, 0), 0)
    nxt = lambda b, j: ((b * nq + j) * r + jnp.where(j < nq - 1, r, r - 1), 0)
    return pl.pallas_call(
        functools.partial(_attn_a_lat_kernel, nq=nq),
        grid=(t // seq, nq),
        in_specs=[pl.BlockSpec((TM, W_A), row),
                  pl.BlockSpec((WINDOW, KA_COLS), prev),
                  pl.BlockSpec((TM, KA_COLS), row),
                  pl.BlockSpec((WINDOW, KA_COLS), nxt),
                  pl.BlockSpec((WINDOW, VA_COLS), prev),
                  pl.BlockSpec((TM, VA_COLS), row),
                  pl.BlockSpec((WINDOW, VA_COLS), nxt),
                  pl.BlockSpec((1, past, KA_COLS), lambda b, j: (b, 0, 0)),
                  pl.BlockSpec((1, past, VA_COLS), lambda b, j: (b, 0, 0)),
                  pl.BlockSpec((TM, W_A), row),
                  pl.BlockSpec((1, N_HEADS_A), lambda b, j: (0, 0))],
        out_specs=pl.BlockSpec((TM, W_A), row),
        out_shape=jax.ShapeDtypeStruct((t, W_A), BF16),
        compiler_params=pltpu.CompilerParams(dimension_semantics=("parallel", "parallel")),
        name="attn_a_lat",
    )(qa, ka, ka, ka, va, va, va, kx, vx, z, sink)


def _attn_b_kernel(*refs, has_ctx, group):
    if has_ctx:
        q_ref, k_ref, v_ref, kx_ref, vx_ref, z_ref, o_ref = refs
        kb = jnp.concatenate([k_ref[...], kx_ref[...]], axis=0)
        vb = jnp.concatenate([v_ref[...], vx_ref[...]], axis=0)
    else:
        q_ref, k_ref, v_ref, z_ref, o_ref = refs
        kb, vb = k_ref[...], v_ref[...]
    q = q_ref[...]
    for h0 in range(0, N_HEADS_B, group):
        heads = range(h0, h0 + group)
        o = _attend([_tile(q, h) for h in heads], [_tile(kb, h) for h in heads],
                    [_tile(vb, h // 2) for h in heads], None, None)
        lo = lax.broadcasted_iota(jnp.int32, o.shape[1:], 1) < HALF
        for i in range(group // 2):
            t = h0 // 2 + i
            z = _tile(z_ref, t).astype(F32)
            o_ref[:, t * LANE:(t + 1) * LANE] = (jnp.where(lo, o[2 * i], o[2 * i + 1]) * _silu(z)).astype(BF16)


def _attn_b(qb, kb, vb, z, seq, qblk, group, kx=None, vx=None):
    t = qb.shape[0]
    nq = seq // qblk
    has_ctx = kx is not None
    hw = N_HEADS_B * MLA_HW
    in_specs = [pl.BlockSpec((qblk, hw), lambda b, j: (b * nq + j, 0)),
                pl.BlockSpec((seq, hw), lambda b, j: (b, 0)),
                pl.BlockSpec((seq, W_B), lambda b, j: (b, 0))]
    args = [qb, kb, vb]
    if has_ctx:
        past = kx.shape[0] // (t // seq)
        in_specs += [pl.BlockSpec((past, hw), lambda b, j: (b, 0)),
                     pl.BlockSpec((past, W_B), lambda b, j: (b, 0))]
        args += [kx, vx]
    in_specs.append(pl.BlockSpec((qblk, W_B), lambda b, j: (b * nq + j, 1)))
    args.append(z)
    return pl.pallas_call(
        functools.partial(_attn_b_kernel, has_ctx=has_ctx, group=group),
        grid=(t // seq, nq),
        in_specs=in_specs,
        out_specs=pl.BlockSpec((qblk, W_B), lambda b, j: (b * nq + j, 0)),
        out_shape=jax.ShapeDtypeStruct((t, W_B), BF16),
        compiler_params=pltpu.CompilerParams(dimension_semantics=("parallel", "parallel")),
        name="attn_b_lat" if has_ctx else "attn_b_ctx",
    )(*args)


def _gdn_local_kernel(cq_ref, prev_ref, next_ref, small_ref, abt_ref, cw_ref, prow_ref, pcol_ref,
                      u_ref, w_ref, qg_ref, kd_ref, attn_ref, eg_ref, qkv_scr, gb_scr, *, tiles_per_seq):
    tpos = pl.program_id(0) % tiles_per_seq
    x = cq_ref[...]
    prev_row = jnp.where(tpos > 0, prev_ref[HALO - 1:HALO, :], 0.0)
    next_row = jnp.where(tpos < tiles_per_seq - 1, next_ref[0:1, :], 0.0)
    rows = lax.broadcasted_iota(jnp.int32, (TM, 1), 0)
    xm1 = jnp.where(rows == 0, prev_row, pltpu.roll(x, 1, 0))
    xp1 = jnp.where(rows == TM - 1, next_row, pltpu.roll(x, TM - 1, 0))
    cw = cw_ref[...]
    y = _silu(xm1 * cw[0:1] + x * cw[1:2] + xp1 * cw[2:3])
    nq = N_HEADS_C * DK_C
    for h in range(N_HEADS_C):
        qh = y[:, h * DK_C:(h + 1) * DK_C]
        kh = y[:, nq + h * DK_C:nq + (h + 1) * DK_C]
        qkv_scr[:, h * DK_C:(h + 1) * DK_C] = (
            qh * lax.rsqrt(jnp.sum(qh * qh, axis=-1, keepdims=True) + EPS) * (DK_C ** -0.5))
        qkv_scr[:, nq + h * DK_C:nq + (h + 1) * DK_C] = kh * lax.rsqrt(jnp.sum(kh * kh, axis=-1, keepdims=True) + EPS)
    qkv_scr[:, 2 * nq:] = y[:, 2 * nq:]

    sm = small_ref[...]
    prow = prow_ref[...]
    gb_scr[:, 0:8] = -jnp.exp(prow[0:1]) * _softplus(sm[:, S_A:S_A + 8] + prow[1:2])
    gb_scr[:, 8:16] = _sigmoid(sm[:, S_B:S_B + 8])
    pcol = pcol_ref[...]

    ri = lax.broadcasted_iota(jnp.int32, (CHUNK, CHUNK), 0)
    ci = lax.broadcasted_iota(jnp.int32, (CHUNK, CHUNK), 1)
    tril = (ri >= ci).astype(F32)
    triu = (ri <= ci).astype(F32)
    xor = ri ^ ci
    eye = (ri == ci).astype(F32)

    def chunk_body(c, carry):
        r0 = pl.multiple_of(c * CHUNK, CHUNK)
        rs = pl.ds(r0, CHUNK)
        gcol = gb_scr[rs, 0:8]
        bcol = gb_scr[rs, 8:16]
        abt = abt_ref[c]
        grow = -jnp.exp(pcol[:, 0:1]) * _softplus(abt[0:8] + pcol[:, 1:2])
        gc_f = _dot_exact(tril, gcol)
        gc_b = _dot_exact(triu, gcol)
        gr_f = _dot_exact(grow, triu)
        gr_b = _dot_exact(grow, tril)
        lows, rhss, order = [], [], []
        for h in range(N_HEADS_C):
            q = qkv_scr[rs, h * DK_C:(h + 1) * DK_C]
            k = qkv_scr[rs, nq + h * DK_C:nq + (h + 1) * DK_C]
            v = qkv_scr[rs, 2 * nq + h * DV_C:2 * nq + (h + 1) * DV_C]
            kb16 = k.astype(BF16)
            kk = _dot_nt(kb16, kb16)
            qk = _dot_nt(q.astype(BF16), kb16)
            for d in range(2):
                dh = d * N_HEADS_C + h
                gc = (gc_f if d == 0 else gc_b)[:, dh:dh + 1]
                gr = (gr_f if d == 0 else gr_b)[dh:dh + 1, :]
                beta = bcol[:, dh:dh + 1]
                incl = (ri >= ci) if d == 0 else (ri <= ci)
                strict = (ri > ci) if d == 0 else (ri < ci)
                decay = jnp.where(incl, jnp.exp(jnp.where(incl, gc - gr, 0.0)), 0.0)
                lows.append(jnp.where(strict, beta * kk * decay, 0.0))
                eg = jnp.exp(gc)
                rhss.append(jnp.concatenate([v * beta, k * (beta * eg)], axis=-1).astype(BF16))
                order.append(dh)
                g_last = gc[CHUNK - 1:CHUNK] if d == 0 else gc[0:1]
                cs = slice(dh * DK_C, (dh + 1) * DK_C)
                qg_ref[rs, cs] = (q * eg).astype(BF16)
                kd_ref[rs, cs] = (k * jnp.exp(g_last - gc)).astype(BF16)
                attn_ref[rs, dh * CHUNK:(dh + 1) * CHUNK] = (qk * decay).astype(BF16)
                eg_ref[c, dh:dh + 1, :] = jnp.broadcast_to(jnp.exp(g_last), (1, LANE))
        low = jnp.stack(lows, axis=0)
        inv = eye[None] - jnp.where(xor[None] == 1, low, 0.0)
        b = 2
        while b < CHUNK:
            cpl = jnp.where((xor[None] >= b) & (xor[None] < 2 * b), low, 0.0)
            tmp = _bdot(cpl.astype(BF16), inv.astype(BF16))
            inv = inv - _bdot(inv.astype(BF16), tmp.astype(BF16))
            b *= 2
        uw = _bdot(inv.astype(BF16), jnp.stack(rhss, axis=0))
        for i, dh in enumerate(order):
            cs = slice(dh * DK_C, (dh + 1) * DK_C)
            u_ref[rs, cs] = uw[i, :, :DV_C]
            w_ref[rs, cs] = uw[i, :, DV_C:].astype(BF16)
        return carry

    lax.fori_loop(0, CPT, chunk_body, 0)


def _gdn_local(cqkv, small, abt, conv_w, prow, pcol, tiles_per_seq):
    t = cqkv.shape[0]
    nt = t // TM
    nh8 = t // HALO
    row = lambda i: (i, 0)
    dh = 2 * N_HEADS_C
    return pl.pallas_call(
        functools.partial(_gdn_local_kernel, tiles_per_seq=tiles_per_seq),
        grid=(nt,),
        in_specs=[pl.BlockSpec((TM, QKV_C), row),
                  pl.BlockSpec((HALO, QKV_C), lambda i: (jnp.maximum(i * (TM // HALO) - 1, 0), 0)),
                  pl.BlockSpec((HALO, QKV_C), lambda i: (jnp.minimum((i + 1) * (TM // HALO), nh8 - 1), 0)),
                  pl.BlockSpec((TM, LANE), row),
                  pl.BlockSpec((CPT, 16, CHUNK), lambda i: (i, 0, 0)),
                  pl.BlockSpec((3, QKV_C), lambda i: (0, 0)),
                  pl.BlockSpec((2, dh), lambda i: (0, 0)),
                  pl.BlockSpec((dh, 2), lambda i: (0, 0))],
        out_specs=[pl.BlockSpec((TM, dh * DV_C), row),
                   pl.BlockSpec((TM, dh * DK_C), row),
                   pl.BlockSpec((TM, dh * DK_C), row),
                   pl.BlockSpec((TM, dh * DK_C), row),
                   pl.BlockSpec((TM, dh * CHUNK), row),
                   pl.BlockSpec((CPT, dh, LANE), lambda i: (i, 0, 0))],
        out_shape=[jax.ShapeDtypeStruct((t, dh * DV_C), F32),
                   jax.ShapeDtypeStruct((t, dh * DK_C), BF16),
                   jax.ShapeDtypeStruct((t, dh * DK_C), BF16),
                   jax.ShapeDtypeStruct((t, dh * DK_C), BF16),
                   jax.ShapeDtypeStruct((t, dh * CHUNK), BF16),
                   jax.ShapeDtypeStruct((t // CHUNK, dh, LANE), F32)],
        scratch_shapes=[pltpu.VMEM((TM, QKV_C), F32), pltpu.VMEM((TM, 16), F32)],
        compiler_params=pltpu.CompilerParams(dimension_semantics=("parallel",)),
        name="gdn_local",
    )(cqkv, cqkv, cqkv, small, abt, conv_w, prow, pcol)


def _gdn_scan_kernel(*refs, nt, has_init, want_state):
    it = iter(refs)
    ins = [[next(it) for _ in range(6)] for _ in range(2)]
    s0_ref = next(it) if has_init else None
    o_refs = [next(it), next(it)]
    st_ref = next(it) if want_state else None
    s_scr = next(it)
    j = pl.program_id(1)
    nh = N_HEADS_C

    @pl.when(j == 0)
    def _():
        if has_init:
            s_scr[...] = s0_ref[0].reshape(2 * nh, DK_C, DV_C)
        else:
            s_scr[...] = jnp.zeros_like(s_scr)

    for step in range(CPT):
        def gather(idx, width):
            parts = []
            for d in range(2):
                c = step if d == 0 else CPT - 1 - step
                for h in range(nh):
                    parts.append(ins[d][idx][c * CHUNK:(c + 1) * CHUNK, h * width:(h + 1) * width])
            return jnp.stack(parts)

        u, w, qg, kd, attn = gather(0, DV_C), gather(1, DK_C), gather(2, DK_C), gather(3, DK_C), gather(4, CHUNK)
        eg = jnp.stack([ins[d][5][step if d == 0 else CPT - 1 - step, d * nh + h:d * nh + h + 1, :]
                        for d in range(2) for h in range(nh)])
        s = s_scr[...]
        sb = s.astype(BF16)
        v_new = u - _bdot(w, sb)
        vb = v_new.astype(BF16)
        o = _bdot(qg, sb) + _bdot(attn, vb)
        s_scr[...] = s * eg + _bdot_tn(kd, vb)
        for d in range(2):
            c = step if d == 0 else CPT - 1 - step
            for h in range(nh):
                o_refs[d][c * CHUNK:(c + 1) * CHUNK, h * DV_C:(h + 1) * DV_C] = o[d * nh + h]

    if want_state:
        @pl.when(j == nt - 1)
        def _():
            st_ref[0] = s_scr[...].reshape(2, nh, DK_C, DV_C)


def _gdn_scan(u, w, qg, kd, attn, eg, s0, seq, want_state):
    t = u.shape[0]
    nt = seq // TM
    nb = t // seq
    half = N_HEADS_C * DK_C
    has_init = s0 is not None
    in_specs, args = [], []
    for d in range(2):
        if d == 0:
            row = lambda b, j: (b * nt + j, 0)
            row3 = lambda b, j: (b * nt + j, 0, 0)
        else:
            row = lambda b, j: (b * nt + nt - 1 - j, 1)
            row3 = lambda b, j: (b * nt + nt - 1 - j, 0, 0)
        in_specs += [pl.BlockSpec((TM, half), row)] * 4
        in_specs += [pl.BlockSpec((TM, N_HEADS_C * CHUNK), row),
                     pl.BlockSpec((CPT, 2 * N_HEADS_C, LANE), row3)]
        args += [u, w, qg, kd, attn, eg]
    st_block = (1, 2, N_HEADS_C, DK_C, DV_C)
    if has_init:
        in_specs.append(pl.BlockSpec(st_block, lambda b, j: (b, 0, 0, 0, 0)))
        args.append(s0)
    out_specs = [pl.BlockSpec((TM, half), lambda b, j: (b * nt + j, 0)),
                 pl.BlockSpec((TM, half), lambda b, j: (b * nt + nt - 1 - j, 0))]
    out_shape = [jax.ShapeDtypeStruct((t, half), F32), jax.ShapeDtypeStruct((t, half), F32)]
    if want_state:
        out_specs.append(pl.BlockSpec(st_block, lambda b, j: (b, 0, 0, 0, 0)))
        out_shape.append(jax.ShapeDtypeStruct((nb,) + st_block[1:], F32))
    return pl.pallas_call(
        functools.partial(_gdn_scan_kernel, nt=nt, has_init=has_init, want_state=want_state),
        grid=(nb, nt),
        in_specs=in_specs,
        out_specs=out_specs,
        out_shape=out_shape,
        scratch_shapes=[pltpu.VMEM((2 * N_HEADS_C, DK_C, DV_C), F32)],
        compiler_params=pltpu.CompilerParams(dimension_semantics=("parallel", "arbitrary")),
        name="gdn_scan",
    )(*args)


def _merge_kernel(x_ref, mod_ref, oa_ref, ob_ref, cf_ref, cb_ref, zc_ref, gates_ref, gn_ref,
                  wa_ref, wb_ref, wc_ref, wo_ref, fg_ref, o_ref, *, last):
    oc = cf_ref[...] + cb_ref[...]
    zc = zc_ref[...].astype(F32)
    gn = gn_ref[...]
    parts = []
    for h in range(N_HEADS_C):
        hs = slice(h * DV_C, (h + 1) * DV_C)
        och = oc[:, hs]
        och = och * lax.rsqrt(jnp.mean(och * och, axis=-1, keepdims=True) + EPS) * gn
        parts.append((och * _silu(zc[:, hs])).astype(BF16))
    ocz = jnp.concatenate(parts, axis=-1)
    pa = _dot(oa_ref[...], wa_ref[...])
    pb = _dot(ob_ref[...], wb_ref[...])
    pc = _dot(ocz, wc_ref[...])
    ga = _sigmoid(gates_ref[:, 0:D_MODEL].astype(F32))
    gb = _sigmoid(gates_ref[:, D_MODEL:2 * D_MODEL].astype(F32))
    gc = _sigmoid(gates_ref[:, 2 * D_MODEL:].astype(F32))
    y = _dot((ga * pa + gb * pb + gc * pc).astype(BF16), wo_ref[...])
    gate = mod_ref[0][:, 2 * D_MODEL:]
    xo = x_ref[...] + gate * y
    if last:
        xo = xo * lax.rsqrt(jnp.mean(xo * xo, axis=-1, keepdims=True) + EPS) * fg_ref[...]
    o_ref[...] = xo


def _merge(x2d, mod, mod_row_fn, oa, ob, cf, cb, z, gates, gn, wa, wb, wc, wo, fg, last):
    t = x2d.shape[0]
    row = lambda i: (i, 0)
    const2 = lambda i: (0, 0)
    return pl.pallas_call(
        functools.partial(_merge_kernel, last=last),
        grid=(t // TM,),
        in_specs=[pl.BlockSpec((TM, D_MODEL), row),
                  pl.BlockSpec((1, 1, 3 * D_MODEL), lambda i: (mod_row_fn(i), 0, 0)),
                  pl.BlockSpec((TM, W_A), row),
                  pl.BlockSpec((TM, W_B), row),
                  pl.BlockSpec((TM, W_C), row),
                  pl.BlockSpec((TM, W_C), row),
                  pl.BlockSpec((TM, W_C), lambda i: (i, 2)),
                  pl.BlockSpec((TM, 3 * D_MODEL), row),
                  pl.BlockSpec((1, DV_C), const2),
                  pl.BlockSpec((W_A, D_MODEL), const2),
                  pl.BlockSpec((W_B, D_MODEL), const2),
                  pl.BlockSpec((W_C, D_MODEL), const2),
                  pl.BlockSpec((D_MODEL, D_MODEL), const2),
                  pl.BlockSpec((1, D_MODEL), const2)],
        out_specs=pl.BlockSpec((TM, D_MODEL), row),
        out_shape=jax.ShapeDtypeStruct((t, D_MODEL), F32),
        compiler_params=pltpu.CompilerParams(dimension_semantics=("parallel",)),
        name="merge",
    )(x2d, mod, oa, ob, cf, cb, z, gates, gn, wa, wb, wc, wo, fg)


def _rope_tables(n_tokens, rot_dim):
    rows = n_tokens // GRID_W
    row = jnp.repeat(jnp.arange(rows), GRID_W).astype(F32)
    col = jnp.tile(jnp.arange(GRID_W), rows).astype(F32)
    n_pairs = rot_dim // 4
    inv = ROPE_BASE ** (-jnp.arange(n_pairs, dtype=F32) / n_pairs)
    ang = jnp.concatenate([row[:, None] * inv, col[:, None] * inv], axis=-1)
    c, s = jnp.cos(ang), jnp.sin(ang)
    return jnp.repeat(c, 2, axis=-1), jnp.stack([-s, s], axis=-1).reshape(n_tokens, rot_dim)


def _swap_halves(x):
    return jnp.concatenate([x[..., HD_A:], x[..., :HD_A]], axis=-1)


def _layer_weights(w_in_l, w_uq_l, w_ukv_l):
    o = [0]
    for n in IN_SIZES:
        o.append(o[-1] + n)
    seg = lambda i: w_in_l[:, o[i]:o[i + 1]]
    zeros = lambda n: jnp.zeros((D_MODEL, n), w_in_l.dtype)
    wp = jnp.concatenate(
        [seg(0), seg(1), _swap_halves(seg(1)), seg(2), _swap_halves(seg(2)),
         seg(3), seg(7), seg(11), seg(12), seg(4), seg(5),
         zeros(S_KPE), seg(6), seg(9), seg(10), zeros(LANE - S_B - 8), seg(8)], axis=1).astype(BF16)
    wab = jnp.concatenate([seg(9), seg(10)], axis=1).T.astype(BF16)
    hd = QK_NOPE_B + QK_ROPE_B
    wuq = jnp.pad(w_uq_l.reshape(Q_RANK_B, N_HEADS_B, hd), ((0, 0), (0, 0), (0, MLA_HW - hd)))
    wuq = wuq.reshape(Q_RANK_B, N_HEADS_B * MLA_HW).astype(BF16)
    kv = w_ukv_l.reshape(KV_RANK_B, N_HEADS_B, QK_NOPE_B + V_HD_B)
    wk = jnp.pad(kv[:, :, :QK_NOPE_B], ((0, 0), (0, 0), (0, MLA_HW - QK_NOPE_B))).reshape(KV_RANK_B, N_HEADS_B * MLA_HW)
    wv = kv[:, :, QK_NOPE_B:].reshape(KV_RANK_B, W_B)
    return wp, wab, wuq, jnp.concatenate([wk, wv], axis=1).astype(BF16)


def _cache_tiles_a(kx, vx):
    k0, k1 = kx[:, :, 0, :], kx[:, :, 1, :]
    z = jnp.zeros_like(k0)
    ka = jnp.concatenate([k0, z, z, k0, k1, z, z, k1], axis=-1).astype(BF16)
    v0, v1 = vx[:, :, 0, :], vx[:, :, 1, :]
    va = jnp.concatenate([v0, v1, v1, v0], axis=-1).astype(BF16)
    return ka, va


def kernel(x_prompt, x_sample, cache_attn_k, cache_attn_v, cache_mla_ckv, cache_mla_kpe, state_gdn, c, c_ctx,
           norm_g, w_ada, b_ada, w_in, attn_sink, mla_q_norm, mla_w_uq, mla_kv_norm, mla_w_ukv, gdn_conv,
           gdn_a_log, gdn_dt_bias, gdn_norm, w_branch_a, w_branch_b, w_branch_c, w_out, final_norm_g):
    depth = w_in.shape[0]
    nb_c, seq_c, _ = x_prompt.shape
    nb_l, seq_l, _ = x_sample.shape
    past = cache_attn_k.shape[2]
    assert P_END % LANE == 0 and seq_c % TM == 0 and seq_l % TM == 0 and nb_l < 8 and TM == 2 * WINDOW

    cond8 = jnp.zeros((8, D_MODEL), F32).at[:nb_l].set(c).at[nb_l].set(c_ctx)
    mods = _modulation(cond8, w_ada, b_ada)

    c_a, s_a = _rope_tables(seq_l, HD_A)
    c_b, s_b = _rope_tables(seq_l, QK_ROPE_B)
    pad_l, pad_r = S_KPE, LANE - S_KPE - QK_ROPE_B
    rope_tabs = (jnp.tile(c_a, (1, LANE // HD_A)), jnp.tile(s_a, (1, LANE // HD_A)),
                 jnp.concatenate([jnp.ones((seq_l, pad_l), F32), c_b, jnp.ones((seq_l, pad_r), F32)], 1),
                 jnp.concatenate([jnp.zeros((seq_l, pad_l), F32), s_b, jnp.zeros((seq_l, pad_r), F32)], 1))

    tps_c, tps_l = seq_c // TM, seq_l // TM
    y_p = x_prompt.reshape(nb_c * seq_c, D_MODEL)
    y_s = x_sample.reshape(nb_l * seq_l, D_MODEL)
    ks, vs, ckvs, kpes, sts = [], [], [], [], []
    for l in range(depth):
        last = l == depth - 1
        wp, wab, wuq, wukv = _layer_weights(w_in[l], mla_w_uq[l], mla_w_ukv[l])
        mod = mods[l].reshape(8, 1, 3 * D_MODEL)
        ng = norm_g[l].reshape(1, D_MODEL)
        qn = mla_q_norm[l].reshape(1, Q_RANK_B)
        kvn = mla_kv_norm[l].reshape(1, KV_RANK_B)
        sink = attn_sink[l].reshape(1, N_HEADS_A)
        prow = jnp.stack([gdn_a_log[l].reshape(-1), gdn_dt_bias[l].reshape(-1)], axis=0)
        pcol = prow.T
        gn = gdn_norm[l].reshape(1, DV_C)
        wa, wb, wc, wo = (w.astype(BF16) for w in (w_branch_a[l], w_branch_b[l], w_branch_c[l], w_out[l]))
        fg = final_norm_g.reshape(1, D_MODEL)

        mod_row_c = lambda i: nb_l
        (qa, ka, va, z, gates, qb, kb, vb, small, cqkv, abt, kva, ckvn) = _inproj(
            y_p, mod, mod_row_c, ng, wp, wab, wuq, wukv, qn, kvn, None, tps_c, True)
        oa = _attn_a_ctx(qa, ka, va, z, sink, seq_c)
        ob = _attn_b(qb, kb, vb, z, seq_c, seq_c, N_HEADS_B)
        u, w, qg, kd, attn, eg = _gdn_local(cqkv, small, abt, gdn_conv[l], prow, pcol, tps_c)
        cf, cb, st = _gdn_scan(u, w, qg, kd, attn, eg, None, seq_c, True)
        y_p = _merge(y_p, mod, mod_row_c, oa, ob, cf, cb, z, gates, gn, wa, wb, wc, wo, fg, last)
        ks.append(kva[:, :N_KV_A * HD_A].reshape(nb_c, seq_c, N_KV_A, HD_A))
        vs.append(kva[:, N_KV_A * HD_A:].reshape(nb_c, seq_c, N_KV_A, HD_A))
        ckvs.append(ckvn.reshape(nb_c, seq_c, KV_RANK_B))
        kpes.append(small[:, S_KPE:S_KPE + QK_ROPE_B].reshape(nb_c, seq_c, QK_ROPE_B))
        sts.append(st)

        mod_row_l = lambda i: i // tps_l
        (qa, ka, va, z, gates, qb, kb, vb, small, cqkv, abt) = _inproj(
            y_s, mod, mod_row_l, ng, wp, wab, wuq, wukv, qn, kvn, rope_tabs, tps_l, False)
        kxa, vxa = _cache_tiles_a(cache_attn_k[:, l], cache_attn_v[:, l])
        oa = _attn_a_lat(qa, ka, va, z, sink, kxa, vxa, seq_l)
        kpex = jnp.pad(cache_mla_kpe[:, l], ((0, 0), (0, 0), (pad_l, pad_r))).reshape(nb_l * past, LANE)
        kxb, vxb = _kvup(cache_mla_ckv[:, l].reshape(nb_l * past, KV_RANK_B), kpex, wukv)
        ob = _attn_b(qb, kb, vb, z, seq_l, TM, N_HEADS_B // 2, kxb, vxb)
        u, w, qg, kd, attn, eg = _gdn_local(cqkv, small, abt, gdn_conv[l], prow, pcol, tps_l)
        cf, cb = _gdn_scan(u, w, qg, kd, attn, eg, state_gdn[:, l], seq_l, False)
        y_s = _merge(y_s, mod, mod_row_l, oa, ob, cf, cb, z, gates, gn, wa, wb, wc, wo, fg, last)

    return (y_p.reshape(nb_c, seq_c, D_MODEL), y_s.reshape(nb_l, seq_l, D_MODEL),
            jnp.stack(ks, axis=1), jnp.stack(vs, axis=1), jnp.stack(ckvs, axis=1), jnp.stack(kpes, axis=1),
            jnp.stack(sts, axis=1))
```

```python
import functools

import numpy as np
import jax
import jax.numpy as jnp
from jax import lax
from jax.experimental import pallas as pl
from jax.experimental.pallas import tpu as pltpu

F32 = jnp.float32
BF16 = jnp.bfloat16

D_MODEL = 1024
GRID_W = 64
ROPE_BASE = 10000.0
EPS = 1e-6
NEG_INF = -1e30
N_HEADS_A = 8
N_KV_A = 2
HD_A = 64
GQA_GROUP = N_HEADS_A // N_KV_A
WINDOW = 128
N_HEADS_B = 8
QK_NOPE_B = 64
QK_ROPE_B = 32
V_HD_B = 64
Q_RANK_B = 384
KV_RANK_B = 256
MLA_SCALE = (QK_NOPE_B + QK_ROPE_B) ** -0.5
N_HEADS_C = 4
DK_C = 128
DV_C = 128
CHUNK = 64
W_A = N_HEADS_A * HD_A
W_B = N_HEADS_B * V_HD_B
W_C = N_HEADS_C * DV_C
QKV_C = 2 * N_HEADS_C * DK_C + W_C
IN_SIZES = (W_A, N_KV_A * HD_A, N_KV_A * HD_A, W_A, Q_RANK_B, KV_RANK_B, QK_ROPE_B, W_B, QKV_C,
            2 * N_HEADS_C, 2 * N_HEADS_C, W_C, 3 * D_MODEL)

LANE = 128
HALF = LANE // 2
TM = 256
CPT = TM // CHUNK
HALO = 8
MLA_HW = 128
KA_COLS = 4 * LANE
VA_COLS = 2 * LANE
HB_COLS = N_HEADS_B * MLA_HW
KVB_COLS = 2 * HB_COLS
LOG2E = 1.4426950408889634

P_QKV = 0
A_COLS = W_A + 4 * LANE
P_Z = P_QKV + A_COLS
P_GATES = P_Z + 1536
P_CQ = P_GATES + 3 * D_MODEL
P_CKV = P_CQ + Q_RANK_B
P_SMALL = P_CKV + KV_RANK_B
P_CQKV = P_SMALL + LANE
P_END = P_CQKV + QKV_C
S_KPE = 64
S_A = 96
S_B = 104


def _sigmoid(x):
    return 1.0 / (1.0 + jnp.exp(-x))


def _silu(x):
    return x * _sigmoid(x)


def _softplus(x):
    return jnp.maximum(x, 0.0) + jnp.log(1.0 + jnp.exp(-jnp.abs(x)))


def _dot(a, b):
    return jnp.dot(a, b, preferred_element_type=F32)


def _dot_nt(a, b):
    return lax.dot_general(a, b, (((1,), (1,)), ((), ())), preferred_element_type=F32)


def _bdot(a, b):
    return lax.dot_general(a, b, (((2,), (1,)), ((0,), (0,))), preferred_element_type=F32)


def _bdot_nt(a, b):
    return lax.dot_general(a, b, (((2,), (2,)), ((0,), (0,))), preferred_element_type=F32)


def _bdot_tn(a, b):
    return lax.dot_general(a, b, (((1,), (1,)), ((0,), (0,))), preferred_element_type=F32)


def _dot_exact(a, b):
    return jnp.dot(a, b, preferred_element_type=F32, precision=lax.Precision.HIGHEST)


def _rope(x, c, s):
    n = x.shape[-1]
    lane = lax.broadcasted_iota(jnp.int32, x.shape, 1)
    swapped = jnp.where(lane % 2 == 0, pltpu.roll(x, n - 1, 1), pltpu.roll(x, 1, 1))
    return x * c + swapped * s


def _mod_kernel(cond_ref, w_ref, b_ref, out_ref):
    cnd = cond_ref[...]
    out_ref[0] = _dot(_silu(cnd).astype(BF16), w_ref[0].astype(BF16)) + b_ref[0]


def _modulation(cond8, w_ada, b_ada):
    depth = w_ada.shape[0]
    tn = 768
    return pl.pallas_call(
        _mod_kernel,
        grid=(depth, 3 * D_MODEL // tn),
        in_specs=[pl.BlockSpec((8, D_MODEL), lambda l, n: (0, 0)),
                  pl.BlockSpec((1, D_MODEL, tn), lambda l, n: (l, 0, n)),
                  pl.BlockSpec((1, 1, tn), lambda l, n: (l, 0, n))],
        out_specs=pl.BlockSpec((1, 8, tn), lambda l, n: (l, 0, n)),
        out_shape=jax.ShapeDtypeStruct((depth, 8, 3 * D_MODEL), F32),
        name="adaln_mod",
    )(cond8, w_ada, b_ada.reshape(depth, 1, 3 * D_MODEL))


def _inproj_kernel(*refs, rope, ctx):
    it = iter(refs)
    x_ref, mod_ref, ng_ref, wp_ref, wab_ref, wuq_ref, wukv_ref, qn_ref, kvn_ref = (next(it) for _ in range(9))
    if rope:
        ca_ref, sa_ref, cb_ref, sb_ref = (next(it) for _ in range(4))
    (qa_ref, ka_ref, va_ref, z_ref, gates_ref, qb_ref, kb_ref, vb_ref, small_ref, cqkv_ref,
     abt_ref) = (next(it) for _ in range(11))
    if ctx:
        kva_ref, ckvn_ref = (next(it) for _ in range(2))

    x = x_ref[...]
    mod = mod_ref[0]
    shift, scale = mod[:, :D_MODEL], mod[:, D_MODEL:2 * D_MODEL]
    xn = x * lax.rsqrt(jnp.mean(x * x, axis=-1, keepdims=True) + EPS) * ng_ref[...]
    hb = (xn * (1.0 + scale) + shift).astype(BF16)
    lane = lax.broadcasted_iota(jnp.int32, (TM, LANE), 1)
    lo = lane < HALF

    def mm(lo_col, hi_col):
        return _dot(hb, wp_ref[:, lo_col:hi_col])

    r = mm(P_QKV, P_QKV + A_COLS)
    tiles = [r[:, t * LANE:(t + 1) * LANE] for t in range(A_COLS // LANE)]
    if ctx:
        kva_ref[:, :LANE] = tiles[4]
        kva_ref[:, LANE:] = tiles[6]
    if rope:
        ca, sa = ca_ref[...], sa_ref[...]
        tiles[:6] = [_rope(t, ca, sa) for t in tiles[:6]]
    for t in range(4):
        qa_ref[:, t * LANE:(t + 1) * LANE] = (tiles[t] * (HD_A ** -0.5 * LOG2E)).astype(BF16)
    k01, k10 = tiles[4], tiles[5]
    ka_ref[:, 0 * LANE:1 * LANE] = jnp.where(lo, k01, 0.0).astype(BF16)
    ka_ref[:, 1 * LANE:2 * LANE] = jnp.where(lo, 0.0, k10).astype(BF16)
    ka_ref[:, 2 * LANE:3 * LANE] = jnp.where(lo, k10, 0.0).astype(BF16)
    ka_ref[:, 3 * LANE:4 * LANE] = jnp.where(lo, 0.0, k01).astype(BF16)
    va_ref[:, :LANE] = jnp.where(lo, tiles[6], 1.0).astype(BF16)
    va_ref[:, LANE:] = jnp.where(lo, tiles[7], 1.0).astype(BF16)

    for t in range(3):
        z_ref[:, t * 512:(t + 1) * 512] = mm(P_Z + t * 512, P_Z + (t + 1) * 512).astype(BF16)
    for t in range(6):
        gates_ref[:, t * 512:(t + 1) * 512] = mm(P_GATES + t * 512, P_GATES + (t + 1) * 512).astype(BF16)

    r = mm(P_CQ, P_CQ + Q_RANK_B)
    qn = r * lax.rsqrt(jnp.mean(r * r, axis=-1, keepdims=True) + EPS) * qn_ref[...]
    q = _dot(qn.astype(BF16), wuq_ref[...])
    if rope:
        cb, sb = cb_ref[...], sb_ref[...]
        for h in range(N_HEADS_B):
            seg = _rope(q[:, h * MLA_HW:(h + 1) * MLA_HW], cb, sb) * (MLA_SCALE * LOG2E)
            qb_ref[:, h * MLA_HW:(h + 1) * MLA_HW] = seg.astype(BF16)
    else:
        qb_ref[...] = (q * (MLA_SCALE * LOG2E)).astype(BF16)

    r = mm(P_SMALL, P_SMALL + LANE)
    small_ref[...] = r
    kp = _rope(r, cb, sb) if rope else r
    kp = jnp.where((lane >= S_KPE) & (lane < S_KPE + QK_ROPE_B), kp, 0.0)

    r = mm(P_CKV, P_CKV + KV_RANK_B)
    cn = r * lax.rsqrt(jnp.mean(r * r, axis=-1, keepdims=True) + EPS) * kvn_ref[...]
    if ctx:
        ckvn_ref[...] = cn
    kv = _dot(cn.astype(BF16), wukv_ref[...])
    for h in range(N_HEADS_B):
        kb_ref[:, h * MLA_HW:(h + 1) * MLA_HW] = (kv[:, h * MLA_HW:(h + 1) * MLA_HW] + kp).astype(BF16)
        vh = kv[:, HB_COLS + h * MLA_HW:HB_COLS + (h + 1) * MLA_HW]
        vb_ref[:, h * MLA_HW:(h + 1) * MLA_HW] = jnp.where(lo, vh, 1.0).astype(BF16)

    for t in range(3):
        cqkv_ref[:, t * 512:(t + 1) * 512] = mm(P_CQKV + t * 512, P_CQKV + (t + 1) * 512)

    for c in range(CPT):
        abt_ref[c] = _dot_nt(wab_ref[...], hb[c * CHUNK:(c + 1) * CHUNK])


def _layer_spec(arr, l):
    nd = arr.ndim - 1
    return pl.BlockSpec((None,) + arr.shape[1:], lambda *_: (l,) + (0,) * nd)


def _inproj(x2d, l, mod, mod_row_fn, ng, wp, wab, wuq, wukv, qn, kvn, rope_tabs, tiles_per_seq, ctx):
    t = x2d.shape[0]
    nt = t // TM
    rope = rope_tabs is not None
    row = lambda i: (i, 0)
    in_specs = [pl.BlockSpec((TM, D_MODEL), row),
                pl.BlockSpec((1, 1, 3 * D_MODEL), lambda i: (mod_row_fn(i), 0, 0))]
    in_specs += [_layer_spec(a, l) for a in (ng, wp, wab, wuq, wukv, qn, kvn)]
    args = [x2d, mod, ng, wp, wab, wuq, wukv, qn, kvn]
    if rope:
        pos = lambda i: (i % tiles_per_seq, 0)
        in_specs += [pl.BlockSpec((TM, LANE), pos)] * 4
        args += list(rope_tabs)
    widths = [(W_A, BF16), (KA_COLS, BF16), (VA_COLS, BF16), (1536, BF16), (3 * D_MODEL, BF16),
              (HB_COLS, BF16), (HB_COLS, BF16), (HB_COLS, BF16), (LANE, F32), (QKV_C, F32)]
    out_shape = [jax.ShapeDtypeStruct((t, w), dt) for w, dt in widths]
    out_specs = [pl.BlockSpec((TM, w), row) for w, _ in widths]
    out_shape.append(jax.ShapeDtypeStruct((t // CHUNK, 16, CHUNK), F32))
    out_specs.append(pl.BlockSpec((CPT, 16, CHUNK), lambda i: (i, 0, 0)))
    if ctx:
        out_shape += [jax.ShapeDtypeStruct((t, 2 * N_KV_A * HD_A), F32), jax.ShapeDtypeStruct((t, KV_RANK_B), F32)]
        out_specs += [pl.BlockSpec((TM, 2 * N_KV_A * HD_A), row), pl.BlockSpec((TM, KV_RANK_B), row)]
    return pl.pallas_call(
        functools.partial(_inproj_kernel, rope=rope, ctx=ctx),
        grid=(nt,),
        in_specs=in_specs,
        out_specs=out_specs,
        out_shape=out_shape,
        compiler_params=pltpu.CompilerParams(dimension_semantics=("parallel",)),
        name="inproj_ctx" if ctx else "inproj_lat",
    )(*args)


def _kvup_kernel(c_ref, kpe_ref, w_ref, k_ref, v_ref):
    kv = _dot(c_ref[...].astype(BF16), w_ref[...])
    kp = kpe_ref[...]
    lo = lax.broadcasted_iota(jnp.int32, kp.shape, 1) < HALF
    for h in range(N_HEADS_B):
        k_ref[:, h * MLA_HW:(h + 1) * MLA_HW] = (kv[:, h * MLA_HW:(h + 1) * MLA_HW] + kp).astype(BF16)
        vh = kv[:, HB_COLS + h * MLA_HW:HB_COLS + (h + 1) * MLA_HW]
        v_ref[:, h * MLA_HW:(h + 1) * MLA_HW] = jnp.where(lo, vh, 1.0).astype(BF16)


def _kvup(ckv, kpe, wukv, l):
    nb, _, past, _ = ckv.shape
    row = lambda b: (b, 0)
    return pl.pallas_call(
        _kvup_kernel,
        grid=(nb,),
        in_specs=[pl.BlockSpec((None, None, past, KV_RANK_B), lambda b: (b, l, 0, 0)),
                  pl.BlockSpec((None, None, past, LANE), lambda b: (b, l, 0, 0)),
                  _layer_spec(wukv, l)],
        out_specs=[pl.BlockSpec((past, HB_COLS), row), pl.BlockSpec((past, HB_COLS), row)],
        out_shape=[jax.ShapeDtypeStruct((nb * past, HB_COLS), BF16)] * 2,
        name="mla_cache_up",
    )(ckv, kpe, wukv)


def _attend(q_tiles, k_tiles, v_tiles, bias, sink):
    s = _bdot_nt(jnp.stack(q_tiles), jnp.stack(k_tiles))
    if bias is not None:
        s = s + bias[None]
    m = jnp.max(s, axis=-1, keepdims=True)
    if sink is not None:
        m = jnp.maximum(m, sink)
    o = _bdot(jnp.exp2(s - m).astype(BF16), jnp.stack(v_tiles))
    outs = []
    for h in range(len(q_tiles)):
        den = pltpu.roll(o[h], HALF, 1)
        if sink is not None:
            den = den + jnp.exp2(sink[h] - m[h])
        outs.append(o[h] / den)
    lo = lax.broadcasted_iota(jnp.int32, outs[0].shape, 1) < HALF
    return [jnp.where(lo, outs[2 * i], pltpu.roll(outs[2 * i + 1], HALF, 1)) for i in range(len(outs) // 2)]


def _tile(x, t):
    return x[:, t * LANE:(t + 1) * LANE]


def _attn_a_heads(q, ka, va, bias, sink_ref):
    qs, ks, vs = [], [], []
    for t in range(N_HEADS_A // 2):
        g = (2 * t) // GQA_GROUP
        for e in range(2):
            qs.append(_tile(q, t))
            ks.append(_tile(ka, 2 * g + e))
            vs.append(_tile(va, g))
    sink = jnp.stack([sink_ref[:, h:h + 1] * LOG2E for h in range(N_HEADS_A)])
    return _attend(qs, ks, vs, bias, sink)


def _attn_a_ctx_kernel(q_ref, ka_ref, va_ref, z_ref, sink_ref, o_ref):
    outs = _attn_a_heads(q_ref[...], ka_ref[...], va_ref[...], None, sink_ref)
    for t, o in enumerate(outs):
        z = _tile(z_ref, t).astype(F32)
        o_ref[:, t * LANE:(t + 1) * LANE] = (o * _silu(z)).astype(BF16)


def _attn_a_ctx(qa, ka, va, z, sink, l, seq):
    t = qa.shape[0]
    row = lambda b: (b, 0)
    return pl.pallas_call(
        _attn_a_ctx_kernel,
        grid=(t // seq,),
        in_specs=[pl.BlockSpec((seq, W_A), row),
                  pl.BlockSpec((seq, KA_COLS), row),
                  pl.BlockSpec((seq, VA_COLS), row),
                  pl.BlockSpec((seq, W_A), row),
                  _layer_spec(sink, l)],
        out_specs=pl.BlockSpec((seq, W_A), row),
        out_shape=jax.ShapeDtypeStruct((t, W_A), BF16),
        compiler_params=pltpu.CompilerParams(dimension_semantics=("parallel",)),
        name="attn_a_ctx",
    )(qa, ka, va, z, sink)


def _attn_a_lat_kernel(q_ref, kp_ref, kc_ref, kn_ref, vp_ref, vc_ref, vn_ref, kx_ref, vx_ref, z_ref, sink_ref,
                       o_ref, *, nq):
    j = pl.program_id(1)
    ka = jnp.concatenate([kp_ref[...], kc_ref[...], kn_ref[...], kx_ref[...]], axis=0)
    va = jnp.concatenate([vp_ref[...], vc_ref[...], vn_ref[...], vx_ref[...]], axis=0)
    n_loc = TM + 2 * WINDOW
    qi = lax.broadcasted_iota(jnp.int32, (TM, ka.shape[0]), 0)
    kj = lax.broadcasted_iota(jnp.int32, (TM, ka.shape[0]), 1)
    ok = (kj >= qi) & (kj <= qi + 2 * WINDOW)
    ok = ok & ((kj >= WINDOW) | (j > 0)) & ((kj < TM + WINDOW) | (j < nq - 1))
    bias = jnp.where(ok | (kj >= n_loc), 0.0, NEG_INF)
    outs = _attn_a_heads(q_ref[...], ka, va, bias, sink_ref)
    for t, o in enumerate(outs):
        z = _tile(z_ref, t).astype(F32)
        o_ref[:, t * LANE:(t + 1) * LANE] = (o * _silu(z)).astype(BF16)


def _attn_a_lat(qa, ka, va, z, sink, kx, vx, l, seq):
    t = qa.shape[0]
    nq = seq // TM
    past = kx.shape[2]
    r = TM // WINDOW
    row = lambda b, j: (b * nq + j, 0)
    prev = lambda b, j: ((b * nq + j) * r - jnp.where(j > 0, 1, 0), 0)
    nxt = lambda b, j: ((b * nq + j) * r + jnp.where(j < nq - 1, r, r - 1), 0)
    return pl.pallas_call(
        functools.partial(_attn_a_lat_kernel, nq=nq),
        grid=(t // seq, nq),
        in_specs=[pl.BlockSpec((TM, W_A), row),
                  pl.BlockSpec((WINDOW, KA_COLS), prev),
                  pl.BlockSpec((TM, KA_COLS), row),
                  pl.BlockSpec((WINDOW, KA_COLS), nxt),
                  pl.BlockSpec((WINDOW, VA_COLS), prev),
                  pl.BlockSpec((TM, VA_COLS), row),
                  pl.BlockSpec((WINDOW, VA_COLS), nxt),
                  pl.BlockSpec((None, None, past, KA_COLS), lambda b, j: (b, l, 0, 0)),
                  pl.BlockSpec((None, None, past, VA_COLS), lambda b, j: (b, l, 0, 0)),
                  pl.BlockSpec((TM, W_A), row),
                  _layer_spec(sink, l)],
        out_specs=pl.BlockSpec((TM, W_A), row),
        out_shape=jax.ShapeDtypeStruct((t, W_A), BF16),
        compiler_params=pltpu.CompilerParams(dimension_semantics=("parallel", "parallel")),
        name="attn_a_lat",
    )(qa, ka, ka, ka, va, va, va, kx, vx, z, sink)


def _attn_b_kernel(*refs, has_ctx, group):
    if has_ctx:
        q_ref, k_ref, v_ref, kx_ref, vx_ref, z_ref, o_ref = refs
        kb = jnp.concatenate([k_ref[...], kx_ref[...]], axis=0)
        vb = jnp.concatenate([v_ref[...], vx_ref[...]], axis=0)
    else:
        q_ref, k_ref, v_ref, z_ref, o_ref = refs
        kb, vb = k_ref[...], v_ref[...]
    q = q_ref[...]
    for h0 in range(0, N_HEADS_B, group):
        heads = range(h0, h0 + group)
        outs = _attend([_tile(q, h) for h in heads], [_tile(kb, h) for h in heads],
                       [_tile(vb, h) for h in heads], None, None)
        for i, o in enumerate(outs):
            t = h0 // 2 + i
            z = _tile(z_ref, t).astype(F32)
            o_ref[:, t * LANE:(t + 1) * LANE] = (o * _silu(z)).astype(BF16)


def _attn_b(qb, kb, vb, z, seq, qblk, group, kx=None, vx=None):
    t = qb.shape[0]
    nq = seq // qblk
    has_ctx = kx is not None
    hw = HB_COLS
    in_specs = [pl.BlockSpec((qblk, hw), lambda b, j: (b * nq + j, 0)),
                pl.BlockSpec((seq, hw), lambda b, j: (b, 0)),
                pl.BlockSpec((seq, hw), lambda b, j: (b, 0))]
    args = [qb, kb, vb]
    if has_ctx:
        past = kx.shape[0] // (t // seq)
        in_specs += [pl.BlockSpec((past, hw), lambda b, j: (b, 0)),
                     pl.BlockSpec((past, hw), lambda b, j: (b, 0))]
        args += [kx, vx]
    in_specs.append(pl.BlockSpec((qblk, W_B), lambda b, j: (b * nq + j, 1)))
    args.append(z)
    return pl.pallas_call(
        functools.partial(_attn_b_kernel, has_ctx=has_ctx, group=group),
        grid=(t // seq, nq),
        in_specs=in_specs,
        out_specs=pl.BlockSpec((qblk, W_B), lambda b, j: (b * nq + j, 0)),
        out_shape=jax.ShapeDtypeStruct((t, W_B), BF16),
        compiler_params=pltpu.CompilerParams(dimension_semantics=("parallel", "parallel")),
        name="attn_b_lat" if has_ctx else "attn_b_ctx",
    )(*args)


def _gdn_local_kernel(cq_ref, prev_ref, next_ref, small_ref, abt_ref, cw_ref, prow_ref, pcol_ref,
                      u_ref, w_ref, qg_ref, kd_ref, attn_ref, eg_ref, qkv_scr, gb_scr, *, tiles_per_seq):
    tpos = pl.program_id(0) % tiles_per_seq
    x = cq_ref[...]
    prev_row = jnp.where(tpos > 0, prev_ref[HALO - 1:HALO, :], 0.0)
    next_row = jnp.where(tpos < tiles_per_seq - 1, next_ref[0:1, :], 0.0)
    rows = lax.broadcasted_iota(jnp.int32, (TM, 1), 0)
    xm1 = jnp.where(rows == 0, prev_row, pltpu.roll(x, 1, 0))
    xp1 = jnp.where(rows == TM - 1, next_row, pltpu.roll(x, TM - 1, 0))
    cw = cw_ref[...]
    y = _silu(xm1 * cw[0:1] + x * cw[1:2] + xp1 * cw[2:3])
    nq = N_HEADS_C * DK_C
    for h in range(N_HEADS_C):
        qh = y[:, h * DK_C:(h + 1) * DK_C]
        kh = y[:, nq + h * DK_C:nq + (h + 1) * DK_C]
        qkv_scr[:, h * DK_C:(h + 1) * DK_C] = (
            qh * lax.rsqrt(jnp.sum(qh * qh, axis=-1, keepdims=True) + EPS) * (DK_C ** -0.5))
        qkv_scr[:, nq + h * DK_C:nq + (h + 1) * DK_C] = kh * lax.rsqrt(jnp.sum(kh * kh, axis=-1, keepdims=True) + EPS)
    qkv_scr[:, 2 * nq:] = y[:, 2 * nq:]

    sm = small_ref[...]
    prow = prow_ref[...]
    gb_scr[:, 0:8] = -jnp.exp(prow[0:1]) * _softplus(sm[:, S_A:S_A + 8] + prow[1:2])
    gb_scr[:, 8:16] = _sigmoid(sm[:, S_B:S_B + 8])
    pcol = pcol_ref[...]

    ri = lax.broadcasted_iota(jnp.int32, (CHUNK, CHUNK), 0)
    ci = lax.broadcasted_iota(jnp.int32, (CHUNK, CHUNK), 1)
    tril = (ri >= ci).astype(F32)
    triu = (ri <= ci).astype(F32)
    xor = ri ^ ci
    eye = (ri == ci).astype(F32)

    lows, rhss, order = [], [], []
    for c in range(CPT):
        rs = slice(c * CHUNK, (c + 1) * CHUNK)
        gcol = gb_scr[rs, 0:8]
        bcol = gb_scr[rs, 8:16]
        abt = abt_ref[c]
        grow = -jnp.exp(pcol[:, 0:1]) * _softplus(abt[0:8] + pcol[:, 1:2])
        gc_f = _dot_exact(tril, gcol)
        gc_b = _dot_exact(triu, gcol)
        gr_f = _dot_exact(grow, triu)
        gr_b = _dot_exact(grow, tril)
        for h in range(N_HEADS_C):
            q = qkv_scr[rs, h * DK_C:(h + 1) * DK_C]
            k = qkv_scr[rs, nq + h * DK_C:nq + (h + 1) * DK_C]
            v = qkv_scr[rs, 2 * nq + h * DV_C:2 * nq + (h + 1) * DV_C]
            kb16 = k.astype(BF16)
            kk = _dot_nt(kb16, kb16)
            qk = _dot_nt(q.astype(BF16), kb16)
            for d in range(2):
                dh = d * N_HEADS_C + h
                gc = (gc_f if d == 0 else gc_b)[:, dh:dh + 1]
                gr = (gr_f if d == 0 else gr_b)[dh:dh + 1, :]
                beta = bcol[:, dh:dh + 1]
                incl = (ri >= ci) if d == 0 else (ri <= ci)
                strict = (ri > ci) if d == 0 else (ri < ci)
                decay = jnp.where(incl, jnp.exp(jnp.where(incl, gc - gr, 0.0)), 0.0)
                lows.append(jnp.where(strict, beta * kk * decay, 0.0))
                eg = jnp.exp(gc)
                rhss.append(jnp.concatenate([v * beta, k * (beta * eg)], axis=-1).astype(BF16))
                order.append((rs, dh))
                g_last = gc[CHUNK - 1:CHUNK] if d == 0 else gc[0:1]
                cs = slice(dh * DK_C, (dh + 1) * DK_C)
                qg_ref[rs, cs] = (q * eg).astype(BF16)
                kd_ref[rs, cs] = (k * jnp.exp(g_last - gc)).astype(BF16)
                attn_ref[rs, dh * CHUNK:(dh + 1) * CHUNK] = (qk * decay).astype(BF16)
                eg_ref[c, dh:dh + 1, :] = jnp.broadcast_to(jnp.exp(g_last), (1, LANE))
    low = jnp.stack(lows, axis=0)
    inv = eye[None] - jnp.where(xor[None] == 1, low, 0.0)
    b = 2
    while b < CHUNK:
        cpl = jnp.where((xor[None] >= b) & (xor[None] < 2 * b), low, 0.0)
        tmp = _bdot(cpl.astype(BF16), inv.astype(BF16))
        inv = inv - _bdot(inv.astype(BF16), tmp.astype(BF16))
        b *= 2
    uw = _bdot(inv.astype(BF16), jnp.stack(rhss, axis=0))
    for i, (rs, dh) in enumerate(order):
        cs = slice(dh * DK_C, (dh + 1) * DK_C)
        u_ref[rs, cs] = uw[i, :, :DV_C]
        w_ref[rs, cs] = uw[i, :, DV_C:].astype(BF16)


def _gdn_local(cqkv, small, abt, conv_w, prow, pcol, l, tiles_per_seq):
    t = cqkv.shape[0]
    nt = t // TM
    nh8 = t // HALO
    row = lambda i: (i, 0)
    dh = 2 * N_HEADS_C
    return pl.pallas_call(
        functools.partial(_gdn_local_kernel, tiles_per_seq=tiles_per_seq),
        grid=(nt,),
        in_specs=[pl.BlockSpec((TM, QKV_C), row),
                  pl.BlockSpec((HALO, QKV_C), lambda i: (jnp.maximum(i * (TM // HALO) - 1, 0), 0)),
                  pl.BlockSpec((HALO, QKV_C), lambda i: (jnp.minimum((i + 1) * (TM // HALO), nh8 - 1), 0)),
                  pl.BlockSpec((TM, LANE), row),
                  pl.BlockSpec((CPT, 16, CHUNK), lambda i: (i, 0, 0)),
                  _layer_spec(conv_w, l), _layer_spec(prow, l), _layer_spec(pcol, l)],
        out_specs=[pl.BlockSpec((TM, dh * DV_C), row),
                   pl.BlockSpec((TM, dh * DK_C), row),
                   pl.BlockSpec((TM, dh * DK_C), row),
                   pl.BlockSpec((TM, dh * DK_C), row),
                   pl.BlockSpec((TM, dh * CHUNK), row),
                   pl.BlockSpec((CPT, dh, LANE), lambda i: (i, 0, 0))],
        out_shape=[jax.ShapeDtypeStruct((t, dh * DV_C), F32),
                   jax.ShapeDtypeStruct((t, dh * DK_C), BF16),
                   jax.ShapeDtypeStruct((t, dh * DK_C), BF16),
                   jax.ShapeDtypeStruct((t, dh * DK_C), BF16),
                   jax.ShapeDtypeStruct((t, dh * CHUNK), BF16),
                   jax.ShapeDtypeStruct((t // CHUNK, dh, LANE), F32)],
        scratch_shapes=[pltpu.VMEM((TM, QKV_C), F32), pltpu.VMEM((TM, 16), F32)],
        compiler_params=pltpu.CompilerParams(dimension_semantics=("parallel",)),
        name="gdn_local",
    )(cqkv, cqkv, cqkv, small, abt, conv_w, prow, pcol)


def _gdn_scan_kernel(*refs, nt, has_init, want_state):
    it = iter(refs)
    ins = [[next(it) for _ in range(6)] for _ in range(2)]
    s0_ref = next(it) if has_init else None
    o_refs = [next(it), next(it)]
    st_ref = next(it) if want_state else None
    s_scr = next(it)
    j = pl.program_id(1)
    nh = N_HEADS_C

    @pl.when(j == 0)
    def _():
        if has_init:
            s_scr[...] = s0_ref[0].reshape(2 * nh, DK_C, DV_C)
        else:
            s_scr[...] = jnp.zeros_like(s_scr)

    for step in range(CPT):
        def gather(idx, width):
            parts = []
            for d in range(2):
                c = step if d == 0 else CPT - 1 - step
                for h in range(nh):
                    parts.append(ins[d][idx][c * CHUNK:(c + 1) * CHUNK, h * width:(h + 1) * width])
            return jnp.stack(parts)

        u, w, qg, kd, attn = gather(0, DV_C), gather(1, DK_C), gather(2, DK_C), gather(3, DK_C), gather(4, CHUNK)
        eg = jnp.stack([ins[d][5][step if d == 0 else CPT - 1 - step, d * nh + h:d * nh + h + 1, :]
                        for d in range(2) for h in range(nh)])
        s = s_scr[...]
        sb = s.astype(BF16)
        v_new = u - _bdot(w, sb)
        vb = v_new.astype(BF16)
        o = _bdot(qg, sb) + _bdot(attn, vb)
        s_scr[...] = s * eg + _bdot_tn(kd, vb)
        for d in range(2):
            c = step if d == 0 else CPT - 1 - step
            for h in range(nh):
                o_refs[d][c * CHUNK:(c + 1) * CHUNK, h * DV_C:(h + 1) * DV_C] = o[d * nh + h]

    if want_state:
        @pl.when(j == nt - 1)
        def _():
            st_ref[0] = s_scr[...].reshape(2, nh, DK_C, DV_C)


def _gdn_scan(u, w, qg, kd, attn, eg, s0, l, seq, want_state):
    t = u.shape[0]
    nt = seq // TM
    nb = t // seq
    half = N_HEADS_C * DK_C
    has_init = s0 is not None
    in_specs, args = [], []
    for d in range(2):
        if d == 0:
            row = lambda b, j: (b * nt + j, 0)
            row3 = lambda b, j: (b * nt + j, 0, 0)
        else:
            row = lambda b, j: (b * nt + nt - 1 - j, 1)
            row3 = lambda b, j: (b * nt + nt - 1 - j, 0, 0)
        in_specs += [pl.BlockSpec((TM, half), row)] * 4
        in_specs += [pl.BlockSpec((TM, N_HEADS_C * CHUNK), row),
                     pl.BlockSpec((CPT, 2 * N_HEADS_C, LANE), row3)]
        args += [u, w, qg, kd, attn, eg]
    st_block = (1, 2, N_HEADS_C, DK_C, DV_C)
    if has_init:
        in_specs.append(pl.BlockSpec((1, None) + st_block[1:], lambda b, j: (b, l, 0, 0, 0, 0)))
        args.append(s0)
    out_specs = [pl.BlockSpec((TM, half), lambda b, j: (b * nt + j, 0)),
                 pl.BlockSpec((TM, half), lambda b, j: (b * nt + nt - 1 - j, 0))]
    out_shape = [jax.ShapeDtypeStruct((t, half), F32), jax.ShapeDtypeStruct((t, half), F32)]
    if want_state:
        out_specs.append(pl.BlockSpec(st_block, lambda b, j: (b, 0, 0, 0, 0)))
        out_shape.append(jax.ShapeDtypeStruct((nb,) + st_block[1:], F32))
    return pl.pallas_call(
        functools.partial(_gdn_scan_kernel, nt=nt, has_init=has_init, want_state=want_state),
        grid=(nb, nt),
        in_specs=in_specs,
        out_specs=out_specs,
        out_shape=out_shape,
        scratch_shapes=[pltpu.VMEM((2 * N_HEADS_C, DK_C, DV_C), F32)],
        compiler_params=pltpu.CompilerParams(dimension_semantics=("parallel", "arbitrary")),
        name="gdn_scan",
    )(*args)


def _merge_kernel(x_ref, mod_ref, oa_ref, ob_ref, cf_ref, cb_ref, zc_ref, gates_ref, gn_ref,
                  wa_ref, wb_ref, wc_ref, wo_ref, fg_ref, o_ref, *, last):
    oc = cf_ref[...] + cb_ref[...]
    zc = zc_ref[...].astype(F32)
    gn = gn_ref[...]
    parts = []
    for h in range(N_HEADS_C):
        hs = slice(h * DV_C, (h + 1) * DV_C)
        och = oc[:, hs]
        och = och * lax.rsqrt(jnp.mean(och * och, axis=-1, keepdims=True) + EPS) * gn
        parts.append((och * _silu(zc[:, hs])).astype(BF16))
    ocz = jnp.concatenate(parts, axis=-1)
    pa = _dot(oa_ref[...], wa_ref[...])
    pb = _dot(ob_ref[...], wb_ref[...])
    pc = _dot(ocz, wc_ref[...])
    ga = _sigmoid(gates_ref[:, 0:D_MODEL].astype(F32))
    gb = _sigmoid(gates_ref[:, D_MODEL:2 * D_MODEL].astype(F32))
    gc = _sigmoid(gates_ref[:, 2 * D_MODEL:].astype(F32))
    y = _dot((ga * pa + gb * pb + gc * pc).astype(BF16), wo_ref[...])
    gate = mod_ref[0][:, 2 * D_MODEL:]
    xo = x_ref[...] + gate * y
    if last:
        xo = xo * lax.rsqrt(jnp.mean(xo * xo, axis=-1, keepdims=True) + EPS) * fg_ref[...]
    o_ref[...] = xo


def _merge(x2d, l, mod, mod_row_fn, oa, ob, cf, cb, z, gates, gn, wa, wb, wc, wo, fg, last):
    t = x2d.shape[0]
    row = lambda i: (i, 0)
    return pl.pallas_call(
        functools.partial(_merge_kernel, last=last),
        grid=(t // TM,),
        in_specs=[pl.BlockSpec((TM, D_MODEL), row),
                  pl.BlockSpec((1, 1, 3 * D_MODEL), lambda i: (mod_row_fn(i), 0, 0)),
                  pl.BlockSpec((TM, W_A), row),
                  pl.BlockSpec((TM, W_B), row),
                  pl.BlockSpec((TM, W_C), row),
                  pl.BlockSpec((TM, W_C), row),
                  pl.BlockSpec((TM, W_C), lambda i: (i, 2)),
                  pl.BlockSpec((TM, 3 * D_MODEL), row),
                  _layer_spec(gn, l), _layer_spec(wa, l), _layer_spec(wb, l), _layer_spec(wc, l),
                  _layer_spec(wo, l),
                  pl.BlockSpec((1, D_MODEL), lambda i: (0, 0))],
        out_specs=pl.BlockSpec((TM, D_MODEL), row),
        out_shape=jax.ShapeDtypeStruct((t, D_MODEL), F32),
        compiler_params=pltpu.CompilerParams(dimension_semantics=("parallel",)),
        name="merge",
    )(x2d, mod, oa, ob, cf, cb, z, gates, gn, wa, wb, wc, wo, fg)


def _rope_tables(n_tokens, rot_dim):
    rows = n_tokens // GRID_W
    row = np.repeat(np.arange(rows), GRID_W).astype(np.float32)
    col = np.tile(np.arange(GRID_W), rows).astype(np.float32)
    n_pairs = rot_dim // 4
    inv = (np.float32(ROPE_BASE) ** (-np.arange(n_pairs, dtype=np.float32) / np.float32(n_pairs))).astype(np.float32)
    ang = np.concatenate([row[:, None] * inv, col[:, None] * inv], axis=-1)
    c, s = np.cos(ang), np.sin(ang)
    return np.repeat(c, 2, axis=-1), np.stack([-s, s], axis=-1).reshape(n_tokens, rot_dim)


def _swap_halves(x):
    return jnp.concatenate([x[..., HD_A:], x[..., :HD_A]], axis=-1)


def _prep_weights(w_in, w_uq, w_ukv):
    depth = w_in.shape[0]
    o = [0]
    for n in IN_SIZES:
        o.append(o[-1] + n)
    seg = lambda i: w_in[..., o[i]:o[i + 1]]
    zeros = lambda n: jnp.zeros((depth, D_MODEL, n), w_in.dtype)
    wp = jnp.concatenate(
        [seg(0), seg(1), _swap_halves(seg(1)), seg(2), _swap_halves(seg(2)),
         seg(3), seg(7), seg(11), seg(12), seg(4), seg(5),
         zeros(S_KPE), seg(6), seg(9), seg(10), zeros(LANE - S_B - 8), seg(8)], axis=-1).astype(BF16)
    wab = jnp.swapaxes(w_in[..., o[9]:o[11]], 1, 2).astype(BF16)
    hd = QK_NOPE_B + QK_ROPE_B
    wuq = jnp.pad(w_uq.reshape(depth, Q_RANK_B, N_HEADS_B, hd), ((0, 0), (0, 0), (0, 0), (0, MLA_HW - hd)))
    wuq = wuq.reshape(depth, Q_RANK_B, HB_COLS).astype(BF16)
    kv = w_ukv.reshape(depth, KV_RANK_B, N_HEADS_B, QK_NOPE_B + V_HD_B)
    wk = jnp.pad(kv[..., :QK_NOPE_B], ((0, 0), (0, 0), (0, 0), (0, MLA_HW - QK_NOPE_B)))
    wv = jnp.pad(kv[..., QK_NOPE_B:], ((0, 0), (0, 0), (0, 0), (0, MLA_HW - V_HD_B)))
    wukv = jnp.concatenate([wk.reshape(depth, KV_RANK_B, HB_COLS), wv.reshape(depth, KV_RANK_B, HB_COLS)], axis=-1)
    return wp, wab, wuq, wukv.astype(BF16)


def _cache_tiles_a(kx, vx):
    k0, k1 = kx[..., 0, :], kx[..., 1, :]
    z = jnp.zeros_like(k0)
    ka = jnp.concatenate([k0, z, z, k0, k1, z, z, k1], axis=-1).astype(BF16)
    v0, v1 = vx[..., 0, :], vx[..., 1, :]
    one = jnp.ones_like(v0)
    va = jnp.concatenate([v0, one, v1, one], axis=-1).astype(BF16)
    return ka, va


def kernel(x_prompt, x_sample, cache_attn_k, cache_attn_v, cache_mla_ckv, cache_mla_kpe, state_gdn, c, c_ctx,
           norm_g, w_ada, b_ada, w_in, attn_sink, mla_q_norm, mla_w_uq, mla_kv_norm, mla_w_ukv, gdn_conv,
           gdn_a_log, gdn_dt_bias, gdn_norm, w_branch_a, w_branch_b, w_branch_c, w_out, final_norm_g):
    depth = w_in.shape[0]
    nb_c, seq_c, _ = x_prompt.shape
    nb_l, seq_l, _ = x_sample.shape
    past = cache_attn_k.shape[2]
    assert P_END % LANE == 0 and seq_c % TM == 0 and seq_l % TM == 0 and nb_l < 8 and TM == 2 * WINDOW

    cond8 = jnp.zeros((8, D_MODEL), F32).at[:nb_l].set(c).at[nb_l].set(c_ctx)
    mod = _modulation(cond8, w_ada, b_ada).reshape(depth * 8, 1, 3 * D_MODEL)

    c_a, s_a = _rope_tables(seq_l, HD_A)
    c_b, s_b = _rope_tables(seq_l, QK_ROPE_B)
    pad_l, pad_r = S_KPE, LANE - S_KPE - QK_ROPE_B
    one, zero = np.ones((seq_l, 1), np.float32), np.zeros((seq_l, 1), np.float32)
    rope_tabs = tuple(jnp.asarray(a) for a in (
        np.tile(c_a, (1, LANE // HD_A)), np.tile(s_a, (1, LANE // HD_A)),
        np.concatenate([np.tile(one, (1, pad_l)), c_b, np.tile(one, (1, pad_r))], 1),
        np.concatenate([np.tile(zero, (1, pad_l)), s_b, np.tile(zero, (1, pad_r))], 1)))

    wp, wab, wuq, wukv = _prep_weights(w_in, mla_w_uq, mla_w_ukv)
    ng = norm_g.reshape(depth, 1, D_MODEL)
    qn = mla_q_norm.reshape(depth, 1, Q_RANK_B)
    kvn = mla_kv_norm.reshape(depth, 1, KV_RANK_B)
    sink = attn_sink.reshape(depth, 1, N_HEADS_A)
    prow = jnp.stack([gdn_a_log.reshape(depth, -1), gdn_dt_bias.reshape(depth, -1)], axis=1)
    pcol = jnp.swapaxes(prow, 1, 2)
    gn = gdn_norm.reshape(depth, 1, DV_C)
    wa, wb, wc, wo = (w.astype(BF16) for w in (w_branch_a, w_branch_b, w_branch_c, w_out))
    fg = final_norm_g.reshape(1, D_MODEL)
    kxa, vxa = _cache_tiles_a(cache_attn_k, cache_attn_v)
    kpex = jnp.pad(cache_mla_kpe, ((0, 0), (0, 0), (0, 0), (pad_l, pad_r)))

    tps_c, tps_l = seq_c // TM, seq_l // TM
    y_p = x_prompt.reshape(nb_c * seq_c, D_MODEL)
    y_s = x_sample.reshape(nb_l * seq_l, D_MODEL)
    ks, vs, ckvs, kpes, sts = [], [], [], [], []
    for l in range(depth):
        last = l == depth - 1

        mod_row_c = lambda i, l=l: l * 8 + nb_l
        (qa, ka, va, z, gates, qb, kb, vb, small, cqkv, abt, kva, ckvn) = _inproj(
            y_p, l, mod, mod_row_c, ng, wp, wab, wuq, wukv, qn, kvn, None, tps_c, True)
        oa = _attn_a_ctx(qa, ka, va, z, sink, l, seq_c)
        ob = _attn_b(qb, kb, vb, z, seq_c, seq_c, N_HEADS_B)
        u, w, qg, kd, attn, eg = _gdn_local(cqkv, small, abt, gdn_conv, prow, pcol, l, tps_c)
        cf, cb, st = _gdn_scan(u, w, qg, kd, attn, eg, None, l, seq_c, True)
        y_p = _merge(y_p, l, mod, mod_row_c, oa, ob, cf, cb, z, gates, gn, wa, wb, wc, wo, fg, last)
        ks.append(kva[:, :N_KV_A * HD_A].reshape(nb_c, seq_c, N_KV_A, HD_A))
        vs.append(kva[:, N_KV_A * HD_A:].reshape(nb_c, seq_c, N_KV_A, HD_A))
        ckvs.append(ckvn.reshape(nb_c, seq_c, KV_RANK_B))
        kpes.append(small[:, S_KPE:S_KPE + QK_ROPE_B].reshape(nb_c, seq_c, QK_ROPE_B))
        sts.append(st)

        mod_row_l = lambda i, l=l: l * 8 + i // tps_l
        (qa, ka, va, z, gates, qb, kb, vb, small, cqkv, abt) = _inproj(
            y_s, l, mod, mod_row_l, ng, wp, wab, wuq, wukv, qn, kvn, rope_tabs, tps_l, False)
        oa = _attn_a_lat(qa, ka, va, z, sink, kxa, vxa, l, seq_l)
        kxb, vxb = _kvup(cache_mla_ckv, kpex, wukv, l)
        ob = _attn_b(qb, kb, vb, z, seq_l, TM, N_HEADS_B // 2, kxb, vxb)
        u, w, qg, kd, attn, eg = _gdn_local(cqkv, small, abt, gdn_conv, prow, pcol, l, tps_l)
        cf, cb = _gdn_scan(u, w, qg, kd, attn, eg, state_gdn, l, seq_l, False)
        y_s = _merge(y_s, l, mod, mod_row_l, oa, ob, cf, cb, z, gates, gn, wa, wb, wc, wo, fg, last)

    return (y_p.reshape(nb_c, seq_c, D_MODEL), y_s.reshape(nb_l, seq_l, D_MODEL),
            jnp.stack(ks, axis=1), jnp.stack(vs, axis=1), jnp.stack(ckvs, axis=1), jnp.stack(kpes, axis=1),
            jnp.stack(sts, axis=1))
```

```python
import functools

import numpy as np
import jax
import jax.numpy as jnp
from jax import lax
from jax.experimental import pallas as pl
from jax.experimental.pallas import tpu as pltpu

F32 = jnp.float32
BF16 = jnp.bfloat16

D_MODEL = 1024
GRID_W = 64
ROPE_BASE = 10000.0
EPS = 1e-6
NEG_INF = -1e30
N_HEADS_A = 8
N_KV_A = 2
HD_A = 64
GQA_GROUP = N_HEADS_A // N_KV_A
WINDOW = 128
N_HEADS_B = 8
QK_NOPE_B = 64
QK_ROPE_B = 32
V_HD_B = 64
Q_RANK_B = 384
KV_RANK_B = 256
MLA_SCALE = (QK_NOPE_B + QK_ROPE_B) ** -0.5
N_HEADS_C = 4
DK_C = 128
DV_C = 128
CHUNK = 64
W_A = N_HEADS_A * HD_A
W_B = N_HEADS_B * V_HD_B
W_C = N_HEADS_C * DV_C
QKV_C = 2 * N_HEADS_C * DK_C + W_C
IN_SIZES = (W_A, N_KV_A * HD_A, N_KV_A * HD_A, W_A, Q_RANK_B, KV_RANK_B, QK_ROPE_B, W_B, QKV_C,
            2 * N_HEADS_C, 2 * N_HEADS_C, W_C, 3 * D_MODEL)

LANE = 128
HALF = LANE // 2
TM = 256
TMD = 512
CPT = TM // CHUNK
HALO = 16
MLA_HW = 128
KA_COLS = 4 * LANE
VA_COLS = 2 * LANE
HB_COLS = N_HEADS_B * MLA_HW
KVB_COLS = 2 * HB_COLS
LOG2E = 1.4426950408889634

P_QKV = 0
A_COLS = W_A + 4 * LANE
P_Z = P_QKV + A_COLS
P_GATES = P_Z + 1536
P_CQ = P_GATES + 3 * D_MODEL
P_CKV = P_CQ + Q_RANK_B
P_SMALL = P_CKV + KV_RANK_B
P_CQKV = P_SMALL + LANE
P_END = P_CQKV + QKV_C
S_KPE = 64
S_A = 96
S_B = 104


def _sigmoid(x):
    return 0.5 * jnp.tanh(0.5 * x) + 0.5


def _silu(x):
    return x * _sigmoid(x)


def _softplus(x):
    return jnp.maximum(x, 0.0) + jnp.log(1.0 + jnp.exp(-jnp.abs(x)))


def _dot(a, b):
    return jnp.dot(a, b, preferred_element_type=F32)


def _dot_nt(a, b):
    return lax.dot_general(a, b, (((1,), (1,)), ((), ())), preferred_element_type=F32)


def _bdot(a, b):
    return lax.dot_general(a, b, (((2,), (1,)), ((0,), (0,))), preferred_element_type=F32)


def _bdot_nt(a, b):
    return lax.dot_general(a, b, (((2,), (2,)), ((0,), (0,))), preferred_element_type=F32)


def _bdot_tn(a, b):
    return lax.dot_general(a, b, (((1,), (1,)), ((0,), (0,))), preferred_element_type=F32)


def _dot_exact(a, b):
    return jnp.dot(a, b, preferred_element_type=F32, precision=lax.Precision.HIGHEST)


def _rope(x, c, s):
    n = x.shape[-1]
    lane = lax.broadcasted_iota(jnp.int32, x.shape, 1)
    swapped = jnp.where(lane % 2 == 0, pltpu.roll(x, n - 1, 1), pltpu.roll(x, 1, 1))
    return x * c + swapped * s


def _mod_kernel(cond_ref, w_ref, b_ref, out_ref):
    cnd = cond_ref[...]
    out_ref[0] = _dot(_silu(cnd).astype(BF16), w_ref[0].astype(BF16)) + b_ref[0]


def _modulation(cond8, w_ada, b_ada):
    depth = w_ada.shape[0]
    tn = 768
    return pl.pallas_call(
        _mod_kernel,
        grid=(depth, 3 * D_MODEL // tn),
        in_specs=[pl.BlockSpec((8, D_MODEL), lambda l, n: (0, 0)),
                  pl.BlockSpec((1, D_MODEL, tn), lambda l, n: (l, 0, n)),
                  pl.BlockSpec((1, 1, tn), lambda l, n: (l, 0, n))],
        out_specs=pl.BlockSpec((1, 8, tn), lambda l, n: (l, 0, n)),
        out_shape=jax.ShapeDtypeStruct((depth, 8, 3 * D_MODEL), F32),
        name="adaln_mod",
    )(cond8, w_ada, b_ada.reshape(depth, 1, 3 * D_MODEL))


def _inproj_kernel(*refs, rope, ctx):
    it = iter(refs)
    x_ref, mod_ref, ng_ref, wp_ref, wab_ref, wuq_ref, wukv_ref, qn_ref, kvn_ref = (next(it) for _ in range(9))
    if rope:
        ca_ref, sa_ref, cb_ref, sb_ref = (next(it) for _ in range(4))
    (qa_ref, ka_ref, va_ref, z_ref, gates_ref, qb_ref, kb_ref, vb_ref, small_ref, cqkv_ref,
     abt_ref) = (next(it) for _ in range(11))
    if ctx:
        kva_ref, ckvn_ref = (next(it) for _ in range(2))

    x = x_ref[...]
    mod = mod_ref[0]
    shift, scale = mod[:, :D_MODEL], mod[:, D_MODEL:2 * D_MODEL]
    xn = x * lax.rsqrt(jnp.mean(x * x, axis=-1, keepdims=True) + EPS) * ng_ref[...]
    hb = (xn * (1.0 + scale) + shift).astype(BF16)
    lane = lax.broadcasted_iota(jnp.int32, (TMD, LANE), 1)
    lo = lane < HALF

    def mm(lo_col, hi_col):
        return _dot(hb, wp_ref[:, lo_col:hi_col])

    r = mm(P_QKV, P_QKV + A_COLS)
    tiles = [r[:, t * LANE:(t + 1) * LANE] for t in range(A_COLS // LANE)]
    if ctx:
        kva_ref[:, :LANE] = tiles[4]
        kva_ref[:, LANE:] = tiles[6]
    if rope:
        ca, sa = ca_ref[...], sa_ref[...]
        tiles[:6] = [_rope(t, ca, sa) for t in tiles[:6]]
    for t in range(4):
        qa_ref[:, t * LANE:(t + 1) * LANE] = (tiles[t] * (HD_A ** -0.5 * LOG2E)).astype(BF16)
    k01, k10 = tiles[4], tiles[5]
    ka_ref[:, 0 * LANE:1 * LANE] = jnp.where(lo, k01, 0.0).astype(BF16)
    ka_ref[:, 1 * LANE:2 * LANE] = jnp.where(lo, 0.0, k10).astype(BF16)
    ka_ref[:, 2 * LANE:3 * LANE] = jnp.where(lo, k10, 0.0).astype(BF16)
    ka_ref[:, 3 * LANE:4 * LANE] = jnp.where(lo, 0.0, k01).astype(BF16)
    va_ref[:, :LANE] = jnp.where(lo, tiles[6], 1.0).astype(BF16)
    va_ref[:, LANE:] = jnp.where(lo, tiles[7], 1.0).astype(BF16)

    for t in range(3):
        z_ref[:, t * 512:(t + 1) * 512] = mm(P_Z + t * 512, P_Z + (t + 1) * 512).astype(BF16)
    for t in range(6):
        gates_ref[:, t * 512:(t + 1) * 512] = mm(P_GATES + t * 512, P_GATES + (t + 1) * 512).astype(BF16)

    r = mm(P_CQ, P_CQ + Q_RANK_B)
    qn = r * lax.rsqrt(jnp.mean(r * r, axis=-1, keepdims=True) + EPS) * qn_ref[...]
    q = _dot(qn.astype(BF16), wuq_ref[...])
    if rope:
        cb, sb = cb_ref[...], sb_ref[...]
        for h in range(N_HEADS_B):
            seg = _rope(q[:, h * MLA_HW:(h + 1) * MLA_HW], cb, sb) * (MLA_SCALE * LOG2E)
            qb_ref[:, h * MLA_HW:(h + 1) * MLA_HW] = seg.astype(BF16)
    else:
        qb_ref[...] = (q * (MLA_SCALE * LOG2E)).astype(BF16)

    r = mm(P_SMALL, P_SMALL + LANE)
    small_ref[...] = r
    kp = _rope(r, cb, sb) if rope else r
    kp = jnp.where((lane >= S_KPE) & (lane < S_KPE + QK_ROPE_B), kp, 0.0)

    r = mm(P_CKV, P_CKV + KV_RANK_B)
    cn = r * lax.rsqrt(jnp.mean(r * r, axis=-1, keepdims=True) + EPS) * kvn_ref[...]
    if ctx:
        ckvn_ref[...] = cn
    kv = _dot(cn.astype(BF16), wukv_ref[...])
    for h in range(N_HEADS_B):
        kb_ref[:, h * MLA_HW:(h + 1) * MLA_HW] = (kv[:, h * MLA_HW:(h + 1) * MLA_HW] + kp).astype(BF16)
        vh = kv[:, HB_COLS + h * MLA_HW:HB_COLS + (h + 1) * MLA_HW]
        vb_ref[:, h * MLA_HW:(h + 1) * MLA_HW] = jnp.where(lo, vh, 1.0).astype(BF16)

    for t in range(3):
        cqkv_ref[:, t * 512:(t + 1) * 512] = mm(P_CQKV + t * 512, P_CQKV + (t + 1) * 512).astype(BF16)

    for c in range(TMD // CHUNK):
        abt_ref[c] = _dot_nt(wab_ref[...], hb[c * CHUNK:(c + 1) * CHUNK])


def _layer_spec(arr, l):
    nd = arr.ndim - 1
    return pl.BlockSpec((None,) + arr.shape[1:], lambda *_: (l,) + (0,) * nd, pipeline_mode=pl.Buffered(1))


def _inproj(x2d, l, mod, mod_row_fn, ng, wp, wab, wuq, wukv, qn, kvn, rope_tabs, tiles_per_seq, ctx):
    t = x2d.shape[0]
    nt = t // TMD
    rope = rope_tabs is not None
    row = lambda i: (i, 0)
    in_specs = [pl.BlockSpec((TMD, D_MODEL), row),
                pl.BlockSpec((1, 1, 3 * D_MODEL), lambda i: (mod_row_fn(i), 0, 0))]
    in_specs += [_layer_spec(a, l) for a in (ng, wp, wab, wuq, wukv, qn, kvn)]
    args = [x2d, mod, ng, wp, wab, wuq, wukv, qn, kvn]
    if rope:
        pos = lambda i: (i % tiles_per_seq, 0)
        in_specs += [pl.BlockSpec((TMD, LANE), pos)] * 4
        args += list(rope_tabs)
    widths = [(W_A, BF16), (KA_COLS, BF16), (VA_COLS, BF16), (1536, BF16), (3 * D_MODEL, BF16),
              (HB_COLS, BF16), (HB_COLS, BF16), (HB_COLS, BF16), (LANE, F32), (QKV_C, BF16)]
    out_shape = [jax.ShapeDtypeStruct((t, w), dt) for w, dt in widths]
    out_specs = [pl.BlockSpec((TMD, w), row) for w, _ in widths]
    out_shape.append(jax.ShapeDtypeStruct((t // CHUNK, 16, CHUNK), F32))
    out_specs.append(pl.BlockSpec((TMD // CHUNK, 16, CHUNK), lambda i: (i, 0, 0)))
    if ctx:
        out_shape += [jax.ShapeDtypeStruct((t, 2 * N_KV_A * HD_A), F32), jax.ShapeDtypeStruct((t, KV_RANK_B), F32)]
        out_specs += [pl.BlockSpec((TMD, 2 * N_KV_A * HD_A), row), pl.BlockSpec((TMD, KV_RANK_B), row)]
    return pl.pallas_call(
        functools.partial(_inproj_kernel, rope=rope, ctx=ctx),
        grid=(nt,),
        in_specs=in_specs,
        out_specs=out_specs,
        out_shape=out_shape,
        compiler_params=pltpu.CompilerParams(dimension_semantics=("parallel",)),
        name="inproj_ctx" if ctx else "inproj_lat",
    )(*args)


def _kvup_kernel(c_ref, kpe_ref, w_ref, k_ref, v_ref):
    kv = _dot(c_ref[...].astype(BF16), w_ref[...])
    kp = kpe_ref[...]
    lo = lax.broadcasted_iota(jnp.int32, kp.shape, 1) < HALF
    for h in range(N_HEADS_B):
        k_ref[:, h * MLA_HW:(h + 1) * MLA_HW] = (kv[:, h * MLA_HW:(h + 1) * MLA_HW] + kp).astype(BF16)
        vh = kv[:, HB_COLS + h * MLA_HW:HB_COLS + (h + 1) * MLA_HW]
        v_ref[:, h * MLA_HW:(h + 1) * MLA_HW] = jnp.where(lo, vh, 1.0).astype(BF16)


def _kvup(ckv, kpe, wukv, l):
    nb, _, past, _ = ckv.shape
    row = lambda b: (b, 0)
    return pl.pallas_call(
        _kvup_kernel,
        grid=(nb,),
        in_specs=[pl.BlockSpec((None, None, past, KV_RANK_B), lambda b: (b, l, 0, 0)),
                  pl.BlockSpec((None, None, past, LANE), lambda b: (b, l, 0, 0)),
                  _layer_spec(wukv, l)],
        out_specs=[pl.BlockSpec((past, HB_COLS), row), pl.BlockSpec((past, HB_COLS), row)],
        out_shape=[jax.ShapeDtypeStruct((nb * past, HB_COLS), BF16)] * 2,
        name="mla_cache_up",
    )(ckv, kpe, wukv)


def _attend(q_tiles, k_tiles, v_tiles, bias, sink):
    s = _bdot_nt(jnp.stack(q_tiles), jnp.stack(k_tiles))
    if bias is not None:
        s = s + bias[None]
    m = jnp.max(s, axis=-1, keepdims=True)
    if sink is not None:
        m = jnp.maximum(m, sink)
    o = _bdot(jnp.exp2(s - m).astype(BF16), jnp.stack(v_tiles))
    outs = []
    for h in range(len(q_tiles)):
        den = pltpu.roll(o[h], HALF, 1)
        if sink is not None:
            den = den + jnp.exp2(sink[h] - m[h])
        outs.append(o[h] / den)
    lo = lax.broadcasted_iota(jnp.int32, outs[0].shape, 1) < HALF
    return [jnp.where(lo, outs[2 * i], pltpu.roll(outs[2 * i + 1], HALF, 1)) for i in range(len(outs) // 2)]


def _tile(x, t):
    return x[:, t * LANE:(t + 1) * LANE]


def _attn_a_heads(q, ka, va, bias, sink_ref):
    qs, ks, vs = [], [], []
    for t in range(N_HEADS_A // 2):
        g = (2 * t) // GQA_GROUP
        for e in range(2):
            qs.append(_tile(q, t))
            ks.append(_tile(ka, 2 * g + e))
            vs.append(_tile(va, g))
    sink = jnp.stack([sink_ref[:, h:h + 1] * LOG2E for h in range(N_HEADS_A)])
    return _attend(qs, ks, vs, bias, sink)


def _attn_a_ctx_kernel(q_ref, ka_ref, va_ref, z_ref, sink_ref, o_ref):
    outs = _attn_a_heads(q_ref[...], ka_ref[...], va_ref[...], None, sink_ref)
    for t, o in enumerate(outs):
        z = _tile(z_ref, t).astype(F32)
        o_ref[:, t * LANE:(t + 1) * LANE] = (o * _silu(z)).astype(BF16)


def _attn_a_ctx(qa, ka, va, z, sink, l, seq):
    t = qa.shape[0]
    row = lambda b: (b, 0)
    return pl.pallas_call(
        _attn_a_ctx_kernel,
        grid=(t // seq,),
        in_specs=[pl.BlockSpec((seq, W_A), row),
                  pl.BlockSpec((seq, KA_COLS), row),
                  pl.BlockSpec((seq, VA_COLS), row),
                  pl.BlockSpec((seq, W_A), row),
                  _layer_spec(sink, l)],
        out_specs=pl.BlockSpec((seq, W_A), row),
        out_shape=jax.ShapeDtypeStruct((t, W_A), BF16),
        compiler_params=pltpu.CompilerParams(dimension_semantics=("parallel",)),
        name="attn_a_ctx",
    )(qa, ka, va, z, sink)


def _attn_a_lat_kernel(q_ref, kp_ref, kc_ref, kn_ref, vp_ref, vc_ref, vn_ref, kx_ref, vx_ref, z_ref, sink_ref,
                       o_ref, *, nq):
    j = pl.program_id(1)
    ka = jnp.concatenate([kp_ref[...], kc_ref[...], kn_ref[...], kx_ref[...]], axis=0)
    va = jnp.concatenate([vp_ref[...], vc_ref[...], vn_ref[...], vx_ref[...]], axis=0)
    n_loc = TM + 2 * WINDOW
    qi = lax.broadcasted_iota(jnp.int32, (TM, ka.shape[0]), 0)
    kj = lax.broadcasted_iota(jnp.int32, (TM, ka.shape[0]), 1)
    ok = (kj >= qi) & (kj <= qi + 2 * WINDOW)
    ok = ok & ((kj >= WINDOW) | (j > 0)) & ((kj < TM + WINDOW) | (j < nq - 1))
    bias = jnp.where(ok | (kj >= n_loc), 0.0, NEG_INF)
    outs = _attn_a_heads(q_ref[...], ka, va, bias, sink_ref)
    for t, o in enumerate(outs):
        z = _tile(z_ref, t).astype(F32)
        o_ref[:, t * LANE:(t + 1) * LANE] = (o * _silu(z)).astype(BF16)


def _attn_a_lat(qa, ka, va, z, sink, kx, vx, l, seq):
    t = qa.shape[0]
    nq = seq // TM
    past = kx.shape[2]
    r = TM // WINDOW
    row = lambda b, j: (b * nq + j, 0)
    prev = lambda b, j: ((b * nq + j) * r - jnp.where(j > 0, 1, 0), 0)
    nxt = lambda b, j: ((b * nq + j) * r + jnp.where(j < nq - 1, r, r - 1), 0)
    return pl.pallas_call(
        functools.partial(_attn_a_lat_kernel, nq=nq),
        grid=(t // seq, nq),
        in_specs=[pl.BlockSpec((TM, W_A), row),
                  pl.BlockSpec((WINDOW, KA_COLS), prev),
                  pl.BlockSpec((TM, KA_COLS), row),
                  pl.BlockSpec((WINDOW, KA_COLS), nxt),
                  pl.BlockSpec((WINDOW, VA_COLS), prev),
                  pl.BlockSpec((TM, VA_COLS), row),
                  pl.BlockSpec((WINDOW, VA_COLS), nxt),
                  pl.BlockSpec((None, None, past, KA_COLS), lambda b, j: (b, l, 0, 0)),
                  pl.BlockSpec((None, None, past, VA_COLS), lambda b, j: (b, l, 0, 0)),
                  pl.BlockSpec((TM, W_A), row),
                  _layer_spec(sink, l)],
        out_specs=pl.BlockSpec((TM, W_A), row),
        out_shape=jax.ShapeDtypeStruct((t, W_A), BF16),
        compiler_params=pltpu.CompilerParams(dimension_semantics=("parallel", "parallel")),
        name="attn_a_lat",
    )(qa, ka, ka, ka, va, va, va, kx, vx, z, sink)


def _attn_b_kernel(*refs, has_ctx, group):
    if has_ctx:
        q_ref, k_ref, v_ref, kx_ref, vx_ref, z_ref, o_ref = refs
        kb = jnp.concatenate([k_ref[...], kx_ref[...]], axis=0)
        vb = jnp.concatenate([v_ref[...], vx_ref[...]], axis=0)
    else:
        q_ref, k_ref, v_ref, z_ref, o_ref = refs
        kb, vb = k_ref[...], v_ref[...]
    q = q_ref[...]
    for h0 in range(0, N_HEADS_B, group):
        heads = range(h0, h0 + group)
        outs = _attend([_tile(q, h) for h in heads], [_tile(kb, h) for h in heads],
                       [_tile(vb, h) for h in heads], None, None)
        for i, o in enumerate(outs):
            t = h0 // 2 + i
            z = _tile(z_ref, t).astype(F32)
            o_ref[:, t * LANE:(t + 1) * LANE] = (o * _silu(z)).astype(BF16)


def _attn_b(qb, kb, vb, z, seq, qblk, group, kx=None, vx=None):
    t = qb.shape[0]
    nq = seq // qblk
    has_ctx = kx is not None
    hw = HB_COLS
    in_specs = [pl.BlockSpec((qblk, hw), lambda b, j: (b * nq + j, 0)),
                pl.BlockSpec((seq, hw), lambda b, j: (b, 0)),
                pl.BlockSpec((seq, hw), lambda b, j: (b, 0))]
    args = [qb, kb, vb]
    if has_ctx:
        past = kx.shape[0] // (t // seq)
        in_specs += [pl.BlockSpec((past, hw), lambda b, j: (b, 0)),
                     pl.BlockSpec((past, hw), lambda b, j: (b, 0))]
        args += [kx, vx]
    in_specs.append(pl.BlockSpec((qblk, W_B), lambda b, j: (b * nq + j, 1)))
    args.append(z)
    return pl.pallas_call(
        functools.partial(_attn_b_kernel, has_ctx=has_ctx, group=group),
        grid=(t // seq, nq),
        in_specs=in_specs,
        out_specs=pl.BlockSpec((qblk, W_B), lambda b, j: (b * nq + j, 0)),
        out_shape=jax.ShapeDtypeStruct((t, W_B), BF16),
        compiler_params=pltpu.CompilerParams(dimension_semantics=("parallel", "parallel")),
        name="attn_b_lat" if has_ctx else "attn_b_ctx",
    )(*args)


def _gdn_local_kernel(cq_ref, prev_ref, next_ref, small_ref, abt_ref, cw_ref, prow_ref, pcol_ref,
                      u_ref, w_ref, qg_ref, kd_ref, attn_ref, eg_ref, qkv_scr, gb_scr, *, tiles_per_seq):
    tpos = pl.program_id(0) % tiles_per_seq
    x = cq_ref[...].astype(F32)
    prev_row = jnp.where(tpos > 0, prev_ref[...].astype(F32)[HALO - 1:HALO, :], 0.0)
    next_row = jnp.where(tpos < tiles_per_seq - 1, next_ref[...].astype(F32)[0:1, :], 0.0)
    rows = lax.broadcasted_iota(jnp.int32, (TM, 1), 0)
    xm1 = jnp.where(rows == 0, prev_row, pltpu.roll(x, 1, 0))
    xp1 = jnp.where(rows == TM - 1, next_row, pltpu.roll(x, TM - 1, 0))
    cw = cw_ref[...]
    y = _silu(xm1 * cw[0:1] + x * cw[1:2] + xp1 * cw[2:3])
    nq = N_HEADS_C * DK_C
    for h in range(N_HEADS_C):
        qh = y[:, h * DK_C:(h + 1) * DK_C]
        kh = y[:, nq + h * DK_C:nq + (h + 1) * DK_C]
        qkv_scr[:, h * DK_C:(h + 1) * DK_C] = (
            qh * lax.rsqrt(jnp.sum(qh * qh, axis=-1, keepdims=True) + EPS) * (DK_C ** -0.5))
        qkv_scr[:, nq + h * DK_C:nq + (h + 1) * DK_C] = kh * lax.rsqrt(jnp.sum(kh * kh, axis=-1, keepdims=True) + EPS)
    qkv_scr[:, 2 * nq:] = y[:, 2 * nq:]

    sm = small_ref[...]
    prow = prow_ref[...]
    gb_scr[:, 0:8] = -jnp.exp(prow[0:1]) * _softplus(sm[:, S_A:S_A + 8] + prow[1:2])
    gb_scr[:, 8:16] = _sigmoid(sm[:, S_B:S_B + 8])
    pcol = pcol_ref[...]

    ri = lax.broadcasted_iota(jnp.int32, (CHUNK, CHUNK), 0)
    ci = lax.broadcasted_iota(jnp.int32, (CHUNK, CHUNK), 1)
    tril = (ri >= ci).astype(F32)
    triu = (ri <= ci).astype(F32)
    xor = ri ^ ci
    eye = (ri == ci).astype(F32)

    lows, rhss, order = [], [], []
    for c in range(CPT):
        rs = slice(c * CHUNK, (c + 1) * CHUNK)
        gcol = gb_scr[rs, 0:8]
        bcol = gb_scr[rs, 8:16]
        abt = abt_ref[c]
        grow = -jnp.exp(pcol[:, 0:1]) * _softplus(abt[0:8] + pcol[:, 1:2])
        gc_f = _dot_exact(tril, gcol)
        gc_b = _dot_exact(triu, gcol)
        gr_f = _dot_exact(grow, triu)
        gr_b = _dot_exact(grow, tril)
        for h in range(N_HEADS_C):
            q = qkv_scr[rs, h * DK_C:(h + 1) * DK_C]
            k = qkv_scr[rs, nq + h * DK_C:nq + (h + 1) * DK_C]
            v = qkv_scr[rs, 2 * nq + h * DV_C:2 * nq + (h + 1) * DV_C]
            kb16 = k.astype(BF16)
            kk = _dot_nt(kb16, kb16)
            qk = _dot_nt(q.astype(BF16), kb16)
            for d in range(2):
                dh = d * N_HEADS_C + h
                gc = (gc_f if d == 0 else gc_b)[:, dh:dh + 1]
                gr = (gr_f if d == 0 else gr_b)[dh:dh + 1, :]
                beta = bcol[:, dh:dh + 1]
                incl = (ri >= ci) if d == 0 else (ri <= ci)
                strict = (ri > ci) if d == 0 else (ri < ci)
                decay = jnp.where(incl, jnp.exp(jnp.where(incl, gc - gr, 0.0)), 0.0)
                lows.append(jnp.where(strict, beta * kk * decay, 0.0))
                eg = jnp.exp(gc)
                rhss.append(jnp.concatenate([v * beta, k * (beta * eg)], axis=-1).astype(BF16))
                order.append((rs, dh))
                g_last = gc[CHUNK - 1:CHUNK] if d == 0 else gc[0:1]
                cs = slice(dh * DK_C, (dh + 1) * DK_C)
                qg_ref[rs, cs] = (q * eg).astype(BF16)
                kd_ref[rs, cs] = (k * jnp.exp(g_last - gc)).astype(BF16)
                attn_ref[rs, dh * CHUNK:(dh + 1) * CHUNK] = (qk * decay).astype(BF16)
                eg_ref[c, dh:dh + 1, :] = jnp.broadcast_to(jnp.exp(g_last), (1, LANE))
    low = jnp.stack(lows, axis=0)
    inv = eye[None] - jnp.where(xor[None] == 1, low, 0.0)
    b = 2
    while b < CHUNK:
        cpl = jnp.where((xor[None] >= b) & (xor[None] < 2 * b), low, 0.0)
        tmp = _bdot(cpl.astype(BF16), inv.astype(BF16))
        inv = inv - _bdot(inv.astype(BF16), tmp.astype(BF16))
        b *= 2
    uw = _bdot(inv.astype(BF16), jnp.stack(rhss, axis=0))
    for i, (rs, dh) in enumerate(order):
        cs = slice(dh * DK_C, (dh + 1) * DK_C)
        u_ref[rs, cs] = uw[i, :, :DV_C]
        w_ref[rs, cs] = uw[i, :, DV_C:].astype(BF16)


def _gdn_local(cqkv, small, abt, conv_w, prow, pcol, l, tiles_per_seq):
    t = cqkv.shape[0]
    nt = t // TM
    nh8 = t // HALO
    row = lambda i: (i, 0)
    dh = 2 * N_HEADS_C
    return pl.pallas_call(
        functools.partial(_gdn_local_kernel, tiles_per_seq=tiles_per_seq),
        grid=(nt,),
        in_specs=[pl.BlockSpec((TM, QKV_C), row),
                  pl.BlockSpec((HALO, QKV_C), lambda i: (jnp.maximum(i * (TM // HALO) - 1, 0), 0)),
                  pl.BlockSpec((HALO, QKV_C), lambda i: (jnp.minimum((i + 1) * (TM // HALO), nh8 - 1), 0)),
                  pl.BlockSpec((TM, LANE), row),
                  pl.BlockSpec((CPT, 16, CHUNK), lambda i: (i, 0, 0)),
                  _layer_spec(conv_w, l), _layer_spec(prow, l), _layer_spec(pcol, l)],
        out_specs=[pl.BlockSpec((TM, dh * DV_C), row),
                   pl.BlockSpec((TM, dh * DK_C), row),
                   pl.BlockSpec((TM, dh * DK_C), row),
                   pl.BlockSpec((TM, dh * DK_C), row),
                   pl.BlockSpec((TM, dh * CHUNK), row),
                   pl.BlockSpec((CPT, dh, LANE), lambda i: (i, 0, 0))],
        out_shape=[jax.ShapeDtypeStruct((t, dh * DV_C), F32),
                   jax.ShapeDtypeStruct((t, dh * DK_C), BF16),
                   jax.ShapeDtypeStruct((t, dh * DK_C), BF16),
                   jax.ShapeDtypeStruct((t, dh * DK_C), BF16),
                   jax.ShapeDtypeStruct((t, dh * CHUNK), BF16),
                   jax.ShapeDtypeStruct((t // CHUNK, dh, LANE), F32)],
        scratch_shapes=[pltpu.VMEM((TM, QKV_C), F32), pltpu.VMEM((TM, 16), F32)],
        compiler_params=pltpu.CompilerParams(dimension_semantics=("parallel",)),
        name="gdn_local",
    )(cqkv, cqkv, cqkv, small, abt, conv_w, prow, pcol)


def _gdn_scan_kernel(*refs, nt, has_init, want_state):
    it = iter(refs)
    ins = [[next(it) for _ in range(6)] for _ in range(2)]
    s0_ref = next(it) if has_init else None
    o_refs = [next(it), next(it)]
    st_ref = next(it) if want_state else None
    s_scr = next(it)
    j = pl.program_id(1)
    nh = N_HEADS_C

    @pl.when(j == 0)
    def _():
        if has_init:
            s_scr[...] = s0_ref[0].reshape(2 * nh, DK_C, DV_C)
        else:
            s_scr[...] = jnp.zeros_like(s_scr)

    for step in range(CPT):
        def gather(idx, width):
            parts = []
            for d in range(2):
                c = step if d == 0 else CPT - 1 - step
                for h in range(nh):
                    parts.append(ins[d][idx][c * CHUNK:(c + 1) * CHUNK, h * width:(h + 1) * width])
            return jnp.stack(parts)

        u, w, qg, kd, attn = gather(0, DV_C), gather(1, DK_C), gather(2, DK_C), gather(3, DK_C), gather(4, CHUNK)
        eg = jnp.stack([ins[d][5][step if d == 0 else CPT - 1 - step, d * nh + h:d * nh + h + 1, :]
                        for d in range(2) for h in range(nh)])
        s = s_scr[...]
        sb = s.astype(BF16)
        v_new = u - _bdot(w, sb)
        vb = v_new.astype(BF16)
        o = _bdot(qg, sb) + _bdot(attn, vb)
        s_scr[...] = s * eg + _bdot_tn(kd, vb)
        for d in range(2):
            c = step if d == 0 else CPT - 1 - step
            for h in range(nh):
                o_refs[d][c * CHUNK:(c + 1) * CHUNK, h * DV_C:(h + 1) * DV_C] = o[d * nh + h]

    if want_state:
        @pl.when(j == nt - 1)
        def _():
            st_ref[0] = s_scr[...].reshape(2, nh, DK_C, DV_C)


def _gdn_scan(u, w, qg, kd, attn, eg, s0, l, seq, want_state):
    t = u.shape[0]
    nt = seq // TM
    nb = t // seq
    half = N_HEADS_C * DK_C
    has_init = s0 is not None
    in_specs, args = [], []
    for d in range(2):
        if d == 0:
            row = lambda b, j: (b * nt + j, 0)
            row3 = lambda b, j: (b * nt + j, 0, 0)
        else:
            row = lambda b, j: (b * nt + nt - 1 - j, 1)
            row3 = lambda b, j: (b * nt + nt - 1 - j, 0, 0)
        in_specs += [pl.BlockSpec((TM, half), row)] * 4
        in_specs += [pl.BlockSpec((TM, N_HEADS_C * CHUNK), row),
                     pl.BlockSpec((CPT, 2 * N_HEADS_C, LANE), row3)]
        args += [u, w, qg, kd, attn, eg]
    st_block = (1, 2, N_HEADS_C, DK_C, DV_C)
    if has_init:
        in_specs.append(pl.BlockSpec((1, None) + st_block[1:], lambda b, j: (b, l, 0, 0, 0, 0)))
        args.append(s0)
    out_specs = [pl.BlockSpec((TM, half), lambda b, j: (b * nt + j, 0)),
                 pl.BlockSpec((TM, half), lambda b, j: (b * nt + nt - 1 - j, 0))]
    out_shape = [jax.ShapeDtypeStruct((t, half), F32), jax.ShapeDtypeStruct((t, half), F32)]
    if want_state:
        out_specs.append(pl.BlockSpec(st_block, lambda b, j: (b, 0, 0, 0, 0)))
        out_shape.append(jax.ShapeDtypeStruct((nb,) + st_block[1:], F32))
    return pl.pallas_call(
        functools.partial(_gdn_scan_kernel, nt=nt, has_init=has_init, want_state=want_state),
        grid=(nb, nt),
        in_specs=in_specs,
        out_specs=out_specs,
        out_shape=out_shape,
        scratch_shapes=[pltpu.VMEM((2 * N_HEADS_C, DK_C, DV_C), F32)],
        compiler_params=pltpu.CompilerParams(dimension_semantics=("parallel", "arbitrary")),
        name="gdn_scan",
    )(*args)


def _merge_kernel(x_ref, mod_ref, oa_ref, ob_ref, cf_ref, cb_ref, zc_ref, gates_ref, gn_ref,
                  wa_ref, wb_ref, wc_ref, wo_ref, fg_ref, o_ref, *, last):
    oc = cf_ref[...] + cb_ref[...]
    zc = zc_ref[...].astype(F32)
    gn = gn_ref[...]
    parts = []
    for h in range(N_HEADS_C):
        hs = slice(h * DV_C, (h + 1) * DV_C)
        och = oc[:, hs]
        och = och * lax.rsqrt(jnp.mean(och * och, axis=-1, keepdims=True) + EPS) * gn
        parts.append((och * _silu(zc[:, hs])).astype(BF16))
    ocz = jnp.concatenate(parts, axis=-1)
    pa = _dot(oa_ref[...], wa_ref[...])
    pb = _dot(ob_ref[...], wb_ref[...])
    pc = _dot(ocz, wc_ref[...])
    ga = _sigmoid(gates_ref[:, 0:D_MODEL].astype(F32))
    gb = _sigmoid(gates_ref[:, D_MODEL:2 * D_MODEL].astype(F32))
    gc = _sigmoid(gates_ref[:, 2 * D_MODEL:].astype(F32))
    y = _dot((ga * pa + gb * pb + gc * pc).astype(BF16), wo_ref[...])
    gate = mod_ref[0][:, 2 * D_MODEL:]
    xo = x_ref[...] + gate * y
    if last:
        xo = xo * lax.rsqrt(jnp.mean(xo * xo, axis=-1, keepdims=True) + EPS) * fg_ref[...]
    o_ref[...] = xo


def _merge(x2d, l, mod, mod_row_fn, oa, ob, cf, cb, z, gates, gn, wa, wb, wc, wo, fg, last):
    t = x2d.shape[0]
    row = lambda i: (i, 0)
    return pl.pallas_call(
        functools.partial(_merge_kernel, last=last),
        grid=(t // TMD,),
        in_specs=[pl.BlockSpec((TMD, D_MODEL), row),
                  pl.BlockSpec((1, 1, 3 * D_MODEL), lambda i: (mod_row_fn(i), 0, 0)),
                  pl.BlockSpec((TMD, W_A), row),
                  pl.BlockSpec((TMD, W_B), row),
                  pl.BlockSpec((TMD, W_C), row),
                  pl.BlockSpec((TMD, W_C), row),
                  pl.BlockSpec((TMD, W_C), lambda i: (i, 2)),
                  pl.BlockSpec((TMD, 3 * D_MODEL), row),
                  _layer_spec(gn, l), _layer_spec(wa, l), _layer_spec(wb, l), _layer_spec(wc, l),
                  _layer_spec(wo, l),
                  pl.BlockSpec((1, D_MODEL), lambda i: (0, 0))],
        out_specs=pl.BlockSpec((TMD, D_MODEL), row),
        out_shape=jax.ShapeDtypeStruct((t, D_MODEL), F32),
        compiler_params=pltpu.CompilerParams(dimension_semantics=("parallel",)),
        name="merge",
    )(x2d, mod, oa, ob, cf, cb, z, gates, gn, wa, wb, wc, wo, fg)


def _rope_tables(n_tokens, rot_dim):
    rows = n_tokens // GRID_W
    row = np.repeat(np.arange(rows), GRID_W).astype(np.float32)
    col = np.tile(np.arange(GRID_W), rows).astype(np.float32)
    n_pairs = rot_dim // 4
    inv = (np.float32(ROPE_BASE) ** (-np.arange(n_pairs, dtype=np.float32) / np.float32(n_pairs))).astype(np.float32)
    ang = np.concatenate([row[:, None] * inv, col[:, None] * inv], axis=-1)
    c, s = np.cos(ang), np.sin(ang)
    return np.repeat(c, 2, axis=-1), np.stack([-s, s], axis=-1).reshape(n_tokens, rot_dim)


def _swap_halves(x):
    return jnp.concatenate([x[..., HD_A:], x[..., :HD_A]], axis=-1)


def _prep_weights(w_in, w_uq, w_ukv):
    depth = w_in.shape[0]
    o = [0]
    for n in IN_SIZES:
        o.append(o[-1] + n)
    seg = lambda i: w_in[..., o[i]:o[i + 1]]
    zeros = lambda n: jnp.zeros((depth, D_MODEL, n), w_in.dtype)
    wp = jnp.concatenate(
        [seg(0), seg(1), _swap_halves(seg(1)), seg(2), _swap_halves(seg(2)),
         seg(3), seg(7), seg(11), seg(12), seg(4), seg(5),
         zeros(S_KPE), seg(6), seg(9), seg(10), zeros(LANE - S_B - 8), seg(8)], axis=-1).astype(BF16)
    wab = jnp.swapaxes(w_in[..., o[9]:o[11]], 1, 2).astype(BF16)
    hd = QK_NOPE_B + QK_ROPE_B
    wuq = jnp.pad(w_uq.reshape(depth, Q_RANK_B, N_HEADS_B, hd), ((0, 0), (0, 0), (0, 0), (0, MLA_HW - hd)))
    wuq = wuq.reshape(depth, Q_RANK_B, HB_COLS).astype(BF16)
    kv = w_ukv.reshape(depth, KV_RANK_B, N_HEADS_B, QK_NOPE_B + V_HD_B)
    wk = jnp.pad(kv[..., :QK_NOPE_B], ((0, 0), (0, 0), (0, 0), (0, MLA_HW - QK_NOPE_B)))
    wv = jnp.pad(kv[..., QK_NOPE_B:], ((0, 0), (0, 0), (0, 0), (0, MLA_HW - V_HD_B)))
    wukv = jnp.concatenate([wk.reshape(depth, KV_RANK_B, HB_COLS), wv.reshape(depth, KV_RANK_B, HB_COLS)], axis=-1)
    return wp, wab, wuq, wukv.astype(BF16)


def _cache_tiles_a(kx, vx):
    k0, k1 = kx[..., 0, :], kx[..., 1, :]
    z = jnp.zeros_like(k0)
    ka = jnp.concatenate([k0, z, z, k0, k1, z, z, k1], axis=-1).astype(BF16)
    v0, v1 = vx[..., 0, :], vx[..., 1, :]
    one = jnp.ones_like(v0)
    va = jnp.concatenate([v0, one, v1, one], axis=-1).astype(BF16)
    return ka, va


def kernel(x_prompt, x_sample, cache_attn_k, cache_attn_v, cache_mla_ckv, cache_mla_kpe, state_gdn, c, c_ctx,
           norm_g, w_ada, b_ada, w_in, attn_sink, mla_q_norm, mla_w_uq, mla_kv_norm, mla_w_ukv, gdn_conv,
           gdn_a_log, gdn_dt_bias, gdn_norm, w_branch_a, w_branch_b, w_branch_c, w_out, final_norm_g):
    depth = w_in.shape[0]
    nb_c, seq_c, _ = x_prompt.shape
    nb_l, seq_l, _ = x_sample.shape
    past = cache_attn_k.shape[2]
    assert P_END % LANE == 0 and seq_c % TM == 0 and seq_l % TMD == 0 and nb_l < 8 and TM == 2 * WINDOW
    assert (nb_c * seq_c) % TMD == 0 and TMD % TM == 0

    cond8 = jnp.zeros((8, D_MODEL), F32).at[:nb_l].set(c).at[nb_l].set(c_ctx)
    mod = _modulation(cond8, w_ada, b_ada).reshape(depth * 8, 1, 3 * D_MODEL)

    c_a, s_a = _rope_tables(seq_l, HD_A)
    c_b, s_b = _rope_tables(seq_l, QK_ROPE_B)
    pad_l, pad_r = S_KPE, LANE - S_KPE - QK_ROPE_B
    one, zero = np.ones((seq_l, 1), np.float32), np.zeros((seq_l, 1), np.float32)
    rope_tabs = tuple(jnp.asarray(a) for a in (
        np.tile(c_a, (1, LANE // HD_A)), np.tile(s_a, (1, LANE // HD_A)),
        np.concatenate([np.tile(one, (1, pad_l)), c_b, np.tile(one, (1, pad_r))], 1),
        np.concatenate([np.tile(zero, (1, pad_l)), s_b, np.tile(zero, (1, pad_r))], 1)))

    wp, wab, wuq, wukv = _prep_weights(w_in, mla_w_uq, mla_w_ukv)
    ng = norm_g.reshape(depth, 1, D_MODEL)
    qn = mla_q_norm.reshape(depth, 1, Q_RANK_B)
    kvn = mla_kv_norm.reshape(depth, 1, KV_RANK_B)
    sink = attn_sink.reshape(depth, 1, N_HEADS_A)
    prow = jnp.stack([gdn_a_log.reshape(depth, -1), gdn_dt_bias.reshape(depth, -1)], axis=1)
    pcol = jnp.swapaxes(prow, 1, 2)
    gn = gdn_norm.reshape(depth, 1, DV_C)
    wa, wb, wc, wo = (w.astype(BF16) for w in (w_branch_a, w_branch_b, w_branch_c, w_out))
    fg = final_norm_g.reshape(1, D_MODEL)
    kxa, vxa = _cache_tiles_a(cache_attn_k, cache_attn_v)
    kpex = jnp.pad(cache_mla_kpe, ((0, 0), (0, 0), (0, 0), (pad_l, pad_r)))

    tps_c, tps_l = seq_c // TM, seq_l // TM
    tpd_l = seq_l // TMD
    y_p = x_prompt.reshape(nb_c * seq_c, D_MODEL)
    y_s = x_sample.reshape(nb_l * seq_l, D_MODEL)
    ks, vs, ckvs, kpes, sts = [], [], [], [], []
    for l in range(depth):
        last = l == depth - 1

        mod_row_c = lambda i, l=l: l * 8 + nb_l
        (qa, ka, va, z, gates, qb, kb, vb, small, cqkv, abt, kva, ckvn) = _inproj(
            y_p, l, mod, mod_row_c, ng, wp, wab, wuq, wukv, qn, kvn, None, tps_c, True)
        oa = _attn_a_ctx(qa, ka, va, z, sink, l, seq_c)
        ob = _attn_b(qb, kb, vb, z, seq_c, seq_c, N_HEADS_B)
        u, w, qg, kd, attn, eg = _gdn_local(cqkv, small, abt, gdn_conv, prow, pcol, l, tps_c)
        cf, cb, st = _gdn_scan(u, w, qg, kd, attn, eg, None, l, seq_c, True)
        y_p = _merge(y_p, l, mod, mod_row_c, oa, ob, cf, cb, z, gates, gn, wa, wb, wc, wo, fg, last)
        ks.append(kva[:, :N_KV_A * HD_A].reshape(nb_c, seq_c, N_KV_A, HD_A))
        vs.append(kva[:, N_KV_A * HD_A:].reshape(nb_c, seq_c, N_KV_A, HD_A))
        ckvs.append(ckvn.reshape(nb_c, seq_c, KV_RANK_B))
        kpes.append(small[:, S_KPE:S_KPE + QK_ROPE_B].reshape(nb_c, seq_c, QK_ROPE_B))
        sts.append(st)

        mod_row_l = lambda i, l=l: l * 8 + i // tpd_l
        (qa, ka, va, z, gates, qb, kb, vb, small, cqkv, abt) = _inproj(
            y_s, l, mod, mod_row_l, ng, wp, wab, wuq, wukv, qn, kvn, rope_tabs, tpd_l, False)
        oa = _attn_a_lat(qa, ka, va, z, sink, kxa, vxa, l, seq_l)
        kxb, vxb = _kvup(cache_mla_ckv, kpex, wukv, l)
        ob = _attn_b(qb, kb, vb, z, seq_l, TMD, 2, kxb, vxb)
        u, w, qg, kd, attn, eg = _gdn_local(cqkv, small, abt, gdn_conv, prow, pcol, l, tps_l)
        cf, cb = _gdn_scan(u, w, qg, kd, attn, eg, state_gdn, l, seq_l, False)
        y_s = _merge(y_s, l, mod, mod_row_l, oa, ob, cf, cb, z, gates, gn, wa, wb, wc, wo, fg, last)

    return (y_p.reshape(nb_c, seq_c, D_MODEL), y_s.reshape(nb_l, seq_l, D_MODEL),
            jnp.stack(ks, axis=1), jnp.stack(vs, axis=1), jnp.stack(ckvs, axis=1), jnp.stack(kpes, axis=1),
            jnp.stack(sts, axis=1))
```

```python
import functools

import numpy as np
import jax
import jax.numpy as jnp
from jax import lax
from jax.experimental import pallas as pl
from jax.experimental.pallas import tpu as pltpu

F32 = jnp.float32
BF16 = jnp.bfloat16

D_MODEL = 1024
GRID_W = 64
ROPE_BASE = 10000.0
EPS = 1e-6
NEG_INF = -1e30
N_HEADS_A = 8
N_KV_A = 2
HD_A = 64
GQA_GROUP = N_HEADS_A // N_KV_A
WINDOW = 128
N_HEADS_B = 8
QK_NOPE_B = 64
QK_ROPE_B = 32
V_HD_B = 64
Q_RANK_B = 384
KV_RANK_B = 256
MLA_SCALE = (QK_NOPE_B + QK_ROPE_B) ** -0.5
N_HEADS_C = 4
DK_C = 128
DV_C = 128
CHUNK = 64
W_A = N_HEADS_A * HD_A
W_B = N_HEADS_B * V_HD_B
W_C = N_HEADS_C * DV_C
QKV_C = 2 * N_HEADS_C * DK_C + W_C
IN_SIZES = (W_A, N_KV_A * HD_A, N_KV_A * HD_A, W_A, Q_RANK_B, KV_RANK_B, QK_ROPE_B, W_B, QKV_C,
            2 * N_HEADS_C, 2 * N_HEADS_C, W_C, 3 * D_MODEL)

LANE = 128
HALF = LANE // 2
TM = 256
TMD = 512
CPT = TM // CHUNK
HALO = 16
MLA_HW = 128
KA_COLS = 4 * LANE
HB_COLS = N_HEADS_B * MLA_HW
VT_ONES = 16
VT_ROWS = V_HD_B + VT_ONES
VAT_ROWS = N_KV_A * VT_ROWS
VBT_ROWS = N_HEADS_B * VT_ROWS
LOG2E = 1.4426950408889634

P_QKV = 0
A_COLS = W_A + 3 * LANE
P_Z = P_QKV + A_COLS
P_GATES = P_Z + 1536
P_CQ = P_GATES + 3 * D_MODEL
P_CKV = P_CQ + Q_RANK_B
P_SMALL = P_CKV + KV_RANK_B
P_CQKV = P_SMALL + LANE
P_END = P_CQKV + QKV_C
S_KPE = 64
S_A = 96
S_B = 104


def _sigmoid(x):
    return 0.5 * jnp.tanh(0.5 * x) + 0.5


def _silu(x):
    return x * _sigmoid(x)


def _softplus(x):
    return jnp.maximum(x, 0.0) + jnp.log(1.0 + jnp.exp(-jnp.abs(x)))


def _dot(a, b):
    return jnp.dot(a, b, preferred_element_type=F32)


def _dot_nt(a, b):
    return lax.dot_general(a, b, (((1,), (1,)), ((), ())), preferred_element_type=F32)


def _bdot(a, b):
    return lax.dot_general(a, b, (((2,), (1,)), ((0,), (0,))), preferred_element_type=F32)


def _bdot_nt(a, b):
    return lax.dot_general(a, b, (((2,), (2,)), ((0,), (0,))), preferred_element_type=F32)


def _bdot_tn(a, b):
    return lax.dot_general(a, b, (((1,), (1,)), ((0,), (0,))), preferred_element_type=F32)


def _dot_exact(a, b):
    return jnp.dot(a, b, preferred_element_type=F32, precision=lax.Precision.HIGHEST)


def _rope(x, c, s):
    n = x.shape[-1]
    lane = lax.broadcasted_iota(jnp.int32, x.shape, 1)
    swapped = jnp.where(lane % 2 == 0, pltpu.roll(x, n - 1, 1), pltpu.roll(x, 1, 1))
    return x * c + swapped * s


def _mod_kernel(cond_ref, w_ref, b_ref, out_ref):
    cnd = cond_ref[...]
    out_ref[0] = _dot(_silu(cnd).astype(BF16), w_ref[0].astype(BF16)) + b_ref[0]


def _modulation(cond8, w_ada, b_ada):
    depth = w_ada.shape[0]
    tn = 768
    return pl.pallas_call(
        _mod_kernel,
        grid=(depth, 3 * D_MODEL // tn),
        in_specs=[pl.BlockSpec((8, D_MODEL), lambda l, n: (0, 0)),
                  pl.BlockSpec((1, D_MODEL, tn), lambda l, n: (l, 0, n)),
                  pl.BlockSpec((1, 1, tn), lambda l, n: (l, 0, n))],
        out_specs=pl.BlockSpec((1, 8, tn), lambda l, n: (l, 0, n)),
        out_shape=jax.ShapeDtypeStruct((depth, 8, 3 * D_MODEL), F32),
        name="adaln_mod",
    )(cond8, w_ada, b_ada.reshape(depth, 1, 3 * D_MODEL))


def _inproj_kernel(*refs, rope, ctx):
    it = iter(refs)
    (x_ref, mod_ref, ng_ref, wp_ref, wab_ref, wvat_ref, wuq_ref, wukv_ref, wvbt_ref, qn_ref,
     kvn_ref) = (next(it) for _ in range(11))
    if rope:
        ca_ref, sa_ref, cb_ref, sb_ref = (next(it) for _ in range(4))
    (qa_ref, ka_ref, vat_ref, z_ref, gates_ref, qb_ref, kb_ref, vbt_ref, small_ref, cqkv_ref,
     abt_ref) = (next(it) for _ in range(11))
    if ctx:
        kva_ref, ckvn_ref = (next(it) for _ in range(2))

    x = x_ref[...]
    mod = mod_ref[0]
    shift, scale = mod[:, :D_MODEL], mod[:, D_MODEL:2 * D_MODEL]
    xn = x * lax.rsqrt(jnp.mean(x * x, axis=-1, keepdims=True) + EPS) * ng_ref[...]
    hb = (xn * (1.0 + scale) + shift).astype(BF16)
    lane = lax.broadcasted_iota(jnp.int32, (TMD, LANE), 1)
    lo = lane < HALF

    def mm(lo_col, hi_col):
        return _dot(hb, wp_ref[:, lo_col:hi_col])

    r = mm(P_QKV, P_QKV + A_COLS)
    tiles = [r[:, t * LANE:(t + 1) * LANE] for t in range(A_COLS // LANE)]
    if ctx:
        kva_ref[:, :LANE] = tiles[4]
        kva_ref[:, LANE:] = tiles[6]
    if rope:
        ca, sa = ca_ref[...], sa_ref[...]
        tiles[:6] = [_rope(t, ca, sa) for t in tiles[:6]]
    for t in range(4):
        qa_ref[:, t * LANE:(t + 1) * LANE] = (tiles[t] * (HD_A ** -0.5 * LOG2E)).astype(BF16)
    k01, k10 = tiles[4], tiles[5]
    ka_ref[:, 0 * LANE:1 * LANE] = jnp.where(lo, k01, 0.0).astype(BF16)
    ka_ref[:, 1 * LANE:2 * LANE] = jnp.where(lo, 0.0, k10).astype(BF16)
    ka_ref[:, 2 * LANE:3 * LANE] = jnp.where(lo, k10, 0.0).astype(BF16)
    ka_ref[:, 3 * LANE:4 * LANE] = jnp.where(lo, 0.0, k01).astype(BF16)
    ones = jnp.ones((VT_ONES, TMD), BF16)
    vt = _dot_nt(wvat_ref[...], hb)
    for g in range(N_KV_A):
        vat_ref[g * VT_ROWS:g * VT_ROWS + HD_A] = vt[g * HD_A:(g + 1) * HD_A].astype(BF16)
        vat_ref[g * VT_ROWS + HD_A:(g + 1) * VT_ROWS] = ones

    for t in range(3):
        z_ref[:, t * 512:(t + 1) * 512] = mm(P_Z + t * 512, P_Z + (t + 1) * 512).astype(BF16)
    for t in range(6):
        gates_ref[:, t * 512:(t + 1) * 512] = mm(P_GATES + t * 512, P_GATES + (t + 1) * 512).astype(BF16)

    r = mm(P_CQ, P_CQ + Q_RANK_B)
    qn = r * lax.rsqrt(jnp.mean(r * r, axis=-1, keepdims=True) + EPS) * qn_ref[...]
    q = _dot(qn.astype(BF16), wuq_ref[...])
    if rope:
        cb, sb = cb_ref[...], sb_ref[...]
        for h in range(N_HEADS_B):
            seg = _rope(q[:, h * MLA_HW:(h + 1) * MLA_HW], cb, sb) * (MLA_SCALE * LOG2E)
            qb_ref[:, h * MLA_HW:(h + 1) * MLA_HW] = seg.astype(BF16)
    else:
        qb_ref[...] = (q * (MLA_SCALE * LOG2E)).astype(BF16)

    r = mm(P_SMALL, P_SMALL + LANE)
    small_ref[...] = r
    kp = _rope(r, cb, sb) if rope else r
    kp = jnp.where((lane >= S_KPE) & (lane < S_KPE + QK_ROPE_B), kp, 0.0)

    r = mm(P_CKV, P_CKV + KV_RANK_B)
    cn = r * lax.rsqrt(jnp.mean(r * r, axis=-1, keepdims=True) + EPS) * kvn_ref[...]
    if ctx:
        ckvn_ref[...] = cn
    cn16 = cn.astype(BF16)
    kv = _dot(cn16, wukv_ref[...])
    vt = _dot_nt(wvbt_ref[...], cn16)
    for h in range(N_HEADS_B):
        kb_ref[:, h * MLA_HW:(h + 1) * MLA_HW] = (kv[:, h * MLA_HW:(h + 1) * MLA_HW] + kp).astype(BF16)
        vbt_ref[h * VT_ROWS:h * VT_ROWS + V_HD_B] = vt[h * V_HD_B:(h + 1) * V_HD_B].astype(BF16)
        vbt_ref[h * VT_ROWS + V_HD_B:(h + 1) * VT_ROWS] = ones

    for t in range(3):
        cqkv_ref[:, t * 512:(t + 1) * 512] = mm(P_CQKV + t * 512, P_CQKV + (t + 1) * 512).astype(BF16)

    for c in range(TMD // CHUNK):
        abt_ref[c] = _dot_nt(wab_ref[...], hb[c * CHUNK:(c + 1) * CHUNK])


def _layer_spec(arr, l):
    nd = arr.ndim - 1
    return pl.BlockSpec((None,) + arr.shape[1:], lambda *_: (l,) + (0,) * nd, pipeline_mode=pl.Buffered(1))


def _inproj(x2d, l, mod, mod_row_fn, ng, weights, qn, kvn, rope_tabs, tiles_per_seq, ctx):
    t = x2d.shape[0]
    nt = t // TMD
    rope = rope_tabs is not None
    row = lambda i: (i, 0)
    col = lambda i: (0, i)
    wp, wab, wvat, wuq, wukv, wvbt = weights
    params = (ng, wp, wab, wvat, wuq, wukv, wvbt, qn, kvn)
    in_specs = [pl.BlockSpec((TMD, D_MODEL), row),
                pl.BlockSpec((1, 1, 3 * D_MODEL), lambda i: (mod_row_fn(i), 0, 0))]
    in_specs += [_layer_spec(a, l) for a in params]
    args = [x2d, mod, *params]
    if rope:
        pos = lambda i: (i % tiles_per_seq, 0)
        in_specs += [pl.BlockSpec((TMD, LANE), pos)] * 4
        args += list(rope_tabs)
    outs = [(W_A, BF16, False), (KA_COLS, BF16, False), (VAT_ROWS, BF16, True), (1536, BF16, False),
            (3 * D_MODEL, BF16, False), (HB_COLS, BF16, False), (HB_COLS, BF16, False), (VBT_ROWS, BF16, True),
            (LANE, F32, False), (QKV_C, BF16, False)]
    out_shape = [jax.ShapeDtypeStruct((w, t) if tr else (t, w), dt) for w, dt, tr in outs]
    out_specs = [pl.BlockSpec((w, TMD), col) if tr else pl.BlockSpec((TMD, w), row) for w, _, tr in outs]
    out_shape.append(jax.ShapeDtypeStruct((t // CHUNK, 16, CHUNK), F32))
    out_specs.append(pl.BlockSpec((TMD // CHUNK, 16, CHUNK), lambda i: (i, 0, 0)))
    if ctx:
        out_shape += [jax.ShapeDtypeStruct((t, 2 * N_KV_A * HD_A), F32), jax.ShapeDtypeStruct((t, KV_RANK_B), F32)]
        out_specs += [pl.BlockSpec((TMD, 2 * N_KV_A * HD_A), row), pl.BlockSpec((TMD, KV_RANK_B), row)]
    return pl.pallas_call(
        functools.partial(_inproj_kernel, rope=rope, ctx=ctx),
        grid=(nt,),
        in_specs=in_specs,
        out_specs=out_specs,
        out_shape=out_shape,
        compiler_params=pltpu.CompilerParams(dimension_semantics=("parallel",)),
        name="inproj_ctx" if ctx else "inproj_lat",
    )(*args)


def _kvup_kernel(c_ref, kpe_ref, w_ref, wvt_ref, k_ref, vt_ref):
    c16 = c_ref[...].astype(BF16)
    kv = _dot(c16, w_ref[...])
    vt = _dot_nt(wvt_ref[...], c16)
    kp = kpe_ref[...]
    ones = jnp.ones((VT_ONES, c16.shape[0]), BF16)
    for h in range(N_HEADS_B):
        k_ref[:, h * MLA_HW:(h + 1) * MLA_HW] = (kv[:, h * MLA_HW:(h + 1) * MLA_HW] + kp).astype(BF16)
        vt_ref[h * VT_ROWS:h * VT_ROWS + V_HD_B] = vt[h * V_HD_B:(h + 1) * V_HD_B].astype(BF16)
        vt_ref[h * VT_ROWS + V_HD_B:(h + 1) * VT_ROWS] = ones


def _kvup(ckv, kpe, wukv, wvbt, l):
    nb, _, past, _ = ckv.shape
    return pl.pallas_call(
        _kvup_kernel,
        grid=(nb,),
        in_specs=[pl.BlockSpec((None, None, past, KV_RANK_B), lambda b: (b, l, 0, 0)),
                  pl.BlockSpec((None, None, past, LANE), lambda b: (b, l, 0, 0)),
                  _layer_spec(wukv, l), _layer_spec(wvbt, l)],
        out_specs=[pl.BlockSpec((past, HB_COLS), lambda b: (b, 0)),
                   pl.BlockSpec((None, VBT_ROWS, past), lambda b: (b, 0, 0))],
        out_shape=[jax.ShapeDtypeStruct((nb * past, HB_COLS), BF16),
                   jax.ShapeDtypeStruct((nb, VBT_ROWS, past), BF16)],
        name="mla_cache_up",
    )(ckv, kpe, wukv, wvbt)


def _attend(q_tiles, k_tiles, vt_tiles, bias_t, sink):
    return _softmax_pv(_scores_t(q_tiles, k_tiles, bias_t), vt_tiles, sink)


def _scores_t(q_tiles, k_tiles, bias_t):
    st = _bdot_nt(jnp.stack(k_tiles), jnp.stack(q_tiles))
    return st if bias_t is None else st + bias_t[None]


def _softmax_pv(st, vt_tiles, sink):
    m = jnp.max(st, axis=1, keepdims=True)
    if sink is not None:
        m = jnp.maximum(m, sink)
    ot = _bdot(jnp.stack(vt_tiles), jnp.exp2(st - m).astype(BF16))
    den = ot[:, V_HD_B:V_HD_B + 1, :]
    if sink is not None:
        den = den + jnp.exp2(sink - m)
    num = ot[:, :V_HD_B, :] / den
    return [jnp.concatenate([num[2 * i], num[2 * i + 1]], axis=0).T for i in range(st.shape[0] // 2)]


def _tile(x, t):
    return x[:, t * LANE:(t + 1) * LANE]


def _attn_a_heads(q, ka, vat, bias_t, sink_ref):
    qs, ks, vs = [], [], []
    for t in range(N_HEADS_A // 2):
        g = (2 * t) // GQA_GROUP
        for e in range(2):
            qs.append(_tile(q, t))
            ks.append(_tile(ka, 2 * g + e))
            vs.append(vat[g * VT_ROWS:(g + 1) * VT_ROWS])
    sink = jnp.stack([sink_ref[:, h:h + 1] * LOG2E for h in range(N_HEADS_A)])
    return _attend(qs, ks, vs, bias_t, sink)


def _gated_store(outs, z_ref, o_ref, first_tile=0):
    for i, o in enumerate(outs):
        t = first_tile + i
        z = _tile(z_ref, t).astype(F32)
        o_ref[:, t * LANE:(t + 1) * LANE] = (o * _silu(z)).astype(BF16)


def _attn_a_ctx_kernel(q_ref, ka_ref, vat_ref, z_ref, sink_ref, o_ref):
    _gated_store(_attn_a_heads(q_ref[...], ka_ref[...], vat_ref[...], None, sink_ref), z_ref, o_ref)


def _attn_a_ctx(qa, ka, vat, z, sink, l, seq):
    t = qa.shape[0]
    row = lambda b: (b, 0)
    return pl.pallas_call(
        _attn_a_ctx_kernel,
        grid=(t // seq,),
        in_specs=[pl.BlockSpec((seq, W_A), row),
                  pl.BlockSpec((seq, KA_COLS), row),
                  pl.BlockSpec((VAT_ROWS, seq), lambda b: (0, b)),
                  pl.BlockSpec((seq, W_A), row),
                  _layer_spec(sink, l)],
        out_specs=pl.BlockSpec((seq, W_A), row),
        out_shape=jax.ShapeDtypeStruct((t, W_A), BF16),
        compiler_params=pltpu.CompilerParams(dimension_semantics=("parallel",)),
        name="attn_a_ctx",
    )(qa, ka, vat, z, sink)


def _attn_a_lat_kernel(q_ref, kp_ref, kc_ref, kn_ref, vp_ref, vc_ref, vn_ref, kx_ref, vx_ref, z_ref, sink_ref,
                       o_ref, *, nq):
    j = pl.program_id(1)
    ka = jnp.concatenate([kp_ref[...], kc_ref[...], kn_ref[...], kx_ref[...]], axis=0)
    vat = jnp.concatenate([vp_ref[...], vc_ref[...], vn_ref[...], vx_ref[...]], axis=1)
    n_loc = TM + 2 * WINDOW
    kj = lax.broadcasted_iota(jnp.int32, (ka.shape[0], TM), 0)
    qi = lax.broadcasted_iota(jnp.int32, (ka.shape[0], TM), 1)
    ok = (kj >= qi) & (kj <= qi + 2 * WINDOW)
    ok = ok & ((kj >= WINDOW) | (j > 0)) & ((kj < TM + WINDOW) | (j < nq - 1))
    bias_t = jnp.where(ok | (kj >= n_loc), 0.0, NEG_INF)
    _gated_store(_attn_a_heads(q_ref[...], ka, vat, bias_t, sink_ref), z_ref, o_ref)


def _attn_a_lat(qa, ka, vat, z, sink, kx, vxt, l, seq):
    t = qa.shape[0]
    nq = seq // TM
    past = kx.shape[2]
    r = TM // WINDOW
    row = lambda b, j: (b * nq + j, 0)
    prev = lambda b, j: ((b * nq + j) * r - jnp.where(j > 0, 1, 0), 0)
    nxt = lambda b, j: ((b * nq + j) * r + jnp.where(j < nq - 1, r, r - 1), 0)
    swap = lambda f: (lambda b, j: f(b, j)[::-1])
    return pl.pallas_call(
        functools.partial(_attn_a_lat_kernel, nq=nq),
        grid=(t // seq, nq),
        in_specs=[pl.BlockSpec((TM, W_A), row),
                  pl.BlockSpec((WINDOW, KA_COLS), prev),
                  pl.BlockSpec((TM, KA_COLS), row),
                  pl.BlockSpec((WINDOW, KA_COLS), nxt),
                  pl.BlockSpec((VAT_ROWS, WINDOW), swap(prev)),
                  pl.BlockSpec((VAT_ROWS, TM), swap(row)),
                  pl.BlockSpec((VAT_ROWS, WINDOW), swap(nxt)),
                  pl.BlockSpec((None, None, past, KA_COLS), lambda b, j: (b, l, 0, 0)),
                  pl.BlockSpec((None, None, VAT_ROWS, past), lambda b, j: (b, l, 0, 0)),
                  pl.BlockSpec((TM, W_A), row),
                  _layer_spec(sink, l)],
        out_specs=pl.BlockSpec((TM, W_A), row),
        out_shape=jax.ShapeDtypeStruct((t, W_A), BF16),
        compiler_params=pltpu.CompilerParams(dimension_semantics=("parallel", "parallel")),
        name="attn_a_lat",
    )(qa, ka, ka, ka, vat, vat, vat, kx, vxt, z, sink)


def _attn_b_kernel(*refs, has_ctx, group):
    if has_ctx:
        q_ref, k_ref, vt_ref, kx_ref, vxt_ref, z_ref, o_ref = refs
        kb = jnp.concatenate([k_ref[...], kx_ref[...]], axis=0)
        vbt = jnp.concatenate([vt_ref[...], vxt_ref[...]], axis=1)
    else:
        q_ref, k_ref, vt_ref, z_ref, o_ref = refs
        kb, vbt = k_ref[...], vt_ref[...]
    q = q_ref[...]

    def scores(h0):
        heads = range(h0, h0 + group)
        return _scores_t([_tile(q, h) for h in heads], [_tile(kb, h) for h in heads], None)

    st = scores(0)
    for h0 in range(0, N_HEADS_B, group):
        st_next = scores(h0 + group) if h0 + group < N_HEADS_B else None
        outs = _softmax_pv(st, [vbt[h * VT_ROWS:(h + 1) * VT_ROWS] for h in range(h0, h0 + group)], None)
        _gated_store(outs, z_ref, o_ref, h0 // 2)
        st = st_next


def _attn_b(qb, kb, vbt, z, seq, qblk, group, kx=None, vxt=None):
    t = qb.shape[0]
    nq = seq // qblk
    has_ctx = kx is not None
    hw = HB_COLS
    in_specs = [pl.BlockSpec((qblk, hw), lambda b, j: (b * nq + j, 0)),
                pl.BlockSpec((seq, hw), lambda b, j: (b, 0)),
                pl.BlockSpec((VBT_ROWS, seq), lambda b, j: (0, b))]
    args = [qb, kb, vbt]
    if has_ctx:
        past = kx.shape[0] // (t // seq)
        in_specs += [pl.BlockSpec((past, hw), lambda b, j: (b, 0)),
                     pl.BlockSpec((None, VBT_ROWS, past), lambda b, j: (b, 0, 0))]
        args += [kx, vxt]
    in_specs.append(pl.BlockSpec((qblk, W_B), lambda b, j: (b * nq + j, 1)))
    args.append(z)
    return pl.pallas_call(
        functools.partial(_attn_b_kernel, has_ctx=has_ctx, group=group),
        grid=(t // seq, nq),
        in_specs=in_specs,
        out_specs=pl.BlockSpec((qblk, W_B), lambda b, j: (b * nq + j, 0)),
        out_shape=jax.ShapeDtypeStruct((t, W_B), BF16),
        compiler_params=pltpu.CompilerParams(dimension_semantics=("parallel", "parallel")),
        name="attn_b_lat" if has_ctx else "attn_b_ctx",
    )(*args)


def _gdn_local_kernel(cq_ref, prev_ref, next_ref, small_ref, abt_ref, cw_ref, prow_ref, pcol_ref,
                      u_ref, w_ref, qg_ref, kd_ref, attn_ref, eg_ref, qkv_scr, gb_scr, *, tiles_per_seq):
    tpos = pl.program_id(0) % tiles_per_seq
    x = cq_ref[...].astype(F32)
    prev_row = jnp.where(tpos > 0, prev_ref[...].astype(F32)[HALO - 1:HALO, :], 0.0)
    next_row = jnp.where(tpos < tiles_per_seq - 1, next_ref[...].astype(F32)[0:1, :], 0.0)
    rows = lax.broadcasted_iota(jnp.int32, (TM, 1), 0)
    xm1 = jnp.where(rows == 0, prev_row, pltpu.roll(x, 1, 0))
    xp1 = jnp.where(rows == TM - 1, next_row, pltpu.roll(x, TM - 1, 0))
    cw = cw_ref[...]
    y = _silu(xm1 * cw[0:1] + x * cw[1:2] + xp1 * cw[2:3])
    nq = N_HEADS_C * DK_C
    for h in range(N_HEADS_C):
        qh = y[:, h * DK_C:(h + 1) * DK_C]
        kh = y[:, nq + h * DK_C:nq + (h + 1) * DK_C]
        qkv_scr[:, h * DK_C:(h + 1) * DK_C] = (
            qh * lax.rsqrt(jnp.sum(qh * qh, axis=-1, keepdims=True) + EPS) * (DK_C ** -0.5))
        qkv_scr[:, nq + h * DK_C:nq + (h + 1) * DK_C] = kh * lax.rsqrt(jnp.sum(kh * kh, axis=-1, keepdims=True) + EPS)
    qkv_scr[:, 2 * nq:] = y[:, 2 * nq:]

    sm = small_ref[...]
    prow = prow_ref[...]
    gb_scr[:, 0:8] = -jnp.exp(prow[0:1]) * _softplus(sm[:, S_A:S_A + 8] + prow[1:2])
    gb_scr[:, 8:16] = _sigmoid(sm[:, S_B:S_B + 8])
    pcol = pcol_ref[...]

    ri = lax.broadcasted_iota(jnp.int32, (CHUNK, CHUNK), 0)
    ci = lax.broadcasted_iota(jnp.int32, (CHUNK, CHUNK), 1)
    tril = (ri >= ci).astype(F32)
    triu = (ri <= ci).astype(F32)
    xor = ri ^ ci
    eye = (ri == ci).astype(F32)

    lows, rhss, order = [], [], []
    for c in range(CPT):
        rs = slice(c * CHUNK, (c + 1) * CHUNK)
        gcol = gb_scr[rs, 0:8]
        bcol = gb_scr[rs, 8:16]
        abt = abt_ref[c]
        grow = -jnp.exp(pcol[:, 0:1]) * _softplus(abt[0:8] + pcol[:, 1:2])
        gc_f = _dot_exact(tril, gcol)
        gc_b = _dot_exact(triu, gcol)
        gr_f = _dot_exact(grow, triu)
        gr_b = _dot_exact(grow, tril)
        for h in range(N_HEADS_C):
            q = qkv_scr[rs, h * DK_C:(h + 1) * DK_C]
            k = qkv_scr[rs, nq + h * DK_C:nq + (h + 1) * DK_C]
            v = qkv_scr[rs, 2 * nq + h * DV_C:2 * nq + (h + 1) * DV_C]
            kb16 = k.astype(BF16)
            kk = _dot_nt(kb16, kb16)
            qk = _dot_nt(q.astype(BF16), kb16)
            for d in range(2):
                dh = d * N_HEADS_C + h
                gc = (gc_f if d == 0 else gc_b)[:, dh:dh + 1]
                gr = (gr_f if d == 0 else gr_b)[dh:dh + 1, :]
                beta = bcol[:, dh:dh + 1]
                incl = (ri >= ci) if d == 0 else (ri <= ci)
                strict = (ri > ci) if d == 0 else (ri < ci)
                decay = jnp.where(incl, jnp.exp(jnp.where(incl, gc - gr, 0.0)), 0.0)
                lows.append(jnp.where(strict, beta * kk * decay, 0.0))
                eg = jnp.exp(gc)
                rhss.append(jnp.concatenate([v * beta, k * (beta * eg)], axis=-1).astype(BF16))
                order.append((rs, dh))
                g_last = gc[CHUNK - 1:CHUNK] if d == 0 else gc[0:1]
                cs = slice(dh * DK_C, (dh + 1) * DK_C)
                qg_ref[rs, cs] = (q * eg).astype(BF16)
                kd_ref[rs, cs] = (k * jnp.exp(g_last - gc)).astype(BF16)
                attn_ref[rs, dh * CHUNK:(dh + 1) * CHUNK] = (qk * decay).astype(BF16)
                eg_ref[c, dh:dh + 1, :] = jnp.broadcast_to(jnp.exp(g_last), (1, LANE))
    low = jnp.stack(lows, axis=0)
    inv = eye[None] - jnp.where(xor[None] == 1, low, 0.0)
    b = 2
    while b < CHUNK:
        cpl = jnp.where((xor[None] >= b) & (xor[None] < 2 * b), low, 0.0)
        tmp = _bdot(cpl.astype(BF16), inv.astype(BF16))
        inv = inv - _bdot(inv.astype(BF16), tmp.astype(BF16))
        b *= 2
    uw = _bdot(inv.astype(BF16), jnp.stack(rhss, axis=0))
    for i, (rs, dh) in enumerate(order):
        cs = slice(dh * DK_C, (dh + 1) * DK_C)
        u_ref[rs, cs] = uw[i, :, :DV_C]
        w_ref[rs, cs] = uw[i, :, DV_C:].astype(BF16)


def _gdn_local(cqkv, small, abt, conv_w, prow, pcol, l, tiles_per_seq):
    t = cqkv.shape[0]
    nt = t // TM
    nh8 = t // HALO
    row = lambda i: (i, 0)
    dh = 2 * N_HEADS_C
    return pl.pallas_call(
        functools.partial(_gdn_local_kernel, tiles_per_seq=tiles_per_seq),
        grid=(nt,),
        in_specs=[pl.BlockSpec((TM, QKV_C), row),
                  pl.BlockSpec((HALO, QKV_C), lambda i: (jnp.maximum(i * (TM // HALO) - 1, 0), 0)),
                  pl.BlockSpec((HALO, QKV_C), lambda i: (jnp.minimum((i + 1) * (TM // HALO), nh8 - 1), 0)),
                  pl.BlockSpec((TM, LANE), row),
                  pl.BlockSpec((CPT, 16, CHUNK), lambda i: (i, 0, 0)),
                  _layer_spec(conv_w, l), _layer_spec(prow, l), _layer_spec(pcol, l)],
        out_specs=[pl.BlockSpec((TM, dh * DV_C), row),
                   pl.BlockSpec((TM, dh * DK_C), row),
                   pl.BlockSpec((TM, dh * DK_C), row),
                   pl.BlockSpec((TM, dh * DK_C), row),
                   pl.BlockSpec((TM, dh * CHUNK), row),
                   pl.BlockSpec((CPT, dh, LANE), lambda i: (i, 0, 0))],
        out_shape=[jax.ShapeDtypeStruct((t, dh * DV_C), F32),
                   jax.ShapeDtypeStruct((t, dh * DK_C), BF16),
                   jax.ShapeDtypeStruct((t, dh * DK_C), BF16),
                   jax.ShapeDtypeStruct((t, dh * DK_C), BF16),
                   jax.ShapeDtypeStruct((t, dh * CHUNK), BF16),
                   jax.ShapeDtypeStruct((t // CHUNK, dh, LANE), F32)],
        scratch_shapes=[pltpu.VMEM((TM, QKV_C), F32), pltpu.VMEM((TM, 16), F32)],
        compiler_params=pltpu.CompilerParams(dimension_semantics=("parallel",)),
        name="gdn_local",
    )(cqkv, cqkv, cqkv, small, abt, conv_w, prow, pcol)


def _gdn_scan_kernel(*refs, nt, has_init, want_state):
    it = iter(refs)
    ins = [[next(it) for _ in range(6)] for _ in range(2)]
    s0_ref = next(it) if has_init else None
    o_refs = [next(it), next(it)]
    st_ref = next(it) if want_state else None
    s_scr = next(it)
    j = pl.program_id(1)
    nh = N_HEADS_C

    @pl.when(j == 0)
    def _():
        if has_init:
            s_scr[...] = s0_ref[0].reshape(2 * nh, DK_C, DV_C)
        else:
            s_scr[...] = jnp.zeros_like(s_scr)

    for step in range(CPT):
        def gather(idx, width):
            parts = []
            for d in range(2):
                c = step if d == 0 else CPT - 1 - step
                for h in range(nh):
                    parts.append(ins[d][idx][c * CHUNK:(c + 1) * CHUNK, h * width:(h + 1) * width])
            return jnp.stack(parts)

        u, w, qg, kd, attn = gather(0, DV_C), gather(1, DK_C), gather(2, DK_C), gather(3, DK_C), gather(4, CHUNK)
        eg = jnp.stack([ins[d][5][step if d == 0 else CPT - 1 - step, d * nh + h:d * nh + h + 1, :]
                        for d in range(2) for h in range(nh)])
        s = s_scr[...]
        sb = s.astype(BF16)
        v_new = u - _bdot(w, sb)
        vb = v_new.astype(BF16)
        o = _bdot(qg, sb) + _bdot(attn, vb)
        s_scr[...] = s * eg + _bdot_tn(kd, vb)
        for d in range(2):
            c = step if d == 0 else CPT - 1 - step
            for h in range(nh):
                o_refs[d][c * CHUNK:(c + 1) * CHUNK, h * DV_C:(h + 1) * DV_C] = o[d * nh + h]

    if want_state:
        @pl.when(j == nt - 1)
        def _():
            st_ref[0] = s_scr[...].reshape(2, nh, DK_C, DV_C)


def _gdn_scan(u, w, qg, kd, attn, eg, s0, l, seq, want_state):
    t = u.shape[0]
    nt = seq // TM
    nb = t // seq
    half = N_HEADS_C * DK_C
    has_init = s0 is not None
    in_specs, args = [], []
    for d in range(2):
        if d == 0:
            row = lambda b, j: (b * nt + j, 0)
            row3 = lambda b, j: (b * nt + j, 0, 0)
        else:
            row = lambda b, j: (b * nt + nt - 1 - j, 1)
            row3 = lambda b, j: (b * nt + nt - 1 - j, 0, 0)
        in_specs += [pl.BlockSpec((TM, half), row)] * 4
        in_specs += [pl.BlockSpec((TM, N_HEADS_C * CHUNK), row),
                     pl.BlockSpec((CPT, 2 * N_HEADS_C, LANE), row3)]
        args += [u, w, qg, kd, attn, eg]
    st_block = (1, 2, N_HEADS_C, DK_C, DV_C)
    if has_init:
        in_specs.append(pl.BlockSpec((1, None) + st_block[1:], lambda b, j: (b, l, 0, 0, 0, 0)))
        args.append(s0)
    out_specs = [pl.BlockSpec((TM, half), lambda b, j: (b * nt + j, 0)),
                 pl.BlockSpec((TM, half), lambda b, j: (b * nt + nt - 1 - j, 0))]
    out_shape = [jax.ShapeDtypeStruct((t, half), F32), jax.ShapeDtypeStruct((t, half), F32)]
    if want_state:
        out_specs.append(pl.BlockSpec(st_block, lambda b, j: (b, 0, 0, 0, 0)))
        out_shape.append(jax.ShapeDtypeStruct((nb,) + st_block[1:], F32))
    return pl.pallas_call(
        functools.partial(_gdn_scan_kernel, nt=nt, has_init=has_init, want_state=want_state),
        grid=(nb, nt),
        in_specs=in_specs,
        out_specs=out_specs,
        out_shape=out_shape,
        scratch_shapes=[pltpu.VMEM((2 * N_HEADS_C, DK_C, DV_C), F32)],
        compiler_params=pltpu.CompilerParams(dimension_semantics=("parallel", "arbitrary")),
        name="gdn_scan",
    )(*args)


def _merge_kernel(x_ref, mod_ref, oa_ref, ob_ref, cf_ref, cb_ref, zc_ref, gates_ref, gn_ref,
                  wa_ref, wb_ref, wc_ref, wo_ref, fg_ref, o_ref, *, last):
    oc = cf_ref[...] + cb_ref[...]
    zc = zc_ref[...].astype(F32)
    gn = gn_ref[...]
    parts = []
    for h in range(N_HEADS_C):
        hs = slice(h * DV_C, (h + 1) * DV_C)
        och = oc[:, hs]
        och = och * lax.rsqrt(jnp.mean(och * och, axis=-1, keepdims=True) + EPS) * gn
        parts.append((och * _silu(zc[:, hs])).astype(BF16))
    ocz = jnp.concatenate(parts, axis=-1)
    pa = _dot(oa_ref[...], wa_ref[...])
    pb = _dot(ob_ref[...], wb_ref[...])
    pc = _dot(ocz, wc_ref[...])
    ga = _sigmoid(gates_ref[:, 0:D_MODEL].astype(F32))
    gb = _sigmoid(gates_ref[:, D_MODEL:2 * D_MODEL].astype(F32))
    gc = _sigmoid(gates_ref[:, 2 * D_MODEL:].astype(F32))
    y = _dot((ga * pa + gb * pb + gc * pc).astype(BF16), wo_ref[...])
    gate = mod_ref[0][:, 2 * D_MODEL:]
    xo = x_ref[...] + gate * y
    if last:
        xo = xo * lax.rsqrt(jnp.mean(xo * xo, axis=-1, keepdims=True) + EPS) * fg_ref[...]
    o_ref[...] = xo


def _merge(x2d, l, mod, mod_row_fn, oa, ob, cf, cb, z, gates, gn, wa, wb, wc, wo, fg, last):
    t = x2d.shape[0]
    row = lambda i: (i, 0)
    return pl.pallas_call(
        functools.partial(_merge_kernel, last=last),
        grid=(t // TMD,),
        in_specs=[pl.BlockSpec((TMD, D_MODEL), row),
                  pl.BlockSpec((1, 1, 3 * D_MODEL), lambda i: (mod_row_fn(i), 0, 0)),
                  pl.BlockSpec((TMD, W_A), row),
                  pl.BlockSpec((TMD, W_B), row),
                  pl.BlockSpec((TMD, W_C), row),
                  pl.BlockSpec((TMD, W_C), row),
                  pl.BlockSpec((TMD, W_C), lambda i: (i, 2)),
                  pl.BlockSpec((TMD, 3 * D_MODEL), row),
                  _layer_spec(gn, l), _layer_spec(wa, l), _layer_spec(wb, l), _layer_spec(wc, l),
                  _layer_spec(wo, l),
                  pl.BlockSpec((1, D_MODEL), lambda i: (0, 0))],
        out_specs=pl.BlockSpec((TMD, D_MODEL), row),
        out_shape=jax.ShapeDtypeStruct((t, D_MODEL), F32),
        compiler_params=pltpu.CompilerParams(dimension_semantics=("parallel",)),
        name="merge",
    )(x2d, mod, oa, ob, cf, cb, z, gates, gn, wa, wb, wc, wo, fg)


def _rope_tables(n_tokens, rot_dim):
    rows = n_tokens // GRID_W
    row = np.repeat(np.arange(rows), GRID_W).astype(np.float32)
    col = np.tile(np.arange(GRID_W), rows).astype(np.float32)
    n_pairs = rot_dim // 4
    inv = (np.float32(ROPE_BASE) ** (-np.arange(n_pairs, dtype=np.float32) / np.float32(n_pairs))).astype(np.float32)
    ang = np.concatenate([row[:, None] * inv, col[:, None] * inv], axis=-1)
    c, s = np.cos(ang), np.sin(ang)
    return np.repeat(c, 2, axis=-1), np.stack([-s, s], axis=-1).reshape(n_tokens, rot_dim)


def _swap_halves(x):
    return jnp.concatenate([x[..., HD_A:], x[..., :HD_A]], axis=-1)


def _prep_weights(w_in, w_uq, w_ukv):
    depth = w_in.shape[0]
    o = [0]
    for n in IN_SIZES:
        o.append(o[-1] + n)
    seg = lambda i: w_in[..., o[i]:o[i + 1]]
    zeros = lambda n: jnp.zeros((depth, D_MODEL, n), w_in.dtype)
    wp = jnp.concatenate(
        [seg(0), seg(1), _swap_halves(seg(1)), seg(2),
         seg(3), seg(7), seg(11), seg(12), seg(4), seg(5),
         zeros(S_KPE), seg(6), seg(9), seg(10), zeros(LANE - S_B - 8), seg(8)], axis=-1).astype(BF16)
    wab = jnp.swapaxes(w_in[..., o[9]:o[11]], 1, 2).astype(BF16)
    wvat = jnp.swapaxes(seg(2), 1, 2).astype(BF16)
    hd = QK_NOPE_B + QK_ROPE_B
    wuq = jnp.pad(w_uq.reshape(depth, Q_RANK_B, N_HEADS_B, hd), ((0, 0), (0, 0), (0, 0), (0, MLA_HW - hd)))
    wuq = wuq.reshape(depth, Q_RANK_B, HB_COLS).astype(BF16)
    kv = w_ukv.reshape(depth, KV_RANK_B, N_HEADS_B, QK_NOPE_B + V_HD_B)
    wk = jnp.pad(kv[..., :QK_NOPE_B], ((0, 0), (0, 0), (0, 0), (0, MLA_HW - QK_NOPE_B)))
    wukv = wk.reshape(depth, KV_RANK_B, HB_COLS).astype(BF16)
    wvbt = jnp.swapaxes(kv[..., QK_NOPE_B:].reshape(depth, KV_RANK_B, W_B), 1, 2).astype(BF16)
    return wp, wab, wvat, wuq, wukv, wvbt


def _cache_tiles_a(kx, vx):
    k0, k1 = kx[..., 0, :], kx[..., 1, :]
    z = jnp.zeros_like(k0)
    ka = jnp.concatenate([k0, z, z, k0, k1, z, z, k1], axis=-1).astype(BF16)
    vt = jnp.transpose(vx, (0, 1, 3, 4, 2))
    vt = jnp.concatenate([vt, jnp.ones(vt.shape[:3] + (VT_ONES, vt.shape[4]), vt.dtype)], axis=3)
    return ka, vt.reshape(vt.shape[:2] + (VAT_ROWS, vt.shape[4])).astype(BF16)


def kernel(x_prompt, x_sample, cache_attn_k, cache_attn_v, cache_mla_ckv, cache_mla_kpe, state_gdn, c, c_ctx,
           norm_g, w_ada, b_ada, w_in, attn_sink, mla_q_norm, mla_w_uq, mla_kv_norm, mla_w_ukv, gdn_conv,
           gdn_a_log, gdn_dt_bias, gdn_norm, w_branch_a, w_branch_b, w_branch_c, w_out, final_norm_g):
    depth = w_in.shape[0]
    nb_c, seq_c, _ = x_prompt.shape
    nb_l, seq_l, _ = x_sample.shape
    past = cache_attn_k.shape[2]
    assert P_END % LANE == 0 and seq_c % TM == 0 and seq_l % TMD == 0 and nb_l < 8 and TM == 2 * WINDOW
    assert (nb_c * seq_c) % TMD == 0 and TMD % TM == 0

    cond8 = jnp.zeros((8, D_MODEL), F32).at[:nb_l].set(c).at[nb_l].set(c_ctx)
    mod = _modulation(cond8, w_ada, b_ada).reshape(depth * 8, 1, 3 * D_MODEL)

    c_a, s_a = _rope_tables(seq_l, HD_A)
    c_b, s_b = _rope_tables(seq_l, QK_ROPE_B)
    pad_l, pad_r = S_KPE, LANE - S_KPE - QK_ROPE_B
    one, zero = np.ones((seq_l, 1), np.float32), np.zeros((seq_l, 1), np.float32)
    rope_tabs = tuple(jnp.asarray(a) for a in (
        np.tile(c_a, (1, LANE // HD_A)), np.tile(s_a, (1, LANE // HD_A)),
        np.concatenate([np.tile(one, (1, pad_l)), c_b, np.tile(one, (1, pad_r))], 1),
        np.concatenate([np.tile(zero, (1, pad_l)), s_b, np.tile(zero, (1, pad_r))], 1)))

    weights = _prep_weights(w_in, mla_w_uq, mla_w_ukv)
    wukv, wvbt = weights[4], weights[5]
    ng = norm_g.reshape(depth, 1, D_MODEL)
    qn = mla_q_norm.reshape(depth, 1, Q_RANK_B)
    kvn = mla_kv_norm.reshape(depth, 1, KV_RANK_B)
    sink = attn_sink.reshape(depth, 1, N_HEADS_A)
    prow = jnp.stack([gdn_a_log.reshape(depth, -1), gdn_dt_bias.reshape(depth, -1)], axis=1)
    pcol = jnp.swapaxes(prow, 1, 2)
    gn = gdn_norm.reshape(depth, 1, DV_C)
    wa, wb, wc, wo = (w.astype(BF16) for w in (w_branch_a, w_branch_b, w_branch_c, w_out))
    fg = final_norm_g.reshape(1, D_MODEL)
    kxa, vxa = _cache_tiles_a(cache_attn_k, cache_attn_v)
    kpex = jnp.pad(cache_mla_kpe, ((0, 0), (0, 0), (0, 0), (pad_l, pad_r)))

    tps_c, tps_l = seq_c // TM, seq_l // TM
    tpd_l = seq_l // TMD
    y_p = x_prompt.reshape(nb_c * seq_c, D_MODEL)
    y_s = x_sample.reshape(nb_l * seq_l, D_MODEL)
    ks, vs, ckvs, kpes, sts = [], [], [], [], []
    for l in range(depth):
        last = l == depth - 1

        mod_row_c = lambda i, l=l: l * 8 + nb_l
        (qa, ka, vat, z, gates, qb, kb, vbt, small, cqkv, abt, kva, ckvn) = _inproj(
            y_p, l, mod, mod_row_c, ng, weights, qn, kvn, None, tps_c, True)
        oa = _attn_a_ctx(qa, ka, vat, z, sink, l, seq_c)
        ob = _attn_b(qb, kb, vbt, z, seq_c, seq_c, N_HEADS_B)
        u, w, qg, kd, attn, eg = _gdn_local(cqkv, small, abt, gdn_conv, prow, pcol, l, tps_c)
        cf, cb, st = _gdn_scan(u, w, qg, kd, attn, eg, None, l, seq_c, True)
        y_p = _merge(y_p, l, mod, mod_row_c, oa, ob, cf, cb, z, gates, gn, wa, wb, wc, wo, fg, last)
        ks.append(kva[:, :N_KV_A * HD_A].reshape(nb_c, seq_c, N_KV_A, HD_A))
        vs.append(kva[:, N_KV_A * HD_A:].reshape(nb_c, seq_c, N_KV_A, HD_A))
        ckvs.append(ckvn.reshape(nb_c, seq_c, KV_RANK_B))
        kpes.append(small[:, S_KPE:S_KPE + QK_ROPE_B].reshape(nb_c, seq_c, QK_ROPE_B))
        sts.append(st)

        mod_row_l = lambda i, l=l: l * 8 + i // tpd_l
        (qa, ka, vat, z, gates, qb, kb, vbt, small, cqkv, abt) = _inproj(
            y_s, l, mod, mod_row_l, ng, weights, qn, kvn, rope_tabs, tpd_l, False)
        oa = _attn_a_lat(qa, ka, vat, z, sink, kxa, vxa, l, seq_l)
        kxb, vxb = _kvup(cache_mla_ckv, kpex, wukv, wvbt, l)
        ob = _attn_b(qb, kb, vbt, z, seq_l, TMD, 2, kxb, vxb)
        u, w, qg, kd, attn, eg = _gdn_local(cqkv, small, abt, gdn_conv, prow, pcol, l, tps_l)
        cf, cb = _gdn_scan(u, w, qg, kd, attn, eg, state_gdn, l, seq_l, False)
        y_s = _merge(y_s, l, mod, mod_row_l, oa, ob, cf, cb, z, gates, gn, wa, wb, wc, wo, fg, last)

    return (y_p.reshape(nb_c, seq_c, D_MODEL), y_s.reshape(nb_l, seq_l, D_MODEL),
            jnp.stack(ks, axis=1), jnp.stack(vs, axis=1), jnp.stack(ckvs, axis=1), jnp.stack(kpes, axis=1),
            jnp.stack(sts, axis=1))
```

```python
import functools

import numpy as np
import jax
import jax.numpy as jnp
from jax import lax
from jax.experimental import pallas as pl
from jax.experimental.pallas import tpu as pltpu

F32 = jnp.float32
BF16 = jnp.bfloat16

D_MODEL = 1024
GRID_W = 64
ROPE_BASE = 10000.0
EPS = 1e-6
NEG_INF = -1e30
N_HEADS_A = 8
N_KV_A = 2
HD_A = 64
GQA_GROUP = N_HEADS_A // N_KV_A
WINDOW = 128
N_HEADS_B = 8
QK_NOPE_B = 64
QK_ROPE_B = 32
V_HD_B = 64
Q_RANK_B = 384
KV_RANK_B = 256
MLA_SCALE = (QK_NOPE_B + QK_ROPE_B) ** -0.5
N_HEADS_C = 4
DK_C = 128
DV_C = 128
CHUNK = 64
W_A = N_HEADS_A * HD_A
W_B = N_HEADS_B * V_HD_B
W_C = N_HEADS_C * DV_C
QKV_C = 2 * N_HEADS_C * DK_C + W_C
IN_SIZES = (W_A, N_KV_A * HD_A, N_KV_A * HD_A, W_A, Q_RANK_B, KV_RANK_B, QK_ROPE_B, W_B, QKV_C,
            2 * N_HEADS_C, 2 * N_HEADS_C, W_C, 3 * D_MODEL)

LANE = 128
HALF = LANE // 2
TM = 256
TMD = 512
CPT = TM // CHUNK
SOLVE_CHUNKS = 2
HALO = 16
MLA_HW = 128
KA_COLS = 4 * LANE
HB_COLS = N_HEADS_B * MLA_HW
VT_ONES = 16
VT_ROWS = V_HD_B + VT_ONES
VAT_ROWS = N_KV_A * VT_ROWS
VBT_ROWS = N_HEADS_B * VT_ROWS
LOG2E = 1.4426950408889634

P_QKV = 0
A_COLS = W_A + 3 * LANE
P_Z = P_QKV + A_COLS
P_GATES = P_Z + 1536
P_CQ = P_GATES + 3 * D_MODEL
P_CKV = P_CQ + Q_RANK_B
P_SMALL = P_CKV + KV_RANK_B
P_CQKV = P_SMALL + LANE
P_END = P_CQKV + QKV_C
S_KPE = 64
S_A = 96
S_B = 104


def _sigmoid(x):
    return 0.5 * jnp.tanh(0.5 * x) + 0.5


def _silu(x):
    return x * _sigmoid(x)


def _softplus(x):
    return jnp.maximum(x, 0.0) + jnp.log(1.0 + jnp.exp(-jnp.abs(x)))


def _dot(a, b):
    return jnp.dot(a, b, preferred_element_type=F32)


def _dot_nt(a, b):
    return lax.dot_general(a, b, (((1,), (1,)), ((), ())), preferred_element_type=F32)


def _bdot(a, b):
    return lax.dot_general(a, b, (((2,), (1,)), ((0,), (0,))), preferred_element_type=F32)


def _bdot_nt(a, b):
    return lax.dot_general(a, b, (((2,), (2,)), ((0,), (0,))), preferred_element_type=F32)


def _bdot_tn(a, b):
    return lax.dot_general(a, b, (((1,), (1,)), ((0,), (0,))), preferred_element_type=F32)


def _dot_exact(a, b):
    return jnp.dot(a, b, preferred_element_type=F32, precision=lax.Precision.HIGHEST)


def _rope(x, c, s):
    n = x.shape[-1]
    lane = lax.broadcasted_iota(jnp.int32, x.shape, 1)
    swapped = jnp.where(lane % 2 == 0, pltpu.roll(x, n - 1, 1), pltpu.roll(x, 1, 1))
    return x * c + swapped * s


def _mod_kernel(cond_ref, w_ref, b_ref, out_ref):
    cnd = cond_ref[...]
    out_ref[0] = _dot(_silu(cnd).astype(BF16), w_ref[0].astype(BF16)) + b_ref[0]


def _modulation(cond8, w_ada, b_ada):
    depth = w_ada.shape[0]
    tn = 768
    return pl.pallas_call(
        _mod_kernel,
        grid=(depth, 3 * D_MODEL // tn),
        in_specs=[pl.BlockSpec((8, D_MODEL), lambda l, n: (0, 0)),
                  pl.BlockSpec((1, D_MODEL, tn), lambda l, n: (l, 0, n)),
                  pl.BlockSpec((1, 1, tn), lambda l, n: (l, 0, n))],
        out_specs=pl.BlockSpec((1, 8, tn), lambda l, n: (l, 0, n)),
        out_shape=jax.ShapeDtypeStruct((depth, 8, 3 * D_MODEL), F32),
        name="adaln_mod",
    )(cond8, w_ada, b_ada.reshape(depth, 1, 3 * D_MODEL))


def _inproj_kernel(*refs, rope, ctx):
    it = iter(refs)
    (x_ref, mod_ref, ng_ref, wp_ref, wab_ref, wvat_ref, wuq_ref, wukv_ref, wvbt_ref, qn_ref,
     kvn_ref) = (next(it) for _ in range(11))
    if rope:
        ca_ref, sa_ref, cb_ref, sb_ref = (next(it) for _ in range(4))
    (qa_ref, ka_ref, vat_ref, z_ref, gates_ref, qb_ref, kb_ref, vbt_ref, small_ref, cqkv_ref,
     abt_ref) = (next(it) for _ in range(11))
    if ctx:
        kva_ref, ckvn_ref = (next(it) for _ in range(2))

    x = x_ref[...]
    mod = mod_ref[0]
    shift, scale = mod[:, :D_MODEL], mod[:, D_MODEL:2 * D_MODEL]
    xn = x * lax.rsqrt(jnp.mean(x * x, axis=-1, keepdims=True) + EPS) * ng_ref[...]
    hb = (xn * (1.0 + scale) + shift).astype(BF16)
    lane = lax.broadcasted_iota(jnp.int32, (TMD, LANE), 1)
    lo = lane < HALF

    def mm(lo_col, hi_col):
        return _dot(hb, wp_ref[:, lo_col:hi_col])

    r = mm(P_QKV, P_QKV + A_COLS)
    tiles = [r[:, t * LANE:(t + 1) * LANE] for t in range(A_COLS // LANE)]
    if ctx:
        kva_ref[:, :LANE] = tiles[4]
        kva_ref[:, LANE:] = tiles[6]
    if rope:
        ca, sa = ca_ref[...], sa_ref[...]
        tiles[:6] = [_rope(t, ca, sa) for t in tiles[:6]]
    for t in range(4):
        qa_ref[:, t * LANE:(t + 1) * LANE] = (tiles[t] * (HD_A ** -0.5 * LOG2E)).astype(BF16)
    k01, k10 = tiles[4], tiles[5]
    ka_ref[:, 0 * LANE:1 * LANE] = jnp.where(lo, k01, 0.0).astype(BF16)
    ka_ref[:, 1 * LANE:2 * LANE] = jnp.where(lo, 0.0, k10).astype(BF16)
    ka_ref[:, 2 * LANE:3 * LANE] = jnp.where(lo, k10, 0.0).astype(BF16)
    ka_ref[:, 3 * LANE:4 * LANE] = jnp.where(lo, 0.0, k01).astype(BF16)
    ones = jnp.ones((VT_ONES, TMD), BF16)
    vt = _dot_nt(wvat_ref[...], hb)
    for g in range(N_KV_A):
        vat_ref[g * VT_ROWS:g * VT_ROWS + HD_A] = vt[g * HD_A:(g + 1) * HD_A].astype(BF16)
        vat_ref[g * VT_ROWS + HD_A:(g + 1) * VT_ROWS] = ones

    for t in range(3):
        z_ref[:, t * 512:(t + 1) * 512] = mm(P_Z + t * 512, P_Z + (t + 1) * 512).astype(BF16)
    for t in range(6):
        gates_ref[:, t * 512:(t + 1) * 512] = mm(P_GATES + t * 512, P_GATES + (t + 1) * 512).astype(BF16)

    r = mm(P_CQ, P_CQ + Q_RANK_B)
    qn = r * lax.rsqrt(jnp.mean(r * r, axis=-1, keepdims=True) + EPS) * qn_ref[...]
    q = _dot(qn.astype(BF16), wuq_ref[...])
    if rope:
        cb, sb = cb_ref[...], sb_ref[...]
        for h in range(N_HEADS_B):
            seg = _rope(q[:, h * MLA_HW:(h + 1) * MLA_HW], cb, sb) * (MLA_SCALE * LOG2E)
            qb_ref[:, h * MLA_HW:(h + 1) * MLA_HW] = seg.astype(BF16)
    else:
        qb_ref[...] = (q * (MLA_SCALE * LOG2E)).astype(BF16)

    r = mm(P_SMALL, P_SMALL + LANE)
    small_ref[...] = r
    kp = _rope(r, cb, sb) if rope else r
    kp = jnp.where((lane >= S_KPE) & (lane < S_KPE + QK_ROPE_B), kp, 0.0)

    r = mm(P_CKV, P_CKV + KV_RANK_B)
    cn = r * lax.rsqrt(jnp.mean(r * r, axis=-1, keepdims=True) + EPS) * kvn_ref[...]
    if ctx:
        ckvn_ref[...] = cn
    cn16 = cn.astype(BF16)
    kv = _dot(cn16, wukv_ref[...])
    vt = _dot_nt(wvbt_ref[...], cn16)
    for h in range(N_HEADS_B):
        kb_ref[:, h * MLA_HW:(h + 1) * MLA_HW] = (kv[:, h * MLA_HW:(h + 1) * MLA_HW] + kp).astype(BF16)
        vbt_ref[h * VT_ROWS:h * VT_ROWS + V_HD_B] = vt[h * V_HD_B:(h + 1) * V_HD_B].astype(BF16)
        vbt_ref[h * VT_ROWS + V_HD_B:(h + 1) * VT_ROWS] = ones

    for t in range(3):
        cqkv_ref[:, t * 512:(t + 1) * 512] = mm(P_CQKV + t * 512, P_CQKV + (t + 1) * 512).astype(BF16)

    for c in range(TMD // CHUNK):
        abt_ref[c] = _dot_nt(wab_ref[...], hb[c * CHUNK:(c + 1) * CHUNK])


def _layer_spec(arr, l):
    nd = arr.ndim - 1
    return pl.BlockSpec((None,) + arr.shape[1:], lambda *_: (l,) + (0,) * nd, pipeline_mode=pl.Buffered(1))


def _inproj(x2d, l, mod, mod_row_fn, ng, weights, qn, kvn, rope_tabs, tiles_per_seq, ctx):
    t = x2d.shape[0]
    nt = t // TMD
    rope = rope_tabs is not None
    row = lambda i: (i, 0)
    col = lambda i: (0, i)
    wp, wab, wvat, wuq, wukv, wvbt = weights
    params = (ng, wp, wab, wvat, wuq, wukv, wvbt, qn, kvn)
    in_specs = [pl.BlockSpec((TMD, D_MODEL), row),
                pl.BlockSpec((1, 1, 3 * D_MODEL), lambda i: (mod_row_fn(i), 0, 0))]
    in_specs += [_layer_spec(a, l) for a in params]
    args = [x2d, mod, *params]
    if rope:
        pos = lambda i: (i % tiles_per_seq, 0)
        in_specs += [pl.BlockSpec((TMD, LANE), pos)] * 4
        args += list(rope_tabs)
    outs = [(W_A, BF16, False), (KA_COLS, BF16, False), (VAT_ROWS, BF16, True), (1536, BF16, False),
            (3 * D_MODEL, BF16, False), (HB_COLS, BF16, False), (HB_COLS, BF16, False), (VBT_ROWS, BF16, True),
            (LANE, F32, False), (QKV_C, BF16, False)]
    out_shape = [jax.ShapeDtypeStruct((w, t) if tr else (t, w), dt) for w, dt, tr in outs]
    out_specs = [pl.BlockSpec((w, TMD), col) if tr else pl.BlockSpec((TMD, w), row) for w, _, tr in outs]
    out_shape.append(jax.ShapeDtypeStruct((t // CHUNK, 16, CHUNK), F32))
    out_specs.append(pl.BlockSpec((TMD // CHUNK, 16, CHUNK), lambda i: (i, 0, 0)))
    if ctx:
        out_shape += [jax.ShapeDtypeStruct((t, 2 * N_KV_A * HD_A), F32), jax.ShapeDtypeStruct((t, KV_RANK_B), F32)]
        out_specs += [pl.BlockSpec((TMD, 2 * N_KV_A * HD_A), row), pl.BlockSpec((TMD, KV_RANK_B), row)]
    return pl.pallas_call(
        functools.partial(_inproj_kernel, rope=rope, ctx=ctx),
        grid=(nt,),
        in_specs=in_specs,
        out_specs=out_specs,
        out_shape=out_shape,
        compiler_params=pltpu.CompilerParams(dimension_semantics=("parallel",)),
        name="inproj_ctx" if ctx else "inproj_lat",
    )(*args)


def _kvup_kernel(c_ref, kpe_ref, w_ref, wvt_ref, k_ref, vt_ref):
    c16 = c_ref[...].astype(BF16)
    kv = _dot(c16, w_ref[...])
    vt = _dot_nt(wvt_ref[...], c16)
    kp = kpe_ref[...]
    ones = jnp.ones((VT_ONES, c16.shape[0]), BF16)
    for h in range(N_HEADS_B):
        k_ref[:, h * MLA_HW:(h + 1) * MLA_HW] = (kv[:, h * MLA_HW:(h + 1) * MLA_HW] + kp).astype(BF16)
        vt_ref[h * VT_ROWS:h * VT_ROWS + V_HD_B] = vt[h * V_HD_B:(h + 1) * V_HD_B].astype(BF16)
        vt_ref[h * VT_ROWS + V_HD_B:(h + 1) * VT_ROWS] = ones


def _kvup(ckv, kpe, wukv, wvbt, l):
    nb, _, past, _ = ckv.shape
    return pl.pallas_call(
        _kvup_kernel,
        grid=(nb,),
        in_specs=[pl.BlockSpec((None, None, past, KV_RANK_B), lambda b: (b, l, 0, 0)),
                  pl.BlockSpec((None, None, past, LANE), lambda b: (b, l, 0, 0)),
                  _layer_spec(wukv, l), _layer_spec(wvbt, l)],
        out_specs=[pl.BlockSpec((past, HB_COLS), lambda b: (b, 0)),
                   pl.BlockSpec((None, VBT_ROWS, past), lambda b: (b, 0, 0))],
        out_shape=[jax.ShapeDtypeStruct((nb * past, HB_COLS), BF16),
                   jax.ShapeDtypeStruct((nb, VBT_ROWS, past), BF16)],
        name="mla_cache_up",
    )(ckv, kpe, wukv, wvbt)


def _attend(q_tiles, k_tiles, vt_tiles, bias_t, sink):
    return _softmax_pv(_scores_t(q_tiles, k_tiles, bias_t), vt_tiles, sink)


def _scores_t(q_tiles, k_tiles, bias_t):
    st = _bdot_nt(jnp.stack(k_tiles), jnp.stack(q_tiles))
    return st if bias_t is None else st + bias_t[None]


def _softmax_pv(st, vt_tiles, sink):
    m = jnp.max(st, axis=1, keepdims=True)
    if sink is not None:
        m = jnp.maximum(m, sink)
    ot = _bdot(jnp.stack(vt_tiles), jnp.exp2(st - m).astype(BF16))
    den = ot[:, V_HD_B:V_HD_B + 1, :]
    if sink is not None:
        den = den + jnp.exp2(sink - m)
    num = ot[:, :V_HD_B, :] / den
    return [jnp.concatenate([num[2 * i], num[2 * i + 1]], axis=0).T for i in range(st.shape[0] // 2)]


def _tile(x, t):
    return x[:, t * LANE:(t + 1) * LANE]


def _attn_a_heads(q, ka, vat, bias_t, sink_ref):
    qs, ks, vs = [], [], []
    for t in range(N_HEADS_A // 2):
        g = (2 * t) // GQA_GROUP
        for e in range(2):
            qs.append(_tile(q, t))
            ks.append(_tile(ka, 2 * g + e))
            vs.append(vat[g * VT_ROWS:(g + 1) * VT_ROWS])
    sink = jnp.stack([sink_ref[:, h:h + 1] * LOG2E for h in range(N_HEADS_A)])
    return _attend(qs, ks, vs, bias_t, sink)


def _gated_store(outs, z_ref, o_ref, first_tile=0):
    for i, o in enumerate(outs):
        t = first_tile + i
        z = _tile(z_ref, t).astype(F32)
        o_ref[:, t * LANE:(t + 1) * LANE] = (o * _silu(z)).astype(BF16)


def _attn_a_ctx_kernel(q_ref, ka_ref, vat_ref, z_ref, sink_ref, o_ref):
    _gated_store(_attn_a_heads(q_ref[...], ka_ref[...], vat_ref[...], None, sink_ref), z_ref, o_ref)


def _attn_a_ctx(qa, ka, vat, z, sink, l, seq):
    t = qa.shape[0]
    row = lambda b: (b, 0)
    return pl.pallas_call(
        _attn_a_ctx_kernel,
        grid=(t // seq,),
        in_specs=[pl.BlockSpec((seq, W_A), row),
                  pl.BlockSpec((seq, KA_COLS), row),
                  pl.BlockSpec((VAT_ROWS, seq), lambda b: (0, b)),
                  pl.BlockSpec((seq, W_A), row),
                  _layer_spec(sink, l)],
        out_specs=pl.BlockSpec((seq, W_A), row),
        out_shape=jax.ShapeDtypeStruct((t, W_A), BF16),
        compiler_params=pltpu.CompilerParams(dimension_semantics=("parallel",)),
        name="attn_a_ctx",
    )(qa, ka, vat, z, sink)


def _attn_a_lat_kernel(q_ref, kp_ref, kc_ref, kn_ref, vp_ref, vc_ref, vn_ref, kx_ref, vx_ref, z_ref, sink_ref,
                       o_ref, *, nq):
    j = pl.program_id(1)
    ka = jnp.concatenate([kp_ref[...], kc_ref[...], kn_ref[...], kx_ref[...]], axis=0)
    vat = jnp.concatenate([vp_ref[...], vc_ref[...], vn_ref[...], vx_ref[...]], axis=1)
    n_loc = TM + 2 * WINDOW
    kj = lax.broadcasted_iota(jnp.int32, (ka.shape[0], TM), 0)
    qi = lax.broadcasted_iota(jnp.int32, (ka.shape[0], TM), 1)
    ok = (kj >= qi) & (kj <= qi + 2 * WINDOW)
    ok = ok & ((kj >= WINDOW) | (j > 0)) & ((kj < TM + WINDOW) | (j < nq - 1))
    bias_t = jnp.where(ok | (kj >= n_loc), 0.0, NEG_INF)
    _gated_store(_attn_a_heads(q_ref[...], ka, vat, bias_t, sink_ref), z_ref, o_ref)


def _attn_a_lat(qa, ka, vat, z, sink, kx, vxt, l, seq):
    t = qa.shape[0]
    nq = seq // TM
    past = kx.shape[2]
    r = TM // WINDOW
    row = lambda b, j: (b * nq + j, 0)
    prev = lambda b, j: ((b * nq + j) * r - jnp.where(j > 0, 1, 0), 0)
    nxt = lambda b, j: ((b * nq + j) * r + jnp.where(j < nq - 1, r, r - 1), 0)
    swap = lambda f: (lambda b, j: f(b, j)[::-1])
    return pl.pallas_call(
        functools.partial(_attn_a_lat_kernel, nq=nq),
        grid=(t // seq, nq),
        in_specs=[pl.BlockSpec((TM, W_A), row),
                  pl.BlockSpec((WINDOW, KA_COLS), prev),
                  pl.BlockSpec((TM, KA_COLS), row),
                  pl.BlockSpec((WINDOW, KA_COLS), nxt),
                  pl.BlockSpec((VAT_ROWS, WINDOW), swap(prev)),
                  pl.BlockSpec((VAT_ROWS, TM), swap(row)),
                  pl.BlockSpec((VAT_ROWS, WINDOW), swap(nxt)),
                  pl.BlockSpec((None, None, past, KA_COLS), lambda b, j: (b, l, 0, 0)),
                  pl.BlockSpec((None, None, VAT_ROWS, past), lambda b, j: (b, l, 0, 0)),
                  pl.BlockSpec((TM, W_A), row),
                  _layer_spec(sink, l)],
        out_specs=pl.BlockSpec((TM, W_A), row),
        out_shape=jax.ShapeDtypeStruct((t, W_A), BF16),
        compiler_params=pltpu.CompilerParams(dimension_semantics=("parallel", "parallel")),
        name="attn_a_lat",
    )(qa, ka, ka, ka, vat, vat, vat, kx, vxt, z, sink)


def _attn_b_kernel(*refs, has_ctx, group):
    if has_ctx:
        q_ref, k_ref, vt_ref, kx_ref, vxt_ref, z_ref, o_ref = refs
        kb = jnp.concatenate([k_ref[...], kx_ref[...]], axis=0)
        vbt = jnp.concatenate([vt_ref[...], vxt_ref[...]], axis=1)
    else:
        q_ref, k_ref, vt_ref, z_ref, o_ref = refs
        kb, vbt = k_ref[...], vt_ref[...]
    q = q_ref[...]

    def scores(h0):
        heads = range(h0, h0 + group)
        return _scores_t([_tile(q, h) for h in heads], [_tile(kb, h) for h in heads], None)

    st = scores(0)
    for h0 in range(0, N_HEADS_B, group):
        st_next = scores(h0 + group) if h0 + group < N_HEADS_B else None
        outs = _softmax_pv(st, [vbt[h * VT_ROWS:(h + 1) * VT_ROWS] for h in range(h0, h0 + group)], None)
        _gated_store(outs, z_ref, o_ref, h0 // 2)
        st = st_next


def _attn_b(qb, kb, vbt, z, seq, qblk, group, kx=None, vxt=None):
    t = qb.shape[0]
    nq = seq // qblk
    has_ctx = kx is not None
    hw = HB_COLS
    in_specs = [pl.BlockSpec((qblk, hw), lambda b, j: (b * nq + j, 0)),
                pl.BlockSpec((seq, hw), lambda b, j: (b, 0)),
                pl.BlockSpec((VBT_ROWS, seq), lambda b, j: (0, b))]
    args = [qb, kb, vbt]
    if has_ctx:
        past = kx.shape[0] // (t // seq)
        in_specs += [pl.BlockSpec((past, hw), lambda b, j: (b, 0)),
                     pl.BlockSpec((None, VBT_ROWS, past), lambda b, j: (b, 0, 0))]
        args += [kx, vxt]
    in_specs.append(pl.BlockSpec((qblk, W_B), lambda b, j: (b * nq + j, 1)))
    args.append(z)
    return pl.pallas_call(
        functools.partial(_attn_b_kernel, has_ctx=has_ctx, group=group),
        grid=(t // seq, nq),
        in_specs=in_specs,
        out_specs=pl.BlockSpec((qblk, W_B), lambda b, j: (b * nq + j, 0)),
        out_shape=jax.ShapeDtypeStruct((t, W_B), BF16),
        compiler_params=pltpu.CompilerParams(dimension_semantics=("parallel", "parallel")),
        name="attn_b_lat" if has_ctx else "attn_b_ctx",
    )(*args)


def _gdn_local_kernel(cq_ref, prev_ref, next_ref, small_ref, abt_ref, cw_ref, prow_ref, pcol_ref,
                      u_ref, w_ref, qg_ref, kd_ref, attn_ref, eg_ref, qkv_scr, gb_scr, *, tiles_per_seq):
    tpos = pl.program_id(0) % tiles_per_seq
    x = cq_ref[...].astype(F32)
    prev_row = jnp.where(tpos > 0, prev_ref[...].astype(F32)[HALO - 1:HALO, :], 0.0)
    next_row = jnp.where(tpos < tiles_per_seq - 1, next_ref[...].astype(F32)[0:1, :], 0.0)
    rows = lax.broadcasted_iota(jnp.int32, (TM, 1), 0)
    xm1 = jnp.where(rows == 0, prev_row, pltpu.roll(x, 1, 0))
    xp1 = jnp.where(rows == TM - 1, next_row, pltpu.roll(x, TM - 1, 0))
    cw = cw_ref[...]
    y = _silu(xm1 * cw[0:1] + x * cw[1:2] + xp1 * cw[2:3])
    nq = N_HEADS_C * DK_C
    for h in range(N_HEADS_C):
        qh = y[:, h * DK_C:(h + 1) * DK_C]
        kh = y[:, nq + h * DK_C:nq + (h + 1) * DK_C]
        qkv_scr[:, h * DK_C:(h + 1) * DK_C] = (
            qh * lax.rsqrt(jnp.sum(qh * qh, axis=-1, keepdims=True) + EPS) * (DK_C ** -0.5))
        qkv_scr[:, nq + h * DK_C:nq + (h + 1) * DK_C] = kh * lax.rsqrt(jnp.sum(kh * kh, axis=-1, keepdims=True) + EPS)
    qkv_scr[:, 2 * nq:] = y[:, 2 * nq:]

    sm = small_ref[...]
    prow = prow_ref[...]
    gb_scr[:, 0:8] = -jnp.exp(prow[0:1]) * _softplus(sm[:, S_A:S_A + 8] + prow[1:2])
    gb_scr[:, 8:16] = _sigmoid(sm[:, S_B:S_B + 8])
    pcol = pcol_ref[...]

    ri = lax.broadcasted_iota(jnp.int32, (CHUNK, CHUNK), 0)
    ci = lax.broadcasted_iota(jnp.int32, (CHUNK, CHUNK), 1)
    tril = (ri >= ci).astype(F32)
    triu = (ri <= ci).astype(F32)
    xor = ri ^ ci
    eye = (ri == ci).astype(F32)

    lows, rhss, order = [], [], []

    def solve():
        low = jnp.stack(lows, axis=0)
        inv = eye[None] - jnp.where(xor[None] == 1, low, 0.0)
        b = 2
        while b < CHUNK:
            cpl = jnp.where((xor[None] >= b) & (xor[None] < 2 * b), low, 0.0)
            tmp = _bdot(cpl.astype(BF16), inv.astype(BF16))
            inv = inv - _bdot(inv.astype(BF16), tmp.astype(BF16))
            b *= 2
        uw = _bdot(inv.astype(BF16), jnp.stack(rhss, axis=0))
        for i, (rs, dh) in enumerate(order):
            cs = slice(dh * DK_C, (dh + 1) * DK_C)
            u_ref[rs, cs] = uw[i, :, :DV_C]
            w_ref[rs, cs] = uw[i, :, DV_C:].astype(BF16)
        lows.clear(), rhss.clear(), order.clear()

    for c in range(CPT):
        if c % SOLVE_CHUNKS == 0 and c > 0:
            solve()
        rs = slice(c * CHUNK, (c + 1) * CHUNK)
        gcol = gb_scr[rs, 0:8]
        bcol = gb_scr[rs, 8:16]
        abt = abt_ref[c]
        grow = -jnp.exp(pcol[:, 0:1]) * _softplus(abt[0:8] + pcol[:, 1:2])
        gc_f = _dot_exact(tril, gcol)
        gc_b = _dot_exact(triu, gcol)
        gr_f = _dot_exact(grow, triu)
        gr_b = _dot_exact(grow, tril)
        for h in range(N_HEADS_C):
            q = qkv_scr[rs, h * DK_C:(h + 1) * DK_C]
            k = qkv_scr[rs, nq + h * DK_C:nq + (h + 1) * DK_C]
            v = qkv_scr[rs, 2 * nq + h * DV_C:2 * nq + (h + 1) * DV_C]
            kb16 = k.astype(BF16)
            kk = _dot_nt(kb16, kb16)
            qk = _dot_nt(q.astype(BF16), kb16)
            for d in range(2):
                dh = d * N_HEADS_C + h
                gc = (gc_f if d == 0 else gc_b)[:, dh:dh + 1]
                gr = (gr_f if d == 0 else gr_b)[dh:dh + 1, :]
                beta = bcol[:, dh:dh + 1]
                incl = (ri >= ci) if d == 0 else (ri <= ci)
                strict = (ri > ci) if d == 0 else (ri < ci)
                decay = jnp.where(incl, jnp.exp(jnp.where(incl, gc - gr, 0.0)), 0.0)
                lows.append(jnp.where(strict, beta * kk * decay, 0.0))
                eg = jnp.exp(gc)
                rhss.append(jnp.concatenate([v * beta, k * (beta * eg)], axis=-1).astype(BF16))
                order.append((rs, dh))
                g_last = gc[CHUNK - 1:CHUNK] if d == 0 else gc[0:1]
                cs = slice(dh * DK_C, (dh + 1) * DK_C)
                qg_ref[rs, cs] = (q * eg).astype(BF16)
                kd_ref[rs, cs] = (k * jnp.exp(g_last - gc)).astype(BF16)
                attn_ref[rs, dh * CHUNK:(dh + 1) * CHUNK] = (qk * decay).astype(BF16)
                eg_ref[c, dh:dh + 1, :] = jnp.broadcast_to(jnp.exp(g_last), (1, LANE))
    solve()


def _gdn_local(cqkv, small, abt, conv_w, prow, pcol, l, tiles_per_seq):
    t = cqkv.shape[0]
    nt = t // TM
    nh8 = t // HALO
    row = lambda i: (i, 0)
    dh = 2 * N_HEADS_C
    return pl.pallas_call(
        functools.partial(_gdn_local_kernel, tiles_per_seq=tiles_per_seq),
        grid=(nt,),
        in_specs=[pl.BlockSpec((TM, QKV_C), row),
                  pl.BlockSpec((HALO, QKV_C), lambda i: (jnp.maximum(i * (TM // HALO) - 1, 0), 0)),
                  pl.BlockSpec((HALO, QKV_C), lambda i: (jnp.minimum((i + 1) * (TM // HALO), nh8 - 1), 0)),
                  pl.BlockSpec((TM, LANE), row),
                  pl.BlockSpec((CPT, 16, CHUNK), lambda i: (i, 0, 0)),
                  _layer_spec(conv_w, l), _layer_spec(prow, l), _layer_spec(pcol, l)],
        out_specs=[pl.BlockSpec((TM, dh * DV_C), row),
                   pl.BlockSpec((TM, dh * DK_C), row),
                   pl.BlockSpec((TM, dh * DK_C), row),
                   pl.BlockSpec((TM, dh * DK_C), row),
                   pl.BlockSpec((TM, dh * CHUNK), row),
                   pl.BlockSpec((CPT, dh, LANE), lambda i: (i, 0, 0))],
        out_shape=[jax.ShapeDtypeStruct((t, dh * DV_C), F32),
                   jax.ShapeDtypeStruct((t, dh * DK_C), BF16),
                   jax.ShapeDtypeStruct((t, dh * DK_C), BF16),
                   jax.ShapeDtypeStruct((t, dh * DK_C), BF16),
                   jax.ShapeDtypeStruct((t, dh * CHUNK), BF16),
                   jax.ShapeDtypeStruct((t // CHUNK, dh, LANE), F32)],
        scratch_shapes=[pltpu.VMEM((TM, QKV_C), F32), pltpu.VMEM((TM, 16), F32)],
        compiler_params=pltpu.CompilerParams(dimension_semantics=("parallel",)),
        name="gdn_local",
    )(cqkv, cqkv, cqkv, small, abt, conv_w, prow, pcol)


def _gdn_scan_kernel(*refs, nt, ns, has_init, want_state):
    it = iter(refs)
    ins = [[next(it) for _ in range(6)] for _ in range(2)]
    s0_ref = next(it) if has_init else None
    o_refs = [next(it), next(it)]
    st_ref = next(it) if want_state else None
    s_scr = next(it)
    j = pl.program_id(1)
    nh = N_HEADS_C
    nst = ns * 2 * nh

    @pl.when(j == 0)
    def _():
        if has_init:
            s_scr[...] = s0_ref[...].reshape(nst, DK_C, DV_C)
        else:
            s_scr[...] = jnp.zeros_like(s_scr)

    for step in range(CPT):
        chunk = lambda d: step if d == 0 else CPT - 1 - step

        def gather(idx, width):
            return jnp.stack([ins[d][idx][s, chunk(d) * CHUNK:(chunk(d) + 1) * CHUNK, h * width:(h + 1) * width]
                              for s in range(ns) for d in range(2) for h in range(nh)])

        u, w, qg, kd, attn = gather(0, DV_C), gather(1, DK_C), gather(2, DK_C), gather(3, DK_C), gather(4, CHUNK)
        eg = jnp.stack([ins[d][5][s, chunk(d), d * nh + h:d * nh + h + 1, :]
                        for s in range(ns) for d in range(2) for h in range(nh)])
        st = s_scr[...]
        sb = st.astype(BF16)
        v_new = u - _bdot(w, sb)
        vb = v_new.astype(BF16)
        o = _bdot(qg, sb) + _bdot(attn, vb)
        s_scr[...] = st * eg + _bdot_tn(kd, vb)
        for s in range(ns):
            for d in range(2):
                for h in range(nh):
                    o_refs[d][s, chunk(d) * CHUNK:(chunk(d) + 1) * CHUNK, h * DV_C:(h + 1) * DV_C] = (
                        o[(s * 2 + d) * nh + h])

    if want_state:
        @pl.when(j == nt - 1)
        def _():
            st_ref[...] = s_scr[...].reshape(ns, 2, nh, DK_C, DV_C)


def _gdn_scan(u, w, qg, kd, attn, eg, s0, l, seq, want_state):
    t = u.shape[0]
    nt = seq // TM
    nb = t // seq
    ns = 2 if nb % 2 == 0 else 1
    half = N_HEADS_C * DK_C
    has_init = s0 is not None
    by_seq = lambda a: a.reshape((nb, a.shape[0] // nb) + a.shape[1:])
    in_specs, args = [], []
    for d in range(2):
        tile = (lambda b, j: j) if d == 0 else (lambda b, j: nt - 1 - j)
        row = lambda b, j, d=d, tile=tile: (b, tile(b, j), d)
        row4 = lambda b, j, tile=tile: (b, tile(b, j), 0, 0)
        in_specs += [pl.BlockSpec((ns, TM, half), row)] * 4
        in_specs += [pl.BlockSpec((ns, TM, N_HEADS_C * CHUNK), row),
                     pl.BlockSpec((ns, CPT, 2 * N_HEADS_C, LANE), row4)]
        args += [by_seq(a) for a in (u, w, qg, kd, attn, eg)]
    st_tail = (2, N_HEADS_C, DK_C, DV_C)
    if has_init:
        in_specs.append(pl.BlockSpec((ns, None) + st_tail, lambda b, j: (b, l, 0, 0, 0, 0)))
        args.append(s0)
    out_specs = [pl.BlockSpec((ns, TM, half), lambda b, j: (b, j, 0)),
                 pl.BlockSpec((ns, TM, half), lambda b, j: (b, nt - 1 - j, 0))]
    out_shape = [jax.ShapeDtypeStruct((nb, seq, half), F32)] * 2
    if want_state:
        out_specs.append(pl.BlockSpec((ns,) + st_tail, lambda b, j: (b, 0, 0, 0, 0)))
        out_shape.append(jax.ShapeDtypeStruct((nb,) + st_tail, F32))
    outs = pl.pallas_call(
        functools.partial(_gdn_scan_kernel, nt=nt, ns=ns, has_init=has_init, want_state=want_state),
        grid=(nb // ns, nt),
        in_specs=in_specs,
        out_specs=out_specs,
        out_shape=out_shape,
        scratch_shapes=[pltpu.VMEM((ns * 2 * N_HEADS_C, DK_C, DV_C), F32)],
        compiler_params=pltpu.CompilerParams(dimension_semantics=("parallel", "arbitrary")),
        name="gdn_scan",
    )(*args)
    return [outs[0].reshape(t, half), outs[1].reshape(t, half)] + list(outs[2:])


def _merge_kernel(x_ref, mod_ref, oa_ref, ob_ref, cf_ref, cb_ref, zc_ref, gates_ref, gn_ref,
                  wa_ref, wb_ref, wc_ref, wo_ref, fg_ref, o_ref, *, last):
    oc = cf_ref[...] + cb_ref[...]
    zc = zc_ref[...].astype(F32)
    gn = gn_ref[...]
    parts = []
    for h in range(N_HEADS_C):
        hs = slice(h * DV_C, (h + 1) * DV_C)
        och = oc[:, hs]
        och = och * lax.rsqrt(jnp.mean(och * och, axis=-1, keepdims=True) + EPS) * gn
        parts.append((och * _silu(zc[:, hs])).astype(BF16))
    ocz = jnp.concatenate(parts, axis=-1)
    pa = _dot(oa_ref[...], wa_ref[...])
    pb = _dot(ob_ref[...], wb_ref[...])
    pc = _dot(ocz, wc_ref[...])
    ga = _sigmoid(gates_ref[:, 0:D_MODEL].astype(F32))
    gb = _sigmoid(gates_ref[:, D_MODEL:2 * D_MODEL].astype(F32))
    gc = _sigmoid(gates_ref[:, 2 * D_MODEL:].astype(F32))
    y = _dot((ga * pa + gb * pb + gc * pc).astype(BF16), wo_ref[...])
    gate = mod_ref[0][:, 2 * D_MODEL:]
    xo = x_ref[...] + gate * y
    if last:
        xo = xo * lax.rsqrt(jnp.mean(xo * xo, axis=-1, keepdims=True) + EPS) * fg_ref[...]
    o_ref[...] = xo


def _merge(x2d, l, mod, mod_row_fn, oa, ob, cf, cb, z, gates, gn, wa, wb, wc, wo, fg, last):
    t = x2d.shape[0]
    row = lambda i: (i, 0)
    return pl.pallas_call(
        functools.partial(_merge_kernel, last=last),
        grid=(t // TMD,),
        in_specs=[pl.BlockSpec((TMD, D_MODEL), row),
                  pl.BlockSpec((1, 1, 3 * D_MODEL), lambda i: (mod_row_fn(i), 0, 0)),
                  pl.BlockSpec((TMD, W_A), row),
                  pl.BlockSpec((TMD, W_B), row),
                  pl.BlockSpec((TMD, W_C), row),
                  pl.BlockSpec((TMD, W_C), row),
                  pl.BlockSpec((TMD, W_C), lambda i: (i, 2)),
                  pl.BlockSpec((TMD, 3 * D_MODEL), row),
                  _layer_spec(gn, l), _layer_spec(wa, l), _layer_spec(wb, l), _layer_spec(wc, l),
                  _layer_spec(wo, l),
                  pl.BlockSpec((1, D_MODEL), lambda i: (0, 0))],
        out_specs=pl.BlockSpec((TMD, D_MODEL), row),
        out_shape=jax.ShapeDtypeStruct((t, D_MODEL), F32),
        compiler_params=pltpu.CompilerParams(dimension_semantics=("parallel",)),
        name="merge",
    )(x2d, mod, oa, ob, cf, cb, z, gates, gn, wa, wb, wc, wo, fg)


def _rope_tables(n_tokens, rot_dim):
    rows = n_tokens // GRID_W
    row = np.repeat(np.arange(rows), GRID_W).astype(np.float32)
    col = np.tile(np.arange(GRID_W), rows).astype(np.float32)
    n_pairs = rot_dim // 4
    inv = (np.float32(ROPE_BASE) ** (-np.arange(n_pairs, dtype=np.float32) / np.float32(n_pairs))).astype(np.float32)
    ang = np.concatenate([row[:, None] * inv, col[:, None] * inv], axis=-1)
    c, s = np.cos(ang), np.sin(ang)
    return np.repeat(c, 2, axis=-1), np.stack([-s, s], axis=-1).reshape(n_tokens, rot_dim)


def _swap_halves(x):
    return jnp.concatenate([x[..., HD_A:], x[..., :HD_A]], axis=-1)


def _in_offsets():
    o = [0]
    for n in IN_SIZES:
        o.append(o[-1] + n)
    return o


def _relayout_moves():
    o = _in_offsets()
    order = [(o[0], W_A), (o[1], LANE), (o[1] + HD_A, HD_A), (o[1], HD_A), (o[2], LANE),
             (o[3], W_A), (o[7], W_B), (o[11], W_C), (o[12], 3 * D_MODEL), (o[4], Q_RANK_B), (o[5], KV_RANK_B),
             (None, S_KPE), (o[6], QK_ROPE_B), (o[9], 4 * N_HEADS_C), (None, LANE - S_B - 8), (o[8], QKV_C)]
    moves, dst = [], 0
    for src, n in order:
        moves.append((src, dst, n))
        dst += n
    assert dst == P_END
    return moves


def _relayout_kernel(w_ref, o_ref):
    for src, dst, n in _relayout_moves():
        if src is None:
            o_ref[:, dst:dst + n] = jnp.zeros((o_ref.shape[0], n), BF16)
        else:
            o_ref[:, dst:dst + n] = w_ref[:, src:src + n].astype(BF16)


def _relayout_w_in(w_in):
    depth, _, width = w_in.shape
    rows = 128
    return pl.pallas_call(
        _relayout_kernel,
        grid=(depth, D_MODEL // rows),
        in_specs=[pl.BlockSpec((None, rows, width), lambda l, i: (l, i, 0))],
        out_specs=pl.BlockSpec((None, rows, P_END), lambda l, i: (l, i, 0)),
        out_shape=jax.ShapeDtypeStruct((depth, D_MODEL, P_END), BF16),
        name="w_in_relayout",
    )(w_in)


def _prep_weights(w_in, w_uq, w_ukv):
    depth = w_in.shape[0]
    o = _in_offsets()
    seg = lambda i: w_in[..., o[i]:o[i + 1]]
    wp = _relayout_w_in(w_in)
    wab = jnp.swapaxes(w_in[..., o[9]:o[11]], 1, 2).astype(BF16)
    wvat = jnp.swapaxes(seg(2), 1, 2).astype(BF16)
    hd = QK_NOPE_B + QK_ROPE_B
    wuq = jnp.pad(w_uq.reshape(depth, Q_RANK_B, N_HEADS_B, hd), ((0, 0), (0, 0), (0, 0), (0, MLA_HW - hd)))
    wuq = wuq.reshape(depth, Q_RANK_B, HB_COLS).astype(BF16)
    kv = w_ukv.reshape(depth, KV_RANK_B, N_HEADS_B, QK_NOPE_B + V_HD_B)
    wk = jnp.pad(kv[..., :QK_NOPE_B], ((0, 0), (0, 0), (0, 0), (0, MLA_HW - QK_NOPE_B)))
    wukv = wk.reshape(depth, KV_RANK_B, HB_COLS).astype(BF16)
    wvbt = jnp.swapaxes(kv[..., QK_NOPE_B:].reshape(depth, KV_RANK_B, W_B), 1, 2).astype(BF16)
    return wp, wab, wvat, wuq, wukv, wvbt


def _cache_tiles_a(kx, vx):
    k0, k1 = kx[..., 0, :], kx[..., 1, :]
    z = jnp.zeros_like(k0)
    ka = jnp.concatenate([k0, z, z, k0, k1, z, z, k1], axis=-1).astype(BF16)
    vt = jnp.transpose(vx, (0, 1, 3, 4, 2))
    vt = jnp.concatenate([vt, jnp.ones(vt.shape[:3] + (VT_ONES, vt.shape[4]), vt.dtype)], axis=3)
    return ka, vt.reshape(vt.shape[:2] + (VAT_ROWS, vt.shape[4])).astype(BF16)


def kernel(x_prompt, x_sample, cache_attn_k, cache_attn_v, cache_mla_ckv, cache_mla_kpe, state_gdn, c, c_ctx,
           norm_g, w_ada, b_ada, w_in, attn_sink, mla_q_norm, mla_w_uq, mla_kv_norm, mla_w_ukv, gdn_conv,
           gdn_a_log, gdn_dt_bias, gdn_norm, w_branch_a, w_branch_b, w_branch_c, w_out, final_norm_g):
    depth = w_in.shape[0]
    nb_c, seq_c, _ = x_prompt.shape
    nb_l, seq_l, _ = x_sample.shape
    past = cache_attn_k.shape[2]
    assert P_END % LANE == 0 and seq_c % TM == 0 and seq_l % TMD == 0 and nb_l < 8 and TM == 2 * WINDOW
    assert (nb_c * seq_c) % TMD == 0 and TMD % TM == 0

    cond8 = jnp.zeros((8, D_MODEL), F32).at[:nb_l].set(c).at[nb_l].set(c_ctx)
    mod = _modulation(cond8, w_ada, b_ada).reshape(depth * 8, 1, 3 * D_MODEL)

    c_a, s_a = _rope_tables(seq_l, HD_A)
    c_b, s_b = _rope_tables(seq_l, QK_ROPE_B)
    pad_l, pad_r = S_KPE, LANE - S_KPE - QK_ROPE_B
    one, zero = np.ones((seq_l, 1), np.float32), np.zeros((seq_l, 1), np.float32)
    rope_tabs = tuple(jnp.asarray(a) for a in (
        np.tile(c_a, (1, LANE // HD_A)), np.tile(s_a, (1, LANE // HD_A)),
        np.concatenate([np.tile(one, (1, pad_l)), c_b, np.tile(one, (1, pad_r))], 1),
        np.concatenate([np.tile(zero, (1, pad_l)), s_b, np.tile(zero, (1, pad_r))], 1)))

    weights = _prep_weights(w_in, mla_w_uq, mla_w_ukv)
    wukv, wvbt = weights[4], weights[5]
    ng = norm_g.reshape(depth, 1, D_MODEL)
    qn = mla_q_norm.reshape(depth, 1, Q_RANK_B)
    kvn = mla_kv_norm.reshape(depth, 1, KV_RANK_B)
    sink = attn_sink.reshape(depth, 1, N_HEADS_A)
    prow = jnp.stack([gdn_a_log.reshape(depth, -1), gdn_dt_bias.reshape(depth, -1)], axis=1)
    pcol = jnp.swapaxes(prow, 1, 2)
    gn = gdn_norm.reshape(depth, 1, DV_C)
    wa, wb, wc, wo = (w.astype(BF16) for w in (w_branch_a, w_branch_b, w_branch_c, w_out))
    fg = final_norm_g.reshape(1, D_MODEL)
    kxa, vxa = _cache_tiles_a(cache_attn_k, cache_attn_v)
    kpex = jnp.pad(cache_mla_kpe, ((0, 0), (0, 0), (0, 0), (pad_l, pad_r)))

    tps_c, tps_l = seq_c // TM, seq_l // TM
    tpd_l = seq_l // TMD
    y_p = x_prompt.reshape(nb_c * seq_c, D_MODEL)
    y_s = x_sample.reshape(nb_l * seq_l, D_MODEL)
    ks, vs, ckvs, kpes, sts = [], [], [], [], []
    for l in range(depth):
        last = l == depth - 1

        mod_row_c = lambda i, l=l: l * 8 + nb_l
        (qa, ka, vat, z, gates, qb, kb, vbt, small, cqkv, abt, kva, ckvn) = _inproj(
            y_p, l, mod, mod_row_c, ng, weights, qn, kvn, None, tps_c, True)
        oa = _attn_a_ctx(qa, ka, vat, z, sink, l, seq_c)
        ob = _attn_b(qb, kb, vbt, z, seq_c, seq_c, N_HEADS_B)
        u, w, qg, kd, attn, eg = _gdn_local(cqkv, small, abt, gdn_conv, prow, pcol, l, tps_c)
        cf, cb, st = _gdn_scan(u, w, qg, kd, attn, eg, None, l, seq_c, True)
        y_p = _merge(y_p, l, mod, mod_row_c, oa, ob, cf, cb, z, gates, gn, wa, wb, wc, wo, fg, last)
        ks.append(kva[:, :N_KV_A * HD_A].reshape(nb_c, seq_c, N_KV_A, HD_A))
        vs.append(kva[:, N_KV_A * HD_A:].reshape(nb_c, seq_c, N_KV_A, HD_A))
        ckvs.append(ckvn.reshape(nb_c, seq_c, KV_RANK_B))
        kpes.append(small[:, S_KPE:S_KPE + QK_ROPE_B].reshape(nb_c, seq_c, QK_ROPE_B))
        sts.append(st)

        mod_row_l = lambda i, l=l: l * 8 + i // tpd_l
        (qa, ka, vat, z, gates, qb, kb, vbt, small, cqkv, abt) = _inproj(
            y_s, l, mod, mod_row_l, ng, weights, qn, kvn, rope_tabs, tpd_l, False)
        oa = _attn_a_lat(qa, ka, vat, z, sink, kxa, vxa, l, seq_l)
        kxb, vxb = _kvup(cache_mla_ckv, kpex, wukv, wvbt, l)
        ob = _attn_b(qb, kb, vbt, z, seq_l, TM, 4, kxb, vxb)
        u, w, qg, kd, attn, eg = _gdn_local(cqkv, small, abt, gdn_conv, prow, pcol, l, tps_l)
        cf, cb = _gdn_scan(u, w, qg, kd, attn, eg, state_gdn, l, seq_l, False)
        y_s = _merge(y_s, l, mod, mod_row_l, oa, ob, cf, cb, z, gates, gn, wa, wb, wc, wo, fg, last)

    return (y_p.reshape(nb_c, seq_c, D_MODEL), y_s.reshape(nb_l, seq_l, D_MODEL),
            jnp.stack(ks, axis=1), jnp.stack(vs, axis=1), jnp.stack(ckvs, axis=1), jnp.stack(kpes, axis=1),
            jnp.stack(sts, axis=1))
```

```python
import functools

import numpy as np
import jax
import jax.numpy as jnp
from jax import lax
from jax.experimental import pallas as pl
from jax.experimental.pallas import tpu as pltpu

F32 = jnp.float32
BF16 = jnp.bfloat16

D_MODEL = 1024
GRID_W = 64
ROPE_BASE = 10000.0
EPS = 1e-6
NEG_INF = -1e30
N_HEADS_A = 8
N_KV_A = 2
HD_A = 64
GQA_GROUP = N_HEADS_A // N_KV_A
WINDOW = 128
N_HEADS_B = 8
QK_NOPE_B = 64
QK_ROPE_B = 32
V_HD_B = 64
Q_RANK_B = 384
KV_RANK_B = 256
MLA_SCALE = (QK_NOPE_B + QK_ROPE_B) ** -0.5
N_HEADS_C = 4
DK_C = 128
DV_C = 128
CHUNK = 64
W_A = N_HEADS_A * HD_A
W_B = N_HEADS_B * V_HD_B
W_C = N_HEADS_C * DV_C
QKV_C = 2 * N_HEADS_C * DK_C + W_C
IN_SIZES = (W_A, N_KV_A * HD_A, N_KV_A * HD_A, W_A, Q_RANK_B, KV_RANK_B, QK_ROPE_B, W_B, QKV_C,
            2 * N_HEADS_C, 2 * N_HEADS_C, W_C, 3 * D_MODEL)

LANE = 128
HALF = LANE // 2
TM = 256
TMD = 512
CPT = TM // CHUNK
SOLVE_CHUNKS = 2
HALO = 16
MLA_HW = 128
KA_COLS = 4 * LANE
HB_COLS = N_HEADS_B * MLA_HW
VT_ONES = 16
VT_ROWS = V_HD_B + VT_ONES
VAT_ROWS = N_KV_A * VT_ROWS
VBT_ROWS = N_HEADS_B * VT_ROWS
LOG2E = 1.4426950408889634

P_QKV = 0
A_COLS = W_A + 3 * LANE
P_Z = P_QKV + A_COLS
P_GATES = P_Z + 1536
P_CQ = P_GATES + 3 * D_MODEL
P_CKV = P_CQ + Q_RANK_B
P_SMALL = P_CKV + KV_RANK_B
P_CQKV = P_SMALL + LANE
P_END = P_CQKV + QKV_C
S_KPE = 64
S_A = 96
S_B = 104


def _sigmoid(x):
    return 0.5 * jnp.tanh(0.5 * x) + 0.5


def _silu(x):
    return x * _sigmoid(x)


def _softplus(x):
    return jnp.maximum(x, 0.0) + jnp.log(1.0 + jnp.exp(-jnp.abs(x)))


def _dot(a, b):
    return jnp.dot(a, b, preferred_element_type=F32)


def _dot_nt(a, b):
    return lax.dot_general(a, b, (((1,), (1,)), ((), ())), preferred_element_type=F32)


def _bdot(a, b):
    return lax.dot_general(a, b, (((2,), (1,)), ((0,), (0,))), preferred_element_type=F32)


def _bdot_nt(a, b):
    return lax.dot_general(a, b, (((2,), (2,)), ((0,), (0,))), preferred_element_type=F32)


def _bdot_tn(a, b):
    return lax.dot_general(a, b, (((1,), (1,)), ((0,), (0,))), preferred_element_type=F32)


def _dot_exact(a, b):
    return jnp.dot(a, b, preferred_element_type=F32, precision=lax.Precision.HIGHEST)


def _rope(x, c, s):
    n = x.shape[-1]
    lane = lax.broadcasted_iota(jnp.int32, x.shape, 1)
    swapped = jnp.where(lane % 2 == 0, pltpu.roll(x, n - 1, 1), pltpu.roll(x, 1, 1))
    return x * c + swapped * s


def _mod_kernel(cond_ref, w_ref, b_ref, out_ref):
    cnd = cond_ref[...]
    out_ref[0] = _dot(_silu(cnd).astype(BF16), w_ref[0].astype(BF16)) + b_ref[0]


def _modulation(cond8, w_ada, b_ada):
    depth = w_ada.shape[0]
    tn = 768
    return pl.pallas_call(
        _mod_kernel,
        grid=(depth, 3 * D_MODEL // tn),
        in_specs=[pl.BlockSpec((8, D_MODEL), lambda l, n: (0, 0)),
                  pl.BlockSpec((1, D_MODEL, tn), lambda l, n: (l, 0, n)),
                  pl.BlockSpec((1, 1, tn), lambda l, n: (l, 0, n))],
        out_specs=pl.BlockSpec((1, 8, tn), lambda l, n: (l, 0, n)),
        out_shape=jax.ShapeDtypeStruct((depth, 8, 3 * D_MODEL), F32),
        name="adaln_mod",
    )(cond8, w_ada, b_ada.reshape(depth, 1, 3 * D_MODEL))


def _inproj_kernel(*refs, rope, cache_seq, n_carry):
    it = iter(refs)
    (x_ref, mod_ref, ng_ref, wp_ref, wab_ref, wvat_ref, wuq_ref, wukv_ref, wvbt_ref, qn_ref,
     kvn_ref) = (next(it) for _ in range(11))
    if rope:
        ca_ref, sa_ref, cb_ref, sb_ref = (next(it) for _ in range(4))
    for _ in range(n_carry):
        next(it)
    (qa_ref, ka_ref, vat_ref, z_ref, gates_ref, qb_ref, kb_ref, vbt_ref, small_ref, cqkv_ref,
     abt_ref) = (next(it) for _ in range(11))
    if cache_seq:
        ck_ref, cv_ref, cckv_ref, ckpe_ref = (next(it) for _ in range(4))

    def to_cache(ref, val):
        for s in range(TMD // cache_seq):
            ref[s] = val[s * cache_seq:(s + 1) * cache_seq]

    x = x_ref[...]
    mod = mod_ref[0]
    shift, scale = mod[:, :D_MODEL], mod[:, D_MODEL:2 * D_MODEL]
    xn = x * lax.rsqrt(jnp.mean(x * x, axis=-1, keepdims=True) + EPS) * ng_ref[...]
    hb = (xn * (1.0 + scale) + shift).astype(BF16)
    lane = lax.broadcasted_iota(jnp.int32, (TMD, LANE), 1)
    lo = lane < HALF

    def mm(lo_col, hi_col):
        return _dot_nt(hb, wp_ref[lo_col:hi_col, :])

    r = mm(P_QKV, P_QKV + A_COLS)
    tiles = [r[:, t * LANE:(t + 1) * LANE] for t in range(A_COLS // LANE)]
    if cache_seq:
        to_cache(ck_ref, tiles[4])
        to_cache(cv_ref, tiles[6])
    if rope:
        ca, sa = ca_ref[...], sa_ref[...]
        tiles[:6] = [_rope(t, ca, sa) for t in tiles[:6]]
    for t in range(4):
        qa_ref[:, t * LANE:(t + 1) * LANE] = (tiles[t] * (HD_A ** -0.5 * LOG2E)).astype(BF16)
    k01, k10 = tiles[4], tiles[5]
    ka_ref[:, 0 * LANE:1 * LANE] = jnp.where(lo, k01, 0.0).astype(BF16)
    ka_ref[:, 1 * LANE:2 * LANE] = jnp.where(lo, 0.0, k10).astype(BF16)
    ka_ref[:, 2 * LANE:3 * LANE] = jnp.where(lo, k10, 0.0).astype(BF16)
    ka_ref[:, 3 * LANE:4 * LANE] = jnp.where(lo, 0.0, k01).astype(BF16)
    ones = jnp.ones((VT_ONES, TMD), BF16)
    vt = _dot_nt(wvat_ref[...], hb)
    for g in range(N_KV_A):
        vat_ref[g * VT_ROWS:g * VT_ROWS + HD_A] = vt[g * HD_A:(g + 1) * HD_A].astype(BF16)
        vat_ref[g * VT_ROWS + HD_A:(g + 1) * VT_ROWS] = ones

    for t in range(3):
        z_ref[:, t * 512:(t + 1) * 512] = mm(P_Z + t * 512, P_Z + (t + 1) * 512).astype(BF16)
    for t in range(6):
        gates_ref[:, t * 512:(t + 1) * 512] = mm(P_GATES + t * 512, P_GATES + (t + 1) * 512).astype(BF16)

    r = mm(P_CQ, P_CQ + Q_RANK_B)
    qn = r * lax.rsqrt(jnp.mean(r * r, axis=-1, keepdims=True) + EPS) * qn_ref[...]
    q = _dot(qn.astype(BF16), wuq_ref[...])
    if rope:
        cb, sb = cb_ref[...], sb_ref[...]
        for h in range(N_HEADS_B):
            seg = _rope(q[:, h * MLA_HW:(h + 1) * MLA_HW], cb, sb) * (MLA_SCALE * LOG2E)
            qb_ref[:, h * MLA_HW:(h + 1) * MLA_HW] = seg.astype(BF16)
    else:
        qb_ref[...] = (q * (MLA_SCALE * LOG2E)).astype(BF16)

    r = mm(P_SMALL, P_SMALL + LANE)
    small_ref[...] = r
    if cache_seq:
        to_cache(ckpe_ref, r[:, S_KPE:S_KPE + QK_ROPE_B])
    kp = _rope(r, cb, sb) if rope else r
    kp = jnp.where((lane >= S_KPE) & (lane < S_KPE + QK_ROPE_B), kp, 0.0)

    r = mm(P_CKV, P_CKV + KV_RANK_B)
    cn = r * lax.rsqrt(jnp.mean(r * r, axis=-1, keepdims=True) + EPS) * kvn_ref[...]
    if cache_seq:
        to_cache(cckv_ref, cn)
    cn16 = cn.astype(BF16)
    kv = _dot(cn16, wukv_ref[...])
    vt = _dot_nt(wvbt_ref[...], cn16)
    for h in range(N_HEADS_B):
        kb_ref[:, h * MLA_HW:(h + 1) * MLA_HW] = (kv[:, h * MLA_HW:(h + 1) * MLA_HW] + kp).astype(BF16)
        vbt_ref[h * VT_ROWS:h * VT_ROWS + V_HD_B] = vt[h * V_HD_B:(h + 1) * V_HD_B].astype(BF16)
        vbt_ref[h * VT_ROWS + V_HD_B:(h + 1) * VT_ROWS] = ones

    for t in range(3):
        cqkv_ref[:, t * 512:(t + 1) * 512] = mm(P_CQKV + t * 512, P_CQKV + (t + 1) * 512).astype(BF16)

    for c in range(TMD // CHUNK):
        abt_ref[c] = _dot_nt(wab_ref[...], hb[c * CHUNK:(c + 1) * CHUNK])


def _layer_spec(arr, l):
    nd = arr.ndim - 1
    return pl.BlockSpec((None,) + arr.shape[1:], lambda *_: (l,) + (0,) * nd, pipeline_mode=pl.Buffered(1))


def _inproj(x2d, l, mod, mod_row_fn, ng, weights, qn, kvn, rope_tabs, tiles_per_seq, cache=None):
    t = x2d.shape[0]
    nt = t // TMD
    rope = rope_tabs is not None
    row = lambda i: (i, 0)
    col = lambda i: (0, i)
    wp, wab, wvat, wuq, wukv, wvbt = weights
    params = (ng, wp, wab, wvat, wuq, wukv, wvbt, qn, kvn)
    in_specs = [pl.BlockSpec((TMD, D_MODEL), row),
                pl.BlockSpec((1, 1, 3 * D_MODEL), lambda i: (mod_row_fn(i), 0, 0))]
    in_specs += [_layer_spec(a, l) for a in params]
    args = [x2d, mod, *params]
    if rope:
        pos = lambda i: (i % tiles_per_seq, 0)
        in_specs += [pl.BlockSpec((TMD, LANE), pos)] * 4
        args += list(rope_tabs)
    outs = [(W_A, BF16, False), (KA_COLS, BF16, False), (VAT_ROWS, BF16, True), (1536, BF16, False),
            (3 * D_MODEL, BF16, False), (HB_COLS, BF16, False), (HB_COLS, BF16, False), (VBT_ROWS, BF16, True),
            (LANE, F32, False), (QKV_C, BF16, False)]
    out_shape = [jax.ShapeDtypeStruct((w, t) if tr else (t, w), dt) for w, dt, tr in outs]
    out_specs = [pl.BlockSpec((w, TMD), col) if tr else pl.BlockSpec((TMD, w), row) for w, _, tr in outs]
    out_shape.append(jax.ShapeDtypeStruct((t // CHUNK, 16, CHUNK), F32))
    out_specs.append(pl.BlockSpec((TMD // CHUNK, 16, CHUNK), lambda i: (i, 0, 0)))
    aliases, cache_seq, n_carry = {}, 0, 0
    if cache is not None:
        depth, cache_seq, carry = cache
        spt = TMD // cache_seq
        for w in (N_KV_A * HD_A, N_KV_A * HD_A, KV_RANK_B, QK_ROPE_B):
            out_shape.append(jax.ShapeDtypeStruct((t // cache_seq, depth, cache_seq, w), F32))
            out_specs.append(pl.BlockSpec((spt, None, cache_seq, w), lambda i: (i, l, 0, 0)))
        if carry is not None:
            n_carry = len(carry)
            for k, a in enumerate(carry):
                aliases[len(args)] = len(out_shape) - n_carry + k
                in_specs.append(pl.BlockSpec(memory_space=pl.ANY))
                args.append(a)
    return pl.pallas_call(
        functools.partial(_inproj_kernel, rope=rope, cache_seq=cache_seq, n_carry=n_carry),
        grid=(nt,),
        in_specs=in_specs,
        out_specs=out_specs,
        out_shape=out_shape,
        input_output_aliases=aliases,
        compiler_params=pltpu.CompilerParams(dimension_semantics=("parallel",)),
        name="inproj_ctx" if cache is not None else "inproj_lat",
    )(*args)


def _kvup_kernel(c_ref, kpe_ref, w_ref, wvt_ref, k_ref, vt_ref):
    c16 = c_ref[...].astype(BF16)
    kv = _dot(c16, w_ref[...])
    vt = _dot_nt(wvt_ref[...], c16)
    kp = kpe_ref[...]
    ones = jnp.ones((VT_ONES, c16.shape[0]), BF16)
    for h in range(N_HEADS_B):
        k_ref[:, h * MLA_HW:(h + 1) * MLA_HW] = (kv[:, h * MLA_HW:(h + 1) * MLA_HW] + kp).astype(BF16)
        vt_ref[h * VT_ROWS:h * VT_ROWS + V_HD_B] = vt[h * V_HD_B:(h + 1) * V_HD_B].astype(BF16)
        vt_ref[h * VT_ROWS + V_HD_B:(h + 1) * VT_ROWS] = ones


def _kvup(ckv, kpe, wukv, wvbt, l):
    nb, _, past, _ = ckv.shape
    return pl.pallas_call(
        _kvup_kernel,
        grid=(nb,),
        in_specs=[pl.BlockSpec((None, None, past, KV_RANK_B), lambda b: (b, l, 0, 0)),
                  pl.BlockSpec((None, None, past, LANE), lambda b: (b, l, 0, 0)),
                  _layer_spec(wukv, l), _layer_spec(wvbt, l)],
        out_specs=[pl.BlockSpec((past, HB_COLS), lambda b: (b, 0)),
                   pl.BlockSpec((None, VBT_ROWS, past), lambda b: (b, 0, 0))],
        out_shape=[jax.ShapeDtypeStruct((nb * past, HB_COLS), BF16),
                   jax.ShapeDtypeStruct((nb, VBT_ROWS, past), BF16)],
        name="mla_cache_up",
    )(ckv, kpe, wukv, wvbt)


def _attend(q_tiles, k_tiles, vt_tiles, bias_t, sink):
    return _softmax_pv(_scores_t(q_tiles, k_tiles, bias_t), vt_tiles, sink)


def _scores_t(q_tiles, k_tiles, bias_t):
    st = _bdot_nt(jnp.stack(k_tiles), jnp.stack(q_tiles))
    return st if bias_t is None else st + bias_t[None]


def _softmax_pv(st, vt_tiles, sink):
    m = jnp.max(st, axis=1, keepdims=True)
    if sink is not None:
        m = jnp.maximum(m, sink)
    ot = _bdot(jnp.stack(vt_tiles), jnp.exp2(st - m).astype(BF16))
    den = ot[:, V_HD_B:V_HD_B + 1, :]
    if sink is not None:
        den = den + jnp.exp2(sink - m)
    num = ot[:, :V_HD_B, :] / den
    return [jnp.concatenate([num[2 * i], num[2 * i + 1]], axis=0).T for i in range(st.shape[0] // 2)]


def _tile(x, t):
    return x[:, t * LANE:(t + 1) * LANE]


def _attn_a_heads(q, ka, vat, bias_t, sink_ref):
    qs, ks, vs = [], [], []
    for t in range(N_HEADS_A // 2):
        g = (2 * t) // GQA_GROUP
        for e in range(2):
            qs.append(_tile(q, t))
            ks.append(_tile(ka, 2 * g + e))
            vs.append(vat[g * VT_ROWS:(g + 1) * VT_ROWS])
    sink = jnp.stack([sink_ref[:, h:h + 1] * LOG2E for h in range(N_HEADS_A)])
    return _attend(qs, ks, vs, bias_t, sink)


def _gated_store(outs, z_ref, o_ref, first_tile=0):
    for i, o in enumerate(outs):
        t = first_tile + i
        z = _tile(z_ref, t).astype(F32)
        o_ref[:, t * LANE:(t + 1) * LANE] = (o * _silu(z)).astype(BF16)


def _attn_a_ctx_kernel(q_ref, ka_ref, vat_ref, z_ref, sink_ref, o_ref):
    _gated_store(_attn_a_heads(q_ref[...], ka_ref[...], vat_ref[...], None, sink_ref), z_ref, o_ref)


def _attn_a_ctx(qa, ka, vat, z, sink, l, seq):
    t = qa.shape[0]
    row = lambda b: (b, 0)
    return pl.pallas_call(
        _attn_a_ctx_kernel,
        grid=(t // seq,),
        in_specs=[pl.BlockSpec((seq, W_A), row),
                  pl.BlockSpec((seq, KA_COLS), row),
                  pl.BlockSpec((VAT_ROWS, seq), lambda b: (0, b)),
                  pl.BlockSpec((seq, W_A), row),
                  _layer_spec(sink, l)],
        out_specs=pl.BlockSpec((seq, W_A), row),
        out_shape=jax.ShapeDtypeStruct((t, W_A), BF16),
        compiler_params=pltpu.CompilerParams(dimension_semantics=("parallel",)),
        name="attn_a_ctx",
    )(qa, ka, vat, z, sink)


def _attn_a_lat_kernel(q_ref, kp_ref, kc_ref, kn_ref, vp_ref, vc_ref, vn_ref, kx_ref, vx_ref, z_ref, sink_ref,
                       o_ref, *, nq):
    j = pl.program_id(1)
    ka = jnp.concatenate([kp_ref[...], kc_ref[...], kn_ref[...], kx_ref[...]], axis=0)
    vat = jnp.concatenate([vp_ref[...], vc_ref[...], vn_ref[...], vx_ref[...]], axis=1)
    n_loc = TM + 2 * WINDOW
    kj = lax.broadcasted_iota(jnp.int32, (ka.shape[0], TM), 0)
    qi = lax.broadcasted_iota(jnp.int32, (ka.shape[0], TM), 1)
    ok = (kj >= qi) & (kj <= qi + 2 * WINDOW)
    ok = ok & ((kj >= WINDOW) | (j > 0)) & ((kj < TM + WINDOW) | (j < nq - 1))
    bias_t = jnp.where(ok | (kj >= n_loc), 0.0, NEG_INF)
    _gated_store(_attn_a_heads(q_ref[...], ka, vat, bias_t, sink_ref), z_ref, o_ref)


def _attn_a_lat(qa, ka, vat, z, sink, kx, vxt, l, seq):
    t = qa.shape[0]
    nq = seq // TM
    past = kx.shape[2]
    r = TM // WINDOW
    row = lambda b, j: (b * nq + j, 0)
    prev = lambda b, j: ((b * nq + j) * r - jnp.where(j > 0, 1, 0), 0)
    nxt = lambda b, j: ((b * nq + j) * r + jnp.where(j < nq - 1, r, r - 1), 0)
    swap = lambda f: (lambda b, j: f(b, j)[::-1])
    return pl.pallas_call(
        functools.partial(_attn_a_lat_kernel, nq=nq),
        grid=(t // seq, nq),
        in_specs=[pl.BlockSpec((TM, W_A), row),
                  pl.BlockSpec((WINDOW, KA_COLS), prev),
                  pl.BlockSpec((TM, KA_COLS), row),
                  pl.BlockSpec((WINDOW, KA_COLS), nxt),
                  pl.BlockSpec((VAT_ROWS, WINDOW), swap(prev)),
                  pl.BlockSpec((VAT_ROWS, TM), swap(row)),
                  pl.BlockSpec((VAT_ROWS, WINDOW), swap(nxt)),
                  pl.BlockSpec((None, None, past, KA_COLS), lambda b, j: (b, l, 0, 0)),
                  pl.BlockSpec((None, None, VAT_ROWS, past), lambda b, j: (b, l, 0, 0)),
                  pl.BlockSpec((TM, W_A), row),
                  _layer_spec(sink, l)],
        out_specs=pl.BlockSpec((TM, W_A), row),
        out_shape=jax.ShapeDtypeStruct((t, W_A), BF16),
        compiler_params=pltpu.CompilerParams(dimension_semantics=("parallel", "parallel")),
        name="attn_a_lat",
    )(qa, ka, ka, ka, vat, vat, vat, kx, vxt, z, sink)


def _attn_b_kernel(*refs, has_ctx, group):
    if has_ctx:
        q_ref, k_ref, vt_ref, kx_ref, vxt_ref, z_ref, o_ref = refs
        kb = jnp.concatenate([k_ref[...], kx_ref[...]], axis=0)
        vbt = jnp.concatenate([vt_ref[...], vxt_ref[...]], axis=1)
    else:
        q_ref, k_ref, vt_ref, z_ref, o_ref = refs
        kb, vbt = k_ref[...], vt_ref[...]
    q = q_ref[...]

    def scores(h0):
        heads = range(h0, h0 + group)
        return _scores_t([_tile(q, h) for h in heads], [_tile(kb, h) for h in heads], None)

    st = scores(0)
    for h0 in range(0, N_HEADS_B, group):
        st_next = scores(h0 + group) if h0 + group < N_HEADS_B else None
        outs = _softmax_pv(st, [vbt[h * VT_ROWS:(h + 1) * VT_ROWS] for h in range(h0, h0 + group)], None)
        _gated_store(outs, z_ref, o_ref, h0 // 2)
        st = st_next


def _attn_b(qb, kb, vbt, z, seq, qblk, group, kx=None, vxt=None):
    t = qb.shape[0]
    nq = seq // qblk
    has_ctx = kx is not None
    hw = HB_COLS
    in_specs = [pl.BlockSpec((qblk, hw), lambda b, j: (b * nq + j, 0)),
                pl.BlockSpec((seq, hw), lambda b, j: (b, 0)),
                pl.BlockSpec((VBT_ROWS, seq), lambda b, j: (0, b))]
    args = [qb, kb, vbt]
    if has_ctx:
        past = kx.shape[0] // (t // seq)
        in_specs += [pl.BlockSpec((past, hw), lambda b, j: (b, 0)),
                     pl.BlockSpec((None, VBT_ROWS, past), lambda b, j: (b, 0, 0))]
        args += [kx, vxt]
    in_specs.append(pl.BlockSpec((qblk, W_B), lambda b, j: (b * nq + j, 1)))
    args.append(z)
    return pl.pallas_call(
        functools.partial(_attn_b_kernel, has_ctx=has_ctx, group=group),
        grid=(t // seq, nq),
        in_specs=in_specs,
        out_specs=pl.BlockSpec((qblk, W_B), lambda b, j: (b * nq + j, 0)),
        out_shape=jax.ShapeDtypeStruct((t, W_B), BF16),
        compiler_params=pltpu.CompilerParams(dimension_semantics=("parallel", "parallel")),
        name="attn_b_lat" if has_ctx else "attn_b_ctx",
    )(*args)


def _gdn_local_kernel(cq_ref, prev_ref, next_ref, small_ref, abt_ref, cw_ref, prow_ref, pcol_ref,
                      u_ref, w_ref, qg_ref, kd_ref, attn_ref, eg_ref, qkv_scr, gb_scr, *, tiles_per_seq):
    tpos = pl.program_id(0) % tiles_per_seq
    x = cq_ref[...].astype(F32)
    prev_row = jnp.where(tpos > 0, prev_ref[...].astype(F32)[HALO - 1:HALO, :], 0.0)
    next_row = jnp.where(tpos < tiles_per_seq - 1, next_ref[...].astype(F32)[0:1, :], 0.0)
    rows = lax.broadcasted_iota(jnp.int32, (TM, 1), 0)
    xm1 = jnp.where(rows == 0, prev_row, pltpu.roll(x, 1, 0))
    xp1 = jnp.where(rows == TM - 1, next_row, pltpu.roll(x, TM - 1, 0))
    cw = cw_ref[...]
    y = _silu(xm1 * cw[0:1] + x * cw[1:2] + xp1 * cw[2:3])
    nq = N_HEADS_C * DK_C
    for h in range(N_HEADS_C):
        qh = y[:, h * DK_C:(h + 1) * DK_C]
        kh = y[:, nq + h * DK_C:nq + (h + 1) * DK_C]
        qkv_scr[:, h * DK_C:(h + 1) * DK_C] = (
            qh * lax.rsqrt(jnp.sum(qh * qh, axis=-1, keepdims=True) + EPS) * (DK_C ** -0.5))
        qkv_scr[:, nq + h * DK_C:nq + (h + 1) * DK_C] = kh * lax.rsqrt(jnp.sum(kh * kh, axis=-1, keepdims=True) + EPS)
    qkv_scr[:, 2 * nq:] = y[:, 2 * nq:]

    sm = small_ref[...]
    prow = prow_ref[...]
    gb_scr[:, 0:8] = -jnp.exp(prow[0:1]) * _softplus(sm[:, S_A:S_A + 8] + prow[1:2])
    gb_scr[:, 8:16] = _sigmoid(sm[:, S_B:S_B + 8])
    pcol = pcol_ref[...]

    ri = lax.broadcasted_iota(jnp.int32, (CHUNK, CHUNK), 0)
    ci = lax.broadcasted_iota(jnp.int32, (CHUNK, CHUNK), 1)
    tril = (ri >= ci).astype(F32)
    triu = (ri <= ci).astype(F32)
    xor = ri ^ ci
    eye = (ri == ci).astype(F32)

    lows, rhss, order = [], [], []

    def solve():
        low = jnp.stack(lows, axis=0)
        inv = eye[None] - jnp.where(xor[None] == 1, low, 0.0)
        b = 2
        while b < CHUNK:
            cpl = jnp.where((xor[None] >= b) & (xor[None] < 2 * b), low, 0.0)
            tmp = _bdot(cpl.astype(BF16), inv.astype(BF16))
            inv = inv - _bdot(inv.astype(BF16), tmp.astype(BF16))
            b *= 2
        uw = _bdot(inv.astype(BF16), jnp.stack(rhss, axis=0))
        for i, (rs, dh) in enumerate(order):
            cs = slice(dh * DK_C, (dh + 1) * DK_C)
            u_ref[rs, cs] = uw[i, :, :DV_C]
            w_ref[rs, cs] = uw[i, :, DV_C:].astype(BF16)
        lows.clear(), rhss.clear(), order.clear()

    for c in range(CPT):
        if c % SOLVE_CHUNKS == 0 and c > 0:
            solve()
        rs = slice(c * CHUNK, (c + 1) * CHUNK)
        gcol = gb_scr[rs, 0:8]
        bcol = gb_scr[rs, 8:16]
        abt = abt_ref[c]
        grow = -jnp.exp(pcol[:, 0:1]) * _softplus(abt[0:8] + pcol[:, 1:2])
        gc_f = _dot_exact(tril, gcol)
        gc_b = _dot_exact(triu, gcol)
        gr_f = _dot_exact(grow, triu)
        gr_b = _dot_exact(grow, tril)
        for h in range(N_HEADS_C):
            q = qkv_scr[rs, h * DK_C:(h + 1) * DK_C]
            k = qkv_scr[rs, nq + h * DK_C:nq + (h + 1) * DK_C]
            v = qkv_scr[rs, 2 * nq + h * DV_C:2 * nq + (h + 1) * DV_C]
            kb16 = k.astype(BF16)
            kk = _dot_nt(kb16, kb16)
            qk = _dot_nt(q.astype(BF16), kb16)
            for d in range(2):
                dh = d * N_HEADS_C + h
                gc = (gc_f if d == 0 else gc_b)[:, dh:dh + 1]
                gr = (gr_f if d == 0 else gr_b)[dh:dh + 1, :]
                beta = bcol[:, dh:dh + 1]
                incl = (ri >= ci) if d == 0 else (ri <= ci)
                strict = (ri > ci) if d == 0 else (ri < ci)
                decay = jnp.where(incl, jnp.exp(jnp.where(incl, gc - gr, 0.0)), 0.0)
                lows.append(jnp.where(strict, beta * kk * decay, 0.0))
                eg = jnp.exp(gc)
                rhss.append(jnp.concatenate([v * beta, k * (beta * eg)], axis=-1).astype(BF16))
                order.append((rs, dh))
                g_last = gc[CHUNK - 1:CHUNK] if d == 0 else gc[0:1]
                cs = slice(dh * DK_C, (dh + 1) * DK_C)
                qg_ref[rs, cs] = (q * eg).astype(BF16)
                kd_ref[rs, cs] = (k * jnp.exp(g_last - gc)).astype(BF16)
                attn_ref[rs, dh * CHUNK:(dh + 1) * CHUNK] = (qk * decay).astype(BF16)
                eg_ref[c, dh:dh + 1, :] = jnp.broadcast_to(jnp.exp(g_last), (1, LANE))
    solve()


def _gdn_local(cqkv, small, abt, conv_w, prow, pcol, l, tiles_per_seq):
    t = cqkv.shape[0]
    nt = t // TM
    nh8 = t // HALO
    row = lambda i: (i, 0)
    dh = 2 * N_HEADS_C
    return pl.pallas_call(
        functools.partial(_gdn_local_kernel, tiles_per_seq=tiles_per_seq),
        grid=(nt,),
        in_specs=[pl.BlockSpec((TM, QKV_C), row),
                  pl.BlockSpec((HALO, QKV_C), lambda i: (jnp.maximum(i * (TM // HALO) - 1, 0), 0)),
                  pl.BlockSpec((HALO, QKV_C), lambda i: (jnp.minimum((i + 1) * (TM // HALO), nh8 - 1), 0)),
                  pl.BlockSpec((TM, LANE), row),
                  pl.BlockSpec((CPT, 16, CHUNK), lambda i: (i, 0, 0)),
                  _layer_spec(conv_w, l), _layer_spec(prow, l), _layer_spec(pcol, l)],
        out_specs=[pl.BlockSpec((TM, dh * DV_C), row),
                   pl.BlockSpec((TM, dh * DK_C), row),
                   pl.BlockSpec((TM, dh * DK_C), row),
                   pl.BlockSpec((TM, dh * DK_C), row),
                   pl.BlockSpec((TM, dh * CHUNK), row),
                   pl.BlockSpec((CPT, dh, LANE), lambda i: (i, 0, 0))],
        out_shape=[jax.ShapeDtypeStruct((t, dh * DV_C), F32),
                   jax.ShapeDtypeStruct((t, dh * DK_C), BF16),
                   jax.ShapeDtypeStruct((t, dh * DK_C), BF16),
                   jax.ShapeDtypeStruct((t, dh * DK_C), BF16),
                   jax.ShapeDtypeStruct((t, dh * CHUNK), BF16),
                   jax.ShapeDtypeStruct((t // CHUNK, dh, LANE), F32)],
        scratch_shapes=[pltpu.VMEM((TM, QKV_C), F32), pltpu.VMEM((TM, 16), F32)],
        compiler_params=pltpu.CompilerParams(dimension_semantics=("parallel",)),
        name="gdn_local",
    )(cqkv, cqkv, cqkv, small, abt, conv_w, prow, pcol)


def _gdn_scan_kernel(*refs, nt, ns, has_init, want_state, n_carry):
    it = iter(refs)
    ins = [[next(it) for _ in range(6)] for _ in range(2)]
    s0_ref = next(it) if has_init else None
    for _ in range(n_carry):
        next(it)
    o_refs = [next(it), next(it)]
    st_ref = next(it) if want_state else None
    s_scr = next(it)
    j = pl.program_id(1)
    nh = N_HEADS_C
    nst = ns * 2 * nh

    @pl.when(j == 0)
    def _():
        if has_init:
            s_scr[...] = s0_ref[...].reshape(nst, DK_C, DV_C)
        else:
            s_scr[...] = jnp.zeros_like(s_scr)

    for step in range(CPT):
        chunk = lambda d: step if d == 0 else CPT - 1 - step

        def gather(idx, width):
            return jnp.stack([ins[d][idx][s, chunk(d) * CHUNK:(chunk(d) + 1) * CHUNK, h * width:(h + 1) * width]
                              for s in range(ns) for d in range(2) for h in range(nh)])

        u, w, qg, kd, attn = gather(0, DV_C), gather(1, DK_C), gather(2, DK_C), gather(3, DK_C), gather(4, CHUNK)
        eg = jnp.stack([ins[d][5][s, chunk(d), d * nh + h:d * nh + h + 1, :]
                        for s in range(ns) for d in range(2) for h in range(nh)])
        st = s_scr[...]
        sb = st.astype(BF16)
        v_new = u - _bdot(w, sb)
        vb = v_new.astype(BF16)
        o = _bdot(qg, sb) + _bdot(attn, vb)
        s_scr[...] = st * eg + _bdot_tn(kd, vb)
        for s in range(ns):
            for d in range(2):
                for h in range(nh):
                    o_refs[d][s, chunk(d) * CHUNK:(chunk(d) + 1) * CHUNK, h * DV_C:(h + 1) * DV_C] = (
                        o[(s * 2 + d) * nh + h])

    if want_state:
        @pl.when(j == nt - 1)
        def _():
            st_ref[...] = s_scr[...].reshape(ns, 2, nh, DK_C, DV_C)


def _gdn_scan(u, w, qg, kd, attn, eg, s0, l, seq, state_out=None):
    t = u.shape[0]
    want_state = state_out is not None
    nt = seq // TM
    nb = t // seq
    ns = 2 if nb % 2 == 0 else 1
    half = N_HEADS_C * DK_C
    has_init = s0 is not None
    by_seq = lambda a: a.reshape((nb, a.shape[0] // nb) + a.shape[1:])
    in_specs, args = [], []
    for d in range(2):
        tile = (lambda b, j: j) if d == 0 else (lambda b, j: nt - 1 - j)
        row = lambda b, j, d=d, tile=tile: (b, tile(b, j), d)
        row4 = lambda b, j, tile=tile: (b, tile(b, j), 0, 0)
        in_specs += [pl.BlockSpec((ns, TM, half), row)] * 4
        in_specs += [pl.BlockSpec((ns, TM, N_HEADS_C * CHUNK), row),
                     pl.BlockSpec((ns, CPT, 2 * N_HEADS_C, LANE), row4)]
        args += [by_seq(a) for a in (u, w, qg, kd, attn, eg)]
    st_tail = (2, N_HEADS_C, DK_C, DV_C)
    if has_init:
        in_specs.append(pl.BlockSpec((ns, None) + st_tail, lambda b, j: (b, l, 0, 0, 0, 0)))
        args.append(s0)
    out_specs = [pl.BlockSpec((ns, TM, half), lambda b, j: (b, j, 0)),
                 pl.BlockSpec((ns, TM, half), lambda b, j: (b, nt - 1 - j, 0))]
    out_shape = [jax.ShapeDtypeStruct((nb, seq, half), F32)] * 2
    aliases, n_carry = {}, 0
    if want_state:
        depth, carry = state_out
        out_specs.append(pl.BlockSpec((ns, None) + st_tail, lambda b, j: (b, l, 0, 0, 0, 0)))
        out_shape.append(jax.ShapeDtypeStruct((nb, depth) + st_tail, F32))
        if carry is not None:
            n_carry = 1
            aliases[len(args)] = 2
            in_specs.append(pl.BlockSpec(memory_space=pl.ANY))
            args.append(carry)
    outs = pl.pallas_call(
        functools.partial(_gdn_scan_kernel, nt=nt, ns=ns, has_init=has_init, want_state=want_state,
                          n_carry=n_carry),
        grid=(nb // ns, nt),
        in_specs=in_specs,
        out_specs=out_specs,
        out_shape=out_shape,
        input_output_aliases=aliases,
        scratch_shapes=[pltpu.VMEM((ns * 2 * N_HEADS_C, DK_C, DV_C), F32)],
        compiler_params=pltpu.CompilerParams(dimension_semantics=("parallel", "arbitrary")),
        name="gdn_scan",
    )(*args)
    return [outs[0].reshape(t, half), outs[1].reshape(t, half)] + list(outs[2:])


def _merge_kernel(x_ref, mod_ref, oa_ref, ob_ref, cf_ref, cb_ref, zc_ref, gates_ref, gn_ref,
                  wa_ref, wb_ref, wc_ref, wo_ref, fg_ref, o_ref, *, last):
    oc = cf_ref[...] + cb_ref[...]
    zc = zc_ref[...].astype(F32)
    gn = gn_ref[...]
    parts = []
    for h in range(N_HEADS_C):
        hs = slice(h * DV_C, (h + 1) * DV_C)
        och = oc[:, hs]
        och = och * lax.rsqrt(jnp.mean(och * och, axis=-1, keepdims=True) + EPS) * gn
        parts.append((och * _silu(zc[:, hs])).astype(BF16))
    ocz = jnp.concatenate(parts, axis=-1)
    pa = _dot(oa_ref[...], wa_ref[...])
    pb = _dot(ob_ref[...], wb_ref[...])
    pc = _dot(ocz, wc_ref[...])
    ga = _sigmoid(gates_ref[:, 0:D_MODEL].astype(F32))
    gb = _sigmoid(gates_ref[:, D_MODEL:2 * D_MODEL].astype(F32))
    gc = _sigmoid(gates_ref[:, 2 * D_MODEL:].astype(F32))
    y = _dot((ga * pa + gb * pb + gc * pc).astype(BF16), wo_ref[...])
    gate = mod_ref[0][:, 2 * D_MODEL:]
    xo = x_ref[...] + gate * y
    if last:
        xo = xo * lax.rsqrt(jnp.mean(xo * xo, axis=-1, keepdims=True) + EPS) * fg_ref[...]
    o_ref[...] = xo


def _merge(x2d, l, mod, mod_row_fn, oa, ob, cf, cb, z, gates, gn, wa, wb, wc, wo, fg, last):
    t = x2d.shape[0]
    row = lambda i: (i, 0)
    return pl.pallas_call(
        functools.partial(_merge_kernel, last=last),
        grid=(t // TMD,),
        in_specs=[pl.BlockSpec((TMD, D_MODEL), row),
                  pl.BlockSpec((1, 1, 3 * D_MODEL), lambda i: (mod_row_fn(i), 0, 0)),
                  pl.BlockSpec((TMD, W_A), row),
                  pl.BlockSpec((TMD, W_B), row),
                  pl.BlockSpec((TMD, W_C), row),
                  pl.BlockSpec((TMD, W_C), row),
                  pl.BlockSpec((TMD, W_C), lambda i: (i, 2)),
                  pl.BlockSpec((TMD, 3 * D_MODEL), row),
                  _layer_spec(gn, l), _layer_spec(wa, l), _layer_spec(wb, l), _layer_spec(wc, l),
                  _layer_spec(wo, l),
                  pl.BlockSpec((1, D_MODEL), lambda i: (0, 0))],
        out_specs=pl.BlockSpec((TMD, D_MODEL), row),
        out_shape=jax.ShapeDtypeStruct((t, D_MODEL), F32),
        compiler_params=pltpu.CompilerParams(dimension_semantics=("parallel",)),
        name="merge",
    )(x2d, mod, oa, ob, cf, cb, z, gates, gn, wa, wb, wc, wo, fg)


def _rope_tables(n_tokens, rot_dim):
    rows = n_tokens // GRID_W
    row = np.repeat(np.arange(rows), GRID_W).astype(np.float32)
    col = np.tile(np.arange(GRID_W), rows).astype(np.float32)
    n_pairs = rot_dim // 4
    inv = (np.float32(ROPE_BASE) ** (-np.arange(n_pairs, dtype=np.float32) / np.float32(n_pairs))).astype(np.float32)
    ang = np.concatenate([row[:, None] * inv, col[:, None] * inv], axis=-1)
    c, s = np.cos(ang), np.sin(ang)
    return np.repeat(c, 2, axis=-1), np.stack([-s, s], axis=-1).reshape(n_tokens, rot_dim)


def _in_offsets():
    o = [0]
    for n in IN_SIZES:
        o.append(o[-1] + n)
    return o


def _relayout_moves():
    o = _in_offsets()
    order = [(o[0], W_A), (o[1], LANE), (o[1] + HD_A, HD_A), (o[1], HD_A), (o[2], LANE),
             (o[3], W_A), (o[7], W_B), (o[11], W_C), (o[12], 3 * D_MODEL), (o[4], Q_RANK_B), (o[5], KV_RANK_B),
             (None, S_KPE), (o[6], QK_ROPE_B), (o[9], 4 * N_HEADS_C), (None, LANE - S_B - 8), (o[8], QKV_C)]
    moves, dst = [], 0
    for src, n in order:
        moves.append((src, dst, n))
        dst += n
    assert dst == P_END
    return moves


def _relayout_kernel(w_ref, o_ref, wab_ref, wvat_ref):
    for src, dst, n in _relayout_moves():
        if src is None:
            o_ref[dst:dst + n, :] = jnp.zeros((n, o_ref.shape[1]), BF16)
        else:
            o_ref[dst:dst + n, :] = w_ref[src:src + n, :].astype(BF16)
    o = _in_offsets()
    wab_ref[...] = w_ref[o[9]:o[11], :].astype(BF16)
    wvat_ref[...] = w_ref[o[2]:o[3], :].astype(BF16)


def _relayout_w_in(w_in):
    w_t = jnp.swapaxes(w_in, 1, 2)
    depth, width, _ = w_t.shape
    cols = 128
    nab, nv = 4 * N_HEADS_C, N_KV_A * HD_A
    col = lambda l, i: (l, 0, i)
    return pl.pallas_call(
        _relayout_kernel,
        grid=(depth, D_MODEL // cols),
        in_specs=[pl.BlockSpec((None, width, cols), col)],
        out_specs=[pl.BlockSpec((None, P_END, cols), col),
                   pl.BlockSpec((None, nab, cols), col),
                   pl.BlockSpec((None, nv, cols), col)],
        out_shape=[jax.ShapeDtypeStruct((depth, P_END, D_MODEL), BF16),
                   jax.ShapeDtypeStruct((depth, nab, D_MODEL), BF16),
                   jax.ShapeDtypeStruct((depth, nv, D_MODEL), BF16)],
        name="w_in_relayout",
    )(w_t)


def _prep_weights(w_in, w_uq, w_ukv):
    depth = w_in.shape[0]
    wp, wab, wvat = _relayout_w_in(w_in)
    hd = QK_NOPE_B + QK_ROPE_B
    wuq = jnp.pad(w_uq.reshape(depth, Q_RANK_B, N_HEADS_B, hd), ((0, 0), (0, 0), (0, 0), (0, MLA_HW - hd)))
    wuq = wuq.reshape(depth, Q_RANK_B, HB_COLS).astype(BF16)
    kv = w_ukv.reshape(depth, KV_RANK_B, N_HEADS_B, QK_NOPE_B + V_HD_B)
    wk = jnp.pad(kv[..., :QK_NOPE_B], ((0, 0), (0, 0), (0, 0), (0, MLA_HW - QK_NOPE_B)))
    wukv = wk.reshape(depth, KV_RANK_B, HB_COLS).astype(BF16)
    wvbt = jnp.swapaxes(kv[..., QK_NOPE_B:].reshape(depth, KV_RANK_B, W_B), 1, 2).astype(BF16)
    return wp, wab, wvat, wuq, wukv, wvbt


def _cache_tiles_a(kx, vx):
    k0, k1 = kx[..., 0, :], kx[..., 1, :]
    z = jnp.zeros_like(k0)
    ka = jnp.concatenate([k0, z, z, k0, k1, z, z, k1], axis=-1).astype(BF16)
    vt = jnp.transpose(vx, (0, 1, 3, 4, 2))
    vt = jnp.concatenate([vt, jnp.ones(vt.shape[:3] + (VT_ONES, vt.shape[4]), vt.dtype)], axis=3)
    return ka, vt.reshape(vt.shape[:2] + (VAT_ROWS, vt.shape[4])).astype(BF16)


def kernel(x_prompt, x_sample, cache_attn_k, cache_attn_v, cache_mla_ckv, cache_mla_kpe, state_gdn, c, c_ctx,
           norm_g, w_ada, b_ada, w_in, attn_sink, mla_q_norm, mla_w_uq, mla_kv_norm, mla_w_ukv, gdn_conv,
           gdn_a_log, gdn_dt_bias, gdn_norm, w_branch_a, w_branch_b, w_branch_c, w_out, final_norm_g):
    depth = w_in.shape[0]
    nb_c, seq_c, _ = x_prompt.shape
    nb_l, seq_l, _ = x_sample.shape
    past = cache_attn_k.shape[2]
    assert P_END % LANE == 0 and seq_c % TM == 0 and seq_l % TMD == 0 and nb_l < 8 and TM == 2 * WINDOW
    assert (nb_c * seq_c) % TMD == 0 and TMD % TM == 0

    cond8 = jnp.zeros((8, D_MODEL), F32).at[:nb_l].set(c).at[nb_l].set(c_ctx)
    mod = _modulation(cond8, w_ada, b_ada).reshape(depth * 8, 1, 3 * D_MODEL)

    c_a, s_a = _rope_tables(seq_l, HD_A)
    c_b, s_b = _rope_tables(seq_l, QK_ROPE_B)
    pad_l, pad_r = S_KPE, LANE - S_KPE - QK_ROPE_B
    one, zero = np.ones((seq_l, 1), np.float32), np.zeros((seq_l, 1), np.float32)
    rope_tabs = tuple(jnp.asarray(a) for a in (
        np.tile(c_a, (1, LANE // HD_A)), np.tile(s_a, (1, LANE // HD_A)),
        np.concatenate([np.tile(one, (1, pad_l)), c_b, np.tile(one, (1, pad_r))], 1),
        np.concatenate([np.tile(zero, (1, pad_l)), s_b, np.tile(zero, (1, pad_r))], 1)))

    weights = _prep_weights(w_in, mla_w_uq, mla_w_ukv)
    wukv, wvbt = weights[4], weights[5]
    ng = norm_g.reshape(depth, 1, D_MODEL)
    qn = mla_q_norm.reshape(depth, 1, Q_RANK_B)
    kvn = mla_kv_norm.reshape(depth, 1, KV_RANK_B)
    sink = attn_sink.reshape(depth, 1, N_HEADS_A)
    prow = jnp.stack([gdn_a_log.reshape(depth, -1), gdn_dt_bias.reshape(depth, -1)], axis=1)
    pcol = jnp.swapaxes(prow, 1, 2)
    gn = gdn_norm.reshape(depth, 1, DV_C)
    wa, wb, wc, wo = (w.astype(BF16) for w in (w_branch_a, w_branch_b, w_branch_c, w_out))
    fg = final_norm_g.reshape(1, D_MODEL)
    kxa, vxa = _cache_tiles_a(cache_attn_k, cache_attn_v)
    kpex = jnp.pad(cache_mla_kpe, ((0, 0), (0, 0), (0, 0), (pad_l, pad_r)))

    tps_c, tps_l = seq_c // TM, seq_l // TM
    tpd_l = seq_l // TMD
    y_p = x_prompt.reshape(nb_c * seq_c, D_MODEL)
    y_s = x_sample.reshape(nb_l * seq_l, D_MODEL)
    new_cache, new_state = None, None
    for l in range(depth):
        last = l == depth - 1

        mod_row_c = lambda i, l=l: l * 8 + nb_l
        outs = _inproj(y_p, l, mod, mod_row_c, ng, weights, qn, kvn, None, tps_c, (depth, seq_c, new_cache))
        (qa, ka, vat, z, gates, qb, kb, vbt, small, cqkv, abt), new_cache = outs[:11], tuple(outs[11:])
        oa = _attn_a_ctx(qa, ka, vat, z, sink, l, seq_c)
        ob = _attn_b(qb, kb, vbt, z, seq_c, seq_c, N_HEADS_B)
        u, w, qg, kd, attn, eg = _gdn_local(cqkv, small, abt, gdn_conv, prow, pcol, l, tps_c)
        cf, cb, new_state = _gdn_scan(u, w, qg, kd, attn, eg, None, l, seq_c, (depth, new_state))
        y_p = _merge(y_p, l, mod, mod_row_c, oa, ob, cf, cb, z, gates, gn, wa, wb, wc, wo, fg, last)

        mod_row_l = lambda i, l=l: l * 8 + i // tpd_l
        (qa, ka, vat, z, gates, qb, kb, vbt, small, cqkv, abt) = _inproj(
            y_s, l, mod, mod_row_l, ng, weights, qn, kvn, rope_tabs, tpd_l)
        oa = _attn_a_lat(qa, ka, vat, z, sink, kxa, vxa, l, seq_l)
        kxb, vxb = _kvup(cache_mla_ckv, kpex, wukv, wvbt, l)
        ob = _attn_b(qb, kb, vbt, z, seq_l, TM, 4, kxb, vxb)
        u, w, qg, kd, attn, eg = _gdn_local(cqkv, small, abt, gdn_conv, prow, pcol, l, tps_l)
        cf, cb = _gdn_scan(u, w, qg, kd, attn, eg, state_gdn, l, seq_l)
        y_s = _merge(y_s, l, mod, mod_row_l, oa, ob, cf, cb, z, gates, gn, wa, wb, wc, wo, fg, last)

    new_k, new_v, new_ckv, new_kpe = new_cache
    kv_shape = (nb_c, depth, seq_c, N_KV_A, HD_A)
    return (y_p.reshape(nb_c, seq_c, D_MODEL), y_s.reshape(nb_l, seq_l, D_MODEL),
            new_k.reshape(kv_shape), new_v.reshape(kv_shape), new_ckv, new_kpe, new_state)
```

```python
import functools

import numpy as np
import jax
import jax.numpy as jnp
from jax import lax
from jax.experimental import pallas as pl
from jax.experimental.pallas import tpu as pltpu

F32 = jnp.float32
BF16 = jnp.bfloat16

D_MODEL = 1024
GRID_W = 64
ROPE_BASE = 10000.0
EPS = 1e-6
NEG_INF = -1e30
N_HEADS_A = 8
N_KV_A = 2
HD_A = 64
GQA_GROUP = N_HEADS_A // N_KV_A
WINDOW = 128
N_HEADS_B = 8
QK_NOPE_B = 64
QK_ROPE_B = 32
V_HD_B = 64
Q_RANK_B = 384
KV_RANK_B = 256
MLA_SCALE = (QK_NOPE_B + QK_ROPE_B) ** -0.5
N_HEADS_C = 4
DK_C = 128
DV_C = 128
CHUNK = 64
W_A = N_HEADS_A * HD_A
W_B = N_HEADS_B * V_HD_B
W_C = N_HEADS_C * DV_C
QKV_C = 2 * N_HEADS_C * DK_C + W_C
IN_SIZES = (W_A, N_KV_A * HD_A, N_KV_A * HD_A, W_A, Q_RANK_B, KV_RANK_B, QK_ROPE_B, W_B, QKV_C,
            2 * N_HEADS_C, 2 * N_HEADS_C, W_C, 3 * D_MODEL)

LANE = 128
HALF = LANE // 2
TM = 256
TMD = 512
CPT = TM // CHUNK
SOLVE_CHUNKS = 4
SCAN_SEQS = 4
HALO = 16
MLA_HW = 128
KA_COLS = 4 * LANE
HB_COLS = N_HEADS_B * MLA_HW
VT_ONES = 16
VT_ROWS = V_HD_B + VT_ONES
VAT_ROWS = N_KV_A * VT_ROWS
VBT_ROWS = N_HEADS_B * VT_ROWS
LOG2E = 1.4426950408889634

P_QKV = 0
A_COLS = W_A + 3 * LANE
P_Z = P_QKV + A_COLS
P_GATES = P_Z + 1536
P_CQ = P_GATES + 3 * D_MODEL
P_CKV = P_CQ + Q_RANK_B
P_SMALL = P_CKV + KV_RANK_B
P_CQKV = P_SMALL + LANE
P_END = P_CQKV + QKV_C
S_KPE = 64
S_A = 96
S_B = 104


def _sigmoid(x):
    return 0.5 * jnp.tanh(0.5 * x) + 0.5


def _silu(x):
    return x * _sigmoid(x)


def _softplus(x):
    return jnp.maximum(x, 0.0) + jnp.log(1.0 + jnp.exp(-jnp.abs(x)))


def _dot(a, b):
    return jnp.dot(a, b, preferred_element_type=F32)


def _dot_nt(a, b):
    return lax.dot_general(a, b, (((1,), (1,)), ((), ())), preferred_element_type=F32)


def _bdot(a, b):
    return lax.dot_general(a, b, (((2,), (1,)), ((0,), (0,))), preferred_element_type=F32)


def _bdot_nt(a, b):
    return lax.dot_general(a, b, (((2,), (2,)), ((0,), (0,))), preferred_element_type=F32)


def _bdot_tn(a, b):
    return lax.dot_general(a, b, (((1,), (1,)), ((0,), (0,))), preferred_element_type=F32)


def _dot_exact(a, b):
    return jnp.dot(a, b, preferred_element_type=F32, precision=lax.Precision.HIGHEST)


def _rope(x, c, s):
    n = x.shape[-1]
    lane = lax.broadcasted_iota(jnp.int32, x.shape, 1)
    swapped = jnp.where(lane % 2 == 0, pltpu.roll(x, n - 1, 1), pltpu.roll(x, 1, 1))
    return x * c + swapped * s


def _mod_kernel(cond_ref, w_ref, b_ref, out_ref):
    cnd = cond_ref[...]
    out_ref[0] = _dot(_silu(cnd).astype(BF16), w_ref[0].astype(BF16)) + b_ref[0]


def _modulation(cond8, w_ada, b_ada):
    depth = w_ada.shape[0]
    tn = 768
    return pl.pallas_call(
        _mod_kernel,
        grid=(depth, 3 * D_MODEL // tn),
        in_specs=[pl.BlockSpec((8, D_MODEL), lambda l, n: (0, 0)),
                  pl.BlockSpec((1, D_MODEL, tn), lambda l, n: (l, 0, n)),
                  pl.BlockSpec((1, 1, tn), lambda l, n: (l, 0, n))],
        out_specs=pl.BlockSpec((1, 8, tn), lambda l, n: (l, 0, n)),
        out_shape=jax.ShapeDtypeStruct((depth, 8, 3 * D_MODEL), F32),
        name="adaln_mod",
    )(cond8, w_ada, b_ada.reshape(depth, 1, 3 * D_MODEL))


def _inproj_kernel(*refs, rope, cache_seq, n_carry):
    it = iter(refs)
    (x_ref, mod_ref, ng_ref, wp_ref, wab_ref, wvat_ref, wuq_ref, wukv_ref, wvbt_ref, qn_ref,
     kvn_ref) = (next(it) for _ in range(11))
    if rope:
        ca_ref, sa_ref, cb_ref, sb_ref = (next(it) for _ in range(4))
    for _ in range(n_carry):
        next(it)
    (qa_ref, ka_ref, vat_ref, z_ref, gates_ref, qb_ref, kb_ref, vbt_ref, small_ref, cqkv_ref,
     abt_ref) = (next(it) for _ in range(11))
    if cache_seq:
        ck_ref, cv_ref, cckv_ref, ckpe_ref = (next(it) for _ in range(4))

    def to_cache(ref, val):
        for s in range(TMD // cache_seq):
            ref[s] = val[s * cache_seq:(s + 1) * cache_seq]

    x = x_ref[...]
    mod = mod_ref[0]
    shift, scale = mod[:, :D_MODEL], mod[:, D_MODEL:2 * D_MODEL]
    xn = x * lax.rsqrt(jnp.mean(x * x, axis=-1, keepdims=True) + EPS) * ng_ref[...]
    hb = (xn * (1.0 + scale) + shift).astype(BF16)
    lane = lax.broadcasted_iota(jnp.int32, (TMD, LANE), 1)
    lo = lane < HALF

    def mm(lo_col, hi_col):
        return _dot_nt(hb, wp_ref[lo_col:hi_col, :])

    r = mm(P_QKV, P_QKV + A_COLS)
    tiles = [r[:, t * LANE:(t + 1) * LANE] for t in range(A_COLS // LANE)]
    if cache_seq:
        to_cache(ck_ref, tiles[4])
        to_cache(cv_ref, tiles[6])
    if rope:
        ca, sa = ca_ref[...], sa_ref[...]
        tiles[:6] = [_rope(t, ca, sa) for t in tiles[:6]]
    for t in range(4):
        qa_ref[:, t * LANE:(t + 1) * LANE] = (tiles[t] * (HD_A ** -0.5 * LOG2E)).astype(BF16)
    k01, k10 = tiles[4], tiles[5]
    ka_ref[:, 0 * LANE:1 * LANE] = jnp.where(lo, k01, 0.0).astype(BF16)
    ka_ref[:, 1 * LANE:2 * LANE] = jnp.where(lo, 0.0, k10).astype(BF16)
    ka_ref[:, 2 * LANE:3 * LANE] = jnp.where(lo, k10, 0.0).astype(BF16)
    ka_ref[:, 3 * LANE:4 * LANE] = jnp.where(lo, 0.0, k01).astype(BF16)
    ones = jnp.ones((VT_ONES, TMD), BF16)
    vt = _dot_nt(wvat_ref[...], hb)
    for g in range(N_KV_A):
        vat_ref[g * VT_ROWS:g * VT_ROWS + HD_A] = vt[g * HD_A:(g + 1) * HD_A].astype(BF16)
        vat_ref[g * VT_ROWS + HD_A:(g + 1) * VT_ROWS] = ones

    for t in range(3):
        z_ref[:, t * 512:(t + 1) * 512] = mm(P_Z + t * 512, P_Z + (t + 1) * 512).astype(BF16)
    for t in range(6):
        gates_ref[:, t * 512:(t + 1) * 512] = mm(P_GATES + t * 512, P_GATES + (t + 1) * 512).astype(BF16)

    r = mm(P_CQ, P_CQ + Q_RANK_B)
    qn = r * lax.rsqrt(jnp.mean(r * r, axis=-1, keepdims=True) + EPS) * qn_ref[...]
    q = _dot(qn.astype(BF16), wuq_ref[...])
    if rope:
        cb, sb = cb_ref[...], sb_ref[...]
        for h in range(N_HEADS_B):
            seg = _rope(q[:, h * MLA_HW:(h + 1) * MLA_HW], cb, sb) * (MLA_SCALE * LOG2E)
            qb_ref[:, h * MLA_HW:(h + 1) * MLA_HW] = seg.astype(BF16)
    else:
        qb_ref[...] = (q * (MLA_SCALE * LOG2E)).astype(BF16)

    r = mm(P_SMALL, P_SMALL + LANE)
    small_ref[...] = r
    if cache_seq:
        to_cache(ckpe_ref, r[:, S_KPE:S_KPE + QK_ROPE_B])
    kp = _rope(r, cb, sb) if rope else r
    kp = jnp.where((lane >= S_KPE) & (lane < S_KPE + QK_ROPE_B), kp, 0.0)

    r = mm(P_CKV, P_CKV + KV_RANK_B)
    cn = r * lax.rsqrt(jnp.mean(r * r, axis=-1, keepdims=True) + EPS) * kvn_ref[...]
    if cache_seq:
        to_cache(cckv_ref, cn)
    cn16 = cn.astype(BF16)
    kv = _dot(cn16, wukv_ref[...])
    vt = _dot_nt(wvbt_ref[...], cn16)
    for h in range(N_HEADS_B):
        kb_ref[:, h * MLA_HW:(h + 1) * MLA_HW] = (kv[:, h * MLA_HW:(h + 1) * MLA_HW] + kp).astype(BF16)
        vbt_ref[h * VT_ROWS:h * VT_ROWS + V_HD_B] = vt[h * V_HD_B:(h + 1) * V_HD_B].astype(BF16)
        vbt_ref[h * VT_ROWS + V_HD_B:(h + 1) * VT_ROWS] = ones

    for t in range(3):
        cqkv_ref[:, t * 512:(t + 1) * 512] = mm(P_CQKV + t * 512, P_CQKV + (t + 1) * 512).astype(BF16)

    for c in range(TMD // CHUNK):
        abt_ref[c] = _dot_nt(wab_ref[...], hb[c * CHUNK:(c + 1) * CHUNK])


def _layer_spec(arr, l):
    nd = arr.ndim - 1
    return pl.BlockSpec((None,) + arr.shape[1:], lambda *_: (l,) + (0,) * nd, pipeline_mode=pl.Buffered(1))


def _inproj(x2d, l, mod, mod_row_fn, ng, weights, qn, kvn, rope_tabs, tiles_per_seq, cache=None):
    t = x2d.shape[0]
    nt = t // TMD
    rope = rope_tabs is not None
    row = lambda i: (i, 0)
    col = lambda i: (0, i)
    wp, wab, wvat, wuq, wukv, wvbt = weights
    params = (ng, wp, wab, wvat, wuq, wukv, wvbt, qn, kvn)
    in_specs = [pl.BlockSpec((TMD, D_MODEL), row),
                pl.BlockSpec((1, 1, 3 * D_MODEL), lambda i: (mod_row_fn(i), 0, 0))]
    in_specs += [_layer_spec(a, l) for a in params]
    args = [x2d, mod, *params]
    if rope:
        pos = lambda i: (i % tiles_per_seq, 0)
        in_specs += [pl.BlockSpec((TMD, LANE), pos)] * 4
        args += list(rope_tabs)
    outs = [(W_A, BF16, False), (KA_COLS, BF16, False), (VAT_ROWS, BF16, True), (1536, BF16, False),
            (3 * D_MODEL, BF16, False), (HB_COLS, BF16, False), (HB_COLS, BF16, False), (VBT_ROWS, BF16, True),
            (LANE, F32, False), (QKV_C, BF16, False)]
    out_shape = [jax.ShapeDtypeStruct((w, t) if tr else (t, w), dt) for w, dt, tr in outs]
    out_specs = [pl.BlockSpec((w, TMD), col) if tr else pl.BlockSpec((TMD, w), row) for w, _, tr in outs]
    out_shape.append(jax.ShapeDtypeStruct((t // CHUNK, 16, CHUNK), F32))
    out_specs.append(pl.BlockSpec((TMD // CHUNK, 16, CHUNK), lambda i: (i, 0, 0)))
    aliases, cache_seq, n_carry = {}, 0, 0
    if cache is not None:
        depth, cache_seq, carry = cache
        spt = TMD // cache_seq
        for w in (N_KV_A * HD_A, N_KV_A * HD_A, KV_RANK_B, QK_ROPE_B):
            out_shape.append(jax.ShapeDtypeStruct((t // cache_seq, depth, cache_seq, w), F32))
            out_specs.append(pl.BlockSpec((spt, None, cache_seq, w), lambda i: (i, l, 0, 0)))
        if carry is not None:
            n_carry = len(carry)
            for k, a in enumerate(carry):
                aliases[len(args)] = len(out_shape) - n_carry + k
                in_specs.append(pl.BlockSpec(memory_space=pl.ANY))
                args.append(a)
    return pl.pallas_call(
        functools.partial(_inproj_kernel, rope=rope, cache_seq=cache_seq, n_carry=n_carry),
        grid=(nt,),
        in_specs=in_specs,
        out_specs=out_specs,
        out_shape=out_shape,
        input_output_aliases=aliases,
        compiler_params=pltpu.CompilerParams(dimension_semantics=("parallel",)),
        name="inproj_ctx" if cache is not None else "inproj_lat",
    )(*args)


def _kvup_kernel(c_ref, kpe_ref, w_ref, wvt_ref, k_ref, vt_ref):
    c16 = c_ref[...].astype(BF16)
    kv = _dot(c16, w_ref[...])
    vt = _dot_nt(wvt_ref[...], c16)
    kp = kpe_ref[...]
    ones = jnp.ones((VT_ONES, c16.shape[0]), BF16)
    for h in range(N_HEADS_B):
        k_ref[:, h * MLA_HW:(h + 1) * MLA_HW] = (kv[:, h * MLA_HW:(h + 1) * MLA_HW] + kp).astype(BF16)
        vt_ref[h * VT_ROWS:h * VT_ROWS + V_HD_B] = vt[h * V_HD_B:(h + 1) * V_HD_B].astype(BF16)
        vt_ref[h * VT_ROWS + V_HD_B:(h + 1) * VT_ROWS] = ones


def _kvup(ckv, kpe, wukv, wvbt, l):
    nb, _, past, _ = ckv.shape
    return pl.pallas_call(
        _kvup_kernel,
        grid=(nb,),
        in_specs=[pl.BlockSpec((None, None, past, KV_RANK_B), lambda b: (b, l, 0, 0)),
                  pl.BlockSpec((None, None, past, LANE), lambda b: (b, l, 0, 0)),
                  _layer_spec(wukv, l), _layer_spec(wvbt, l)],
        out_specs=[pl.BlockSpec((past, HB_COLS), lambda b: (b, 0)),
                   pl.BlockSpec((None, VBT_ROWS, past), lambda b: (b, 0, 0))],
        out_shape=[jax.ShapeDtypeStruct((nb * past, HB_COLS), BF16),
                   jax.ShapeDtypeStruct((nb, VBT_ROWS, past), BF16)],
        name="mla_cache_up",
    )(ckv, kpe, wukv, wvbt)


def _attend(q_tiles, k_tiles, vt_tiles, bias_t, sink):
    return _softmax_pv(_scores_t(q_tiles, k_tiles, bias_t), vt_tiles, sink)


def _scores_t(q_tiles, k_tiles, bias_t):
    st = _bdot_nt(jnp.stack(k_tiles), jnp.stack(q_tiles))
    return st if bias_t is None else st + bias_t[None]


def _softmax_pv(st, vt_tiles, sink):
    m = jnp.max(st, axis=1, keepdims=True)
    if sink is not None:
        m = jnp.maximum(m, sink)
    ot = _bdot(jnp.stack(vt_tiles), jnp.exp2(st - m).astype(BF16))
    den = ot[:, V_HD_B:V_HD_B + 1, :]
    if sink is not None:
        den = den + jnp.exp2(sink - m)
    num = ot[:, :V_HD_B, :] / den
    return [jnp.concatenate([num[2 * i], num[2 * i + 1]], axis=0).T for i in range(st.shape[0] // 2)]


def _tile(x, t):
    return x[:, t * LANE:(t + 1) * LANE]


def _attn_a_heads(q, ka, vat, bias_t, sink_ref):
    qs, ks, vs = [], [], []
    for t in range(N_HEADS_A // 2):
        g = (2 * t) // GQA_GROUP
        for e in range(2):
            qs.append(_tile(q, t))
            ks.append(_tile(ka, 2 * g + e))
            vs.append(vat[g * VT_ROWS:(g + 1) * VT_ROWS])
    sink = jnp.stack([sink_ref[:, h:h + 1] * LOG2E for h in range(N_HEADS_A)])
    return _attend(qs, ks, vs, bias_t, sink)


def _gated_store(outs, z_ref, o_ref, first_tile=0):
    for i, o in enumerate(outs):
        t = first_tile + i
        z = _tile(z_ref, t).astype(F32)
        o_ref[:, t * LANE:(t + 1) * LANE] = (o * _silu(z)).astype(BF16)


def _attn_a_ctx_kernel(q_ref, ka_ref, vat_ref, z_ref, sink_ref, o_ref):
    _gated_store(_attn_a_heads(q_ref[...], ka_ref[...], vat_ref[...], None, sink_ref), z_ref, o_ref)


def _attn_a_ctx(qa, ka, vat, z, sink, l, seq):
    t = qa.shape[0]
    row = lambda b: (b, 0)
    return pl.pallas_call(
        _attn_a_ctx_kernel,
        grid=(t // seq,),
        in_specs=[pl.BlockSpec((seq, W_A), row),
                  pl.BlockSpec((seq, KA_COLS), row),
                  pl.BlockSpec((VAT_ROWS, seq), lambda b: (0, b)),
                  pl.BlockSpec((seq, W_A), row),
                  _layer_spec(sink, l)],
        out_specs=pl.BlockSpec((seq, W_A), row),
        out_shape=jax.ShapeDtypeStruct((t, W_A), BF16),
        compiler_params=pltpu.CompilerParams(dimension_semantics=("parallel",)),
        name="attn_a_ctx",
    )(qa, ka, vat, z, sink)


def _attn_a_lat_kernel(q_ref, kp_ref, kc_ref, kn_ref, vp_ref, vc_ref, vn_ref, kx_ref, vx_ref, z_ref, sink_ref,
                       o_ref, *, nq):
    j = pl.program_id(1)
    ka = jnp.concatenate([kp_ref[...], kc_ref[...], kn_ref[...], kx_ref[...]], axis=0)
    vat = jnp.concatenate([vp_ref[...], vc_ref[...], vn_ref[...], vx_ref[...]], axis=1)
    n_loc = TM + 2 * WINDOW
    kj = lax.broadcasted_iota(jnp.int32, (ka.shape[0], TM), 0)
    qi = lax.broadcasted_iota(jnp.int32, (ka.shape[0], TM), 1)
    ok = (kj >= qi) & (kj <= qi + 2 * WINDOW)
    ok = ok & ((kj >= WINDOW) | (j > 0)) & ((kj < TM + WINDOW) | (j < nq - 1))
    bias_t = jnp.where(ok | (kj >= n_loc), 0.0, NEG_INF)
    _gated_store(_attn_a_heads(q_ref[...], ka, vat, bias_t, sink_ref), z_ref, o_ref)


def _attn_a_lat(qa, ka, vat, z, sink, kx, vxt, l, seq):
    t = qa.shape[0]
    nq = seq // TM
    past = kx.shape[2]
    r = TM // WINDOW
    row = lambda b, j: (b * nq + j, 0)
    prev = lambda b, j: ((b * nq + j) * r - jnp.where(j > 0, 1, 0), 0)
    nxt = lambda b, j: ((b * nq + j) * r + jnp.where(j < nq - 1, r, r - 1), 0)
    swap = lambda f: (lambda b, j: f(b, j)[::-1])
    return pl.pallas_call(
        functools.partial(_attn_a_lat_kernel, nq=nq),
        grid=(t // seq, nq),
        in_specs=[pl.BlockSpec((TM, W_A), row),
                  pl.BlockSpec((WINDOW, KA_COLS), prev),
                  pl.BlockSpec((TM, KA_COLS), row),
                  pl.BlockSpec((WINDOW, KA_COLS), nxt),
                  pl.BlockSpec((VAT_ROWS, WINDOW), swap(prev)),
                  pl.BlockSpec((VAT_ROWS, TM), swap(row)),
                  pl.BlockSpec((VAT_ROWS, WINDOW), swap(nxt)),
                  pl.BlockSpec((None, None, past, KA_COLS), lambda b, j: (b, l, 0, 0)),
                  pl.BlockSpec((None, None, VAT_ROWS, past), lambda b, j: (b, l, 0, 0)),
                  pl.BlockSpec((TM, W_A), row),
                  _layer_spec(sink, l)],
        out_specs=pl.BlockSpec((TM, W_A), row),
        out_shape=jax.ShapeDtypeStruct((t, W_A), BF16),
        compiler_params=pltpu.CompilerParams(dimension_semantics=("parallel", "parallel")),
        name="attn_a_lat",
    )(qa, ka, ka, ka, vat, vat, vat, kx, vxt, z, sink)


def _attn_b_kernel(*refs, has_ctx, group):
    if has_ctx:
        q_ref, k_ref, vt_ref, kx_ref, vxt_ref, z_ref, o_ref = refs
        kb = jnp.concatenate([k_ref[...], kx_ref[...]], axis=0)
        vbt = jnp.concatenate([vt_ref[...], vxt_ref[...]], axis=1)
    else:
        q_ref, k_ref, vt_ref, z_ref, o_ref = refs
        kb, vbt = k_ref[...], vt_ref[...]
    q = q_ref[...]

    def scores(h0):
        heads = range(h0, h0 + group)
        return _scores_t([_tile(q, h) for h in heads], [_tile(kb, h) for h in heads], None)

    st = scores(0)
    for h0 in range(0, N_HEADS_B, group):
        st_next = scores(h0 + group) if h0 + group < N_HEADS_B else None
        outs = _softmax_pv(st, [vbt[h * VT_ROWS:(h + 1) * VT_ROWS] for h in range(h0, h0 + group)], None)
        _gated_store(outs, z_ref, o_ref, h0 // 2)
        st = st_next


def _attn_b(qb, kb, vbt, z, seq, qblk, group, kx=None, vxt=None):
    t = qb.shape[0]
    nq = seq // qblk
    has_ctx = kx is not None
    hw = HB_COLS
    in_specs = [pl.BlockSpec((qblk, hw), lambda b, j: (b * nq + j, 0)),
                pl.BlockSpec((seq, hw), lambda b, j: (b, 0)),
                pl.BlockSpec((VBT_ROWS, seq), lambda b, j: (0, b))]
    args = [qb, kb, vbt]
    if has_ctx:
        past = kx.shape[0] // (t // seq)
        in_specs += [pl.BlockSpec((past, hw), lambda b, j: (b, 0)),
                     pl.BlockSpec((None, VBT_ROWS, past), lambda b, j: (b, 0, 0))]
        args += [kx, vxt]
    in_specs.append(pl.BlockSpec((qblk, W_B), lambda b, j: (b * nq + j, 1)))
    args.append(z)
    return pl.pallas_call(
        functools.partial(_attn_b_kernel, has_ctx=has_ctx, group=group),
        grid=(t // seq, nq),
        in_specs=in_specs,
        out_specs=pl.BlockSpec((qblk, W_B), lambda b, j: (b * nq + j, 0)),
        out_shape=jax.ShapeDtypeStruct((t, W_B), BF16),
        compiler_params=pltpu.CompilerParams(dimension_semantics=("parallel", "parallel")),
        name="attn_b_lat" if has_ctx else "attn_b_ctx",
    )(*args)


def _gdn_local_kernel(cq_ref, prev_ref, next_ref, small_ref, abt_ref, cw_ref, prow_ref, pcol_ref,
                      u_ref, w_ref, qg_ref, kd_ref, attn_ref, eg_ref, qkv_scr, gb_scr, *, tiles_per_seq):
    tpos = pl.program_id(0) % tiles_per_seq
    x = cq_ref[...].astype(F32)
    prev_row = jnp.where(tpos > 0, prev_ref[...].astype(F32)[HALO - 1:HALO, :], 0.0)
    next_row = jnp.where(tpos < tiles_per_seq - 1, next_ref[...].astype(F32)[0:1, :], 0.0)
    rows = lax.broadcasted_iota(jnp.int32, (TM, 1), 0)
    xm1 = jnp.where(rows == 0, prev_row, pltpu.roll(x, 1, 0))
    xp1 = jnp.where(rows == TM - 1, next_row, pltpu.roll(x, TM - 1, 0))
    cw = cw_ref[...]
    y = _silu(xm1 * cw[0:1] + x * cw[1:2] + xp1 * cw[2:3])
    nq = N_HEADS_C * DK_C
    for h in range(N_HEADS_C):
        qh = y[:, h * DK_C:(h + 1) * DK_C]
        kh = y[:, nq + h * DK_C:nq + (h + 1) * DK_C]
        qkv_scr[:, h * DK_C:(h + 1) * DK_C] = (
            qh * lax.rsqrt(jnp.sum(qh * qh, axis=-1, keepdims=True) + EPS) * (DK_C ** -0.5))
        qkv_scr[:, nq + h * DK_C:nq + (h + 1) * DK_C] = kh * lax.rsqrt(jnp.sum(kh * kh, axis=-1, keepdims=True) + EPS)
    qkv_scr[:, 2 * nq:] = y[:, 2 * nq:]

    sm = small_ref[...]
    prow = prow_ref[...]
    gb_scr[:, 0:8] = -jnp.exp(prow[0:1]) * _softplus(sm[:, S_A:S_A + 8] + prow[1:2])
    gb_scr[:, 8:16] = _sigmoid(sm[:, S_B:S_B + 8])
    pcol = pcol_ref[...]

    ri = lax.broadcasted_iota(jnp.int32, (CHUNK, LANE), 0)
    lane = lax.broadcasted_iota(jnp.int32, (CHUNK, LANE), 1)
    fwd = lane < CHUNK
    cj = lane & (CHUNK - 1)
    incl = (fwd & (ri >= cj)) | (~fwd & (ri <= cj))
    strict = (fwd & (ri > cj)) | (~fwd & (ri < cj))
    xor = ri ^ cj
    eye = (ri == cj).astype(F32)
    r2 = lax.broadcasted_iota(jnp.int32, (2 * CHUNK, LANE), 0)
    l2 = lax.broadcasted_iota(jnp.int32, (2 * CHUNK, LANE), 1)
    same_dir = (r2 < CHUNK) == (l2 < CHUNK)
    rs_ = lax.broadcasted_iota(jnp.int32, (CHUNK, CHUNK), 0)
    cs_ = lax.broadcasted_iota(jnp.int32, (CHUNK, CHUNK), 1)
    tril = (rs_ >= cs_).astype(F32)
    triu = (rs_ <= cs_).astype(F32)
    tri_rows = jnp.concatenate([triu, tril], axis=1)

    def block_diag(x):
        return jnp.where(same_dir[None], jnp.concatenate([x, x], axis=1), 0.0).astype(BF16)

    lows, rhss, order = [], [], []

    def solve():
        low = jnp.stack(lows, axis=0)
        inv = eye[None] - jnp.where(xor[None] == 1, low, 0.0)
        b = 2
        while b < CHUNK:
            cpl = jnp.where((xor[None] >= b) & (xor[None] < 2 * b), low, 0.0)
            tmp = _bdot(cpl.astype(BF16), block_diag(inv))
            inv = inv - _bdot(inv.astype(BF16), block_diag(tmp))
            b *= 2
        rhs = jnp.stack(rhss, axis=0)
        uw = (_bdot(jnp.where(fwd[None], inv, 0.0).astype(BF16), rhs),
              _bdot(jnp.where(fwd[None], 0.0, inv).astype(BF16), rhs))
        for i, (rs, h) in enumerate(order):
            for d in range(2):
                cs = slice((d * N_HEADS_C + h) * DK_C, (d * N_HEADS_C + h + 1) * DK_C)
                u_ref[rs, cs] = uw[d][i, :, :DV_C]
                w_ref[rs, cs] = uw[d][i, :, DV_C:].astype(BF16)
        lows.clear(), rhss.clear(), order.clear()

    for c in range(CPT):
        if c % SOLVE_CHUNKS == 0 and c > 0:
            solve()
        rs = slice(c * CHUNK, (c + 1) * CHUNK)
        gcol = gb_scr[rs, 0:8]
        bcol = gb_scr[rs, 8:16]
        abt = abt_ref[c]
        grow = -jnp.exp(pcol[:, 0:1]) * _softplus(abt[0:8] + pcol[:, 1:2])
        gc_f = _dot_exact(tril, gcol)
        gc_b = _dot_exact(triu, gcol)
        gr = _dot_exact(grow, tri_rows)
        for h in range(N_HEADS_C):
            hb_ = N_HEADS_C + h
            q = qkv_scr[rs, h * DK_C:(h + 1) * DK_C]
            k = qkv_scr[rs, nq + h * DK_C:nq + (h + 1) * DK_C]
            v = qkv_scr[rs, 2 * nq + h * DV_C:2 * nq + (h + 1) * DV_C]
            k16 = k.astype(BF16)
            kk16 = jnp.concatenate([k16, k16], axis=0)
            kk = _dot_nt(k16, kk16)
            qk = _dot_nt(q.astype(BF16), kk16)
            gcs = (gc_f[:, h:h + 1], gc_b[:, hb_:hb_ + 1])
            betas = (bcol[:, h:h + 1], bcol[:, hb_:hb_ + 1])
            gc2 = jnp.where(fwd, gcs[0], gcs[1])
            gr2 = jnp.where(fwd[0:1], gr[h:h + 1, :], gr[hb_:hb_ + 1, :])
            decay = jnp.where(incl, jnp.exp(jnp.where(incl, gc2 - gr2, 0.0)), 0.0)
            lows.append(jnp.where(strict, jnp.where(fwd, betas[0], betas[1]) * kk * decay, 0.0))
            attn_ref[rs, h * LANE:(h + 1) * LANE] = (qk * decay).astype(BF16)
            order.append((rs, h))
            halves = []
            for d in range(2):
                dh = d * N_HEADS_C + h
                gc, beta = gcs[d], betas[d]
                eg = jnp.exp(gc)
                halves.append(jnp.concatenate([v * beta, k * (beta * eg)], axis=-1).astype(BF16))
                g_last = gc[CHUNK - 1:CHUNK] if d == 0 else gc[0:1]
                cs = slice(dh * DK_C, (dh + 1) * DK_C)
                qg_ref[rs, cs] = (q * eg).astype(BF16)
                kd_ref[rs, cs] = (k * jnp.exp(g_last - gc)).astype(BF16)
                eg_ref[c, dh:dh + 1, :] = jnp.broadcast_to(jnp.exp(g_last), (1, LANE))
            rhss.append(jnp.concatenate(halves, axis=0))
    solve()


def _gdn_local(cqkv, small, abt, conv_w, prow, pcol, l, tiles_per_seq):
    t = cqkv.shape[0]
    nt = t // TM
    nh8 = t // HALO
    row = lambda i: (i, 0)
    dh = 2 * N_HEADS_C
    return pl.pallas_call(
        functools.partial(_gdn_local_kernel, tiles_per_seq=tiles_per_seq),
        grid=(nt,),
        in_specs=[pl.BlockSpec((TM, QKV_C), row),
                  pl.BlockSpec((HALO, QKV_C), lambda i: (jnp.maximum(i * (TM // HALO) - 1, 0), 0)),
                  pl.BlockSpec((HALO, QKV_C), lambda i: (jnp.minimum((i + 1) * (TM // HALO), nh8 - 1), 0)),
                  pl.BlockSpec((TM, LANE), row),
                  pl.BlockSpec((CPT, 16, CHUNK), lambda i: (i, 0, 0)),
                  _layer_spec(conv_w, l), _layer_spec(prow, l), _layer_spec(pcol, l)],
        out_specs=[pl.BlockSpec((TM, dh * DV_C), row),
                   pl.BlockSpec((TM, dh * DK_C), row),
                   pl.BlockSpec((TM, dh * DK_C), row),
                   pl.BlockSpec((TM, dh * DK_C), row),
                   pl.BlockSpec((TM, dh * CHUNK), row),
                   pl.BlockSpec((CPT, dh, LANE), lambda i: (i, 0, 0))],
        out_shape=[jax.ShapeDtypeStruct((t, dh * DV_C), F32),
                   jax.ShapeDtypeStruct((t, dh * DK_C), BF16),
                   jax.ShapeDtypeStruct((t, dh * DK_C), BF16),
                   jax.ShapeDtypeStruct((t, dh * DK_C), BF16),
                   jax.ShapeDtypeStruct((t, dh * CHUNK), BF16),
                   jax.ShapeDtypeStruct((t // CHUNK, dh, LANE), F32)],
        scratch_shapes=[pltpu.VMEM((TM, QKV_C), F32), pltpu.VMEM((TM, 16), F32)],
        compiler_params=pltpu.CompilerParams(dimension_semantics=("parallel",)),
        name="gdn_local",
    )(cqkv, cqkv, cqkv, small, abt, conv_w, prow, pcol)


def _gdn_scan_kernel(*refs, nt, ns, has_init, want_state, n_carry):
    it = iter(refs)
    ins = [[next(it) for _ in range(6)] for _ in range(2)]
    s0_ref = next(it) if has_init else None
    for _ in range(n_carry):
        next(it)
    o_refs = [next(it), next(it)]
    st_ref = next(it) if want_state else None
    s_scr = next(it)
    j = pl.program_id(1)
    nh = N_HEADS_C
    nst = ns * 2 * nh

    @pl.when(j == 0)
    def _():
        if has_init:
            s_scr[...] = s0_ref[...].reshape(nst, DK_C, DV_C)
        else:
            s_scr[...] = jnp.zeros_like(s_scr)

    for step in range(CPT):
        chunk = lambda d: step if d == 0 else CPT - 1 - step

        def gather(idx, width):
            return jnp.stack([ins[d][idx][s, chunk(d) * CHUNK:(chunk(d) + 1) * CHUNK, h * width:(h + 1) * width]
                              for s in range(ns) for d in range(2) for h in range(nh)])

        u, w, qg, kd, attn = gather(0, DV_C), gather(1, DK_C), gather(2, DK_C), gather(3, DK_C), gather(4, LANE)
        eg = jnp.stack([ins[d][5][s, chunk(d), d * nh + h:d * nh + h + 1, :]
                        for s in range(ns) for d in range(2) for h in range(nh)])
        st = s_scr[...]
        sb = st.astype(BF16)
        v_new = u - _bdot(w, sb)
        vb = v_new.astype(BF16)
        zeros = jnp.zeros((CHUNK, DV_C), BF16)
        vb2 = jnp.stack([jnp.concatenate([vb[i], zeros] if (i // nh) % 2 == 0 else [zeros, vb[i]], axis=0)
                         for i in range(nst)])
        o = _bdot(qg, sb) + _bdot(attn, vb2)
        s_scr[...] = st * eg + _bdot_tn(kd, vb)
        for s in range(ns):
            for d in range(2):
                for h in range(nh):
                    o_refs[d][s, chunk(d) * CHUNK:(chunk(d) + 1) * CHUNK, h * DV_C:(h + 1) * DV_C] = (
                        o[(s * 2 + d) * nh + h])

    if want_state:
        @pl.when(j == nt - 1)
        def _():
            st_ref[...] = s_scr[...].reshape(ns, 2, nh, DK_C, DV_C)


def _gdn_scan(u, w, qg, kd, attn, eg, s0, l, seq, state_out=None):
    t = u.shape[0]
    want_state = state_out is not None
    nt = seq // TM
    nb = t // seq
    ns = next(n for n in (SCAN_SEQS, 2, 1) if nb % n == 0)
    half = N_HEADS_C * DK_C
    has_init = s0 is not None
    by_seq = lambda a: a.reshape((nb, a.shape[0] // nb) + a.shape[1:])
    in_specs, args = [], []
    for d in range(2):
        tile = (lambda b, j: j) if d == 0 else (lambda b, j: nt - 1 - j)
        row = lambda b, j, d=d, tile=tile: (b, tile(b, j), d)
        row4 = lambda b, j, tile=tile: (b, tile(b, j), 0, 0)
        in_specs += [pl.BlockSpec((ns, TM, half), row)] * 4
        in_specs += [pl.BlockSpec((ns, TM, N_HEADS_C * LANE), lambda b, j, tile=tile: (b, tile(b, j), 0)),
                     pl.BlockSpec((ns, CPT, 2 * N_HEADS_C, LANE), row4)]
        args += [by_seq(a) for a in (u, w, qg, kd, attn, eg)]
    st_tail = (2, N_HEADS_C, DK_C, DV_C)
    if has_init:
        in_specs.append(pl.BlockSpec((ns, None) + st_tail, lambda b, j: (b, l, 0, 0, 0, 0)))
        args.append(s0)
    out_specs = [pl.BlockSpec((ns, TM, half), lambda b, j: (b, j, 0)),
                 pl.BlockSpec((ns, TM, half), lambda b, j: (b, nt - 1 - j, 0))]
    out_shape = [jax.ShapeDtypeStruct((nb, seq, half), F32)] * 2
    aliases, n_carry = {}, 0
    if want_state:
        depth, carry = state_out
        out_specs.append(pl.BlockSpec((ns, None) + st_tail, lambda b, j: (b, l, 0, 0, 0, 0)))
        out_shape.append(jax.ShapeDtypeStruct((nb, depth) + st_tail, F32))
        if carry is not None:
            n_carry = 1
            aliases[len(args)] = 2
            in_specs.append(pl.BlockSpec(memory_space=pl.ANY))
            args.append(carry)
    outs = pl.pallas_call(
        functools.partial(_gdn_scan_kernel, nt=nt, ns=ns, has_init=has_init, want_state=want_state,
                          n_carry=n_carry),
        grid=(nb // ns, nt),
        in_specs=in_specs,
        out_specs=out_specs,
        out_shape=out_shape,
        input_output_aliases=aliases,
        scratch_shapes=[pltpu.VMEM((ns * 2 * N_HEADS_C, DK_C, DV_C), F32)],
        compiler_params=pltpu.CompilerParams(dimension_semantics=("parallel", "arbitrary")),
        name="gdn_scan",
    )(*args)
    return [outs[0].reshape(t, half), outs[1].reshape(t, half)] + list(outs[2:])


def _merge_kernel(x_ref, mod_ref, oa_ref, ob_ref, cf_ref, cb_ref, zc_ref, gates_ref, gn_ref,
                  wa_ref, wb_ref, wc_ref, wo_ref, fg_ref, o_ref, *, last):
    oc = cf_ref[...] + cb_ref[...]
    zc = zc_ref[...].astype(F32)
    gn = gn_ref[...]
    parts = []
    for h in range(N_HEADS_C):
        hs = slice(h * DV_C, (h + 1) * DV_C)
        och = oc[:, hs]
        och = och * lax.rsqrt(jnp.mean(och * och, axis=-1, keepdims=True) + EPS) * gn
        parts.append((och * _silu(zc[:, hs])).astype(BF16))
    ocz = jnp.concatenate(parts, axis=-1)
    pa = _dot(oa_ref[...], wa_ref[...])
    pb = _dot(ob_ref[...], wb_ref[...])
    pc = _dot(ocz, wc_ref[...])
    ga = _sigmoid(gates_ref[:, 0:D_MODEL].astype(F32))
    gb = _sigmoid(gates_ref[:, D_MODEL:2 * D_MODEL].astype(F32))
    gc = _sigmoid(gates_ref[:, 2 * D_MODEL:].astype(F32))
    y = _dot((ga * pa + gb * pb + gc * pc).astype(BF16), wo_ref[...])
    gate = mod_ref[0][:, 2 * D_MODEL:]
    xo = x_ref[...] + gate * y
    if last:
        xo = xo * lax.rsqrt(jnp.mean(xo * xo, axis=-1, keepdims=True) + EPS) * fg_ref[...]
    o_ref[...] = xo


def _merge(x2d, l, mod, mod_row_fn, oa, ob, cf, cb, z, gates, gn, wa, wb, wc, wo, fg, last):
    t = x2d.shape[0]
    row = lambda i: (i, 0)
    return pl.pallas_call(
        functools.partial(_merge_kernel, last=last),
        grid=(t // TMD,),
        in_specs=[pl.BlockSpec((TMD, D_MODEL), row),
                  pl.BlockSpec((1, 1, 3 * D_MODEL), lambda i: (mod_row_fn(i), 0, 0)),
                  pl.BlockSpec((TMD, W_A), row),
                  pl.BlockSpec((TMD, W_B), row),
                  pl.BlockSpec((TMD, W_C), row),
                  pl.BlockSpec((TMD, W_C), row),
                  pl.BlockSpec((TMD, W_C), lambda i: (i, 2)),
                  pl.BlockSpec((TMD, 3 * D_MODEL), row),
                  _layer_spec(gn, l), _layer_spec(wa, l), _layer_spec(wb, l), _layer_spec(wc, l),
                  _layer_spec(wo, l),
                  pl.BlockSpec((1, D_MODEL), lambda i: (0, 0))],
        out_specs=pl.BlockSpec((TMD, D_MODEL), row),
        out_shape=jax.ShapeDtypeStruct((t, D_MODEL), F32),
        compiler_params=pltpu.CompilerParams(dimension_semantics=("parallel",)),
        name="merge",
    )(x2d, mod, oa, ob, cf, cb, z, gates, gn, wa, wb, wc, wo, fg)


def _rope_tables(n_tokens, rot_dim):
    rows = n_tokens // GRID_W
    row = np.repeat(np.arange(rows), GRID_W).astype(np.float32)
    col = np.tile(np.arange(GRID_W), rows).astype(np.float32)
    n_pairs = rot_dim // 4
    inv = (np.float32(ROPE_BASE) ** (-np.arange(n_pairs, dtype=np.float32) / np.float32(n_pairs))).astype(np.float32)
    ang = np.concatenate([row[:, None] * inv, col[:, None] * inv], axis=-1)
    c, s = np.cos(ang), np.sin(ang)
    return np.repeat(c, 2, axis=-1), np.stack([-s, s], axis=-1).reshape(n_tokens, rot_dim)


def _in_offsets():
    o = [0]
    for n in IN_SIZES:
        o.append(o[-1] + n)
    return o


def _relayout_moves():
    o = _in_offsets()
    order = [(o[0], W_A), (o[1], LANE), (o[1] + HD_A, HD_A), (o[1], HD_A), (o[2], LANE),
             (o[3], W_A), (o[7], W_B), (o[11], W_C), (o[12], 3 * D_MODEL), (o[4], Q_RANK_B), (o[5], KV_RANK_B),
             (None, S_KPE), (o[6], QK_ROPE_B), (o[9], 4 * N_HEADS_C), (None, LANE - S_B - 8), (o[8], QKV_C)]
    moves, dst = [], 0
    for src, n in order:
        moves.append((src, dst, n))
        dst += n
    assert dst == P_END
    return moves


def _relayout_kernel(w_ref, o_ref, wab_ref, wvat_ref):
    for src, dst, n in _relayout_moves():
        if src is None:
            o_ref[dst:dst + n, :] = jnp.zeros((n, o_ref.shape[1]), BF16)
        else:
            o_ref[dst:dst + n, :] = w_ref[src:src + n, :].astype(BF16)
    o = _in_offsets()
    wab_ref[...] = w_ref[o[9]:o[11], :].astype(BF16)
    wvat_ref[...] = w_ref[o[2]:o[3], :].astype(BF16)


def _relayout_w_in(w_in):
    w_t = jnp.swapaxes(w_in, 1, 2)
    depth, width, _ = w_t.shape
    cols = 128
    nab, nv = 4 * N_HEADS_C, N_KV_A * HD_A
    col = lambda l, i: (l, 0, i)
    return pl.pallas_call(
        _relayout_kernel,
        grid=(depth, D_MODEL // cols),
        in_specs=[pl.BlockSpec((None, width, cols), col)],
        out_specs=[pl.BlockSpec((None, P_END, cols), col),
                   pl.BlockSpec((None, nab, cols), col),
                   pl.BlockSpec((None, nv, cols), col)],
        out_shape=[jax.ShapeDtypeStruct((depth, P_END, D_MODEL), BF16),
                   jax.ShapeDtypeStruct((depth, nab, D_MODEL), BF16),
                   jax.ShapeDtypeStruct((depth, nv, D_MODEL), BF16)],
        name="w_in_relayout",
    )(w_t)


def _prep_weights(w_in, w_uq, w_ukv):
    depth = w_in.shape[0]
    wp, wab, wvat = _relayout_w_in(w_in)
    hd = QK_NOPE_B + QK_ROPE_B
    wuq = jnp.pad(w_uq.reshape(depth, Q_RANK_B, N_HEADS_B, hd), ((0, 0), (0, 0), (0, 0), (0, MLA_HW - hd)))
    wuq = wuq.reshape(depth, Q_RANK_B, HB_COLS).astype(BF16)
    kv = w_ukv.reshape(depth, KV_RANK_B, N_HEADS_B, QK_NOPE_B + V_HD_B)
    wk = jnp.pad(kv[..., :QK_NOPE_B], ((0, 0), (0, 0), (0, 0), (0, MLA_HW - QK_NOPE_B)))
    wukv = wk.reshape(depth, KV_RANK_B, HB_COLS).astype(BF16)
    wvbt = jnp.swapaxes(kv[..., QK_NOPE_B:].reshape(depth, KV_RANK_B, W_B), 1, 2).astype(BF16)
    return wp, wab, wvat, wuq, wukv, wvbt


def _cache_tiles_a(kx, vx):
    k0, k1 = kx[..., 0, :], kx[..., 1, :]
    z = jnp.zeros_like(k0)
    ka = jnp.concatenate([k0, z, z, k0, k1, z, z, k1], axis=-1).astype(BF16)
    vt = jnp.transpose(vx, (0, 1, 3, 4, 2))
    vt = jnp.concatenate([vt, jnp.ones(vt.shape[:3] + (VT_ONES, vt.shape[4]), vt.dtype)], axis=3)
    return ka, vt.reshape(vt.shape[:2] + (VAT_ROWS, vt.shape[4])).astype(BF16)


def kernel(x_prompt, x_sample, cache_attn_k, cache_attn_v, cache_mla_ckv, cache_mla_kpe, state_gdn, c, c_ctx,
           norm_g, w_ada, b_ada, w_in, attn_sink, mla_q_norm, mla_w_uq, mla_kv_norm, mla_w_ukv, gdn_conv,
           gdn_a_log, gdn_dt_bias, gdn_norm, w_branch_a, w_branch_b, w_branch_c, w_out, final_norm_g):
    depth = w_in.shape[0]
    nb_c, seq_c, _ = x_prompt.shape
    nb_l, seq_l, _ = x_sample.shape
    past = cache_attn_k.shape[2]
    assert P_END % LANE == 0 and seq_c % TM == 0 and seq_l % TMD == 0 and nb_l < 8 and TM == 2 * WINDOW
    assert (nb_c * seq_c) % TMD == 0 and TMD % TM == 0

    cond8 = jnp.zeros((8, D_MODEL), F32).at[:nb_l].set(c).at[nb_l].set(c_ctx)
    mod = _modulation(cond8, w_ada, b_ada).reshape(depth * 8, 1, 3 * D_MODEL)

    c_a, s_a = _rope_tables(seq_l, HD_A)
    c_b, s_b = _rope_tables(seq_l, QK_ROPE_B)
    pad_l, pad_r = S_KPE, LANE - S_KPE - QK_ROPE_B
    one, zero = np.ones((seq_l, 1), np.float32), np.zeros((seq_l, 1), np.float32)
    rope_tabs = tuple(jnp.asarray(a) for a in (
        np.tile(c_a, (1, LANE // HD_A)), np.tile(s_a, (1, LANE // HD_A)),
        np.concatenate([np.tile(one, (1, pad_l)), c_b, np.tile(one, (1, pad_r))], 1),
        np.concatenate([np.tile(zero, (1, pad_l)), s_b, np.tile(zero, (1, pad_r))], 1)))

    weights = _prep_weights(w_in, mla_w_uq, mla_w_ukv)
    wukv, wvbt = weights[4], weights[5]
    ng = norm_g.reshape(depth, 1, D_MODEL)
    qn = mla_q_norm.reshape(depth, 1, Q_RANK_B)
    kvn = mla_kv_norm.reshape(depth, 1, KV_RANK_B)
    sink = attn_sink.reshape(depth, 1, N_HEADS_A)
    prow = jnp.stack([gdn_a_log.reshape(depth, -1), gdn_dt_bias.reshape(depth, -1)], axis=1)
    pcol = jnp.swapaxes(prow, 1, 2)
    gn = gdn_norm.reshape(depth, 1, DV_C)
    wa, wb, wc, wo = (w.astype(BF16) for w in (w_branch_a, w_branch_b, w_branch_c, w_out))
    fg = final_norm_g.reshape(1, D_MODEL)
    kxa, vxa = _cache_tiles_a(cache_attn_k, cache_attn_v)
    kpex = jnp.pad(cache_mla_kpe, ((0, 0), (0, 0), (0, 0), (pad_l, pad_r)))

    tps_c, tps_l = seq_c // TM, seq_l // TM
    tpd_l = seq_l // TMD
    y_p = x_prompt.reshape(nb_c * seq_c, D_MODEL)
    y_s = x_sample.reshape(nb_l * seq_l, D_MODEL)
    new_cache, new_state = None, None
    for l in range(depth):
        last = l == depth - 1

        mod_row_c = lambda i, l=l: l * 8 + nb_l
        outs = _inproj(y_p, l, mod, mod_row_c, ng, weights, qn, kvn, None, tps_c, (depth, seq_c, new_cache))
        (qa, ka, vat, z, gates, qb, kb, vbt, small, cqkv, abt), new_cache = outs[:11], tuple(outs[11:])
        oa = _attn_a_ctx(qa, ka, vat, z, sink, l, seq_c)
        ob = _attn_b(qb, kb, vbt, z, seq_c, seq_c, N_HEADS_B)
        u, w, qg, kd, attn, eg = _gdn_local(cqkv, small, abt, gdn_conv, prow, pcol, l, tps_c)
        cf, cb, new_state = _gdn_scan(u, w, qg, kd, attn, eg, None, l, seq_c, (depth, new_state))
        y_p = _merge(y_p, l, mod, mod_row_c, oa, ob, cf, cb, z, gates, gn, wa, wb, wc, wo, fg, last)

        mod_row_l = lambda i, l=l: l * 8 + i // tpd_l
        (qa, ka, vat, z, gates, qb, kb, vbt, small, cqkv, abt) = _inproj(
            y_s, l, mod, mod_row_l, ng, weights, qn, kvn, rope_tabs, tpd_l)
        oa = _attn_a_lat(qa, ka, vat, z, sink, kxa, vxa, l, seq_l)
        kxb, vxb = _kvup(cache_mla_ckv, kpex, wukv, wvbt, l)
        ob = _attn_b(qb, kb, vbt, z, seq_l, TM, 4, kxb, vxb)
        u, w, qg, kd, attn, eg = _gdn_local(cqkv, small, abt, gdn_conv, prow, pcol, l, tps_l)
        cf, cb = _gdn_scan(u, w, qg, kd, attn, eg, state_gdn, l, seq_l)
        y_s = _merge(y_s, l, mod, mod_row_l, oa, ob, cf, cb, z, gates, gn, wa, wb, wc, wo, fg, last)

    new_k, new_v, new_ckv, new_kpe = new_cache
    kv_shape = (nb_c, depth, seq_c, N_KV_A, HD_A)
    return (y_p.reshape(nb_c, seq_c, D_MODEL), y_s.reshape(nb_l, seq_l, D_MODEL),
            new_k.reshape(kv_shape), new_v.reshape(kv_shape), new_ckv, new_kpe, new_state)
```

```python
import functools

import numpy as np
import jax
import jax.numpy as jnp
from jax import lax
from jax.experimental import pallas as pl
from jax.experimental.pallas import tpu as pltpu

F32 = jnp.float32
BF16 = jnp.bfloat16

D_MODEL = 1024
GRID_W = 64
ROPE_BASE = 10000.0
EPS = 1e-6
NEG_INF = -1e30
N_HEADS_A = 8
N_KV_A = 2
HD_A = 64
GQA_GROUP = N_HEADS_A // N_KV_A
WINDOW = 128
N_HEADS_B = 8
QK_NOPE_B = 64
QK_ROPE_B = 32
V_HD_B = 64
Q_RANK_B = 384
KV_RANK_B = 256
MLA_SCALE = (QK_NOPE_B + QK_ROPE_B) ** -0.5
N_HEADS_C = 4
DK_C = 128
DV_C = 128
CHUNK = 64
W_A = N_HEADS_A * HD_A
W_B = N_HEADS_B * V_HD_B
W_C = N_HEADS_C * DV_C
QKV_C = 2 * N_HEADS_C * DK_C + W_C
IN_SIZES = (W_A, N_KV_A * HD_A, N_KV_A * HD_A, W_A, Q_RANK_B, KV_RANK_B, QK_ROPE_B, W_B, QKV_C,
            2 * N_HEADS_C, 2 * N_HEADS_C, W_C, 3 * D_MODEL)

LANE = 128
HALF = LANE // 2
TM = 256
TMD = 512
CPT = TM // CHUNK
SOLVE_CHUNKS = 4
SCAN_SEQS = 4
HALO = 16
MLA_HW = 128
KA_COLS = 4 * LANE
HB_COLS = N_HEADS_B * MLA_HW
VT_ONES = 16
VT_ROWS = V_HD_B + VT_ONES
VAT_ROWS = N_KV_A * VT_ROWS
VBT_ROWS = N_HEADS_B * VT_ROWS
LOG2E = 1.4426950408889634

P_QKV = 0
A_COLS = W_A + 3 * LANE
P_Z = P_QKV + A_COLS
P_GATES = P_Z + 1536
P_CQ = P_GATES + 3 * D_MODEL
P_CKV = P_CQ + Q_RANK_B
P_SMALL = P_CKV + KV_RANK_B
P_CQKV = P_SMALL + LANE
P_END = P_CQKV + QKV_C
S_KPE = 64
S_A = 96
S_B = 104


def _sigmoid(x):
    return 0.5 * jnp.tanh(0.5 * x) + 0.5


def _silu(x):
    return x * _sigmoid(x)


def _softplus(x):
    return jnp.maximum(x, 0.0) + jnp.log(1.0 + jnp.exp(-jnp.abs(x)))


def _dot(a, b):
    return jnp.dot(a, b, preferred_element_type=F32)


def _dot_nt(a, b):
    return lax.dot_general(a, b, (((1,), (1,)), ((), ())), preferred_element_type=F32)


def _bdot(a, b):
    return lax.dot_general(a, b, (((2,), (1,)), ((0,), (0,))), preferred_element_type=F32)


def _bdot_nt(a, b):
    return lax.dot_general(a, b, (((2,), (2,)), ((0,), (0,))), preferred_element_type=F32)


def _bdot_tn(a, b):
    return lax.dot_general(a, b, (((1,), (1,)), ((0,), (0,))), preferred_element_type=F32)


def _dot_exact(a, b):
    return jnp.dot(a, b, preferred_element_type=F32, precision=lax.Precision.HIGHEST)


def _rope(x, c, s):
    n = x.shape[-1]
    lane = lax.broadcasted_iota(jnp.int32, x.shape, 1)
    swapped = jnp.where(lane % 2 == 0, pltpu.roll(x, n - 1, 1), pltpu.roll(x, 1, 1))
    return x * c + swapped * s


def _mod_kernel(cond_ref, w_ref, b_ref, out_ref):
    cnd = cond_ref[...]
    out_ref[0] = _dot(_silu(cnd).astype(BF16), w_ref[0].astype(BF16)) + b_ref[0]


def _modulation(cond8, w_ada, b_ada):
    depth = w_ada.shape[0]
    tn = 768
    return pl.pallas_call(
        _mod_kernel,
        grid=(depth, 3 * D_MODEL // tn),
        in_specs=[pl.BlockSpec((8, D_MODEL), lambda l, n: (0, 0)),
                  pl.BlockSpec((1, D_MODEL, tn), lambda l, n: (l, 0, n)),
                  pl.BlockSpec((1, 1, tn), lambda l, n: (l, 0, n))],
        out_specs=pl.BlockSpec((1, 8, tn), lambda l, n: (l, 0, n)),
        out_shape=jax.ShapeDtypeStruct((depth, 8, 3 * D_MODEL), F32),
        name="adaln_mod",
    )(cond8, w_ada, b_ada.reshape(depth, 1, 3 * D_MODEL))


def _inproj_kernel(*refs, rope, cache_seq, n_carry):
    it = iter(refs)
    (x_ref, mod_ref, ng_ref, wp_ref, wab_ref, wvat_ref, wuq_ref, wukv_ref, wvbt_ref, qn_ref,
     kvn_ref) = (next(it) for _ in range(11))
    if rope:
        ca_ref, sa_ref, cb_ref, sb_ref = (next(it) for _ in range(4))
    for _ in range(n_carry):
        next(it)
    (qa_ref, ka_ref, vat_ref, z_ref, gates_ref, qb_ref, kb_ref, vbt_ref, small_ref, cqkv_ref,
     abt_ref) = (next(it) for _ in range(11))
    if cache_seq:
        ck_ref, cv_ref, cckv_ref, ckpe_ref = (next(it) for _ in range(4))

    def to_cache(ref, val):
        for s in range(TMD // cache_seq):
            ref[s] = val[s * cache_seq:(s + 1) * cache_seq]

    x = x_ref[...]
    mod = mod_ref[0]
    shift, scale = mod[:, :D_MODEL], mod[:, D_MODEL:2 * D_MODEL]
    xn = x * lax.rsqrt(jnp.mean(x * x, axis=-1, keepdims=True) + EPS) * ng_ref[...]
    hb = (xn * (1.0 + scale) + shift).astype(BF16)
    lane = lax.broadcasted_iota(jnp.int32, (TMD, LANE), 1)
    lo = lane < HALF

    def mm(lo_col, hi_col):
        return _dot_nt(hb, wp_ref[lo_col:hi_col, :])

    r = mm(P_QKV, P_QKV + A_COLS)
    tiles = [r[:, t * LANE:(t + 1) * LANE] for t in range(A_COLS // LANE)]
    if cache_seq:
        to_cache(ck_ref, tiles[4])
        to_cache(cv_ref, tiles[6])
    if rope:
        ca, sa = ca_ref[...], sa_ref[...]
        tiles[:6] = [_rope(t, ca, sa) for t in tiles[:6]]
    for t in range(4):
        qa_ref[:, t * LANE:(t + 1) * LANE] = (tiles[t] * (HD_A ** -0.5 * LOG2E)).astype(BF16)
    k01, k10 = tiles[4], tiles[5]
    ka_ref[:, 0 * LANE:1 * LANE] = jnp.where(lo, k01, 0.0).astype(BF16)
    ka_ref[:, 1 * LANE:2 * LANE] = jnp.where(lo, 0.0, k10).astype(BF16)
    ka_ref[:, 2 * LANE:3 * LANE] = jnp.where(lo, k10, 0.0).astype(BF16)
    ka_ref[:, 3 * LANE:4 * LANE] = jnp.where(lo, 0.0, k01).astype(BF16)
    ones = jnp.ones((VT_ONES, TMD), BF16)
    vt = _dot_nt(wvat_ref[...], hb)
    for g in range(N_KV_A):
        vat_ref[g * VT_ROWS:g * VT_ROWS + HD_A] = vt[g * HD_A:(g + 1) * HD_A].astype(BF16)
        vat_ref[g * VT_ROWS + HD_A:(g + 1) * VT_ROWS] = ones

    for t in range(3):
        z_ref[:, t * 512:(t + 1) * 512] = mm(P_Z + t * 512, P_Z + (t + 1) * 512).astype(BF16)
    for t in range(6):
        gates_ref[:, t * 512:(t + 1) * 512] = mm(P_GATES + t * 512, P_GATES + (t + 1) * 512).astype(BF16)

    r = mm(P_CQ, P_CQ + Q_RANK_B)
    qn = r * lax.rsqrt(jnp.mean(r * r, axis=-1, keepdims=True) + EPS) * qn_ref[...]
    q = _dot(qn.astype(BF16), wuq_ref[...])
    if rope:
        cb, sb = cb_ref[...], sb_ref[...]
        for h in range(N_HEADS_B):
            seg = _rope(q[:, h * MLA_HW:(h + 1) * MLA_HW], cb, sb) * (MLA_SCALE * LOG2E)
            qb_ref[:, h * MLA_HW:(h + 1) * MLA_HW] = seg.astype(BF16)
    else:
        qb_ref[...] = (q * (MLA_SCALE * LOG2E)).astype(BF16)

    r = mm(P_SMALL, P_SMALL + LANE)
    small_ref[...] = r
    if cache_seq:
        to_cache(ckpe_ref, r[:, S_KPE:S_KPE + QK_ROPE_B])
    kp = _rope(r, cb, sb) if rope else r
    kp = jnp.where((lane >= S_KPE) & (lane < S_KPE + QK_ROPE_B), kp, 0.0)

    r = mm(P_CKV, P_CKV + KV_RANK_B)
    cn = r * lax.rsqrt(jnp.mean(r * r, axis=-1, keepdims=True) + EPS) * kvn_ref[...]
    if cache_seq:
        to_cache(cckv_ref, cn)
    cn16 = cn.astype(BF16)
    kv = _dot(cn16, wukv_ref[...])
    vt = _dot_nt(wvbt_ref[...], cn16)
    for h in range(N_HEADS_B):
        kb_ref[:, h * MLA_HW:(h + 1) * MLA_HW] = (kv[:, h * MLA_HW:(h + 1) * MLA_HW] + kp).astype(BF16)
        vbt_ref[h * VT_ROWS:h * VT_ROWS + V_HD_B] = vt[h * V_HD_B:(h + 1) * V_HD_B].astype(BF16)
        vbt_ref[h * VT_ROWS + V_HD_B:(h + 1) * VT_ROWS] = ones

    for t in range(3):
        cqkv_ref[:, t * 512:(t + 1) * 512] = mm(P_CQKV + t * 512, P_CQKV + (t + 1) * 512).astype(BF16)

    for c in range(TMD // CHUNK):
        abt_ref[c] = _dot_nt(wab_ref[...], hb[c * CHUNK:(c + 1) * CHUNK])


def _layer_spec(arr, l):
    nd = arr.ndim - 1
    return pl.BlockSpec((None,) + arr.shape[1:], lambda *_: (l,) + (0,) * nd, pipeline_mode=pl.Buffered(1))


def _inproj(x2d, l, mod, mod_row_fn, ng, weights, qn, kvn, rope_tabs, tiles_per_seq, cache=None):
    t = x2d.shape[0]
    nt = t // TMD
    rope = rope_tabs is not None
    row = lambda i: (i, 0)
    col = lambda i: (0, i)
    wp, wab, wvat, wuq, wukv, wvbt = weights
    params = (ng, wp, wab, wvat, wuq, wukv, wvbt, qn, kvn)
    in_specs = [pl.BlockSpec((TMD, D_MODEL), row),
                pl.BlockSpec((1, 1, 3 * D_MODEL), lambda i: (mod_row_fn(i), 0, 0))]
    in_specs += [_layer_spec(a, l) for a in params]
    args = [x2d, mod, *params]
    if rope:
        pos = lambda i: (i % tiles_per_seq, 0)
        in_specs += [pl.BlockSpec((TMD, LANE), pos)] * 4
        args += list(rope_tabs)
    outs = [(W_A, BF16, False), (KA_COLS, BF16, False), (VAT_ROWS, BF16, True), (1536, BF16, False),
            (3 * D_MODEL, BF16, False), (HB_COLS, BF16, False), (HB_COLS, BF16, False), (VBT_ROWS, BF16, True),
            (LANE, F32, False), (QKV_C, BF16, False)]
    out_shape = [jax.ShapeDtypeStruct((w, t) if tr else (t, w), dt) for w, dt, tr in outs]
    out_specs = [pl.BlockSpec((w, TMD), col) if tr else pl.BlockSpec((TMD, w), row) for w, _, tr in outs]
    out_shape.append(jax.ShapeDtypeStruct((t // CHUNK, 16, CHUNK), F32))
    out_specs.append(pl.BlockSpec((TMD // CHUNK, 16, CHUNK), lambda i: (i, 0, 0)))
    aliases, cache_seq, n_carry = {}, 0, 0
    if cache is not None:
        depth, cache_seq, carry = cache
        spt = TMD // cache_seq
        for w in (N_KV_A * HD_A, N_KV_A * HD_A, KV_RANK_B, QK_ROPE_B):
            out_shape.append(jax.ShapeDtypeStruct((t // cache_seq, depth, cache_seq, w), F32))
            out_specs.append(pl.BlockSpec((spt, None, cache_seq, w), lambda i: (i, l, 0, 0)))
        n_carry = len(carry)
        for k, a in enumerate(carry):
            aliases[len(args)] = len(out_shape) - n_carry + k
            in_specs.append(pl.BlockSpec(memory_space=pl.ANY))
            args.append(a)
    return pl.pallas_call(
        functools.partial(_inproj_kernel, rope=rope, cache_seq=cache_seq, n_carry=n_carry),
        grid=(nt,),
        in_specs=in_specs,
        out_specs=out_specs,
        out_shape=out_shape,
        input_output_aliases=aliases,
        compiler_params=pltpu.CompilerParams(dimension_semantics=("parallel",)),
        name="inproj_ctx" if cache is not None else "inproj_lat",
    )(*args)


def _kvup_kernel(c_ref, kpe_ref, w_ref, wvt_ref, k_ref, vt_ref):
    c16 = c_ref[...].astype(BF16)
    kv = _dot(c16, w_ref[...])
    vt = _dot_nt(wvt_ref[...], c16)
    kp = kpe_ref[...]
    ones = jnp.ones((VT_ONES, c16.shape[0]), BF16)
    for h in range(N_HEADS_B):
        k_ref[:, h * MLA_HW:(h + 1) * MLA_HW] = (kv[:, h * MLA_HW:(h + 1) * MLA_HW] + kp).astype(BF16)
        vt_ref[h * VT_ROWS:h * VT_ROWS + V_HD_B] = vt[h * V_HD_B:(h + 1) * V_HD_B].astype(BF16)
        vt_ref[h * VT_ROWS + V_HD_B:(h + 1) * VT_ROWS] = ones


def _kvup(ckv, kpe, wukv, wvbt, l):
    nb, _, past, _ = ckv.shape
    return pl.pallas_call(
        _kvup_kernel,
        grid=(nb,),
        in_specs=[pl.BlockSpec((None, None, past, KV_RANK_B), lambda b: (b, l, 0, 0)),
                  pl.BlockSpec((None, None, past, LANE), lambda b: (b, l, 0, 0)),
                  _layer_spec(wukv, l), _layer_spec(wvbt, l)],
        out_specs=[pl.BlockSpec((past, HB_COLS), lambda b: (b, 0)),
                   pl.BlockSpec((None, VBT_ROWS, past), lambda b: (b, 0, 0))],
        out_shape=[jax.ShapeDtypeStruct((nb * past, HB_COLS), BF16),
                   jax.ShapeDtypeStruct((nb, VBT_ROWS, past), BF16)],
        name="mla_cache_up",
    )(ckv, kpe, wukv, wvbt)


def _scores_t(q_tiles, k_tiles, bias_t):
    st = _bdot_nt(jnp.stack(k_tiles), jnp.stack(q_tiles))
    return st if bias_t is None else st + bias_t[None]


def _softmax_pv(sts, vts, sink):
    m = jnp.max(sts[0], axis=1, keepdims=True)
    for st in sts[1:]:
        m = jnp.maximum(m, jnp.max(st, axis=1, keepdims=True))
    if sink is not None:
        m = jnp.maximum(m, sink)
    ot = _bdot(jnp.stack(vts[0]), jnp.exp2(sts[0] - m).astype(BF16))
    for st, vt in zip(sts[1:], vts[1:]):
        ot = ot + _bdot(jnp.stack(vt), jnp.exp2(st - m).astype(BF16))
    den = ot[:, V_HD_B:V_HD_B + 1, :]
    if sink is not None:
        den = den + jnp.exp2(sink - m)
    num = ot[:, :V_HD_B, :] / den
    return [jnp.concatenate([num[2 * i], num[2 * i + 1]], axis=0).T for i in range(sts[0].shape[0] // 2)]


def _tile(x, t):
    return x[:, t * LANE:(t + 1) * LANE]


def _attn_a_heads(q, segments, sink_ref):
    sts, vts = [], []
    for ka, vat, bias_t in segments:
        qs, ks, vs = [], [], []
        for t in range(N_HEADS_A // 2):
            g = (2 * t) // GQA_GROUP
            for e in range(2):
                qs.append(_tile(q, t))
                ks.append(_tile(ka, 2 * g + e))
                vs.append(vat[g * VT_ROWS:(g + 1) * VT_ROWS])
        sts.append(_scores_t(qs, ks, bias_t))
        vts.append(vs)
    sink = jnp.stack([sink_ref[:, h:h + 1] * LOG2E for h in range(N_HEADS_A)])
    return _softmax_pv(sts, vts, sink)


def _gated_store(outs, z_ref, o_ref, first_tile=0):
    for i, o in enumerate(outs):
        t = first_tile + i
        z = _tile(z_ref, t).astype(F32)
        o_ref[:, t * LANE:(t + 1) * LANE] = (o * _silu(z)).astype(BF16)


def _attn_a_ctx_kernel(q_ref, ka_ref, vat_ref, z_ref, sink_ref, o_ref):
    _gated_store(_attn_a_heads(q_ref[...], [(ka_ref[...], vat_ref[...], None)], sink_ref), z_ref, o_ref)


def _attn_a_ctx(qa, ka, vat, z, sink, l, seq):
    t = qa.shape[0]
    row = lambda b: (b, 0)
    return pl.pallas_call(
        _attn_a_ctx_kernel,
        grid=(t // seq,),
        in_specs=[pl.BlockSpec((seq, W_A), row),
                  pl.BlockSpec((seq, KA_COLS), row),
                  pl.BlockSpec((VAT_ROWS, seq), lambda b: (0, b)),
                  pl.BlockSpec((seq, W_A), row),
                  _layer_spec(sink, l)],
        out_specs=pl.BlockSpec((seq, W_A), row),
        out_shape=jax.ShapeDtypeStruct((t, W_A), BF16),
        compiler_params=pltpu.CompilerParams(dimension_semantics=("parallel",)),
        name="attn_a_ctx",
    )(qa, ka, vat, z, sink)


def _attn_a_lat_kernel(q_ref, kp_ref, kc_ref, kn_ref, vp_ref, vc_ref, vn_ref, kx_ref, vx_ref, z_ref, sink_ref,
                       o_ref, *, nq):
    j = pl.program_id(1)
    ka = jnp.concatenate([kp_ref[...], kc_ref[...], kn_ref[...]], axis=0)
    vat = jnp.concatenate([vp_ref[...], vc_ref[...], vn_ref[...]], axis=1)
    kj = lax.broadcasted_iota(jnp.int32, (ka.shape[0], TM), 0)
    qi = lax.broadcasted_iota(jnp.int32, (ka.shape[0], TM), 1)
    ok = (kj >= qi) & (kj <= qi + 2 * WINDOW)
    ok = ok & ((kj >= WINDOW) | (j > 0)) & ((kj < TM + WINDOW) | (j < nq - 1))
    bias_t = jnp.where(ok, 0.0, NEG_INF)
    segments = [(ka, vat, bias_t), (kx_ref[...], vx_ref[...], None)]
    _gated_store(_attn_a_heads(q_ref[...], segments, sink_ref), z_ref, o_ref)


def _attn_a_lat(qa, ka, vat, z, sink, kx, vxt, l, seq):
    t = qa.shape[0]
    nq = seq // TM
    past = kx.shape[2]
    r = TM // WINDOW
    row = lambda b, j: (b * nq + j, 0)
    prev = lambda b, j: ((b * nq + j) * r - jnp.where(j > 0, 1, 0), 0)
    nxt = lambda b, j: ((b * nq + j) * r + jnp.where(j < nq - 1, r, r - 1), 0)
    swap = lambda f: (lambda b, j: f(b, j)[::-1])
    return pl.pallas_call(
        functools.partial(_attn_a_lat_kernel, nq=nq),
        grid=(t // seq, nq),
        in_specs=[pl.BlockSpec((TM, W_A), row),
                  pl.BlockSpec((WINDOW, KA_COLS), prev),
                  pl.BlockSpec((TM, KA_COLS), row),
                  pl.BlockSpec((WINDOW, KA_COLS), nxt),
                  pl.BlockSpec((VAT_ROWS, WINDOW), swap(prev)),
                  pl.BlockSpec((VAT_ROWS, TM), swap(row)),
                  pl.BlockSpec((VAT_ROWS, WINDOW), swap(nxt)),
                  pl.BlockSpec((None, None, past, KA_COLS), lambda b, j: (b, l, 0, 0)),
                  pl.BlockSpec((None, None, VAT_ROWS, past), lambda b, j: (b, l, 0, 0)),
                  pl.BlockSpec((TM, W_A), row),
                  _layer_spec(sink, l)],
        out_specs=pl.BlockSpec((TM, W_A), row),
        out_shape=jax.ShapeDtypeStruct((t, W_A), BF16),
        compiler_params=pltpu.CompilerParams(dimension_semantics=("parallel", "parallel")),
        name="attn_a_lat",
    )(qa, ka, ka, ka, vat, vat, vat, kx, vxt, z, sink)


def _attn_b_kernel(*refs, has_ctx, group):
    if has_ctx:
        q_ref, k_ref, vt_ref, kx_ref, vxt_ref, z_ref, o_ref = refs
        kb = jnp.concatenate([k_ref[...], kx_ref[...]], axis=0)
        vbt = jnp.concatenate([vt_ref[...], vxt_ref[...]], axis=1)
    else:
        q_ref, k_ref, vt_ref, z_ref, o_ref = refs
        kb, vbt = k_ref[...], vt_ref[...]
    q = q_ref[...]

    def scores(h0):
        heads = range(h0, h0 + group)
        return _scores_t([_tile(q, h) for h in heads], [_tile(kb, h) for h in heads], None)

    st = scores(0)
    for h0 in range(0, N_HEADS_B, group):
        st_next = scores(h0 + group) if h0 + group < N_HEADS_B else None
        outs = _softmax_pv([st], [[vbt[h * VT_ROWS:(h + 1) * VT_ROWS] for h in range(h0, h0 + group)]], None)
        _gated_store(outs, z_ref, o_ref, h0 // 2)
        st = st_next


def _attn_b(qb, kb, vbt, z, seq, qblk, group, kx=None, vxt=None):
    t = qb.shape[0]
    nq = seq // qblk
    has_ctx = kx is not None
    hw = HB_COLS
    in_specs = [pl.BlockSpec((qblk, hw), lambda b, j: (b * nq + j, 0)),
                pl.BlockSpec((seq, hw), lambda b, j: (b, 0)),
                pl.BlockSpec((VBT_ROWS, seq), lambda b, j: (0, b))]
    args = [qb, kb, vbt]
    if has_ctx:
        past = kx.shape[0] // (t // seq)
        in_specs += [pl.BlockSpec((past, hw), lambda b, j: (b, 0)),
                     pl.BlockSpec((None, VBT_ROWS, past), lambda b, j: (b, 0, 0))]
        args += [kx, vxt]
    in_specs.append(pl.BlockSpec((qblk, W_B), lambda b, j: (b * nq + j, 1)))
    args.append(z)
    return pl.pallas_call(
        functools.partial(_attn_b_kernel, has_ctx=has_ctx, group=group),
        grid=(t // seq, nq),
        in_specs=in_specs,
        out_specs=pl.BlockSpec((qblk, W_B), lambda b, j: (b * nq + j, 0)),
        out_shape=jax.ShapeDtypeStruct((t, W_B), BF16),
        compiler_params=pltpu.CompilerParams(dimension_semantics=("parallel", "parallel")),
        name="attn_b_lat" if has_ctx else "attn_b_ctx",
    )(*args)


def _gdn_local_kernel(cq_ref, prev_ref, next_ref, small_ref, abt_ref, cw_ref, prow_ref, pcol_ref,
                      u_ref, w_ref, qg_ref, kd_ref, attn_ref, eg_ref, qkv_scr, gb_scr, *, tiles_per_seq):
    tpos = pl.program_id(0) % tiles_per_seq
    x = cq_ref[...].astype(F32)
    prev_row = jnp.where(tpos > 0, prev_ref[...].astype(F32)[HALO - 1:HALO, :], 0.0)
    next_row = jnp.where(tpos < tiles_per_seq - 1, next_ref[...].astype(F32)[0:1, :], 0.0)
    rows = lax.broadcasted_iota(jnp.int32, (TM, 1), 0)
    xm1 = jnp.where(rows == 0, prev_row, pltpu.roll(x, 1, 0))
    xp1 = jnp.where(rows == TM - 1, next_row, pltpu.roll(x, TM - 1, 0))
    cw = cw_ref[...]
    y = _silu(xm1 * cw[0:1] + x * cw[1:2] + xp1 * cw[2:3])
    nq = N_HEADS_C * DK_C
    for h in range(N_HEADS_C):
        qh = y[:, h * DK_C:(h + 1) * DK_C]
        kh = y[:, nq + h * DK_C:nq + (h + 1) * DK_C]
        qkv_scr[:, h * DK_C:(h + 1) * DK_C] = (
            qh * lax.rsqrt(jnp.sum(qh * qh, axis=-1, keepdims=True) + EPS) * (DK_C ** -0.5))
        qkv_scr[:, nq + h * DK_C:nq + (h + 1) * DK_C] = kh * lax.rsqrt(jnp.sum(kh * kh, axis=-1, keepdims=True) + EPS)
    qkv_scr[:, 2 * nq:] = y[:, 2 * nq:]

    sm = small_ref[...]
    prow = prow_ref[...]
    gb_scr[:, 0:8] = -jnp.exp(prow[0:1]) * _softplus(sm[:, S_A:S_A + 8] + prow[1:2])
    gb_scr[:, 8:16] = _sigmoid(sm[:, S_B:S_B + 8])
    pcol = pcol_ref[...]

    ri = lax.broadcasted_iota(jnp.int32, (CHUNK, LANE), 0)
    lane = lax.broadcasted_iota(jnp.int32, (CHUNK, LANE), 1)
    fwd = lane < CHUNK
    cj = lane & (CHUNK - 1)
    incl = (fwd & (ri >= cj)) | (~fwd & (ri <= cj))
    strict = (fwd & (ri > cj)) | (~fwd & (ri < cj))
    xor = ri ^ cj
    eye = (ri == cj).astype(F32)
    r2 = lax.broadcasted_iota(jnp.int32, (2 * CHUNK, LANE), 0)
    l2 = lax.broadcasted_iota(jnp.int32, (2 * CHUNK, LANE), 1)
    same_dir = (r2 < CHUNK) == (l2 < CHUNK)
    rs_ = lax.broadcasted_iota(jnp.int32, (CHUNK, CHUNK), 0)
    cs_ = lax.broadcasted_iota(jnp.int32, (CHUNK, CHUNK), 1)
    tril = (rs_ >= cs_).astype(F32)
    triu = (rs_ <= cs_).astype(F32)
    tri_rows = jnp.concatenate([triu, tril], axis=1)
    dup_rows = jnp.concatenate([(rs_ == cs_).astype(F32)] * 2, axis=1)

    def block_diag(x):
        return jnp.where(same_dir[None], jnp.concatenate([x, x], axis=1), 0.0).astype(BF16)

    lows, rhss, order = [], [], []

    def solve():
        low = jnp.stack(lows, axis=0)
        inv = eye[None] - jnp.where(xor[None] == 1, low, 0.0)
        b = 2
        while b < CHUNK:
            cpl = jnp.where((xor[None] >= b) & (xor[None] < 2 * b), low, 0.0)
            tmp = _bdot(cpl.astype(BF16), block_diag(inv))
            inv = inv - _bdot(inv.astype(BF16), block_diag(tmp))
            b *= 2
        scale_u, scale_w, vs, ks = (jnp.stack(a, axis=0) for a in zip(*rhss))
        u = _bdot((inv * scale_u).astype(BF16), vs)
        w = _bdot((inv * scale_w).astype(BF16), ks)
        for i, (rs, h) in enumerate(order):
            for d in range(2):
                cs = slice((d * N_HEADS_C + h) * DK_C, (d * N_HEADS_C + h + 1) * DK_C)
                u_ref[rs, cs] = u[i, :, d * DV_C:(d + 1) * DV_C]
                w_ref[rs, cs] = w[i, :, d * DK_C:(d + 1) * DK_C].astype(BF16)
        lows.clear(), rhss.clear(), order.clear()

    for c in range(CPT):
        if c % SOLVE_CHUNKS == 0 and c > 0:
            solve()
        rs = slice(c * CHUNK, (c + 1) * CHUNK)
        gcol = gb_scr[rs, 0:8]
        bcol = gb_scr[rs, 8:16]
        abt = abt_ref[c]
        grow = -jnp.exp(pcol[:, 0:1]) * _softplus(abt[0:8] + pcol[:, 1:2])
        gc_f = _dot_exact(tril, gcol)
        gc_b = _dot_exact(triu, gcol)
        gr = _dot_exact(grow, tri_rows)
        br = _dot_exact(_sigmoid(abt[8:16]), dup_rows)
        for h in range(N_HEADS_C):
            hb_ = N_HEADS_C + h
            q = qkv_scr[rs, h * DK_C:(h + 1) * DK_C]
            k = qkv_scr[rs, nq + h * DK_C:nq + (h + 1) * DK_C]
            v = qkv_scr[rs, 2 * nq + h * DV_C:2 * nq + (h + 1) * DV_C]
            k16 = k.astype(BF16)
            kk16 = jnp.concatenate([k16, k16], axis=0)
            kk = _dot_nt(k16, kk16)
            qk = _dot_nt(q.astype(BF16), kk16)
            gcs = (gc_f[:, h:h + 1], gc_b[:, hb_:hb_ + 1])
            betas = (bcol[:, h:h + 1], bcol[:, hb_:hb_ + 1])
            gc2 = jnp.where(fwd, gcs[0], gcs[1])
            gr2 = jnp.where(fwd[0:1], gr[h:h + 1, :], gr[hb_:hb_ + 1, :])
            decay = jnp.where(incl, jnp.exp(jnp.where(incl, gc2 - gr2, 0.0)), 0.0)
            lows.append(jnp.where(strict, jnp.where(fwd, betas[0], betas[1]) * kk * decay, 0.0))
            attn_ref[rs, h * LANE:(h + 1) * LANE] = (qk * decay).astype(BF16)
            order.append((rs, h))
            br2 = jnp.where(fwd[0:1], br[h:h + 1, :], br[hb_:hb_ + 1, :])
            v16 = v.astype(BF16)
            zeros = jnp.zeros_like(v16)
            rhss.append((br2, br2 * jnp.exp(gr2),
                         jnp.concatenate([jnp.concatenate([v16, zeros], axis=1),
                                          jnp.concatenate([zeros, v16], axis=1)], axis=0),
                         jnp.concatenate([jnp.concatenate([k16, zeros], axis=1),
                                          jnp.concatenate([zeros, k16], axis=1)], axis=0)))
            for d in range(2):
                dh = d * N_HEADS_C + h
                gc = gcs[d]
                eg = jnp.exp(gc)
                g_last = gc[CHUNK - 1:CHUNK] if d == 0 else gc[0:1]
                cs = slice(dh * DK_C, (dh + 1) * DK_C)
                qg_ref[rs, cs] = (q * eg).astype(BF16)
                kd_ref[rs, cs] = (k * jnp.exp(g_last - gc)).astype(BF16)
                eg_ref[c, dh:dh + 1, :] = jnp.broadcast_to(jnp.exp(g_last), (1, LANE))
    solve()


def _gdn_local(cqkv, small, abt, conv_w, prow, pcol, l, tiles_per_seq):
    t = cqkv.shape[0]
    nt = t // TM
    nh8 = t // HALO
    row = lambda i: (i, 0)
    dh = 2 * N_HEADS_C
    return pl.pallas_call(
        functools.partial(_gdn_local_kernel, tiles_per_seq=tiles_per_seq),
        grid=(nt,),
        in_specs=[pl.BlockSpec((TM, QKV_C), row),
                  pl.BlockSpec((HALO, QKV_C), lambda i: (jnp.maximum(i * (TM // HALO) - 1, 0), 0)),
                  pl.BlockSpec((HALO, QKV_C), lambda i: (jnp.minimum((i + 1) * (TM // HALO), nh8 - 1), 0)),
                  pl.BlockSpec((TM, LANE), row),
                  pl.BlockSpec((CPT, 16, CHUNK), lambda i: (i, 0, 0)),
                  _layer_spec(conv_w, l), _layer_spec(prow, l), _layer_spec(pcol, l)],
        out_specs=[pl.BlockSpec((TM, dh * DV_C), row),
                   pl.BlockSpec((TM, dh * DK_C), row),
                   pl.BlockSpec((TM, dh * DK_C), row),
                   pl.BlockSpec((TM, dh * DK_C), row),
                   pl.BlockSpec((TM, dh * CHUNK), row),
                   pl.BlockSpec((CPT, dh, LANE), lambda i: (i, 0, 0))],
        out_shape=[jax.ShapeDtypeStruct((t, dh * DV_C), F32),
                   jax.ShapeDtypeStruct((t, dh * DK_C), BF16),
                   jax.ShapeDtypeStruct((t, dh * DK_C), BF16),
                   jax.ShapeDtypeStruct((t, dh * DK_C), BF16),
                   jax.ShapeDtypeStruct((t, dh * CHUNK), BF16),
                   jax.ShapeDtypeStruct((t // CHUNK, dh, LANE), F32)],
        scratch_shapes=[pltpu.VMEM((TM, QKV_C), F32), pltpu.VMEM((TM, 16), F32)],
        compiler_params=pltpu.CompilerParams(dimension_semantics=("parallel",)),
        name="gdn_local",
    )(cqkv, cqkv, cqkv, small, abt, conv_w, prow, pcol)


def _gdn_scan_kernel(*refs, nt, ns, has_init, want_state, n_carry):
    it = iter(refs)
    ins = [[next(it) for _ in range(6)] for _ in range(2)]
    s0_ref = next(it) if has_init else None
    for _ in range(n_carry):
        next(it)
    o_refs = [next(it), next(it)]
    st_ref = next(it) if want_state else None
    s_scr = next(it)
    j = pl.program_id(1)
    nh = N_HEADS_C
    nst = ns * 2 * nh

    @pl.when(j == 0)
    def _():
        if has_init:
            s_scr[...] = s0_ref[...].reshape(nst, DK_C, DV_C)
        else:
            s_scr[...] = jnp.zeros_like(s_scr)

    for step in range(CPT):
        chunk = lambda d: step if d == 0 else CPT - 1 - step

        def gather(idx, width):
            return jnp.stack([ins[d][idx][s, chunk(d) * CHUNK:(chunk(d) + 1) * CHUNK, h * width:(h + 1) * width]
                              for s in range(ns) for d in range(2) for h in range(nh)])

        u, w, qg, kd, attn = gather(0, DV_C), gather(1, DK_C), gather(2, DK_C), gather(3, DK_C), gather(4, LANE)
        eg = jnp.stack([ins[d][5][s, chunk(d), d * nh + h:d * nh + h + 1, :]
                        for s in range(ns) for d in range(2) for h in range(nh)])
        st = s_scr[...]
        sb = st.astype(BF16)
        v_new = u - _bdot(w, sb)
        vb = v_new.astype(BF16)
        zeros = jnp.zeros((CHUNK, DV_C), BF16)
        vb2 = jnp.stack([jnp.concatenate([vb[i], zeros] if (i // nh) % 2 == 0 else [zeros, vb[i]], axis=0)
                         for i in range(nst)])
        o = _bdot(qg, sb) + _bdot(attn, vb2)
        s_scr[...] = st * eg + _bdot_tn(kd, vb)
        for s in range(ns):
            for d in range(2):
                for h in range(nh):
                    o_refs[d][s, chunk(d) * CHUNK:(chunk(d) + 1) * CHUNK, h * DV_C:(h + 1) * DV_C] = (
                        o[(s * 2 + d) * nh + h])

    if want_state:
        @pl.when(j == nt - 1)
        def _():
            st_ref[...] = s_scr[...].reshape(ns, 2, nh, DK_C, DV_C)


def _gdn_scan(u, w, qg, kd, attn, eg, s0, l, seq, state_out=None):
    t = u.shape[0]
    want_state = state_out is not None
    nt = seq // TM
    nb = t // seq
    ns = next(n for n in (SCAN_SEQS, 2, 1) if nb % n == 0)
    half = N_HEADS_C * DK_C
    has_init = s0 is not None
    by_seq = lambda a: a.reshape((nb, a.shape[0] // nb) + a.shape[1:])
    in_specs, args = [], []
    for d in range(2):
        tile = (lambda b, j: j) if d == 0 else (lambda b, j: nt - 1 - j)
        row = lambda b, j, d=d, tile=tile: (b, tile(b, j), d)
        row4 = lambda b, j, tile=tile: (b, tile(b, j), 0, 0)
        in_specs += [pl.BlockSpec((ns, TM, half), row)] * 4
        in_specs += [pl.BlockSpec((ns, TM, N_HEADS_C * LANE), lambda b, j, tile=tile: (b, tile(b, j), 0)),
                     pl.BlockSpec((ns, CPT, 2 * N_HEADS_C, LANE), row4)]
        args += [by_seq(a) for a in (u, w, qg, kd, attn, eg)]
    st_tail = (2, N_HEADS_C, DK_C, DV_C)
    if has_init:
        in_specs.append(pl.BlockSpec((ns, None) + st_tail, lambda b, j: (b, l, 0, 0, 0, 0)))
        args.append(s0)
    out_specs = [pl.BlockSpec((ns, TM, half), lambda b, j: (b, j, 0)),
                 pl.BlockSpec((ns, TM, half), lambda b, j: (b, nt - 1 - j, 0))]
    out_shape = [jax.ShapeDtypeStruct((nb, seq, half), F32)] * 2
    aliases, n_carry = {}, 0
    if want_state:
        depth, carry = state_out
        out_specs.append(pl.BlockSpec((ns, None) + st_tail, lambda b, j: (b, l, 0, 0, 0, 0)))
        out_shape.append(jax.ShapeDtypeStruct((nb, depth) + st_tail, F32))
        n_carry = 1
        aliases[len(args)] = 2
        in_specs.append(pl.BlockSpec(memory_space=pl.ANY))
        args.append(carry)
    outs = pl.pallas_call(
        functools.partial(_gdn_scan_kernel, nt=nt, ns=ns, has_init=has_init, want_state=want_state,
                          n_carry=n_carry),
        grid=(nb // ns, nt),
        in_specs=in_specs,
        out_specs=out_specs,
        out_shape=out_shape,
        input_output_aliases=aliases,
        scratch_shapes=[pltpu.VMEM((ns * 2 * N_HEADS_C, DK_C, DV_C), F32)],
        compiler_params=pltpu.CompilerParams(dimension_semantics=("parallel", "arbitrary")),
        name="gdn_scan",
    )(*args)
    return [outs[0].reshape(t, half), outs[1].reshape(t, half)] + list(outs[2:])


def _merge_kernel(x_ref, mod_ref, oa_ref, ob_ref, cf_ref, cb_ref, zc_ref, gates_ref, gn_ref,
                  wa_ref, wb_ref, wc_ref, wo_ref, fg_ref, o_ref, *, last):
    oc = cf_ref[...] + cb_ref[...]
    zc = zc_ref[...].astype(F32)
    gn = gn_ref[...]
    parts = []
    for h in range(N_HEADS_C):
        hs = slice(h * DV_C, (h + 1) * DV_C)
        och = oc[:, hs]
        och = och * lax.rsqrt(jnp.mean(och * och, axis=-1, keepdims=True) + EPS) * gn
        parts.append((och * _silu(zc[:, hs])).astype(BF16))
    ocz = jnp.concatenate(parts, axis=-1)
    pa = _dot(oa_ref[...], wa_ref[...])
    pb = _dot(ob_ref[...], wb_ref[...])
    pc = _dot(ocz, wc_ref[...])
    ga = _sigmoid(gates_ref[:, 0:D_MODEL].astype(F32))
    gb = _sigmoid(gates_ref[:, D_MODEL:2 * D_MODEL].astype(F32))
    gc = _sigmoid(gates_ref[:, 2 * D_MODEL:].astype(F32))
    y = _dot((ga * pa + gb * pb + gc * pc).astype(BF16), wo_ref[...])
    gate = mod_ref[0][:, 2 * D_MODEL:]
    xo = x_ref[...] + gate * y
    if last:
        xo = xo * lax.rsqrt(jnp.mean(xo * xo, axis=-1, keepdims=True) + EPS) * fg_ref[...]
    o_ref[...] = xo


def _merge(x2d, l, mod, mod_row_fn, oa, ob, cf, cb, z, gates, gn, wa, wb, wc, wo, fg, last):
    t = x2d.shape[0]
    row = lambda i: (i, 0)
    return pl.pallas_call(
        functools.partial(_merge_kernel, last=last),
        grid=(t // TMD,),
        in_specs=[pl.BlockSpec((TMD, D_MODEL), row),
                  pl.BlockSpec((1, 1, 3 * D_MODEL), lambda i: (mod_row_fn(i), 0, 0)),
                  pl.BlockSpec((TMD, W_A), row),
                  pl.BlockSpec((TMD, W_B), row),
                  pl.BlockSpec((TMD, W_C), row),
                  pl.BlockSpec((TMD, W_C), row),
                  pl.BlockSpec((TMD, W_C), lambda i: (i, 2)),
                  pl.BlockSpec((TMD, 3 * D_MODEL), row),
                  _layer_spec(gn, l), _layer_spec(wa, l), _layer_spec(wb, l), _layer_spec(wc, l),
                  _layer_spec(wo, l),
                  pl.BlockSpec((1, D_MODEL), lambda i: (0, 0))],
        out_specs=pl.BlockSpec((TMD, D_MODEL), row),
        out_shape=jax.ShapeDtypeStruct((t, D_MODEL), F32),
        compiler_params=pltpu.CompilerParams(dimension_semantics=("parallel",)),
        name="merge",
    )(x2d, mod, oa, ob, cf, cb, z, gates, gn, wa, wb, wc, wo, fg)


def _rope_tables(n_tokens, rot_dim):
    rows = n_tokens // GRID_W
    row = np.repeat(np.arange(rows), GRID_W).astype(np.float32)
    col = np.tile(np.arange(GRID_W), rows).astype(np.float32)
    n_pairs = rot_dim // 4
    inv = (np.float32(ROPE_BASE) ** (-np.arange(n_pairs, dtype=np.float32) / np.float32(n_pairs))).astype(np.float32)
    ang = np.concatenate([row[:, None] * inv, col[:, None] * inv], axis=-1)
    c, s = np.cos(ang), np.sin(ang)
    return np.repeat(c, 2, axis=-1), np.stack([-s, s], axis=-1).reshape(n_tokens, rot_dim)


def _in_offsets():
    o = [0]
    for n in IN_SIZES:
        o.append(o[-1] + n)
    return o


def _relayout_moves():
    o = _in_offsets()
    order = [(o[0], W_A), (o[1], LANE), (o[1] + HD_A, HD_A), (o[1], HD_A), (o[2], LANE),
             (o[3], W_A), (o[7], W_B), (o[11], W_C), (o[12], 3 * D_MODEL), (o[4], Q_RANK_B), (o[5], KV_RANK_B),
             (None, S_KPE), (o[6], QK_ROPE_B), (o[9], 4 * N_HEADS_C), (None, LANE - S_B - 8), (o[8], QKV_C)]
    moves, dst = [], 0
    for src, n in order:
        moves.append((src, dst, n))
        dst += n
    assert dst == P_END
    return moves


def _relayout_kernel(w_ref, o_ref, wab_ref, wvat_ref):
    for src, dst, n in _relayout_moves():
        if src is None:
            o_ref[dst:dst + n, :] = jnp.zeros((n, o_ref.shape[1]), BF16)
        else:
            o_ref[dst:dst + n, :] = w_ref[src:src + n, :].astype(BF16)
    o = _in_offsets()
    wab_ref[...] = w_ref[o[9]:o[11], :].astype(BF16)
    wvat_ref[...] = w_ref[o[2]:o[3], :].astype(BF16)


def _relayout_w_in(w_in):
    w_t = jnp.swapaxes(w_in, 1, 2)
    depth, width, _ = w_t.shape
    cols = 128
    nab, nv = 4 * N_HEADS_C, N_KV_A * HD_A
    col = lambda l, i: (l, 0, i)
    return pl.pallas_call(
        _relayout_kernel,
        grid=(depth, D_MODEL // cols),
        in_specs=[pl.BlockSpec((None, width, cols), col)],
        out_specs=[pl.BlockSpec((None, P_END, cols), col),
                   pl.BlockSpec((None, nab, cols), col),
                   pl.BlockSpec((None, nv, cols), col)],
        out_shape=[jax.ShapeDtypeStruct((depth, P_END, D_MODEL), BF16),
                   jax.ShapeDtypeStruct((depth, nab, D_MODEL), BF16),
                   jax.ShapeDtypeStruct((depth, nv, D_MODEL), BF16)],
        name="w_in_relayout",
    )(w_t)


def _prep_weights(w_in, w_uq, w_ukv):
    depth = w_in.shape[0]
    wp, wab, wvat = _relayout_w_in(w_in)
    hd = QK_NOPE_B + QK_ROPE_B
    wuq = jnp.pad(w_uq.reshape(depth, Q_RANK_B, N_HEADS_B, hd), ((0, 0), (0, 0), (0, 0), (0, MLA_HW - hd)))
    wuq = wuq.reshape(depth, Q_RANK_B, HB_COLS).astype(BF16)
    kv = w_ukv.reshape(depth, KV_RANK_B, N_HEADS_B, QK_NOPE_B + V_HD_B)
    wk = jnp.pad(kv[..., :QK_NOPE_B], ((0, 0), (0, 0), (0, 0), (0, MLA_HW - QK_NOPE_B)))
    wukv = wk.reshape(depth, KV_RANK_B, HB_COLS).astype(BF16)
    wvbt = jnp.swapaxes(kv[..., QK_NOPE_B:].reshape(depth, KV_RANK_B, W_B), 1, 2).astype(BF16)
    return wp, wab, wvat, wuq, wukv, wvbt


def _cache_tiles_a(kx, vx):
    k0, k1 = kx[..., 0, :], kx[..., 1, :]
    z = jnp.zeros_like(k0)
    ka = jnp.concatenate([k0, z, z, k0, k1, z, z, k1], axis=-1).astype(BF16)
    vt = jnp.transpose(vx, (0, 1, 3, 4, 2))
    vt = jnp.concatenate([vt, jnp.ones(vt.shape[:3] + (VT_ONES, vt.shape[4]), vt.dtype)], axis=3)
    return ka, vt.reshape(vt.shape[:2] + (VAT_ROWS, vt.shape[4])).astype(BF16)


def kernel(x_prompt, x_sample, cache_attn_k, cache_attn_v, cache_mla_ckv, cache_mla_kpe, state_gdn, c, c_ctx,
           norm_g, w_ada, b_ada, w_in, attn_sink, mla_q_norm, mla_w_uq, mla_kv_norm, mla_w_ukv, gdn_conv,
           gdn_a_log, gdn_dt_bias, gdn_norm, w_branch_a, w_branch_b, w_branch_c, w_out, final_norm_g):
    depth = w_in.shape[0]
    nb_c, seq_c, _ = x_prompt.shape
    nb_l, seq_l, _ = x_sample.shape
    past = cache_attn_k.shape[2]
    assert P_END % LANE == 0 and seq_c % TM == 0 and seq_l % TMD == 0 and nb_l < 8 and TM == 2 * WINDOW
    assert (nb_c * seq_c) % TMD == 0 and TMD % TM == 0

    cond8 = jnp.zeros((8, D_MODEL), F32).at[:nb_l].set(c).at[nb_l].set(c_ctx)
    mod = _modulation(cond8, w_ada, b_ada).reshape(depth * 8, 1, 3 * D_MODEL)

    c_a, s_a = _rope_tables(seq_l, HD_A)
    c_b, s_b = _rope_tables(seq_l, QK_ROPE_B)
    pad_l, pad_r = S_KPE, LANE - S_KPE - QK_ROPE_B
    one, zero = np.ones((seq_l, 1), np.float32), np.zeros((seq_l, 1), np.float32)
    rope_tabs = tuple(jnp.asarray(a) for a in (
        np.tile(c_a, (1, LANE // HD_A)), np.tile(s_a, (1, LANE // HD_A)),
        np.concatenate([np.tile(one, (1, pad_l)), c_b, np.tile(one, (1, pad_r))], 1),
        np.concatenate([np.tile(zero, (1, pad_l)), s_b, np.tile(zero, (1, pad_r))], 1)))

    weights = _prep_weights(w_in, mla_w_uq, mla_w_ukv)
    wukv, wvbt = weights[4], weights[5]
    ng = norm_g.reshape(depth, 1, D_MODEL)
    qn = mla_q_norm.reshape(depth, 1, Q_RANK_B)
    kvn = mla_kv_norm.reshape(depth, 1, KV_RANK_B)
    sink = attn_sink.reshape(depth, 1, N_HEADS_A)
    prow = jnp.stack([gdn_a_log.reshape(depth, -1), gdn_dt_bias.reshape(depth, -1)], axis=1)
    pcol = jnp.swapaxes(prow, 1, 2)
    gn = gdn_norm.reshape(depth, 1, DV_C)
    wa, wb, wc, wo = (w.astype(BF16) for w in (w_branch_a, w_branch_b, w_branch_c, w_out))
    fg = final_norm_g.reshape(1, D_MODEL)
    kxa, vxa = _cache_tiles_a(cache_attn_k, cache_attn_v)
    kpex = jnp.pad(cache_mla_kpe, ((0, 0), (0, 0), (0, 0), (pad_l, pad_r)))

    tps_c, tps_l = seq_c // TM, seq_l // TM
    tpd_l = seq_l // TMD
    y_p = x_prompt.reshape(nb_c * seq_c, D_MODEL)
    y_s = x_sample.reshape(nb_l * seq_l, D_MODEL)
    new_cache = tuple(jnp.zeros((nb_c, depth, seq_c, w), F32)
                      for w in (N_KV_A * HD_A, N_KV_A * HD_A, KV_RANK_B, QK_ROPE_B))
    new_state = jnp.zeros((nb_c, depth, 2, N_HEADS_C, DK_C, DV_C), F32)
    for l in range(depth):
        last = l == depth - 1

        mod_row_c = lambda i, l=l: l * 8 + nb_l
        outs = _inproj(y_p, l, mod, mod_row_c, ng, weights, qn, kvn, None, tps_c, (depth, seq_c, new_cache))
        (qa, ka, vat, z, gates, qb, kb, vbt, small, cqkv, abt), new_cache = outs[:11], tuple(outs[11:])
        oa = _attn_a_ctx(qa, ka, vat, z, sink, l, seq_c)
        ob = _attn_b(qb, kb, vbt, z, seq_c, seq_c, N_HEADS_B)
        u, w, qg, kd, attn, eg = _gdn_local(cqkv, small, abt, gdn_conv, prow, pcol, l, tps_c)
        cf, cb, new_state = _gdn_scan(u, w, qg, kd, attn, eg, None, l, seq_c, (depth, new_state))
        y_p = _merge(y_p, l, mod, mod_row_c, oa, ob, cf, cb, z, gates, gn, wa, wb, wc, wo, fg, last)

        mod_row_l = lambda i, l=l: l * 8 + i // tpd_l
        (qa, ka, vat, z, gates, qb, kb, vbt, small, cqkv, abt) = _inproj(
            y_s, l, mod, mod_row_l, ng, weights, qn, kvn, rope_tabs, tpd_l)
        oa = _attn_a_lat(qa, ka, vat, z, sink, kxa, vxa, l, seq_l)
        kxb, vxb = _kvup(cache_mla_ckv, kpex, wukv, wvbt, l)
        ob = _attn_b(qb, kb, vbt, z, seq_l, TM, 4, kxb, vxb)
        u, w, qg, kd, attn, eg = _gdn_local(cqkv, small, abt, gdn_conv, prow, pcol, l, tps_l)
        cf, cb = _gdn_scan(u, w, qg, kd, attn, eg, state_gdn, l, seq_l)
        y_s = _merge(y_s, l, mod, mod_row_l, oa, ob, cf, cb, z, gates, gn, wa, wb, wc, wo, fg, last)

    new_k, new_v, new_ckv, new_kpe = new_cache
    kv_shape = (nb_c, depth, seq_c, N_KV_A, HD_A)
    return (y_p.reshape(nb_c, seq_c, D_MODEL), y_s.reshape(nb_l, seq_l, D_MODEL),
            new_k.reshape(kv_shape), new_v.reshape(kv_shape), new_ckv, new_kpe, new_state)
```

```python
import functools

import numpy as np
import jax
import jax.numpy as jnp
from jax import lax
from jax.experimental import pallas as pl
from jax.experimental.pallas import tpu as pltpu

F32 = jnp.float32
BF16 = jnp.bfloat16

D_MODEL = 1024
GRID_W = 64
ROPE_BASE = 10000.0
EPS = 1e-6
NEG_INF = -1e30
N_HEADS_A = 8
N_KV_A = 2
HD_A = 64
GQA_GROUP = N_HEADS_A // N_KV_A
WINDOW = 128
N_HEADS_B = 8
QK_NOPE_B = 64
QK_ROPE_B = 32
V_HD_B = 64
Q_RANK_B = 384
KV_RANK_B = 256
MLA_SCALE = (QK_NOPE_B + QK_ROPE_B) ** -0.5
N_HEADS_C = 4
DK_C = 128
DV_C = 128
CHUNK = 64
W_A = N_HEADS_A * HD_A
W_B = N_HEADS_B * V_HD_B
W_C = N_HEADS_C * DV_C
QKV_C = 2 * N_HEADS_C * DK_C + W_C
IN_SIZES = (W_A, N_KV_A * HD_A, N_KV_A * HD_A, W_A, Q_RANK_B, KV_RANK_B, QK_ROPE_B, W_B, QKV_C,
            2 * N_HEADS_C, 2 * N_HEADS_C, W_C, 3 * D_MODEL)

LANE = 128
HALF = LANE // 2
TM = 256
TMD = 512
TMG = 256
CPT = TM // CHUNK
SOLVE_CHUNKS = 4
SCAN_SEQS = 4
HALO = 16
MLA_HW = 128
KA_COLS = 4 * LANE
HB_COLS = N_HEADS_B * MLA_HW
VT_ONES = 16
VT_ROWS = V_HD_B + VT_ONES
VAT_ROWS = N_KV_A * VT_ROWS
VBT_ROWS = N_HEADS_B * VT_ROWS
LOG2E = 1.4426950408889634

P_QKV = 0
A_COLS = W_A + 3 * LANE
P_Z = P_QKV + A_COLS
P_GATES = P_Z + 1536
P_CQ = P_GATES + 3 * D_MODEL
P_CKV = P_CQ + Q_RANK_B
P_SMALL = P_CKV + KV_RANK_B
P_CQKV = P_SMALL + LANE
P_END = P_CQKV + QKV_C
S_KPE = 64
S_A = 96
S_B = 104


def _sigmoid(x):
    return 0.5 * jnp.tanh(0.5 * x) + 0.5


def _silu(x):
    return x * _sigmoid(x)


def _softplus(x):
    return jnp.maximum(x, 0.0) + jnp.log(1.0 + jnp.exp(-jnp.abs(x)))


def _dot(a, b):
    return jnp.dot(a, b, preferred_element_type=F32)


def _dot_nt(a, b):
    return lax.dot_general(a, b, (((1,), (1,)), ((), ())), preferred_element_type=F32)


def _bdot(a, b):
    return lax.dot_general(a, b, (((2,), (1,)), ((0,), (0,))), preferred_element_type=F32)


def _bdot_nt(a, b):
    return lax.dot_general(a, b, (((2,), (2,)), ((0,), (0,))), preferred_element_type=F32)


def _bdot_tn(a, b):
    return lax.dot_general(a, b, (((1,), (1,)), ((0,), (0,))), preferred_element_type=F32)


def _dot_exact(a, b):
    return jnp.dot(a, b, preferred_element_type=F32, precision=lax.Precision.HIGHEST)


def _rope(x, c, s):
    n = x.shape[-1]
    lane = lax.broadcasted_iota(jnp.int32, x.shape, 1)
    swapped = jnp.where(lane % 2 == 0, pltpu.roll(x, n - 1, 1), pltpu.roll(x, 1, 1))
    return x * c + swapped * s


def _mod_kernel(cond_ref, w_ref, b_ref, out_ref):
    cnd = cond_ref[...]
    out_ref[0] = _dot(_silu(cnd).astype(BF16), w_ref[0].astype(BF16)) + b_ref[0]


def _modulation(cond8, w_ada, b_ada):
    depth = w_ada.shape[0]
    tn = 768
    return pl.pallas_call(
        _mod_kernel,
        grid=(depth, 3 * D_MODEL // tn),
        in_specs=[pl.BlockSpec((8, D_MODEL), lambda l, n: (0, 0)),
                  pl.BlockSpec((1, D_MODEL, tn), lambda l, n: (l, 0, n)),
                  pl.BlockSpec((1, 1, tn), lambda l, n: (l, 0, n))],
        out_specs=pl.BlockSpec((1, 8, tn), lambda l, n: (l, 0, n)),
        out_shape=jax.ShapeDtypeStruct((depth, 8, 3 * D_MODEL), F32),
        name="adaln_mod",
    )(cond8, w_ada, b_ada.reshape(depth, 1, 3 * D_MODEL))


def _inproj_kernel(*refs, rope, cache_seq, n_carry):
    it = iter(refs)
    (x_ref, mod_ref, ng_ref, wp_ref, wab_ref, wvat_ref, wuq_ref, wukv_ref, wvbt_ref, qn_ref,
     kvn_ref) = (next(it) for _ in range(11))
    if rope:
        ca_ref, sa_ref, cb_ref, sb_ref = (next(it) for _ in range(4))
    for _ in range(n_carry):
        next(it)
    (qa_ref, ka_ref, vat_ref, z_ref, gates_ref, qb_ref, kb_ref, vbt_ref, small_ref, cqkv_ref,
     abt_ref) = (next(it) for _ in range(11))
    if cache_seq:
        ck_ref, cv_ref, cckv_ref, ckpe_ref = (next(it) for _ in range(4))

    def to_cache(ref, val):
        for s in range(TMD // cache_seq):
            ref[s] = val[s * cache_seq:(s + 1) * cache_seq]

    x = x_ref[...]
    mod = mod_ref[0]
    shift, scale = mod[:, :D_MODEL], mod[:, D_MODEL:2 * D_MODEL]
    xn = x * lax.rsqrt(jnp.mean(x * x, axis=-1, keepdims=True) + EPS) * ng_ref[...]
    hb = (xn * (1.0 + scale) + shift).astype(BF16)
    lane = lax.broadcasted_iota(jnp.int32, (TMD, LANE), 1)
    lo = lane < HALF

    def mm(lo_col, hi_col):
        return _dot_nt(hb, wp_ref[lo_col:hi_col, :])

    r = mm(P_QKV, P_QKV + A_COLS)
    tiles = [r[:, t * LANE:(t + 1) * LANE] for t in range(A_COLS // LANE)]
    if cache_seq:
        to_cache(ck_ref, tiles[4])
        to_cache(cv_ref, tiles[6])
    if rope:
        ca, sa = ca_ref[...], sa_ref[...]
        tiles[:6] = [_rope(t, ca, sa) for t in tiles[:6]]
    for t in range(4):
        qa_ref[:, t * LANE:(t + 1) * LANE] = (tiles[t] * (HD_A ** -0.5 * LOG2E)).astype(BF16)
    k01, k10 = tiles[4], tiles[5]
    ka_ref[:, 0 * LANE:1 * LANE] = jnp.where(lo, k01, 0.0).astype(BF16)
    ka_ref[:, 1 * LANE:2 * LANE] = jnp.where(lo, 0.0, k10).astype(BF16)
    ka_ref[:, 2 * LANE:3 * LANE] = jnp.where(lo, k10, 0.0).astype(BF16)
    ka_ref[:, 3 * LANE:4 * LANE] = jnp.where(lo, 0.0, k01).astype(BF16)
    ones = jnp.ones((VT_ONES, TMD), BF16)
    vt = _dot_nt(wvat_ref[...], hb)
    for g in range(N_KV_A):
        vat_ref[g * VT_ROWS:g * VT_ROWS + HD_A] = vt[g * HD_A:(g + 1) * HD_A].astype(BF16)
        vat_ref[g * VT_ROWS + HD_A:(g + 1) * VT_ROWS] = ones

    for t in range(3):
        z_ref[:, t * 512:(t + 1) * 512] = mm(P_Z + t * 512, P_Z + (t + 1) * 512).astype(BF16)
    for t in range(6):
        gates_ref[:, t * 512:(t + 1) * 512] = mm(P_GATES + t * 512, P_GATES + (t + 1) * 512).astype(BF16)

    r = mm(P_CQ, P_CQ + Q_RANK_B)
    qn = r * lax.rsqrt(jnp.mean(r * r, axis=-1, keepdims=True) + EPS) * qn_ref[...]
    q = _dot(qn.astype(BF16), wuq_ref[...])
    if rope:
        cb, sb = cb_ref[...], sb_ref[...]
        for h in range(N_HEADS_B):
            seg = _rope(q[:, h * MLA_HW:(h + 1) * MLA_HW], cb, sb) * (MLA_SCALE * LOG2E)
            qb_ref[:, h * MLA_HW:(h + 1) * MLA_HW] = seg.astype(BF16)
    else:
        qb_ref[...] = (q * (MLA_SCALE * LOG2E)).astype(BF16)

    r = mm(P_SMALL, P_SMALL + LANE)
    small_ref[...] = r
    if cache_seq:
        to_cache(ckpe_ref, r[:, S_KPE:S_KPE + QK_ROPE_B])
    kp = _rope(r, cb, sb) if rope else r
    kp = jnp.where((lane >= S_KPE) & (lane < S_KPE + QK_ROPE_B), kp, 0.0)

    r = mm(P_CKV, P_CKV + KV_RANK_B)
    cn = r * lax.rsqrt(jnp.mean(r * r, axis=-1, keepdims=True) + EPS) * kvn_ref[...]
    if cache_seq:
        to_cache(cckv_ref, cn)
    cn16 = cn.astype(BF16)
    kv = _dot(cn16, wukv_ref[...])
    vt = _dot_nt(wvbt_ref[...], cn16)
    for h in range(N_HEADS_B):
        kb_ref[:, h * MLA_HW:(h + 1) * MLA_HW] = (kv[:, h * MLA_HW:(h + 1) * MLA_HW] + kp).astype(BF16)
        vbt_ref[h * VT_ROWS:h * VT_ROWS + V_HD_B] = vt[h * V_HD_B:(h + 1) * V_HD_B].astype(BF16)
        vbt_ref[h * VT_ROWS + V_HD_B:(h + 1) * VT_ROWS] = ones

    for t in range(3):
        cqkv_ref[:, t * 512:(t + 1) * 512] = mm(P_CQKV + t * 512, P_CQKV + (t + 1) * 512).astype(BF16)

    for c in range(TMD // CHUNK):
        abt_ref[c] = _dot_nt(wab_ref[...], hb[c * CHUNK:(c + 1) * CHUNK])


def _layer_spec(arr, l):
    nd = arr.ndim - 1
    return pl.BlockSpec((None,) + arr.shape[1:], lambda *_: (l,) + (0,) * nd, pipeline_mode=pl.Buffered(1))


def _inproj(x2d, l, mod, mod_row_fn, ng, weights, qn, kvn, rope_tabs, tiles_per_seq, cache=None):
    t = x2d.shape[0]
    nt = t // TMD
    rope = rope_tabs is not None
    row = lambda i: (i, 0)
    col = lambda i: (0, i)
    wp, wab, wvat, wuq, wukv, wvbt = weights
    params = (ng, wp, wab, wvat, wuq, wukv, wvbt, qn, kvn)
    in_specs = [pl.BlockSpec((TMD, D_MODEL), row),
                pl.BlockSpec((1, 1, 3 * D_MODEL), lambda i: (mod_row_fn(i), 0, 0))]
    in_specs += [_layer_spec(a, l) for a in params]
    args = [x2d, mod, *params]
    if rope:
        pos = lambda i: (i % tiles_per_seq, 0)
        in_specs += [pl.BlockSpec((TMD, LANE), pos)] * 4
        args += list(rope_tabs)
    outs = [(W_A, BF16, False), (KA_COLS, BF16, False), (VAT_ROWS, BF16, True), (1536, BF16, False),
            (3 * D_MODEL, BF16, False), (HB_COLS, BF16, False), (HB_COLS, BF16, False), (VBT_ROWS, BF16, True),
            (LANE, F32, False), (QKV_C, BF16, False)]
    out_shape = [jax.ShapeDtypeStruct((w, t) if tr else (t, w), dt) for w, dt, tr in outs]
    out_specs = [pl.BlockSpec((w, TMD), col) if tr else pl.BlockSpec((TMD, w), row) for w, _, tr in outs]
    out_shape.append(jax.ShapeDtypeStruct((t // CHUNK, 16, CHUNK), F32))
    out_specs.append(pl.BlockSpec((TMD // CHUNK, 16, CHUNK), lambda i: (i, 0, 0)))
    aliases, cache_seq, n_carry = {}, 0, 0
    if cache is not None:
        depth, cache_seq, carry = cache
        spt = TMD // cache_seq
        for w in (N_KV_A * HD_A, N_KV_A * HD_A, KV_RANK_B, QK_ROPE_B):
            out_shape.append(jax.ShapeDtypeStruct((t // cache_seq, depth, cache_seq, w), F32))
            out_specs.append(pl.BlockSpec((spt, None, cache_seq, w), lambda i: (i, l, 0, 0)))
        n_carry = len(carry)
        for k, a in enumerate(carry):
            aliases[len(args)] = len(out_shape) - n_carry + k
            in_specs.append(pl.BlockSpec(memory_space=pl.ANY))
            args.append(a)
    return pl.pallas_call(
        functools.partial(_inproj_kernel, rope=rope, cache_seq=cache_seq, n_carry=n_carry),
        grid=(nt,),
        in_specs=in_specs,
        out_specs=out_specs,
        out_shape=out_shape,
        input_output_aliases=aliases,
        compiler_params=pltpu.CompilerParams(dimension_semantics=("parallel",)),
        name="inproj_ctx" if cache is not None else "inproj_lat",
    )(*args)


def _kvup_kernel(c_ref, kpe_ref, w_ref, wvt_ref, k_ref, vt_ref):
    c16 = c_ref[...].astype(BF16)
    kv = _dot(c16, w_ref[...])
    vt = _dot_nt(wvt_ref[...], c16)
    kp = kpe_ref[...]
    ones = jnp.ones((VT_ONES, c16.shape[0]), BF16)
    for h in range(N_HEADS_B):
        k_ref[:, h * MLA_HW:(h + 1) * MLA_HW] = (kv[:, h * MLA_HW:(h + 1) * MLA_HW] + kp).astype(BF16)
        vt_ref[h * VT_ROWS:h * VT_ROWS + V_HD_B] = vt[h * V_HD_B:(h + 1) * V_HD_B].astype(BF16)
        vt_ref[h * VT_ROWS + V_HD_B:(h + 1) * VT_ROWS] = ones


def _kvup(ckv, kpe, wukv, wvbt, l):
    nb, _, past, _ = ckv.shape
    return pl.pallas_call(
        _kvup_kernel,
        grid=(nb,),
        in_specs=[pl.BlockSpec((None, None, past, KV_RANK_B), lambda b: (b, l, 0, 0)),
                  pl.BlockSpec((None, None, past, LANE), lambda b: (b, l, 0, 0)),
                  _layer_spec(wukv, l), _layer_spec(wvbt, l)],
        out_specs=[pl.BlockSpec((past, HB_COLS), lambda b: (b, 0)),
                   pl.BlockSpec((None, VBT_ROWS, past), lambda b: (b, 0, 0))],
        out_shape=[jax.ShapeDtypeStruct((nb * past, HB_COLS), BF16),
                   jax.ShapeDtypeStruct((nb, VBT_ROWS, past), BF16)],
        name="mla_cache_up",
    )(ckv, kpe, wukv, wvbt)


def _scores_t(q_tiles, k_tiles, bias_t):
    st = _bdot_nt(jnp.stack(k_tiles), jnp.stack(q_tiles))
    return st if bias_t is None else st + bias_t[None]


def _softmax_pv(sts, vts, sink):
    m = jnp.max(sts[0], axis=1, keepdims=True)
    for st in sts[1:]:
        m = jnp.maximum(m, jnp.max(st, axis=1, keepdims=True))
    if sink is not None:
        m = jnp.maximum(m, sink)
    ot = _bdot(jnp.stack(vts[0]), jnp.exp2(sts[0] - m).astype(BF16))
    for st, vt in zip(sts[1:], vts[1:]):
        ot = ot + _bdot(jnp.stack(vt), jnp.exp2(st - m).astype(BF16))
    den = ot[:, V_HD_B:V_HD_B + 1, :]
    if sink is not None:
        den = den + jnp.exp2(sink - m)
    num = ot[:, :V_HD_B, :] / den
    return [jnp.concatenate([num[2 * i], num[2 * i + 1]], axis=0).T for i in range(sts[0].shape[0] // 2)]


def _tile(x, t):
    return x[:, t * LANE:(t + 1) * LANE]


def _attn_a_heads(q, segments, sink_ref):
    sts, vts = [], []
    for ka, vat, bias_t in segments:
        qs, ks, vs = [], [], []
        for t in range(N_HEADS_A // 2):
            g = (2 * t) // GQA_GROUP
            for e in range(2):
                qs.append(_tile(q, t))
                ks.append(_tile(ka, 2 * g + e))
                vs.append(vat[g * VT_ROWS:(g + 1) * VT_ROWS])
        sts.append(_scores_t(qs, ks, bias_t))
        vts.append(vs)
    sink = jnp.stack([sink_ref[:, h:h + 1] * LOG2E for h in range(N_HEADS_A)])
    return _softmax_pv(sts, vts, sink)


def _gated_store(outs, z_ref, o_ref, first_tile=0):
    for i, o in enumerate(outs):
        t = first_tile + i
        z = _tile(z_ref, t).astype(F32)
        o_ref[:, t * LANE:(t + 1) * LANE] = (o * _silu(z)).astype(BF16)


def _attn_ctx_kernel(qa_ref, ka_ref, vat_ref, qb_ref, kb_ref, vbt_ref, za_ref, zb_ref, sink_ref, oa_ref, ob_ref):
    _gated_store(_attn_a_heads(qa_ref[...], [(ka_ref[...], vat_ref[...], None)], sink_ref), za_ref, oa_ref)
    q, kb, vbt = qb_ref[...], kb_ref[...], vbt_ref[...]
    heads = range(N_HEADS_B)
    st = _scores_t([_tile(q, h) for h in heads], [_tile(kb, h) for h in heads], None)
    _gated_store(_softmax_pv([st], [[vbt[h * VT_ROWS:(h + 1) * VT_ROWS] for h in heads]], None), zb_ref, ob_ref)


def _attn_ctx(qa, ka, vat, qb, kb, vbt, z, sink, l, seq):
    t = qa.shape[0]
    row = lambda b: (b, 0)
    col = lambda b: (0, b)
    return pl.pallas_call(
        _attn_ctx_kernel,
        grid=(t // seq,),
        in_specs=[pl.BlockSpec((seq, W_A), row),
                  pl.BlockSpec((seq, KA_COLS), row),
                  pl.BlockSpec((VAT_ROWS, seq), col),
                  pl.BlockSpec((seq, HB_COLS), row),
                  pl.BlockSpec((seq, HB_COLS), row),
                  pl.BlockSpec((VBT_ROWS, seq), col),
                  pl.BlockSpec((seq, W_A), row),
                  pl.BlockSpec((seq, W_B), lambda b: (b, 1)),
                  _layer_spec(sink, l)],
        out_specs=[pl.BlockSpec((seq, W_A), row), pl.BlockSpec((seq, W_B), row)],
        out_shape=[jax.ShapeDtypeStruct((t, W_A), BF16), jax.ShapeDtypeStruct((t, W_B), BF16)],
        compiler_params=pltpu.CompilerParams(dimension_semantics=("parallel",)),
        name="attn_ctx",
    )(qa, ka, vat, qb, kb, vbt, z, z, sink)


def _attn_a_lat_kernel(q_ref, kp_ref, kc_ref, kn_ref, vp_ref, vc_ref, vn_ref, kx_ref, vx_ref, z_ref, sink_ref,
                       o_ref, *, nq):
    j = pl.program_id(1)
    ka = jnp.concatenate([kp_ref[...], kc_ref[...], kn_ref[...]], axis=0)
    vat = jnp.concatenate([vp_ref[...], vc_ref[...], vn_ref[...]], axis=1)
    kj = lax.broadcasted_iota(jnp.int32, (ka.shape[0], TM), 0)
    qi = lax.broadcasted_iota(jnp.int32, (ka.shape[0], TM), 1)
    ok = (kj >= qi) & (kj <= qi + 2 * WINDOW)
    ok = ok & ((kj >= WINDOW) | (j > 0)) & ((kj < TM + WINDOW) | (j < nq - 1))
    bias_t = jnp.where(ok, 0.0, NEG_INF)
    segments = [(ka, vat, bias_t), (kx_ref[...], vx_ref[...], None)]
    _gated_store(_attn_a_heads(q_ref[...], segments, sink_ref), z_ref, o_ref)


def _attn_a_lat(qa, ka, vat, z, sink, kx, vxt, l, seq):
    t = qa.shape[0]
    nq = seq // TM
    past = kx.shape[2]
    r = TM // WINDOW
    row = lambda b, j: (b * nq + j, 0)
    prev = lambda b, j: ((b * nq + j) * r - jnp.where(j > 0, 1, 0), 0)
    nxt = lambda b, j: ((b * nq + j) * r + jnp.where(j < nq - 1, r, r - 1), 0)
    swap = lambda f: (lambda b, j: f(b, j)[::-1])
    return pl.pallas_call(
        functools.partial(_attn_a_lat_kernel, nq=nq),
        grid=(t // seq, nq),
        in_specs=[pl.BlockSpec((TM, W_A), row),
                  pl.BlockSpec((WINDOW, KA_COLS), prev),
                  pl.BlockSpec((TM, KA_COLS), row),
                  pl.BlockSpec((WINDOW, KA_COLS), nxt),
                  pl.BlockSpec((VAT_ROWS, WINDOW), swap(prev)),
                  pl.BlockSpec((VAT_ROWS, TM), swap(row)),
                  pl.BlockSpec((VAT_ROWS, WINDOW), swap(nxt)),
                  pl.BlockSpec((None, None, past, KA_COLS), lambda b, j: (b, l, 0, 0)),
                  pl.BlockSpec((None, None, VAT_ROWS, past), lambda b, j: (b, l, 0, 0)),
                  pl.BlockSpec((TM, W_A), row),
                  _layer_spec(sink, l)],
        out_specs=pl.BlockSpec((TM, W_A), row),
        out_shape=jax.ShapeDtypeStruct((t, W_A), BF16),
        compiler_params=pltpu.CompilerParams(dimension_semantics=("parallel", "parallel")),
        name="attn_a_lat",
    )(qa, ka, ka, ka, vat, vat, vat, kx, vxt, z, sink)


def _attn_b_lat_kernel(q_ref, k_ref, vt_ref, kx_ref, vxt_ref, z_ref, o_ref, *, group):
    kb = jnp.concatenate([k_ref[...], kx_ref[...]], axis=0)
    vbt = jnp.concatenate([vt_ref[...], vxt_ref[...]], axis=1)
    q = q_ref[...]

    def scores(h0):
        heads = range(h0, h0 + group)
        return _scores_t([_tile(q, h) for h in heads], [_tile(kb, h) for h in heads], None)

    st = scores(0)
    for h0 in range(0, N_HEADS_B, group):
        st_next = scores(h0 + group) if h0 + group < N_HEADS_B else None
        outs = _softmax_pv([st], [[vbt[h * VT_ROWS:(h + 1) * VT_ROWS] for h in range(h0, h0 + group)]], None)
        _gated_store(outs, z_ref, o_ref, h0 // 2)
        st = st_next


def _attn_b_lat(qb, kb, vbt, z, kx, vxt, seq, qblk, group):
    t = qb.shape[0]
    nq = seq // qblk
    past = kx.shape[0] // (t // seq)
    hw = HB_COLS
    return pl.pallas_call(
        functools.partial(_attn_b_lat_kernel, group=group),
        grid=(t // seq, nq),
        in_specs=[pl.BlockSpec((qblk, hw), lambda b, j: (b * nq + j, 0)),
                  pl.BlockSpec((seq, hw), lambda b, j: (b, 0)),
                  pl.BlockSpec((VBT_ROWS, seq), lambda b, j: (0, b)),
                  pl.BlockSpec((past, hw), lambda b, j: (b, 0)),
                  pl.BlockSpec((None, VBT_ROWS, past), lambda b, j: (b, 0, 0)),
                  pl.BlockSpec((qblk, W_B), lambda b, j: (b * nq + j, 1))],
        out_specs=pl.BlockSpec((qblk, W_B), lambda b, j: (b * nq + j, 0)),
        out_shape=jax.ShapeDtypeStruct((t, W_B), BF16),
        compiler_params=pltpu.CompilerParams(dimension_semantics=("parallel", "parallel")),
        name="attn_b_lat",
    )(qb, kb, vbt, kx, vxt, z)


def _gdn_local_kernel(cq_ref, prev_ref, next_ref, small_ref, abt_ref, cw_ref, prow_ref, pcol_ref,
                      u_ref, w_ref, qg_ref, kd_ref, attn_ref, eg_ref, qkv_scr, gb_scr, *, seq):
    x = cq_ref[...].astype(F32)
    tiles_per_seq = seq // TMG
    tpos = pl.program_id(0) % tiles_per_seq
    prev_row = jnp.where(tpos > 0, prev_ref[...].astype(F32)[HALO - 1:HALO, :], 0.0)
    next_row = jnp.where(tpos < tiles_per_seq - 1, next_ref[...].astype(F32)[0:1, :], 0.0)
    rows = lax.broadcasted_iota(jnp.int32, (TMG, 1), 0)
    xm1 = jnp.where(rows == 0, prev_row, pltpu.roll(x, 1, 0))
    xp1 = jnp.where(rows == TMG - 1, next_row, pltpu.roll(x, TMG - 1, 0))
    cw = cw_ref[...]
    y = _silu(xm1 * cw[0:1] + x * cw[1:2] + xp1 * cw[2:3])
    nq = N_HEADS_C * DK_C
    for h in range(N_HEADS_C):
        qh = y[:, h * DK_C:(h + 1) * DK_C]
        kh = y[:, nq + h * DK_C:nq + (h + 1) * DK_C]
        qkv_scr[:, h * DK_C:(h + 1) * DK_C] = (
            qh * lax.rsqrt(jnp.sum(qh * qh, axis=-1, keepdims=True) + EPS) * (DK_C ** -0.5))
        qkv_scr[:, nq + h * DK_C:nq + (h + 1) * DK_C] = kh * lax.rsqrt(jnp.sum(kh * kh, axis=-1, keepdims=True) + EPS)
    qkv_scr[:, 2 * nq:] = y[:, 2 * nq:]

    sm = small_ref[...]
    prow = prow_ref[...]
    gb_scr[:, 0:8] = -jnp.exp(prow[0:1]) * _softplus(sm[:, S_A:S_A + 8] + prow[1:2])
    gb_scr[:, 8:16] = _sigmoid(sm[:, S_B:S_B + 8])
    pcol = pcol_ref[...]

    ri = lax.broadcasted_iota(jnp.int32, (CHUNK, LANE), 0)
    lane = lax.broadcasted_iota(jnp.int32, (CHUNK, LANE), 1)
    fwd = lane < CHUNK
    cj = lane & (CHUNK - 1)
    incl = (fwd & (ri >= cj)) | (~fwd & (ri <= cj))
    strict = (fwd & (ri > cj)) | (~fwd & (ri < cj))
    xor = ri ^ cj
    eye = (ri == cj).astype(F32)
    r2 = lax.broadcasted_iota(jnp.int32, (2 * CHUNK, LANE), 0)
    l2 = lax.broadcasted_iota(jnp.int32, (2 * CHUNK, LANE), 1)
    same_dir = (r2 < CHUNK) == (l2 < CHUNK)
    rs_ = lax.broadcasted_iota(jnp.int32, (CHUNK, CHUNK), 0)
    cs_ = lax.broadcasted_iota(jnp.int32, (CHUNK, CHUNK), 1)
    tril = (rs_ >= cs_).astype(F32)
    triu = (rs_ <= cs_).astype(F32)
    tri_rows = jnp.concatenate([triu, tril], axis=1)
    dup_rows = jnp.concatenate([(rs_ == cs_).astype(F32)] * 2, axis=1)

    def block_diag(x):
        return jnp.where(same_dir[None], jnp.concatenate([x, x], axis=1), 0.0).astype(BF16)

    lows, rhss, order = [], [], []

    def solve():
        low = jnp.stack(lows, axis=0)
        inv = eye[None] - jnp.where(xor[None] == 1, low, 0.0)
        b = 2
        while b < CHUNK:
            cpl = jnp.where((xor[None] >= b) & (xor[None] < 2 * b), low, 0.0)
            tmp = _bdot(cpl.astype(BF16), block_diag(inv))
            inv = inv - _bdot(inv.astype(BF16), block_diag(tmp))
            b *= 2
        scale_u, scale_w, vs, ks = (jnp.stack(a, axis=0) for a in zip(*rhss))
        u = _bdot((inv * scale_u).astype(BF16), vs)
        w = _bdot((inv * scale_w).astype(BF16), ks)
        for i, (rs, h) in enumerate(order):
            for d in range(2):
                cs = slice((d * N_HEADS_C + h) * DK_C, (d * N_HEADS_C + h + 1) * DK_C)
                u_ref[rs, cs] = u[i, :, d * DV_C:(d + 1) * DV_C].astype(BF16)
                w_ref[rs, cs] = w[i, :, d * DK_C:(d + 1) * DK_C].astype(BF16)
        lows.clear(), rhss.clear(), order.clear()

    for c in range(TMG // CHUNK):
        if c % SOLVE_CHUNKS == 0 and c > 0:
            solve()
        rs = slice(c * CHUNK, (c + 1) * CHUNK)
        gcol = gb_scr[rs, 0:8]
        bcol = gb_scr[rs, 8:16]
        abt = abt_ref[c]
        grow = -jnp.exp(pcol[:, 0:1]) * _softplus(abt[0:8] + pcol[:, 1:2])
        gc_f = _dot_exact(tril, gcol)
        gc_b = _dot_exact(triu, gcol)
        gr = _dot_exact(grow, tri_rows)
        br = _dot_exact(_sigmoid(abt[8:16]), dup_rows)
        for h in range(N_HEADS_C):
            hb_ = N_HEADS_C + h
            q = qkv_scr[rs, h * DK_C:(h + 1) * DK_C]
            k = qkv_scr[rs, nq + h * DK_C:nq + (h + 1) * DK_C]
            v = qkv_scr[rs, 2 * nq + h * DV_C:2 * nq + (h + 1) * DV_C]
            k16 = k.astype(BF16)
            kk16 = jnp.concatenate([k16, k16], axis=0)
            kk = _dot_nt(k16, kk16)
            qk = _dot_nt(q.astype(BF16), kk16)
            gcs = (gc_f[:, h:h + 1], gc_b[:, hb_:hb_ + 1])
            betas = (bcol[:, h:h + 1], bcol[:, hb_:hb_ + 1])
            gc2 = jnp.where(fwd, gcs[0], gcs[1])
            gr2 = jnp.where(fwd[0:1], gr[h:h + 1, :], gr[hb_:hb_ + 1, :])
            decay = jnp.where(incl, jnp.exp(jnp.where(incl, gc2 - gr2, 0.0)), 0.0)
            lows.append(jnp.where(strict, jnp.where(fwd, betas[0], betas[1]) * kk * decay, 0.0))
            attn_ref[rs, h * LANE:(h + 1) * LANE] = (qk * decay).astype(BF16)
            order.append((rs, h))
            br2 = jnp.where(fwd[0:1], br[h:h + 1, :], br[hb_:hb_ + 1, :])
            v16 = v.astype(BF16)
            zeros = jnp.zeros_like(v16)
            rhss.append((br2, br2 * jnp.exp(gr2),
                         jnp.concatenate([jnp.concatenate([v16, zeros], axis=1),
                                          jnp.concatenate([zeros, v16], axis=1)], axis=0),
                         jnp.concatenate([jnp.concatenate([k16, zeros], axis=1),
                                          jnp.concatenate([zeros, k16], axis=1)], axis=0)))
            for d in range(2):
                dh = d * N_HEADS_C + h
                gc = gcs[d]
                eg = jnp.exp(gc)
                g_last = gc[CHUNK - 1:CHUNK] if d == 0 else gc[0:1]
                cs = slice(dh * DK_C, (dh + 1) * DK_C)
                qg_ref[rs, cs] = (q * eg).astype(BF16)
                kd_ref[rs, cs] = (k * jnp.exp(g_last - gc)).astype(BF16)
                eg_ref[c, dh:dh + 1, :] = jnp.broadcast_to(jnp.exp(g_last), (1, LANE))
    solve()


def _gdn_local(cqkv, small, abt, conv_w, prow, pcol, l, seq):
    t = cqkv.shape[0]
    assert seq % TMG == 0
    nt = t // TMG
    nh8 = t // HALO
    cpg = TMG // CHUNK
    row = lambda i: (i, 0)
    dh = 2 * N_HEADS_C
    return pl.pallas_call(
        functools.partial(_gdn_local_kernel, seq=seq),
        grid=(nt,),
        in_specs=[pl.BlockSpec((TMG, QKV_C), row),
                  pl.BlockSpec((HALO, QKV_C), lambda i: (jnp.maximum(i * (TMG // HALO) - 1, 0), 0)),
                  pl.BlockSpec((HALO, QKV_C), lambda i: (jnp.minimum((i + 1) * (TMG // HALO), nh8 - 1), 0)),
                  pl.BlockSpec((TMG, LANE), row),
                  pl.BlockSpec((cpg, 16, CHUNK), lambda i: (i, 0, 0)),
                  _layer_spec(conv_w, l), _layer_spec(prow, l), _layer_spec(pcol, l)],
        out_specs=[pl.BlockSpec((TMG, dh * DV_C), row),
                   pl.BlockSpec((TMG, dh * DK_C), row),
                   pl.BlockSpec((TMG, dh * DK_C), row),
                   pl.BlockSpec((TMG, dh * DK_C), row),
                   pl.BlockSpec((TMG, dh * CHUNK), row),
                   pl.BlockSpec((cpg, dh, LANE), lambda i: (i, 0, 0))],
        out_shape=[jax.ShapeDtypeStruct((t, dh * DV_C), BF16),
                   jax.ShapeDtypeStruct((t, dh * DK_C), BF16),
                   jax.ShapeDtypeStruct((t, dh * DK_C), BF16),
                   jax.ShapeDtypeStruct((t, dh * DK_C), BF16),
                   jax.ShapeDtypeStruct((t, dh * CHUNK), BF16),
                   jax.ShapeDtypeStruct((t // CHUNK, dh, LANE), F32)],
        scratch_shapes=[pltpu.VMEM((TMG, QKV_C), F32), pltpu.VMEM((TMG, 16), F32)],
        compiler_params=pltpu.CompilerParams(dimension_semantics=("parallel",)),
        name="gdn_local",
    )(cqkv, cqkv, cqkv, small, abt, conv_w, prow, pcol)


def _gdn_scan_kernel(*refs, nt, ns, has_init, want_state, n_carry):
    it = iter(refs)
    ins = [[next(it) for _ in range(6)] for _ in range(2)]
    s0_ref = next(it) if has_init else None
    for _ in range(n_carry):
        next(it)
    o_refs = [next(it), next(it)]
    st_ref = next(it) if want_state else None
    s_scr = next(it)
    j = pl.program_id(1)
    nh = N_HEADS_C
    nst = ns * 2 * nh

    @pl.when(j == 0)
    def _():
        if has_init:
            s_scr[...] = s0_ref[...].reshape(nst, DK_C, DV_C)
        else:
            s_scr[...] = jnp.zeros_like(s_scr)

    for step in range(CPT):
        chunk = lambda d: step if d == 0 else CPT - 1 - step

        def gather(idx, width):
            return jnp.stack([ins[d][idx][s, chunk(d) * CHUNK:(chunk(d) + 1) * CHUNK, h * width:(h + 1) * width]
                              for s in range(ns) for d in range(2) for h in range(nh)])

        u, w, qg, kd, attn = gather(0, DV_C), gather(1, DK_C), gather(2, DK_C), gather(3, DK_C), gather(4, LANE)
        eg = jnp.stack([ins[d][5][s, chunk(d), d * nh + h:d * nh + h + 1, :]
                        for s in range(ns) for d in range(2) for h in range(nh)])
        st = s_scr[...]
        sb = st.astype(BF16)
        v_new = u.astype(F32) - _bdot(w, sb)
        vb = v_new.astype(BF16)
        zeros = jnp.zeros((CHUNK, DV_C), BF16)
        vb2 = jnp.stack([jnp.concatenate([vb[i], zeros] if (i // nh) % 2 == 0 else [zeros, vb[i]], axis=0)
                         for i in range(nst)])
        o = _bdot(qg, sb) + _bdot(attn, vb2)
        s_scr[...] = st * eg + _bdot_tn(kd, vb)
        for s in range(ns):
            for d in range(2):
                for h in range(nh):
                    o_refs[d][s, chunk(d) * CHUNK:(chunk(d) + 1) * CHUNK, h * DV_C:(h + 1) * DV_C] = (
                        o[(s * 2 + d) * nh + h].astype(BF16))

    if want_state:
        @pl.when(j == nt - 1)
        def _():
            st_ref[...] = s_scr[...].reshape(ns, 2, nh, DK_C, DV_C)


def _gdn_scan(u, w, qg, kd, attn, eg, s0, l, seq, state_out=None):
    t = u.shape[0]
    want_state = state_out is not None
    nt = seq // TM
    nb = t // seq
    ns = next(n for n in (SCAN_SEQS, 2, 1) if nb % n == 0)
    half = N_HEADS_C * DK_C
    has_init = s0 is not None
    by_seq = lambda a: a.reshape((nb, a.shape[0] // nb) + a.shape[1:])
    in_specs, args = [], []
    for d in range(2):
        tile = (lambda b, j: j) if d == 0 else (lambda b, j: nt - 1 - j)
        row = lambda b, j, d=d, tile=tile: (b, tile(b, j), d)
        row4 = lambda b, j, tile=tile: (b, tile(b, j), 0, 0)
        in_specs += [pl.BlockSpec((ns, TM, half), row)] * 4
        in_specs += [pl.BlockSpec((ns, TM, N_HEADS_C * LANE), lambda b, j, tile=tile: (b, tile(b, j), 0)),
                     pl.BlockSpec((ns, CPT, 2 * N_HEADS_C, LANE), row4)]
        args += [by_seq(a) for a in (u, w, qg, kd, attn, eg)]
    st_tail = (2, N_HEADS_C, DK_C, DV_C)
    if has_init:
        in_specs.append(pl.BlockSpec((ns, None) + st_tail, lambda b, j: (b, l, 0, 0, 0, 0)))
        args.append(s0)
    out_specs = [pl.BlockSpec((ns, TM, half), lambda b, j: (b, j, 0)),
                 pl.BlockSpec((ns, TM, half), lambda b, j: (b, nt - 1 - j, 0))]
    out_shape = [jax.ShapeDtypeStruct((nb, seq, half), BF16)] * 2
    aliases, n_carry = {}, 0
    if want_state:
        depth, carry = state_out
        out_specs.append(pl.BlockSpec((ns, None) + st_tail, lambda b, j: (b, l, 0, 0, 0, 0)))
        out_shape.append(jax.ShapeDtypeStruct((nb, depth) + st_tail, F32))
        n_carry = 1
        aliases[len(args)] = 2
        in_specs.append(pl.BlockSpec(memory_space=pl.ANY))
        args.append(carry)
    outs = pl.pallas_call(
        functools.partial(_gdn_scan_kernel, nt=nt, ns=ns, has_init=has_init, want_state=want_state,
                          n_carry=n_carry),
        grid=(nb // ns, nt),
        in_specs=in_specs,
        out_specs=out_specs,
        out_shape=out_shape,
        input_output_aliases=aliases,
        scratch_shapes=[pltpu.VMEM((ns * 2 * N_HEADS_C, DK_C, DV_C), F32)],
        compiler_params=pltpu.CompilerParams(dimension_semantics=("parallel", "arbitrary")),
        name="gdn_scan",
    )(*args)
    return [outs[0].reshape(t, half), outs[1].reshape(t, half)] + list(outs[2:])


def _merge_kernel(x_ref, mod_ref, oa_ref, ob_ref, cf_ref, cb_ref, zc_ref, gates_ref, gn_ref,
                  wa_ref, wb_ref, wc_ref, wo_ref, fg_ref, o_ref, *, last):
    oc = cf_ref[...].astype(F32) + cb_ref[...].astype(F32)
    zc = zc_ref[...].astype(F32)
    gn = gn_ref[...]
    parts = []
    for h in range(N_HEADS_C):
        hs = slice(h * DV_C, (h + 1) * DV_C)
        och = oc[:, hs]
        och = och * lax.rsqrt(jnp.mean(och * och, axis=-1, keepdims=True) + EPS) * gn
        parts.append((och * _silu(zc[:, hs])).astype(BF16))
    ocz = jnp.concatenate(parts, axis=-1)
    pa = _dot(oa_ref[...], wa_ref[...])
    pb = _dot(ob_ref[...], wb_ref[...])
    pc = _dot(ocz, wc_ref[...])
    ga = _sigmoid(gates_ref[:, 0:D_MODEL].astype(F32))
    gb = _sigmoid(gates_ref[:, D_MODEL:2 * D_MODEL].astype(F32))
    gc = _sigmoid(gates_ref[:, 2 * D_MODEL:].astype(F32))
    y = _dot((ga * pa + gb * pb + gc * pc).astype(BF16), wo_ref[...])
    gate = mod_ref[0][:, 2 * D_MODEL:]
    xo = x_ref[...] + gate * y
    if last:
        xo = xo * lax.rsqrt(jnp.mean(xo * xo, axis=-1, keepdims=True) + EPS) * fg_ref[...]
    o_ref[...] = xo


def _merge(x2d, l, mod, mod_row_fn, oa, ob, cf, cb, z, gates, gn, wa, wb, wc, wo, fg, last):
    t = x2d.shape[0]
    row = lambda i: (i, 0)
    return pl.pallas_call(
        functools.partial(_merge_kernel, last=last),
        grid=(t // TMD,),
        in_specs=[pl.BlockSpec((TMD, D_MODEL), row),
                  pl.BlockSpec((1, 1, 3 * D_MODEL), lambda i: (mod_row_fn(i), 0, 0)),
                  pl.BlockSpec((TMD, W_A), row),
                  pl.BlockSpec((TMD, W_B), row),
                  pl.BlockSpec((TMD, W_C), row),
                  pl.BlockSpec((TMD, W_C), row),
                  pl.BlockSpec((TMD, W_C), lambda i: (i, 2)),
                  pl.BlockSpec((TMD, 3 * D_MODEL), row),
                  _layer_spec(gn, l), _layer_spec(wa, l), _layer_spec(wb, l), _layer_spec(wc, l),
                  _layer_spec(wo, l),
                  pl.BlockSpec((1, D_MODEL), lambda i: (0, 0))],
        out_specs=pl.BlockSpec((TMD, D_MODEL), row),
        out_shape=jax.ShapeDtypeStruct((t, D_MODEL), F32),
        compiler_params=pltpu.CompilerParams(dimension_semantics=("parallel",)),
        name="merge",
    )(x2d, mod, oa, ob, cf, cb, z, gates, gn, wa, wb, wc, wo, fg)


def _rope_tables(n_tokens, rot_dim):
    rows = n_tokens // GRID_W
    row = np.repeat(np.arange(rows), GRID_W).astype(np.float32)
    col = np.tile(np.arange(GRID_W), rows).astype(np.float32)
    n_pairs = rot_dim // 4
    inv = (np.float32(ROPE_BASE) ** (-np.arange(n_pairs, dtype=np.float32) / np.float32(n_pairs))).astype(np.float32)
    ang = np.concatenate([row[:, None] * inv, col[:, None] * inv], axis=-1)
    c, s = np.cos(ang), np.sin(ang)
    return np.repeat(c, 2, axis=-1), np.stack([-s, s], axis=-1).reshape(n_tokens, rot_dim)


def _in_offsets():
    o = [0]
    for n in IN_SIZES:
        o.append(o[-1] + n)
    return o


def _relayout_moves():
    o = _in_offsets()
    order = [(o[0], W_A), (o[1], LANE), (o[1] + HD_A, HD_A), (o[1], HD_A), (o[2], LANE),
             (o[3], W_A), (o[7], W_B), (o[11], W_C), (o[12], 3 * D_MODEL), (o[4], Q_RANK_B), (o[5], KV_RANK_B),
             (None, S_KPE), (o[6], QK_ROPE_B), (o[9], 4 * N_HEADS_C), (None, LANE - S_B - 8), (o[8], QKV_C)]
    moves, dst = [], 0
    for src, n in order:
        moves.append((src, dst, n))
        dst += n
    assert dst == P_END
    return moves


def _relayout_kernel(w_ref, o_ref, wab_ref, wvat_ref):
    for src, dst, n in _relayout_moves():
        if src is None:
            o_ref[dst:dst + n, :] = jnp.zeros((n, o_ref.shape[1]), BF16)
        else:
            o_ref[dst:dst + n, :] = w_ref[src:src + n, :].astype(BF16)
    o = _in_offsets()
    wab_ref[...] = w_ref[o[9]:o[11], :].astype(BF16)
    wvat_ref[...] = w_ref[o[2]:o[3], :].astype(BF16)


def _relayout_w_in(w_in):
    w_t = jnp.swapaxes(w_in, 1, 2)
    depth, width, _ = w_t.shape
    cols = 128
    nab, nv = 4 * N_HEADS_C, N_KV_A * HD_A
    col = lambda l, i: (l, 0, i)
    return pl.pallas_call(
        _relayout_kernel,
        grid=(depth, D_MODEL // cols),
        in_specs=[pl.BlockSpec((None, width, cols), col)],
        out_specs=[pl.BlockSpec((None, P_END, cols), col),
                   pl.BlockSpec((None, nab, cols), col),
                   pl.BlockSpec((None, nv, cols), col)],
        out_shape=[jax.ShapeDtypeStruct((depth, P_END, D_MODEL), BF16),
                   jax.ShapeDtypeStruct((depth, nab, D_MODEL), BF16),
                   jax.ShapeDtypeStruct((depth, nv, D_MODEL), BF16)],
        name="w_in_relayout",
    )(w_t)


def _prep_weights(w_in, w_uq, w_ukv):
    depth = w_in.shape[0]
    wp, wab, wvat = _relayout_w_in(w_in)
    hd = QK_NOPE_B + QK_ROPE_B
    wuq = jnp.pad(w_uq.reshape(depth, Q_RANK_B, N_HEADS_B, hd), ((0, 0), (0, 0), (0, 0), (0, MLA_HW - hd)))
    wuq = wuq.reshape(depth, Q_RANK_B, HB_COLS).astype(BF16)
    kv = w_ukv.reshape(depth, KV_RANK_B, N_HEADS_B, QK_NOPE_B + V_HD_B)
    wk = jnp.pad(kv[..., :QK_NOPE_B], ((0, 0), (0, 0), (0, 0), (0, MLA_HW - QK_NOPE_B)))
    wukv = wk.reshape(depth, KV_RANK_B, HB_COLS).astype(BF16)
    wvbt = jnp.swapaxes(kv[..., QK_NOPE_B:].reshape(depth, KV_RANK_B, W_B), 1, 2).astype(BF16)
    return wp, wab, wvat, wuq, wukv, wvbt


def _cache_tiles_a(kx, vx):
    k0, k1 = kx[..., 0, :], kx[..., 1, :]
    z = jnp.zeros_like(k0)
    ka = jnp.concatenate([k0, z, z, k0, k1, z, z, k1], axis=-1).astype(BF16)
    vt = jnp.transpose(vx, (0, 1, 3, 4, 2))
    vt = jnp.concatenate([vt, jnp.ones(vt.shape[:3] + (VT_ONES, vt.shape[4]), vt.dtype)], axis=3)
    return ka, vt.reshape(vt.shape[:2] + (VAT_ROWS, vt.shape[4])).astype(BF16)


def kernel(x_prompt, x_sample, cache_attn_k, cache_attn_v, cache_mla_ckv, cache_mla_kpe, state_gdn, c, c_ctx,
           norm_g, w_ada, b_ada, w_in, attn_sink, mla_q_norm, mla_w_uq, mla_kv_norm, mla_w_ukv, gdn_conv,
           gdn_a_log, gdn_dt_bias, gdn_norm, w_branch_a, w_branch_b, w_branch_c, w_out, final_norm_g):
    depth = w_in.shape[0]
    nb_c, seq_c, _ = x_prompt.shape
    nb_l, seq_l, _ = x_sample.shape
    past = cache_attn_k.shape[2]
    assert P_END % LANE == 0 and seq_c % TM == 0 and seq_l % TMD == 0 and nb_l < 8 and TM == 2 * WINDOW
    assert (nb_c * seq_c) % TMD == 0 and TMD % TM == 0

    cond8 = jnp.zeros((8, D_MODEL), F32).at[:nb_l].set(c).at[nb_l].set(c_ctx)
    mod = _modulation(cond8, w_ada, b_ada).reshape(depth * 8, 1, 3 * D_MODEL)

    c_a, s_a = _rope_tables(seq_l, HD_A)
    c_b, s_b = _rope_tables(seq_l, QK_ROPE_B)
    pad_l, pad_r = S_KPE, LANE - S_KPE - QK_ROPE_B
    one, zero = np.ones((seq_l, 1), np.float32), np.zeros((seq_l, 1), np.float32)
    rope_tabs = tuple(jnp.asarray(a) for a in (
        np.tile(c_a, (1, LANE // HD_A)), np.tile(s_a, (1, LANE // HD_A)),
        np.concatenate([np.tile(one, (1, pad_l)), c_b, np.tile(one, (1, pad_r))], 1),
        np.concatenate([np.tile(zero, (1, pad_l)), s_b, np.tile(zero, (1, pad_r))], 1)))

    weights = _prep_weights(w_in, mla_w_uq, mla_w_ukv)
    wukv, wvbt = weights[4], weights[5]
    ng = norm_g.reshape(depth, 1, D_MODEL)
    qn = mla_q_norm.reshape(depth, 1, Q_RANK_B)
    kvn = mla_kv_norm.reshape(depth, 1, KV_RANK_B)
    sink = attn_sink.reshape(depth, 1, N_HEADS_A)
    prow = jnp.stack([gdn_a_log.reshape(depth, -1), gdn_dt_bias.reshape(depth, -1)], axis=1)
    pcol = jnp.swapaxes(prow, 1, 2)
    gn = gdn_norm.reshape(depth, 1, DV_C)
    wa, wb, wc, wo = (w.astype(BF16) for w in (w_branch_a, w_branch_b, w_branch_c, w_out))
    fg = final_norm_g.reshape(1, D_MODEL)
    kxa, vxa = _cache_tiles_a(cache_attn_k, cache_attn_v)
    kpex = jnp.pad(cache_mla_kpe, ((0, 0), (0, 0), (0, 0), (pad_l, pad_r)))

    tps_c, tps_l = seq_c // TM, seq_l // TM
    tpd_l = seq_l // TMD
    y_p = x_prompt.reshape(nb_c * seq_c, D_MODEL)
    y_s = x_sample.reshape(nb_l * seq_l, D_MODEL)
    new_cache = tuple(jnp.zeros((nb_c, depth, seq_c, w), F32)
                      for w in (N_KV_A * HD_A, N_KV_A * HD_A, KV_RANK_B, QK_ROPE_B))
    new_state = jnp.zeros((nb_c, depth, 2, N_HEADS_C, DK_C, DV_C), F32)
    for l in range(depth):
        last = l == depth - 1

        mod_row_c = lambda i, l=l: l * 8 + nb_l
        outs = _inproj(y_p, l, mod, mod_row_c, ng, weights, qn, kvn, None, tps_c, (depth, seq_c, new_cache))
        (qa, ka, vat, z, gates, qb, kb, vbt, small, cqkv, abt), new_cache = outs[:11], tuple(outs[11:])
        oa, ob = _attn_ctx(qa, ka, vat, qb, kb, vbt, z, sink, l, seq_c)
        u, w, qg, kd, attn, eg = _gdn_local(cqkv, small, abt, gdn_conv, prow, pcol, l, seq_c)
        cf, cb, new_state = _gdn_scan(u, w, qg, kd, attn, eg, None, l, seq_c, (depth, new_state))
        y_p = _merge(y_p, l, mod, mod_row_c, oa, ob, cf, cb, z, gates, gn, wa, wb, wc, wo, fg, last)

        mod_row_l = lambda i, l=l: l * 8 + i // tpd_l
        (qa, ka, vat, z, gates, qb, kb, vbt, small, cqkv, abt) = _inproj(
            y_s, l, mod, mod_row_l, ng, weights, qn, kvn, rope_tabs, tpd_l)
        oa = _attn_a_lat(qa, ka, vat, z, sink, kxa, vxa, l, seq_l)
        kxb, vxb = _kvup(cache_mla_ckv, kpex, wukv, wvbt, l)
        ob = _attn_b_lat(qb, kb, vbt, z, kxb, vxb, seq_l, TM, N_HEADS_B // 2)
        u, w, qg, kd, attn, eg = _gdn_local(cqkv, small, abt, gdn_conv, prow, pcol, l, seq_l)
        cf, cb = _gdn_scan(u, w, qg, kd, attn, eg, state_gdn, l, seq_l)
        y_s = _merge(y_s, l, mod, mod_row_l, oa, ob, cf, cb, z, gates, gn, wa, wb, wc, wo, fg, last)

    new_k, new_v, new_ckv, new_kpe = new_cache
    kv_shape = (nb_c, depth, seq_c, N_KV_A, HD_A)
    return (y_p.reshape(nb_c, seq_c, D_MODEL), y_s.reshape(nb_l, seq_l, D_MODEL),
            new_k.reshape(kv_shape), new_v.reshape(kv_shape), new_ckv, new_kpe, new_state)
```

```python
import functools

import numpy as np
import jax
import jax.numpy as jnp
from jax import lax
from jax.experimental import pallas as pl
from jax.experimental.pallas import tpu as pltpu

F32 = jnp.float32
BF16 = jnp.bfloat16

D_MODEL = 1024
GRID_W = 64
ROPE_BASE = 10000.0
EPS = 1e-6
NEG_INF = -1e30
N_HEADS_A = 8
N_KV_A = 2
HD_A = 64
GQA_GROUP = N_HEADS_A // N_KV_A
WINDOW = 128
N_HEADS_B = 8
QK_NOPE_B = 64
QK_ROPE_B = 32
V_HD_B = 64
Q_RANK_B = 384
KV_RANK_B = 256
MLA_SCALE = (QK_NOPE_B + QK_ROPE_B) ** -0.5
N_HEADS_C = 4
DK_C = 128
DV_C = 128
CHUNK = 64
W_A = N_HEADS_A * HD_A
W_B = N_HEADS_B * V_HD_B
W_C = N_HEADS_C * DV_C
QKV_C = 2 * N_HEADS_C * DK_C + W_C
IN_SIZES = (W_A, N_KV_A * HD_A, N_KV_A * HD_A, W_A, Q_RANK_B, KV_RANK_B, QK_ROPE_B, W_B, QKV_C,
            2 * N_HEADS_C, 2 * N_HEADS_C, W_C, 3 * D_MODEL)

LANE = 128
HALF = LANE // 2
TM = 256
TMD = 512
TMG = 256
CPT = TM // CHUNK
SOLVE_CHUNKS = 4
SCAN_SEQS = 4
HALO = 16
MLA_HW = 128
KA_COLS = 4 * LANE
HB_COLS = N_HEADS_B * MLA_HW
VT_ONES = 16
VT_ROWS = V_HD_B + VT_ONES
VAT_ROWS = N_KV_A * VT_ROWS
VBT_ROWS = N_HEADS_B * VT_ROWS
LOG2E = 1.4426950408889634
SHIFT_MAX = 60.0
SHIFT_SLACK = 1.001
SHIFT_SLACK_ABS = 0.01

P_QKV = 0
A_COLS = W_A + 3 * LANE
P_Z = P_QKV + A_COLS
P_GATES = P_Z + 1536
P_CQ = P_GATES + 3 * D_MODEL
P_CKV = P_CQ + Q_RANK_B
P_SMALL = P_CKV + KV_RANK_B
P_CQKV = P_SMALL + LANE
P_END = P_CQKV + QKV_C
S_KPE = 64
S_A = 96
S_B = 104


def _sigmoid(x):
    return 0.5 * jnp.tanh(0.5 * x) + 0.5


def _silu(x):
    return x * _sigmoid(x)


def _softplus(x):
    return jnp.maximum(x, 0.0) + jnp.log(1.0 + jnp.exp(-jnp.abs(x)))


def _dot(a, b):
    return jnp.dot(a, b, preferred_element_type=F32)


def _dot_nt(a, b):
    return lax.dot_general(a, b, (((1,), (1,)), ((), ())), preferred_element_type=F32)


def _bdot(a, b):
    return lax.dot_general(a, b, (((2,), (1,)), ((0,), (0,))), preferred_element_type=F32)


def _bdot_nt(a, b):
    return lax.dot_general(a, b, (((2,), (2,)), ((0,), (0,))), preferred_element_type=F32)


def _bdot_tn(a, b):
    return lax.dot_general(a, b, (((1,), (1,)), ((0,), (0,))), preferred_element_type=F32)


def _dot_exact(a, b):
    return jnp.dot(a, b, preferred_element_type=F32, precision=lax.Precision.HIGHEST)


def _rope(x, c, s):
    n = x.shape[-1]
    lane = lax.broadcasted_iota(jnp.int32, x.shape, 1)
    swapped = jnp.where(lane % 2 == 0, pltpu.roll(x, n - 1, 1), pltpu.roll(x, 1, 1))
    return x * c + swapped * s


def _mod_kernel(cond_ref, w_ref, b_ref, out_ref):
    cnd = cond_ref[...]
    out_ref[0] = _dot(_silu(cnd).astype(BF16), w_ref[0].astype(BF16)) + b_ref[0]


def _modulation(cond8, w_ada, b_ada):
    depth = w_ada.shape[0]
    tn = 768
    return pl.pallas_call(
        _mod_kernel,
        grid=(depth, 3 * D_MODEL // tn),
        in_specs=[pl.BlockSpec((8, D_MODEL), lambda l, n: (0, 0)),
                  pl.BlockSpec((1, D_MODEL, tn), lambda l, n: (l, 0, n)),
                  pl.BlockSpec((1, 1, tn), lambda l, n: (l, 0, n))],
        out_specs=pl.BlockSpec((1, 8, tn), lambda l, n: (l, 0, n)),
        out_shape=jax.ShapeDtypeStruct((depth, 8, 3 * D_MODEL), F32),
        name="adaln_mod",
    )(cond8, w_ada, b_ada.reshape(depth, 1, 3 * D_MODEL))


def _inproj_kernel(*refs, rope, cache_seq, n_carry):
    it = iter(refs)
    (x_ref, mod_ref, ng_ref, wp_ref, wab_ref, wvat_ref, wuq_ref, wukv_ref, wvbt_ref, qn_ref,
     kvn_ref) = (next(it) for _ in range(11))
    if rope:
        ca_ref, sa_ref, cb_ref, sb_ref = (next(it) for _ in range(4))
    for _ in range(n_carry):
        next(it)
    (qa_ref, ka_ref, vat_ref, z_ref, gates_ref, qb_ref, kb_ref, vbt_ref, small_ref, cqkv_ref,
     abt_ref) = (next(it) for _ in range(11))
    if rope:
        qnorm_ref = next(it)
    if cache_seq:
        ck_ref, cv_ref, cckv_ref, ckpe_ref = (next(it) for _ in range(4))

    def to_cache(ref, val):
        for s in range(TMD // cache_seq):
            ref[s] = val[s * cache_seq:(s + 1) * cache_seq]

    x = x_ref[...]
    mod = mod_ref[0]
    shift, scale = mod[:, :D_MODEL], mod[:, D_MODEL:2 * D_MODEL]
    xn = x * lax.rsqrt(jnp.mean(x * x, axis=-1, keepdims=True) + EPS) * ng_ref[...]
    hb = (xn * (1.0 + scale) + shift).astype(BF16)
    lane = lax.broadcasted_iota(jnp.int32, (TMD, LANE), 1)
    lo = lane < HALF

    def mm(lo_col, hi_col):
        return _dot_nt(hb, wp_ref[lo_col:hi_col, :])

    r = mm(P_QKV, P_QKV + A_COLS)
    tiles = [r[:, t * LANE:(t + 1) * LANE] for t in range(A_COLS // LANE)]
    if cache_seq:
        to_cache(ck_ref, tiles[4])
        to_cache(cv_ref, tiles[6])
    if rope:
        ca, sa = ca_ref[...], sa_ref[...]
        tiles[:6] = [_rope(t, ca, sa) for t in tiles[:6]]
    for t in range(4):
        qa_ref[:, t * LANE:(t + 1) * LANE] = (tiles[t] * (HD_A ** -0.5 * LOG2E)).astype(BF16)
    k01, k10 = tiles[4], tiles[5]
    ka_ref[:, 0 * LANE:1 * LANE] = jnp.where(lo, k01, 0.0).astype(BF16)
    ka_ref[:, 1 * LANE:2 * LANE] = jnp.where(lo, 0.0, k10).astype(BF16)
    ka_ref[:, 2 * LANE:3 * LANE] = jnp.where(lo, k10, 0.0).astype(BF16)
    ka_ref[:, 3 * LANE:4 * LANE] = jnp.where(lo, 0.0, k01).astype(BF16)
    ones = jnp.ones((VT_ONES, TMD), BF16)
    vt = _dot_nt(wvat_ref[...], hb)
    for g in range(N_KV_A):
        vat_ref[g * VT_ROWS:g * VT_ROWS + HD_A] = vt[g * HD_A:(g + 1) * HD_A].astype(BF16)
        vat_ref[g * VT_ROWS + HD_A:(g + 1) * VT_ROWS] = ones

    for t in range(3):
        z_ref[:, t * 512:(t + 1) * 512] = mm(P_Z + t * 512, P_Z + (t + 1) * 512).astype(BF16)
    for t in range(6):
        gates_ref[:, t * 512:(t + 1) * 512] = mm(P_GATES + t * 512, P_GATES + (t + 1) * 512).astype(BF16)

    r = mm(P_CQ, P_CQ + Q_RANK_B)
    qn = r * lax.rsqrt(jnp.mean(r * r, axis=-1, keepdims=True) + EPS) * qn_ref[...]
    q = _dot(qn.astype(BF16), wuq_ref[...])
    if rope:
        cb, sb = cb_ref[...], sb_ref[...]
        for h in range(N_HEADS_B):
            seg = (_rope(q[:, h * MLA_HW:(h + 1) * MLA_HW], cb, sb) * (MLA_SCALE * LOG2E)).astype(BF16)
            qb_ref[:, h * MLA_HW:(h + 1) * MLA_HW] = seg
            s32 = seg.astype(F32)
            qnorm_ref[:, h:h + 1] = jnp.sqrt(jnp.sum(s32 * s32, axis=-1, keepdims=True))
    else:
        qb_ref[...] = (q * (MLA_SCALE * LOG2E)).astype(BF16)

    r = mm(P_SMALL, P_SMALL + LANE)
    small_ref[...] = r
    if cache_seq:
        to_cache(ckpe_ref, r[:, S_KPE:S_KPE + QK_ROPE_B])
    kp = _rope(r, cb, sb) if rope else r
    kp = jnp.where((lane >= S_KPE) & (lane < S_KPE + QK_ROPE_B), kp, 0.0)

    r = mm(P_CKV, P_CKV + KV_RANK_B)
    cn = r * lax.rsqrt(jnp.mean(r * r, axis=-1, keepdims=True) + EPS) * kvn_ref[...]
    if cache_seq:
        to_cache(cckv_ref, cn)
    cn16 = cn.astype(BF16)
    kv = _dot(cn16, wukv_ref[...])
    vt = _dot_nt(wvbt_ref[...], cn16)
    for h in range(N_HEADS_B):
        kb_ref[:, h * MLA_HW:(h + 1) * MLA_HW] = (kv[:, h * MLA_HW:(h + 1) * MLA_HW] + kp).astype(BF16)
        vbt_ref[h * VT_ROWS:h * VT_ROWS + V_HD_B] = vt[h * V_HD_B:(h + 1) * V_HD_B].astype(BF16)
        vbt_ref[h * VT_ROWS + V_HD_B:(h + 1) * VT_ROWS] = ones

    for t in range(3):
        cqkv_ref[:, t * 512:(t + 1) * 512] = mm(P_CQKV + t * 512, P_CQKV + (t + 1) * 512).astype(BF16)

    for c in range(TMD // CHUNK):
        abt_ref[c] = _dot_nt(wab_ref[...], hb[c * CHUNK:(c + 1) * CHUNK])


def _layer_spec(arr, l):
    nd = arr.ndim - 1
    return pl.BlockSpec((None,) + arr.shape[1:], lambda *_: (l,) + (0,) * nd, pipeline_mode=pl.Buffered(1))


def _inproj(x2d, l, mod, mod_row_fn, ng, weights, qn, kvn, rope_tabs, tiles_per_seq, cache=None):
    t = x2d.shape[0]
    nt = t // TMD
    rope = rope_tabs is not None
    row = lambda i: (i, 0)
    col = lambda i: (0, i)
    wp, wab, wvat, wuq, wukv, wvbt = weights
    params = (ng, wp, wab, wvat, wuq, wukv, wvbt, qn, kvn)
    in_specs = [pl.BlockSpec((TMD, D_MODEL), row),
                pl.BlockSpec((1, 1, 3 * D_MODEL), lambda i: (mod_row_fn(i), 0, 0))]
    in_specs += [_layer_spec(a, l) for a in params]
    args = [x2d, mod, *params]
    if rope:
        pos = lambda i: (i % tiles_per_seq, 0)
        in_specs += [pl.BlockSpec((TMD, LANE), pos)] * 4
        args += list(rope_tabs)
    outs = [(W_A, BF16, False), (KA_COLS, BF16, False), (VAT_ROWS, BF16, True), (1536, BF16, False),
            (3 * D_MODEL, BF16, False), (HB_COLS, BF16, False), (HB_COLS, BF16, False), (VBT_ROWS, BF16, True),
            (LANE, F32, False), (QKV_C, BF16, False)]
    out_shape = [jax.ShapeDtypeStruct((w, t) if tr else (t, w), dt) for w, dt, tr in outs]
    out_specs = [pl.BlockSpec((w, TMD), col) if tr else pl.BlockSpec((TMD, w), row) for w, _, tr in outs]
    out_shape.append(jax.ShapeDtypeStruct((t // CHUNK, 16, CHUNK), F32))
    out_specs.append(pl.BlockSpec((TMD // CHUNK, 16, CHUNK), lambda i: (i, 0, 0)))
    if rope:
        out_shape.append(jax.ShapeDtypeStruct((t, N_HEADS_B), F32))
        out_specs.append(pl.BlockSpec((TMD, N_HEADS_B), row))
    aliases, cache_seq, n_carry = {}, 0, 0
    if cache is not None:
        depth, cache_seq, carry = cache
        spt = TMD // cache_seq
        for w in (N_KV_A * HD_A, N_KV_A * HD_A, KV_RANK_B, QK_ROPE_B):
            out_shape.append(jax.ShapeDtypeStruct((t // cache_seq, depth, cache_seq, w), F32))
            out_specs.append(pl.BlockSpec((spt, None, cache_seq, w), lambda i: (i, l, 0, 0)))
        n_carry = len(carry)
        for k, a in enumerate(carry):
            aliases[len(args)] = len(out_shape) - n_carry + k
            in_specs.append(pl.BlockSpec(memory_space=pl.ANY))
            args.append(a)
    return pl.pallas_call(
        functools.partial(_inproj_kernel, rope=rope, cache_seq=cache_seq, n_carry=n_carry),
        grid=(nt,),
        in_specs=in_specs,
        out_specs=out_specs,
        out_shape=out_shape,
        input_output_aliases=aliases,
        compiler_params=pltpu.CompilerParams(dimension_semantics=("parallel",)),
        name="inproj_ctx" if cache is not None else "inproj_lat",
    )(*args)


def _kvup_kernel(c_ref, kpe_ref, w_ref, wvt_ref, k_ref, vt_ref):
    c16 = c_ref[...].astype(BF16)
    kv = _dot(c16, w_ref[...])
    vt = _dot_nt(wvt_ref[...], c16)
    kp = kpe_ref[...]
    ones = jnp.ones((VT_ONES, c16.shape[0]), BF16)
    for h in range(N_HEADS_B):
        k_ref[:, h * MLA_HW:(h + 1) * MLA_HW] = (kv[:, h * MLA_HW:(h + 1) * MLA_HW] + kp).astype(BF16)
        vt_ref[h * VT_ROWS:h * VT_ROWS + V_HD_B] = vt[h * V_HD_B:(h + 1) * V_HD_B].astype(BF16)
        vt_ref[h * VT_ROWS + V_HD_B:(h + 1) * VT_ROWS] = ones


def _kvup(ckv, kpe, wukv, wvbt, l):
    nb, _, past, _ = ckv.shape
    return pl.pallas_call(
        _kvup_kernel,
        grid=(nb,),
        in_specs=[pl.BlockSpec((None, None, past, KV_RANK_B), lambda b: (b, l, 0, 0)),
                  pl.BlockSpec((None, None, past, LANE), lambda b: (b, l, 0, 0)),
                  _layer_spec(wukv, l), _layer_spec(wvbt, l)],
        out_specs=[pl.BlockSpec((past, HB_COLS), lambda b: (b, 0)),
                   pl.BlockSpec((None, VBT_ROWS, past), lambda b: (b, 0, 0))],
        out_shape=[jax.ShapeDtypeStruct((nb * past, HB_COLS), BF16),
                   jax.ShapeDtypeStruct((nb, VBT_ROWS, past), BF16)],
        name="mla_cache_up",
    )(ckv, kpe, wukv, wvbt)


def _scores_t(q_tiles, k_tiles, bias_t):
    st = _bdot_nt(jnp.stack(k_tiles), jnp.stack(q_tiles))
    return st if bias_t is None else st + bias_t[None]


def _softmax_pv(sts, vts, sink, shift=None):
    if shift is None:
        m = jnp.max(sts[0], axis=1, keepdims=True)
        for st in sts[1:]:
            m = jnp.maximum(m, jnp.max(st, axis=1, keepdims=True))
    else:
        m = shift
    if sink is not None:
        m = jnp.maximum(m, sink)
    ot = _bdot(jnp.stack(vts[0]), jnp.exp2(sts[0] - m).astype(BF16))
    for st, vt in zip(sts[1:], vts[1:]):
        ot = ot + _bdot(jnp.stack(vt), jnp.exp2(st - m).astype(BF16))
    den = ot[:, V_HD_B:V_HD_B + 1, :]
    if sink is not None:
        den = den + jnp.exp2(sink - m)
    num = ot[:, :V_HD_B, :] / den
    return [jnp.concatenate([num[2 * i], num[2 * i + 1]], axis=0).T for i in range(sts[0].shape[0] // 2)]


def _tile(x, t):
    return x[:, t * LANE:(t + 1) * LANE]


def _attn_a_heads(q, segments, sink_ref):
    sts, vts = [], []
    for ka, vat, bias_t in segments:
        qs, ks, vs = [], [], []
        for t in range(N_HEADS_A // 2):
            g = (2 * t) // GQA_GROUP
            for e in range(2):
                qs.append(_tile(q, t))
                ks.append(_tile(ka, 2 * g + e))
                vs.append(vat[g * VT_ROWS:(g + 1) * VT_ROWS])
        sts.append(_scores_t(qs, ks, bias_t))
        vts.append(vs)
    sink = jnp.stack([sink_ref[:, h:h + 1] * LOG2E for h in range(N_HEADS_A)])
    return _softmax_pv(sts, vts, sink)


def _gated_store(outs, z_ref, o_ref, first_tile=0):
    for i, o in enumerate(outs):
        t = first_tile + i
        z = _tile(z_ref, t).astype(F32)
        o_ref[:, t * LANE:(t + 1) * LANE] = (o * _silu(z)).astype(BF16)


def _attn_ctx_kernel(qa_ref, ka_ref, vat_ref, qb_ref, kb_ref, vbt_ref, za_ref, zb_ref, sink_ref, oa_ref, ob_ref):
    _gated_store(_attn_a_heads(qa_ref[...], [(ka_ref[...], vat_ref[...], None)], sink_ref), za_ref, oa_ref)
    q, kb, vbt = qb_ref[...], kb_ref[...], vbt_ref[...]
    heads = range(N_HEADS_B)
    st = _scores_t([_tile(q, h) for h in heads], [_tile(kb, h) for h in heads], None)
    _gated_store(_softmax_pv([st], [[vbt[h * VT_ROWS:(h + 1) * VT_ROWS] for h in heads]], None), zb_ref, ob_ref)


def _attn_ctx(qa, ka, vat, qb, kb, vbt, z, sink, l, seq):
    t = qa.shape[0]
    row = lambda b: (b, 0)
    col = lambda b: (0, b)
    return pl.pallas_call(
        _attn_ctx_kernel,
        grid=(t // seq,),
        in_specs=[pl.BlockSpec((seq, W_A), row),
                  pl.BlockSpec((seq, KA_COLS), row),
                  pl.BlockSpec((VAT_ROWS, seq), col),
                  pl.BlockSpec((seq, HB_COLS), row),
                  pl.BlockSpec((seq, HB_COLS), row),
                  pl.BlockSpec((VBT_ROWS, seq), col),
                  pl.BlockSpec((seq, W_A), row),
                  pl.BlockSpec((seq, W_B), lambda b: (b, 1)),
                  _layer_spec(sink, l)],
        out_specs=[pl.BlockSpec((seq, W_A), row), pl.BlockSpec((seq, W_B), row)],
        out_shape=[jax.ShapeDtypeStruct((t, W_A), BF16), jax.ShapeDtypeStruct((t, W_B), BF16)],
        compiler_params=pltpu.CompilerParams(dimension_semantics=("parallel",)),
        name="attn_ctx",
    )(qa, ka, vat, qb, kb, vbt, z, z, sink)


def _attn_a_lat_kernel(q_ref, kp_ref, kc_ref, kn_ref, vp_ref, vc_ref, vn_ref, kx_ref, vx_ref, z_ref, sink_ref,
                       o_ref, *, nq):
    j = pl.program_id(1)
    ka = jnp.concatenate([kp_ref[...], kc_ref[...], kn_ref[...]], axis=0)
    vat = jnp.concatenate([vp_ref[...], vc_ref[...], vn_ref[...]], axis=1)
    kj = lax.broadcasted_iota(jnp.int32, (ka.shape[0], TM), 0)
    qi = lax.broadcasted_iota(jnp.int32, (ka.shape[0], TM), 1)
    ok = (kj >= qi) & (kj <= qi + 2 * WINDOW)
    ok = ok & ((kj >= WINDOW) | (j > 0)) & ((kj < TM + WINDOW) | (j < nq - 1))
    bias_t = jnp.where(ok, 0.0, NEG_INF)
    segments = [(ka, vat, bias_t), (kx_ref[...], vx_ref[...], None)]
    _gated_store(_attn_a_heads(q_ref[...], segments, sink_ref), z_ref, o_ref)


def _attn_a_lat(qa, ka, vat, z, sink, kx, vxt, l, seq):
    t = qa.shape[0]
    nq = seq // TM
    past = kx.shape[2]
    r = TM // WINDOW
    row = lambda b, j: (b * nq + j, 0)
    prev = lambda b, j: ((b * nq + j) * r - jnp.where(j > 0, 1, 0), 0)
    nxt = lambda b, j: ((b * nq + j) * r + jnp.where(j < nq - 1, r, r - 1), 0)
    swap = lambda f: (lambda b, j: f(b, j)[::-1])
    return pl.pallas_call(
        functools.partial(_attn_a_lat_kernel, nq=nq),
        grid=(t // seq, nq),
        in_specs=[pl.BlockSpec((TM, W_A), row),
                  pl.BlockSpec((WINDOW, KA_COLS), prev),
                  pl.BlockSpec((TM, KA_COLS), row),
                  pl.BlockSpec((WINDOW, KA_COLS), nxt),
                  pl.BlockSpec((VAT_ROWS, WINDOW), swap(prev)),
                  pl.BlockSpec((VAT_ROWS, TM), swap(row)),
                  pl.BlockSpec((VAT_ROWS, WINDOW), swap(nxt)),
                  pl.BlockSpec((None, None, past, KA_COLS), lambda b, j: (b, l, 0, 0)),
                  pl.BlockSpec((None, None, VAT_ROWS, past), lambda b, j: (b, l, 0, 0)),
                  pl.BlockSpec((TM, W_A), row),
                  _layer_spec(sink, l)],
        out_specs=pl.BlockSpec((TM, W_A), row),
        out_shape=jax.ShapeDtypeStruct((t, W_A), BF16),
        compiler_params=pltpu.CompilerParams(dimension_semantics=("parallel", "parallel")),
        name="attn_a_lat",
    )(qa, ka, ka, ka, vat, vat, vat, kx, vxt, z, sink)


def _attn_b_lat_kernel(q_ref, qn_ref, k_ref, vt_ref, kx_ref, vxt_ref, z_ref, o_ref, knorm_scr, *, group):
    segments = [(k_ref, vt_ref), (kx_ref, vxt_ref)]
    q = q_ref[...]

    @pl.when(pl.program_id(1) == 0)
    def _():
        for h in range(N_HEADS_B):
            ksq = [jnp.max(jnp.sum(jnp.square(_tile(kr, h).astype(F32)), axis=-1, keepdims=True), axis=0,
                           keepdims=True) for kr, _ in segments]
            knorm_scr[h:h + 1, :] = jnp.broadcast_to(jnp.sqrt(jnp.maximum(ksq[0], ksq[1])), (1, LANE))

    qmax = jnp.max(qn_ref[...], axis=0, keepdims=True)
    bound = jnp.stack([qmax[:, h:h + 1] * knorm_scr[h:h + 1, 0:1] * SHIFT_SLACK + SHIFT_SLACK_ABS
                       for h in range(N_HEADS_B)])
    bound_ok = jnp.max(bound) <= SHIFT_MAX

    def scores(h0):
        heads = range(h0, h0 + group)
        return [_scores_t([_tile(q, h) for h in heads], [_tile(kr, h) for h in heads], None) for kr, _ in segments]

    def attend(shift):
        sts = scores(0)
        for h0 in range(0, N_HEADS_B, group):
            sts_next = scores(h0 + group) if h0 + group < N_HEADS_B else None
            vts = [[vr[h * VT_ROWS:(h + 1) * VT_ROWS, :] for h in range(h0, h0 + group)] for _, vr in segments]
            outs = _softmax_pv(sts, vts, None, None if shift is None else shift[h0:h0 + group])
            _gated_store(outs, z_ref, o_ref, h0 // 2)
            sts = sts_next

    @pl.when(bound_ok)
    def _():
        attend(bound)

    @pl.when(jnp.logical_not(bound_ok))
    def _():
        attend(None)


def _attn_b_lat(qb, qnorm, kb, vbt, z, kx, vxt, seq, qblk, group):
    t = qb.shape[0]
    nq = seq // qblk
    past = kx.shape[0] // (t // seq)
    hw = HB_COLS
    return pl.pallas_call(
        functools.partial(_attn_b_lat_kernel, group=group),
        grid=(t // seq, nq),
        in_specs=[pl.BlockSpec((qblk, hw), lambda b, j: (b * nq + j, 0)),
                  pl.BlockSpec((qblk, N_HEADS_B), lambda b, j: (b * nq + j, 0)),
                  pl.BlockSpec((seq, hw), lambda b, j: (b, 0)),
                  pl.BlockSpec((VBT_ROWS, seq), lambda b, j: (0, b)),
                  pl.BlockSpec((past, hw), lambda b, j: (b, 0)),
                  pl.BlockSpec((None, VBT_ROWS, past), lambda b, j: (b, 0, 0)),
                  pl.BlockSpec((qblk, W_B), lambda b, j: (b * nq + j, 1))],
        out_specs=pl.BlockSpec((qblk, W_B), lambda b, j: (b * nq + j, 0)),
        out_shape=jax.ShapeDtypeStruct((t, W_B), BF16),
        scratch_shapes=[pltpu.VMEM((N_HEADS_B, LANE), F32)],
        compiler_params=pltpu.CompilerParams(dimension_semantics=("parallel", "arbitrary")),
        name="attn_b_lat",
    )(qb, qnorm, kb, vbt, kx, vxt, z)


def _gdn_local_kernel(cq_ref, prev_ref, next_ref, small_ref, abt_ref, cw_ref, prow_ref, pcol_ref,
                      u_ref, w_ref, qg_ref, kd_ref, attn_ref, eg_ref, qkv_scr, gb_scr, *, seq):
    x = cq_ref[...].astype(F32)
    tiles_per_seq = seq // TMG
    tpos = pl.program_id(0) % tiles_per_seq
    prev_row = jnp.where(tpos > 0, prev_ref[...].astype(F32)[HALO - 1:HALO, :], 0.0)
    next_row = jnp.where(tpos < tiles_per_seq - 1, next_ref[...].astype(F32)[0:1, :], 0.0)
    rows = lax.broadcasted_iota(jnp.int32, (TMG, 1), 0)
    xm1 = jnp.where(rows == 0, prev_row, pltpu.roll(x, 1, 0))
    xp1 = jnp.where(rows == TMG - 1, next_row, pltpu.roll(x, TMG - 1, 0))
    cw = cw_ref[...]
    y = _silu(xm1 * cw[0:1] + x * cw[1:2] + xp1 * cw[2:3])
    nq = N_HEADS_C * DK_C
    for h in range(N_HEADS_C):
        qh = y[:, h * DK_C:(h + 1) * DK_C]
        kh = y[:, nq + h * DK_C:nq + (h + 1) * DK_C]
        qkv_scr[:, h * DK_C:(h + 1) * DK_C] = (
            qh * lax.rsqrt(jnp.sum(qh * qh, axis=-1, keepdims=True) + EPS) * (DK_C ** -0.5))
        qkv_scr[:, nq + h * DK_C:nq + (h + 1) * DK_C] = kh * lax.rsqrt(jnp.sum(kh * kh, axis=-1, keepdims=True) + EPS)
    qkv_scr[:, 2 * nq:] = y[:, 2 * nq:]

    sm = small_ref[...]
    prow = prow_ref[...]
    gb_scr[:, 0:8] = -jnp.exp(prow[0:1]) * _softplus(sm[:, S_A:S_A + 8] + prow[1:2])
    gb_scr[:, 8:16] = _sigmoid(sm[:, S_B:S_B + 8])
    pcol = pcol_ref[...]

    ri = lax.broadcasted_iota(jnp.int32, (CHUNK, LANE), 0)
    lane = lax.broadcasted_iota(jnp.int32, (CHUNK, LANE), 1)
    fwd = lane < CHUNK
    cj = lane & (CHUNK - 1)
    incl = (fwd & (ri >= cj)) | (~fwd & (ri <= cj))
    strict = (fwd & (ri > cj)) | (~fwd & (ri < cj))
    xor = ri ^ cj
    eye = (ri == cj).astype(F32)
    r2 = lax.broadcasted_iota(jnp.int32, (2 * CHUNK, LANE), 0)
    l2 = lax.broadcasted_iota(jnp.int32, (2 * CHUNK, LANE), 1)
    same_dir = (r2 < CHUNK) == (l2 < CHUNK)
    rs_ = lax.broadcasted_iota(jnp.int32, (CHUNK, CHUNK), 0)
    cs_ = lax.broadcasted_iota(jnp.int32, (CHUNK, CHUNK), 1)
    tril = (rs_ >= cs_).astype(F32)
    triu = (rs_ <= cs_).astype(F32)
    tri_rows = jnp.concatenate([triu, tril], axis=1)
    dup_rows = jnp.concatenate([(rs_ == cs_).astype(F32)] * 2, axis=1)

    def block_diag(x):
        return jnp.where(same_dir[None], jnp.concatenate([x, x], axis=1), 0.0).astype(BF16)

    lows, rhss, order = [], [], []

    def solve():
        low = jnp.stack(lows, axis=0)
        inv = eye[None] - jnp.where(xor[None] == 1, low, 0.0)
        b = 2
        while b < CHUNK:
            cpl = jnp.where((xor[None] >= b) & (xor[None] < 2 * b), low, 0.0)
            tmp = _bdot(cpl.astype(BF16), block_diag(inv))
            inv = inv - _bdot(inv.astype(BF16), block_diag(tmp))
            b *= 2
        scale_u, scale_w, vs, ks = (jnp.stack(a, axis=0) for a in zip(*rhss))
        u = _bdot((inv * scale_u).astype(BF16), vs)
        w = _bdot((inv * scale_w).astype(BF16), ks)
        for i, (rs, h) in enumerate(order):
            for d in range(2):
                cs = slice((d * N_HEADS_C + h) * DK_C, (d * N_HEADS_C + h + 1) * DK_C)
                u_ref[rs, cs] = u[i, :, d * DV_C:(d + 1) * DV_C].astype(BF16)
                w_ref[rs, cs] = w[i, :, d * DK_C:(d + 1) * DK_C].astype(BF16)
        lows.clear(), rhss.clear(), order.clear()

    for c in range(TMG // CHUNK):
        if c % SOLVE_CHUNKS == 0 and c > 0:
            solve()
        rs = slice(c * CHUNK, (c + 1) * CHUNK)
        gcol = gb_scr[rs, 0:8]
        bcol = gb_scr[rs, 8:16]
        abt = abt_ref[c]
        grow = -jnp.exp(pcol[:, 0:1]) * _softplus(abt[0:8] + pcol[:, 1:2])
        gc_f = _dot_exact(tril, gcol)
        gc_b = _dot_exact(triu, gcol)
        gr = _dot_exact(grow, tri_rows)
        br = _dot_exact(_sigmoid(abt[8:16]), dup_rows)
        for h in range(N_HEADS_C):
            hb_ = N_HEADS_C + h
            q = qkv_scr[rs, h * DK_C:(h + 1) * DK_C]
            k = qkv_scr[rs, nq + h * DK_C:nq + (h + 1) * DK_C]
            v = qkv_scr[rs, 2 * nq + h * DV_C:2 * nq + (h + 1) * DV_C]
            k16 = k.astype(BF16)
            kk16 = jnp.concatenate([k16, k16], axis=0)
            kk = _dot_nt(k16, kk16)
            qk = _dot_nt(q.astype(BF16), kk16)
            gcs = (gc_f[:, h:h + 1], gc_b[:, hb_:hb_ + 1])
            betas = (bcol[:, h:h + 1], bcol[:, hb_:hb_ + 1])
            gc2 = jnp.where(fwd, gcs[0], gcs[1])
            gr2 = jnp.where(fwd[0:1], gr[h:h + 1, :], gr[hb_:hb_ + 1, :])
            decay = jnp.where(incl, jnp.exp(jnp.where(incl, gc2 - gr2, 0.0)), 0.0)
            lows.append(jnp.where(strict, jnp.where(fwd, betas[0], betas[1]) * kk * decay, 0.0))
            attn_ref[rs, h * LANE:(h + 1) * LANE] = (qk * decay).astype(BF16)
            order.append((rs, h))
            br2 = jnp.where(fwd[0:1], br[h:h + 1, :], br[hb_:hb_ + 1, :])
            v16 = v.astype(BF16)
            zeros = jnp.zeros_like(v16)
            rhss.append((br2, br2 * jnp.exp(gr2),
                         jnp.concatenate([jnp.concatenate([v16, zeros], axis=1),
                                          jnp.concatenate([zeros, v16], axis=1)], axis=0),
                         jnp.concatenate([jnp.concatenate([k16, zeros], axis=1),
                                          jnp.concatenate([zeros, k16], axis=1)], axis=0)))
            for d in range(2):
                dh = d * N_HEADS_C + h
                gc = gcs[d]
                eg = jnp.exp(gc)
                g_last = gc[CHUNK - 1:CHUNK] if d == 0 else gc[0:1]
                cs = slice(dh * DK_C, (dh + 1) * DK_C)
                qg_ref[rs, cs] = (q * eg).astype(BF16)
                kd_ref[rs, cs] = (k * jnp.exp(g_last - gc)).astype(BF16)
                eg_ref[c, dh:dh + 1, :] = jnp.broadcast_to(jnp.exp(g_last), (1, LANE))
    solve()


def _gdn_local(cqkv, small, abt, conv_w, prow, pcol, l, seq):
    t = cqkv.shape[0]
    assert seq % TMG == 0
    nt = t // TMG
    nh8 = t // HALO
    cpg = TMG // CHUNK
    row = lambda i: (i, 0)
    dh = 2 * N_HEADS_C
    return pl.pallas_call(
        functools.partial(_gdn_local_kernel, seq=seq),
        grid=(nt,),
        in_specs=[pl.BlockSpec((TMG, QKV_C), row),
                  pl.BlockSpec((HALO, QKV_C), lambda i: (jnp.maximum(i * (TMG // HALO) - 1, 0), 0)),
                  pl.BlockSpec((HALO, QKV_C), lambda i: (jnp.minimum((i + 1) * (TMG // HALO), nh8 - 1), 0)),
                  pl.BlockSpec((TMG, LANE), row),
                  pl.BlockSpec((cpg, 16, CHUNK), lambda i: (i, 0, 0)),
                  _layer_spec(conv_w, l), _layer_spec(prow, l), _layer_spec(pcol, l)],
        out_specs=[pl.BlockSpec((TMG, dh * DV_C), row),
                   pl.BlockSpec((TMG, dh * DK_C), row),
                   pl.BlockSpec((TMG, dh * DK_C), row),
                   pl.BlockSpec((TMG, dh * DK_C), row),
                   pl.BlockSpec((TMG, dh * CHUNK), row),
                   pl.BlockSpec((cpg, dh, LANE), lambda i: (i, 0, 0))],
        out_shape=[jax.ShapeDtypeStruct((t, dh * DV_C), BF16),
                   jax.ShapeDtypeStruct((t, dh * DK_C), BF16),
                   jax.ShapeDtypeStruct((t, dh * DK_C), BF16),
                   jax.ShapeDtypeStruct((t, dh * DK_C), BF16),
                   jax.ShapeDtypeStruct((t, dh * CHUNK), BF16),
                   jax.ShapeDtypeStruct((t // CHUNK, dh, LANE), F32)],
        scratch_shapes=[pltpu.VMEM((TMG, QKV_C), F32), pltpu.VMEM((TMG, 16), F32)],
        compiler_params=pltpu.CompilerParams(dimension_semantics=("parallel",)),
        name="gdn_local",
    )(cqkv, cqkv, cqkv, small, abt, conv_w, prow, pcol)


def _gdn_scan_kernel(*refs, nt, ns, has_init, want_state, n_carry):
    it = iter(refs)
    ins = [[next(it) for _ in range(6)] for _ in range(2)]
    s0_ref = next(it) if has_init else None
    for _ in range(n_carry):
        next(it)
    o_refs = [next(it), next(it)]
    st_ref = next(it) if want_state else None
    s_scr = next(it)
    j = pl.program_id(1)
    nh = N_HEADS_C
    nst = ns * 2 * nh

    @pl.when(j == 0)
    def _():
        if has_init:
            s_scr[...] = s0_ref[...].reshape(nst, DK_C, DV_C)
        else:
            s_scr[...] = jnp.zeros_like(s_scr)

    for step in range(CPT):
        chunk = lambda d: step if d == 0 else CPT - 1 - step

        def gather(idx, width):
            return jnp.stack([ins[d][idx][s, chunk(d) * CHUNK:(chunk(d) + 1) * CHUNK, h * width:(h + 1) * width]
                              for s in range(ns) for d in range(2) for h in range(nh)])

        u, w, qg, kd, attn = gather(0, DV_C), gather(1, DK_C), gather(2, DK_C), gather(3, DK_C), gather(4, LANE)
        eg = jnp.stack([ins[d][5][s, chunk(d), d * nh + h:d * nh + h + 1, :]
                        for s in range(ns) for d in range(2) for h in range(nh)])
        st = s_scr[...]
        sb = st.astype(BF16)
        v_new = u.astype(F32) - _bdot(w, sb)
        vb = v_new.astype(BF16)
        zeros = jnp.zeros((CHUNK, DV_C), BF16)
        vb2 = jnp.stack([jnp.concatenate([vb[i], zeros] if (i // nh) % 2 == 0 else [zeros, vb[i]], axis=0)
                         for i in range(nst)])
        o = _bdot(qg, sb) + _bdot(attn, vb2)
        s_scr[...] = st * eg + _bdot_tn(kd, vb)
        for s in range(ns):
            for d in range(2):
                for h in range(nh):
                    o_refs[d][s, chunk(d) * CHUNK:(chunk(d) + 1) * CHUNK, h * DV_C:(h + 1) * DV_C] = (
                        o[(s * 2 + d) * nh + h].astype(BF16))

    if want_state:
        @pl.when(j == nt - 1)
        def _():
            st_ref[...] = s_scr[...].reshape(ns, 2, nh, DK_C, DV_C)


def _gdn_scan(u, w, qg, kd, attn, eg, s0, l, seq, state_out=None):
    t = u.shape[0]
    want_state = state_out is not None
    nt = seq // TM
    nb = t // seq
    ns = next(n for n in (SCAN_SEQS, 2, 1) if nb % n == 0)
    half = N_HEADS_C * DK_C
    has_init = s0 is not None
    by_seq = lambda a: a.reshape((nb, a.shape[0] // nb) + a.shape[1:])
    in_specs, args = [], []
    for d in range(2):
        tile = (lambda b, j: j) if d == 0 else (lambda b, j: nt - 1 - j)
        row = lambda b, j, d=d, tile=tile: (b, tile(b, j), d)
        row4 = lambda b, j, tile=tile: (b, tile(b, j), 0, 0)
        in_specs += [pl.BlockSpec((ns, TM, half), row)] * 4
        in_specs += [pl.BlockSpec((ns, TM, N_HEADS_C * LANE), lambda b, j, tile=tile: (b, tile(b, j), 0)),
                     pl.BlockSpec((ns, CPT, 2 * N_HEADS_C, LANE), row4)]
        args += [by_seq(a) for a in (u, w, qg, kd, attn, eg)]
    st_tail = (2, N_HEADS_C, DK_C, DV_C)
    if has_init:
        in_specs.append(pl.BlockSpec((ns, None) + st_tail, lambda b, j: (b, l, 0, 0, 0, 0)))
        args.append(s0)
    out_specs = [pl.BlockSpec((ns, TM, half), lambda b, j: (b, j, 0)),
                 pl.BlockSpec((ns, TM, half), lambda b, j: (b, nt - 1 - j, 0))]
    out_shape = [jax.ShapeDtypeStruct((nb, seq, half), BF16)] * 2
    aliases, n_carry = {}, 0
    if want_state:
        depth, carry = state_out
        out_specs.append(pl.BlockSpec((ns, None) + st_tail, lambda b, j: (b, l, 0, 0, 0, 0)))
        out_shape.append(jax.ShapeDtypeStruct((nb, depth) + st_tail, F32))
        n_carry = 1
        aliases[len(args)] = 2
        in_specs.append(pl.BlockSpec(memory_space=pl.ANY))
        args.append(carry)
    outs = pl.pallas_call(
        functools.partial(_gdn_scan_kernel, nt=nt, ns=ns, has_init=has_init, want_state=want_state,
                          n_carry=n_carry),
        grid=(nb // ns, nt),
        in_specs=in_specs,
        out_specs=out_specs,
        out_shape=out_shape,
        input_output_aliases=aliases,
        scratch_shapes=[pltpu.VMEM((ns * 2 * N_HEADS_C, DK_C, DV_C), F32)],
        compiler_params=pltpu.CompilerParams(dimension_semantics=("parallel", "arbitrary")),
        name="gdn_scan",
    )(*args)
    return [outs[0].reshape(t, half), outs[1].reshape(t, half)] + list(outs[2:])


def _merge_kernel(x_ref, mod_ref, oa_ref, ob_ref, cf_ref, cb_ref, zc_ref, gates_ref, gn_ref,
                  wa_ref, wb_ref, wc_ref, wo_ref, fg_ref, o_ref, *, last):
    oc = cf_ref[...].astype(F32) + cb_ref[...].astype(F32)
    zc = zc_ref[...].astype(F32)
    gn = gn_ref[...]
    parts = []
    for h in range(N_HEADS_C):
        hs = slice(h * DV_C, (h + 1) * DV_C)
        och = oc[:, hs]
        och = och * lax.rsqrt(jnp.mean(och * och, axis=-1, keepdims=True) + EPS) * gn
        parts.append((och * _silu(zc[:, hs])).astype(BF16))
    ocz = jnp.concatenate(parts, axis=-1)
    pa = _dot(oa_ref[...], wa_ref[...])
    pb = _dot(ob_ref[...], wb_ref[...])
    pc = _dot(ocz, wc_ref[...])
    ga = _sigmoid(gates_ref[:, 0:D_MODEL].astype(F32))
    gb = _sigmoid(gates_ref[:, D_MODEL:2 * D_MODEL].astype(F32))
    gc = _sigmoid(gates_ref[:, 2 * D_MODEL:].astype(F32))
    y = _dot((ga * pa + gb * pb + gc * pc).astype(BF16), wo_ref[...])
    gate = mod_ref[0][:, 2 * D_MODEL:]
    xo = x_ref[...] + gate * y
    if last:
        xo = xo * lax.rsqrt(jnp.mean(xo * xo, axis=-1, keepdims=True) + EPS) * fg_ref[...]
    o_ref[...] = xo


def _merge(x2d, l, mod, mod_row_fn, oa, ob, cf, cb, z, gates, gn, wa, wb, wc, wo, fg, last):
    t = x2d.shape[0]
    row = lambda i: (i, 0)
    return pl.pallas_call(
        functools.partial(_merge_kernel, last=last),
        grid=(t // TMD,),
        in_specs=[pl.BlockSpec((TMD, D_MODEL), row),
                  pl.BlockSpec((1, 1, 3 * D_MODEL), lambda i: (mod_row_fn(i), 0, 0)),
                  pl.BlockSpec((TMD, W_A), row),
                  pl.BlockSpec((TMD, W_B), row),
                  pl.BlockSpec((TMD, W_C), row),
                  pl.BlockSpec((TMD, W_C), row),
                  pl.BlockSpec((TMD, W_C), lambda i: (i, 2)),
                  pl.BlockSpec((TMD, 3 * D_MODEL), row),
                  _layer_spec(gn, l), _layer_spec(wa, l), _layer_spec(wb, l), _layer_spec(wc, l),
                  _layer_spec(wo, l),
                  pl.BlockSpec((1, D_MODEL), lambda i: (0, 0))],
        out_specs=pl.BlockSpec((TMD, D_MODEL), row),
        out_shape=jax.ShapeDtypeStruct((t, D_MODEL), F32),
        compiler_params=pltpu.CompilerParams(dimension_semantics=("parallel",)),
        name="merge",
    )(x2d, mod, oa, ob, cf, cb, z, gates, gn, wa, wb, wc, wo, fg)


def _rope_tables(n_tokens, rot_dim):
    rows = n_tokens // GRID_W
    row = np.repeat(np.arange(rows), GRID_W).astype(np.float32)
    col = np.tile(np.arange(GRID_W), rows).astype(np.float32)
    n_pairs = rot_dim // 4
    inv = (np.float32(ROPE_BASE) ** (-np.arange(n_pairs, dtype=np.float32) / np.float32(n_pairs))).astype(np.float32)
    ang = np.concatenate([row[:, None] * inv, col[:, None] * inv], axis=-1)
    c, s = np.cos(ang), np.sin(ang)
    return np.repeat(c, 2, axis=-1), np.stack([-s, s], axis=-1).reshape(n_tokens, rot_dim)


def _in_offsets():
    o = [0]
    for n in IN_SIZES:
        o.append(o[-1] + n)
    return o


def _relayout_moves():
    o = _in_offsets()
    order = [(o[0], W_A), (o[1], LANE), (o[1] + HD_A, HD_A), (o[1], HD_A), (o[2], LANE),
             (o[3], W_A), (o[7], W_B), (o[11], W_C), (o[12], 3 * D_MODEL), (o[4], Q_RANK_B), (o[5], KV_RANK_B),
             (None, S_KPE), (o[6], QK_ROPE_B), (o[9], 4 * N_HEADS_C), (None, LANE - S_B - 8), (o[8], QKV_C)]
    moves, dst = [], 0
    for src, n in order:
        moves.append((src, dst, n))
        dst += n
    assert dst == P_END
    return moves


def _relayout_kernel(w_ref, o_ref, wab_ref, wvat_ref):
    for src, dst, n in _relayout_moves():
        if src is None:
            o_ref[dst:dst + n, :] = jnp.zeros((n, o_ref.shape[1]), BF16)
        else:
            o_ref[dst:dst + n, :] = w_ref[src:src + n, :].astype(BF16)
    o = _in_offsets()
    wab_ref[...] = w_ref[o[9]:o[11], :].astype(BF16)
    wvat_ref[...] = w_ref[o[2]:o[3], :].astype(BF16)


def _relayout_w_in(w_in):
    w_t = jnp.swapaxes(w_in, 1, 2)
    depth, width, _ = w_t.shape
    cols = 128
    nab, nv = 4 * N_HEADS_C, N_KV_A * HD_A
    col = lambda l, i: (l, 0, i)
    return pl.pallas_call(
        _relayout_kernel,
        grid=(depth, D_MODEL // cols),
        in_specs=[pl.BlockSpec((None, width, cols), col)],
        out_specs=[pl.BlockSpec((None, P_END, cols), col),
                   pl.BlockSpec((None, nab, cols), col),
                   pl.BlockSpec((None, nv, cols), col)],
        out_shape=[jax.ShapeDtypeStruct((depth, P_END, D_MODEL), BF16),
                   jax.ShapeDtypeStruct((depth, nab, D_MODEL), BF16),
                   jax.ShapeDtypeStruct((depth, nv, D_MODEL), BF16)],
        name="w_in_relayout",
    )(w_t)


def _prep_weights(w_in, w_uq, w_ukv):
    depth = w_in.shape[0]
    wp, wab, wvat = _relayout_w_in(w_in)
    hd = QK_NOPE_B + QK_ROPE_B
    wuq = jnp.pad(w_uq.reshape(depth, Q_RANK_B, N_HEADS_B, hd), ((0, 0), (0, 0), (0, 0), (0, MLA_HW - hd)))
    wuq = wuq.reshape(depth, Q_RANK_B, HB_COLS).astype(BF16)
    kv = w_ukv.reshape(depth, KV_RANK_B, N_HEADS_B, QK_NOPE_B + V_HD_B)
    wk = jnp.pad(kv[..., :QK_NOPE_B], ((0, 0), (0, 0), (0, 0), (0, MLA_HW - QK_NOPE_B)))
    wukv = wk.reshape(depth, KV_RANK_B, HB_COLS).astype(BF16)
    wvbt = jnp.swapaxes(kv[..., QK_NOPE_B:].reshape(depth, KV_RANK_B, W_B), 1, 2).astype(BF16)
    return wp, wab, wvat, wuq, wukv, wvbt


def _cache_tiles_a(kx, vx):
    k0, k1 = kx[..., 0, :], kx[..., 1, :]
    z = jnp.zeros_like(k0)
    ka = jnp.concatenate([k0, z, z, k0, k1, z, z, k1], axis=-1).astype(BF16)
    vt = jnp.transpose(vx, (0, 1, 3, 4, 2))
    vt = jnp.concatenate([vt, jnp.ones(vt.shape[:3] + (VT_ONES, vt.shape[4]), vt.dtype)], axis=3)
    return ka, vt.reshape(vt.shape[:2] + (VAT_ROWS, vt.shape[4])).astype(BF16)


def kernel(x_prompt, x_sample, cache_attn_k, cache_attn_v, cache_mla_ckv, cache_mla_kpe, state_gdn, c, c_ctx,
           norm_g, w_ada, b_ada, w_in, attn_sink, mla_q_norm, mla_w_uq, mla_kv_norm, mla_w_ukv, gdn_conv,
           gdn_a_log, gdn_dt_bias, gdn_norm, w_branch_a, w_branch_b, w_branch_c, w_out, final_norm_g):
    depth = w_in.shape[0]
    nb_c, seq_c, _ = x_prompt.shape
    nb_l, seq_l, _ = x_sample.shape
    past = cache_attn_k.shape[2]
    assert P_END % LANE == 0 and seq_c % TM == 0 and seq_l % TMD == 0 and nb_l < 8 and TM == 2 * WINDOW
    assert (nb_c * seq_c) % TMD == 0 and TMD % TM == 0

    cond8 = jnp.zeros((8, D_MODEL), F32).at[:nb_l].set(c).at[nb_l].set(c_ctx)
    mod = _modulation(cond8, w_ada, b_ada).reshape(depth * 8, 1, 3 * D_MODEL)

    c_a, s_a = _rope_tables(seq_l, HD_A)
    c_b, s_b = _rope_tables(seq_l, QK_ROPE_B)
    pad_l, pad_r = S_KPE, LANE - S_KPE - QK_ROPE_B
    one, zero = np.ones((seq_l, 1), np.float32), np.zeros((seq_l, 1), np.float32)
    rope_tabs = tuple(jnp.asarray(a) for a in (
        np.tile(c_a, (1, LANE // HD_A)), np.tile(s_a, (1, LANE // HD_A)),
        np.concatenate([np.tile(one, (1, pad_l)), c_b, np.tile(one, (1, pad_r))], 1),
        np.concatenate([np.tile(zero, (1, pad_l)), s_b, np.tile(zero, (1, pad_r))], 1)))

    weights = _prep_weights(w_in, mla_w_uq, mla_w_ukv)
    wukv, wvbt = weights[4], weights[5]
    ng = norm_g.reshape(depth, 1, D_MODEL)
    qn = mla_q_norm.reshape(depth, 1, Q_RANK_B)
    kvn = mla_kv_norm.reshape(depth, 1, KV_RANK_B)
    sink = attn_sink.reshape(depth, 1, N_HEADS_A)
    prow = jnp.stack([gdn_a_log.reshape(depth, -1), gdn_dt_bias.reshape(depth, -1)], axis=1)
    pcol = jnp.swapaxes(prow, 1, 2)
    gn = gdn_norm.reshape(depth, 1, DV_C)
    wa, wb, wc, wo = (w.astype(BF16) for w in (w_branch_a, w_branch_b, w_branch_c, w_out))
    fg = final_norm_g.reshape(1, D_MODEL)
    kxa, vxa = _cache_tiles_a(cache_attn_k, cache_attn_v)
    kpex = jnp.pad(cache_mla_kpe, ((0, 0), (0, 0), (0, 0), (pad_l, pad_r)))

    tps_c, tps_l = seq_c // TM, seq_l // TM
    tpd_l = seq_l // TMD
    y_p = x_prompt.reshape(nb_c * seq_c, D_MODEL)
    y_s = x_sample.reshape(nb_l * seq_l, D_MODEL)
    new_cache = tuple(jnp.zeros((nb_c, depth, seq_c, w), F32)
                      for w in (N_KV_A * HD_A, N_KV_A * HD_A, KV_RANK_B, QK_ROPE_B))
    new_state = jnp.zeros((nb_c, depth, 2, N_HEADS_C, DK_C, DV_C), F32)
    for l in range(depth):
        last = l == depth - 1

        mod_row_c = lambda i, l=l: l * 8 + nb_l
        outs = _inproj(y_p, l, mod, mod_row_c, ng, weights, qn, kvn, None, tps_c, (depth, seq_c, new_cache))
        (qa, ka, vat, z, gates, qb, kb, vbt, small, cqkv, abt), new_cache = outs[:11], tuple(outs[11:])
        oa, ob = _attn_ctx(qa, ka, vat, qb, kb, vbt, z, sink, l, seq_c)
        u, w, qg, kd, attn, eg = _gdn_local(cqkv, small, abt, gdn_conv, prow, pcol, l, seq_c)
        cf, cb, new_state = _gdn_scan(u, w, qg, kd, attn, eg, None, l, seq_c, (depth, new_state))
        y_p = _merge(y_p, l, mod, mod_row_c, oa, ob, cf, cb, z, gates, gn, wa, wb, wc, wo, fg, last)

        mod_row_l = lambda i, l=l: l * 8 + i // tpd_l
        (qa, ka, vat, z, gates, qb, kb, vbt, small, cqkv, abt, qnorm) = _inproj(
            y_s, l, mod, mod_row_l, ng, weights, qn, kvn, rope_tabs, tpd_l)
        oa = _attn_a_lat(qa, ka, vat, z, sink, kxa, vxa, l, seq_l)
        kxb, vxb = _kvup(cache_mla_ckv, kpex, wukv, wvbt, l)
        ob = _attn_b_lat(qb, qnorm, kb, vbt, z, kxb, vxb, seq_l, TM, N_HEADS_B // 2)
        u, w, qg, kd, attn, eg = _gdn_local(cqkv, small, abt, gdn_conv, prow, pcol, l, seq_l)
        cf, cb = _gdn_scan(u, w, qg, kd, attn, eg, state_gdn, l, seq_l)
        y_s = _merge(y_s, l, mod, mod_row_l, oa, ob, cf, cb, z, gates, gn, wa, wb, wc, wo, fg, last)

    new_k, new_v, new_ckv, new_kpe = new_cache
    kv_shape = (nb_c, depth, seq_c, N_KV_A, HD_A)
    return (y_p.reshape(nb_c, seq_c, D_MODEL), y_s.reshape(nb_l, seq_l, D_MODEL),
            new_k.reshape(kv_shape), new_v.reshape(kv_shape), new_ckv, new_kpe, new_state)
```

```python
import functools

import numpy as np
import jax
import jax.numpy as jnp
from jax import lax
from jax.experimental import pallas as pl
from jax.experimental.pallas import tpu as pltpu

F32 = jnp.float32
BF16 = jnp.bfloat16

D_MODEL = 1024
GRID_W = 64
ROPE_BASE = 10000.0
EPS = 1e-6
NEG_INF = -1e30
N_HEADS_A = 8
N_KV_A = 2
HD_A = 64
GQA_GROUP = N_HEADS_A // N_KV_A
WINDOW = 128
N_HEADS_B = 8
QK_NOPE_B = 64
QK_ROPE_B = 32
V_HD_B = 64
Q_RANK_B = 384
KV_RANK_B = 256
MLA_SCALE = (QK_NOPE_B + QK_ROPE_B) ** -0.5
N_HEADS_C = 4
DK_C = 128
DV_C = 128
CHUNK = 64
W_A = N_HEADS_A * HD_A
W_B = N_HEADS_B * V_HD_B
W_C = N_HEADS_C * DV_C
QKV_C = 2 * N_HEADS_C * DK_C + W_C
IN_SIZES = (W_A, N_KV_A * HD_A, N_KV_A * HD_A, W_A, Q_RANK_B, KV_RANK_B, QK_ROPE_B, W_B, QKV_C,
            2 * N_HEADS_C, 2 * N_HEADS_C, W_C, 3 * D_MODEL)

LANE = 128
HALF = LANE // 2
TM = 256
TMD = 512
TMG = 256
CPT = TM // CHUNK
SOLVE_CHUNKS = 4
SCAN_SEQS = 4
HALO = 16
MLA_HW = 128
KA_COLS = 4 * LANE
HB_COLS = N_HEADS_B * MLA_HW
VT_ONES = 16
VT_ROWS = V_HD_B + VT_ONES
VAT_ROWS = N_KV_A * VT_ROWS
VBT_ROWS = N_HEADS_B * VT_ROWS
LOG2E = 1.4426950408889634
SHIFT_MAX = 60.0
SHIFT_SLACK = 1.001
SHIFT_SLACK_ABS = 0.01

P_GATES = 0
W_GATES = 3 * D_MODEL
P_QKV = P_GATES + W_GATES
A_COLS = W_A + 3 * LANE
P_Z = P_QKV + A_COLS
W_Z = W_A + W_B + W_C
P_CQ = P_Z + W_Z
P_CKV = P_CQ + Q_RANK_B
P_SMALL = P_CKV + KV_RANK_B
P_CQKV = P_SMALL + LANE
P_END = P_CQKV + QKV_C
S_KPE = 64
S_A = 96
S_B = 104


def _sigmoid(x):
    return 0.5 * jnp.tanh(0.5 * x) + 0.5


def _silu(x):
    return x * _sigmoid(x)


def _softplus(x):
    return jnp.maximum(x, 0.0) + jnp.log(1.0 + jnp.exp(-jnp.abs(x)))


def _dot(a, b):
    return jnp.dot(a, b, preferred_element_type=F32)


def _dot_nt(a, b):
    return lax.dot_general(a, b, (((1,), (1,)), ((), ())), preferred_element_type=F32)


def _bdot(a, b):
    return lax.dot_general(a, b, (((2,), (1,)), ((0,), (0,))), preferred_element_type=F32)


def _bdot_nt(a, b):
    return lax.dot_general(a, b, (((2,), (2,)), ((0,), (0,))), preferred_element_type=F32)


def _bdot_tn(a, b):
    return lax.dot_general(a, b, (((1,), (1,)), ((0,), (0,))), preferred_element_type=F32)


def _dot_exact(a, b):
    return jnp.dot(a, b, preferred_element_type=F32, precision=lax.Precision.HIGHEST)


def _rope(x, c, s):
    n = x.shape[-1]
    lane = lax.broadcasted_iota(jnp.int32, x.shape, 1)
    swapped = jnp.where(lane % 2 == 0, pltpu.roll(x, n - 1, 1), pltpu.roll(x, 1, 1))
    return x * c + swapped * s


def _mod_kernel(cond_ref, w_ref, b_ref, out_ref):
    cnd = cond_ref[...]
    out_ref[0] = _dot(_silu(cnd).astype(BF16), w_ref[0].astype(BF16)) + b_ref[0]


def _modulation(cond8, w_ada, b_ada):
    depth = w_ada.shape[0]
    tn = 768
    return pl.pallas_call(
        _mod_kernel,
        grid=(depth, 3 * D_MODEL // tn),
        in_specs=[pl.BlockSpec((8, D_MODEL), lambda l, n: (0, 0)),
                  pl.BlockSpec((1, D_MODEL, tn), lambda l, n: (l, 0, n)),
                  pl.BlockSpec((1, 1, tn), lambda l, n: (l, 0, n))],
        out_specs=pl.BlockSpec((1, 8, tn), lambda l, n: (l, 0, n)),
        out_shape=jax.ShapeDtypeStruct((depth, 8, 3 * D_MODEL), F32),
        name="adaln_mod",
    )(cond8, w_ada, b_ada.reshape(depth, 1, 3 * D_MODEL))


def _inproj_kernel(*refs, rope, cache_seq, n_carry):
    it = iter(refs)
    (x_ref, mod_ref, ng_ref, wp_ref, wab_ref, wvat_ref, wuq_ref, wukv_ref, wvbt_ref, qn_ref,
     kvn_ref) = (next(it) for _ in range(11))
    if rope:
        ca_ref, sa_ref, cb_ref, sb_ref = (next(it) for _ in range(4))
    for _ in range(n_carry):
        next(it)
    (qa_ref, ka_ref, vat_ref, z_ref, hb_ref, qb_ref, kb_ref, vbt_ref, small_ref, cqkv_ref,
     abt_ref) = (next(it) for _ in range(11))
    if rope:
        qnorm_ref = next(it)
    if cache_seq:
        ck_ref, cv_ref, cckv_ref, ckpe_ref = (next(it) for _ in range(4))

    def to_cache(ref, val):
        for s in range(TMD // cache_seq):
            ref[s] = val[s * cache_seq:(s + 1) * cache_seq]

    x = x_ref[...]
    mod = mod_ref[0]
    shift, scale = mod[:, :D_MODEL], mod[:, D_MODEL:2 * D_MODEL]
    xn = x * lax.rsqrt(jnp.mean(x * x, axis=-1, keepdims=True) + EPS) * ng_ref[...]
    hb = (xn * (1.0 + scale) + shift).astype(BF16)
    lane = lax.broadcasted_iota(jnp.int32, (TMD, LANE), 1)
    lo = lane < HALF

    def mm(lo_col, hi_col):
        return _dot_nt(hb, wp_ref[lo_col:hi_col, :])

    r = mm(P_QKV, P_QKV + A_COLS)
    tiles = [r[:, t * LANE:(t + 1) * LANE] for t in range(A_COLS // LANE)]
    if cache_seq:
        to_cache(ck_ref, tiles[4])
        to_cache(cv_ref, tiles[6])
    if rope:
        ca, sa = ca_ref[...], sa_ref[...]
        tiles[:6] = [_rope(t, ca, sa) for t in tiles[:6]]
    for t in range(4):
        qa_ref[:, t * LANE:(t + 1) * LANE] = (tiles[t] * (HD_A ** -0.5 * LOG2E)).astype(BF16)
    k01, k10 = tiles[4], tiles[5]
    ka_ref[:, 0 * LANE:1 * LANE] = jnp.where(lo, k01, 0.0).astype(BF16)
    ka_ref[:, 1 * LANE:2 * LANE] = jnp.where(lo, 0.0, k10).astype(BF16)
    ka_ref[:, 2 * LANE:3 * LANE] = jnp.where(lo, k10, 0.0).astype(BF16)
    ka_ref[:, 3 * LANE:4 * LANE] = jnp.where(lo, 0.0, k01).astype(BF16)
    ones = jnp.ones((VT_ONES, TMD), BF16)
    vt = _dot_nt(wvat_ref[...], hb)
    for g in range(N_KV_A):
        vat_ref[g * VT_ROWS:g * VT_ROWS + HD_A] = vt[g * HD_A:(g + 1) * HD_A].astype(BF16)
        vat_ref[g * VT_ROWS + HD_A:(g + 1) * VT_ROWS] = ones

    for t in range(3):
        z_ref[:, t * 512:(t + 1) * 512] = mm(P_Z + t * 512, P_Z + (t + 1) * 512).astype(BF16)
    hb_ref[...] = hb

    r = mm(P_CQ, P_CQ + Q_RANK_B)
    qn = r * lax.rsqrt(jnp.mean(r * r, axis=-1, keepdims=True) + EPS) * qn_ref[...]
    q = _dot(qn.astype(BF16), wuq_ref[...])
    if rope:
        cb, sb = cb_ref[...], sb_ref[...]
        for h in range(N_HEADS_B):
            seg = (_rope(q[:, h * MLA_HW:(h + 1) * MLA_HW], cb, sb) * (MLA_SCALE * LOG2E)).astype(BF16)
            qb_ref[:, h * MLA_HW:(h + 1) * MLA_HW] = seg
            s32 = seg.astype(F32)
            qnorm_ref[:, h:h + 1] = jnp.sqrt(jnp.sum(s32 * s32, axis=-1, keepdims=True))
    else:
        qb_ref[...] = (q * (MLA_SCALE * LOG2E)).astype(BF16)

    r = mm(P_SMALL, P_SMALL + LANE)
    small_ref[...] = r
    if cache_seq:
        to_cache(ckpe_ref, r[:, S_KPE:S_KPE + QK_ROPE_B])
    kp = _rope(r, cb, sb) if rope else r
    kp = jnp.where((lane >= S_KPE) & (lane < S_KPE + QK_ROPE_B), kp, 0.0)

    r = mm(P_CKV, P_CKV + KV_RANK_B)
    cn = r * lax.rsqrt(jnp.mean(r * r, axis=-1, keepdims=True) + EPS) * kvn_ref[...]
    if cache_seq:
        to_cache(cckv_ref, cn)
    cn16 = cn.astype(BF16)
    kv = _dot(cn16, wukv_ref[...])
    vt = _dot_nt(wvbt_ref[...], cn16)
    for h in range(N_HEADS_B):
        kb_ref[:, h * MLA_HW:(h + 1) * MLA_HW] = (kv[:, h * MLA_HW:(h + 1) * MLA_HW] + kp).astype(BF16)
        vbt_ref[h * VT_ROWS:h * VT_ROWS + V_HD_B] = vt[h * V_HD_B:(h + 1) * V_HD_B].astype(BF16)
        vbt_ref[h * VT_ROWS + V_HD_B:(h + 1) * VT_ROWS] = ones

    for t in range(3):
        cqkv_ref[:, t * 512:(t + 1) * 512] = mm(P_CQKV + t * 512, P_CQKV + (t + 1) * 512).astype(BF16)

    for c in range(TMD // CHUNK):
        abt_ref[c] = _dot_nt(wab_ref[...], hb[c * CHUNK:(c + 1) * CHUNK])


def _layer_spec(arr, l):
    nd = arr.ndim - 1
    return pl.BlockSpec((None,) + arr.shape[1:], lambda *_: (l,) + (0,) * nd, pipeline_mode=pl.Buffered(1))


def _inproj(x2d, l, mod, mod_row_fn, ng, weights, qn, kvn, rope_tabs, tiles_per_seq, cache=None):
    t = x2d.shape[0]
    nt = t // TMD
    rope = rope_tabs is not None
    row = lambda i: (i, 0)
    col = lambda i: (0, i)
    wp, wab, wvat, wuq, wukv, wvbt = weights
    params = (ng, wp, wab, wvat, wuq, wukv, wvbt, qn, kvn)
    in_specs = [pl.BlockSpec((TMD, D_MODEL), row),
                pl.BlockSpec((1, 1, 3 * D_MODEL), lambda i: (mod_row_fn(i), 0, 0))]
    in_specs += [_layer_spec(a, l) for a in params]
    args = [x2d, mod, *params]
    if rope:
        pos = lambda i: (i % tiles_per_seq, 0)
        in_specs += [pl.BlockSpec((TMD, LANE), pos)] * 4
        args += list(rope_tabs)
    outs = [(W_A, BF16, False), (KA_COLS, BF16, False), (VAT_ROWS, BF16, True), (W_Z, BF16, False),
            (D_MODEL, BF16, False), (HB_COLS, BF16, False), (HB_COLS, BF16, False), (VBT_ROWS, BF16, True),
            (LANE, F32, False), (QKV_C, BF16, False)]
    out_shape = [jax.ShapeDtypeStruct((w, t) if tr else (t, w), dt) for w, dt, tr in outs]
    out_specs = [pl.BlockSpec((w, TMD), col) if tr else pl.BlockSpec((TMD, w), row) for w, _, tr in outs]
    out_shape.append(jax.ShapeDtypeStruct((t // CHUNK, 16, CHUNK), F32))
    out_specs.append(pl.BlockSpec((TMD // CHUNK, 16, CHUNK), lambda i: (i, 0, 0)))
    if rope:
        out_shape.append(jax.ShapeDtypeStruct((t, N_HEADS_B), F32))
        out_specs.append(pl.BlockSpec((TMD, N_HEADS_B), row))
    aliases, cache_seq, n_carry = {}, 0, 0
    if cache is not None:
        depth, cache_seq, carry = cache
        spt = TMD // cache_seq
        for w in (N_KV_A * HD_A, N_KV_A * HD_A, KV_RANK_B, QK_ROPE_B):
            out_shape.append(jax.ShapeDtypeStruct((t // cache_seq, depth, cache_seq, w), F32))
            out_specs.append(pl.BlockSpec((spt, None, cache_seq, w), lambda i: (i, l, 0, 0)))
        n_carry = len(carry)
        for k, a in enumerate(carry):
            aliases[len(args)] = len(out_shape) - n_carry + k
            in_specs.append(pl.BlockSpec(memory_space=pl.ANY))
            args.append(a)
    return pl.pallas_call(
        functools.partial(_inproj_kernel, rope=rope, cache_seq=cache_seq, n_carry=n_carry),
        grid=(nt,),
        in_specs=in_specs,
        out_specs=out_specs,
        out_shape=out_shape,
        input_output_aliases=aliases,
        compiler_params=pltpu.CompilerParams(dimension_semantics=("parallel",)),
        name="inproj_ctx" if cache is not None else "inproj_lat",
    )(*args)


def _kvup_kernel(c_ref, kpe_ref, w_ref, wvt_ref, k_ref, vt_ref):
    c16 = c_ref[...].astype(BF16)
    kv = _dot(c16, w_ref[...])
    vt = _dot_nt(wvt_ref[...], c16)
    kp = kpe_ref[...]
    ones = jnp.ones((VT_ONES, c16.shape[0]), BF16)
    for h in range(N_HEADS_B):
        k_ref[:, h * MLA_HW:(h + 1) * MLA_HW] = (kv[:, h * MLA_HW:(h + 1) * MLA_HW] + kp).astype(BF16)
        vt_ref[h * VT_ROWS:h * VT_ROWS + V_HD_B] = vt[h * V_HD_B:(h + 1) * V_HD_B].astype(BF16)
        vt_ref[h * VT_ROWS + V_HD_B:(h + 1) * VT_ROWS] = ones


def _kvup(ckv, kpe, wukv, wvbt, l):
    nb, _, past, _ = ckv.shape
    return pl.pallas_call(
        _kvup_kernel,
        grid=(nb,),
        in_specs=[pl.BlockSpec((None, None, past, KV_RANK_B), lambda b: (b, l, 0, 0)),
                  pl.BlockSpec((None, None, past, LANE), lambda b: (b, l, 0, 0)),
                  _layer_spec(wukv, l), _layer_spec(wvbt, l)],
        out_specs=[pl.BlockSpec((past, HB_COLS), lambda b: (b, 0)),
                   pl.BlockSpec((None, VBT_ROWS, past), lambda b: (b, 0, 0))],
        out_shape=[jax.ShapeDtypeStruct((nb * past, HB_COLS), BF16),
                   jax.ShapeDtypeStruct((nb, VBT_ROWS, past), BF16)],
        name="mla_cache_up",
    )(ckv, kpe, wukv, wvbt)


def _scores_t(q_tiles, k_tiles, bias_t):
    st = _bdot_nt(jnp.stack(k_tiles), jnp.stack(q_tiles))
    return st if bias_t is None else st + bias_t[None]


def _softmax_pv(sts, vts, sink, shift=None):
    if shift is None:
        m = jnp.max(sts[0], axis=1, keepdims=True)
        for st in sts[1:]:
            m = jnp.maximum(m, jnp.max(st, axis=1, keepdims=True))
    else:
        m = shift
    if sink is not None:
        m = jnp.maximum(m, sink)
    ot = _bdot(jnp.stack(vts[0]), jnp.exp2(sts[0] - m).astype(BF16))
    for st, vt in zip(sts[1:], vts[1:]):
        ot = ot + _bdot(jnp.stack(vt), jnp.exp2(st - m).astype(BF16))
    den = ot[:, V_HD_B:V_HD_B + 1, :]
    if sink is not None:
        den = den + jnp.exp2(sink - m)
    num = ot[:, :V_HD_B, :] / den
    return [jnp.concatenate([num[2 * i], num[2 * i + 1]], axis=0).T for i in range(sts[0].shape[0] // 2)]


def _tile(x, t):
    return x[:, t * LANE:(t + 1) * LANE]


def _attn_a_heads(q, segments, sink_ref):
    sts, vts = [], []
    for ka, vat, bias_t in segments:
        qs, ks, vs = [], [], []
        for t in range(N_HEADS_A // 2):
            g = (2 * t) // GQA_GROUP
            for e in range(2):
                qs.append(_tile(q, t))
                ks.append(_tile(ka, 2 * g + e))
                vs.append(vat[g * VT_ROWS:(g + 1) * VT_ROWS])
        sts.append(_scores_t(qs, ks, bias_t))
        vts.append(vs)
    sink = jnp.stack([sink_ref[:, h:h + 1] * LOG2E for h in range(N_HEADS_A)])
    return _softmax_pv(sts, vts, sink)


def _gated_store(outs, z_ref, o_ref, first_tile=0):
    for i, o in enumerate(outs):
        t = first_tile + i
        z = _tile(z_ref, t).astype(F32)
        o_ref[:, t * LANE:(t + 1) * LANE] = (o * _silu(z)).astype(BF16)


def _attn_ctx_kernel(qa_ref, ka_ref, vat_ref, qb_ref, kb_ref, vbt_ref, za_ref, zb_ref, sink_ref, oa_ref, ob_ref):
    _gated_store(_attn_a_heads(qa_ref[...], [(ka_ref[...], vat_ref[...], None)], sink_ref), za_ref, oa_ref)
    q, kb, vbt = qb_ref[...], kb_ref[...], vbt_ref[...]
    heads = range(N_HEADS_B)
    st = _scores_t([_tile(q, h) for h in heads], [_tile(kb, h) for h in heads], None)
    _gated_store(_softmax_pv([st], [[vbt[h * VT_ROWS:(h + 1) * VT_ROWS] for h in heads]], None), zb_ref, ob_ref)


def _attn_ctx(qa, ka, vat, qb, kb, vbt, z, sink, l, seq):
    t = qa.shape[0]
    row = lambda b: (b, 0)
    col = lambda b: (0, b)
    return pl.pallas_call(
        _attn_ctx_kernel,
        grid=(t // seq,),
        in_specs=[pl.BlockSpec((seq, W_A), row),
                  pl.BlockSpec((seq, KA_COLS), row),
                  pl.BlockSpec((VAT_ROWS, seq), col),
                  pl.BlockSpec((seq, HB_COLS), row),
                  pl.BlockSpec((seq, HB_COLS), row),
                  pl.BlockSpec((VBT_ROWS, seq), col),
                  pl.BlockSpec((seq, W_A), row),
                  pl.BlockSpec((seq, W_B), lambda b: (b, 1)),
                  _layer_spec(sink, l)],
        out_specs=[pl.BlockSpec((seq, W_A), row), pl.BlockSpec((seq, W_B), row)],
        out_shape=[jax.ShapeDtypeStruct((t, W_A), BF16), jax.ShapeDtypeStruct((t, W_B), BF16)],
        compiler_params=pltpu.CompilerParams(dimension_semantics=("parallel",)),
        name="attn_ctx",
    )(qa, ka, vat, qb, kb, vbt, z, z, sink)


def _attn_a_lat_kernel(q_ref, kp_ref, kc_ref, kn_ref, vp_ref, vc_ref, vn_ref, kx_ref, vx_ref, z_ref, sink_ref,
                       o_ref, *, nq):
    j = pl.program_id(1)
    ka = jnp.concatenate([kp_ref[...], kc_ref[...], kn_ref[...]], axis=0)
    vat = jnp.concatenate([vp_ref[...], vc_ref[...], vn_ref[...]], axis=1)
    kj = lax.broadcasted_iota(jnp.int32, (ka.shape[0], TM), 0)
    qi = lax.broadcasted_iota(jnp.int32, (ka.shape[0], TM), 1)
    ok = (kj >= qi) & (kj <= qi + 2 * WINDOW)
    ok = ok & ((kj >= WINDOW) | (j > 0)) & ((kj < TM + WINDOW) | (j < nq - 1))
    bias_t = jnp.where(ok, 0.0, NEG_INF)
    segments = [(ka, vat, bias_t), (kx_ref[...], vx_ref[...], None)]
    _gated_store(_attn_a_heads(q_ref[...], segments, sink_ref), z_ref, o_ref)


def _attn_a_lat(qa, ka, vat, z, sink, kx, vxt, l, seq):
    t = qa.shape[0]
    nq = seq // TM
    past = kx.shape[2]
    r = TM // WINDOW
    row = lambda b, j: (b * nq + j, 0)
    prev = lambda b, j: ((b * nq + j) * r - jnp.where(j > 0, 1, 0), 0)
    nxt = lambda b, j: ((b * nq + j) * r + jnp.where(j < nq - 1, r, r - 1), 0)
    swap = lambda f: (lambda b, j: f(b, j)[::-1])
    return pl.pallas_call(
        functools.partial(_attn_a_lat_kernel, nq=nq),
        grid=(t // seq, nq),
        in_specs=[pl.BlockSpec((TM, W_A), row),
                  pl.BlockSpec((WINDOW, KA_COLS), prev),
                  pl.BlockSpec((TM, KA_COLS), row),
                  pl.BlockSpec((WINDOW, KA_COLS), nxt),
                  pl.BlockSpec((VAT_ROWS, WINDOW), swap(prev)),
                  pl.BlockSpec((VAT_ROWS, TM), swap(row)),
                  pl.BlockSpec((VAT_ROWS, WINDOW), swap(nxt)),
                  pl.BlockSpec((None, None, past, KA_COLS), lambda b, j: (b, l, 0, 0)),
                  pl.BlockSpec((None, None, VAT_ROWS, past), lambda b, j: (b, l, 0, 0)),
                  pl.BlockSpec((TM, W_A), row),
                  _layer_spec(sink, l)],
        out_specs=pl.BlockSpec((TM, W_A), row),
        out_shape=jax.ShapeDtypeStruct((t, W_A), BF16),
        compiler_params=pltpu.CompilerParams(dimension_semantics=("parallel", "parallel")),
        name="attn_a_lat",
    )(qa, ka, ka, ka, vat, vat, vat, kx, vxt, z, sink)


def _attn_b_lat_kernel(q_ref, qn_ref, k_ref, vt_ref, kx_ref, vxt_ref, z_ref, o_ref, knorm_scr, *, group):
    segments = [(k_ref, vt_ref), (kx_ref, vxt_ref)]
    q = q_ref[...]

    @pl.when(pl.program_id(1) == 0)
    def _():
        for h in range(N_HEADS_B):
            ksq = [jnp.max(jnp.sum(jnp.square(_tile(kr, h).astype(F32)), axis=-1, keepdims=True), axis=0,
                           keepdims=True) for kr, _ in segments]
            knorm_scr[h:h + 1, :] = jnp.broadcast_to(jnp.sqrt(jnp.maximum(ksq[0], ksq[1])), (1, LANE))

    qmax = jnp.max(qn_ref[...], axis=0, keepdims=True)
    bound = jnp.stack([qmax[:, h:h + 1] * knorm_scr[h:h + 1, 0:1] * SHIFT_SLACK + SHIFT_SLACK_ABS
                       for h in range(N_HEADS_B)])
    bound_ok = jnp.max(bound) <= SHIFT_MAX

    def scores(h0):
        heads = range(h0, h0 + group)
        return [_scores_t([_tile(q, h) for h in heads], [_tile(kr, h) for h in heads], None) for kr, _ in segments]

    def attend(shift):
        sts = scores(0)
        for h0 in range(0, N_HEADS_B, group):
            sts_next = scores(h0 + group) if h0 + group < N_HEADS_B else None
            vts = [[vr[h * VT_ROWS:(h + 1) * VT_ROWS, :] for h in range(h0, h0 + group)] for _, vr in segments]
            outs = _softmax_pv(sts, vts, None, None if shift is None else shift[h0:h0 + group])
            _gated_store(outs, z_ref, o_ref, h0 // 2)
            sts = sts_next

    @pl.when(bound_ok)
    def _():
        attend(bound)

    @pl.when(jnp.logical_not(bound_ok))
    def _():
        attend(None)


def _attn_b_lat(qb, qnorm, kb, vbt, z, kx, vxt, seq, qblk, group):
    t = qb.shape[0]
    nq = seq // qblk
    past = kx.shape[0] // (t // seq)
    hw = HB_COLS
    return pl.pallas_call(
        functools.partial(_attn_b_lat_kernel, group=group),
        grid=(t // seq, nq),
        in_specs=[pl.BlockSpec((qblk, hw), lambda b, j: (b * nq + j, 0)),
                  pl.BlockSpec((qblk, N_HEADS_B), lambda b, j: (b * nq + j, 0)),
                  pl.BlockSpec((seq, hw), lambda b, j: (b, 0)),
                  pl.BlockSpec((VBT_ROWS, seq), lambda b, j: (0, b)),
                  pl.BlockSpec((past, hw), lambda b, j: (b, 0)),
                  pl.BlockSpec((None, VBT_ROWS, past), lambda b, j: (b, 0, 0)),
                  pl.BlockSpec((qblk, W_B), lambda b, j: (b * nq + j, 1))],
        out_specs=pl.BlockSpec((qblk, W_B), lambda b, j: (b * nq + j, 0)),
        out_shape=jax.ShapeDtypeStruct((t, W_B), BF16),
        scratch_shapes=[pltpu.VMEM((N_HEADS_B, LANE), F32)],
        compiler_params=pltpu.CompilerParams(dimension_semantics=("parallel", "arbitrary")),
        name="attn_b_lat",
    )(qb, qnorm, kb, vbt, kx, vxt, z)


def _gdn_local_kernel(cq_ref, prev_ref, next_ref, small_ref, abt_ref, cw_ref, prow_ref, pcol_ref,
                      u_ref, w_ref, qg_ref, kd_ref, attn_ref, eg_ref, qkv_scr, gb_scr, *, seq):
    x = cq_ref[...].astype(F32)
    tiles_per_seq = seq // TMG
    tpos = pl.program_id(0) % tiles_per_seq
    prev_row = jnp.where(tpos > 0, prev_ref[...].astype(F32)[HALO - 1:HALO, :], 0.0)
    next_row = jnp.where(tpos < tiles_per_seq - 1, next_ref[...].astype(F32)[0:1, :], 0.0)
    rows = lax.broadcasted_iota(jnp.int32, (TMG, 1), 0)
    xm1 = jnp.where(rows == 0, prev_row, pltpu.roll(x, 1, 0))
    xp1 = jnp.where(rows == TMG - 1, next_row, pltpu.roll(x, TMG - 1, 0))
    cw = cw_ref[...]
    y = _silu(xm1 * cw[0:1] + x * cw[1:2] + xp1 * cw[2:3])
    nq = N_HEADS_C * DK_C
    for h in range(N_HEADS_C):
        qh = y[:, h * DK_C:(h + 1) * DK_C]
        kh = y[:, nq + h * DK_C:nq + (h + 1) * DK_C]
        qkv_scr[:, h * DK_C:(h + 1) * DK_C] = (
            qh * lax.rsqrt(jnp.sum(qh * qh, axis=-1, keepdims=True) + EPS) * (DK_C ** -0.5))
        qkv_scr[:, nq + h * DK_C:nq + (h + 1) * DK_C] = kh * lax.rsqrt(jnp.sum(kh * kh, axis=-1, keepdims=True) + EPS)
    qkv_scr[:, 2 * nq:] = y[:, 2 * nq:]

    sm = small_ref[...]
    prow = prow_ref[...]
    gb_scr[:, 0:8] = -jnp.exp(prow[0:1]) * _softplus(sm[:, S_A:S_A + 8] + prow[1:2])
    gb_scr[:, 8:16] = _sigmoid(sm[:, S_B:S_B + 8])
    pcol = pcol_ref[...]

    ri = lax.broadcasted_iota(jnp.int32, (CHUNK, LANE), 0)
    lane = lax.broadcasted_iota(jnp.int32, (CHUNK, LANE), 1)
    fwd = lane < CHUNK
    cj = lane & (CHUNK - 1)
    incl = (fwd & (ri >= cj)) | (~fwd & (ri <= cj))
    strict = (fwd & (ri > cj)) | (~fwd & (ri < cj))
    xor = ri ^ cj
    eye = (ri == cj).astype(F32)
    r2 = lax.broadcasted_iota(jnp.int32, (2 * CHUNK, LANE), 0)
    l2 = lax.broadcasted_iota(jnp.int32, (2 * CHUNK, LANE), 1)
    same_dir = (r2 < CHUNK) == (l2 < CHUNK)
    rs_ = lax.broadcasted_iota(jnp.int32, (CHUNK, CHUNK), 0)
    cs_ = lax.broadcasted_iota(jnp.int32, (CHUNK, CHUNK), 1)
    tril = (rs_ >= cs_).astype(F32)
    triu = (rs_ <= cs_).astype(F32)
    tri_rows = jnp.concatenate([triu, tril], axis=1)
    dup_rows = jnp.concatenate([(rs_ == cs_).astype(F32)] * 2, axis=1)

    def block_diag(x):
        return jnp.where(same_dir[None], jnp.concatenate([x, x], axis=1), 0.0).astype(BF16)

    lows, rhss, order = [], [], []

    def solve():
        low = jnp.stack(lows, axis=0)
        inv = eye[None] - jnp.where(xor[None] == 1, low, 0.0)
        b = 2
        while b < CHUNK:
            cpl = jnp.where((xor[None] >= b) & (xor[None] < 2 * b), low, 0.0)
            tmp = _bdot(cpl.astype(BF16), block_diag(inv))
            inv = inv - _bdot(inv.astype(BF16), block_diag(tmp))
            b *= 2
        scale_u, scale_w, vs, ks = (jnp.stack(a, axis=0) for a in zip(*rhss))
        u = _bdot((inv * scale_u).astype(BF16), vs)
        w = _bdot((inv * scale_w).astype(BF16), ks)
        for i, (rs, h) in enumerate(order):
            for d in range(2):
                cs = slice((d * N_HEADS_C + h) * DK_C, (d * N_HEADS_C + h + 1) * DK_C)
                u_ref[rs, cs] = u[i, :, d * DV_C:(d + 1) * DV_C].astype(BF16)
                w_ref[rs, cs] = w[i, :, d * DK_C:(d + 1) * DK_C].astype(BF16)
        lows.clear(), rhss.clear(), order.clear()

    for c in range(TMG // CHUNK):
        if c % SOLVE_CHUNKS == 0 and c > 0:
            solve()
        rs = slice(c * CHUNK, (c + 1) * CHUNK)
        gcol = gb_scr[rs, 0:8]
        bcol = gb_scr[rs, 8:16]
        abt = abt_ref[c]
        grow = -jnp.exp(pcol[:, 0:1]) * _softplus(abt[0:8] + pcol[:, 1:2])
        gc_f = _dot_exact(tril, gcol)
        gc_b = _dot_exact(triu, gcol)
        gr = _dot_exact(grow, tri_rows)
        br = _dot_exact(_sigmoid(abt[8:16]), dup_rows)
        for h in range(N_HEADS_C):
            hb_ = N_HEADS_C + h
            q = qkv_scr[rs, h * DK_C:(h + 1) * DK_C]
            k = qkv_scr[rs, nq + h * DK_C:nq + (h + 1) * DK_C]
            v = qkv_scr[rs, 2 * nq + h * DV_C:2 * nq + (h + 1) * DV_C]
            k16 = k.astype(BF16)
            kk16 = jnp.concatenate([k16, k16], axis=0)
            kk = _dot_nt(k16, kk16)
            qk = _dot_nt(q.astype(BF16), kk16)
            gcs = (gc_f[:, h:h + 1], gc_b[:, hb_:hb_ + 1])
            betas = (bcol[:, h:h + 1], bcol[:, hb_:hb_ + 1])
            gc2 = jnp.where(fwd, gcs[0], gcs[1])
            gr2 = jnp.where(fwd[0:1], gr[h:h + 1, :], gr[hb_:hb_ + 1, :])
            decay = jnp.where(incl, jnp.exp(jnp.where(incl, gc2 - gr2, 0.0)), 0.0)
            lows.append(jnp.where(strict, jnp.where(fwd, betas[0], betas[1]) * kk * decay, 0.0))
            attn_ref[rs, h * LANE:(h + 1) * LANE] = (qk * decay).astype(BF16)
            order.append((rs, h))
            br2 = jnp.where(fwd[0:1], br[h:h + 1, :], br[hb_:hb_ + 1, :])
            v16 = v.astype(BF16)
            zeros = jnp.zeros_like(v16)
            rhss.append((br2, br2 * jnp.exp(gr2),
                         jnp.concatenate([jnp.concatenate([v16, zeros], axis=1),
                                          jnp.concatenate([zeros, v16], axis=1)], axis=0),
                         jnp.concatenate([jnp.concatenate([k16, zeros], axis=1),
                                          jnp.concatenate([zeros, k16], axis=1)], axis=0)))
            for d in range(2):
                dh = d * N_HEADS_C + h
                gc = gcs[d]
                eg = jnp.exp(gc)
                g_last = gc[CHUNK - 1:CHUNK] if d == 0 else gc[0:1]
                cs = slice(dh * DK_C, (dh + 1) * DK_C)
                qg_ref[rs, cs] = (q * eg).astype(BF16)
                kd_ref[rs, cs] = (k * jnp.exp(g_last - gc)).astype(BF16)
                eg_ref[c, dh:dh + 1, :] = jnp.broadcast_to(jnp.exp(g_last), (1, LANE))
    solve()


def _gdn_local(cqkv, small, abt, conv_w, prow, pcol, l, seq):
    t = cqkv.shape[0]
    assert seq % TMG == 0
    nt = t // TMG
    nh8 = t // HALO
    cpg = TMG // CHUNK
    row = lambda i: (i, 0)
    dh = 2 * N_HEADS_C
    return pl.pallas_call(
        functools.partial(_gdn_local_kernel, seq=seq),
        grid=(nt,),
        in_specs=[pl.BlockSpec((TMG, QKV_C), row),
                  pl.BlockSpec((HALO, QKV_C), lambda i: (jnp.maximum(i * (TMG // HALO) - 1, 0), 0)),
                  pl.BlockSpec((HALO, QKV_C), lambda i: (jnp.minimum((i + 1) * (TMG // HALO), nh8 - 1), 0)),
                  pl.BlockSpec((TMG, LANE), row),
                  pl.BlockSpec((cpg, 16, CHUNK), lambda i: (i, 0, 0)),
                  _layer_spec(conv_w, l), _layer_spec(prow, l), _layer_spec(pcol, l)],
        out_specs=[pl.BlockSpec((TMG, dh * DV_C), row),
                   pl.BlockSpec((TMG, dh * DK_C), row),
                   pl.BlockSpec((TMG, dh * DK_C), row),
                   pl.BlockSpec((TMG, dh * DK_C), row),
                   pl.BlockSpec((TMG, dh * CHUNK), row),
                   pl.BlockSpec((cpg, dh, LANE), lambda i: (i, 0, 0))],
        out_shape=[jax.ShapeDtypeStruct((t, dh * DV_C), BF16),
                   jax.ShapeDtypeStruct((t, dh * DK_C), BF16),
                   jax.ShapeDtypeStruct((t, dh * DK_C), BF16),
                   jax.ShapeDtypeStruct((t, dh * DK_C), BF16),
                   jax.ShapeDtypeStruct((t, dh * CHUNK), BF16),
                   jax.ShapeDtypeStruct((t // CHUNK, dh, LANE), F32)],
        scratch_shapes=[pltpu.VMEM((TMG, QKV_C), F32), pltpu.VMEM((TMG, 16), F32)],
        compiler_params=pltpu.CompilerParams(dimension_semantics=("parallel",)),
        name="gdn_local",
    )(cqkv, cqkv, cqkv, small, abt, conv_w, prow, pcol)


def _gdn_scan_kernel(*refs, nt, ns, has_init, want_state, n_carry):
    it = iter(refs)
    ins = [[next(it) for _ in range(6)] for _ in range(2)]
    s0_ref = next(it) if has_init else None
    for _ in range(n_carry):
        next(it)
    o_refs = [next(it), next(it)]
    st_ref = next(it) if want_state else None
    s_scr = next(it)
    j = pl.program_id(1)
    nh = N_HEADS_C
    nst = ns * 2 * nh

    @pl.when(j == 0)
    def _():
        if has_init:
            s_scr[...] = s0_ref[...].reshape(nst, DK_C, DV_C)
        else:
            s_scr[...] = jnp.zeros_like(s_scr)

    for step in range(CPT):
        chunk = lambda d: step if d == 0 else CPT - 1 - step

        def gather(idx, width):
            return jnp.stack([ins[d][idx][s, chunk(d) * CHUNK:(chunk(d) + 1) * CHUNK, h * width:(h + 1) * width]
                              for s in range(ns) for d in range(2) for h in range(nh)])

        u, w, qg, kd, attn = gather(0, DV_C), gather(1, DK_C), gather(2, DK_C), gather(3, DK_C), gather(4, LANE)
        eg = jnp.stack([ins[d][5][s, chunk(d), d * nh + h:d * nh + h + 1, :]
                        for s in range(ns) for d in range(2) for h in range(nh)])
        st = s_scr[...]
        sb = st.astype(BF16)
        v_new = u.astype(F32) - _bdot(w, sb)
        vb = v_new.astype(BF16)
        zeros = jnp.zeros((CHUNK, DV_C), BF16)
        vb2 = jnp.stack([jnp.concatenate([vb[i], zeros] if (i // nh) % 2 == 0 else [zeros, vb[i]], axis=0)
                         for i in range(nst)])
        o = _bdot(qg, sb) + _bdot(attn, vb2)
        s_scr[...] = st * eg + _bdot_tn(kd, vb)
        for s in range(ns):
            for d in range(2):
                for h in range(nh):
                    o_refs[d][s, chunk(d) * CHUNK:(chunk(d) + 1) * CHUNK, h * DV_C:(h + 1) * DV_C] = (
                        o[(s * 2 + d) * nh + h].astype(BF16))

    if want_state:
        @pl.when(j == nt - 1)
        def _():
            st_ref[...] = s_scr[...].reshape(ns, 2, nh, DK_C, DV_C)


def _gdn_scan(u, w, qg, kd, attn, eg, s0, l, seq, state_out=None):
    t = u.shape[0]
    want_state = state_out is not None
    nt = seq // TM
    nb = t // seq
    ns = next(n for n in (SCAN_SEQS, 2, 1) if nb % n == 0)
    half = N_HEADS_C * DK_C
    has_init = s0 is not None
    by_seq = lambda a: a.reshape((nb, a.shape[0] // nb) + a.shape[1:])
    in_specs, args = [], []
    for d in range(2):
        tile = (lambda b, j: j) if d == 0 else (lambda b, j: nt - 1 - j)
        row = lambda b, j, d=d, tile=tile: (b, tile(b, j), d)
        row4 = lambda b, j, tile=tile: (b, tile(b, j), 0, 0)
        in_specs += [pl.BlockSpec((ns, TM, half), row)] * 4
        in_specs += [pl.BlockSpec((ns, TM, N_HEADS_C * LANE), lambda b, j, tile=tile: (b, tile(b, j), 0)),
                     pl.BlockSpec((ns, CPT, 2 * N_HEADS_C, LANE), row4)]
        args += [by_seq(a) for a in (u, w, qg, kd, attn, eg)]
    st_tail = (2, N_HEADS_C, DK_C, DV_C)
    if has_init:
        in_specs.append(pl.BlockSpec((ns, None) + st_tail, lambda b, j: (b, l, 0, 0, 0, 0)))
        args.append(s0)
    out_specs = [pl.BlockSpec((ns, TM, half), lambda b, j: (b, j, 0)),
                 pl.BlockSpec((ns, TM, half), lambda b, j: (b, nt - 1 - j, 0))]
    out_shape = [jax.ShapeDtypeStruct((nb, seq, half), BF16)] * 2
    aliases, n_carry = {}, 0
    if want_state:
        depth, carry = state_out
        out_specs.append(pl.BlockSpec((ns, None) + st_tail, lambda b, j: (b, l, 0, 0, 0, 0)))
        out_shape.append(jax.ShapeDtypeStruct((nb, depth) + st_tail, F32))
        n_carry = 1
        aliases[len(args)] = 2
        in_specs.append(pl.BlockSpec(memory_space=pl.ANY))
        args.append(carry)
    outs = pl.pallas_call(
        functools.partial(_gdn_scan_kernel, nt=nt, ns=ns, has_init=has_init, want_state=want_state,
                          n_carry=n_carry),
        grid=(nb // ns, nt),
        in_specs=in_specs,
        out_specs=out_specs,
        out_shape=out_shape,
        input_output_aliases=aliases,
        scratch_shapes=[pltpu.VMEM((ns * 2 * N_HEADS_C, DK_C, DV_C), F32)],
        compiler_params=pltpu.CompilerParams(dimension_semantics=("parallel", "arbitrary")),
        name="gdn_scan",
    )(*args)
    return [outs[0].reshape(t, half), outs[1].reshape(t, half)] + list(outs[2:])


def _merge_kernel(x_ref, mod_ref, oa_ref, ob_ref, cf_ref, cb_ref, zc_ref, hb_ref, wg_ref, gn_ref,
                  wa_ref, wb_ref, wc_ref, wo_ref, fg_ref, o_ref, *, last):
    oc = cf_ref[...].astype(F32) + cb_ref[...].astype(F32)
    zc = zc_ref[...].astype(F32)
    gn = gn_ref[...]
    parts = []
    for h in range(N_HEADS_C):
        hs = slice(h * DV_C, (h + 1) * DV_C)
        och = oc[:, hs]
        och = och * lax.rsqrt(jnp.mean(och * och, axis=-1, keepdims=True) + EPS) * gn
        parts.append((och * _silu(zc[:, hs])).astype(BF16))
    ocz = jnp.concatenate(parts, axis=-1)
    pa = _dot(oa_ref[...], wa_ref[...])
    pb = _dot(ob_ref[...], wb_ref[...])
    pc = _dot(ocz, wc_ref[...])
    hb = hb_ref[...]
    ga = _sigmoid(_dot_nt(hb, wg_ref[0:D_MODEL, :]))
    gb = _sigmoid(_dot_nt(hb, wg_ref[D_MODEL:2 * D_MODEL, :]))
    gc = _sigmoid(_dot_nt(hb, wg_ref[2 * D_MODEL:, :]))
    y = _dot((ga * pa + gb * pb + gc * pc).astype(BF16), wo_ref[...])
    gate = mod_ref[0][:, 2 * D_MODEL:]
    xo = x_ref[...] + gate * y
    if last:
        xo = xo * lax.rsqrt(jnp.mean(xo * xo, axis=-1, keepdims=True) + EPS) * fg_ref[...]
    o_ref[...] = xo


def _merge(x2d, l, mod, mod_row_fn, oa, ob, cf, cb, z, hb, wp, gn, wa, wb, wc, wo, fg, last):
    t = x2d.shape[0]
    row = lambda i: (i, 0)
    assert P_GATES == 0
    return pl.pallas_call(
        functools.partial(_merge_kernel, last=last),
        grid=(t // TMD,),
        in_specs=[pl.BlockSpec((TMD, D_MODEL), row),
                  pl.BlockSpec((1, 1, 3 * D_MODEL), lambda i: (mod_row_fn(i), 0, 0)),
                  pl.BlockSpec((TMD, W_A), row),
                  pl.BlockSpec((TMD, W_B), row),
                  pl.BlockSpec((TMD, W_C), row),
                  pl.BlockSpec((TMD, W_C), row),
                  pl.BlockSpec((TMD, W_C), lambda i: (i, 2)),
                  pl.BlockSpec((TMD, D_MODEL), row),
                  pl.BlockSpec((None, W_GATES, D_MODEL), lambda i: (l, 0, 0), pipeline_mode=pl.Buffered(1)),
                  _layer_spec(gn, l), _layer_spec(wa, l), _layer_spec(wb, l), _layer_spec(wc, l),
                  _layer_spec(wo, l),
                  pl.BlockSpec((1, D_MODEL), lambda i: (0, 0))],
        out_specs=pl.BlockSpec((TMD, D_MODEL), row),
        out_shape=jax.ShapeDtypeStruct((t, D_MODEL), F32),
        compiler_params=pltpu.CompilerParams(dimension_semantics=("parallel",)),
        name="merge",
    )(x2d, mod, oa, ob, cf, cb, z, hb, wp, gn, wa, wb, wc, wo, fg)


def _rope_tables(n_tokens, rot_dim):
    rows = n_tokens // GRID_W
    row = np.repeat(np.arange(rows), GRID_W).astype(np.float32)
    col = np.tile(np.arange(GRID_W), rows).astype(np.float32)
    n_pairs = rot_dim // 4
    inv = (np.float32(ROPE_BASE) ** (-np.arange(n_pairs, dtype=np.float32) / np.float32(n_pairs))).astype(np.float32)
    ang = np.concatenate([row[:, None] * inv, col[:, None] * inv], axis=-1)
    c, s = np.cos(ang), np.sin(ang)
    return np.repeat(c, 2, axis=-1), np.stack([-s, s], axis=-1).reshape(n_tokens, rot_dim)


def _in_offsets():
    o = [0]
    for n in IN_SIZES:
        o.append(o[-1] + n)
    return o


def _relayout_moves():
    o = _in_offsets()
    order = [(o[12], W_GATES), (o[0], W_A), (o[1], LANE), (o[1] + HD_A, HD_A), (o[1], HD_A), (o[2], LANE),
             (o[3], W_A), (o[7], W_B), (o[11], W_C), (o[4], Q_RANK_B), (o[5], KV_RANK_B),
             (None, S_KPE), (o[6], QK_ROPE_B), (o[9], 4 * N_HEADS_C), (None, LANE - S_B - 8), (o[8], QKV_C)]
    moves, dst = [], 0
    for src, n in order:
        moves.append((src, dst, n))
        dst += n
    assert dst == P_END
    return moves


def _relayout_kernel(w_ref, o_ref, wab_ref, wvat_ref):
    for src, dst, n in _relayout_moves():
        if src is None:
            o_ref[dst:dst + n, :] = jnp.zeros((n, o_ref.shape[1]), BF16)
        else:
            o_ref[dst:dst + n, :] = w_ref[src:src + n, :].astype(BF16)
    o = _in_offsets()
    wab_ref[...] = w_ref[o[9]:o[11], :].astype(BF16)
    wvat_ref[...] = w_ref[o[2]:o[3], :].astype(BF16)


def _relayout_w_in(w_in):
    w_t = jnp.swapaxes(w_in, 1, 2)
    depth, width, _ = w_t.shape
    cols = 128
    nab, nv = 4 * N_HEADS_C, N_KV_A * HD_A
    col = lambda l, i: (l, 0, i)
    return pl.pallas_call(
        _relayout_kernel,
        grid=(depth, D_MODEL // cols),
        in_specs=[pl.BlockSpec((None, width, cols), col)],
        out_specs=[pl.BlockSpec((None, P_END, cols), col),
                   pl.BlockSpec((None, nab, cols), col),
                   pl.BlockSpec((None, nv, cols), col)],
        out_shape=[jax.ShapeDtypeStruct((depth, P_END, D_MODEL), BF16),
                   jax.ShapeDtypeStruct((depth, nab, D_MODEL), BF16),
                   jax.ShapeDtypeStruct((depth, nv, D_MODEL), BF16)],
        name="w_in_relayout",
    )(w_t)


def _prep_weights(w_in, w_uq, w_ukv):
    depth = w_in.shape[0]
    wp, wab, wvat = _relayout_w_in(w_in)
    hd = QK_NOPE_B + QK_ROPE_B
    wuq = jnp.pad(w_uq.reshape(depth, Q_RANK_B, N_HEADS_B, hd), ((0, 0), (0, 0), (0, 0), (0, MLA_HW - hd)))
    wuq = wuq.reshape(depth, Q_RANK_B, HB_COLS).astype(BF16)
    kv = w_ukv.reshape(depth, KV_RANK_B, N_HEADS_B, QK_NOPE_B + V_HD_B)
    wk = jnp.pad(kv[..., :QK_NOPE_B], ((0, 0), (0, 0), (0, 0), (0, MLA_HW - QK_NOPE_B)))
    wukv = wk.reshape(depth, KV_RANK_B, HB_COLS).astype(BF16)
    wvbt = jnp.swapaxes(kv[..., QK_NOPE_B:].reshape(depth, KV_RANK_B, W_B), 1, 2).astype(BF16)
    return wp, wab, wvat, wuq, wukv, wvbt


def _cache_tiles_a(kx, vx):
    k0, k1 = kx[..., 0, :], kx[..., 1, :]
    z = jnp.zeros_like(k0)
    ka = jnp.concatenate([k0, z, z, k0, k1, z, z, k1], axis=-1).astype(BF16)
    vt = jnp.transpose(vx, (0, 1, 3, 4, 2))
    vt = jnp.concatenate([vt, jnp.ones(vt.shape[:3] + (VT_ONES, vt.shape[4]), vt.dtype)], axis=3)
    return ka, vt.reshape(vt.shape[:2] + (VAT_ROWS, vt.shape[4])).astype(BF16)


def kernel(x_prompt, x_sample, cache_attn_k, cache_attn_v, cache_mla_ckv, cache_mla_kpe, state_gdn, c, c_ctx,
           norm_g, w_ada, b_ada, w_in, attn_sink, mla_q_norm, mla_w_uq, mla_kv_norm, mla_w_ukv, gdn_conv,
           gdn_a_log, gdn_dt_bias, gdn_norm, w_branch_a, w_branch_b, w_branch_c, w_out, final_norm_g):
    depth = w_in.shape[0]
    nb_c, seq_c, _ = x_prompt.shape
    nb_l, seq_l, _ = x_sample.shape
    past = cache_attn_k.shape[2]
    assert P_END % LANE == 0 and seq_c % TM == 0 and seq_l % TMD == 0 and nb_l < 8 and TM == 2 * WINDOW
    assert (nb_c * seq_c) % TMD == 0 and TMD % TM == 0

    cond8 = jnp.zeros((8, D_MODEL), F32).at[:nb_l].set(c).at[nb_l].set(c_ctx)
    mod = _modulation(cond8, w_ada, b_ada).reshape(depth * 8, 1, 3 * D_MODEL)

    c_a, s_a = _rope_tables(seq_l, HD_A)
    c_b, s_b = _rope_tables(seq_l, QK_ROPE_B)
    pad_l, pad_r = S_KPE, LANE - S_KPE - QK_ROPE_B
    one, zero = np.ones((seq_l, 1), np.float32), np.zeros((seq_l, 1), np.float32)
    rope_tabs = tuple(jnp.asarray(a) for a in (
        np.tile(c_a, (1, LANE // HD_A)), np.tile(s_a, (1, LANE // HD_A)),
        np.concatenate([np.tile(one, (1, pad_l)), c_b, np.tile(one, (1, pad_r))], 1),
        np.concatenate([np.tile(zero, (1, pad_l)), s_b, np.tile(zero, (1, pad_r))], 1)))

    weights = _prep_weights(w_in, mla_w_uq, mla_w_ukv)
    wukv, wvbt = weights[4], weights[5]
    ng = norm_g.reshape(depth, 1, D_MODEL)
    qn = mla_q_norm.reshape(depth, 1, Q_RANK_B)
    kvn = mla_kv_norm.reshape(depth, 1, KV_RANK_B)
    sink = attn_sink.reshape(depth, 1, N_HEADS_A)
    prow = jnp.stack([gdn_a_log.reshape(depth, -1), gdn_dt_bias.reshape(depth, -1)], axis=1)
    pcol = jnp.swapaxes(prow, 1, 2)
    gn = gdn_norm.reshape(depth, 1, DV_C)
    wa, wb, wc, wo = (w.astype(BF16) for w in (w_branch_a, w_branch_b, w_branch_c, w_out))
    fg = final_norm_g.reshape(1, D_MODEL)
    kxa, vxa = _cache_tiles_a(cache_attn_k, cache_attn_v)
    kpex = jnp.pad(cache_mla_kpe, ((0, 0), (0, 0), (0, 0), (pad_l, pad_r)))

    tps_c, tps_l = seq_c // TM, seq_l // TM
    tpd_l = seq_l // TMD
    y_p = x_prompt.reshape(nb_c * seq_c, D_MODEL)
    y_s = x_sample.reshape(nb_l * seq_l, D_MODEL)
    new_cache = tuple(jnp.zeros((nb_c, depth, seq_c, w), F32)
                      for w in (N_KV_A * HD_A, N_KV_A * HD_A, KV_RANK_B, QK_ROPE_B))
    new_state = jnp.zeros((nb_c, depth, 2, N_HEADS_C, DK_C, DV_C), F32)
    for l in range(depth):
        last = l == depth - 1

        mod_row_c = lambda i, l=l: l * 8 + nb_l
        outs = _inproj(y_p, l, mod, mod_row_c, ng, weights, qn, kvn, None, tps_c, (depth, seq_c, new_cache))
        (qa, ka, vat, z, hb, qb, kb, vbt, small, cqkv, abt), new_cache = outs[:11], tuple(outs[11:])
        oa, ob = _attn_ctx(qa, ka, vat, qb, kb, vbt, z, sink, l, seq_c)
        u, w, qg, kd, attn, eg = _gdn_local(cqkv, small, abt, gdn_conv, prow, pcol, l, seq_c)
        cf, cb, new_state = _gdn_scan(u, w, qg, kd, attn, eg, None, l, seq_c, (depth, new_state))
        y_p = _merge(y_p, l, mod, mod_row_c, oa, ob, cf, cb, z, hb, weights[0], gn, wa, wb, wc, wo, fg, last)

        mod_row_l = lambda i, l=l: l * 8 + i // tpd_l
        (qa, ka, vat, z, hb, qb, kb, vbt, small, cqkv, abt, qnorm) = _inproj(
            y_s, l, mod, mod_row_l, ng, weights, qn, kvn, rope_tabs, tpd_l)
        oa = _attn_a_lat(qa, ka, vat, z, sink, kxa, vxa, l, seq_l)
        kxb, vxb = _kvup(cache_mla_ckv, kpex, wukv, wvbt, l)
        ob = _attn_b_lat(qb, qnorm, kb, vbt, z, kxb, vxb, seq_l, TM, N_HEADS_B // 2)
        u, w, qg, kd, attn, eg = _gdn_local(cqkv, small, abt, gdn_conv, prow, pcol, l, seq_l)
        cf, cb = _gdn_scan(u, w, qg, kd, attn, eg, state_gdn, l, seq_l)
        y_s = _merge(y_s, l, mod, mod_row_l, oa, ob, cf, cb, z, hb, weights[0], gn, wa, wb, wc, wo, fg, last)

    new_k, new_v, new_ckv, new_kpe = new_cache
    kv_shape = (nb_c, depth, seq_c, N_KV_A, HD_A)
    return (y_p.reshape(nb_c, seq_c, D_MODEL), y_s.reshape(nb_l, seq_l, D_MODEL),
            new_k.reshape(kv_shape), new_v.reshape(kv_shape), new_ckv, new_kpe, new_state)
```

```python
import functools

import numpy as np
import jax
import jax.numpy as jnp
from jax import lax
from jax.experimental import pallas as pl
from jax.experimental.pallas import tpu as pltpu

F32 = jnp.float32
BF16 = jnp.bfloat16

D_MODEL = 1024
GRID_W = 64
ROPE_BASE = 10000.0
EPS = 1e-6
NEG_INF = -1e30
N_HEADS_A = 8
N_KV_A = 2
HD_A = 64
GQA_GROUP = N_HEADS_A // N_KV_A
WINDOW = 128
N_HEADS_B = 8
QK_NOPE_B = 64
QK_ROPE_B = 32
V_HD_B = 64
Q_RANK_B = 384
KV_RANK_B = 256
MLA_SCALE = (QK_NOPE_B + QK_ROPE_B) ** -0.5
N_HEADS_C = 4
DK_C = 128
DV_C = 128
CHUNK = 64
W_A = N_HEADS_A * HD_A
W_B = N_HEADS_B * V_HD_B
W_C = N_HEADS_C * DV_C
QKV_C = 2 * N_HEADS_C * DK_C + W_C
IN_SIZES = (W_A, N_KV_A * HD_A, N_KV_A * HD_A, W_A, Q_RANK_B, KV_RANK_B, QK_ROPE_B, W_B, QKV_C,
            2 * N_HEADS_C, 2 * N_HEADS_C, W_C, 3 * D_MODEL)

LANE = 128
HALF = LANE // 2
TM = 256
TMD = 512
TMG = 256
CPT = TM // CHUNK
SCAN_SEQS = 4
HALO = 16
MLA_HW = 128
KA_COLS = 4 * LANE
HB_COLS = N_HEADS_B * MLA_HW
VT_ONES = 16
VT_ROWS = V_HD_B + VT_ONES
VAT_ROWS = N_KV_A * VT_ROWS
VBT_ROWS = N_HEADS_B * VT_ROWS
LOG2E = 1.4426950408889634
SHIFT_MAX = 60.0
SHIFT_SLACK = 1.001
SHIFT_SLACK_ABS = 0.01
N_QB = 0
N_KB = N_QB + N_HEADS_B
N_QA = N_KB + N_HEADS_B
N_KA = N_QA + N_HEADS_A
NORM_COLS = 32

P_GATES = 0
W_GATES = 3 * D_MODEL
P_QKV = P_GATES + W_GATES
A_COLS = W_A + 3 * LANE
P_Z = P_QKV + A_COLS
W_Z = W_A + W_B + W_C
P_CQ = P_Z + W_Z
P_CKV = P_CQ + Q_RANK_B
P_SMALL = P_CKV + KV_RANK_B
P_CQKV = P_SMALL + LANE
P_END = P_CQKV + QKV_C
S_KPE = 64
S_A = 96
S_B = 104


def _sigmoid(x):
    return 0.5 * jnp.tanh(0.5 * x) + 0.5


def _silu(x):
    return x * _sigmoid(x)


def _softplus(x):
    return jnp.maximum(x, 0.0) + jnp.log(1.0 + jnp.exp(-jnp.abs(x)))


def _dot(a, b):
    return jnp.dot(a, b, preferred_element_type=F32)


def _dot_nt(a, b):
    return lax.dot_general(a, b, (((1,), (1,)), ((), ())), preferred_element_type=F32)


def _bdot(a, b):
    return lax.dot_general(a, b, (((2,), (1,)), ((0,), (0,))), preferred_element_type=F32)


def _bdot_nt(a, b):
    return lax.dot_general(a, b, (((2,), (2,)), ((0,), (0,))), preferred_element_type=F32)


def _bdot_tn(a, b):
    return lax.dot_general(a, b, (((1,), (1,)), ((0,), (0,))), preferred_element_type=F32)


def _dot_exact(a, b):
    return jnp.dot(a, b, preferred_element_type=F32, precision=lax.Precision.HIGHEST)


def _row_norm(x16):
    x = x16.astype(F32)
    return jnp.sqrt(jnp.sum(x * x, axis=-1, keepdims=True))


def _rope(x, c, s):
    n = x.shape[-1]
    lane = lax.broadcasted_iota(jnp.int32, x.shape, 1)
    swapped = jnp.where(lane % 2 == 0, pltpu.roll(x, n - 1, 1), pltpu.roll(x, 1, 1))
    return x * c + swapped * s


def _mod_kernel(cond_ref, w_ref, b_ref, out_ref):
    cnd = cond_ref[...]
    out_ref[0] = _dot(_silu(cnd).astype(BF16), w_ref[0].astype(BF16)) + b_ref[0]


def _modulation(cond8, w_ada, b_ada):
    depth = w_ada.shape[0]
    tn = 768
    return pl.pallas_call(
        _mod_kernel,
        grid=(depth, 3 * D_MODEL // tn),
        in_specs=[pl.BlockSpec((8, D_MODEL), lambda l, n: (0, 0)),
                  pl.BlockSpec((1, D_MODEL, tn), lambda l, n: (l, 0, n)),
                  pl.BlockSpec((1, 1, tn), lambda l, n: (l, 0, n))],
        out_specs=pl.BlockSpec((1, 8, tn), lambda l, n: (l, 0, n)),
        out_shape=jax.ShapeDtypeStruct((depth, 8, 3 * D_MODEL), F32),
        name="adaln_mod",
    )(cond8, w_ada, b_ada.reshape(depth, 1, 3 * D_MODEL))


def _inproj_kernel(*refs, rope, cache_seq, n_carry):
    it = iter(refs)
    (x_ref, mod_ref, ng_ref, wp_ref, wab_ref, wvat_ref, wuq_ref, wukv_ref, wvbt_ref, qn_ref,
     kvn_ref) = (next(it) for _ in range(11))
    if rope:
        ca_ref, sa_ref, cb_ref, sb_ref = (next(it) for _ in range(4))
    for _ in range(n_carry):
        next(it)
    (qa_ref, ka_ref, vat_ref, z_ref, hb_ref, qb_ref, kb_ref, vbt_ref, small_ref, cqkv_ref,
     abt_ref) = (next(it) for _ in range(11))
    if rope:
        qnorm_ref = next(it)
    if cache_seq:
        ck_ref, cv_ref, cckv_ref, ckpe_ref = (next(it) for _ in range(4))

    def to_cache(ref, val):
        for s in range(TMD // cache_seq):
            ref[s] = val[s * cache_seq:(s + 1) * cache_seq]

    x = x_ref[...]
    mod = mod_ref[0]
    shift, scale = mod[:, :D_MODEL], mod[:, D_MODEL:2 * D_MODEL]
    xn = x * lax.rsqrt(jnp.mean(x * x, axis=-1, keepdims=True) + EPS) * ng_ref[...]
    hb = (xn * (1.0 + scale) + shift).astype(BF16)
    lane = lax.broadcasted_iota(jnp.int32, (TMD, LANE), 1)
    lo = lane < HALF

    def mm(lo_col, hi_col):
        return _dot_nt(hb, wp_ref[lo_col:hi_col, :])

    r = mm(P_QKV, P_QKV + A_COLS)
    tiles = [r[:, t * LANE:(t + 1) * LANE] for t in range(A_COLS // LANE)]
    if cache_seq:
        to_cache(ck_ref, tiles[4])
        to_cache(cv_ref, tiles[6])
    if rope:
        ca, sa = ca_ref[...], sa_ref[...]
        tiles[:6] = [_rope(t, ca, sa) for t in tiles[:6]]
    for t in range(4):
        q16 = (tiles[t] * (HD_A ** -0.5 * LOG2E)).astype(BF16)
        qa_ref[:, t * LANE:(t + 1) * LANE] = q16
        if rope:
            sq = jnp.square(q16.astype(F32))
            qnorm_ref[:, N_QA + 2 * t:N_QA + 2 * t + 1] = jnp.sqrt(jnp.sum(jnp.where(lo, sq, 0.0), -1, keepdims=True))
            qnorm_ref[:, N_QA + 2 * t + 1:N_QA + 2 * t + 2] = jnp.sqrt(jnp.sum(jnp.where(lo, 0.0, sq), -1, keepdims=True))
    k01, k10 = tiles[4], tiles[5]
    k0_16 = jnp.where(lo, k01, 0.0).astype(BF16)
    k1_16 = jnp.where(lo, k10, 0.0).astype(BF16)
    ka_ref[:, 0 * LANE:1 * LANE] = k0_16
    ka_ref[:, 1 * LANE:2 * LANE] = jnp.where(lo, 0.0, k10).astype(BF16)
    ka_ref[:, 2 * LANE:3 * LANE] = k1_16
    ka_ref[:, 3 * LANE:4 * LANE] = jnp.where(lo, 0.0, k01).astype(BF16)
    if rope:
        qnorm_ref[:, N_KA:N_KA + 1] = _row_norm(k0_16)
        qnorm_ref[:, N_KA + 1:N_KA + 2] = _row_norm(k1_16)
        qnorm_ref[:, N_KA + N_KV_A:] = jnp.zeros((TMD, NORM_COLS - N_KA - N_KV_A), F32)
    ones = jnp.ones((VT_ONES, TMD), BF16)
    vt = _dot_nt(wvat_ref[...], hb)
    for g in range(N_KV_A):
        vat_ref[g * VT_ROWS:g * VT_ROWS + HD_A] = vt[g * HD_A:(g + 1) * HD_A].astype(BF16)
        vat_ref[g * VT_ROWS + HD_A:(g + 1) * VT_ROWS] = ones

    for t in range(3):
        z_ref[:, t * 512:(t + 1) * 512] = mm(P_Z + t * 512, P_Z + (t + 1) * 512).astype(BF16)
    hb_ref[...] = hb

    r = mm(P_CQ, P_CQ + Q_RANK_B)
    qn = r * lax.rsqrt(jnp.mean(r * r, axis=-1, keepdims=True) + EPS) * qn_ref[...]
    q = _dot(qn.astype(BF16), wuq_ref[...])
    if rope:
        cb, sb = cb_ref[...], sb_ref[...]
        for h in range(N_HEADS_B):
            seg = (_rope(q[:, h * MLA_HW:(h + 1) * MLA_HW], cb, sb) * (MLA_SCALE * LOG2E)).astype(BF16)
            qb_ref[:, h * MLA_HW:(h + 1) * MLA_HW] = seg
            qnorm_ref[:, N_QB + h:N_QB + h + 1] = _row_norm(seg)
    else:
        qb_ref[...] = (q * (MLA_SCALE * LOG2E)).astype(BF16)

    r = mm(P_SMALL, P_SMALL + LANE)
    small_ref[...] = r
    if cache_seq:
        to_cache(ckpe_ref, r[:, S_KPE:S_KPE + QK_ROPE_B])
    kp = _rope(r, cb, sb) if rope else r
    kp = jnp.where((lane >= S_KPE) & (lane < S_KPE + QK_ROPE_B), kp, 0.0)

    r = mm(P_CKV, P_CKV + KV_RANK_B)
    cn = r * lax.rsqrt(jnp.mean(r * r, axis=-1, keepdims=True) + EPS) * kvn_ref[...]
    if cache_seq:
        to_cache(cckv_ref, cn)
    cn16 = cn.astype(BF16)
    kv = _dot(cn16, wukv_ref[...])
    vt = _dot_nt(wvbt_ref[...], cn16)
    for h in range(N_HEADS_B):
        k16 = (kv[:, h * MLA_HW:(h + 1) * MLA_HW] + kp).astype(BF16)
        kb_ref[:, h * MLA_HW:(h + 1) * MLA_HW] = k16
        if rope:
            qnorm_ref[:, N_KB + h:N_KB + h + 1] = _row_norm(k16)
        vbt_ref[h * VT_ROWS:h * VT_ROWS + V_HD_B] = vt[h * V_HD_B:(h + 1) * V_HD_B].astype(BF16)
        vbt_ref[h * VT_ROWS + V_HD_B:(h + 1) * VT_ROWS] = ones

    for t in range(3):
        cqkv_ref[:, t * 512:(t + 1) * 512] = mm(P_CQKV + t * 512, P_CQKV + (t + 1) * 512).astype(BF16)

    for c in range(TMD // CHUNK):
        abt_ref[c] = _dot_nt(wab_ref[...], hb[c * CHUNK:(c + 1) * CHUNK])


def _layer_spec(arr, l):
    nd = arr.ndim - 1
    return pl.BlockSpec((None,) + arr.shape[1:], lambda *_: (l,) + (0,) * nd, pipeline_mode=pl.Buffered(1))


def _inproj(x2d, l, mod, mod_row_fn, ng, weights, qn, kvn, rope_tabs, tiles_per_seq, cache=None):
    t = x2d.shape[0]
    nt = t // TMD
    rope = rope_tabs is not None
    row = lambda i: (i, 0)
    col = lambda i: (0, i)
    wp, wab, wvat, wuq, wukv, wvbt = weights
    params = (ng, wp, wab, wvat, wuq, wukv, wvbt, qn, kvn)
    in_specs = [pl.BlockSpec((TMD, D_MODEL), row),
                pl.BlockSpec((1, 1, 3 * D_MODEL), lambda i: (mod_row_fn(i), 0, 0))]
    in_specs += [_layer_spec(a, l) for a in params]
    args = [x2d, mod, *params]
    if rope:
        pos = lambda i: (i % tiles_per_seq, 0)
        in_specs += [pl.BlockSpec((TMD, LANE), pos)] * 4
        args += list(rope_tabs)
    outs = [(W_A, BF16, False), (KA_COLS, BF16, False), (VAT_ROWS, BF16, True), (W_Z, BF16, False),
            (D_MODEL, BF16, False), (HB_COLS, BF16, False), (HB_COLS, BF16, False), (VBT_ROWS, BF16, True),
            (LANE, F32, False), (QKV_C, BF16, False)]
    out_shape = [jax.ShapeDtypeStruct((w, t) if tr else (t, w), dt) for w, dt, tr in outs]
    out_specs = [pl.BlockSpec((w, TMD), col) if tr else pl.BlockSpec((TMD, w), row) for w, _, tr in outs]
    out_shape.append(jax.ShapeDtypeStruct((t // CHUNK, 16, CHUNK), F32))
    out_specs.append(pl.BlockSpec((TMD // CHUNK, 16, CHUNK), lambda i: (i, 0, 0)))
    if rope:
        out_shape.append(jax.ShapeDtypeStruct((t, NORM_COLS), F32))
        out_specs.append(pl.BlockSpec((TMD, NORM_COLS), row))
    aliases, cache_seq, n_carry = {}, 0, 0
    if cache is not None:
        depth, cache_seq, carry = cache
        spt = TMD // cache_seq
        for w in (N_KV_A * HD_A, N_KV_A * HD_A, KV_RANK_B, QK_ROPE_B):
            out_shape.append(jax.ShapeDtypeStruct((t // cache_seq, depth, cache_seq, w), F32))
            out_specs.append(pl.BlockSpec((spt, None, cache_seq, w), lambda i: (i, l, 0, 0)))
        n_carry = len(carry)
        for k, a in enumerate(carry):
            aliases[len(args)] = len(out_shape) - n_carry + k
            in_specs.append(pl.BlockSpec(memory_space=pl.ANY))
            args.append(a)
    return pl.pallas_call(
        functools.partial(_inproj_kernel, rope=rope, cache_seq=cache_seq, n_carry=n_carry),
        grid=(nt,),
        in_specs=in_specs,
        out_specs=out_specs,
        out_shape=out_shape,
        input_output_aliases=aliases,
        compiler_params=pltpu.CompilerParams(dimension_semantics=("parallel",)),
        name="inproj_ctx" if cache is not None else "inproj_lat",
    )(*args)


def _kvup_kernel(c_ref, kpe_ref, w_ref, wvt_ref, k_ref, vt_ref, kn_ref):
    c16 = c_ref[...].astype(BF16)
    kv = _dot(c16, w_ref[...])
    vt = _dot_nt(wvt_ref[...], c16)
    kp = kpe_ref[...]
    ones = jnp.ones((VT_ONES, c16.shape[0]), BF16)
    for h in range(N_HEADS_B):
        k16 = (kv[:, h * MLA_HW:(h + 1) * MLA_HW] + kp).astype(BF16)
        k_ref[:, h * MLA_HW:(h + 1) * MLA_HW] = k16
        kn_ref[:, h:h + 1] = _row_norm(k16)
        vt_ref[h * VT_ROWS:h * VT_ROWS + V_HD_B] = vt[h * V_HD_B:(h + 1) * V_HD_B].astype(BF16)
        vt_ref[h * VT_ROWS + V_HD_B:(h + 1) * VT_ROWS] = ones


def _kvup(ckv, kpe, wukv, wvbt, l):
    nb, _, past, _ = ckv.shape
    return pl.pallas_call(
        _kvup_kernel,
        grid=(nb,),
        in_specs=[pl.BlockSpec((None, None, past, KV_RANK_B), lambda b: (b, l, 0, 0)),
                  pl.BlockSpec((None, None, past, LANE), lambda b: (b, l, 0, 0)),
                  _layer_spec(wukv, l), _layer_spec(wvbt, l)],
        out_specs=[pl.BlockSpec((past, HB_COLS), lambda b: (b, 0)),
                   pl.BlockSpec((None, VBT_ROWS, past), lambda b: (b, 0, 0)),
                   pl.BlockSpec((past, N_HEADS_B), lambda b: (b, 0))],
        out_shape=[jax.ShapeDtypeStruct((nb * past, HB_COLS), BF16),
                   jax.ShapeDtypeStruct((nb, VBT_ROWS, past), BF16),
                   jax.ShapeDtypeStruct((nb * past, N_HEADS_B), F32)],
        name="mla_cache_up",
    )(ckv, kpe, wukv, wvbt)


def _scores_t(q_tiles, k_tiles, bias_t):
    st = _bdot_nt(jnp.stack(k_tiles), jnp.stack(q_tiles))
    return st if bias_t is None else st + bias_t[None]


def _softmax_pv(sts, vts, sink, shift=None):
    if shift is None:
        m = jnp.max(sts[0], axis=1, keepdims=True)
        for st in sts[1:]:
            m = jnp.maximum(m, jnp.max(st, axis=1, keepdims=True))
    else:
        m = shift
    if sink is not None:
        m = jnp.maximum(m, sink)
    ot = _bdot(jnp.stack(vts[0]), jnp.exp2(sts[0] - m).astype(BF16))
    for st, vt in zip(sts[1:], vts[1:]):
        ot = ot + _bdot(jnp.stack(vt), jnp.exp2(st - m).astype(BF16))
    den = ot[:, V_HD_B:V_HD_B + 1, :]
    if sink is not None:
        den = den + jnp.exp2(sink - m)
    num = ot[:, :V_HD_B, :] / den
    return [jnp.concatenate([num[2 * i], num[2 * i + 1]], axis=0).T for i in range(sts[0].shape[0] // 2)]


def _tile(x, t):
    return x[:, t * LANE:(t + 1) * LANE]


def _attn_a_heads(q, segments, sink_ref, shift=None):
    sts, vts = [], []
    for ka, vat, bias_t in segments:
        qs, ks, vs = [], [], []
        for t in range(N_HEADS_A // 2):
            g = (2 * t) // GQA_GROUP
            for e in range(2):
                qs.append(_tile(q, t))
                ks.append(_tile(ka, 2 * g + e))
                vs.append(vat[g * VT_ROWS:(g + 1) * VT_ROWS])
        sts.append(_scores_t(qs, ks, bias_t))
        vts.append(vs)
    sink = jnp.stack([sink_ref[:, h:h + 1] * LOG2E for h in range(N_HEADS_A)])
    return _softmax_pv(sts, vts, sink, shift)


def _gated_store(outs, z_ref, o_ref, first_tile=0):
    for i, o in enumerate(outs):
        t = first_tile + i
        z = _tile(z_ref, t).astype(F32)
        o_ref[:, t * LANE:(t + 1) * LANE] = (o * _silu(z)).astype(BF16)


def _attn_ctx_kernel(qa_ref, ka_ref, vat_ref, qb_ref, kb_ref, vbt_ref, za_ref, zb_ref, sink_ref, oa_ref, ob_ref):
    _gated_store(_attn_a_heads(qa_ref[...], [(ka_ref[...], vat_ref[...], None)], sink_ref), za_ref, oa_ref)
    q, kb, vbt = qb_ref[...], kb_ref[...], vbt_ref[...]
    heads = range(N_HEADS_B)
    st = _scores_t([_tile(q, h) for h in heads], [_tile(kb, h) for h in heads], None)
    _gated_store(_softmax_pv([st], [[vbt[h * VT_ROWS:(h + 1) * VT_ROWS] for h in heads]], None), zb_ref, ob_ref)


def _attn_ctx(qa, ka, vat, qb, kb, vbt, z, sink, l, seq):
    t = qa.shape[0]
    row = lambda b: (b, 0)
    col = lambda b: (0, b)
    return pl.pallas_call(
        _attn_ctx_kernel,
        grid=(t // seq,),
        in_specs=[pl.BlockSpec((seq, W_A), row),
                  pl.BlockSpec((seq, KA_COLS), row),
                  pl.BlockSpec((VAT_ROWS, seq), col),
                  pl.BlockSpec((seq, HB_COLS), row),
                  pl.BlockSpec((seq, HB_COLS), row),
                  pl.BlockSpec((VBT_ROWS, seq), col),
                  pl.BlockSpec((seq, W_A), row),
                  pl.BlockSpec((seq, W_B), lambda b: (b, 1)),
                  _layer_spec(sink, l)],
        out_specs=[pl.BlockSpec((seq, W_A), row), pl.BlockSpec((seq, W_B), row)],
        out_shape=[jax.ShapeDtypeStruct((t, W_A), BF16), jax.ShapeDtypeStruct((t, W_B), BF16)],
        compiler_params=pltpu.CompilerParams(dimension_semantics=("parallel",)),
        name="attn_ctx",
    )(qa, ka, vat, qb, kb, vbt, z, z, sink)


def _attn_a_lat_kernel(q_ref, qn_ref, ksn_ref, kp_ref, kc_ref, kn_ref, vp_ref, vc_ref, vn_ref, kx_ref, vx_ref,
                       z_ref, sink_ref, o_ref, knorm_scr, *, nq):
    j = pl.program_id(1)

    @pl.when(j == 0)
    def _():
        kmax = jnp.max(ksn_ref[:, N_KA:N_KA + N_KV_A], axis=0, keepdims=True)
        kxmax = [jnp.max(_row_norm(kx_ref[:, 2 * g * LANE:(2 * g + 1) * LANE]), axis=0, keepdims=True)
                 for g in range(N_KV_A)]
        knorm_scr[...] = jnp.broadcast_to(jnp.maximum(kmax, jnp.concatenate(kxmax, axis=1)), knorm_scr.shape)

    qmax = jnp.max(qn_ref[:, N_QA:N_QA + N_HEADS_A], axis=0, keepdims=True)
    bound = jnp.stack([qmax[:, h:h + 1] * knorm_scr[0:1, h // GQA_GROUP:h // GQA_GROUP + 1] * SHIFT_SLACK
                       + SHIFT_SLACK_ABS for h in range(N_HEADS_A)])
    bound_ok = jnp.max(bound) <= SHIFT_MAX

    ka = jnp.concatenate([kp_ref[...], kc_ref[...], kn_ref[...]], axis=0)
    vat = jnp.concatenate([vp_ref[...], vc_ref[...], vn_ref[...]], axis=1)
    kj = lax.broadcasted_iota(jnp.int32, (ka.shape[0], TM), 0)
    qi = lax.broadcasted_iota(jnp.int32, (ka.shape[0], TM), 1)
    ok = (kj >= qi) & (kj <= qi + 2 * WINDOW)
    ok = ok & ((kj >= WINDOW) | (j > 0)) & ((kj < TM + WINDOW) | (j < nq - 1))
    bias_t = jnp.where(ok, 0.0, NEG_INF)
    segments = [(ka, vat, bias_t), (kx_ref[...], vx_ref[...], None)]

    @pl.when(bound_ok)
    def _():
        _gated_store(_attn_a_heads(q_ref[...], segments, sink_ref, bound), z_ref, o_ref)

    @pl.when(jnp.logical_not(bound_ok))
    def _():
        _gated_store(_attn_a_heads(q_ref[...], segments, sink_ref), z_ref, o_ref)


def _attn_a_lat(qa, norms, ka, vat, z, sink, kx, vxt, l, seq):
    t = qa.shape[0]
    nq = seq // TM
    past = kx.shape[2]
    r = TM // WINDOW
    row = lambda b, j: (b * nq + j, 0)
    prev = lambda b, j: ((b * nq + j) * r - jnp.where(j > 0, 1, 0), 0)
    nxt = lambda b, j: ((b * nq + j) * r + jnp.where(j < nq - 1, r, r - 1), 0)
    swap = lambda f: (lambda b, j: f(b, j)[::-1])
    return pl.pallas_call(
        functools.partial(_attn_a_lat_kernel, nq=nq),
        grid=(t // seq, nq),
        in_specs=[pl.BlockSpec((TM, W_A), row),
                  pl.BlockSpec((TM, NORM_COLS), row),
                  pl.BlockSpec((seq, NORM_COLS), lambda b, j: (b, 0)),
                  pl.BlockSpec((WINDOW, KA_COLS), prev),
                  pl.BlockSpec((TM, KA_COLS), row),
                  pl.BlockSpec((WINDOW, KA_COLS), nxt),
                  pl.BlockSpec((VAT_ROWS, WINDOW), swap(prev)),
                  pl.BlockSpec((VAT_ROWS, TM), swap(row)),
                  pl.BlockSpec((VAT_ROWS, WINDOW), swap(nxt)),
                  pl.BlockSpec((None, None, past, KA_COLS), lambda b, j: (b, l, 0, 0)),
                  pl.BlockSpec((None, None, VAT_ROWS, past), lambda b, j: (b, l, 0, 0)),
                  pl.BlockSpec((TM, W_A), row),
                  _layer_spec(sink, l)],
        out_specs=pl.BlockSpec((TM, W_A), row),
        out_shape=jax.ShapeDtypeStruct((t, W_A), BF16),
        scratch_shapes=[pltpu.VMEM((8, N_KV_A), F32)],
        compiler_params=pltpu.CompilerParams(dimension_semantics=("parallel", "arbitrary")),
        name="attn_a_lat",
    )(qa, norms, norms, ka, ka, ka, vat, vat, vat, kx, vxt, z, sink)


def _attn_b_lat_kernel(q_ref, qn_ref, kn_ref, knx_ref, k_ref, vt_ref, kx_ref, vxt_ref, z_ref, o_ref, knorm_scr, *,
                       group):
    segments = [(k_ref, vt_ref), (kx_ref, vxt_ref)]
    q = q_ref[...]

    @pl.when(pl.program_id(1) == 0)
    def _():
        kmax = jnp.maximum(jnp.max(kn_ref[:, N_KB:N_KB + N_HEADS_B], axis=0, keepdims=True),
                           jnp.max(knx_ref[...], axis=0, keepdims=True))
        knorm_scr[...] = jnp.broadcast_to(kmax, knorm_scr.shape)

    qmax = jnp.max(qn_ref[:, N_QB:N_QB + N_HEADS_B], axis=0, keepdims=True)
    bound = qmax * knorm_scr[0:1, :] * SHIFT_SLACK + SHIFT_SLACK_ABS
    bound = jnp.stack([bound[:, h:h + 1] for h in range(N_HEADS_B)])
    bound_ok = jnp.max(bound) <= SHIFT_MAX

    def scores(h0):
        heads = range(h0, h0 + group)
        return [_scores_t([_tile(q, h) for h in heads], [_tile(kr, h) for h in heads], None) for kr, _ in segments]

    def attend(shift):
        sts = scores(0)
        for h0 in range(0, N_HEADS_B, group):
            sts_next = scores(h0 + group) if h0 + group < N_HEADS_B else None
            vts = [[vr[h * VT_ROWS:(h + 1) * VT_ROWS, :] for h in range(h0, h0 + group)] for _, vr in segments]
            outs = _softmax_pv(sts, vts, None, None if shift is None else shift[h0:h0 + group])
            _gated_store(outs, z_ref, o_ref, h0 // 2)
            sts = sts_next

    @pl.when(bound_ok)
    def _():
        attend(bound)

    @pl.when(jnp.logical_not(bound_ok))
    def _():
        attend(None)


def _attn_b_lat(qb, norms, kb, vbt, z, kx, vxt, knx, seq, qblk, group):
    t = qb.shape[0]
    nq = seq // qblk
    past = kx.shape[0] // (t // seq)
    hw = HB_COLS
    return pl.pallas_call(
        functools.partial(_attn_b_lat_kernel, group=group),
        grid=(t // seq, nq),
        in_specs=[pl.BlockSpec((qblk, hw), lambda b, j: (b * nq + j, 0)),
                  pl.BlockSpec((qblk, NORM_COLS), lambda b, j: (b * nq + j, 0)),
                  pl.BlockSpec((seq, NORM_COLS), lambda b, j: (b, 0)),
                  pl.BlockSpec((past, N_HEADS_B), lambda b, j: (b, 0)),
                  pl.BlockSpec((seq, hw), lambda b, j: (b, 0)),
                  pl.BlockSpec((VBT_ROWS, seq), lambda b, j: (0, b)),
                  pl.BlockSpec((past, hw), lambda b, j: (b, 0)),
                  pl.BlockSpec((None, VBT_ROWS, past), lambda b, j: (b, 0, 0)),
                  pl.BlockSpec((qblk, W_B), lambda b, j: (b * nq + j, 1))],
        out_specs=pl.BlockSpec((qblk, W_B), lambda b, j: (b * nq + j, 0)),
        out_shape=jax.ShapeDtypeStruct((t, W_B), BF16),
        scratch_shapes=[pltpu.VMEM((8, N_HEADS_B), F32)],
        compiler_params=pltpu.CompilerParams(dimension_semantics=("parallel", "arbitrary")),
        name="attn_b_lat",
    )(qb, norms, norms, knx, kb, vbt, kx, vxt, z)


def _gdn_local_kernel(cq_ref, prev_ref, next_ref, small_ref, abt_ref, cw_ref, prow_ref, pcol_ref,
                      u_ref, w_ref, qg_ref, kd_ref, attn_ref, eg_ref, qkv_scr, gb_scr, *, seq):
    x = cq_ref[...].astype(F32)
    tiles_per_seq = seq // TMG
    tpos = pl.program_id(0) % tiles_per_seq
    prev_row = jnp.where(tpos > 0, prev_ref[...].astype(F32)[HALO - 1:HALO, :], 0.0)
    next_row = jnp.where(tpos < tiles_per_seq - 1, next_ref[...].astype(F32)[0:1, :], 0.0)
    rows = lax.broadcasted_iota(jnp.int32, (TMG, 1), 0)
    xm1 = jnp.where(rows == 0, prev_row, pltpu.roll(x, 1, 0))
    xp1 = jnp.where(rows == TMG - 1, next_row, pltpu.roll(x, TMG - 1, 0))
    cw = cw_ref[...]
    y = _silu(xm1 * cw[0:1] + x * cw[1:2] + xp1 * cw[2:3])
    nq = N_HEADS_C * DK_C
    for h in range(N_HEADS_C):
        qh = y[:, h * DK_C:(h + 1) * DK_C]
        kh = y[:, nq + h * DK_C:nq + (h + 1) * DK_C]
        qkv_scr[:, h * DK_C:(h + 1) * DK_C] = (
            qh * lax.rsqrt(jnp.sum(qh * qh, axis=-1, keepdims=True) + EPS) * (DK_C ** -0.5))
        qkv_scr[:, nq + h * DK_C:nq + (h + 1) * DK_C] = kh * lax.rsqrt(jnp.sum(kh * kh, axis=-1, keepdims=True) + EPS)
    qkv_scr[:, 2 * nq:] = y[:, 2 * nq:]

    sm = small_ref[...]
    prow = prow_ref[...]
    gb_scr[:, 0:8] = -jnp.exp(prow[0:1]) * _softplus(sm[:, S_A:S_A + 8] + prow[1:2])
    gb_scr[:, 8:16] = _sigmoid(sm[:, S_B:S_B + 8])
    pcol = pcol_ref[...]

    ri = lax.broadcasted_iota(jnp.int32, (CHUNK, LANE), 0)
    lane = lax.broadcasted_iota(jnp.int32, (CHUNK, LANE), 1)
    fwd = lane < CHUNK
    cj = lane & (CHUNK - 1)
    incl = (fwd & (ri >= cj)) | (~fwd & (ri <= cj))
    strict = (fwd & (ri > cj)) | (~fwd & (ri < cj))
    xor = ri ^ cj
    eye = (ri == cj).astype(F32)
    r2 = lax.broadcasted_iota(jnp.int32, (2 * CHUNK, LANE), 0)
    l2 = lax.broadcasted_iota(jnp.int32, (2 * CHUNK, LANE), 1)
    same_dir = (r2 < CHUNK) == (l2 < CHUNK)
    rs_ = lax.broadcasted_iota(jnp.int32, (CHUNK, CHUNK), 0)
    cs_ = lax.broadcasted_iota(jnp.int32, (CHUNK, CHUNK), 1)
    tril = (rs_ >= cs_).astype(F32)
    triu = (rs_ <= cs_).astype(F32)
    tri_rows = jnp.concatenate([triu, tril], axis=1)
    dup_rows = jnp.concatenate([(rs_ == cs_).astype(F32)] * 2, axis=1)

    def block_diag(x):
        return jnp.where(same_dir[None], jnp.concatenate([x, x], axis=1), 0.0).astype(BF16)

    lows, rhss, order = [], [], []

    def solve():
        low = jnp.stack(lows, axis=0)
        inv = eye[None] - jnp.where(xor[None] == 1, low, 0.0)
        b = 2
        while b < CHUNK:
            cpl = jnp.where((xor[None] >= b) & (xor[None] < 2 * b), low, 0.0)
            tmp = _bdot(cpl.astype(BF16), block_diag(inv))
            inv = inv - _bdot(inv.astype(BF16), block_diag(tmp))
            b *= 2
        scale_u, scale_w, vs, ks = (jnp.stack(a, axis=0) for a in zip(*rhss))
        u = _bdot((inv * scale_u).astype(BF16), vs)
        w = _bdot((inv * scale_w).astype(BF16), ks)
        for i, (rs, h) in enumerate(order):
            for d in range(2):
                cs = slice((d * N_HEADS_C + h) * DK_C, (d * N_HEADS_C + h + 1) * DK_C)
                u_ref[rs, cs] = u[i, :, d * DV_C:(d + 1) * DV_C].astype(BF16)
                w_ref[rs, cs] = w[i, :, d * DK_C:(d + 1) * DK_C].astype(BF16)

    for c in range(TMG // CHUNK):
        rs = slice(c * CHUNK, (c + 1) * CHUNK)
        gcol = gb_scr[rs, 0:8]
        bcol = gb_scr[rs, 8:16]
        abt = abt_ref[c]
        grow = -jnp.exp(pcol[:, 0:1]) * _softplus(abt[0:8] + pcol[:, 1:2])
        gc_f = _dot_exact(tril, gcol)
        gc_b = _dot_exact(triu, gcol)
        gr = _dot_exact(grow, tri_rows)
        br = _dot_exact(_sigmoid(abt[8:16]), dup_rows)
        for h in range(N_HEADS_C):
            hb_ = N_HEADS_C + h
            q = qkv_scr[rs, h * DK_C:(h + 1) * DK_C]
            k = qkv_scr[rs, nq + h * DK_C:nq + (h + 1) * DK_C]
            v = qkv_scr[rs, 2 * nq + h * DV_C:2 * nq + (h + 1) * DV_C]
            k16 = k.astype(BF16)
            kk16 = jnp.concatenate([k16, k16], axis=0)
            kk = _dot_nt(k16, kk16)
            qk = _dot_nt(q.astype(BF16), kk16)
            gcs = (gc_f[:, h:h + 1], gc_b[:, hb_:hb_ + 1])
            betas = (bcol[:, h:h + 1], bcol[:, hb_:hb_ + 1])
            gc2 = jnp.where(fwd, gcs[0], gcs[1])
            gr2 = jnp.where(fwd[0:1], gr[h:h + 1, :], gr[hb_:hb_ + 1, :])
            decay = jnp.where(incl, jnp.exp(jnp.where(incl, gc2 - gr2, 0.0)), 0.0)
            lows.append(jnp.where(strict, jnp.where(fwd, betas[0], betas[1]) * kk * decay, 0.0))
            attn_ref[rs, h * LANE:(h + 1) * LANE] = (qk * decay).astype(BF16)
            order.append((rs, h))
            br2 = jnp.where(fwd[0:1], br[h:h + 1, :], br[hb_:hb_ + 1, :])
            v16 = v.astype(BF16)
            zeros = jnp.zeros_like(v16)
            rhss.append((br2, br2 * jnp.exp(gr2),
                         jnp.concatenate([jnp.concatenate([v16, zeros], axis=1),
                                          jnp.concatenate([zeros, v16], axis=1)], axis=0),
                         jnp.concatenate([jnp.concatenate([k16, zeros], axis=1),
                                          jnp.concatenate([zeros, k16], axis=1)], axis=0)))
            for d in range(2):
                dh = d * N_HEADS_C + h
                gc = gcs[d]
                eg = jnp.exp(gc)
                g_last = gc[CHUNK - 1:CHUNK] if d == 0 else gc[0:1]
                cs = slice(dh * DK_C, (dh + 1) * DK_C)
                qg_ref[rs, cs] = (q * eg).astype(BF16)
                kd_ref[rs, cs] = (k * jnp.exp(g_last - gc)).astype(BF16)
                eg_ref[c, dh:dh + 1, :] = jnp.broadcast_to(jnp.exp(g_last), (1, LANE))
    solve()


def _gdn_local(cqkv, small, abt, conv_w, prow, pcol, l, seq):
    t = cqkv.shape[0]
    assert seq % TMG == 0
    nt = t // TMG
    nh8 = t // HALO
    cpg = TMG // CHUNK
    row = lambda i: (i, 0)
    dh = 2 * N_HEADS_C
    return pl.pallas_call(
        functools.partial(_gdn_local_kernel, seq=seq),
        grid=(nt,),
        in_specs=[pl.BlockSpec((TMG, QKV_C), row),
                  pl.BlockSpec((HALO, QKV_C), lambda i: (jnp.maximum(i * (TMG // HALO) - 1, 0), 0)),
                  pl.BlockSpec((HALO, QKV_C), lambda i: (jnp.minimum((i + 1) * (TMG // HALO), nh8 - 1), 0)),
                  pl.BlockSpec((TMG, LANE), row),
                  pl.BlockSpec((cpg, 16, CHUNK), lambda i: (i, 0, 0)),
                  _layer_spec(conv_w, l), _layer_spec(prow, l), _layer_spec(pcol, l)],
        out_specs=[pl.BlockSpec((TMG, dh * DV_C), row),
                   pl.BlockSpec((TMG, dh * DK_C), row),
                   pl.BlockSpec((TMG, dh * DK_C), row),
                   pl.BlockSpec((TMG, dh * DK_C), row),
                   pl.BlockSpec((TMG, dh * CHUNK), row),
                   pl.BlockSpec((cpg, dh, LANE), lambda i: (i, 0, 0))],
        out_shape=[jax.ShapeDtypeStruct((t, dh * DV_C), BF16),
                   jax.ShapeDtypeStruct((t, dh * DK_C), BF16),
                   jax.ShapeDtypeStruct((t, dh * DK_C), BF16),
                   jax.ShapeDtypeStruct((t, dh * DK_C), BF16),
                   jax.ShapeDtypeStruct((t, dh * CHUNK), BF16),
                   jax.ShapeDtypeStruct((t // CHUNK, dh, LANE), F32)],
        scratch_shapes=[pltpu.VMEM((TMG, QKV_C), F32), pltpu.VMEM((TMG, 16), F32)],
        compiler_params=pltpu.CompilerParams(dimension_semantics=("parallel",)),
        name="gdn_local",
    )(cqkv, cqkv, cqkv, small, abt, conv_w, prow, pcol)


def _gdn_scan_kernel(*refs, nt, ns, has_init, want_state, n_carry):
    it = iter(refs)
    ins = [[next(it) for _ in range(6)] for _ in range(2)]
    s0_ref = next(it) if has_init else None
    for _ in range(n_carry):
        next(it)
    o_refs = [next(it), next(it)]
    st_ref = next(it) if want_state else None
    s_scr = next(it)
    j = pl.program_id(1)
    nh = N_HEADS_C
    nst = ns * 2 * nh

    @pl.when(j == 0)
    def _():
        if has_init:
            s_scr[...] = s0_ref[...].reshape(nst, DK_C, DV_C)
        else:
            s_scr[...] = jnp.zeros_like(s_scr)

    for step in range(CPT):
        chunk = lambda d: step if d == 0 else CPT - 1 - step

        def gather(idx, width):
            return jnp.stack([ins[d][idx][s, chunk(d) * CHUNK:(chunk(d) + 1) * CHUNK, h * width:(h + 1) * width]
                              for s in range(ns) for d in range(2) for h in range(nh)])

        u, w, qg, kd, attn = gather(0, DV_C), gather(1, DK_C), gather(2, DK_C), gather(3, DK_C), gather(4, LANE)
        eg = jnp.stack([ins[d][5][s, chunk(d), d * nh + h:d * nh + h + 1, :]
                        for s in range(ns) for d in range(2) for h in range(nh)])
        st = s_scr[...]
        sb = st.astype(BF16)
        v_new = u.astype(F32) - _bdot(w, sb)
        vb = v_new.astype(BF16)
        zeros = jnp.zeros((CHUNK, DV_C), BF16)
        vb2 = jnp.stack([jnp.concatenate([vb[i], zeros] if (i // nh) % 2 == 0 else [zeros, vb[i]], axis=0)
                         for i in range(nst)])
        o = _bdot(qg, sb) + _bdot(attn, vb2)
        s_scr[...] = st * eg + _bdot_tn(kd, vb)
        for s in range(ns):
            for d in range(2):
                for h in range(nh):
                    o_refs[d][s, chunk(d) * CHUNK:(chunk(d) + 1) * CHUNK, h * DV_C:(h + 1) * DV_C] = (
                        o[(s * 2 + d) * nh + h].astype(BF16))

    if want_state:
        @pl.when(j == nt - 1)
        def _():
            st_ref[...] = s_scr[...].reshape(ns, 2, nh, DK_C, DV_C)


def _gdn_scan(u, w, qg, kd, attn, eg, s0, l, seq, state_out=None):
    t = u.shape[0]
    want_state = state_out is not None
    nt = seq // TM
    nb = t // seq
    ns = next(n for n in (SCAN_SEQS, 2, 1) if nb % n == 0)
    half = N_HEADS_C * DK_C
    has_init = s0 is not None
    by_seq = lambda a: a.reshape((nb, a.shape[0] // nb) + a.shape[1:])
    in_specs, args = [], []
    for d in range(2):
        tile = (lambda b, j: j) if d == 0 else (lambda b, j: nt - 1 - j)
        row = lambda b, j, d=d, tile=tile: (b, tile(b, j), d)
        row4 = lambda b, j, tile=tile: (b, tile(b, j), 0, 0)
        in_specs += [pl.BlockSpec((ns, TM, half), row)] * 4
        in_specs += [pl.BlockSpec((ns, TM, N_HEADS_C * LANE), lambda b, j, tile=tile: (b, tile(b, j), 0)),
                     pl.BlockSpec((ns, CPT, 2 * N_HEADS_C, LANE), row4)]
        args += [by_seq(a) for a in (u, w, qg, kd, attn, eg)]
    st_tail = (2, N_HEADS_C, DK_C, DV_C)
    if has_init:
        in_specs.append(pl.BlockSpec((ns, None) + st_tail, lambda b, j: (b, l, 0, 0, 0, 0)))
        args.append(s0)
    out_specs = [pl.BlockSpec((ns, TM, half), lambda b, j: (b, j, 0)),
                 pl.BlockSpec((ns, TM, half), lambda b, j: (b, nt - 1 - j, 0))]
    out_shape = [jax.ShapeDtypeStruct((nb, seq, half), BF16)] * 2
    aliases, n_carry = {}, 0
    if want_state:
        depth, carry = state_out
        out_specs.append(pl.BlockSpec((ns, None) + st_tail, lambda b, j: (b, l, 0, 0, 0, 0)))
        out_shape.append(jax.ShapeDtypeStruct((nb, depth) + st_tail, F32))
        n_carry = 1
        aliases[len(args)] = 2
        in_specs.append(pl.BlockSpec(memory_space=pl.ANY))
        args.append(carry)
    outs = pl.pallas_call(
        functools.partial(_gdn_scan_kernel, nt=nt, ns=ns, has_init=has_init, want_state=want_state,
                          n_carry=n_carry),
        grid=(nb // ns, nt),
        in_specs=in_specs,
        out_specs=out_specs,
        out_shape=out_shape,
        input_output_aliases=aliases,
        scratch_shapes=[pltpu.VMEM((ns * 2 * N_HEADS_C, DK_C, DV_C), F32)],
        compiler_params=pltpu.CompilerParams(dimension_semantics=("parallel", "arbitrary")),
        name="gdn_scan",
    )(*args)
    return [outs[0].reshape(t, half), outs[1].reshape(t, half)] + list(outs[2:])


def _merge_kernel(x_ref, mod_ref, oa_ref, ob_ref, cf_ref, cb_ref, zc_ref, hb_ref, wg_ref, gn_ref,
                  wa_ref, wb_ref, wc_ref, wo_ref, fg_ref, o_ref, *, last):
    oc = cf_ref[...].astype(F32) + cb_ref[...].astype(F32)
    zc = zc_ref[...].astype(F32)
    gn = gn_ref[...]
    parts = []
    for h in range(N_HEADS_C):
        hs = slice(h * DV_C, (h + 1) * DV_C)
        och = oc[:, hs]
        och = och * lax.rsqrt(jnp.mean(och * och, axis=-1, keepdims=True) + EPS) * gn
        parts.append((och * _silu(zc[:, hs])).astype(BF16))
    ocz = jnp.concatenate(parts, axis=-1)
    pa = _dot(oa_ref[...], wa_ref[...])
    pb = _dot(ob_ref[...], wb_ref[...])
    pc = _dot(ocz, wc_ref[...])
    hb = hb_ref[...]
    ga = _sigmoid(_dot_nt(hb, wg_ref[0:D_MODEL, :]))
    gb = _sigmoid(_dot_nt(hb, wg_ref[D_MODEL:2 * D_MODEL, :]))
    gc = _sigmoid(_dot_nt(hb, wg_ref[2 * D_MODEL:, :]))
    y = _dot((ga * pa + gb * pb + gc * pc).astype(BF16), wo_ref[...])
    gate = mod_ref[0][:, 2 * D_MODEL:]
    xo = x_ref[...] + gate * y
    if last:
        xo = xo * lax.rsqrt(jnp.mean(xo * xo, axis=-1, keepdims=True) + EPS) * fg_ref[...]
    o_ref[...] = xo


def _merge(x2d, l, mod, mod_row_fn, oa, ob, cf, cb, z, hb, wp, gn, wa, wb, wc, wo, fg, last):
    t = x2d.shape[0]
    row = lambda i: (i, 0)
    assert P_GATES == 0
    return pl.pallas_call(
        functools.partial(_merge_kernel, last=last),
        grid=(t // TMD,),
        in_specs=[pl.BlockSpec((TMD, D_MODEL), row),
                  pl.BlockSpec((1, 1, 3 * D_MODEL), lambda i: (mod_row_fn(i), 0, 0)),
                  pl.BlockSpec((TMD, W_A), row),
                  pl.BlockSpec((TMD, W_B), row),
                  pl.BlockSpec((TMD, W_C), row),
                  pl.BlockSpec((TMD, W_C), row),
                  pl.BlockSpec((TMD, W_C), lambda i: (i, 2)),
                  pl.BlockSpec((TMD, D_MODEL), row),
                  pl.BlockSpec((None, W_GATES, D_MODEL), lambda i: (l, 0, 0), pipeline_mode=pl.Buffered(1)),
                  _layer_spec(gn, l), _layer_spec(wa, l), _layer_spec(wb, l), _layer_spec(wc, l),
                  _layer_spec(wo, l),
                  pl.BlockSpec((1, D_MODEL), lambda i: (0, 0))],
        out_specs=pl.BlockSpec((TMD, D_MODEL), row),
        out_shape=jax.ShapeDtypeStruct((t, D_MODEL), F32),
        compiler_params=pltpu.CompilerParams(dimension_semantics=("parallel",)),
        name="merge",
    )(x2d, mod, oa, ob, cf, cb, z, hb, wp, gn, wa, wb, wc, wo, fg)


def _rope_tables(n_tokens, rot_dim):
    rows = n_tokens // GRID_W
    row = np.repeat(np.arange(rows), GRID_W).astype(np.float32)
    col = np.tile(np.arange(GRID_W), rows).astype(np.float32)
    n_pairs = rot_dim // 4
    inv = (np.float32(ROPE_BASE) ** (-np.arange(n_pairs, dtype=np.float32) / np.float32(n_pairs))).astype(np.float32)
    ang = np.concatenate([row[:, None] * inv, col[:, None] * inv], axis=-1)
    c, s = np.cos(ang), np.sin(ang)
    return np.repeat(c, 2, axis=-1), np.stack([-s, s], axis=-1).reshape(n_tokens, rot_dim)


def _in_offsets():
    o = [0]
    for n in IN_SIZES:
        o.append(o[-1] + n)
    return o


def _relayout_moves():
    o = _in_offsets()
    order = [(o[12], W_GATES), (o[0], W_A), (o[1], LANE), (o[1] + HD_A, HD_A), (o[1], HD_A), (o[2], LANE),
             (o[3], W_A), (o[7], W_B), (o[11], W_C), (o[4], Q_RANK_B), (o[5], KV_RANK_B),
             (None, S_KPE), (o[6], QK_ROPE_B), (o[9], 4 * N_HEADS_C), (None, LANE - S_B - 8), (o[8], QKV_C)]
    moves, dst = [], 0
    for src, n in order:
        moves.append((src, dst, n))
        dst += n
    assert dst == P_END
    return moves


def _relayout_kernel(w_ref, o_ref, wab_ref, wvat_ref):
    for src, dst, n in _relayout_moves():
        if src is None:
            o_ref[dst:dst + n, :] = jnp.zeros((n, o_ref.shape[1]), BF16)
        else:
            o_ref[dst:dst + n, :] = w_ref[src:src + n, :].astype(BF16)
    o = _in_offsets()
    wab_ref[...] = w_ref[o[9]:o[11], :].astype(BF16)
    wvat_ref[...] = w_ref[o[2]:o[3], :].astype(BF16)


def _relayout_w_in(w_in):
    w_t = jnp.swapaxes(w_in, 1, 2)
    depth, width, _ = w_t.shape
    cols = 128
    nab, nv = 4 * N_HEADS_C, N_KV_A * HD_A
    col = lambda l, i: (l, 0, i)
    return pl.pallas_call(
        _relayout_kernel,
        grid=(depth, D_MODEL // cols),
        in_specs=[pl.BlockSpec((None, width, cols), col)],
        out_specs=[pl.BlockSpec((None, P_END, cols), col),
                   pl.BlockSpec((None, nab, cols), col),
                   pl.BlockSpec((None, nv, cols), col)],
        out_shape=[jax.ShapeDtypeStruct((depth, P_END, D_MODEL), BF16),
                   jax.ShapeDtypeStruct((depth, nab, D_MODEL), BF16),
                   jax.ShapeDtypeStruct((depth, nv, D_MODEL), BF16)],
        name="w_in_relayout",
    )(w_t)


def _prep_weights(w_in, w_uq, w_ukv):
    depth = w_in.shape[0]
    wp, wab, wvat = _relayout_w_in(w_in)
    hd = QK_NOPE_B + QK_ROPE_B
    wuq = jnp.pad(w_uq.reshape(depth, Q_RANK_B, N_HEADS_B, hd), ((0, 0), (0, 0), (0, 0), (0, MLA_HW - hd)))
    wuq = wuq.reshape(depth, Q_RANK_B, HB_COLS).astype(BF16)
    kv = w_ukv.reshape(depth, KV_RANK_B, N_HEADS_B, QK_NOPE_B + V_HD_B)
    wk = jnp.pad(kv[..., :QK_NOPE_B], ((0, 0), (0, 0), (0, 0), (0, MLA_HW - QK_NOPE_B)))
    wukv = wk.reshape(depth, KV_RANK_B, HB_COLS).astype(BF16)
    wvbt = jnp.swapaxes(kv[..., QK_NOPE_B:].reshape(depth, KV_RANK_B, W_B), 1, 2).astype(BF16)
    return wp, wab, wvat, wuq, wukv, wvbt


def _cache_tiles_a(kx, vx):
    k0, k1 = kx[..., 0, :], kx[..., 1, :]
    z = jnp.zeros_like(k0)
    ka = jnp.concatenate([k0, z, z, k0, k1, z, z, k1], axis=-1).astype(BF16)
    vt = jnp.transpose(vx, (0, 1, 3, 4, 2))
    vt = jnp.concatenate([vt, jnp.ones(vt.shape[:3] + (VT_ONES, vt.shape[4]), vt.dtype)], axis=3)
    return ka, vt.reshape(vt.shape[:2] + (VAT_ROWS, vt.shape[4])).astype(BF16)


def kernel(x_prompt, x_sample, cache_attn_k, cache_attn_v, cache_mla_ckv, cache_mla_kpe, state_gdn, c, c_ctx,
           norm_g, w_ada, b_ada, w_in, attn_sink, mla_q_norm, mla_w_uq, mla_kv_norm, mla_w_ukv, gdn_conv,
           gdn_a_log, gdn_dt_bias, gdn_norm, w_branch_a, w_branch_b, w_branch_c, w_out, final_norm_g):
    depth = w_in.shape[0]
    nb_c, seq_c, _ = x_prompt.shape
    nb_l, seq_l, _ = x_sample.shape
    past = cache_attn_k.shape[2]
    assert P_END % LANE == 0 and seq_c % TM == 0 and seq_l % TMD == 0 and nb_l < 8 and TM == 2 * WINDOW
    assert (nb_c * seq_c) % TMD == 0 and TMD % TM == 0

    cond8 = jnp.zeros((8, D_MODEL), F32).at[:nb_l].set(c).at[nb_l].set(c_ctx)
    mod = _modulation(cond8, w_ada, b_ada).reshape(depth * 8, 1, 3 * D_MODEL)

    c_a, s_a = _rope_tables(seq_l, HD_A)
    c_b, s_b = _rope_tables(seq_l, QK_ROPE_B)
    pad_l, pad_r = S_KPE, LANE - S_KPE - QK_ROPE_B
    one, zero = np.ones((seq_l, 1), np.float32), np.zeros((seq_l, 1), np.float32)
    rope_tabs = tuple(jnp.asarray(a) for a in (
        np.tile(c_a, (1, LANE // HD_A)), np.tile(s_a, (1, LANE // HD_A)),
        np.concatenate([np.tile(one, (1, pad_l)), c_b, np.tile(one, (1, pad_r))], 1),
        np.concatenate([np.tile(zero, (1, pad_l)), s_b, np.tile(zero, (1, pad_r))], 1)))

    weights = _prep_weights(w_in, mla_w_uq, mla_w_ukv)
    wukv, wvbt = weights[4], weights[5]
    ng = norm_g.reshape(depth, 1, D_MODEL)
    qn = mla_q_norm.reshape(depth, 1, Q_RANK_B)
    kvn = mla_kv_norm.reshape(depth, 1, KV_RANK_B)
    sink = attn_sink.reshape(depth, 1, N_HEADS_A)
    prow = jnp.stack([gdn_a_log.reshape(depth, -1), gdn_dt_bias.reshape(depth, -1)], axis=1)
    pcol = jnp.swapaxes(prow, 1, 2)
    gn = gdn_norm.reshape(depth, 1, DV_C)
    wa, wb, wc, wo = (w.astype(BF16) for w in (w_branch_a, w_branch_b, w_branch_c, w_out))
    fg = final_norm_g.reshape(1, D_MODEL)
    kxa, vxa = _cache_tiles_a(cache_attn_k, cache_attn_v)
    kpex = jnp.pad(cache_mla_kpe, ((0, 0), (0, 0), (0, 0), (pad_l, pad_r)))

    tps_c, tps_l = seq_c // TM, seq_l // TM
    tpd_l = seq_l // TMD
    y_p = x_prompt.reshape(nb_c * seq_c, D_MODEL)
    y_s = x_sample.reshape(nb_l * seq_l, D_MODEL)
    new_cache = tuple(jnp.zeros((nb_c, depth, seq_c, w), F32)
                      for w in (N_KV_A * HD_A, N_KV_A * HD_A, KV_RANK_B, QK_ROPE_B))
    new_state = jnp.zeros((nb_c, depth, 2, N_HEADS_C, DK_C, DV_C), F32)
    for l in range(depth):
        last = l == depth - 1

        mod_row_c = lambda i, l=l: l * 8 + nb_l
        outs = _inproj(y_p, l, mod, mod_row_c, ng, weights, qn, kvn, None, tps_c, (depth, seq_c, new_cache))
        (qa, ka, vat, z, hb, qb, kb, vbt, small, cqkv, abt), new_cache = outs[:11], tuple(outs[11:])
        oa, ob = _attn_ctx(qa, ka, vat, qb, kb, vbt, z, sink, l, seq_c)
        u, w, qg, kd, attn, eg = _gdn_local(cqkv, small, abt, gdn_conv, prow, pcol, l, seq_c)
        cf, cb, new_state = _gdn_scan(u, w, qg, kd, attn, eg, None, l, seq_c, (depth, new_state))
        y_p = _merge(y_p, l, mod, mod_row_c, oa, ob, cf, cb, z, hb, weights[0], gn, wa, wb, wc, wo, fg, last)

        mod_row_l = lambda i, l=l: l * 8 + i // tpd_l
        (qa, ka, vat, z, hb, qb, kb, vbt, small, cqkv, abt, qnorm) = _inproj(
            y_s, l, mod, mod_row_l, ng, weights, qn, kvn, rope_tabs, tpd_l)
        oa = _attn_a_lat(qa, qnorm, ka, vat, z, sink, kxa, vxa, l, seq_l)
        kxb, vxb, knx = _kvup(cache_mla_ckv, kpex, wukv, wvbt, l)
        ob = _attn_b_lat(qb, qnorm, kb, vbt, z, kxb, vxb, knx, seq_l, TM, N_HEADS_B // 2)
        u, w, qg, kd, attn, eg = _gdn_local(cqkv, small, abt, gdn_conv, prow, pcol, l, seq_l)
        cf, cb = _gdn_scan(u, w, qg, kd, attn, eg, state_gdn, l, seq_l)
        y_s = _merge(y_s, l, mod, mod_row_l, oa, ob, cf, cb, z, hb, weights[0], gn, wa, wb, wc, wo, fg, last)

    new_k, new_v, new_ckv, new_kpe = new_cache
    kv_shape = (nb_c, depth, seq_c, N_KV_A, HD_A)
    return (y_p.reshape(nb_c, seq_c, D_MODEL), y_s.reshape(nb_l, seq_l, D_MODEL),
            new_k.reshape(kv_shape), new_v.reshape(kv_shape), new_ckv, new_kpe, new_state)
```

```python
import functools

import numpy as np
import jax
import jax.numpy as jnp
from jax import lax
from jax.experimental import pallas as pl
from jax.experimental.pallas import tpu as pltpu

F32 = jnp.float32
BF16 = jnp.bfloat16

D_MODEL = 1024
GRID_W = 64
ROPE_BASE = 10000.0
EPS = 1e-6
NEG_INF = -1e30
N_HEADS_A = 8
N_KV_A = 2
HD_A = 64
GQA_GROUP = N_HEADS_A // N_KV_A
WINDOW = 128
N_HEADS_B = 8
QK_NOPE_B = 64
QK_ROPE_B = 32
V_HD_B = 64
Q_RANK_B = 384
KV_RANK_B = 256
MLA_SCALE = (QK_NOPE_B + QK_ROPE_B) ** -0.5
N_HEADS_C = 4
DK_C = 128
DV_C = 128
CHUNK = 64
W_A = N_HEADS_A * HD_A
W_B = N_HEADS_B * V_HD_B
W_C = N_HEADS_C * DV_C
QKV_C = 2 * N_HEADS_C * DK_C + W_C
IN_SIZES = (W_A, N_KV_A * HD_A, N_KV_A * HD_A, W_A, Q_RANK_B, KV_RANK_B, QK_ROPE_B, W_B, QKV_C,
            2 * N_HEADS_C, 2 * N_HEADS_C, W_C, 3 * D_MODEL)

LANE = 128
HALF = LANE // 2
TM = 256
TMD = 512
TMG = 256
CPT = TM // CHUNK
SCAN_SEQS = 4
HALO = 16
MLA_HW = 128
KA_COLS = 4 * LANE
HB_COLS = N_HEADS_B * MLA_HW
VT_ONES = 16
VT_ROWS = V_HD_B + VT_ONES
VAT_ROWS = N_KV_A * VT_ROWS
VBT_ROWS = N_HEADS_B * VT_ROWS
LOG2E = 1.4426950408889634
SHIFT_MAX = 60.0
SHIFT_SLACK = 1.001
SHIFT_SLACK_ABS = 0.01
N_QB = 0
N_KB = N_QB + N_HEADS_B
N_QA = N_KB + N_HEADS_B
N_KA = N_QA + N_HEADS_A
NORM_COLS = 32

P_GATES = 0
W_GATES = 3 * D_MODEL
P_QKV = P_GATES + W_GATES
A_COLS = W_A + 3 * LANE
P_Z = P_QKV + A_COLS
W_Z = W_A + W_B + W_C
P_CQ = P_Z + W_Z
P_CKV = P_CQ + Q_RANK_B
P_SMALL = P_CKV + KV_RANK_B
P_CQKV = P_SMALL + LANE
P_END = P_CQKV + QKV_C
S_KPE = 64
S_A = 96
S_B = 104


def _sigmoid(x):
    return 0.5 * jnp.tanh(0.5 * x) + 0.5


def _silu(x):
    return x * _sigmoid(x)


def _softplus(x):
    return jnp.maximum(x, 0.0) + jnp.log(1.0 + jnp.exp(-jnp.abs(x)))


def _dot(a, b):
    return jnp.dot(a, b, preferred_element_type=F32)


def _dot_nt(a, b):
    return lax.dot_general(a, b, (((1,), (1,)), ((), ())), preferred_element_type=F32)


def _bdot(a, b):
    return lax.dot_general(a, b, (((2,), (1,)), ((0,), (0,))), preferred_element_type=F32)


def _bdot_nt(a, b):
    return lax.dot_general(a, b, (((2,), (2,)), ((0,), (0,))), preferred_element_type=F32)


def _bdot_tn(a, b):
    return lax.dot_general(a, b, (((1,), (1,)), ((0,), (0,))), preferred_element_type=F32)


def _dot_exact(a, b):
    return jnp.dot(a, b, preferred_element_type=F32, precision=lax.Precision.HIGHEST)


def _row_sumsq(x16):
    x = x16.astype(F32)
    return jnp.sum(x * x, axis=-1, keepdims=True)


def _row_norm(x16):
    return jnp.sqrt(_row_sumsq(x16))


def _rope(x, c, s):
    n = x.shape[-1]
    lane = lax.broadcasted_iota(jnp.int32, x.shape, 1)
    swapped = jnp.where(lane % 2 == 0, pltpu.roll(x, n - 1, 1), pltpu.roll(x, 1, 1))
    return x * c + swapped * s


def _mod_kernel(cond_ref, w_ref, b_ref, out_ref):
    cnd = cond_ref[...]
    out_ref[0] = _dot(_silu(cnd).astype(BF16), w_ref[0].astype(BF16)) + b_ref[0]


def _modulation(cond8, w_ada, b_ada):
    depth = w_ada.shape[0]
    tn = 768
    return pl.pallas_call(
        _mod_kernel,
        grid=(depth, 3 * D_MODEL // tn),
        in_specs=[pl.BlockSpec((8, D_MODEL), lambda l, n: (0, 0)),
                  pl.BlockSpec((1, D_MODEL, tn), lambda l, n: (l, 0, n)),
                  pl.BlockSpec((1, 1, tn), lambda l, n: (l, 0, n))],
        out_specs=pl.BlockSpec((1, 8, tn), lambda l, n: (l, 0, n)),
        out_shape=jax.ShapeDtypeStruct((depth, 8, 3 * D_MODEL), F32),
        name="adaln_mod",
    )(cond8, w_ada, b_ada.reshape(depth, 1, 3 * D_MODEL))


def _inproj_kernel(*refs, rope, cache_seq, n_carry):
    it = iter(refs)
    (x_ref, mod_ref, ng_ref, wp_ref, wab_ref, wvat_ref, wuq_ref, wukv_ref, wvbt_ref, qn_ref,
     kvn_ref) = (next(it) for _ in range(11))
    if rope:
        ca_ref, sa_ref, cb_ref, sb_ref = (next(it) for _ in range(4))
    for _ in range(n_carry):
        next(it)
    (qa_ref, ka_ref, vat_ref, z_ref, hb_ref, qb_ref, kb_ref, vbt_ref, small_ref, cqkv_ref,
     abt_ref) = (next(it) for _ in range(11))
    if rope:
        qnorm_ref = next(it)
    if cache_seq:
        ck_ref, cv_ref, cckv_ref, ckpe_ref = (next(it) for _ in range(4))

    def to_cache(ref, val):
        for s in range(TMD // cache_seq):
            ref[s] = val[s * cache_seq:(s + 1) * cache_seq]

    x = x_ref[...]
    mod = mod_ref[0]
    shift, scale = mod[:, :D_MODEL], mod[:, D_MODEL:2 * D_MODEL]
    xn = x * lax.rsqrt(jnp.mean(x * x, axis=-1, keepdims=True) + EPS) * ng_ref[...]
    hb = (xn * (1.0 + scale) + shift).astype(BF16)
    lane = lax.broadcasted_iota(jnp.int32, (TMD, LANE), 1)
    lo = lane < HALF

    def mm(lo_col, hi_col):
        return _dot_nt(hb, wp_ref[lo_col:hi_col, :])

    r = mm(P_QKV, P_QKV + A_COLS)
    tiles = [r[:, t * LANE:(t + 1) * LANE] for t in range(A_COLS // LANE)]
    if cache_seq:
        to_cache(ck_ref, tiles[4])
        to_cache(cv_ref, tiles[6])
    if rope:
        ca, sa = ca_ref[...], sa_ref[...]
        tiles[:6] = [_rope(t, ca, sa) for t in tiles[:6]]
    for t in range(4):
        q16 = (tiles[t] * (HD_A ** -0.5 * LOG2E)).astype(BF16)
        qa_ref[:, t * LANE:(t + 1) * LANE] = q16
        if rope:
            sq = jnp.square(q16.astype(F32))
            qnorm_ref[:, N_QA + 2 * t:N_QA + 2 * t + 1] = jnp.sum(jnp.where(lo, sq, 0.0), -1, keepdims=True)
            qnorm_ref[:, N_QA + 2 * t + 1:N_QA + 2 * t + 2] = jnp.sum(jnp.where(lo, 0.0, sq), -1, keepdims=True)
    k01, k10 = tiles[4], tiles[5]
    k0_16 = jnp.where(lo, k01, 0.0).astype(BF16)
    k1_16 = jnp.where(lo, k10, 0.0).astype(BF16)
    ka_ref[:, 0 * LANE:1 * LANE] = k0_16
    ka_ref[:, 1 * LANE:2 * LANE] = jnp.where(lo, 0.0, k10).astype(BF16)
    ka_ref[:, 2 * LANE:3 * LANE] = k1_16
    ka_ref[:, 3 * LANE:4 * LANE] = jnp.where(lo, 0.0, k01).astype(BF16)
    if rope:
        qnorm_ref[:, N_KA:N_KA + 1] = _row_sumsq(k0_16)
        qnorm_ref[:, N_KA + 1:N_KA + 2] = _row_sumsq(k1_16)
        qnorm_ref[:, N_KA + N_KV_A:] = jnp.zeros((TMD, NORM_COLS - N_KA - N_KV_A), F32)
    ones = jnp.ones((VT_ONES, TMD), BF16)
    vt = _dot_nt(wvat_ref[...], hb)
    for g in range(N_KV_A):
        vat_ref[g * VT_ROWS:g * VT_ROWS + HD_A] = vt[g * HD_A:(g + 1) * HD_A].astype(BF16)
        vat_ref[g * VT_ROWS + HD_A:(g + 1) * VT_ROWS] = ones

    for t in range(3):
        z_ref[:, t * 512:(t + 1) * 512] = mm(P_Z + t * 512, P_Z + (t + 1) * 512).astype(BF16)
    hb_ref[...] = hb

    r = mm(P_CQ, P_CQ + Q_RANK_B)
    qn = r * lax.rsqrt(jnp.mean(r * r, axis=-1, keepdims=True) + EPS) * qn_ref[...]
    q = _dot(qn.astype(BF16), wuq_ref[...])
    if rope:
        cb, sb = cb_ref[...], sb_ref[...]
        for h in range(N_HEADS_B):
            seg = (_rope(q[:, h * MLA_HW:(h + 1) * MLA_HW], cb, sb) * (MLA_SCALE * LOG2E)).astype(BF16)
            qb_ref[:, h * MLA_HW:(h + 1) * MLA_HW] = seg
            qnorm_ref[:, N_QB + h:N_QB + h + 1] = _row_sumsq(seg)
    else:
        qb_ref[...] = (q * (MLA_SCALE * LOG2E)).astype(BF16)

    r = mm(P_SMALL, P_SMALL + LANE)
    small_ref[...] = r
    if cache_seq:
        to_cache(ckpe_ref, r[:, S_KPE:S_KPE + QK_ROPE_B])
    kp = _rope(r, cb, sb) if rope else r
    kp = jnp.where((lane >= S_KPE) & (lane < S_KPE + QK_ROPE_B), kp, 0.0)

    r = mm(P_CKV, P_CKV + KV_RANK_B)
    cn = r * lax.rsqrt(jnp.mean(r * r, axis=-1, keepdims=True) + EPS) * kvn_ref[...]
    if cache_seq:
        to_cache(cckv_ref, cn)
    cn16 = cn.astype(BF16)
    kv = _dot(cn16, wukv_ref[...])
    vt = _dot_nt(wvbt_ref[...], cn16)
    for h in range(N_HEADS_B):
        k16 = (kv[:, h * MLA_HW:(h + 1) * MLA_HW] + kp).astype(BF16)
        kb_ref[:, h * MLA_HW:(h + 1) * MLA_HW] = k16
        if rope:
            qnorm_ref[:, N_KB + h:N_KB + h + 1] = _row_sumsq(k16)
        vbt_ref[h * VT_ROWS:h * VT_ROWS + V_HD_B] = vt[h * V_HD_B:(h + 1) * V_HD_B].astype(BF16)
        vbt_ref[h * VT_ROWS + V_HD_B:(h + 1) * VT_ROWS] = ones

    for t in range(3):
        cqkv_ref[:, t * 512:(t + 1) * 512] = mm(P_CQKV + t * 512, P_CQKV + (t + 1) * 512).astype(BF16)
    if rope:
        qnorm_ref[...] = jnp.sqrt(qnorm_ref[...])

    for c in range(TMD // CHUNK):
        abt_ref[c] = _dot_nt(wab_ref[...], hb[c * CHUNK:(c + 1) * CHUNK])


def _layer_spec(arr, l):
    nd = arr.ndim - 1
    return pl.BlockSpec((None,) + arr.shape[1:], lambda *_: (l,) + (0,) * nd, pipeline_mode=pl.Buffered(1))


def _inproj(x2d, l, mod, mod_row_fn, ng, weights, qn, kvn, rope_tabs, tiles_per_seq, cache=None):
    t = x2d.shape[0]
    nt = t // TMD
    rope = rope_tabs is not None
    row = lambda i: (i, 0)
    col = lambda i: (0, i)
    wp, wab, wvat, wuq, wukv, wvbt = weights
    params = (ng, wp, wab, wvat, wuq, wukv, wvbt, qn, kvn)
    in_specs = [pl.BlockSpec((TMD, D_MODEL), row),
                pl.BlockSpec((1, 1, 3 * D_MODEL), lambda i: (mod_row_fn(i), 0, 0))]
    in_specs += [_layer_spec(a, l) for a in params]
    args = [x2d, mod, *params]
    if rope:
        pos = lambda i: (i % tiles_per_seq, 0)
        in_specs += [pl.BlockSpec((TMD, LANE), pos)] * 4
        args += list(rope_tabs)
    outs = [(W_A, BF16, False), (KA_COLS, BF16, False), (VAT_ROWS, BF16, True), (W_Z, BF16, False),
            (D_MODEL, BF16, False), (HB_COLS, BF16, False), (HB_COLS, BF16, False), (VBT_ROWS, BF16, True),
            (LANE, F32, False), (QKV_C, BF16, False)]
    out_shape = [jax.ShapeDtypeStruct((w, t) if tr else (t, w), dt) for w, dt, tr in outs]
    out_specs = [pl.BlockSpec((w, TMD), col) if tr else pl.BlockSpec((TMD, w), row) for w, _, tr in outs]
    out_shape.append(jax.ShapeDtypeStruct((t // CHUNK, 16, CHUNK), F32))
    out_specs.append(pl.BlockSpec((TMD // CHUNK, 16, CHUNK), lambda i: (i, 0, 0)))
    if rope:
        out_shape.append(jax.ShapeDtypeStruct((t, NORM_COLS), F32))
        out_specs.append(pl.BlockSpec((TMD, NORM_COLS), row))
    aliases, cache_seq, n_carry = {}, 0, 0
    if cache is not None:
        depth, cache_seq, carry = cache
        spt = TMD // cache_seq
        for w in (N_KV_A * HD_A, N_KV_A * HD_A, KV_RANK_B, QK_ROPE_B):
            out_shape.append(jax.ShapeDtypeStruct((t // cache_seq, depth, cache_seq, w), F32))
            out_specs.append(pl.BlockSpec((spt, None, cache_seq, w), lambda i: (i, l, 0, 0)))
        n_carry = len(carry)
        for k, a in enumerate(carry):
            aliases[len(args)] = len(out_shape) - n_carry + k
            in_specs.append(pl.BlockSpec(memory_space=pl.ANY))
            args.append(a)
    return pl.pallas_call(
        functools.partial(_inproj_kernel, rope=rope, cache_seq=cache_seq, n_carry=n_carry),
        grid=(nt,),
        in_specs=in_specs,
        out_specs=out_specs,
        out_shape=out_shape,
        input_output_aliases=aliases,
        compiler_params=pltpu.CompilerParams(dimension_semantics=("parallel",)),
        name="inproj_ctx" if cache is not None else "inproj_lat",
    )(*args)


def _kvup_kernel(c_ref, kpe_ref, w_ref, wvt_ref, k_ref, vt_ref, kn_ref):
    c16 = c_ref[...].astype(BF16)
    kv = _dot(c16, w_ref[...])
    vt = _dot_nt(wvt_ref[...], c16)
    kp = kpe_ref[...]
    ones = jnp.ones((VT_ONES, c16.shape[0]), BF16)
    for h in range(N_HEADS_B):
        k16 = (kv[:, h * MLA_HW:(h + 1) * MLA_HW] + kp).astype(BF16)
        k_ref[:, h * MLA_HW:(h + 1) * MLA_HW] = k16
        kn_ref[:, h:h + 1] = _row_sumsq(k16)
        vt_ref[h * VT_ROWS:h * VT_ROWS + V_HD_B] = vt[h * V_HD_B:(h + 1) * V_HD_B].astype(BF16)
        vt_ref[h * VT_ROWS + V_HD_B:(h + 1) * VT_ROWS] = ones
    kn_ref[...] = jnp.sqrt(kn_ref[...])


def _kvup(ckv, kpe, wukv, wvbt, l):
    nb, _, past, _ = ckv.shape
    return pl.pallas_call(
        _kvup_kernel,
        grid=(nb,),
        in_specs=[pl.BlockSpec((None, None, past, KV_RANK_B), lambda b: (b, l, 0, 0)),
                  pl.BlockSpec((None, None, past, LANE), lambda b: (b, l, 0, 0)),
                  _layer_spec(wukv, l), _layer_spec(wvbt, l)],
        out_specs=[pl.BlockSpec((past, HB_COLS), lambda b: (b, 0)),
                   pl.BlockSpec((None, VBT_ROWS, past), lambda b: (b, 0, 0)),
                   pl.BlockSpec((past, N_HEADS_B), lambda b: (b, 0))],
        out_shape=[jax.ShapeDtypeStruct((nb * past, HB_COLS), BF16),
                   jax.ShapeDtypeStruct((nb, VBT_ROWS, past), BF16),
                   jax.ShapeDtypeStruct((nb * past, N_HEADS_B), F32)],
        name="mla_cache_up",
    )(ckv, kpe, wukv, wvbt)


def _scores_t(q_tiles, k_tiles, bias_t):
    st = _bdot_nt(jnp.stack(k_tiles), jnp.stack(q_tiles))
    return st if bias_t is None else st + bias_t[None]


def _softmax_pv(sts, vts, sink, shift=None):
    if shift is None:
        m = jnp.max(sts[0], axis=1, keepdims=True)
        for st in sts[1:]:
            m = jnp.maximum(m, jnp.max(st, axis=1, keepdims=True))
    else:
        m = shift
    if sink is not None:
        m = jnp.maximum(m, sink)
    ot = _bdot(jnp.stack(vts[0]), jnp.exp2(sts[0] - m).astype(BF16))
    for st, vt in zip(sts[1:], vts[1:]):
        ot = ot + _bdot(jnp.stack(vt), jnp.exp2(st - m).astype(BF16))
    den = ot[:, V_HD_B:V_HD_B + 1, :]
    if sink is not None:
        den = den + jnp.exp2(sink - m)
    num = ot[:, :V_HD_B, :] / den
    return [jnp.concatenate([num[2 * i], num[2 * i + 1]], axis=0).T for i in range(sts[0].shape[0] // 2)]


def _tile(x, t):
    return x[:, t * LANE:(t + 1) * LANE]


def _attn_a_heads(q, segments, sink_ref, shift=None):
    sts, vts = [], []
    for ka, vat, bias_t in segments:
        qs, ks, vs = [], [], []
        for t in range(N_HEADS_A // 2):
            g = (2 * t) // GQA_GROUP
            for e in range(2):
                qs.append(_tile(q, t))
                ks.append(_tile(ka, 2 * g + e))
                vs.append(vat[g * VT_ROWS:(g + 1) * VT_ROWS])
        sts.append(_scores_t(qs, ks, bias_t))
        vts.append(vs)
    sink = jnp.stack([sink_ref[:, h:h + 1] * LOG2E for h in range(N_HEADS_A)])
    return _softmax_pv(sts, vts, sink, shift)


def _gated_store(outs, z_ref, o_ref, first_tile=0):
    for i, o in enumerate(outs):
        t = first_tile + i
        z = _tile(z_ref, t).astype(F32)
        o_ref[:, t * LANE:(t + 1) * LANE] = (o * _silu(z)).astype(BF16)


def _attn_ctx_kernel(qa_ref, ka_ref, vat_ref, qb_ref, kb_ref, vbt_ref, za_ref, zb_ref, sink_ref, oa_ref, ob_ref):
    _gated_store(_attn_a_heads(qa_ref[...], [(ka_ref[...], vat_ref[...], None)], sink_ref), za_ref, oa_ref)
    q, kb, vbt = qb_ref[...], kb_ref[...], vbt_ref[...]
    heads = range(N_HEADS_B)
    st = _scores_t([_tile(q, h) for h in heads], [_tile(kb, h) for h in heads], None)
    _gated_store(_softmax_pv([st], [[vbt[h * VT_ROWS:(h + 1) * VT_ROWS] for h in heads]], None), zb_ref, ob_ref)


def _attn_ctx(qa, ka, vat, qb, kb, vbt, z, sink, l, seq):
    t = qa.shape[0]
    row = lambda b: (b, 0)
    col = lambda b: (0, b)
    return pl.pallas_call(
        _attn_ctx_kernel,
        grid=(t // seq,),
        in_specs=[pl.BlockSpec((seq, W_A), row),
                  pl.BlockSpec((seq, KA_COLS), row),
                  pl.BlockSpec((VAT_ROWS, seq), col),
                  pl.BlockSpec((seq, HB_COLS), row),
                  pl.BlockSpec((seq, HB_COLS), row),
                  pl.BlockSpec((VBT_ROWS, seq), col),
                  pl.BlockSpec((seq, W_A), row),
                  pl.BlockSpec((seq, W_B), lambda b: (b, 1)),
                  _layer_spec(sink, l)],
        out_specs=[pl.BlockSpec((seq, W_A), row), pl.BlockSpec((seq, W_B), row)],
        out_shape=[jax.ShapeDtypeStruct((t, W_A), BF16), jax.ShapeDtypeStruct((t, W_B), BF16)],
        compiler_params=pltpu.CompilerParams(dimension_semantics=("parallel",)),
        name="attn_ctx",
    )(qa, ka, vat, qb, kb, vbt, z, z, sink)


def _attn_a_lat_kernel(q_ref, qn_ref, ksn_ref, kp_ref, kc_ref, kn_ref, vp_ref, vc_ref, vn_ref, kx_ref, vx_ref,
                       z_ref, sink_ref, o_ref, knorm_scr, *, nq):
    j = pl.program_id(1)

    @pl.when(j == 0)
    def _():
        kmax = jnp.max(ksn_ref[:, N_KA:N_KA + N_KV_A], axis=0, keepdims=True)
        kxmax = [jnp.max(_row_norm(kx_ref[:, 2 * g * LANE:(2 * g + 1) * LANE]), axis=0, keepdims=True)
                 for g in range(N_KV_A)]
        knorm_scr[...] = jnp.broadcast_to(jnp.maximum(kmax, jnp.concatenate(kxmax, axis=1)), knorm_scr.shape)

    qmax = jnp.max(qn_ref[:, N_QA:N_QA + N_HEADS_A], axis=0, keepdims=True)
    bound = jnp.stack([qmax[:, h:h + 1] * knorm_scr[0:1, h // GQA_GROUP:h // GQA_GROUP + 1] * SHIFT_SLACK
                       + SHIFT_SLACK_ABS for h in range(N_HEADS_A)])
    bound_ok = jnp.max(bound) <= SHIFT_MAX

    ka = jnp.concatenate([kp_ref[...], kc_ref[...], kn_ref[...]], axis=0)
    vat = jnp.concatenate([vp_ref[...], vc_ref[...], vn_ref[...]], axis=1)
    kj = lax.broadcasted_iota(jnp.int32, (ka.shape[0], TM), 0)
    qi = lax.broadcasted_iota(jnp.int32, (ka.shape[0], TM), 1)
    ok = (kj >= qi) & (kj <= qi + 2 * WINDOW)
    ok = ok & ((kj >= WINDOW) | (j > 0)) & ((kj < TM + WINDOW) | (j < nq - 1))
    bias_t = jnp.where(ok, 0.0, NEG_INF)
    segments = [(ka, vat, bias_t), (kx_ref[...], vx_ref[...], None)]

    @pl.when(bound_ok)
    def _():
        _gated_store(_attn_a_heads(q_ref[...], segments, sink_ref, bound), z_ref, o_ref)

    @pl.when(jnp.logical_not(bound_ok))
    def _():
        _gated_store(_attn_a_heads(q_ref[...], segments, sink_ref), z_ref, o_ref)


def _attn_a_lat(qa, norms, ka, vat, z, sink, kx, vxt, l, seq):
    t = qa.shape[0]
    nq = seq // TM
    past = kx.shape[2]
    r = TM // WINDOW
    row = lambda b, j: (b * nq + j, 0)
    prev = lambda b, j: ((b * nq + j) * r - jnp.where(j > 0, 1, 0), 0)
    nxt = lambda b, j: ((b * nq + j) * r + jnp.where(j < nq - 1, r, r - 1), 0)
    swap = lambda f: (lambda b, j: f(b, j)[::-1])
    return pl.pallas_call(
        functools.partial(_attn_a_lat_kernel, nq=nq),
        grid=(t // seq, nq),
        in_specs=[pl.BlockSpec((TM, W_A), row),
                  pl.BlockSpec((TM, NORM_COLS), row),
                  pl.BlockSpec((seq, NORM_COLS), lambda b, j: (b, 0)),
                  pl.BlockSpec((WINDOW, KA_COLS), prev),
                  pl.BlockSpec((TM, KA_COLS), row),
                  pl.BlockSpec((WINDOW, KA_COLS), nxt),
                  pl.BlockSpec((VAT_ROWS, WINDOW), swap(prev)),
                  pl.BlockSpec((VAT_ROWS, TM), swap(row)),
                  pl.BlockSpec((VAT_ROWS, WINDOW), swap(nxt)),
                  pl.BlockSpec((None, None, past, KA_COLS), lambda b, j: (b, l, 0, 0)),
                  pl.BlockSpec((None, None, VAT_ROWS, past), lambda b, j: (b, l, 0, 0)),
                  pl.BlockSpec((TM, W_A), row),
                  _layer_spec(sink, l)],
        out_specs=pl.BlockSpec((TM, W_A), row),
        out_shape=jax.ShapeDtypeStruct((t, W_A), BF16),
        scratch_shapes=[pltpu.VMEM((8, N_KV_A), F32)],
        compiler_params=pltpu.CompilerParams(dimension_semantics=("parallel", "arbitrary")),
        name="attn_a_lat",
    )(qa, norms, norms, ka, ka, ka, vat, vat, vat, kx, vxt, z, sink)


def _attn_b_lat_kernel(q_ref, qn_ref, kn_ref, knx_ref, k_ref, vt_ref, kx_ref, vxt_ref, z_ref, o_ref, knorm_scr, *,
                       group):
    segments = [(k_ref, vt_ref), (kx_ref, vxt_ref)]
    q = q_ref[...]

    @pl.when(pl.program_id(1) == 0)
    def _():
        kmax = jnp.maximum(jnp.max(kn_ref[:, N_KB:N_KB + N_HEADS_B], axis=0, keepdims=True),
                           jnp.max(knx_ref[...], axis=0, keepdims=True))
        knorm_scr[...] = jnp.broadcast_to(kmax, knorm_scr.shape)

    qmax = jnp.max(qn_ref[:, N_QB:N_QB + N_HEADS_B], axis=0, keepdims=True)
    bound = qmax * knorm_scr[0:1, :] * SHIFT_SLACK + SHIFT_SLACK_ABS
    bound = jnp.stack([bound[:, h:h + 1] for h in range(N_HEADS_B)])
    bound_ok = jnp.max(bound) <= SHIFT_MAX

    def scores(h0):
        heads = range(h0, h0 + group)
        return [_scores_t([_tile(q, h) for h in heads], [_tile(kr, h) for h in heads], None) for kr, _ in segments]

    def attend(shift):
        sts = scores(0)
        for h0 in range(0, N_HEADS_B, group):
            sts_next = scores(h0 + group) if h0 + group < N_HEADS_B else None
            vts = [[vr[h * VT_ROWS:(h + 1) * VT_ROWS, :] for h in range(h0, h0 + group)] for _, vr in segments]
            outs = _softmax_pv(sts, vts, None, None if shift is None else shift[h0:h0 + group])
            _gated_store(outs, z_ref, o_ref, h0 // 2)
            sts = sts_next

    @pl.when(bound_ok)
    def _():
        attend(bound)

    @pl.when(jnp.logical_not(bound_ok))
    def _():
        attend(None)


def _attn_b_lat(qb, norms, kb, vbt, z, kx, vxt, knx, seq, qblk, group):
    t = qb.shape[0]
    nq = seq // qblk
    past = kx.shape[0] // (t // seq)
    hw = HB_COLS
    return pl.pallas_call(
        functools.partial(_attn_b_lat_kernel, group=group),
        grid=(t // seq, nq),
        in_specs=[pl.BlockSpec((qblk, hw), lambda b, j: (b * nq + j, 0)),
                  pl.BlockSpec((qblk, NORM_COLS), lambda b, j: (b * nq + j, 0)),
                  pl.BlockSpec((seq, NORM_COLS), lambda b, j: (b, 0)),
                  pl.BlockSpec((past, N_HEADS_B), lambda b, j: (b, 0)),
                  pl.BlockSpec((seq, hw), lambda b, j: (b, 0)),
                  pl.BlockSpec((VBT_ROWS, seq), lambda b, j: (0, b)),
                  pl.BlockSpec((past, hw), lambda b, j: (b, 0)),
                  pl.BlockSpec((None, VBT_ROWS, past), lambda b, j: (b, 0, 0)),
                  pl.BlockSpec((qblk, W_B), lambda b, j: (b * nq + j, 1))],
        out_specs=pl.BlockSpec((qblk, W_B), lambda b, j: (b * nq + j, 0)),
        out_shape=jax.ShapeDtypeStruct((t, W_B), BF16),
        scratch_shapes=[pltpu.VMEM((8, N_HEADS_B), F32)],
        compiler_params=pltpu.CompilerParams(dimension_semantics=("parallel", "arbitrary")),
        name="attn_b_lat",
    )(qb, norms, norms, knx, kb, vbt, kx, vxt, z)


def _gdn_local_kernel(cq_ref, prev_ref, next_ref, small_ref, abt_ref, cw_ref, prow_ref, pcol_ref,
                      u_ref, w_ref, qg_ref, kd_ref, attn_ref, eg_ref, qkv_scr, gb_scr, *, seq):
    x = cq_ref[...].astype(F32)
    tiles_per_seq = seq // TMG
    tpos = pl.program_id(0) % tiles_per_seq
    prev_row = jnp.where(tpos > 0, prev_ref[...].astype(F32)[HALO - 1:HALO, :], 0.0)
    next_row = jnp.where(tpos < tiles_per_seq - 1, next_ref[...].astype(F32)[0:1, :], 0.0)
    rows = lax.broadcasted_iota(jnp.int32, (TMG, 1), 0)
    xm1 = jnp.where(rows == 0, prev_row, pltpu.roll(x, 1, 0))
    xp1 = jnp.where(rows == TMG - 1, next_row, pltpu.roll(x, TMG - 1, 0))
    cw = cw_ref[...]
    y = _silu(xm1 * cw[0:1] + x * cw[1:2] + xp1 * cw[2:3])
    nq = N_HEADS_C * DK_C
    for h in range(N_HEADS_C):
        qh = y[:, h * DK_C:(h + 1) * DK_C]
        kh = y[:, nq + h * DK_C:nq + (h + 1) * DK_C]
        qkv_scr[:, h * DK_C:(h + 1) * DK_C] = (
            qh * lax.rsqrt(jnp.sum(qh * qh, axis=-1, keepdims=True) + EPS) * (DK_C ** -0.5))
        qkv_scr[:, nq + h * DK_C:nq + (h + 1) * DK_C] = kh * lax.rsqrt(jnp.sum(kh * kh, axis=-1, keepdims=True) + EPS)
    qkv_scr[:, 2 * nq:] = y[:, 2 * nq:]

    sm = small_ref[...]
    prow = prow_ref[...]
    gb_scr[:, 0:8] = -jnp.exp(prow[0:1]) * _softplus(sm[:, S_A:S_A + 8] + prow[1:2])
    gb_scr[:, 8:16] = _sigmoid(sm[:, S_B:S_B + 8])
    pcol = pcol_ref[...]

    ri = lax.broadcasted_iota(jnp.int32, (CHUNK, LANE), 0)
    lane = lax.broadcasted_iota(jnp.int32, (CHUNK, LANE), 1)
    fwd = lane < CHUNK
    cj = lane & (CHUNK - 1)
    incl = (fwd & (ri >= cj)) | (~fwd & (ri <= cj))
    strict = (fwd & (ri > cj)) | (~fwd & (ri < cj))
    xor = ri ^ cj
    eye = (ri == cj).astype(F32)
    r2 = lax.broadcasted_iota(jnp.int32, (2 * CHUNK, LANE), 0)
    l2 = lax.broadcasted_iota(jnp.int32, (2 * CHUNK, LANE), 1)
    same_dir = (r2 < CHUNK) == (l2 < CHUNK)
    rs_ = lax.broadcasted_iota(jnp.int32, (CHUNK, CHUNK), 0)
    cs_ = lax.broadcasted_iota(jnp.int32, (CHUNK, CHUNK), 1)
    tril = (rs_ >= cs_).astype(F32)
    triu = (rs_ <= cs_).astype(F32)
    tri_rows = jnp.concatenate([triu, tril], axis=1)
    dup_rows = jnp.concatenate([(rs_ == cs_).astype(F32)] * 2, axis=1)

    def block_diag(x):
        return jnp.where(same_dir[None], jnp.concatenate([x, x], axis=1), 0.0).astype(BF16)

    lows, rhss, order = [], [], []

    def solve():
        low = jnp.stack(lows, axis=0)
        inv = eye[None] - jnp.where(xor[None] == 1, low, 0.0)
        b = 2
        while b < CHUNK:
            cpl = jnp.where((xor[None] >= b) & (xor[None] < 2 * b), low, 0.0)
            tmp = _bdot(cpl.astype(BF16), block_diag(inv))
            inv = inv - _bdot(inv.astype(BF16), block_diag(tmp))
            b *= 2
        scale_u, scale_w, vs, ks = (jnp.stack(a, axis=0) for a in zip(*rhss))
        u = _bdot((inv * scale_u).astype(BF16), vs)
        w = _bdot((inv * scale_w).astype(BF16), ks)
        for i, (rs, h) in enumerate(order):
            for d in range(2):
                cs = slice((d * N_HEADS_C + h) * DK_C, (d * N_HEADS_C + h + 1) * DK_C)
                u_ref[rs, cs] = u[i, :, d * DV_C:(d + 1) * DV_C].astype(BF16)
                w_ref[rs, cs] = w[i, :, d * DK_C:(d + 1) * DK_C].astype(BF16)

    for c in range(TMG // CHUNK):
        rs = slice(c * CHUNK, (c + 1) * CHUNK)
        gcol = gb_scr[rs, 0:8]
        bcol = gb_scr[rs, 8:16]
        abt = abt_ref[c]
        grow = -jnp.exp(pcol[:, 0:1]) * _softplus(abt[0:8] + pcol[:, 1:2])
        gc_f = _dot_exact(tril, gcol)
        gc_b = _dot_exact(triu, gcol)
        gr = _dot_exact(grow, tri_rows)
        br = _dot_exact(_sigmoid(abt[8:16]), dup_rows)
        for h in range(N_HEADS_C):
            hb_ = N_HEADS_C + h
            q = qkv_scr[rs, h * DK_C:(h + 1) * DK_C]
            k = qkv_scr[rs, nq + h * DK_C:nq + (h + 1) * DK_C]
            v = qkv_scr[rs, 2 * nq + h * DV_C:2 * nq + (h + 1) * DV_C]
            k16 = k.astype(BF16)
            kk16 = jnp.concatenate([k16, k16], axis=0)
            kk = _dot_nt(k16, kk16)
            qk = _dot_nt(q.astype(BF16), kk16)
            gcs = (gc_f[:, h:h + 1], gc_b[:, hb_:hb_ + 1])
            betas = (bcol[:, h:h + 1], bcol[:, hb_:hb_ + 1])
            gc2 = jnp.where(fwd, gcs[0], gcs[1])
            gr2 = jnp.where(fwd[0:1], gr[h:h + 1, :], gr[hb_:hb_ + 1, :])
            decay = jnp.where(incl, jnp.exp(jnp.where(incl, gc2 - gr2, 0.0)), 0.0)
            lows.append(jnp.where(strict, jnp.where(fwd, betas[0], betas[1]) * kk * decay, 0.0))
            attn_ref[rs, h * LANE:(h + 1) * LANE] = (qk * decay).astype(BF16)
            order.append((rs, h))
            br2 = jnp.where(fwd[0:1], br[h:h + 1, :], br[hb_:hb_ + 1, :])
            v16 = v.astype(BF16)
            zeros = jnp.zeros_like(v16)
            rhss.append((br2, br2 * jnp.exp(gr2),
                         jnp.concatenate([jnp.concatenate([v16, zeros], axis=1),
                                          jnp.concatenate([zeros, v16], axis=1)], axis=0),
                         jnp.concatenate([jnp.concatenate([k16, zeros], axis=1),
                                          jnp.concatenate([zeros, k16], axis=1)], axis=0)))
            for d in range(2):
                dh = d * N_HEADS_C + h
                gc = gcs[d]
                eg = jnp.exp(gc)
                g_last = gc[CHUNK - 1:CHUNK] if d == 0 else gc[0:1]
                cs = slice(dh * DK_C, (dh + 1) * DK_C)
                qg_ref[rs, cs] = (q * eg).astype(BF16)
                kd_ref[rs, cs] = (k * jnp.exp(g_last - gc)).astype(BF16)
                eg_ref[c, dh:dh + 1, :] = jnp.broadcast_to(jnp.exp(g_last), (1, LANE))
    solve()


def _gdn_local(cqkv, small, abt, conv_w, prow, pcol, l, seq):
    t = cqkv.shape[0]
    assert seq % TMG == 0
    nt = t // TMG
    nh8 = t // HALO
    cpg = TMG // CHUNK
    row = lambda i: (i, 0)
    dh = 2 * N_HEADS_C
    return pl.pallas_call(
        functools.partial(_gdn_local_kernel, seq=seq),
        grid=(nt,),
        in_specs=[pl.BlockSpec((TMG, QKV_C), row),
                  pl.BlockSpec((HALO, QKV_C), lambda i: (jnp.maximum(i * (TMG // HALO) - 1, 0), 0)),
                  pl.BlockSpec((HALO, QKV_C), lambda i: (jnp.minimum((i + 1) * (TMG // HALO), nh8 - 1), 0)),
                  pl.BlockSpec((TMG, LANE), row),
                  pl.BlockSpec((cpg, 16, CHUNK), lambda i: (i, 0, 0)),
                  _layer_spec(conv_w, l), _layer_spec(prow, l), _layer_spec(pcol, l)],
        out_specs=[pl.BlockSpec((TMG, dh * DV_C), row),
                   pl.BlockSpec((TMG, dh * DK_C), row),
                   pl.BlockSpec((TMG, dh * DK_C), row),
                   pl.BlockSpec((TMG, dh * DK_C), row),
                   pl.BlockSpec((TMG, dh * CHUNK), row),
                   pl.BlockSpec((cpg, dh, LANE), lambda i: (i, 0, 0))],
        out_shape=[jax.ShapeDtypeStruct((t, dh * DV_C), BF16),
                   jax.ShapeDtypeStruct((t, dh * DK_C), BF16),
                   jax.ShapeDtypeStruct((t, dh * DK_C), BF16),
                   jax.ShapeDtypeStruct((t, dh * DK_C), BF16),
                   jax.ShapeDtypeStruct((t, dh * CHUNK), BF16),
                   jax.ShapeDtypeStruct((t // CHUNK, dh, LANE), F32)],
        scratch_shapes=[pltpu.VMEM((TMG, QKV_C), F32), pltpu.VMEM((TMG, 16), F32)],
        compiler_params=pltpu.CompilerParams(dimension_semantics=("parallel",)),
        name="gdn_local",
    )(cqkv, cqkv, cqkv, small, abt, conv_w, prow, pcol)


def _gdn_scan_kernel(*refs, nt, ns, has_init, want_state, n_carry):
    it = iter(refs)
    ins = [[next(it) for _ in range(6)] for _ in range(2)]
    s0_ref = next(it) if has_init else None
    for _ in range(n_carry):
        next(it)
    o_refs = [next(it), next(it)]
    st_ref = next(it) if want_state else None
    s_scr = next(it)
    j = pl.program_id(1)
    nh = N_HEADS_C
    nst = ns * 2 * nh

    @pl.when(j == 0)
    def _():
        if has_init:
            s_scr[...] = s0_ref[...].reshape(nst, DK_C, DV_C)
        else:
            s_scr[...] = jnp.zeros_like(s_scr)

    for step in range(CPT):
        chunk = lambda d: step if d == 0 else CPT - 1 - step

        def gather(idx, width):
            return jnp.stack([ins[d][idx][s, chunk(d) * CHUNK:(chunk(d) + 1) * CHUNK, h * width:(h + 1) * width]
                              for s in range(ns) for d in range(2) for h in range(nh)])

        u, w, qg, kd, attn = gather(0, DV_C), gather(1, DK_C), gather(2, DK_C), gather(3, DK_C), gather(4, LANE)
        eg = jnp.stack([ins[d][5][s, chunk(d), d * nh + h:d * nh + h + 1, :]
                        for s in range(ns) for d in range(2) for h in range(nh)])
        st = s_scr[...]
        sb = st.astype(BF16)
        v_new = u.astype(F32) - _bdot(w, sb)
        vb = v_new.astype(BF16)
        zeros = jnp.zeros((CHUNK, DV_C), BF16)
        vb2 = jnp.stack([jnp.concatenate([vb[i], zeros] if (i // nh) % 2 == 0 else [zeros, vb[i]], axis=0)
                         for i in range(nst)])
        o = _bdot(qg, sb) + _bdot(attn, vb2)
        s_scr[...] = st * eg + _bdot_tn(kd, vb)
        for s in range(ns):
            for d in range(2):
                for h in range(nh):
                    o_refs[d][s, chunk(d) * CHUNK:(chunk(d) + 1) * CHUNK, h * DV_C:(h + 1) * DV_C] = (
                        o[(s * 2 + d) * nh + h].astype(BF16))

    if want_state:
        @pl.when(j == nt - 1)
        def _():
            st_ref[...] = s_scr[...].reshape(ns, 2, nh, DK_C, DV_C)


def _gdn_scan(u, w, qg, kd, attn, eg, s0, l, seq, state_out=None):
    t = u.shape[0]
    want_state = state_out is not None
    nt = seq // TM
    nb = t // seq
    ns = next(n for n in (SCAN_SEQS, 2, 1) if nb % n == 0)
    half = N_HEADS_C * DK_C
    has_init = s0 is not None
    by_seq = lambda a: a.reshape((nb, a.shape[0] // nb) + a.shape[1:])
    in_specs, args = [], []
    for d in range(2):
        tile = (lambda b, j: j) if d == 0 else (lambda b, j: nt - 1 - j)
        row = lambda b, j, d=d, tile=tile: (b, tile(b, j), d)
        row4 = lambda b, j, tile=tile: (b, tile(b, j), 0, 0)
        in_specs += [pl.BlockSpec((ns, TM, half), row)] * 4
        in_specs += [pl.BlockSpec((ns, TM, N_HEADS_C * LANE), lambda b, j, tile=tile: (b, tile(b, j), 0)),
                     pl.BlockSpec((ns, CPT, 2 * N_HEADS_C, LANE), row4)]
        args += [by_seq(a) for a in (u, w, qg, kd, attn, eg)]
    st_tail = (2, N_HEADS_C, DK_C, DV_C)
    if has_init:
        in_specs.append(pl.BlockSpec((ns, None) + st_tail, lambda b, j: (b, l, 0, 0, 0, 0)))
        args.append(s0)
    out_specs = [pl.BlockSpec((ns, TM, half), lambda b, j: (b, j, 0)),
                 pl.BlockSpec((ns, TM, half), lambda b, j: (b, nt - 1 - j, 0))]
    out_shape = [jax.ShapeDtypeStruct((nb, seq, half), BF16)] * 2
    aliases, n_carry = {}, 0
    if want_state:
        depth, carry = state_out
        out_specs.append(pl.BlockSpec((ns, None) + st_tail, lambda b, j: (b, l, 0, 0, 0, 0)))
        out_shape.append(jax.ShapeDtypeStruct((nb, depth) + st_tail, F32))
        n_carry = 1
        aliases[len(args)] = 2
        in_specs.append(pl.BlockSpec(memory_space=pl.ANY))
        args.append(carry)
    outs = pl.pallas_call(
        functools.partial(_gdn_scan_kernel, nt=nt, ns=ns, has_init=has_init, want_state=want_state,
                          n_carry=n_carry),
        grid=(nb // ns, nt),
        in_specs=in_specs,
        out_specs=out_specs,
        out_shape=out_shape,
        input_output_aliases=aliases,
        scratch_shapes=[pltpu.VMEM((ns * 2 * N_HEADS_C, DK_C, DV_C), F32)],
        compiler_params=pltpu.CompilerParams(dimension_semantics=("parallel", "arbitrary")),
        name="gdn_scan",
    )(*args)
    return [outs[0].reshape(t, half), outs[1].reshape(t, half)] + list(outs[2:])


def _merge_kernel(x_ref, mod_ref, oa_ref, ob_ref, cf_ref, cb_ref, zc_ref, hb_ref, wg_ref, gn_ref,
                  wa_ref, wb_ref, wc_ref, wo_ref, fg_ref, o_ref, *, last):
    oc = cf_ref[...].astype(F32) + cb_ref[...].astype(F32)
    zc = zc_ref[...].astype(F32)
    gn = gn_ref[...]
    parts = []
    for h in range(N_HEADS_C):
        hs = slice(h * DV_C, (h + 1) * DV_C)
        och = oc[:, hs]
        och = och * lax.rsqrt(jnp.mean(och * och, axis=-1, keepdims=True) + EPS) * gn
        parts.append((och * _silu(zc[:, hs])).astype(BF16))
    ocz = jnp.concatenate(parts, axis=-1)
    pa = _dot(oa_ref[...], wa_ref[...])
    pb = _dot(ob_ref[...], wb_ref[...])
    pc = _dot(ocz, wc_ref[...])
    hb = hb_ref[...]
    ga = _sigmoid(_dot_nt(hb, wg_ref[0:D_MODEL, :]))
    gb = _sigmoid(_dot_nt(hb, wg_ref[D_MODEL:2 * D_MODEL, :]))
    gc = _sigmoid(_dot_nt(hb, wg_ref[2 * D_MODEL:, :]))
    y = _dot((ga * pa + gb * pb + gc * pc).astype(BF16), wo_ref[...])
    gate = mod_ref[0][:, 2 * D_MODEL:]
    xo = x_ref[...] + gate * y
    if last:
        xo = xo * lax.rsqrt(jnp.mean(xo * xo, axis=-1, keepdims=True) + EPS) * fg_ref[...]
    o_ref[...] = xo


def _merge(x2d, l, mod, mod_row_fn, oa, ob, cf, cb, z, hb, wp, gn, wa, wb, wc, wo, fg, last):
    t = x2d.shape[0]
    row = lambda i: (i, 0)
    assert P_GATES == 0
    return pl.pallas_call(
        functools.partial(_merge_kernel, last=last),
        grid=(t // TMD,),
        in_specs=[pl.BlockSpec((TMD, D_MODEL), row),
                  pl.BlockSpec((1, 1, 3 * D_MODEL), lambda i: (mod_row_fn(i), 0, 0)),
                  pl.BlockSpec((TMD, W_A), row),
                  pl.BlockSpec((TMD, W_B), row),
                  pl.BlockSpec((TMD, W_C), row),
                  pl.BlockSpec((TMD, W_C), row),
                  pl.BlockSpec((TMD, W_C), lambda i: (i, 2)),
                  pl.BlockSpec((TMD, D_MODEL), row),
                  pl.BlockSpec((None, W_GATES, D_MODEL), lambda i: (l, 0, 0), pipeline_mode=pl.Buffered(1)),
                  _layer_spec(gn, l), _layer_spec(wa, l), _layer_spec(wb, l), _layer_spec(wc, l),
                  _layer_spec(wo, l),
                  pl.BlockSpec((1, D_MODEL), lambda i: (0, 0))],
        out_specs=pl.BlockSpec((TMD, D_MODEL), row),
        out_shape=jax.ShapeDtypeStruct((t, D_MODEL), F32),
        compiler_params=pltpu.CompilerParams(dimension_semantics=("parallel",)),
        name="merge",
    )(x2d, mod, oa, ob, cf, cb, z, hb, wp, gn, wa, wb, wc, wo, fg)


def _rope_tables(n_tokens, rot_dim):
    rows = n_tokens // GRID_W
    row = np.repeat(np.arange(rows), GRID_W).astype(np.float32)
    col = np.tile(np.arange(GRID_W), rows).astype(np.float32)
    n_pairs = rot_dim // 4
    inv = (np.float32(ROPE_BASE) ** (-np.arange(n_pairs, dtype=np.float32) / np.float32(n_pairs))).astype(np.float32)
    ang = np.concatenate([row[:, None] * inv, col[:, None] * inv], axis=-1)
    c, s = np.cos(ang), np.sin(ang)
    return np.repeat(c, 2, axis=-1), np.stack([-s, s], axis=-1).reshape(n_tokens, rot_dim)


def _in_offsets():
    o = [0]
    for n in IN_SIZES:
        o.append(o[-1] + n)
    return o


def _relayout_moves():
    o = _in_offsets()
    order = [(o[12], W_GATES), (o[0], W_A), (o[1], LANE), (o[1] + HD_A, HD_A), (o[1], HD_A), (o[2], LANE),
             (o[3], W_A), (o[7], W_B), (o[11], W_C), (o[4], Q_RANK_B), (o[5], KV_RANK_B),
             (None, S_KPE), (o[6], QK_ROPE_B), (o[9], 4 * N_HEADS_C), (None, LANE - S_B - 8), (o[8], QKV_C)]
    moves, dst = [], 0
    for src, n in order:
        moves.append((src, dst, n))
        dst += n
    assert dst == P_END
    return moves


def _relayout_kernel(w_ref, o_ref, wab_ref, wvat_ref):
    for src, dst, n in _relayout_moves():
        if src is None:
            o_ref[dst:dst + n, :] = jnp.zeros((n, o_ref.shape[1]), BF16)
        else:
            o_ref[dst:dst + n, :] = w_ref[src:src + n, :].astype(BF16)
    o = _in_offsets()
    wab_ref[...] = w_ref[o[9]:o[11], :].astype(BF16)
    wvat_ref[...] = w_ref[o[2]:o[3], :].astype(BF16)


def _relayout_w_in(w_in):
    w_t = jnp.swapaxes(w_in, 1, 2)
    depth, width, _ = w_t.shape
    cols = 128
    nab, nv = 4 * N_HEADS_C, N_KV_A * HD_A
    col = lambda l, i: (l, 0, i)
    return pl.pallas_call(
        _relayout_kernel,
        grid=(depth, D_MODEL // cols),
        in_specs=[pl.BlockSpec((None, width, cols), col)],
        out_specs=[pl.BlockSpec((None, P_END, cols), col),
                   pl.BlockSpec((None, nab, cols), col),
                   pl.BlockSpec((None, nv, cols), col)],
        out_shape=[jax.ShapeDtypeStruct((depth, P_END, D_MODEL), BF16),
                   jax.ShapeDtypeStruct((depth, nab, D_MODEL), BF16),
                   jax.ShapeDtypeStruct((depth, nv, D_MODEL), BF16)],
        name="w_in_relayout",
    )(w_t)


def _prep_weights(w_in, w_uq, w_ukv):
    depth = w_in.shape[0]
    wp, wab, wvat = _relayout_w_in(w_in)
    hd = QK_NOPE_B + QK_ROPE_B
    wuq = jnp.pad(w_uq.reshape(depth, Q_RANK_B, N_HEADS_B, hd), ((0, 0), (0, 0), (0, 0), (0, MLA_HW - hd)))
    wuq = wuq.reshape(depth, Q_RANK_B, HB_COLS).astype(BF16)
    kv = w_ukv.reshape(depth, KV_RANK_B, N_HEADS_B, QK_NOPE_B + V_HD_B)
    wk = jnp.pad(kv[..., :QK_NOPE_B], ((0, 0), (0, 0), (0, 0), (0, MLA_HW - QK_NOPE_B)))
    wukv = wk.reshape(depth, KV_RANK_B, HB_COLS).astype(BF16)
    wvbt = jnp.swapaxes(kv[..., QK_NOPE_B:].reshape(depth, KV_RANK_B, W_B), 1, 2).astype(BF16)
    return wp, wab, wvat, wuq, wukv, wvbt


def _cache_tiles_a(kx, vx):
    k0, k1 = kx[..., 0, :], kx[..., 1, :]
    z = jnp.zeros_like(k0)
    ka = jnp.concatenate([k0, z, z, k0, k1, z, z, k1], axis=-1).astype(BF16)
    vt = jnp.transpose(vx, (0, 1, 3, 4, 2))
    vt = jnp.concatenate([vt, jnp.ones(vt.shape[:3] + (VT_ONES, vt.shape[4]), vt.dtype)], axis=3)
    return ka, vt.reshape(vt.shape[:2] + (VAT_ROWS, vt.shape[4])).astype(BF16)


def kernel(x_prompt, x_sample, cache_attn_k, cache_attn_v, cache_mla_ckv, cache_mla_kpe, state_gdn, c, c_ctx,
           norm_g, w_ada, b_ada, w_in, attn_sink, mla_q_norm, mla_w_uq, mla_kv_norm, mla_w_ukv, gdn_conv,
           gdn_a_log, gdn_dt_bias, gdn_norm, w_branch_a, w_branch_b, w_branch_c, w_out, final_norm_g):
    depth = w_in.shape[0]
    nb_c, seq_c, _ = x_prompt.shape
    nb_l, seq_l, _ = x_sample.shape
    past = cache_attn_k.shape[2]
    assert P_END % LANE == 0 and seq_c % TM == 0 and seq_l % TMD == 0 and nb_l < 8 and TM == 2 * WINDOW
    assert (nb_c * seq_c) % TMD == 0 and TMD % TM == 0

    cond8 = jnp.zeros((8, D_MODEL), F32).at[:nb_l].set(c).at[nb_l].set(c_ctx)
    mod = _modulation(cond8, w_ada, b_ada).reshape(depth * 8, 1, 3 * D_MODEL)

    c_a, s_a = _rope_tables(seq_l, HD_A)
    c_b, s_b = _rope_tables(seq_l, QK_ROPE_B)
    pad_l, pad_r = S_KPE, LANE - S_KPE - QK_ROPE_B
    one, zero = np.ones((seq_l, 1), np.float32), np.zeros((seq_l, 1), np.float32)
    rope_tabs = tuple(jnp.asarray(a) for a in (
        np.tile(c_a, (1, LANE // HD_A)), np.tile(s_a, (1, LANE // HD_A)),
        np.concatenate([np.tile(one, (1, pad_l)), c_b, np.tile(one, (1, pad_r))], 1),
        np.concatenate([np.tile(zero, (1, pad_l)), s_b, np.tile(zero, (1, pad_r))], 1)))

    weights = _prep_weights(w_in, mla_w_uq, mla_w_ukv)
    wukv, wvbt = weights[4], weights[5]
    ng = norm_g.reshape(depth, 1, D_MODEL)
    qn = mla_q_norm.reshape(depth, 1, Q_RANK_B)
    kvn = mla_kv_norm.reshape(depth, 1, KV_RANK_B)
    sink = attn_sink.reshape(depth, 1, N_HEADS_A)
    prow = jnp.stack([gdn_a_log.reshape(depth, -1), gdn_dt_bias.reshape(depth, -1)], axis=1)
    pcol = jnp.swapaxes(prow, 1, 2)
    gn = gdn_norm.reshape(depth, 1, DV_C)
    wa, wb, wc, wo = (w.astype(BF16) for w in (w_branch_a, w_branch_b, w_branch_c, w_out))
    fg = final_norm_g.reshape(1, D_MODEL)
    kxa, vxa = _cache_tiles_a(cache_attn_k, cache_attn_v)
    kpex = jnp.pad(cache_mla_kpe, ((0, 0), (0, 0), (0, 0), (pad_l, pad_r)))

    tps_c, tps_l = seq_c // TM, seq_l // TM
    tpd_l = seq_l // TMD
    y_p = x_prompt.reshape(nb_c * seq_c, D_MODEL)
    y_s = x_sample.reshape(nb_l * seq_l, D_MODEL)
    new_cache = tuple(jnp.zeros((nb_c, depth, seq_c, w), F32)
                      for w in (N_KV_A * HD_A, N_KV_A * HD_A, KV_RANK_B, QK_ROPE_B))
    new_state = jnp.zeros((nb_c, depth, 2, N_HEADS_C, DK_C, DV_C), F32)
    for l in range(depth):
        last = l == depth - 1

        mod_row_c = lambda i, l=l: l * 8 + nb_l
        outs = _inproj(y_p, l, mod, mod_row_c, ng, weights, qn, kvn, None, tps_c, (depth, seq_c, new_cache))
        (qa, ka, vat, z, hb, qb, kb, vbt, small, cqkv, abt), new_cache = outs[:11], tuple(outs[11:])
        oa, ob = _attn_ctx(qa, ka, vat, qb, kb, vbt, z, sink, l, seq_c)
        u, w, qg, kd, attn, eg = _gdn_local(cqkv, small, abt, gdn_conv, prow, pcol, l, seq_c)
        cf, cb, new_state = _gdn_scan(u, w, qg, kd, attn, eg, None, l, seq_c, (depth, new_state))
        y_p = _merge(y_p, l, mod, mod_row_c, oa, ob, cf, cb, z, hb, weights[0], gn, wa, wb, wc, wo, fg, last)

        mod_row_l = lambda i, l=l: l * 8 + i // tpd_l
        (qa, ka, vat, z, hb, qb, kb, vbt, small, cqkv, abt, qnorm) = _inproj(
            y_s, l, mod, mod_row_l, ng, weights, qn, kvn, rope_tabs, tpd_l)
        oa = _attn_a_lat(qa, qnorm, ka, vat, z, sink, kxa, vxa, l, seq_l)
        kxb, vxb, knx = _kvup(cache_mla_ckv, kpex, wukv, wvbt, l)
        ob = _attn_b_lat(qb, qnorm, kb, vbt, z, kxb, vxb, knx, seq_l, TM, N_HEADS_B // 2)
        u, w, qg, kd, attn, eg = _gdn_local(cqkv, small, abt, gdn_conv, prow, pcol, l, seq_l)
        cf, cb = _gdn_scan(u, w, qg, kd, attn, eg, state_gdn, l, seq_l)
        y_s = _merge(y_s, l, mod, mod_row_l, oa, ob, cf, cb, z, hb, weights[0], gn, wa, wb, wc, wo, fg, last)

    new_k, new_v, new_ckv, new_kpe = new_cache
    kv_shape = (nb_c, depth, seq_c, N_KV_A, HD_A)
    return (y_p.reshape(nb_c, seq_c, D_MODEL), y_s.reshape(nb_l, seq_l, D_MODEL),
            new_k.reshape(kv_shape), new_v.reshape(kv_shape), new_ckv, new_kpe, new_state)
```

```python
import functools

import numpy as np
import jax
import jax.numpy as jnp
from jax import lax
from jax.experimental import pallas as pl
from jax.experimental.pallas import tpu as pltpu

F32 = jnp.float32
BF16 = jnp.bfloat16

D_MODEL = 1024
GRID_W = 64
ROPE_BASE = 10000.0
EPS = 1e-6
NEG_INF = -1e30
N_HEADS_A = 8
N_KV_A = 2
HD_A = 64
GQA_GROUP = N_HEADS_A // N_KV_A
WINDOW = 128
N_HEADS_B = 8
QK_NOPE_B = 64
QK_ROPE_B = 32
V_HD_B = 64
Q_RANK_B = 384
KV_RANK_B = 256
MLA_SCALE = (QK_NOPE_B + QK_ROPE_B) ** -0.5
N_HEADS_C = 4
DK_C = 128
DV_C = 128
CHUNK = 64
W_A = N_HEADS_A * HD_A
W_B = N_HEADS_B * V_HD_B
W_C = N_HEADS_C * DV_C
QKV_C = 2 * N_HEADS_C * DK_C + W_C
IN_SIZES = (W_A, N_KV_A * HD_A, N_KV_A * HD_A, W_A, Q_RANK_B, KV_RANK_B, QK_ROPE_B, W_B, QKV_C,
            2 * N_HEADS_C, 2 * N_HEADS_C, W_C, 3 * D_MODEL)

LANE = 128
HALF = LANE // 2
TM = 256
TMD = 512
TMG = 256
CPT = TM // CHUNK
SCAN_SEQS = 4
HALO = 16
MLA_HW = 128
KA_COLS = 4 * LANE
HB_COLS = N_HEADS_B * MLA_HW
VT_ONES = 16
VT_ROWS = V_HD_B + VT_ONES
VAT_ROWS = N_KV_A * VT_ROWS
VBT_ROWS = N_HEADS_B * VT_ROWS
LOG2E = 1.4426950408889634
SHIFT_MAX = 60.0
SHIFT_SLACK = 1.001
SHIFT_SLACK_ABS = 0.01
N_QB = 0
N_KB = N_QB + N_HEADS_B
N_QA = N_KB + N_HEADS_B
N_KA = N_QA + N_HEADS_A
NORM_COLS = 32

P_GATES = 0
W_GATES = 3 * D_MODEL
P_QKV = P_GATES + W_GATES
A_COLS = W_A + 3 * LANE
P_Z = P_QKV + A_COLS
W_Z = W_A + W_B + W_C
P_CQ = P_Z + W_Z
P_CKV = P_CQ + Q_RANK_B
P_SMALL = P_CKV + KV_RANK_B
P_CQKV = P_SMALL + LANE
P_END = P_CQKV + QKV_C
S_KPE = 64
S_A = 96
S_B = 104


def _sigmoid(x):
    return 0.5 * jnp.tanh(0.5 * x) + 0.5


def _silu(x):
    return x * _sigmoid(x)


def _softplus(x):
    return jnp.maximum(x, 0.0) + jnp.log(1.0 + jnp.exp(-jnp.abs(x)))


def _dot(a, b):
    return jnp.dot(a, b, preferred_element_type=F32)


def _dot_nt(a, b):
    return lax.dot_general(a, b, (((1,), (1,)), ((), ())), preferred_element_type=F32)


def _bdot(a, b):
    return lax.dot_general(a, b, (((2,), (1,)), ((0,), (0,))), preferred_element_type=F32)


def _bdot_nt(a, b):
    return lax.dot_general(a, b, (((2,), (2,)), ((0,), (0,))), preferred_element_type=F32)


def _bdot_tn(a, b):
    return lax.dot_general(a, b, (((1,), (1,)), ((0,), (0,))), preferred_element_type=F32)


def _dot_exact(a, b):
    return jnp.dot(a, b, preferred_element_type=F32, precision=lax.Precision.HIGHEST)


def _row_sumsq(x16):
    x = x16.astype(F32)
    return jnp.sum(x * x, axis=-1, keepdims=True)


def _row_norm(x16):
    return jnp.sqrt(_row_sumsq(x16))


def _rope(x, c, s):
    n = x.shape[-1]
    lane = lax.broadcasted_iota(jnp.int32, x.shape, 1)
    swapped = jnp.where(lane % 2 == 0, pltpu.roll(x, n - 1, 1), pltpu.roll(x, 1, 1))
    return x * c + swapped * s


def _mod_kernel(cond_ref, w_ref, b_ref, out_ref):
    cnd = cond_ref[...]
    out_ref[0] = _dot(_silu(cnd).astype(BF16), w_ref[0].astype(BF16)) + b_ref[0]


def _modulation(cond8, w_ada, b_ada):
    depth = w_ada.shape[0]
    tn = 768
    return pl.pallas_call(
        _mod_kernel,
        grid=(depth, 3 * D_MODEL // tn),
        in_specs=[pl.BlockSpec((8, D_MODEL), lambda l, n: (0, 0)),
                  pl.BlockSpec((1, D_MODEL, tn), lambda l, n: (l, 0, n)),
                  pl.BlockSpec((1, 1, tn), lambda l, n: (l, 0, n))],
        out_specs=pl.BlockSpec((1, 8, tn), lambda l, n: (l, 0, n)),
        out_shape=jax.ShapeDtypeStruct((depth, 8, 3 * D_MODEL), F32),
        name="adaln_mod",
    )(cond8, w_ada, b_ada.reshape(depth, 1, 3 * D_MODEL))


def _inproj_kernel(*refs, rope, cache_seq, n_carry, layer, fill_depth):
    it = iter(refs)
    (x_ref, mod_ref, ng_ref, wp_ref, wab_ref, wvat_ref, wuq_ref, wukv_ref, wvbt_ref, qn_ref,
     kvn_ref) = (next(it) for _ in range(11))
    if rope:
        ca_ref, sa_ref, cb_ref, sb_ref = (next(it) for _ in range(4))
    for _ in range(n_carry):
        next(it)
    (qa_ref, ka_ref, vat_ref, z_ref, hb_ref, qb_ref, kb_ref, vbt_ref, small_ref, cqkv_ref,
     abt_ref) = (next(it) for _ in range(11))
    if rope:
        qnorm_ref = next(it)
    if cache_seq:
        ck_ref, cv_ref, cckv_ref, ckpe_ref = (next(it) for _ in range(4))

    def to_cache(ref, val):
        for s in range(TMD // cache_seq):
            rows = val[s * cache_seq:(s + 1) * cache_seq]
            if fill_depth:
                for k in range(fill_depth):
                    ref[s, k] = rows if k == layer else jnp.zeros_like(rows)
            else:
                ref[s] = rows

    x = x_ref[...]
    mod = mod_ref[0]
    shift, scale = mod[:, :D_MODEL], mod[:, D_MODEL:2 * D_MODEL]
    xn = x * lax.rsqrt(jnp.mean(x * x, axis=-1, keepdims=True) + EPS) * ng_ref[...]
    hb = (xn * (1.0 + scale) + shift).astype(BF16)
    lane = lax.broadcasted_iota(jnp.int32, (TMD, LANE), 1)
    lo = lane < HALF

    def mm(lo_col, hi_col):
        return _dot_nt(hb, wp_ref[lo_col:hi_col, :])

    r = mm(P_QKV, P_QKV + A_COLS)
    tiles = [r[:, t * LANE:(t + 1) * LANE] for t in range(A_COLS // LANE)]
    if cache_seq:
        to_cache(ck_ref, tiles[4])
        to_cache(cv_ref, tiles[6])
    if rope:
        ca, sa = ca_ref[...], sa_ref[...]
        tiles[:6] = [_rope(t, ca, sa) for t in tiles[:6]]
    for t in range(4):
        q16 = (tiles[t] * (HD_A ** -0.5 * LOG2E)).astype(BF16)
        qa_ref[:, t * LANE:(t + 1) * LANE] = q16
        if rope:
            sq = jnp.square(q16.astype(F32))
            qnorm_ref[:, N_QA + 2 * t:N_QA + 2 * t + 1] = jnp.sum(jnp.where(lo, sq, 0.0), -1, keepdims=True)
            qnorm_ref[:, N_QA + 2 * t + 1:N_QA + 2 * t + 2] = jnp.sum(jnp.where(lo, 0.0, sq), -1, keepdims=True)
    k01, k10 = tiles[4], tiles[5]
    k0_16 = jnp.where(lo, k01, 0.0).astype(BF16)
    k1_16 = jnp.where(lo, k10, 0.0).astype(BF16)
    ka_ref[:, 0 * LANE:1 * LANE] = k0_16
    ka_ref[:, 1 * LANE:2 * LANE] = jnp.where(lo, 0.0, k10).astype(BF16)
    ka_ref[:, 2 * LANE:3 * LANE] = k1_16
    ka_ref[:, 3 * LANE:4 * LANE] = jnp.where(lo, 0.0, k01).astype(BF16)
    if rope:
        qnorm_ref[:, N_KA:N_KA + 1] = _row_sumsq(k0_16)
        qnorm_ref[:, N_KA + 1:N_KA + 2] = _row_sumsq(k1_16)
        qnorm_ref[:, N_KA + N_KV_A:] = jnp.zeros((TMD, NORM_COLS - N_KA - N_KV_A), F32)
    ones = jnp.ones((VT_ONES, TMD), BF16)
    vt = _dot_nt(wvat_ref[...], hb)
    for g in range(N_KV_A):
        vat_ref[g * VT_ROWS:g * VT_ROWS + HD_A] = vt[g * HD_A:(g + 1) * HD_A].astype(BF16)
        vat_ref[g * VT_ROWS + HD_A:(g + 1) * VT_ROWS] = ones

    for t in range(3):
        z_ref[:, t * 512:(t + 1) * 512] = mm(P_Z + t * 512, P_Z + (t + 1) * 512).astype(BF16)
    hb_ref[...] = hb

    r = mm(P_CQ, P_CQ + Q_RANK_B)
    qn = r * lax.rsqrt(jnp.mean(r * r, axis=-1, keepdims=True) + EPS) * qn_ref[...]
    q = _dot(qn.astype(BF16), wuq_ref[...])
    if rope:
        cb, sb = cb_ref[...], sb_ref[...]
        for h in range(N_HEADS_B):
            seg = (_rope(q[:, h * MLA_HW:(h + 1) * MLA_HW], cb, sb) * (MLA_SCALE * LOG2E)).astype(BF16)
            qb_ref[:, h * MLA_HW:(h + 1) * MLA_HW] = seg
            qnorm_ref[:, N_QB + h:N_QB + h + 1] = _row_sumsq(seg)
    else:
        qb_ref[...] = (q * (MLA_SCALE * LOG2E)).astype(BF16)

    r = mm(P_SMALL, P_SMALL + LANE)
    small_ref[...] = r
    if cache_seq:
        to_cache(ckpe_ref, r[:, S_KPE:S_KPE + QK_ROPE_B])
    kp = _rope(r, cb, sb) if rope else r
    kp = jnp.where((lane >= S_KPE) & (lane < S_KPE + QK_ROPE_B), kp, 0.0)

    r = mm(P_CKV, P_CKV + KV_RANK_B)
    cn = r * lax.rsqrt(jnp.mean(r * r, axis=-1, keepdims=True) + EPS) * kvn_ref[...]
    if cache_seq:
        to_cache(cckv_ref, cn)
    cn16 = cn.astype(BF16)
    kv = _dot(cn16, wukv_ref[...])
    vt = _dot_nt(wvbt_ref[...], cn16)
    for h in range(N_HEADS_B):
        k16 = (kv[:, h * MLA_HW:(h + 1) * MLA_HW] + kp).astype(BF16)
        kb_ref[:, h * MLA_HW:(h + 1) * MLA_HW] = k16
        if rope:
            qnorm_ref[:, N_KB + h:N_KB + h + 1] = _row_sumsq(k16)
        vbt_ref[h * VT_ROWS:h * VT_ROWS + V_HD_B] = vt[h * V_HD_B:(h + 1) * V_HD_B].astype(BF16)
        vbt_ref[h * VT_ROWS + V_HD_B:(h + 1) * VT_ROWS] = ones

    for t in range(3):
        cqkv_ref[:, t * 512:(t + 1) * 512] = mm(P_CQKV + t * 512, P_CQKV + (t + 1) * 512).astype(BF16)
    if rope:
        qnorm_ref[...] = jnp.sqrt(qnorm_ref[...])

    for c in range(TMD // CHUNK):
        abt_ref[c] = _dot_nt(wab_ref[...], hb[c * CHUNK:(c + 1) * CHUNK])


def _layer_spec(arr, l):
    nd = arr.ndim - 1
    return pl.BlockSpec((None,) + arr.shape[1:], lambda *_: (l,) + (0,) * nd, pipeline_mode=pl.Buffered(1))


def _inproj(x2d, l, mod, mod_row_fn, ng, weights, qn, kvn, rope_tabs, tiles_per_seq, cache=None):
    t = x2d.shape[0]
    nt = t // TMD
    rope = rope_tabs is not None
    row = lambda i: (i, 0)
    col = lambda i: (0, i)
    wp, wab, wvat, wuq, wukv, wvbt = weights
    params = (ng, wp, wab, wvat, wuq, wukv, wvbt, qn, kvn)
    in_specs = [pl.BlockSpec((TMD, D_MODEL), row),
                pl.BlockSpec((1, 1, 3 * D_MODEL), lambda i: (mod_row_fn(i), 0, 0))]
    in_specs += [_layer_spec(a, l) for a in params]
    args = [x2d, mod, *params]
    if rope:
        pos = lambda i: (i % tiles_per_seq, 0)
        in_specs += [pl.BlockSpec((TMD, LANE), pos)] * 4
        args += list(rope_tabs)
    outs = [(W_A, BF16, False), (KA_COLS, BF16, False), (VAT_ROWS, BF16, True), (W_Z, BF16, False),
            (D_MODEL, BF16, False), (HB_COLS, BF16, False), (HB_COLS, BF16, False), (VBT_ROWS, BF16, True),
            (LANE, F32, False), (QKV_C, BF16, False)]
    out_shape = [jax.ShapeDtypeStruct((w, t) if tr else (t, w), dt) for w, dt, tr in outs]
    out_specs = [pl.BlockSpec((w, TMD), col) if tr else pl.BlockSpec((TMD, w), row) for w, _, tr in outs]
    out_shape.append(jax.ShapeDtypeStruct((t // CHUNK, 16, CHUNK), F32))
    out_specs.append(pl.BlockSpec((TMD // CHUNK, 16, CHUNK), lambda i: (i, 0, 0)))
    if rope:
        out_shape.append(jax.ShapeDtypeStruct((t, NORM_COLS), F32))
        out_specs.append(pl.BlockSpec((TMD, NORM_COLS), row))
    aliases, cache_seq, n_carry, fill_depth = {}, 0, 0, 0
    if cache is not None:
        depth, cache_seq, carry = cache
        spt = TMD // cache_seq
        if carry is None:
            fill_depth, carry = depth, ()
        for w in (N_KV_A * HD_A, N_KV_A * HD_A, KV_RANK_B, QK_ROPE_B):
            out_shape.append(jax.ShapeDtypeStruct((t // cache_seq, depth, cache_seq, w), F32))
            if fill_depth:
                out_specs.append(pl.BlockSpec((spt, depth, cache_seq, w), lambda i: (i, 0, 0, 0)))
            else:
                out_specs.append(pl.BlockSpec((spt, None, cache_seq, w), lambda i: (i, l, 0, 0)))
        n_carry = len(carry)
        for k, a in enumerate(carry):
            aliases[len(args)] = len(out_shape) - n_carry + k
            in_specs.append(pl.BlockSpec(memory_space=pl.ANY))
            args.append(a)
    return pl.pallas_call(
        functools.partial(_inproj_kernel, rope=rope, cache_seq=cache_seq, n_carry=n_carry, layer=l,
                          fill_depth=fill_depth),
        grid=(nt,),
        in_specs=in_specs,
        out_specs=out_specs,
        out_shape=out_shape,
        input_output_aliases=aliases,
        compiler_params=pltpu.CompilerParams(dimension_semantics=("parallel",)),
        name="inproj_ctx" if cache is not None else "inproj_lat",
    )(*args)


def _kvup_kernel(c_ref, kpe_ref, w_ref, wvt_ref, k_ref, vt_ref, kn_ref):
    c16 = c_ref[...].astype(BF16)
    kv = _dot(c16, w_ref[...])
    vt = _dot_nt(wvt_ref[...], c16)
    kp = kpe_ref[...]
    ones = jnp.ones((VT_ONES, c16.shape[0]), BF16)
    for h in range(N_HEADS_B):
        k16 = (kv[:, h * MLA_HW:(h + 1) * MLA_HW] + kp).astype(BF16)
        k_ref[:, h * MLA_HW:(h + 1) * MLA_HW] = k16
        kn_ref[:, h:h + 1] = _row_sumsq(k16)
        vt_ref[h * VT_ROWS:h * VT_ROWS + V_HD_B] = vt[h * V_HD_B:(h + 1) * V_HD_B].astype(BF16)
        vt_ref[h * VT_ROWS + V_HD_B:(h + 1) * VT_ROWS] = ones
    kn_ref[...] = jnp.sqrt(kn_ref[...])


def _kvup(ckv, kpe, wukv, wvbt, l):
    nb, _, past, _ = ckv.shape
    return pl.pallas_call(
        _kvup_kernel,
        grid=(nb,),
        in_specs=[pl.BlockSpec((None, None, past, KV_RANK_B), lambda b: (b, l, 0, 0)),
                  pl.BlockSpec((None, None, past, LANE), lambda b: (b, l, 0, 0)),
                  _layer_spec(wukv, l), _layer_spec(wvbt, l)],
        out_specs=[pl.BlockSpec((past, HB_COLS), lambda b: (b, 0)),
                   pl.BlockSpec((None, VBT_ROWS, past), lambda b: (b, 0, 0)),
                   pl.BlockSpec((past, N_HEADS_B), lambda b: (b, 0))],
        out_shape=[jax.ShapeDtypeStruct((nb * past, HB_COLS), BF16),
                   jax.ShapeDtypeStruct((nb, VBT_ROWS, past), BF16),
                   jax.ShapeDtypeStruct((nb * past, N_HEADS_B), F32)],
        name="mla_cache_up",
    )(ckv, kpe, wukv, wvbt)


def _scores_t(q_tiles, k_tiles, bias_t):
    st = _bdot_nt(jnp.stack(k_tiles), jnp.stack(q_tiles))
    return st if bias_t is None else st + bias_t[None]


def _softmax_pv(sts, vts, sink, shift=None):
    if shift is None:
        m = jnp.max(sts[0], axis=1, keepdims=True)
        for st in sts[1:]:
            m = jnp.maximum(m, jnp.max(st, axis=1, keepdims=True))
    else:
        m = shift
    if sink is not None:
        m = jnp.maximum(m, sink)
    ot = _bdot(jnp.stack(vts[0]), jnp.exp2(sts[0] - m).astype(BF16))
    for st, vt in zip(sts[1:], vts[1:]):
        ot = ot + _bdot(jnp.stack(vt), jnp.exp2(st - m).astype(BF16))
    den = ot[:, V_HD_B:V_HD_B + 1, :]
    if sink is not None:
        den = den + jnp.exp2(sink - m)
    num = ot[:, :V_HD_B, :] / den
    return [jnp.concatenate([num[2 * i], num[2 * i + 1]], axis=0).T for i in range(sts[0].shape[0] // 2)]


def _tile(x, t):
    return x[:, t * LANE:(t + 1) * LANE]


def _attn_a_heads(q, segments, sink_ref, shift=None):
    sts, vts = [], []
    for ka, vat, bias_t in segments:
        qs, ks, vs = [], [], []
        for t in range(N_HEADS_A // 2):
            g = (2 * t) // GQA_GROUP
            for e in range(2):
                qs.append(_tile(q, t))
                ks.append(_tile(ka, 2 * g + e))
                vs.append(vat[g * VT_ROWS:(g + 1) * VT_ROWS])
        sts.append(_scores_t(qs, ks, bias_t))
        vts.append(vs)
    sink = jnp.stack([sink_ref[:, h:h + 1] * LOG2E for h in range(N_HEADS_A)])
    return _softmax_pv(sts, vts, sink, shift)


def _gated_store(outs, z_ref, o_ref, first_tile=0):
    for i, o in enumerate(outs):
        t = first_tile + i
        z = _tile(z_ref, t).astype(F32)
        o_ref[:, t * LANE:(t + 1) * LANE] = (o * _silu(z)).astype(BF16)


def _attn_ctx_kernel(qa_ref, ka_ref, vat_ref, qb_ref, kb_ref, vbt_ref, za_ref, zb_ref, sink_ref, oa_ref, ob_ref):
    _gated_store(_attn_a_heads(qa_ref[...], [(ka_ref[...], vat_ref[...], None)], sink_ref), za_ref, oa_ref)
    q, kb, vbt = qb_ref[...], kb_ref[...], vbt_ref[...]
    heads = range(N_HEADS_B)
    st = _scores_t([_tile(q, h) for h in heads], [_tile(kb, h) for h in heads], None)
    _gated_store(_softmax_pv([st], [[vbt[h * VT_ROWS:(h + 1) * VT_ROWS] for h in heads]], None), zb_ref, ob_ref)


def _attn_ctx(qa, ka, vat, qb, kb, vbt, z, sink, l, seq):
    t = qa.shape[0]
    row = lambda b: (b, 0)
    col = lambda b: (0, b)
    return pl.pallas_call(
        _attn_ctx_kernel,
        grid=(t // seq,),
        in_specs=[pl.BlockSpec((seq, W_A), row),
                  pl.BlockSpec((seq, KA_COLS), row),
                  pl.BlockSpec((VAT_ROWS, seq), col),
                  pl.BlockSpec((seq, HB_COLS), row),
                  pl.BlockSpec((seq, HB_COLS), row),
                  pl.BlockSpec((VBT_ROWS, seq), col),
                  pl.BlockSpec((seq, W_A), row),
                  pl.BlockSpec((seq, W_B), lambda b: (b, 1)),
                  _layer_spec(sink, l)],
        out_specs=[pl.BlockSpec((seq, W_A), row), pl.BlockSpec((seq, W_B), row)],
        out_shape=[jax.ShapeDtypeStruct((t, W_A), BF16), jax.ShapeDtypeStruct((t, W_B), BF16)],
        compiler_params=pltpu.CompilerParams(dimension_semantics=("parallel",)),
        name="attn_ctx",
    )(qa, ka, vat, qb, kb, vbt, z, z, sink)


def _attn_a_lat_kernel(q_ref, qn_ref, ksn_ref, kp_ref, kc_ref, kn_ref, vp_ref, vc_ref, vn_ref, kx_ref, vx_ref,
                       z_ref, sink_ref, o_ref, knorm_scr, *, nq):
    j = pl.program_id(1)

    @pl.when(j == 0)
    def _():
        kmax = jnp.max(ksn_ref[:, N_KA:N_KA + N_KV_A], axis=0, keepdims=True)
        kxmax = [jnp.max(_row_norm(kx_ref[:, 2 * g * LANE:(2 * g + 1) * LANE]), axis=0, keepdims=True)
                 for g in range(N_KV_A)]
        knorm_scr[...] = jnp.broadcast_to(jnp.maximum(kmax, jnp.concatenate(kxmax, axis=1)), knorm_scr.shape)

    qmax = jnp.max(qn_ref[:, N_QA:N_QA + N_HEADS_A], axis=0, keepdims=True)
    bound = jnp.stack([qmax[:, h:h + 1] * knorm_scr[0:1, h // GQA_GROUP:h // GQA_GROUP + 1] * SHIFT_SLACK
                       + SHIFT_SLACK_ABS for h in range(N_HEADS_A)])
    bound_ok = jnp.max(bound) <= SHIFT_MAX

    ka = jnp.concatenate([kp_ref[...], kc_ref[...], kn_ref[...]], axis=0)
    vat = jnp.concatenate([vp_ref[...], vc_ref[...], vn_ref[...]], axis=1)
    kj = lax.broadcasted_iota(jnp.int32, (ka.shape[0], TM), 0)
    qi = lax.broadcasted_iota(jnp.int32, (ka.shape[0], TM), 1)
    ok = (kj >= qi) & (kj <= qi + 2 * WINDOW)
    ok = ok & ((kj >= WINDOW) | (j > 0)) & ((kj < TM + WINDOW) | (j < nq - 1))
    bias_t = jnp.where(ok, 0.0, NEG_INF)
    segments = [(ka, vat, bias_t), (kx_ref[...], vx_ref[...], None)]

    @pl.when(bound_ok)
    def _():
        _gated_store(_attn_a_heads(q_ref[...], segments, sink_ref, bound), z_ref, o_ref)

    @pl.when(jnp.logical_not(bound_ok))
    def _():
        _gated_store(_attn_a_heads(q_ref[...], segments, sink_ref), z_ref, o_ref)


def _attn_a_lat(qa, norms, ka, vat, z, sink, kx, vxt, l, seq):
    t = qa.shape[0]
    nq = seq // TM
    past = kx.shape[2]
    r = TM // WINDOW
    row = lambda b, j: (b * nq + j, 0)
    prev = lambda b, j: ((b * nq + j) * r - jnp.where(j > 0, 1, 0), 0)
    nxt = lambda b, j: ((b * nq + j) * r + jnp.where(j < nq - 1, r, r - 1), 0)
    swap = lambda f: (lambda b, j: f(b, j)[::-1])
    return pl.pallas_call(
        functools.partial(_attn_a_lat_kernel, nq=nq),
        grid=(t // seq, nq),
        in_specs=[pl.BlockSpec((TM, W_A), row),
                  pl.BlockSpec((TM, NORM_COLS), row),
                  pl.BlockSpec((seq, NORM_COLS), lambda b, j: (b, 0)),
                  pl.BlockSpec((WINDOW, KA_COLS), prev),
                  pl.BlockSpec((TM, KA_COLS), row),
                  pl.BlockSpec((WINDOW, KA_COLS), nxt),
                  pl.BlockSpec((VAT_ROWS, WINDOW), swap(prev)),
                  pl.BlockSpec((VAT_ROWS, TM), swap(row)),
                  pl.BlockSpec((VAT_ROWS, WINDOW), swap(nxt)),
                  pl.BlockSpec((None, None, past, KA_COLS), lambda b, j: (b, l, 0, 0)),
                  pl.BlockSpec((None, None, VAT_ROWS, past), lambda b, j: (b, l, 0, 0)),
                  pl.BlockSpec((TM, W_A), row),
                  _layer_spec(sink, l)],
        out_specs=pl.BlockSpec((TM, W_A), row),
        out_shape=jax.ShapeDtypeStruct((t, W_A), BF16),
        scratch_shapes=[pltpu.VMEM((8, N_KV_A), F32)],
        compiler_params=pltpu.CompilerParams(dimension_semantics=("parallel", "arbitrary")),
        name="attn_a_lat",
    )(qa, norms, norms, ka, ka, ka, vat, vat, vat, kx, vxt, z, sink)


def _attn_b_lat_kernel(q_ref, qn_ref, kn_ref, knx_ref, k_ref, vt_ref, kx_ref, vxt_ref, z_ref, o_ref, knorm_scr, *,
                       group):
    segments = [(k_ref, vt_ref), (kx_ref, vxt_ref)]
    q = q_ref[...]

    @pl.when(pl.program_id(1) == 0)
    def _():
        kmax = jnp.maximum(jnp.max(kn_ref[:, N_KB:N_KB + N_HEADS_B], axis=0, keepdims=True),
                           jnp.max(knx_ref[...], axis=0, keepdims=True))
        knorm_scr[...] = jnp.broadcast_to(kmax, knorm_scr.shape)

    qmax = jnp.max(qn_ref[:, N_QB:N_QB + N_HEADS_B], axis=0, keepdims=True)
    bound = qmax * knorm_scr[0:1, :] * SHIFT_SLACK + SHIFT_SLACK_ABS
    bound = jnp.stack([bound[:, h:h + 1] for h in range(N_HEADS_B)])
    bound_ok = jnp.max(bound) <= SHIFT_MAX

    def scores(h0):
        heads = range(h0, h0 + group)
        return [_scores_t([_tile(q, h) for h in heads], [_tile(kr, h) for h in heads], None) for kr, _ in segments]

    def attend(shift):
        sts = scores(0)
        for h0 in range(0, N_HEADS_B, group):
            sts_next = scores(h0 + group) if h0 + group < N_HEADS_B else None
            vts = [[vr[h * VT_ROWS:(h + 1) * VT_ROWS, :] for h in range(h0, h0 + group)] for _, vr in segments]
            outs = _softmax_pv(sts, vts, None, None if shift is None else shift[h0:h0 + group])
            _gated_store(outs, z_ref, o_ref, h0 // 2)
            sts = sts_next

    @pl.when(bound_ok)
    def _():
        attend(bound)

    @pl.when(jnp.logical_not(bound_ok))
    def _():
        attend(None)


def _attn_b_lat(qb, norms, kb, vbt, z, kx, vxt, knx, seq, qblk, group):
    t = qb.shape[0]
    nq = seq // qblk
    past = kx.shape[0] // (t // seq)
    hw = HB_COLS
    return pl.pallas_call(
        functools.partial(_attn_b_lat_kernel, group=group),
        grid=(t // seq, nq),
        in_specs=[pl.BlockSpec((qblk, hw), lambda b, j: (b * nq + j, 0)),
                  pl.BlockSpec((qblk, NORM_COLS), lambda b, j: (b * nq + j, 0)),
                  pl.BlockSpec((seq, NORM_COLS), lambda b, j: (b, 0)),
                  pl.BlockSpec((past, N_HEADS_B), lambda b, j: (b, 0)),
                  pl.BlockSpec((seq, hw), lambda b, j: (b, 0)),
                  pl.BlockSpec((VBT_ROWS, seq), lambda b, j: (0, b)),
                  pl.BlockSpec((past, hw), lambda b, j: (b, 0)),
                  pl.BlockSpec((None, VBT_ROWS, past), lambda b, j: (b, 0, 0)),
                  pl.BlockSpec((qblk, W_B), lambda b, j: (b * nq + j, 1))],
        out_specs=pl.BlockSpec((qblk, W_B), lambda b, j: (b * nq + j, 0)),
        out_shape=jax.ShapeDtypeStruct((t, W_B), BF16),
        scratch_shapes=[pltpu.VMEM((8, N_HEADS_B), F32)],
        compiler_params=pltpu.CompilerParams(dimension_semantics=("parallel", "arbitrary")),
        name="attn_b_lat",
    )(qb, norms, norms, knx, kb, vbt, kx, vxt, z)


def _gdn_local_kernel(cq_ref, prev_ref, next_ref, small_ref, abt_ref, cw_ref, prow_ref, pcol_ref,
                      u_ref, w_ref, qg_ref, kd_ref, attn_ref, eg_ref, qkv_scr, gb_scr, *, seq):
    x = cq_ref[...].astype(F32)
    tiles_per_seq = seq // TMG
    tpos = pl.program_id(0) % tiles_per_seq
    prev_row = jnp.where(tpos > 0, prev_ref[...].astype(F32)[HALO - 1:HALO, :], 0.0)
    next_row = jnp.where(tpos < tiles_per_seq - 1, next_ref[...].astype(F32)[0:1, :], 0.0)
    rows = lax.broadcasted_iota(jnp.int32, (TMG, 1), 0)
    xm1 = jnp.where(rows == 0, prev_row, pltpu.roll(x, 1, 0))
    xp1 = jnp.where(rows == TMG - 1, next_row, pltpu.roll(x, TMG - 1, 0))
    cw = cw_ref[...]
    y = _silu(xm1 * cw[0:1] + x * cw[1:2] + xp1 * cw[2:3])
    nq = N_HEADS_C * DK_C
    for h in range(N_HEADS_C):
        qh = y[:, h * DK_C:(h + 1) * DK_C]
        kh = y[:, nq + h * DK_C:nq + (h + 1) * DK_C]
        qkv_scr[:, h * DK_C:(h + 1) * DK_C] = (
            qh * lax.rsqrt(jnp.sum(qh * qh, axis=-1, keepdims=True) + EPS) * (DK_C ** -0.5))
        qkv_scr[:, nq + h * DK_C:nq + (h + 1) * DK_C] = kh * lax.rsqrt(jnp.sum(kh * kh, axis=-1, keepdims=True) + EPS)
    qkv_scr[:, 2 * nq:] = y[:, 2 * nq:]

    sm = small_ref[...]
    prow = prow_ref[...]
    gb_scr[:, 0:8] = -jnp.exp(prow[0:1]) * _softplus(sm[:, S_A:S_A + 8] + prow[1:2])
    gb_scr[:, 8:16] = _sigmoid(sm[:, S_B:S_B + 8])
    pcol = pcol_ref[...]

    ri = lax.broadcasted_iota(jnp.int32, (CHUNK, LANE), 0)
    lane = lax.broadcasted_iota(jnp.int32, (CHUNK, LANE), 1)
    fwd = lane < CHUNK
    cj = lane & (CHUNK - 1)
    incl = (fwd & (ri >= cj)) | (~fwd & (ri <= cj))
    strict = (fwd & (ri > cj)) | (~fwd & (ri < cj))
    xor = ri ^ cj
    eye = (ri == cj).astype(F32)
    r2 = lax.broadcasted_iota(jnp.int32, (2 * CHUNK, LANE), 0)
    l2 = lax.broadcasted_iota(jnp.int32, (2 * CHUNK, LANE), 1)
    same_dir = (r2 < CHUNK) == (l2 < CHUNK)
    rs_ = lax.broadcasted_iota(jnp.int32, (CHUNK, CHUNK), 0)
    cs_ = lax.broadcasted_iota(jnp.int32, (CHUNK, CHUNK), 1)
    tril = (rs_ >= cs_).astype(F32)
    triu = (rs_ <= cs_).astype(F32)
    tri_rows = jnp.concatenate([triu, tril], axis=1)
    dup_rows = jnp.concatenate([(rs_ == cs_).astype(F32)] * 2, axis=1)

    def block_diag(x):
        return jnp.where(same_dir[None], jnp.concatenate([x, x], axis=1), 0.0).astype(BF16)

    lows, rhss, order = [], [], []

    def solve():
        low = jnp.stack(lows, axis=0)
        inv = eye[None] - jnp.where(xor[None] == 1, low, 0.0)
        b = 2
        while b < CHUNK:
            cpl = jnp.where((xor[None] >= b) & (xor[None] < 2 * b), low, 0.0)
            tmp = _bdot(cpl.astype(BF16), block_diag(inv))
            inv = inv - _bdot(inv.astype(BF16), block_diag(tmp))
            b *= 2
        scale_u, scale_w, vs, ks = (jnp.stack(a, axis=0) for a in zip(*rhss))
        u = _bdot((inv * scale_u).astype(BF16), vs)
        w = _bdot((inv * scale_w).astype(BF16), ks)
        for i, (rs, h) in enumerate(order):
            for d in range(2):
                cs = slice((d * N_HEADS_C + h) * DK_C, (d * N_HEADS_C + h + 1) * DK_C)
                u_ref[rs, cs] = u[i, :, d * DV_C:(d + 1) * DV_C].astype(BF16)
                w_ref[rs, cs] = w[i, :, d * DK_C:(d + 1) * DK_C].astype(BF16)

    for c in range(TMG // CHUNK):
        rs = slice(c * CHUNK, (c + 1) * CHUNK)
        gcol = gb_scr[rs, 0:8]
        bcol = gb_scr[rs, 8:16]
        abt = abt_ref[c]
        grow = -jnp.exp(pcol[:, 0:1]) * _softplus(abt[0:8] + pcol[:, 1:2])
        gc_f = _dot_exact(tril, gcol)
        gc_b = _dot_exact(triu, gcol)
        gr = _dot_exact(grow, tri_rows)
        br = _dot_exact(_sigmoid(abt[8:16]), dup_rows)
        for h in range(N_HEADS_C):
            hb_ = N_HEADS_C + h
            q = qkv_scr[rs, h * DK_C:(h + 1) * DK_C]
            k = qkv_scr[rs, nq + h * DK_C:nq + (h + 1) * DK_C]
            v = qkv_scr[rs, 2 * nq + h * DV_C:2 * nq + (h + 1) * DV_C]
            k16 = k.astype(BF16)
            kk16 = jnp.concatenate([k16, k16], axis=0)
            kk = _dot_nt(k16, kk16)
            qk = _dot_nt(q.astype(BF16), kk16)
            gcs = (gc_f[:, h:h + 1], gc_b[:, hb_:hb_ + 1])
            betas = (bcol[:, h:h + 1], bcol[:, hb_:hb_ + 1])
            gc2 = jnp.where(fwd, gcs[0], gcs[1])
            gr2 = jnp.where(fwd[0:1], gr[h:h + 1, :], gr[hb_:hb_ + 1, :])
            decay = jnp.where(incl, jnp.exp(jnp.where(incl, gc2 - gr2, 0.0)), 0.0)
            lows.append(jnp.where(strict, jnp.where(fwd, betas[0], betas[1]) * kk * decay, 0.0))
            attn_ref[rs, h * LANE:(h + 1) * LANE] = (qk * decay).astype(BF16)
            order.append((rs, h))
            br2 = jnp.where(fwd[0:1], br[h:h + 1, :], br[hb_:hb_ + 1, :])
            v16 = v.astype(BF16)
            zeros = jnp.zeros_like(v16)
            rhss.append((br2, br2 * jnp.exp(gr2),
                         jnp.concatenate([jnp.concatenate([v16, zeros], axis=1),
                                          jnp.concatenate([zeros, v16], axis=1)], axis=0),
                         jnp.concatenate([jnp.concatenate([k16, zeros], axis=1),
                                          jnp.concatenate([zeros, k16], axis=1)], axis=0)))
            for d in range(2):
                dh = d * N_HEADS_C + h
                gc = gcs[d]
                eg = jnp.exp(gc)
                g_last = gc[CHUNK - 1:CHUNK] if d == 0 else gc[0:1]
                cs = slice(dh * DK_C, (dh + 1) * DK_C)
                qg_ref[rs, cs] = (q * eg).astype(BF16)
                kd_ref[rs, cs] = (k * jnp.exp(g_last - gc)).astype(BF16)
                eg_ref[c, dh:dh + 1, :] = jnp.broadcast_to(jnp.exp(g_last), (1, LANE))
    solve()


def _gdn_local(cqkv, small, abt, conv_w, prow, pcol, l, seq):
    t = cqkv.shape[0]
    assert seq % TMG == 0
    nt = t // TMG
    nh8 = t // HALO
    cpg = TMG // CHUNK
    row = lambda i: (i, 0)
    dh = 2 * N_HEADS_C
    return pl.pallas_call(
        functools.partial(_gdn_local_kernel, seq=seq),
        grid=(nt,),
        in_specs=[pl.BlockSpec((TMG, QKV_C), row),
                  pl.BlockSpec((HALO, QKV_C), lambda i: (jnp.maximum(i * (TMG // HALO) - 1, 0), 0)),
                  pl.BlockSpec((HALO, QKV_C), lambda i: (jnp.minimum((i + 1) * (TMG // HALO), nh8 - 1), 0)),
                  pl.BlockSpec((TMG, LANE), row),
                  pl.BlockSpec((cpg, 16, CHUNK), lambda i: (i, 0, 0)),
                  _layer_spec(conv_w, l), _layer_spec(prow, l), _layer_spec(pcol, l)],
        out_specs=[pl.BlockSpec((TMG, dh * DV_C), row),
                   pl.BlockSpec((TMG, dh * DK_C), row),
                   pl.BlockSpec((TMG, dh * DK_C), row),
                   pl.BlockSpec((TMG, dh * DK_C), row),
                   pl.BlockSpec((TMG, dh * CHUNK), row),
                   pl.BlockSpec((cpg, dh, LANE), lambda i: (i, 0, 0))],
        out_shape=[jax.ShapeDtypeStruct((t, dh * DV_C), BF16),
                   jax.ShapeDtypeStruct((t, dh * DK_C), BF16),
                   jax.ShapeDtypeStruct((t, dh * DK_C), BF16),
                   jax.ShapeDtypeStruct((t, dh * DK_C), BF16),
                   jax.ShapeDtypeStruct((t, dh * CHUNK), BF16),
                   jax.ShapeDtypeStruct((t // CHUNK, dh, LANE), F32)],
        scratch_shapes=[pltpu.VMEM((TMG, QKV_C), F32), pltpu.VMEM((TMG, 16), F32)],
        compiler_params=pltpu.CompilerParams(dimension_semantics=("parallel",)),
        name="gdn_local",
    )(cqkv, cqkv, cqkv, small, abt, conv_w, prow, pcol)


def _gdn_scan_kernel(*refs, nt, ns, has_init, want_state, n_carry, layer, fill_depth):
    it = iter(refs)
    ins = [[next(it) for _ in range(6)] for _ in range(2)]
    s0_ref = next(it) if has_init else None
    for _ in range(n_carry):
        next(it)
    o_refs = [next(it), next(it)]
    st_ref = next(it) if want_state else None
    s_scr = next(it)
    j = pl.program_id(1)
    nh = N_HEADS_C
    nst = ns * 2 * nh

    @pl.when(j == 0)
    def _():
        if has_init:
            s_scr[...] = s0_ref[...].reshape(nst, DK_C, DV_C)
        else:
            s_scr[...] = jnp.zeros_like(s_scr)

    for step in range(CPT):
        chunk = lambda d: step if d == 0 else CPT - 1 - step

        def gather(idx, width):
            return jnp.stack([ins[d][idx][s, chunk(d) * CHUNK:(chunk(d) + 1) * CHUNK, h * width:(h + 1) * width]
                              for s in range(ns) for d in range(2) for h in range(nh)])

        u, w, qg, kd, attn = gather(0, DV_C), gather(1, DK_C), gather(2, DK_C), gather(3, DK_C), gather(4, LANE)
        eg = jnp.stack([ins[d][5][s, chunk(d), d * nh + h:d * nh + h + 1, :]
                        for s in range(ns) for d in range(2) for h in range(nh)])
        st = s_scr[...]
        sb = st.astype(BF16)
        v_new = u.astype(F32) - _bdot(w, sb)
        vb = v_new.astype(BF16)
        zeros = jnp.zeros((CHUNK, DV_C), BF16)
        vb2 = jnp.stack([jnp.concatenate([vb[i], zeros] if (i // nh) % 2 == 0 else [zeros, vb[i]], axis=0)
                         for i in range(nst)])
        o = _bdot(qg, sb) + _bdot(attn, vb2)
        s_scr[...] = st * eg + _bdot_tn(kd, vb)
        for s in range(ns):
            for d in range(2):
                for h in range(nh):
                    o_refs[d][s, chunk(d) * CHUNK:(chunk(d) + 1) * CHUNK, h * DV_C:(h + 1) * DV_C] = (
                        o[(s * 2 + d) * nh + h].astype(BF16))

    if want_state:
        @pl.when(j == nt - 1)
        def _():
            final = s_scr[...].reshape(ns, 2, nh, DK_C, DV_C)
            if fill_depth:
                for k in range(fill_depth):
                    st_ref[:, k] = final if k == layer else jnp.zeros_like(final)
            else:
                st_ref[...] = final


def _gdn_scan(u, w, qg, kd, attn, eg, s0, l, seq, state_out=None):
    t = u.shape[0]
    want_state = state_out is not None
    nt = seq // TM
    nb = t // seq
    ns = next(n for n in (SCAN_SEQS, 2, 1) if nb % n == 0)
    half = N_HEADS_C * DK_C
    has_init = s0 is not None
    by_seq = lambda a: a.reshape((nb, a.shape[0] // nb) + a.shape[1:])
    in_specs, args = [], []
    for d in range(2):
        tile = (lambda b, j: j) if d == 0 else (lambda b, j: nt - 1 - j)
        row = lambda b, j, d=d, tile=tile: (b, tile(b, j), d)
        row4 = lambda b, j, tile=tile: (b, tile(b, j), 0, 0)
        in_specs += [pl.BlockSpec((ns, TM, half), row)] * 4
        in_specs += [pl.BlockSpec((ns, TM, N_HEADS_C * LANE), lambda b, j, tile=tile: (b, tile(b, j), 0)),
                     pl.BlockSpec((ns, CPT, 2 * N_HEADS_C, LANE), row4)]
        args += [by_seq(a) for a in (u, w, qg, kd, attn, eg)]
    st_tail = (2, N_HEADS_C, DK_C, DV_C)
    if has_init:
        in_specs.append(pl.BlockSpec((ns, None) + st_tail, lambda b, j: (b, l, 0, 0, 0, 0)))
        args.append(s0)
    out_specs = [pl.BlockSpec((ns, TM, half), lambda b, j: (b, j, 0)),
                 pl.BlockSpec((ns, TM, half), lambda b, j: (b, nt - 1 - j, 0))]
    out_shape = [jax.ShapeDtypeStruct((nb, seq, half), BF16)] * 2
    aliases, n_carry, fill_depth = {}, 0, 0
    if want_state:
        depth, carry = state_out
        out_shape.append(jax.ShapeDtypeStruct((nb, depth) + st_tail, F32))
        if carry is None:
            fill_depth = depth
            out_specs.append(pl.BlockSpec((ns, depth) + st_tail, lambda b, j: (b, 0, 0, 0, 0, 0)))
        else:
            out_specs.append(pl.BlockSpec((ns, None) + st_tail, lambda b, j: (b, l, 0, 0, 0, 0)))
            n_carry = 1
            aliases[len(args)] = 2
            in_specs.append(pl.BlockSpec(memory_space=pl.ANY))
            args.append(carry)
    outs = pl.pallas_call(
        functools.partial(_gdn_scan_kernel, nt=nt, ns=ns, has_init=has_init, want_state=want_state,
                          n_carry=n_carry, layer=l, fill_depth=fill_depth),
        grid=(nb // ns, nt),
        in_specs=in_specs,
        out_specs=out_specs,
        out_shape=out_shape,
        input_output_aliases=aliases,
        scratch_shapes=[pltpu.VMEM((ns * 2 * N_HEADS_C, DK_C, DV_C), F32)],
        compiler_params=pltpu.CompilerParams(dimension_semantics=("parallel", "arbitrary")),
        name="gdn_scan",
    )(*args)
    return [outs[0].reshape(t, half), outs[1].reshape(t, half)] + list(outs[2:])


def _merge_kernel(x_ref, mod_ref, oa_ref, ob_ref, cf_ref, cb_ref, zc_ref, hb_ref, wg_ref, gn_ref,
                  wa_ref, wb_ref, wc_ref, wo_ref, fg_ref, o_ref, *, last):
    oc = cf_ref[...].astype(F32) + cb_ref[...].astype(F32)
    zc = zc_ref[...].astype(F32)
    gn = gn_ref[...]
    parts = []
    for h in range(N_HEADS_C):
        hs = slice(h * DV_C, (h + 1) * DV_C)
        och = oc[:, hs]
        och = och * lax.rsqrt(jnp.mean(och * och, axis=-1, keepdims=True) + EPS) * gn
        parts.append((och * _silu(zc[:, hs])).astype(BF16))
    ocz = jnp.concatenate(parts, axis=-1)
    pa = _dot(oa_ref[...], wa_ref[...])
    pb = _dot(ob_ref[...], wb_ref[...])
    pc = _dot(ocz, wc_ref[...])
    hb = hb_ref[...]
    ga = _sigmoid(_dot_nt(hb, wg_ref[0:D_MODEL, :]))
    gb = _sigmoid(_dot_nt(hb, wg_ref[D_MODEL:2 * D_MODEL, :]))
    gc = _sigmoid(_dot_nt(hb, wg_ref[2 * D_MODEL:, :]))
    y = _dot((ga * pa + gb * pb + gc * pc).astype(BF16), wo_ref[...])
    gate = mod_ref[0][:, 2 * D_MODEL:]
    xo = x_ref[...] + gate * y
    if last:
        xo = xo * lax.rsqrt(jnp.mean(xo * xo, axis=-1, keepdims=True) + EPS) * fg_ref[...]
    o_ref[...] = xo


def _merge(x2d, l, mod, mod_row_fn, oa, ob, cf, cb, z, hb, wp, gn, wa, wb, wc, wo, fg, last):
    t = x2d.shape[0]
    row = lambda i: (i, 0)
    assert P_GATES == 0
    return pl.pallas_call(
        functools.partial(_merge_kernel, last=last),
        grid=(t // TMD,),
        in_specs=[pl.BlockSpec((TMD, D_MODEL), row),
                  pl.BlockSpec((1, 1, 3 * D_MODEL), lambda i: (mod_row_fn(i), 0, 0)),
                  pl.BlockSpec((TMD, W_A), row),
                  pl.BlockSpec((TMD, W_B), row),
                  pl.BlockSpec((TMD, W_C), row),
                  pl.BlockSpec((TMD, W_C), row),
                  pl.BlockSpec((TMD, W_C), lambda i: (i, 2)),
                  pl.BlockSpec((TMD, D_MODEL), row),
                  pl.BlockSpec((None, W_GATES, D_MODEL), lambda i: (l, 0, 0), pipeline_mode=pl.Buffered(1)),
                  _layer_spec(gn, l), _layer_spec(wa, l), _layer_spec(wb, l), _layer_spec(wc, l),
                  _layer_spec(wo, l),
                  pl.BlockSpec((1, D_MODEL), lambda i: (0, 0))],
        out_specs=pl.BlockSpec((TMD, D_MODEL), row),
        out_shape=jax.ShapeDtypeStruct((t, D_MODEL), F32),
        compiler_params=pltpu.CompilerParams(dimension_semantics=("parallel",)),
        name="merge",
    )(x2d, mod, oa, ob, cf, cb, z, hb, wp, gn, wa, wb, wc, wo, fg)


def _rope_tables(n_tokens, rot_dim):
    rows = n_tokens // GRID_W
    row = np.repeat(np.arange(rows), GRID_W).astype(np.float32)
    col = np.tile(np.arange(GRID_W), rows).astype(np.float32)
    n_pairs = rot_dim // 4
    inv = (np.float32(ROPE_BASE) ** (-np.arange(n_pairs, dtype=np.float32) / np.float32(n_pairs))).astype(np.float32)
    ang = np.concatenate([row[:, None] * inv, col[:, None] * inv], axis=-1)
    c, s = np.cos(ang), np.sin(ang)
    return np.repeat(c, 2, axis=-1), np.stack([-s, s], axis=-1).reshape(n_tokens, rot_dim)


def _in_offsets():
    o = [0]
    for n in IN_SIZES:
        o.append(o[-1] + n)
    return o


def _relayout_moves():
    o = _in_offsets()
    order = [(o[12], W_GATES), (o[0], W_A), (o[1], LANE), (o[1] + HD_A, HD_A), (o[1], HD_A), (o[2], LANE),
             (o[3], W_A), (o[7], W_B), (o[11], W_C), (o[4], Q_RANK_B), (o[5], KV_RANK_B),
             (None, S_KPE), (o[6], QK_ROPE_B), (o[9], 4 * N_HEADS_C), (None, LANE - S_B - 8), (o[8], QKV_C)]
    moves, dst = [], 0
    for src, n in order:
        moves.append((src, dst, n))
        dst += n
    assert dst == P_END
    return moves


def _relayout_kernel(w_ref, o_ref, wab_ref, wvat_ref):
    for src, dst, n in _relayout_moves():
        if src is None:
            o_ref[dst:dst + n, :] = jnp.zeros((n, o_ref.shape[1]), BF16)
        else:
            o_ref[dst:dst + n, :] = w_ref[src:src + n, :].astype(BF16)
    o = _in_offsets()
    wab_ref[...] = w_ref[o[9]:o[11], :].astype(BF16)
    wvat_ref[...] = w_ref[o[2]:o[3], :].astype(BF16)


def _relayout_w_in(w_in):
    w_t = jnp.swapaxes(w_in, 1, 2)
    depth, width, _ = w_t.shape
    cols = 128
    nab, nv = 4 * N_HEADS_C, N_KV_A * HD_A
    col = lambda l, i: (l, 0, i)
    return pl.pallas_call(
        _relayout_kernel,
        grid=(depth, D_MODEL // cols),
        in_specs=[pl.BlockSpec((None, width, cols), col)],
        out_specs=[pl.BlockSpec((None, P_END, cols), col),
                   pl.BlockSpec((None, nab, cols), col),
                   pl.BlockSpec((None, nv, cols), col)],
        out_shape=[jax.ShapeDtypeStruct((depth, P_END, D_MODEL), BF16),
                   jax.ShapeDtypeStruct((depth, nab, D_MODEL), BF16),
                   jax.ShapeDtypeStruct((depth, nv, D_MODEL), BF16)],
        name="w_in_relayout",
    )(w_t)


def _prep_weights(w_in, w_uq, w_ukv):
    depth = w_in.shape[0]
    wp, wab, wvat = _relayout_w_in(w_in)
    hd = QK_NOPE_B + QK_ROPE_B
    wuq = jnp.pad(w_uq.reshape(depth, Q_RANK_B, N_HEADS_B, hd), ((0, 0), (0, 0), (0, 0), (0, MLA_HW - hd)))
    wuq = wuq.reshape(depth, Q_RANK_B, HB_COLS).astype(BF16)
    kv = w_ukv.reshape(depth, KV_RANK_B, N_HEADS_B, QK_NOPE_B + V_HD_B)
    wk = jnp.pad(kv[..., :QK_NOPE_B], ((0, 0), (0, 0), (0, 0), (0, MLA_HW - QK_NOPE_B)))
    wukv = wk.reshape(depth, KV_RANK_B, HB_COLS).astype(BF16)
    wvbt = jnp.swapaxes(kv[..., QK_NOPE_B:].reshape(depth, KV_RANK_B, W_B), 1, 2).astype(BF16)
    return wp, wab, wvat, wuq, wukv, wvbt


def _cache_tiles_a(kx, vx):
    k0, k1 = kx[..., 0, :], kx[..., 1, :]
    z = jnp.zeros_like(k0)
    ka = jnp.concatenate([k0, z, z, k0, k1, z, z, k1], axis=-1).astype(BF16)
    vt = jnp.transpose(vx, (0, 1, 3, 4, 2))
    vt = jnp.concatenate([vt, jnp.ones(vt.shape[:3] + (VT_ONES, vt.shape[4]), vt.dtype)], axis=3)
    return ka, vt.reshape(vt.shape[:2] + (VAT_ROWS, vt.shape[4])).astype(BF16)


def kernel(x_prompt, x_sample, cache_attn_k, cache_attn_v, cache_mla_ckv, cache_mla_kpe, state_gdn, c, c_ctx,
           norm_g, w_ada, b_ada, w_in, attn_sink, mla_q_norm, mla_w_uq, mla_kv_norm, mla_w_ukv, gdn_conv,
           gdn_a_log, gdn_dt_bias, gdn_norm, w_branch_a, w_branch_b, w_branch_c, w_out, final_norm_g):
    depth = w_in.shape[0]
    nb_c, seq_c, _ = x_prompt.shape
    nb_l, seq_l, _ = x_sample.shape
    past = cache_attn_k.shape[2]
    assert P_END % LANE == 0 and seq_c % TM == 0 and seq_l % TMD == 0 and nb_l < 8 and TM == 2 * WINDOW
    assert (nb_c * seq_c) % TMD == 0 and TMD % TM == 0

    cond8 = jnp.zeros((8, D_MODEL), F32).at[:nb_l].set(c).at[nb_l].set(c_ctx)
    mod = _modulation(cond8, w_ada, b_ada).reshape(depth * 8, 1, 3 * D_MODEL)

    c_a, s_a = _rope_tables(seq_l, HD_A)
    c_b, s_b = _rope_tables(seq_l, QK_ROPE_B)
    pad_l, pad_r = S_KPE, LANE - S_KPE - QK_ROPE_B
    one, zero = np.ones((seq_l, 1), np.float32), np.zeros((seq_l, 1), np.float32)
    rope_tabs = tuple(jnp.asarray(a) for a in (
        np.tile(c_a, (1, LANE // HD_A)), np.tile(s_a, (1, LANE // HD_A)),
        np.concatenate([np.tile(one, (1, pad_l)), c_b, np.tile(one, (1, pad_r))], 1),
        np.concatenate([np.tile(zero, (1, pad_l)), s_b, np.tile(zero, (1, pad_r))], 1)))

    weights = _prep_weights(w_in, mla_w_uq, mla_w_ukv)
    wukv, wvbt = weights[4], weights[5]
    ng = norm_g.reshape(depth, 1, D_MODEL)
    qn = mla_q_norm.reshape(depth, 1, Q_RANK_B)
    kvn = mla_kv_norm.reshape(depth, 1, KV_RANK_B)
    sink = attn_sink.reshape(depth, 1, N_HEADS_A)
    prow = jnp.stack([gdn_a_log.reshape(depth, -1), gdn_dt_bias.reshape(depth, -1)], axis=1)
    pcol = jnp.swapaxes(prow, 1, 2)
    gn = gdn_norm.reshape(depth, 1, DV_C)
    wa, wb, wc, wo = (w.astype(BF16) for w in (w_branch_a, w_branch_b, w_branch_c, w_out))
    fg = final_norm_g.reshape(1, D_MODEL)
    kxa, vxa = _cache_tiles_a(cache_attn_k, cache_attn_v)
    kpex = jnp.pad(cache_mla_kpe, ((0, 0), (0, 0), (0, 0), (pad_l, pad_r)))

    tps_c, tps_l = seq_c // TM, seq_l // TM
    tpd_l = seq_l // TMD
    y_p = x_prompt.reshape(nb_c * seq_c, D_MODEL)
    y_s = x_sample.reshape(nb_l * seq_l, D_MODEL)
    new_cache = new_state = None
    for l in range(depth):
        last = l == depth - 1

        mod_row_c = lambda i, l=l: l * 8 + nb_l
        outs = _inproj(y_p, l, mod, mod_row_c, ng, weights, qn, kvn, None, tps_c, (depth, seq_c, new_cache))
        (qa, ka, vat, z, hb, qb, kb, vbt, small, cqkv, abt), new_cache = outs[:11], tuple(outs[11:])
        oa, ob = _attn_ctx(qa, ka, vat, qb, kb, vbt, z, sink, l, seq_c)
        u, w, qg, kd, attn, eg = _gdn_local(cqkv, small, abt, gdn_conv, prow, pcol, l, seq_c)
        cf, cb, new_state = _gdn_scan(u, w, qg, kd, attn, eg, None, l, seq_c, (depth, new_state))
        y_p = _merge(y_p, l, mod, mod_row_c, oa, ob, cf, cb, z, hb, weights[0], gn, wa, wb, wc, wo, fg, last)

        mod_row_l = lambda i, l=l: l * 8 + i // tpd_l
        (qa, ka, vat, z, hb, qb, kb, vbt, small, cqkv, abt, qnorm) = _inproj(
            y_s, l, mod, mod_row_l, ng, weights, qn, kvn, rope_tabs, tpd_l)
        oa = _attn_a_lat(qa, qnorm, ka, vat, z, sink, kxa, vxa, l, seq_l)
        kxb, vxb, knx = _kvup(cache_mla_ckv, kpex, wukv, wvbt, l)
        ob = _attn_b_lat(qb, qnorm, kb, vbt, z, kxb, vxb, knx, seq_l, TM, N_HEADS_B // 2)
        u, w, qg, kd, attn, eg = _gdn_local(cqkv, small, abt, gdn_conv, prow, pcol, l, seq_l)
        cf, cb = _gdn_scan(u, w, qg, kd, attn, eg, state_gdn, l, seq_l)
        y_s = _merge(y_s, l, mod, mod_row_l, oa, ob, cf, cb, z, hb, weights[0], gn, wa, wb, wc, wo, fg, last)

    new_k, new_v, new_ckv, new_kpe = new_cache
    kv_shape = (nb_c, depth, seq_c, N_KV_A, HD_A)
    return (y_p.reshape(nb_c, seq_c, D_MODEL), y_s.reshape(nb_l, seq_l, D_MODEL),
            new_k.reshape(kv_shape), new_v.reshape(kv_shape), new_ckv, new_kpe, new_state)
```

```python
import functools

import numpy as np
import jax
import jax.numpy as jnp
from jax import lax
from jax.experimental import pallas as pl
from jax.experimental.pallas import tpu as pltpu

F32 = jnp.float32
BF16 = jnp.bfloat16

D_MODEL = 1024
GRID_W = 64
ROPE_BASE = 10000.0
EPS = 1e-6
NEG_INF = -1e30
N_HEADS_A = 8
N_KV_A = 2
HD_A = 64
GQA_GROUP = N_HEADS_A // N_KV_A
WINDOW = 128
N_HEADS_B = 8
QK_NOPE_B = 64
QK_ROPE_B = 32
V_HD_B = 64
Q_RANK_B = 384
KV_RANK_B = 256
MLA_SCALE = (QK_NOPE_B + QK_ROPE_B) ** -0.5
N_HEADS_C = 4
DK_C = 128
DV_C = 128
CHUNK = 64
W_A = N_HEADS_A * HD_A
W_B = N_HEADS_B * V_HD_B
W_C = N_HEADS_C * DV_C
QKV_C = 2 * N_HEADS_C * DK_C + W_C
IN_SIZES = (W_A, N_KV_A * HD_A, N_KV_A * HD_A, W_A, Q_RANK_B, KV_RANK_B, QK_ROPE_B, W_B, QKV_C,
            2 * N_HEADS_C, 2 * N_HEADS_C, W_C, 3 * D_MODEL)

LANE = 128
HALF = LANE // 2
TM = 256
TMD = 512
TMG = 256
CPT = TM // CHUNK
SCAN_SEQS = 4
HALO = 16
MLA_HW = 128
KA_COLS = 4 * LANE
HB_COLS = N_HEADS_B * MLA_HW
VT_ONES = 16
VT_ROWS = V_HD_B + VT_ONES
VAT_ROWS = N_KV_A * VT_ROWS
VBT_ROWS = N_HEADS_B * VT_ROWS
LOG2E = 1.4426950408889634
SHIFT_MAX = 60.0
SHIFT_SLACK = 1.001
SHIFT_SLACK_ABS = 0.01
N_QB = 0
N_KB = N_QB + N_HEADS_B
N_QA = N_KB + N_HEADS_B
N_KA = N_QA + N_HEADS_A
NORM_COLS = 32

P_GATES = 0
W_GATES = 3 * D_MODEL
P_QKV = P_GATES + W_GATES
A_COLS = W_A + 3 * LANE
P_Z = P_QKV + A_COLS
W_Z = W_A + W_B + W_C
P_CQ = P_Z + W_Z
P_CKV = P_CQ + Q_RANK_B
P_SMALL = P_CKV + KV_RANK_B
P_CQKV = P_SMALL + LANE
P_END = P_CQKV + QKV_C
S_KPE = 64
S_A = 96
S_B = 104


def _sigmoid(x):
    return 0.5 * jnp.tanh(0.5 * x) + 0.5


def _silu(x):
    return x * _sigmoid(x)


def _softplus(x):
    return jnp.maximum(x, 0.0) + jnp.log(1.0 + jnp.exp(-jnp.abs(x)))


def _dot(a, b):
    return jnp.dot(a, b, preferred_element_type=F32)


def _dot_nt(a, b):
    return lax.dot_general(a, b, (((1,), (1,)), ((), ())), preferred_element_type=F32)


def _bdot(a, b):
    return lax.dot_general(a, b, (((2,), (1,)), ((0,), (0,))), preferred_element_type=F32)


def _bdot_nt(a, b):
    return lax.dot_general(a, b, (((2,), (2,)), ((0,), (0,))), preferred_element_type=F32)


def _bdot_tn(a, b):
    return lax.dot_general(a, b, (((1,), (1,)), ((0,), (0,))), preferred_element_type=F32)


def _dot_exact(a, b):
    return jnp.dot(a, b, preferred_element_type=F32, precision=lax.Precision.HIGHEST)


def _row_sumsq(x16):
    x = x16.astype(F32)
    return jnp.sum(x * x, axis=-1, keepdims=True)


def _row_norm(x16):
    return jnp.sqrt(_row_sumsq(x16))


def _rope(x, c, s):
    n = x.shape[-1]
    lane = lax.broadcasted_iota(jnp.int32, x.shape, 1)
    swapped = jnp.where(lane % 2 == 0, pltpu.roll(x, n - 1, 1), pltpu.roll(x, 1, 1))
    return x * c + swapped * s


def _mod_kernel(cond_ref, w_ref, b_ref, out_ref):
    cnd = cond_ref[...]
    out_ref[0] = _dot(_silu(cnd).astype(BF16), w_ref[0].astype(BF16)) + b_ref[0]


def _modulation(cond8, w_ada, b_ada):
    depth = w_ada.shape[0]
    tn = 768
    return pl.pallas_call(
        _mod_kernel,
        grid=(depth, 3 * D_MODEL // tn),
        in_specs=[pl.BlockSpec((8, D_MODEL), lambda l, n: (0, 0)),
                  pl.BlockSpec((1, D_MODEL, tn), lambda l, n: (l, 0, n)),
                  pl.BlockSpec((1, 1, tn), lambda l, n: (l, 0, n))],
        out_specs=pl.BlockSpec((1, 8, tn), lambda l, n: (l, 0, n)),
        out_shape=jax.ShapeDtypeStruct((depth, 8, 3 * D_MODEL), F32),
        name="adaln_mod",
    )(cond8, w_ada, b_ada.reshape(depth, 1, 3 * D_MODEL))


def _inproj_kernel(*refs, rope, cache_seq, n_carry, layer, fill_depth):
    it = iter(refs)
    (x_ref, mod_ref, ng_ref, wp_ref, wab_ref, wvat_ref, wuq_ref, wukv_ref, wvbt_ref, qn_ref,
     kvn_ref) = (next(it) for _ in range(11))
    if rope:
        ca_ref, sa_ref, cb_ref, sb_ref = (next(it) for _ in range(4))
    for _ in range(n_carry):
        next(it)
    (qa_ref, ka_ref, vat_ref, z_ref, hb_ref, qb_ref, kb_ref, vbt_ref, small_ref, cqkv_ref,
     abt_ref) = (next(it) for _ in range(11))
    if rope:
        qnorm_ref = next(it)
    if cache_seq:
        ck_ref, cv_ref, cckv_ref, ckpe_ref = (next(it) for _ in range(4))

    def to_cache(ref, val):
        for s in range(TMD // cache_seq):
            rows = val[s * cache_seq:(s + 1) * cache_seq]
            if fill_depth:
                for k in range(fill_depth):
                    ref[s, k] = rows if k == layer else jnp.zeros_like(rows)
            else:
                ref[s] = rows

    x = x_ref[...]
    mod = mod_ref[0]
    shift, scale = mod[:, :D_MODEL], mod[:, D_MODEL:2 * D_MODEL]
    xn = x * lax.rsqrt(jnp.mean(x * x, axis=-1, keepdims=True) + EPS) * ng_ref[...]
    hb = (xn * (1.0 + scale) + shift).astype(BF16)
    lane = lax.broadcasted_iota(jnp.int32, (TMD, LANE), 1)
    lo = lane < HALF

    def mm(lo_col, hi_col):
        return _dot_nt(hb, wp_ref[lo_col:hi_col, :])

    r = mm(P_QKV, P_QKV + A_COLS)
    tiles = [r[:, t * LANE:(t + 1) * LANE] for t in range(A_COLS // LANE)]
    if cache_seq:
        to_cache(ck_ref, tiles[4])
        to_cache(cv_ref, tiles[6])
    if rope:
        ca, sa = ca_ref[...], sa_ref[...]
        tiles[:6] = [_rope(t, ca, sa) for t in tiles[:6]]
    for t in range(4):
        q16 = (tiles[t] * (HD_A ** -0.5 * LOG2E)).astype(BF16)
        qa_ref[:, t * LANE:(t + 1) * LANE] = q16
        if rope:
            sq = jnp.square(q16.astype(F32))
            qnorm_ref[:, N_QA + 2 * t:N_QA + 2 * t + 1] = jnp.sum(jnp.where(lo, sq, 0.0), -1, keepdims=True)
            qnorm_ref[:, N_QA + 2 * t + 1:N_QA + 2 * t + 2] = jnp.sum(jnp.where(lo, 0.0, sq), -1, keepdims=True)
    k01, k10 = tiles[4], tiles[5]
    k0_16 = jnp.where(lo, k01, 0.0).astype(BF16)
    k1_16 = jnp.where(lo, k10, 0.0).astype(BF16)
    ka_ref[:, 0 * LANE:1 * LANE] = k0_16
    ka_ref[:, 1 * LANE:2 * LANE] = jnp.where(lo, 0.0, k10).astype(BF16)
    ka_ref[:, 2 * LANE:3 * LANE] = k1_16
    ka_ref[:, 3 * LANE:4 * LANE] = jnp.where(lo, 0.0, k01).astype(BF16)
    if rope:
        qnorm_ref[:, N_KA:N_KA + 1] = _row_sumsq(k0_16)
        qnorm_ref[:, N_KA + 1:N_KA + 2] = _row_sumsq(k1_16)
        qnorm_ref[:, N_KA + N_KV_A:] = jnp.zeros((TMD, NORM_COLS - N_KA - N_KV_A), F32)
    ones = jnp.ones((VT_ONES, TMD), BF16)
    vt = _dot_nt(wvat_ref[...], hb)
    for g in range(N_KV_A):
        vat_ref[g * VT_ROWS:g * VT_ROWS + HD_A] = vt[g * HD_A:(g + 1) * HD_A].astype(BF16)
        vat_ref[g * VT_ROWS + HD_A:(g + 1) * VT_ROWS] = ones

    for t in range(3):
        z_ref[:, t * 512:(t + 1) * 512] = mm(P_Z + t * 512, P_Z + (t + 1) * 512).astype(BF16)
    hb_ref[...] = hb

    r = mm(P_CQ, P_CQ + Q_RANK_B)
    qn = r * lax.rsqrt(jnp.mean(r * r, axis=-1, keepdims=True) + EPS) * qn_ref[...]
    q = _dot(qn.astype(BF16), wuq_ref[...])
    if rope:
        cb, sb = cb_ref[...], sb_ref[...]
        for h in range(N_HEADS_B):
            seg = (_rope(q[:, h * MLA_HW:(h + 1) * MLA_HW], cb, sb) * (MLA_SCALE * LOG2E)).astype(BF16)
            qb_ref[:, h * MLA_HW:(h + 1) * MLA_HW] = seg
            qnorm_ref[:, N_QB + h:N_QB + h + 1] = _row_sumsq(seg)
    else:
        qb_ref[...] = (q * (MLA_SCALE * LOG2E)).astype(BF16)

    r = mm(P_SMALL, P_SMALL + LANE)
    small_ref[...] = r
    if cache_seq:
        to_cache(ckpe_ref, r[:, S_KPE:S_KPE + QK_ROPE_B])
    kp = _rope(r, cb, sb) if rope else r
    kp = jnp.where((lane >= S_KPE) & (lane < S_KPE + QK_ROPE_B), kp, 0.0)

    r = mm(P_CKV, P_CKV + KV_RANK_B)
    cn = r * lax.rsqrt(jnp.mean(r * r, axis=-1, keepdims=True) + EPS) * kvn_ref[...]
    if cache_seq:
        to_cache(cckv_ref, cn)
    cn16 = cn.astype(BF16)
    kv = _dot(cn16, wukv_ref[...])
    vt = _dot_nt(wvbt_ref[...], cn16)
    for h in range(N_HEADS_B):
        k16 = (kv[:, h * MLA_HW:(h + 1) * MLA_HW] + kp).astype(BF16)
        kb_ref[:, h * MLA_HW:(h + 1) * MLA_HW] = k16
        if rope:
            qnorm_ref[:, N_KB + h:N_KB + h + 1] = _row_sumsq(k16)
        vbt_ref[h * VT_ROWS:h * VT_ROWS + V_HD_B] = vt[h * V_HD_B:(h + 1) * V_HD_B].astype(BF16)
        vbt_ref[h * VT_ROWS + V_HD_B:(h + 1) * VT_ROWS] = ones

    for t in range(3):
        cqkv_ref[:, t * 512:(t + 1) * 512] = mm(P_CQKV + t * 512, P_CQKV + (t + 1) * 512).astype(BF16)
    if rope:
        qnorm_ref[...] = jnp.sqrt(qnorm_ref[...])

    for c in range(TMD // CHUNK):
        abt_ref[c] = _dot_nt(wab_ref[...], hb[c * CHUNK:(c + 1) * CHUNK])


def _layer_spec(arr, l):
    nd = arr.ndim - 1
    return pl.BlockSpec((None,) + arr.shape[1:], lambda *_: (l,) + (0,) * nd, pipeline_mode=pl.Buffered(1))


def _inproj(x2d, l, mod, mod_row_fn, ng, weights, qn, kvn, rope_tabs, tiles_per_seq, cache=None):
    t = x2d.shape[0]
    nt = t // TMD
    rope = rope_tabs is not None
    row = lambda i: (i, 0)
    col = lambda i: (0, i)
    wp, wab, wvat, wuq, wukv, wvbt = weights
    params = (ng, wp, wab, wvat, wuq, wukv, wvbt, qn, kvn)
    in_specs = [pl.BlockSpec((TMD, D_MODEL), row),
                pl.BlockSpec((1, 1, 3 * D_MODEL), lambda i: (mod_row_fn(i), 0, 0))]
    in_specs += [_layer_spec(a, l) for a in params]
    args = [x2d, mod, *params]
    if rope:
        pos = lambda i: (i % tiles_per_seq, 0)
        in_specs += [pl.BlockSpec((TMD, LANE), pos)] * 4
        args += list(rope_tabs)
    outs = [(W_A, BF16, False), (KA_COLS, BF16, False), (VAT_ROWS, BF16, True), (W_Z, BF16, False),
            (D_MODEL, BF16, False), (HB_COLS, BF16, False), (HB_COLS, BF16, False), (VBT_ROWS, BF16, True),
            (LANE, F32, False), (QKV_C, BF16, False)]
    out_shape = [jax.ShapeDtypeStruct((w, t) if tr else (t, w), dt) for w, dt, tr in outs]
    out_specs = [pl.BlockSpec((w, TMD), col) if tr else pl.BlockSpec((TMD, w), row) for w, _, tr in outs]
    out_shape.append(jax.ShapeDtypeStruct((t // CHUNK, 16, CHUNK), F32))
    out_specs.append(pl.BlockSpec((TMD // CHUNK, 16, CHUNK), lambda i: (i, 0, 0)))
    if rope:
        out_shape.append(jax.ShapeDtypeStruct((t, NORM_COLS), F32))
        out_specs.append(pl.BlockSpec((TMD, NORM_COLS), row))
    aliases, cache_seq, n_carry, fill_depth = {}, 0, 0, 0
    if cache is not None:
        depth, cache_seq, carry = cache
        spt = TMD // cache_seq
        if carry is None:
            fill_depth, carry = depth, ()
        for w in (N_KV_A * HD_A, N_KV_A * HD_A, KV_RANK_B, QK_ROPE_B):
            out_shape.append(jax.ShapeDtypeStruct((t // cache_seq, depth, cache_seq, w), F32))
            if fill_depth:
                out_specs.append(pl.BlockSpec((spt, depth, cache_seq, w), lambda i: (i, 0, 0, 0)))
            else:
                out_specs.append(pl.BlockSpec((spt, None, cache_seq, w), lambda i: (i, l, 0, 0)))
        n_carry = len(carry)
        for k, a in enumerate(carry):
            aliases[len(args)] = len(out_shape) - n_carry + k
            in_specs.append(pl.BlockSpec(memory_space=pl.ANY))
            args.append(a)
    return pl.pallas_call(
        functools.partial(_inproj_kernel, rope=rope, cache_seq=cache_seq, n_carry=n_carry, layer=l,
                          fill_depth=fill_depth),
        grid=(nt,),
        in_specs=in_specs,
        out_specs=out_specs,
        out_shape=out_shape,
        input_output_aliases=aliases,
        compiler_params=pltpu.CompilerParams(dimension_semantics=("parallel",)),
        name="inproj_ctx" if cache is not None else "inproj_lat",
    )(*args)


def _kvup_kernel(c_ref, kpe_ref, w_ref, wvt_ref, k_ref, vt_ref, kn_ref):
    c16 = c_ref[...].astype(BF16)
    kv = _dot(c16, w_ref[...])
    vt = _dot_nt(wvt_ref[...], c16)
    kp = kpe_ref[...]
    ones = jnp.ones((VT_ONES, c16.shape[0]), BF16)
    for h in range(N_HEADS_B):
        k16 = (kv[:, h * MLA_HW:(h + 1) * MLA_HW] + kp).astype(BF16)
        k_ref[:, h * MLA_HW:(h + 1) * MLA_HW] = k16
        kn_ref[:, h:h + 1] = _row_sumsq(k16)
        vt_ref[h * VT_ROWS:h * VT_ROWS + V_HD_B] = vt[h * V_HD_B:(h + 1) * V_HD_B].astype(BF16)
        vt_ref[h * VT_ROWS + V_HD_B:(h + 1) * VT_ROWS] = ones
    kn_ref[...] = jnp.sqrt(kn_ref[...])


def _kvup(ckv, kpe, wukv, wvbt, l):
    nb, _, past, _ = ckv.shape
    return pl.pallas_call(
        _kvup_kernel,
        grid=(nb,),
        in_specs=[pl.BlockSpec((None, None, past, KV_RANK_B), lambda b: (b, l, 0, 0)),
                  pl.BlockSpec((None, None, past, LANE), lambda b: (b, l, 0, 0)),
                  _layer_spec(wukv, l), _layer_spec(wvbt, l)],
        out_specs=[pl.BlockSpec((past, HB_COLS), lambda b: (b, 0)),
                   pl.BlockSpec((None, VBT_ROWS, past), lambda b: (b, 0, 0)),
                   pl.BlockSpec((past, N_HEADS_B), lambda b: (b, 0))],
        out_shape=[jax.ShapeDtypeStruct((nb * past, HB_COLS), BF16),
                   jax.ShapeDtypeStruct((nb, VBT_ROWS, past), BF16),
                   jax.ShapeDtypeStruct((nb * past, N_HEADS_B), F32)],
        name="mla_cache_up",
    )(ckv, kpe, wukv, wvbt)


def _scores_t(q_tiles, k_tiles, bias_t):
    st = _bdot_nt(jnp.stack(k_tiles), jnp.stack(q_tiles))
    return st if bias_t is None else st + bias_t[None]


def _softmax_pv(sts, vts, sink, shift=None):
    if shift is None:
        m = jnp.max(sts[0], axis=1, keepdims=True)
        for st in sts[1:]:
            m = jnp.maximum(m, jnp.max(st, axis=1, keepdims=True))
    else:
        m = shift
    if sink is not None:
        m = jnp.maximum(m, sink)
    ot = _bdot(jnp.stack(vts[0]), jnp.exp2(sts[0] - m).astype(BF16))
    for st, vt in zip(sts[1:], vts[1:]):
        ot = ot + _bdot(jnp.stack(vt), jnp.exp2(st - m).astype(BF16))
    den = ot[:, V_HD_B:V_HD_B + 1, :]
    if sink is not None:
        den = den + jnp.exp2(sink - m)
    num = ot[:, :V_HD_B, :] / den
    return [jnp.concatenate([num[2 * i], num[2 * i + 1]], axis=0).T for i in range(sts[0].shape[0] // 2)]


def _tile(x, t):
    return x[:, t * LANE:(t + 1) * LANE]


def _attn_a_heads(q, segments, sink_ref, shift=None):
    sts, vts = [], []
    for ka, vat, bias_t in segments:
        qs, ks, vs = [], [], []
        for t in range(N_HEADS_A // 2):
            g = (2 * t) // GQA_GROUP
            for e in range(2):
                qs.append(_tile(q, t))
                ks.append(_tile(ka, 2 * g + e))
                vs.append(vat[g * VT_ROWS:(g + 1) * VT_ROWS])
        sts.append(_scores_t(qs, ks, bias_t))
        vts.append(vs)
    sink = jnp.stack([sink_ref[:, h:h + 1] * LOG2E for h in range(N_HEADS_A)])
    return _softmax_pv(sts, vts, sink, shift)


def _gated_store(outs, z_ref, o_ref, first_tile=0):
    for i, o in enumerate(outs):
        t = first_tile + i
        z = _tile(z_ref, t).astype(F32)
        o_ref[:, t * LANE:(t + 1) * LANE] = (o * _silu(z)).astype(BF16)


def _attn_ctx_kernel(qa_ref, ka_ref, vat_ref, qb_ref, kb_ref, vbt_ref, za_ref, zb_ref, sink_ref, oa_ref, ob_ref):
    _gated_store(_attn_a_heads(qa_ref[...], [(ka_ref[...], vat_ref[...], None)], sink_ref), za_ref, oa_ref)
    q, kb, vbt = qb_ref[...], kb_ref[...], vbt_ref[...]
    heads = range(N_HEADS_B)
    st = _scores_t([_tile(q, h) for h in heads], [_tile(kb, h) for h in heads], None)
    _gated_store(_softmax_pv([st], [[vbt[h * VT_ROWS:(h + 1) * VT_ROWS] for h in heads]], None), zb_ref, ob_ref)


def _attn_ctx(qa, ka, vat, qb, kb, vbt, z, sink, l, seq):
    t = qa.shape[0]
    row = lambda b: (b, 0)
    col = lambda b: (0, b)
    return pl.pallas_call(
        _attn_ctx_kernel,
        grid=(t // seq,),
        in_specs=[pl.BlockSpec((seq, W_A), row),
                  pl.BlockSpec((seq, KA_COLS), row),
                  pl.BlockSpec((VAT_ROWS, seq), col),
                  pl.BlockSpec((seq, HB_COLS), row),
                  pl.BlockSpec((seq, HB_COLS), row),
                  pl.BlockSpec((VBT_ROWS, seq), col),
                  pl.BlockSpec((seq, W_A), row),
                  pl.BlockSpec((seq, W_B), lambda b: (b, 1)),
                  _layer_spec(sink, l)],
        out_specs=[pl.BlockSpec((seq, W_A), row), pl.BlockSpec((seq, W_B), row)],
        out_shape=[jax.ShapeDtypeStruct((t, W_A), BF16), jax.ShapeDtypeStruct((t, W_B), BF16)],
        compiler_params=pltpu.CompilerParams(dimension_semantics=("parallel",)),
        name="attn_ctx",
    )(qa, ka, vat, qb, kb, vbt, z, z, sink)


def _attn_a_lat_kernel(q_ref, qn_ref, ksn_ref, kp_ref, kc_ref, kn_ref, vp_ref, vc_ref, vn_ref, kx_ref, vx_ref,
                       z_ref, sink_ref, o_ref, knorm_scr, *, nq):
    j = pl.program_id(1)

    @pl.when(j == 0)
    def _():
        kmax = jnp.max(ksn_ref[:, N_KA:N_KA + N_KV_A], axis=0, keepdims=True)
        kxmax = [jnp.max(_row_norm(kx_ref[:, 2 * g * LANE:(2 * g + 1) * LANE]), axis=0, keepdims=True)
                 for g in range(N_KV_A)]
        knorm_scr[...] = jnp.broadcast_to(jnp.maximum(kmax, jnp.concatenate(kxmax, axis=1)), knorm_scr.shape)

    qmax = jnp.max(qn_ref[:, N_QA:N_QA + N_HEADS_A], axis=0, keepdims=True)
    bound = jnp.stack([qmax[:, h:h + 1] * knorm_scr[0:1, h // GQA_GROUP:h // GQA_GROUP + 1] * SHIFT_SLACK
                       + SHIFT_SLACK_ABS for h in range(N_HEADS_A)])
    bound_ok = jnp.max(bound) <= SHIFT_MAX

    ka = jnp.concatenate([kp_ref[...], kc_ref[...], kn_ref[...]], axis=0)
    vat = jnp.concatenate([vp_ref[...], vc_ref[...], vn_ref[...]], axis=1)
    kj = lax.broadcasted_iota(jnp.int32, (ka.shape[0], TM), 0)
    qi = lax.broadcasted_iota(jnp.int32, (ka.shape[0], TM), 1)
    ok = (kj >= qi) & (kj <= qi + 2 * WINDOW)
    ok = ok & ((kj >= WINDOW) | (j > 0)) & ((kj < TM + WINDOW) | (j < nq - 1))
    bias_t = jnp.where(ok, 0.0, NEG_INF)
    segments = [(ka, vat, bias_t), (kx_ref[...], vx_ref[...], None)]

    @pl.when(bound_ok)
    def _():
        _gated_store(_attn_a_heads(q_ref[...], segments, sink_ref, bound), z_ref, o_ref)

    @pl.when(jnp.logical_not(bound_ok))
    def _():
        _gated_store(_attn_a_heads(q_ref[...], segments, sink_ref), z_ref, o_ref)


def _attn_a_lat(qa, norms, ka, vat, z, sink, kx, vxt, l, seq):
    t = qa.shape[0]
    nq = seq // TM
    past = kx.shape[2]
    r = TM // WINDOW
    row = lambda b, j: (b * nq + j, 0)
    prev = lambda b, j: ((b * nq + j) * r - jnp.where(j > 0, 1, 0), 0)
    nxt = lambda b, j: ((b * nq + j) * r + jnp.where(j < nq - 1, r, r - 1), 0)
    swap = lambda f: (lambda b, j: f(b, j)[::-1])
    return pl.pallas_call(
        functools.partial(_attn_a_lat_kernel, nq=nq),
        grid=(t // seq, nq),
        in_specs=[pl.BlockSpec((TM, W_A), row),
                  pl.BlockSpec((TM, NORM_COLS), row),
                  pl.BlockSpec((seq, NORM_COLS), lambda b, j: (b, 0)),
                  pl.BlockSpec((WINDOW, KA_COLS), prev),
                  pl.BlockSpec((TM, KA_COLS), row),
                  pl.BlockSpec((WINDOW, KA_COLS), nxt),
                  pl.BlockSpec((VAT_ROWS, WINDOW), swap(prev)),
                  pl.BlockSpec((VAT_ROWS, TM), swap(row)),
                  pl.BlockSpec((VAT_ROWS, WINDOW), swap(nxt)),
                  pl.BlockSpec((None, None, past, KA_COLS), lambda b, j: (b, l, 0, 0)),
                  pl.BlockSpec((None, None, VAT_ROWS, past), lambda b, j: (b, l, 0, 0)),
                  pl.BlockSpec((TM, W_A), row),
                  _layer_spec(sink, l)],
        out_specs=pl.BlockSpec((TM, W_A), row),
        out_shape=jax.ShapeDtypeStruct((t, W_A), BF16),
        scratch_shapes=[pltpu.VMEM((8, N_KV_A), F32)],
        compiler_params=pltpu.CompilerParams(dimension_semantics=("parallel", "arbitrary")),
        name="attn_a_lat",
    )(qa, norms, norms, ka, ka, ka, vat, vat, vat, kx, vxt, z, sink)


def _attn_b_lat_kernel(q_ref, qn_ref, kn_ref, knx_ref, k_ref, vt_ref, kx_ref, vxt_ref, z_ref, o_ref, knorm_scr, *,
                       group):
    segments = [(k_ref, vt_ref), (kx_ref, vxt_ref)]
    q = q_ref[...]

    @pl.when(pl.program_id(1) == 0)
    def _():
        kmax = jnp.maximum(jnp.max(kn_ref[:, N_KB:N_KB + N_HEADS_B], axis=0, keepdims=True),
                           jnp.max(knx_ref[...], axis=0, keepdims=True))
        knorm_scr[...] = jnp.broadcast_to(kmax, knorm_scr.shape)

    qmax = jnp.max(qn_ref[:, N_QB:N_QB + N_HEADS_B], axis=0, keepdims=True)
    bound = qmax * knorm_scr[0:1, :] * SHIFT_SLACK + SHIFT_SLACK_ABS
    bound = jnp.stack([bound[:, h:h + 1] for h in range(N_HEADS_B)])
    bound_ok = jnp.max(bound) <= SHIFT_MAX

    def scores(h0):
        heads = range(h0, h0 + group)
        return [_scores_t([_tile(q, h) for h in heads], [_tile(kr, h) for h in heads], None) for kr, _ in segments]

    def attend(shift):
        sts = scores(0)
        for h0 in range(0, N_HEADS_B, group):
            sts_next = scores(h0 + group) if h0 + group < N_HEADS_B else None
            vts = [[vr[h * VT_ROWS:(h + 1) * VT_ROWS, :] for h in range(h0, h0 + group)] for _, vr in segments]
            outs = _softmax_pv(sts, vts, None, None if shift is None else shift[h0:h0 + group])
            _gated_store(outs, z_ref, o_ref, h0 // 2)
            sts = sts_next

    @pl.when(bound_ok)
    def _():
        attend(bound)

    @pl.when(jnp.logical_not(bound_ok))
    def _():
        attend(None)


def _attn_b_lat(qb, norms, kb, vbt, z, kx, vxt, knx, seq, qblk, group):
    t = qb.shape[0]
    nq = seq // qblk
    past = kx.shape[0] // (t // seq)
    hw = HB_COLS
    return pl.pallas_call(
        functools.partial(_attn_b_lat_kernel, group=group),
        grid=(t // seq, nq),
        in_specs=[pl.BlockSpec((qblk, hw), lambda b, j: (b * nq + j, 0)),
                  pl.BlockSpec((qblk, NORM_COLS), lambda b, j: (b * nq + j, 0)),
                  pl.BlockSpec((seq, NORM_COLS), lambda b, j: (b, 0)),
                  pl.BlockSpec((past, N_HEADS_B), lambda b, j: (b, 0)),
                  pl.BlockSpec((seq, hw), lambda b, j: (b, 0)),
                  pl.BlockSpec((VBT_ROWS, seq), lambda b, j: (0, b)),
                  pl.BlockSpec((past, hw), lambda b, j: (b, 0)),
                  pl.BlockSpec((None, VBT_ROWS, past), lambda b, j: (b, 0, 0)),
                  pl.BlockSpec((qblk, W_B), lambda b, j: (b * nq + j, 1))],
        out_specs=pl.BlockSpec((qblk, W_B), lambda b, j: (b * nq + j, 0)),
        out_shape=jax.ShapeDtypeStruct((t, W_B), BF16),
        scratch_shapes=[pltpu.VMEM((8, N_HEADS_B), F32)],
        compiler_params=pltpu.CompilerParams(dimension_semantics=("parallel", "arbitrary")),
        name="attn_b_lat",
    )(qb, norms, norms, knx, kb, vbt, kx, vxt, z)


def _gdn_local_kernel(cq_ref, prev_ref, next_ref, small_ref, abt_ref, cw_ref, prow_ref, pcol_ref,
                      u_ref, w_ref, qg_ref, kd_ref, attn_ref, eg_ref, qkv_scr, gb_scr, *, seq):
    x = cq_ref[...].astype(F32)
    tiles_per_seq = seq // TMG
    tpos = pl.program_id(0) % tiles_per_seq
    prev_row = jnp.where(tpos > 0, prev_ref[...].astype(F32)[HALO - 1:HALO, :], 0.0)
    next_row = jnp.where(tpos < tiles_per_seq - 1, next_ref[...].astype(F32)[0:1, :], 0.0)
    rows = lax.broadcasted_iota(jnp.int32, (TMG, 1), 0)
    xm1 = jnp.where(rows == 0, prev_row, pltpu.roll(x, 1, 0))
    xp1 = jnp.where(rows == TMG - 1, next_row, pltpu.roll(x, TMG - 1, 0))
    cw = cw_ref[...]
    y = _silu(xm1 * cw[0:1] + x * cw[1:2] + xp1 * cw[2:3])
    nq = N_HEADS_C * DK_C
    for h in range(N_HEADS_C):
        qh = y[:, h * DK_C:(h + 1) * DK_C]
        kh = y[:, nq + h * DK_C:nq + (h + 1) * DK_C]
        qkv_scr[:, h * DK_C:(h + 1) * DK_C] = (
            qh * lax.rsqrt(jnp.sum(qh * qh, axis=-1, keepdims=True) + EPS) * (DK_C ** -0.5))
        qkv_scr[:, nq + h * DK_C:nq + (h + 1) * DK_C] = kh * lax.rsqrt(jnp.sum(kh * kh, axis=-1, keepdims=True) + EPS)
    qkv_scr[:, 2 * nq:] = y[:, 2 * nq:]

    sm = small_ref[...]
    prow = prow_ref[...]
    gb_scr[:, 0:8] = -jnp.exp(prow[0:1]) * _softplus(sm[:, S_A:S_A + 8] + prow[1:2])
    gb_scr[:, 8:16] = _sigmoid(sm[:, S_B:S_B + 8])
    pcol = pcol_ref[...]

    ri = lax.broadcasted_iota(jnp.int32, (CHUNK, LANE), 0)
    lane = lax.broadcasted_iota(jnp.int32, (CHUNK, LANE), 1)
    fwd = lane < CHUNK
    cj = lane & (CHUNK - 1)
    incl = (fwd & (ri >= cj)) | (~fwd & (ri <= cj))
    strict = (fwd & (ri > cj)) | (~fwd & (ri < cj))
    xor = ri ^ cj
    eye = (ri == cj).astype(F32)
    r2 = lax.broadcasted_iota(jnp.int32, (2 * CHUNK, LANE), 0)
    l2 = lax.broadcasted_iota(jnp.int32, (2 * CHUNK, LANE), 1)
    same_dir = (r2 < CHUNK) == (l2 < CHUNK)
    rs_ = lax.broadcasted_iota(jnp.int32, (CHUNK, CHUNK), 0)
    cs_ = lax.broadcasted_iota(jnp.int32, (CHUNK, CHUNK), 1)
    tril = (rs_ >= cs_).astype(F32)
    triu = (rs_ <= cs_).astype(F32)
    tri_rows = jnp.concatenate([triu, tril], axis=1)
    dup_rows = jnp.concatenate([(rs_ == cs_).astype(F32)] * 2, axis=1)

    def block_diag(x):
        return jnp.where(same_dir[None], jnp.concatenate([x, x], axis=1), 0.0).astype(BF16)

    lows, rhss, order = [], [], []

    def solve():
        low = jnp.stack(lows, axis=0)
        inv = eye[None] - jnp.where(xor[None] == 1, low, 0.0)
        b = 2
        while b < CHUNK:
            cpl = jnp.where((xor[None] >= b) & (xor[None] < 2 * b), low, 0.0)
            tmp = _bdot(cpl.astype(BF16), block_diag(inv))
            inv = inv - _bdot(inv.astype(BF16), block_diag(tmp))
            b *= 2
        scale_u, scale_w, vs, ks = (jnp.stack(a, axis=0) for a in zip(*rhss))
        u = _bdot((inv * scale_u).astype(BF16), vs)
        w = _bdot((inv * scale_w).astype(BF16), ks)
        for i, (rs, h) in enumerate(order):
            for d in range(2):
                cs = slice((d * N_HEADS_C + h) * DK_C, (d * N_HEADS_C + h + 1) * DK_C)
                u_ref[rs, cs] = u[i, :, d * DV_C:(d + 1) * DV_C].astype(BF16)
                w_ref[rs, cs] = w[i, :, d * DK_C:(d + 1) * DK_C].astype(BF16)

    for c in range(TMG // CHUNK):
        rs = slice(c * CHUNK, (c + 1) * CHUNK)
        gcol = gb_scr[rs, 0:8]
        bcol = gb_scr[rs, 8:16]
        abt = abt_ref[c]
        grow = -jnp.exp(pcol[:, 0:1]) * _softplus(abt[0:8] + pcol[:, 1:2])
        gc_f = _dot_exact(tril, gcol)
        gc_b = _dot_exact(triu, gcol)
        gr = _dot_exact(grow, tri_rows)
        br = _dot_exact(_sigmoid(abt[8:16]), dup_rows)
        for h in range(N_HEADS_C):
            hb_ = N_HEADS_C + h
            q = qkv_scr[rs, h * DK_C:(h + 1) * DK_C]
            k = qkv_scr[rs, nq + h * DK_C:nq + (h + 1) * DK_C]
            v = qkv_scr[rs, 2 * nq + h * DV_C:2 * nq + (h + 1) * DV_C]
            k16 = k.astype(BF16)
            kk16 = jnp.concatenate([k16, k16], axis=0)
            kk = _dot_nt(k16, kk16)
            qk = _dot_nt(q.astype(BF16), kk16)
            gcs = (gc_f[:, h:h + 1], gc_b[:, hb_:hb_ + 1])
            betas = (bcol[:, h:h + 1], bcol[:, hb_:hb_ + 1])
            gc2 = jnp.where(fwd, gcs[0], gcs[1])
            gr2 = jnp.where(fwd[0:1], gr[h:h + 1, :], gr[hb_:hb_ + 1, :])
            decay = jnp.where(incl, jnp.exp(jnp.where(incl, gc2 - gr2, 0.0)), 0.0)
            lows.append(jnp.where(strict, jnp.where(fwd, betas[0], betas[1]) * kk * decay, 0.0))
            attn_ref[rs, h * LANE:(h + 1) * LANE] = (qk * decay).astype(BF16)
            order.append((rs, h))
            br2 = jnp.where(fwd[0:1], br[h:h + 1, :], br[hb_:hb_ + 1, :])
            v16 = v.astype(BF16)
            zeros = jnp.zeros_like(v16)
            rhss.append((br2, br2 * jnp.exp(gr2),
                         jnp.concatenate([jnp.concatenate([v16, zeros], axis=1),
                                          jnp.concatenate([zeros, v16], axis=1)], axis=0),
                         jnp.concatenate([jnp.concatenate([k16, zeros], axis=1),
                                          jnp.concatenate([zeros, k16], axis=1)], axis=0)))
            for d in range(2):
                dh = d * N_HEADS_C + h
                gc = gcs[d]
                eg = jnp.exp(gc)
                g_last = gc[CHUNK - 1:CHUNK] if d == 0 else gc[0:1]
                cs = slice(dh * DK_C, (dh + 1) * DK_C)
                qg_ref[rs, cs] = (q * eg).astype(BF16)
                kd_ref[rs, cs] = (k * jnp.exp(g_last - gc)).astype(BF16)
                eg_ref[c, dh:dh + 1, :] = jnp.broadcast_to(jnp.exp(g_last), (1, LANE))
    solve()


def _gdn_local(cqkv, small, abt, conv_w, prow, pcol, l, seq):
    t = cqkv.shape[0]
    assert seq % TMG == 0
    nt = t // TMG
    nh8 = t // HALO
    cpg = TMG // CHUNK
    row = lambda i: (i, 0)
    dh = 2 * N_HEADS_C
    return pl.pallas_call(
        functools.partial(_gdn_local_kernel, seq=seq),
        grid=(nt,),
        in_specs=[pl.BlockSpec((TMG, QKV_C), row),
                  pl.BlockSpec((HALO, QKV_C), lambda i: (jnp.maximum(i * (TMG // HALO) - 1, 0), 0)),
                  pl.BlockSpec((HALO, QKV_C), lambda i: (jnp.minimum((i + 1) * (TMG // HALO), nh8 - 1), 0)),
                  pl.BlockSpec((TMG, LANE), row),
                  pl.BlockSpec((cpg, 16, CHUNK), lambda i: (i, 0, 0)),
                  _layer_spec(conv_w, l), _layer_spec(prow, l), _layer_spec(pcol, l)],
        out_specs=[pl.BlockSpec((TMG, dh * DV_C), row),
                   pl.BlockSpec((TMG, dh * DK_C), row),
                   pl.BlockSpec((TMG, dh * DK_C), row),
                   pl.BlockSpec((TMG, dh * DK_C), row),
                   pl.BlockSpec((TMG, dh * CHUNK), row),
                   pl.BlockSpec((cpg, dh, LANE), lambda i: (i, 0, 0))],
        out_shape=[jax.ShapeDtypeStruct((t, dh * DV_C), BF16),
                   jax.ShapeDtypeStruct((t, dh * DK_C), BF16),
                   jax.ShapeDtypeStruct((t, dh * DK_C), BF16),
                   jax.ShapeDtypeStruct((t, dh * DK_C), BF16),
                   jax.ShapeDtypeStruct((t, dh * CHUNK), BF16),
                   jax.ShapeDtypeStruct((t // CHUNK, dh, LANE), F32)],
        scratch_shapes=[pltpu.VMEM((TMG, QKV_C), F32), pltpu.VMEM((TMG, 16), F32)],
        compiler_params=pltpu.CompilerParams(dimension_semantics=("parallel",)),
        name="gdn_local",
    )(cqkv, cqkv, cqkv, small, abt, conv_w, prow, pcol)


def _gdn_scan_kernel(*refs, nt, ns, has_init, want_state, n_carry, layer, fill_depth):
    it = iter(refs)
    ins = [[next(it) for _ in range(6)] for _ in range(2)]
    s0_ref = next(it) if has_init else None
    for _ in range(n_carry):
        next(it)
    o_refs = [next(it), next(it)]
    st_ref = next(it) if want_state else None
    s_scr = next(it)
    j = pl.program_id(1)
    nh = N_HEADS_C
    nst = ns * 2 * nh

    @pl.when(j == 0)
    def _():
        if has_init:
            s_scr[...] = s0_ref[...].reshape(nst, DK_C, DV_C)
        else:
            s_scr[...] = jnp.zeros_like(s_scr)

    for step in range(CPT):
        chunk = lambda d: step if d == 0 else CPT - 1 - step

        def gather(idx, width):
            return jnp.stack([ins[d][idx][s, chunk(d) * CHUNK:(chunk(d) + 1) * CHUNK, h * width:(h + 1) * width]
                              for s in range(ns) for d in range(2) for h in range(nh)])

        u, w, qg, kd, attn = gather(0, DV_C), gather(1, DK_C), gather(2, DK_C), gather(3, DK_C), gather(4, LANE)
        eg = jnp.stack([ins[d][5][s, chunk(d), d * nh + h:d * nh + h + 1, :]
                        for s in range(ns) for d in range(2) for h in range(nh)])
        st = s_scr[...]
        sb = st.astype(BF16)
        v_new = u.astype(F32) - _bdot(w, sb)
        vb = v_new.astype(BF16)
        zeros = jnp.zeros((CHUNK, DV_C), BF16)
        vb2 = jnp.stack([jnp.concatenate([vb[i], zeros] if (i // nh) % 2 == 0 else [zeros, vb[i]], axis=0)
                         for i in range(nst)])
        o = _bdot(qg, sb) + _bdot(attn, vb2)
        s_scr[...] = st * eg + _bdot_tn(kd, vb)
        for s in range(ns):
            for d in range(2):
                for h in range(nh):
                    o_refs[d][s, chunk(d) * CHUNK:(chunk(d) + 1) * CHUNK, h * DV_C:(h + 1) * DV_C] = (
                        o[(s * 2 + d) * nh + h].astype(BF16))

    if want_state:
        @pl.when(j == nt - 1)
        def _():
            final = s_scr[...].reshape(ns, 2, nh, DK_C, DV_C)
            if fill_depth:
                for k in range(fill_depth):
                    st_ref[:, k] = final if k == layer else jnp.zeros_like(final)
            else:
                st_ref[...] = final


def _gdn_scan(u, w, qg, kd, attn, eg, s0, l, seq, state_out=None):
    t = u.shape[0]
    want_state = state_out is not None
    nt = seq // TM
    nb = t // seq
    ns = next(n for n in (SCAN_SEQS, 2, 1) if nb % n == 0)
    half = N_HEADS_C * DK_C
    has_init = s0 is not None
    by_seq = lambda a: a.reshape((nb, a.shape[0] // nb) + a.shape[1:])
    in_specs, args = [], []
    for d in range(2):
        tile = (lambda b, j: j) if d == 0 else (lambda b, j: nt - 1 - j)
        row = lambda b, j, d=d, tile=tile: (b, tile(b, j), d)
        row4 = lambda b, j, tile=tile: (b, tile(b, j), 0, 0)
        in_specs += [pl.BlockSpec((ns, TM, half), row)] * 4
        in_specs += [pl.BlockSpec((ns, TM, N_HEADS_C * LANE), lambda b, j, tile=tile: (b, tile(b, j), 0)),
                     pl.BlockSpec((ns, CPT, 2 * N_HEADS_C, LANE), row4)]
        args += [by_seq(a) for a in (u, w, qg, kd, attn, eg)]
    st_tail = (2, N_HEADS_C, DK_C, DV_C)
    if has_init:
        in_specs.append(pl.BlockSpec((ns, None) + st_tail, lambda b, j: (b, l, 0, 0, 0, 0)))
        args.append(s0)
    out_specs = [pl.BlockSpec((ns, TM, half), lambda b, j: (b, j, 0)),
                 pl.BlockSpec((ns, TM, half), lambda b, j: (b, nt - 1 - j, 0))]
    out_shape = [jax.ShapeDtypeStruct((nb, seq, half), BF16)] * 2
    aliases, n_carry, fill_depth = {}, 0, 0
    if want_state:
        depth, carry = state_out
        out_shape.append(jax.ShapeDtypeStruct((nb, depth) + st_tail, F32))
        if carry is None:
            fill_depth = depth
            out_specs.append(pl.BlockSpec((ns, depth) + st_tail, lambda b, j: (b, 0, 0, 0, 0, 0)))
        else:
            out_specs.append(pl.BlockSpec((ns, None) + st_tail, lambda b, j: (b, l, 0, 0, 0, 0)))
            n_carry = 1
            aliases[len(args)] = 2
            in_specs.append(pl.BlockSpec(memory_space=pl.ANY))
            args.append(carry)
    outs = pl.pallas_call(
        functools.partial(_gdn_scan_kernel, nt=nt, ns=ns, has_init=has_init, want_state=want_state,
                          n_carry=n_carry, layer=l, fill_depth=fill_depth),
        grid=(nb // ns, nt),
        in_specs=in_specs,
        out_specs=out_specs,
        out_shape=out_shape,
        input_output_aliases=aliases,
        scratch_shapes=[pltpu.VMEM((ns * 2 * N_HEADS_C, DK_C, DV_C), F32)],
        compiler_params=pltpu.CompilerParams(dimension_semantics=("parallel", "arbitrary")),
        name="gdn_scan",
    )(*args)
    return [outs[0].reshape(t, half), outs[1].reshape(t, half)] + list(outs[2:])


def _merge_kernel(x_ref, mod_ref, oa_ref, ob_ref, cf_ref, cb_ref, zc_ref, hb_ref, wg_ref, gn_ref,
                  wa_ref, wb_ref, wc_ref, wo_ref, fg_ref, o_ref, *, last):
    oc = cf_ref[...].astype(F32) + cb_ref[...].astype(F32)
    zc = zc_ref[...].astype(F32)
    gn = gn_ref[...]
    parts = []
    for h in range(N_HEADS_C):
        hs = slice(h * DV_C, (h + 1) * DV_C)
        och = oc[:, hs]
        och = och * lax.rsqrt(jnp.mean(och * och, axis=-1, keepdims=True) + EPS) * gn
        parts.append((och * _silu(zc[:, hs])).astype(BF16))
    ocz = jnp.concatenate(parts, axis=-1)
    pa = _dot(oa_ref[...], wa_ref[...])
    pb = _dot(ob_ref[...], wb_ref[...])
    pc = _dot(ocz, wc_ref[...])
    hb = hb_ref[...]
    ga = _sigmoid(_dot_nt(hb, wg_ref[0:D_MODEL, :]))
    gb = _sigmoid(_dot_nt(hb, wg_ref[D_MODEL:2 * D_MODEL, :]))
    gc = _sigmoid(_dot_nt(hb, wg_ref[2 * D_MODEL:, :]))
    y = _dot((ga * pa + gb * pb + gc * pc).astype(BF16), wo_ref[...])
    gate = mod_ref[0][:, 2 * D_MODEL:]
    xo = x_ref[...] + gate * y
    if last:
        xo = xo * lax.rsqrt(jnp.mean(xo * xo, axis=-1, keepdims=True) + EPS) * fg_ref[...]
    o_ref[...] = xo


def _merge(x2d, l, mod, mod_row_fn, oa, ob, cf, cb, z, hb, wp, gn, wa, wb, wc, wo, fg, last):
    t = x2d.shape[0]
    row = lambda i: (i, 0)
    assert P_GATES == 0
    return pl.pallas_call(
        functools.partial(_merge_kernel, last=last),
        grid=(t // TMD,),
        in_specs=[pl.BlockSpec((TMD, D_MODEL), row),
                  pl.BlockSpec((1, 1, 3 * D_MODEL), lambda i: (mod_row_fn(i), 0, 0)),
                  pl.BlockSpec((TMD, W_A), row),
                  pl.BlockSpec((TMD, W_B), row),
                  pl.BlockSpec((TMD, W_C), row),
                  pl.BlockSpec((TMD, W_C), row),
                  pl.BlockSpec((TMD, W_C), lambda i: (i, 2)),
                  pl.BlockSpec((TMD, D_MODEL), row),
                  pl.BlockSpec((None, W_GATES, D_MODEL), lambda i: (l, 0, 0), pipeline_mode=pl.Buffered(1)),
                  _layer_spec(gn, l), _layer_spec(wa, l), _layer_spec(wb, l), _layer_spec(wc, l),
                  _layer_spec(wo, l),
                  pl.BlockSpec((1, D_MODEL), lambda i: (0, 0))],
        out_specs=pl.BlockSpec((TMD, D_MODEL), row),
        out_shape=jax.ShapeDtypeStruct((t, D_MODEL), F32),
        compiler_params=pltpu.CompilerParams(dimension_semantics=("parallel",)),
        name="merge",
    )(x2d, mod, oa, ob, cf, cb, z, hb, wp, gn, wa, wb, wc, wo, fg)


def _rope_tables(n_tokens, rot_dim):
    rows = n_tokens // GRID_W
    row = np.repeat(np.arange(rows), GRID_W).astype(np.float32)
    col = np.tile(np.arange(GRID_W), rows).astype(np.float32)
    n_pairs = rot_dim // 4
    inv = (np.float32(ROPE_BASE) ** (-np.arange(n_pairs, dtype=np.float32) / np.float32(n_pairs))).astype(np.float32)
    ang = np.concatenate([row[:, None] * inv, col[:, None] * inv], axis=-1)
    c, s = np.cos(ang), np.sin(ang)
    return np.repeat(c, 2, axis=-1), np.stack([-s, s], axis=-1).reshape(n_tokens, rot_dim)


def _in_offsets():
    o = [0]
    for n in IN_SIZES:
        o.append(o[-1] + n)
    return o


def _relayout_moves():
    o = _in_offsets()
    order = [(o[12], W_GATES), (o[0], W_A), (o[1], LANE), (o[1] + HD_A, HD_A), (o[1], HD_A), (o[2], LANE),
             (o[3], W_A), (o[7], W_B), (o[11], W_C), (o[4], Q_RANK_B), (o[5], KV_RANK_B),
             (None, S_KPE), (o[6], QK_ROPE_B), (o[9], 4 * N_HEADS_C), (None, LANE - S_B - 8), (o[8], QKV_C)]
    moves, dst = [], 0
    for src, n in order:
        moves.append((src, dst, n))
        dst += n
    assert dst == P_END
    return moves


def _relayout_kernel(w_ref, o_ref, wab_ref, wvat_ref):
    for src, dst, n in _relayout_moves():
        if src is None:
            o_ref[dst:dst + n, :] = jnp.zeros((n, o_ref.shape[1]), BF16)
        else:
            o_ref[dst:dst + n, :] = w_ref[src:src + n, :].astype(BF16)
    o = _in_offsets()
    wab_ref[...] = w_ref[o[9]:o[11], :].astype(BF16)
    wvat_ref[...] = w_ref[o[2]:o[3], :].astype(BF16)


def _relayout_w_in(w_in):
    w_t = jnp.swapaxes(w_in, 1, 2)
    depth, width, _ = w_t.shape
    cols = 128
    nab, nv = 4 * N_HEADS_C, N_KV_A * HD_A
    col = lambda l, i: (l, 0, i)
    return pl.pallas_call(
        _relayout_kernel,
        grid=(depth, D_MODEL // cols),
        in_specs=[pl.BlockSpec((None, width, cols), col)],
        out_specs=[pl.BlockSpec((None, P_END, cols), col),
                   pl.BlockSpec((None, nab, cols), col),
                   pl.BlockSpec((None, nv, cols), col)],
        out_shape=[jax.ShapeDtypeStruct((depth, P_END, D_MODEL), BF16),
                   jax.ShapeDtypeStruct((depth, nab, D_MODEL), BF16),
                   jax.ShapeDtypeStruct((depth, nv, D_MODEL), BF16)],
        name="w_in_relayout",
    )(w_t)


def _prep_weights(w_in, w_uq, w_ukv):
    depth = w_in.shape[0]
    wp, wab, wvat = _relayout_w_in(w_in)
    hd = QK_NOPE_B + QK_ROPE_B
    wuq = jnp.pad(w_uq.reshape(depth, Q_RANK_B, N_HEADS_B, hd), ((0, 0), (0, 0), (0, 0), (0, MLA_HW - hd)))
    wuq = wuq.reshape(depth, Q_RANK_B, HB_COLS).astype(BF16)
    kv = w_ukv.reshape(depth, KV_RANK_B, N_HEADS_B, QK_NOPE_B + V_HD_B)
    wk = jnp.pad(kv[..., :QK_NOPE_B], ((0, 0), (0, 0), (0, 0), (0, MLA_HW - QK_NOPE_B)))
    wukv = wk.reshape(depth, KV_RANK_B, HB_COLS).astype(BF16)
    wvbt = jnp.swapaxes(kv[..., QK_NOPE_B:].reshape(depth, KV_RANK_B, W_B), 1, 2).astype(BF16)
    return wp, wab, wvat, wuq, wukv, wvbt


def _cache_tiles_a(kx, vx):
    k0, k1 = kx[..., 0, :], kx[..., 1, :]
    z = jnp.zeros_like(k0)
    ka = jnp.concatenate([k0, z, z, k0, k1, z, z, k1], axis=-1).astype(BF16)
    vt = jnp.transpose(vx, (0, 1, 3, 4, 2))
    vt = jnp.concatenate([vt, jnp.ones(vt.shape[:3] + (VT_ONES, vt.shape[4]), vt.dtype)], axis=3)
    return ka, vt.reshape(vt.shape[:2] + (VAT_ROWS, vt.shape[4])).astype(BF16)


def kernel(x_prompt, x_sample, cache_attn_k, cache_attn_v, cache_mla_ckv, cache_mla_kpe, state_gdn, c, c_ctx,
           norm_g, w_ada, b_ada, w_in, attn_sink, mla_q_norm, mla_w_uq, mla_kv_norm, mla_w_ukv, gdn_conv,
           gdn_a_log, gdn_dt_bias, gdn_norm, w_branch_a, w_branch_b, w_branch_c, w_out, final_norm_g):
    depth = w_in.shape[0]
    nb_c, seq_c, _ = x_prompt.shape
    nb_l, seq_l, _ = x_sample.shape
    past = cache_attn_k.shape[2]
    assert P_END % LANE == 0 and seq_c % TM == 0 and seq_l % TMD == 0 and nb_l < 8 and TM == 2 * WINDOW
    assert (nb_c * seq_c) % TMD == 0 and TMD % TM == 0

    cond8 = jnp.zeros((8, D_MODEL), F32).at[:nb_l].set(c).at[nb_l].set(c_ctx)
    mod = _modulation(cond8, w_ada, b_ada).reshape(depth * 8, 1, 3 * D_MODEL)

    c_a, s_a = _rope_tables(seq_l, HD_A)
    c_b, s_b = _rope_tables(seq_l, QK_ROPE_B)
    pad_l, pad_r = S_KPE, LANE - S_KPE - QK_ROPE_B
    one, zero = np.ones((seq_l, 1), np.float32), np.zeros((seq_l, 1), np.float32)
    rope_tabs = tuple(jnp.asarray(a) for a in (
        np.tile(c_a, (1, LANE // HD_A)), np.tile(s_a, (1, LANE // HD_A)),
        np.concatenate([np.tile(one, (1, pad_l)), c_b, np.tile(one, (1, pad_r))], 1),
        np.concatenate([np.tile(zero, (1, pad_l)), s_b, np.tile(zero, (1, pad_r))], 1)))

    weights = _prep_weights(w_in, mla_w_uq, mla_w_ukv)
    wukv, wvbt = weights[4], weights[5]
    ng = norm_g.reshape(depth, 1, D_MODEL)
    qn = mla_q_norm.reshape(depth, 1, Q_RANK_B)
    kvn = mla_kv_norm.reshape(depth, 1, KV_RANK_B)
    sink = attn_sink.reshape(depth, 1, N_HEADS_A)
    prow = jnp.stack([gdn_a_log.reshape(depth, -1), gdn_dt_bias.reshape(depth, -1)], axis=1)
    pcol = jnp.swapaxes(prow, 1, 2)
    gn = gdn_norm.reshape(depth, 1, DV_C)
    wa, wb, wc, wo = (w.astype(BF16) for w in (w_branch_a, w_branch_b, w_branch_c, w_out))
    fg = final_norm_g.reshape(1, D_MODEL)
    kxa, vxa = _cache_tiles_a(cache_attn_k, cache_attn_v)
    kpex = jnp.pad(cache_mla_kpe, ((0, 0), (0, 0), (0, 0), (pad_l, pad_r)))

    tps_c, tps_l = seq_c // TM, seq_l // TM
    tpd_l = seq_l // TMD
    y_p = x_prompt.reshape(nb_c * seq_c, D_MODEL)
    y_s = x_sample.reshape(nb_l * seq_l, D_MODEL)
    new_cache = new_state = None
    for l in range(depth):
        last = l == depth - 1

        mod_row_c = lambda i, l=l: l * 8 + nb_l
        outs = _inproj(y_p, l, mod, mod_row_c, ng, weights, qn, kvn, None, tps_c, (depth, seq_c, new_cache))
        (qa, ka, vat, z, hb, qb, kb, vbt, small, cqkv, abt), new_cache = outs[:11], tuple(outs[11:])
        oa, ob = _attn_ctx(qa, ka, vat, qb, kb, vbt, z, sink, l, seq_c)
        u, w, qg, kd, attn, eg = _gdn_local(cqkv, small, abt, gdn_conv, prow, pcol, l, seq_c)
        cf, cb, new_state = _gdn_scan(u, w, qg, kd, attn, eg, None, l, seq_c, (depth, new_state))
        y_p = _merge(y_p, l, mod, mod_row_c, oa, ob, cf, cb, z, hb, weights[0], gn, wa, wb, wc, wo, fg, last)

        mod_row_l = lambda i, l=l: l * 8 + i // tpd_l
        (qa, ka, vat, z, hb, qb, kb, vbt, small, cqkv, abt, qnorm) = _inproj(
            y_s, l, mod, mod_row_l, ng, weights, qn, kvn, rope_tabs, tpd_l)
        oa = _attn_a_lat(qa, qnorm, ka, vat, z, sink, kxa, vxa, l, seq_l)
        kxb, vxb, knx = _kvup(cache_mla_ckv, kpex, wukv, wvbt, l)
        ob = _attn_b_lat(qb, qnorm, kb, vbt, z, kxb, vxb, knx, seq_l, 2 * TM, N_HEADS_B // 4)
        u, w, qg, kd, attn, eg = _gdn_local(cqkv, small, abt, gdn_conv, prow, pcol, l, seq_l)
        cf, cb = _gdn_scan(u, w, qg, kd, attn, eg, state_gdn, l, seq_l)
        y_s = _merge(y_s, l, mod, mod_row_l, oa, ob, cf, cb, z, hb, weights[0], gn, wa, wb, wc, wo, fg, last)

    new_k, new_v, new_ckv, new_kpe = new_cache
    kv_shape = (nb_c, depth, seq_c, N_KV_A, HD_A)
    return (y_p.reshape(nb_c, seq_c, D_MODEL), y_s.reshape(nb_l, seq_l, D_MODEL),
            new_k.reshape(kv_shape), new_v.reshape(kv_shape), new_ckv, new_kpe, new_state)
```

```python
import functools

import numpy as np
import jax
import jax.numpy as jnp
from jax import lax
from jax.experimental import pallas as pl
from jax.experimental.pallas import tpu as pltpu

F32 = jnp.float32
BF16 = jnp.bfloat16

D_MODEL = 1024
GRID_W = 64
ROPE_BASE = 10000.0
EPS = 1e-6
NEG_INF = -1e30
N_HEADS_A = 8
N_KV_A = 2
HD_A = 64
GQA_GROUP = N_HEADS_A // N_KV_A
WINDOW = 128
N_HEADS_B = 8
QK_NOPE_B = 64
QK_ROPE_B = 32
V_HD_B = 64
Q_RANK_B = 384
KV_RANK_B = 256
MLA_SCALE = (QK_NOPE_B + QK_ROPE_B) ** -0.5
N_HEADS_C = 4
DK_C = 128
DV_C = 128
CHUNK = 64
W_A = N_HEADS_A * HD_A
W_B = N_HEADS_B * V_HD_B
W_C = N_HEADS_C * DV_C
QKV_C = 2 * N_HEADS_C * DK_C + W_C
IN_SIZES = (W_A, N_KV_A * HD_A, N_KV_A * HD_A, W_A, Q_RANK_B, KV_RANK_B, QK_ROPE_B, W_B, QKV_C,
            2 * N_HEADS_C, 2 * N_HEADS_C, W_C, 3 * D_MODEL)

LANE = 128
HALF = LANE // 2
TM = 256
TMD = 512
TMG = 256
CPT = TM // CHUNK
SCAN_SEQS = 4
CTX_SEQS = 4
HALO = 16
MLA_HW = 128
KA_COLS = 4 * LANE
HB_COLS = N_HEADS_B * MLA_HW
VT_ONES = 16
VT_ROWS = V_HD_B + VT_ONES
VAT_ROWS = N_KV_A * VT_ROWS
VBT_ROWS = N_HEADS_B * VT_ROWS
LOG2E = 1.4426950408889634
SHIFT_MAX = 60.0
SHIFT_SLACK = 1.001
SHIFT_SLACK_ABS = 0.01
N_QB = 0
N_KB = N_QB + N_HEADS_B
N_QA = N_KB + N_HEADS_B
N_KA = N_QA + N_HEADS_A
NORM_COLS = 32

P_GATES = 0
W_GATES = 3 * D_MODEL
P_QKV = P_GATES + W_GATES
A_COLS = W_A + 3 * LANE
P_Z = P_QKV + A_COLS
W_Z = W_A + W_B + W_C
P_CQ = P_Z + W_Z
P_CKV = P_CQ + Q_RANK_B
P_SMALL = P_CKV + KV_RANK_B
P_CQKV = P_SMALL + LANE
P_END = P_CQKV + QKV_C
S_KPE = 64
S_A = 96
S_B = 104


def _sigmoid(x):
    return 0.5 * jnp.tanh(0.5 * x) + 0.5


def _silu(x):
    return x * _sigmoid(x)


def _softplus(x):
    return jnp.maximum(x, 0.0) + jnp.log(1.0 + jnp.exp(-jnp.abs(x)))


def _dot(a, b):
    return jnp.dot(a, b, preferred_element_type=F32)


def _dot_nt(a, b):
    return lax.dot_general(a, b, (((1,), (1,)), ((), ())), preferred_element_type=F32)


def _bdot(a, b):
    return lax.dot_general(a, b, (((2,), (1,)), ((0,), (0,))), preferred_element_type=F32)


def _bdot_nt(a, b):
    return lax.dot_general(a, b, (((2,), (2,)), ((0,), (0,))), preferred_element_type=F32)


def _bdot_tn(a, b):
    return lax.dot_general(a, b, (((1,), (1,)), ((0,), (0,))), preferred_element_type=F32)


def _dot_exact(a, b):
    return jnp.dot(a, b, preferred_element_type=F32, precision=lax.Precision.HIGHEST)


def _row_sumsq(x16):
    x = x16.astype(F32)
    return jnp.sum(x * x, axis=-1, keepdims=True)


def _row_norm(x16):
    return jnp.sqrt(_row_sumsq(x16))


def _rope(x, c, s):
    n = x.shape[-1]
    lane = lax.broadcasted_iota(jnp.int32, x.shape, 1)
    swapped = jnp.where(lane % 2 == 0, pltpu.roll(x, n - 1, 1), pltpu.roll(x, 1, 1))
    return x * c + swapped * s


def _mod_kernel(cond_ref, w_ref, b_ref, out_ref):
    cnd = cond_ref[...]
    out_ref[0] = _dot(_silu(cnd).astype(BF16), w_ref[0].astype(BF16)) + b_ref[0]


def _modulation(cond8, w_ada, b_ada):
    depth = w_ada.shape[0]
    tn = 768
    return pl.pallas_call(
        _mod_kernel,
        grid=(depth, 3 * D_MODEL // tn),
        in_specs=[pl.BlockSpec((8, D_MODEL), lambda l, n: (0, 0)),
                  pl.BlockSpec((1, D_MODEL, tn), lambda l, n: (l, 0, n)),
                  pl.BlockSpec((1, 1, tn), lambda l, n: (l, 0, n))],
        out_specs=pl.BlockSpec((1, 8, tn), lambda l, n: (l, 0, n)),
        out_shape=jax.ShapeDtypeStruct((depth, 8, 3 * D_MODEL), F32),
        name="adaln_mod",
    )(cond8, w_ada, b_ada.reshape(depth, 1, 3 * D_MODEL))


def _inproj_kernel(*refs, rope, cache_seq, n_carry, layer, fill_depth):
    it = iter(refs)
    (x_ref, mod_ref, ng_ref, wp_ref, wab_ref, wvat_ref, wuq_ref, wukv_ref, wvbt_ref, qn_ref,
     kvn_ref) = (next(it) for _ in range(11))
    if rope:
        ca_ref, sa_ref, cb_ref, sb_ref = (next(it) for _ in range(4))
    for _ in range(n_carry):
        next(it)
    (qa_ref, ka_ref, vat_ref, z_ref, hb_ref, qb_ref, kb_ref, vbt_ref, small_ref, cqkv_ref,
     abt_ref) = (next(it) for _ in range(11))
    if rope:
        qnorm_ref = next(it)
    if cache_seq:
        ck_ref, cv_ref, cckv_ref, ckpe_ref = (next(it) for _ in range(4))

    def to_cache(ref, val):
        for s in range(TMD // cache_seq):
            rows = val[s * cache_seq:(s + 1) * cache_seq]
            if fill_depth:
                for k in range(fill_depth):
                    ref[s, k] = rows if k == layer else jnp.zeros_like(rows)
            else:
                ref[s] = rows

    x = x_ref[...]
    mod = mod_ref[0]
    shift, scale = mod[:, :D_MODEL], mod[:, D_MODEL:2 * D_MODEL]
    xn = x * lax.rsqrt(jnp.mean(x * x, axis=-1, keepdims=True) + EPS) * ng_ref[...]
    hb = (xn * (1.0 + scale) + shift).astype(BF16)
    lane = lax.broadcasted_iota(jnp.int32, (TMD, LANE), 1)
    lo = lane < HALF

    def mm(lo_col, hi_col):
        return _dot_nt(hb, wp_ref[lo_col:hi_col, :])

    r = mm(P_QKV, P_QKV + A_COLS)
    tiles = [r[:, t * LANE:(t + 1) * LANE] for t in range(A_COLS // LANE)]
    if cache_seq:
        to_cache(ck_ref, tiles[4])
        to_cache(cv_ref, tiles[6])
    if rope:
        ca, sa = ca_ref[...], sa_ref[...]
        tiles[:6] = [_rope(t, ca, sa) for t in tiles[:6]]
    for t in range(4):
        q16 = (tiles[t] * (HD_A ** -0.5 * LOG2E)).astype(BF16)
        qa_ref[:, t * LANE:(t + 1) * LANE] = q16
        if rope:
            sq = jnp.square(q16.astype(F32))
            qnorm_ref[:, N_QA + 2 * t:N_QA + 2 * t + 1] = jnp.sum(jnp.where(lo, sq, 0.0), -1, keepdims=True)
            qnorm_ref[:, N_QA + 2 * t + 1:N_QA + 2 * t + 2] = jnp.sum(jnp.where(lo, 0.0, sq), -1, keepdims=True)
    k01, k10 = tiles[4], tiles[5]
    k0_16 = jnp.where(lo, k01, 0.0).astype(BF16)
    k1_16 = jnp.where(lo, k10, 0.0).astype(BF16)
    ka_ref[:, 0 * LANE:1 * LANE] = k0_16
    ka_ref[:, 1 * LANE:2 * LANE] = jnp.where(lo, 0.0, k10).astype(BF16)
    ka_ref[:, 2 * LANE:3 * LANE] = k1_16
    ka_ref[:, 3 * LANE:4 * LANE] = jnp.where(lo, 0.0, k01).astype(BF16)
    if rope:
        qnorm_ref[:, N_KA:N_KA + 1] = _row_sumsq(k0_16)
        qnorm_ref[:, N_KA + 1:N_KA + 2] = _row_sumsq(k1_16)
        qnorm_ref[:, N_KA + N_KV_A:] = jnp.zeros((TMD, NORM_COLS - N_KA - N_KV_A), F32)
    ones = jnp.ones((VT_ONES, TMD), BF16)
    vt = _dot_nt(wvat_ref[...], hb)
    for g in range(N_KV_A):
        vat_ref[g * VT_ROWS:g * VT_ROWS + HD_A] = vt[g * HD_A:(g + 1) * HD_A].astype(BF16)
        vat_ref[g * VT_ROWS + HD_A:(g + 1) * VT_ROWS] = ones

    for t in range(3):
        z_ref[:, t * 512:(t + 1) * 512] = mm(P_Z + t * 512, P_Z + (t + 1) * 512).astype(BF16)
    hb_ref[...] = hb

    r = mm(P_CQ, P_CQ + Q_RANK_B)
    qn = r * lax.rsqrt(jnp.mean(r * r, axis=-1, keepdims=True) + EPS) * qn_ref[...]
    q = _dot(qn.astype(BF16), wuq_ref[...])
    if rope:
        cb, sb = cb_ref[...], sb_ref[...]
        for h in range(N_HEADS_B):
            seg = (_rope(q[:, h * MLA_HW:(h + 1) * MLA_HW], cb, sb) * (MLA_SCALE * LOG2E)).astype(BF16)
            qb_ref[:, h * MLA_HW:(h + 1) * MLA_HW] = seg
            qnorm_ref[:, N_QB + h:N_QB + h + 1] = _row_sumsq(seg)
    else:
        qb_ref[...] = (q * (MLA_SCALE * LOG2E)).astype(BF16)

    r = mm(P_SMALL, P_SMALL + LANE)
    small_ref[...] = r
    if cache_seq:
        to_cache(ckpe_ref, r[:, S_KPE:S_KPE + QK_ROPE_B])
    kp = _rope(r, cb, sb) if rope else r
    kp = jnp.where((lane >= S_KPE) & (lane < S_KPE + QK_ROPE_B), kp, 0.0)

    r = mm(P_CKV, P_CKV + KV_RANK_B)
    cn = r * lax.rsqrt(jnp.mean(r * r, axis=-1, keepdims=True) + EPS) * kvn_ref[...]
    if cache_seq:
        to_cache(cckv_ref, cn)
    cn16 = cn.astype(BF16)
    kv = _dot(cn16, wukv_ref[...])
    vt = _dot_nt(wvbt_ref[...], cn16)
    for h in range(N_HEADS_B):
        k16 = (kv[:, h * MLA_HW:(h + 1) * MLA_HW] + kp).astype(BF16)
        kb_ref[:, h * MLA_HW:(h + 1) * MLA_HW] = k16
        if rope:
            qnorm_ref[:, N_KB + h:N_KB + h + 1] = _row_sumsq(k16)
        vbt_ref[h * VT_ROWS:h * VT_ROWS + V_HD_B] = vt[h * V_HD_B:(h + 1) * V_HD_B].astype(BF16)
        vbt_ref[h * VT_ROWS + V_HD_B:(h + 1) * VT_ROWS] = ones

    for t in range(3):
        cqkv_ref[:, t * 512:(t + 1) * 512] = mm(P_CQKV + t * 512, P_CQKV + (t + 1) * 512).astype(BF16)
    if rope:
        qnorm_ref[...] = jnp.sqrt(qnorm_ref[...])

    for c in range(TMD // CHUNK):
        abt_ref[c] = _dot_nt(wab_ref[...], hb[c * CHUNK:(c + 1) * CHUNK])


def _layer_spec(arr, l):
    nd = arr.ndim - 1
    return pl.BlockSpec((None,) + arr.shape[1:], lambda *_: (l,) + (0,) * nd, pipeline_mode=pl.Buffered(1))


def _inproj(x2d, l, mod, mod_row_fn, ng, weights, qn, kvn, rope_tabs, tiles_per_seq, cache=None):
    t = x2d.shape[0]
    nt = t // TMD
    rope = rope_tabs is not None
    row = lambda i: (i, 0)
    col = lambda i: (0, i)
    wp, wab, wvat, wuq, wukv, wvbt = weights
    params = (ng, wp, wab, wvat, wuq, wukv, wvbt, qn, kvn)
    in_specs = [pl.BlockSpec((TMD, D_MODEL), row),
                pl.BlockSpec((1, 1, 3 * D_MODEL), lambda i: (mod_row_fn(i), 0, 0))]
    in_specs += [_layer_spec(a, l) for a in params]
    args = [x2d, mod, *params]
    if rope:
        pos = lambda i: (i % tiles_per_seq, 0)
        in_specs += [pl.BlockSpec((TMD, LANE), pos)] * 4
        args += list(rope_tabs)
    outs = [(W_A, BF16, False), (KA_COLS, BF16, False), (VAT_ROWS, BF16, True), (W_Z, BF16, False),
            (D_MODEL, BF16, False), (HB_COLS, BF16, False), (HB_COLS, BF16, False), (VBT_ROWS, BF16, True),
            (LANE, F32, False), (QKV_C, BF16, False)]
    out_shape = [jax.ShapeDtypeStruct((w, t) if tr else (t, w), dt) for w, dt, tr in outs]
    out_specs = [pl.BlockSpec((w, TMD), col) if tr else pl.BlockSpec((TMD, w), row) for w, _, tr in outs]
    out_shape.append(jax.ShapeDtypeStruct((t // CHUNK, 16, CHUNK), F32))
    out_specs.append(pl.BlockSpec((TMD // CHUNK, 16, CHUNK), lambda i: (i, 0, 0)))
    if rope:
        out_shape.append(jax.ShapeDtypeStruct((t, NORM_COLS), F32))
        out_specs.append(pl.BlockSpec((TMD, NORM_COLS), row))
    aliases, cache_seq, n_carry, fill_depth = {}, 0, 0, 0
    if cache is not None:
        depth, cache_seq, carry = cache
        spt = TMD // cache_seq
        if carry is None:
            fill_depth, carry = depth, ()
        for w in (N_KV_A * HD_A, N_KV_A * HD_A, KV_RANK_B, QK_ROPE_B):
            out_shape.append(jax.ShapeDtypeStruct((t // cache_seq, depth, cache_seq, w), F32))
            if fill_depth:
                out_specs.append(pl.BlockSpec((spt, depth, cache_seq, w), lambda i: (i, 0, 0, 0)))
            else:
                out_specs.append(pl.BlockSpec((spt, None, cache_seq, w), lambda i: (i, l, 0, 0)))
        n_carry = len(carry)
        for k, a in enumerate(carry):
            aliases[len(args)] = len(out_shape) - n_carry + k
            in_specs.append(pl.BlockSpec(memory_space=pl.ANY))
            args.append(a)
    return pl.pallas_call(
        functools.partial(_inproj_kernel, rope=rope, cache_seq=cache_seq, n_carry=n_carry, layer=l,
                          fill_depth=fill_depth),
        grid=(nt,),
        in_specs=in_specs,
        out_specs=out_specs,
        out_shape=out_shape,
        input_output_aliases=aliases,
        compiler_params=pltpu.CompilerParams(dimension_semantics=("parallel",)),
        name="inproj_ctx" if cache is not None else "inproj_lat",
    )(*args)


def _kvup_kernel(c_ref, kpe_ref, w_ref, wvt_ref, k_ref, vt_ref, kn_ref):
    c16 = c_ref[...].astype(BF16)
    kv = _dot(c16, w_ref[...])
    vt = _dot_nt(wvt_ref[...], c16)
    kp = kpe_ref[...]
    ones = jnp.ones((VT_ONES, c16.shape[0]), BF16)
    for h in range(N_HEADS_B):
        k16 = (kv[:, h * MLA_HW:(h + 1) * MLA_HW] + kp).astype(BF16)
        k_ref[:, h * MLA_HW:(h + 1) * MLA_HW] = k16
        kn_ref[:, h:h + 1] = _row_sumsq(k16)
        vt_ref[h * VT_ROWS:h * VT_ROWS + V_HD_B] = vt[h * V_HD_B:(h + 1) * V_HD_B].astype(BF16)
        vt_ref[h * VT_ROWS + V_HD_B:(h + 1) * VT_ROWS] = ones
    kn_ref[...] = jnp.sqrt(kn_ref[...])


def _kvup(ckv, kpe, wukv, wvbt, l):
    nb, _, past, _ = ckv.shape
    return pl.pallas_call(
        _kvup_kernel,
        grid=(nb,),
        in_specs=[pl.BlockSpec((None, None, past, KV_RANK_B), lambda b: (b, l, 0, 0)),
                  pl.BlockSpec((None, None, past, LANE), lambda b: (b, l, 0, 0)),
                  _layer_spec(wukv, l), _layer_spec(wvbt, l)],
        out_specs=[pl.BlockSpec((past, HB_COLS), lambda b: (b, 0)),
                   pl.BlockSpec((None, VBT_ROWS, past), lambda b: (b, 0, 0)),
                   pl.BlockSpec((past, N_HEADS_B), lambda b: (b, 0))],
        out_shape=[jax.ShapeDtypeStruct((nb * past, HB_COLS), BF16),
                   jax.ShapeDtypeStruct((nb, VBT_ROWS, past), BF16),
                   jax.ShapeDtypeStruct((nb * past, N_HEADS_B), F32)],
        name="mla_cache_up",
    )(ckv, kpe, wukv, wvbt)


def _scores_t(q_tiles, k_tiles, bias_t):
    st = _bdot_nt(jnp.stack(k_tiles), jnp.stack(q_tiles))
    return st if bias_t is None else st + bias_t[None]


def _softmax_pv(sts, vts, sink, shift=None):
    if shift is None:
        m = jnp.max(sts[0], axis=1, keepdims=True)
        for st in sts[1:]:
            m = jnp.maximum(m, jnp.max(st, axis=1, keepdims=True))
    else:
        m = shift
    if sink is not None:
        m = jnp.maximum(m, sink)
    ot = _bdot(jnp.stack(vts[0]), jnp.exp2(sts[0] - m).astype(BF16))
    for st, vt in zip(sts[1:], vts[1:]):
        ot = ot + _bdot(jnp.stack(vt), jnp.exp2(st - m).astype(BF16))
    den = ot[:, V_HD_B:V_HD_B + 1, :]
    if sink is not None:
        den = den + jnp.exp2(sink - m)
    num = ot[:, :V_HD_B, :] / den
    return [jnp.concatenate([num[2 * i], num[2 * i + 1]], axis=0).T for i in range(sts[0].shape[0] // 2)]


def _tile(x, t):
    return x[:, t * LANE:(t + 1) * LANE]


def _attn_a_heads(q, segments, sink_ref, shift=None):
    sts, vts = [], []
    for ka, vat, bias_t in segments:
        qs, ks, vs = [], [], []
        for t in range(N_HEADS_A // 2):
            g = (2 * t) // GQA_GROUP
            for e in range(2):
                qs.append(_tile(q, t))
                ks.append(_tile(ka, 2 * g + e))
                vs.append(vat[g * VT_ROWS:(g + 1) * VT_ROWS])
        sts.append(_scores_t(qs, ks, bias_t))
        vts.append(vs)
    sink = jnp.stack([sink_ref[:, h:h + 1] * LOG2E for h in range(N_HEADS_A)])
    return _softmax_pv(sts, vts, sink, shift)


def _gated_store(outs, z_ref, o_ref, first_tile=0):
    for i, o in enumerate(outs):
        t = first_tile + i
        z = _tile(z_ref, t).astype(F32)
        o_ref[:, t * LANE:(t + 1) * LANE] = (o * _silu(z)).astype(BF16)


def _attn_ctx_kernel(qa_ref, ka_ref, vat_ref, qb_ref, kb_ref, vbt_ref, za_ref, zb_ref, sink_ref, oa_ref, ob_ref, *,
                     seq):
    heads = range(N_HEADS_B)
    for s in range(qa_ref.shape[0] // seq):
        rs = slice(s * seq, (s + 1) * seq)
        _gated_store(_attn_a_heads(qa_ref[rs], [(ka_ref[rs], vat_ref[:, rs], None)], sink_ref),
                     za_ref.at[rs], oa_ref.at[rs])
        q, kb, vbt = qb_ref[rs], kb_ref[rs], vbt_ref[:, rs]
        st = _scores_t([_tile(q, h) for h in heads], [_tile(kb, h) for h in heads], None)
        _gated_store(_softmax_pv([st], [[vbt[h * VT_ROWS:(h + 1) * VT_ROWS] for h in heads]], None),
                     zb_ref.at[rs], ob_ref.at[rs])


def _attn_ctx(qa, ka, vat, qb, kb, vbt, z, sink, l, seq):
    t = qa.shape[0]
    row = lambda b: (b, 0)
    col = lambda b: (0, b)
    ns = next(n for n in (CTX_SEQS, 1) if (t // seq) % n == 0)
    rows = ns * seq
    return pl.pallas_call(
        functools.partial(_attn_ctx_kernel, seq=seq),
        grid=(t // rows,),
        in_specs=[pl.BlockSpec((rows, W_A), row),
                  pl.BlockSpec((rows, KA_COLS), row),
                  pl.BlockSpec((VAT_ROWS, rows), col),
                  pl.BlockSpec((rows, HB_COLS), row),
                  pl.BlockSpec((rows, HB_COLS), row),
                  pl.BlockSpec((VBT_ROWS, rows), col),
                  pl.BlockSpec((rows, W_A), row),
                  pl.BlockSpec((rows, W_B), lambda b: (b, 1)),
                  _layer_spec(sink, l)],
        out_specs=[pl.BlockSpec((rows, W_A), row), pl.BlockSpec((rows, W_B), row)],
        out_shape=[jax.ShapeDtypeStruct((t, W_A), BF16), jax.ShapeDtypeStruct((t, W_B), BF16)],
        compiler_params=pltpu.CompilerParams(dimension_semantics=("parallel",)),
        name="attn_ctx",
    )(qa, ka, vat, qb, kb, vbt, z, z, sink)


def _attn_a_lat_kernel(q_ref, qn_ref, ksn_ref, kp_ref, kc_ref, kn_ref, vp_ref, vc_ref, vn_ref, kx_ref, vx_ref,
                       z_ref, sink_ref, o_ref, knorm_scr, *, nq):
    j = pl.program_id(1)

    @pl.when(j == 0)
    def _():
        kmax = jnp.max(ksn_ref[:, N_KA:N_KA + N_KV_A], axis=0, keepdims=True)
        kxmax = [jnp.max(_row_norm(kx_ref[:, 2 * g * LANE:(2 * g + 1) * LANE]), axis=0, keepdims=True)
                 for g in range(N_KV_A)]
        knorm_scr[...] = jnp.broadcast_to(jnp.maximum(kmax, jnp.concatenate(kxmax, axis=1)), knorm_scr.shape)

    qmax = jnp.max(qn_ref[:, N_QA:N_QA + N_HEADS_A], axis=0, keepdims=True)
    bound = jnp.stack([qmax[:, h:h + 1] * knorm_scr[0:1, h // GQA_GROUP:h // GQA_GROUP + 1] * SHIFT_SLACK
                       + SHIFT_SLACK_ABS for h in range(N_HEADS_A)])
    bound_ok = jnp.max(bound) <= SHIFT_MAX

    ka = jnp.concatenate([kp_ref[...], kc_ref[...], kn_ref[...]], axis=0)
    vat = jnp.concatenate([vp_ref[...], vc_ref[...], vn_ref[...]], axis=1)
    kj = lax.broadcasted_iota(jnp.int32, (ka.shape[0], TM), 0)
    qi = lax.broadcasted_iota(jnp.int32, (ka.shape[0], TM), 1)
    ok = (kj >= qi) & (kj <= qi + 2 * WINDOW)
    ok = ok & ((kj >= WINDOW) | (j > 0)) & ((kj < TM + WINDOW) | (j < nq - 1))
    bias_t = jnp.where(ok, 0.0, NEG_INF)
    segments = [(ka, vat, bias_t), (kx_ref[...], vx_ref[...], None)]

    @pl.when(bound_ok)
    def _():
        _gated_store(_attn_a_heads(q_ref[...], segments, sink_ref, bound), z_ref, o_ref)

    @pl.when(jnp.logical_not(bound_ok))
    def _():
        _gated_store(_attn_a_heads(q_ref[...], segments, sink_ref), z_ref, o_ref)


def _attn_a_lat(qa, norms, ka, vat, z, sink, kx, vxt, l, seq):
    t = qa.shape[0]
    nq = seq // TM
    past = kx.shape[2]
    r = TM // WINDOW
    row = lambda b, j: (b * nq + j, 0)
    prev = lambda b, j: ((b * nq + j) * r - jnp.where(j > 0, 1, 0), 0)
    nxt = lambda b, j: ((b * nq + j) * r + jnp.where(j < nq - 1, r, r - 1), 0)
    swap = lambda f: (lambda b, j: f(b, j)[::-1])
    return pl.pallas_call(
        functools.partial(_attn_a_lat_kernel, nq=nq),
        grid=(t // seq, nq),
        in_specs=[pl.BlockSpec((TM, W_A), row),
                  pl.BlockSpec((TM, NORM_COLS), row),
                  pl.BlockSpec((seq, NORM_COLS), lambda b, j: (b, 0)),
                  pl.BlockSpec((WINDOW, KA_COLS), prev),
                  pl.BlockSpec((TM, KA_COLS), row),
                  pl.BlockSpec((WINDOW, KA_COLS), nxt),
                  pl.BlockSpec((VAT_ROWS, WINDOW), swap(prev)),
                  pl.BlockSpec((VAT_ROWS, TM), swap(row)),
                  pl.BlockSpec((VAT_ROWS, WINDOW), swap(nxt)),
                  pl.BlockSpec((None, None, past, KA_COLS), lambda b, j: (b, l, 0, 0)),
                  pl.BlockSpec((None, None, VAT_ROWS, past), lambda b, j: (b, l, 0, 0)),
                  pl.BlockSpec((TM, W_A), row),
                  _layer_spec(sink, l)],
        out_specs=pl.BlockSpec((TM, W_A), row),
        out_shape=jax.ShapeDtypeStruct((t, W_A), BF16),
        scratch_shapes=[pltpu.VMEM((8, N_KV_A), F32)],
        compiler_params=pltpu.CompilerParams(dimension_semantics=("parallel", "arbitrary")),
        name="attn_a_lat",
    )(qa, norms, norms, ka, ka, ka, vat, vat, vat, kx, vxt, z, sink)


def _attn_b_lat_kernel(q_ref, qn_ref, kn_ref, knx_ref, k_ref, vt_ref, kx_ref, vxt_ref, z_ref, o_ref, knorm_scr, *,
                       group):
    segments = [(k_ref, vt_ref), (kx_ref, vxt_ref)]
    q = q_ref[...]

    @pl.when(pl.program_id(1) == 0)
    def _():
        kmax = jnp.maximum(jnp.max(kn_ref[:, N_KB:N_KB + N_HEADS_B], axis=0, keepdims=True),
                           jnp.max(knx_ref[...], axis=0, keepdims=True))
        knorm_scr[...] = jnp.broadcast_to(kmax, knorm_scr.shape)

    qmax = jnp.max(qn_ref[:, N_QB:N_QB + N_HEADS_B], axis=0, keepdims=True)
    bound = qmax * knorm_scr[0:1, :] * SHIFT_SLACK + SHIFT_SLACK_ABS
    bound = jnp.stack([bound[:, h:h + 1] for h in range(N_HEADS_B)])
    bound_ok = jnp.max(bound) <= SHIFT_MAX

    def scores(h0):
        heads = range(h0, h0 + group)
        return [_scores_t([_tile(q, h) for h in heads], [_tile(kr, h) for h in heads], None) for kr, _ in segments]

    def attend(shift):
        sts = scores(0)
        for h0 in range(0, N_HEADS_B, group):
            sts_next = scores(h0 + group) if h0 + group < N_HEADS_B else None
            vts = [[vr[h * VT_ROWS:(h + 1) * VT_ROWS, :] for h in range(h0, h0 + group)] for _, vr in segments]
            outs = _softmax_pv(sts, vts, None, None if shift is None else shift[h0:h0 + group])
            _gated_store(outs, z_ref, o_ref, h0 // 2)
            sts = sts_next

    @pl.when(bound_ok)
    def _():
        attend(bound)

    @pl.when(jnp.logical_not(bound_ok))
    def _():
        attend(None)


def _attn_b_lat(qb, norms, kb, vbt, z, kx, vxt, knx, seq, qblk, group):
    t = qb.shape[0]
    nq = seq // qblk
    past = kx.shape[0] // (t // seq)
    hw = HB_COLS
    return pl.pallas_call(
        functools.partial(_attn_b_lat_kernel, group=group),
        grid=(t // seq, nq),
        in_specs=[pl.BlockSpec((qblk, hw), lambda b, j: (b * nq + j, 0)),
                  pl.BlockSpec((qblk, NORM_COLS), lambda b, j: (b * nq + j, 0)),
                  pl.BlockSpec((seq, NORM_COLS), lambda b, j: (b, 0)),
                  pl.BlockSpec((past, N_HEADS_B), lambda b, j: (b, 0)),
                  pl.BlockSpec((seq, hw), lambda b, j: (b, 0)),
                  pl.BlockSpec((VBT_ROWS, seq), lambda b, j: (0, b)),
                  pl.BlockSpec((past, hw), lambda b, j: (b, 0)),
                  pl.BlockSpec((None, VBT_ROWS, past), lambda b, j: (b, 0, 0)),
                  pl.BlockSpec((qblk, W_B), lambda b, j: (b * nq + j, 1))],
        out_specs=pl.BlockSpec((qblk, W_B), lambda b, j: (b * nq + j, 0)),
        out_shape=jax.ShapeDtypeStruct((t, W_B), BF16),
        scratch_shapes=[pltpu.VMEM((8, N_HEADS_B), F32)],
        compiler_params=pltpu.CompilerParams(dimension_semantics=("parallel", "arbitrary")),
        name="attn_b_lat",
    )(qb, norms, norms, knx, kb, vbt, kx, vxt, z)


def _gdn_local_kernel(cq_ref, prev_ref, next_ref, small_ref, abt_ref, cw_ref, prow_ref, pcol_ref,
                      u_ref, w_ref, qg_ref, kd_ref, attn_ref, eg_ref, qkv_scr, gb_scr, *, seq):
    x = cq_ref[...].astype(F32)
    tiles_per_seq = seq // TMG
    tpos = pl.program_id(0) % tiles_per_seq
    prev_row = jnp.where(tpos > 0, prev_ref[...].astype(F32)[HALO - 1:HALO, :], 0.0)
    next_row = jnp.where(tpos < tiles_per_seq - 1, next_ref[...].astype(F32)[0:1, :], 0.0)
    rows = lax.broadcasted_iota(jnp.int32, (TMG, 1), 0)
    xm1 = jnp.where(rows == 0, prev_row, pltpu.roll(x, 1, 0))
    xp1 = jnp.where(rows == TMG - 1, next_row, pltpu.roll(x, TMG - 1, 0))
    cw = cw_ref[...]
    y = _silu(xm1 * cw[0:1] + x * cw[1:2] + xp1 * cw[2:3])
    nq = N_HEADS_C * DK_C
    for h in range(N_HEADS_C):
        qh = y[:, h * DK_C:(h + 1) * DK_C]
        kh = y[:, nq + h * DK_C:nq + (h + 1) * DK_C]
        qkv_scr[:, h * DK_C:(h + 1) * DK_C] = (
            qh * lax.rsqrt(jnp.sum(qh * qh, axis=-1, keepdims=True) + EPS) * (DK_C ** -0.5))
        qkv_scr[:, nq + h * DK_C:nq + (h + 1) * DK_C] = kh * lax.rsqrt(jnp.sum(kh * kh, axis=-1, keepdims=True) + EPS)
    qkv_scr[:, 2 * nq:] = y[:, 2 * nq:]

    sm = small_ref[...]
    prow = prow_ref[...]
    gb_scr[:, 0:8] = -jnp.exp(prow[0:1]) * _softplus(sm[:, S_A:S_A + 8] + prow[1:2])
    gb_scr[:, 8:16] = _sigmoid(sm[:, S_B:S_B + 8])
    pcol = pcol_ref[...]

    ri = lax.broadcasted_iota(jnp.int32, (CHUNK, LANE), 0)
    lane = lax.broadcasted_iota(jnp.int32, (CHUNK, LANE), 1)
    fwd = lane < CHUNK
    cj = lane & (CHUNK - 1)
    incl = (fwd & (ri >= cj)) | (~fwd & (ri <= cj))
    strict = (fwd & (ri > cj)) | (~fwd & (ri < cj))
    xor = ri ^ cj
    eye = (ri == cj).astype(F32)
    r2 = lax.broadcasted_iota(jnp.int32, (2 * CHUNK, LANE), 0)
    l2 = lax.broadcasted_iota(jnp.int32, (2 * CHUNK, LANE), 1)
    same_dir = (r2 < CHUNK) == (l2 < CHUNK)
    rs_ = lax.broadcasted_iota(jnp.int32, (CHUNK, CHUNK), 0)
    cs_ = lax.broadcasted_iota(jnp.int32, (CHUNK, CHUNK), 1)
    tril = (rs_ >= cs_).astype(F32)
    triu = (rs_ <= cs_).astype(F32)
    tri_rows = jnp.concatenate([triu, tril], axis=1)
    dup_rows = jnp.concatenate([(rs_ == cs_).astype(F32)] * 2, axis=1)

    def block_diag(x):
        return jnp.where(same_dir[None], jnp.concatenate([x, x], axis=1), 0.0).astype(BF16)

    lows, rhss, order = [], [], []

    def solve():
        low = jnp.stack(lows, axis=0)
        inv = eye[None] - jnp.where(xor[None] == 1, low, 0.0)
        b = 2
        while b < CHUNK:
            cpl = jnp.where((xor[None] >= b) & (xor[None] < 2 * b), low, 0.0)
            tmp = _bdot(cpl.astype(BF16), block_diag(inv))
            inv = inv - _bdot(inv.astype(BF16), block_diag(tmp))
            b *= 2
        scale_u, scale_w, vs, ks = (jnp.stack(a, axis=0) for a in zip(*rhss))
        u = _bdot((inv * scale_u).astype(BF16), vs)
        w = _bdot((inv * scale_w).astype(BF16), ks)
        for i, (rs, h) in enumerate(order):
            for d in range(2):
                cs = slice((d * N_HEADS_C + h) * DK_C, (d * N_HEADS_C + h + 1) * DK_C)
                u_ref[rs, cs] = u[i, :, d * DV_C:(d + 1) * DV_C].astype(BF16)
                w_ref[rs, cs] = w[i, :, d * DK_C:(d + 1) * DK_C].astype(BF16)

    for c in range(TMG // CHUNK):
        rs = slice(c * CHUNK, (c + 1) * CHUNK)
        gcol = gb_scr[rs, 0:8]
        bcol = gb_scr[rs, 8:16]
        abt = abt_ref[c]
        grow = -jnp.exp(pcol[:, 0:1]) * _softplus(abt[0:8] + pcol[:, 1:2])
        gc_f = _dot_exact(tril, gcol)
        gc_b = _dot_exact(triu, gcol)
        gr = _dot_exact(grow, tri_rows)
        br = _dot_exact(_sigmoid(abt[8:16]), dup_rows)
        for h in range(N_HEADS_C):
            hb_ = N_HEADS_C + h
            q = qkv_scr[rs, h * DK_C:(h + 1) * DK_C]
            k = qkv_scr[rs, nq + h * DK_C:nq + (h + 1) * DK_C]
            v = qkv_scr[rs, 2 * nq + h * DV_C:2 * nq + (h + 1) * DV_C]
            k16 = k.astype(BF16)
            kk16 = jnp.concatenate([k16, k16], axis=0)
            kk = _dot_nt(k16, kk16)
            qk = _dot_nt(q.astype(BF16), kk16)
            gcs = (gc_f[:, h:h + 1], gc_b[:, hb_:hb_ + 1])
            betas = (bcol[:, h:h + 1], bcol[:, hb_:hb_ + 1])
            gc2 = jnp.where(fwd, gcs[0], gcs[1])
            gr2 = jnp.where(fwd[0:1], gr[h:h + 1, :], gr[hb_:hb_ + 1, :])
            decay = jnp.where(incl, jnp.exp(jnp.where(incl, gc2 - gr2, 0.0)), 0.0)
            lows.append(jnp.where(strict, jnp.where(fwd, betas[0], betas[1]) * kk * decay, 0.0))
            attn_ref[rs, h * LANE:(h + 1) * LANE] = (qk * decay).astype(BF16)
            order.append((rs, h))
            br2 = jnp.where(fwd[0:1], br[h:h + 1, :], br[hb_:hb_ + 1, :])
            v16 = v.astype(BF16)
            zeros = jnp.zeros_like(v16)
            rhss.append((br2, br2 * jnp.exp(gr2),
                         jnp.concatenate([jnp.concatenate([v16, zeros], axis=1),
                                          jnp.concatenate([zeros, v16], axis=1)], axis=0),
                         jnp.concatenate([jnp.concatenate([k16, zeros], axis=1),
                                          jnp.concatenate([zeros, k16], axis=1)], axis=0)))
            for d in range(2):
                dh = d * N_HEADS_C + h
                gc = gcs[d]
                eg = jnp.exp(gc)
                g_last = gc[CHUNK - 1:CHUNK] if d == 0 else gc[0:1]
                cs = slice(dh * DK_C, (dh + 1) * DK_C)
                qg_ref[rs, cs] = (q * eg).astype(BF16)
                kd_ref[rs, cs] = (k * jnp.exp(g_last - gc)).astype(BF16)
                eg_ref[c, dh:dh + 1, :] = jnp.broadcast_to(jnp.exp(g_last), (1, LANE))
    solve()


def _gdn_local(cqkv, small, abt, conv_w, prow, pcol, l, seq):
    t = cqkv.shape[0]
    assert seq % TMG == 0
    nt = t // TMG
    nh8 = t // HALO
    cpg = TMG // CHUNK
    row = lambda i: (i, 0)
    dh = 2 * N_HEADS_C
    return pl.pallas_call(
        functools.partial(_gdn_local_kernel, seq=seq),
        grid=(nt,),
        in_specs=[pl.BlockSpec((TMG, QKV_C), row),
                  pl.BlockSpec((HALO, QKV_C), lambda i: (jnp.maximum(i * (TMG // HALO) - 1, 0), 0)),
                  pl.BlockSpec((HALO, QKV_C), lambda i: (jnp.minimum((i + 1) * (TMG // HALO), nh8 - 1), 0)),
                  pl.BlockSpec((TMG, LANE), row),
                  pl.BlockSpec((cpg, 16, CHUNK), lambda i: (i, 0, 0)),
                  _layer_spec(conv_w, l), _layer_spec(prow, l), _layer_spec(pcol, l)],
        out_specs=[pl.BlockSpec((TMG, dh * DV_C), row),
                   pl.BlockSpec((TMG, dh * DK_C), row),
                   pl.BlockSpec((TMG, dh * DK_C), row),
                   pl.BlockSpec((TMG, dh * DK_C), row),
                   pl.BlockSpec((TMG, dh * CHUNK), row),
                   pl.BlockSpec((cpg, dh, LANE), lambda i: (i, 0, 0))],
        out_shape=[jax.ShapeDtypeStruct((t, dh * DV_C), BF16),
                   jax.ShapeDtypeStruct((t, dh * DK_C), BF16),
                   jax.ShapeDtypeStruct((t, dh * DK_C), BF16),
                   jax.ShapeDtypeStruct((t, dh * DK_C), BF16),
                   jax.ShapeDtypeStruct((t, dh * CHUNK), BF16),
                   jax.ShapeDtypeStruct((t // CHUNK, dh, LANE), F32)],
        scratch_shapes=[pltpu.VMEM((TMG, QKV_C), F32), pltpu.VMEM((TMG, 16), F32)],
        compiler_params=pltpu.CompilerParams(dimension_semantics=("parallel",)),
        name="gdn_local",
    )(cqkv, cqkv, cqkv, small, abt, conv_w, prow, pcol)


def _gdn_scan_kernel(*refs, nt, ns, has_init, want_state, n_carry, layer, fill_depth):
    it = iter(refs)
    ins = [[next(it) for _ in range(6)] for _ in range(2)]
    s0_ref = next(it) if has_init else None
    for _ in range(n_carry):
        next(it)
    o_refs = [next(it), next(it)]
    st_ref = next(it) if want_state else None
    s_scr = next(it)
    j = pl.program_id(1)
    nh = N_HEADS_C
    nst = ns * 2 * nh

    @pl.when(j == 0)
    def _():
        if has_init:
            s_scr[...] = s0_ref[...].reshape(nst, DK_C, DV_C)
        else:
            s_scr[...] = jnp.zeros_like(s_scr)

    for step in range(CPT):
        chunk = lambda d: step if d == 0 else CPT - 1 - step

        def gather(idx, width):
            return jnp.stack([ins[d][idx][s, chunk(d) * CHUNK:(chunk(d) + 1) * CHUNK, h * width:(h + 1) * width]
                              for s in range(ns) for d in range(2) for h in range(nh)])

        u, w, qg, kd, attn = gather(0, DV_C), gather(1, DK_C), gather(2, DK_C), gather(3, DK_C), gather(4, LANE)
        eg = jnp.stack([ins[d][5][s, chunk(d), d * nh + h:d * nh + h + 1, :]
                        for s in range(ns) for d in range(2) for h in range(nh)])
        st = s_scr[...]
        sb = st.astype(BF16)
        v_new = u.astype(F32) - _bdot(w, sb)
        vb = v_new.astype(BF16)
        zeros = jnp.zeros((CHUNK, DV_C), BF16)
        vb2 = jnp.stack([jnp.concatenate([vb[i], zeros] if (i // nh) % 2 == 0 else [zeros, vb[i]], axis=0)
                         for i in range(nst)])
        o = _bdot(qg, sb) + _bdot(attn, vb2)
        s_scr[...] = st * eg + _bdot_tn(kd, vb)
        for s in range(ns):
            for d in range(2):
                for h in range(nh):
                    o_refs[d][s, chunk(d) * CHUNK:(chunk(d) + 1) * CHUNK, h * DV_C:(h + 1) * DV_C] = (
                        o[(s * 2 + d) * nh + h].astype(BF16))

    if want_state:
        @pl.when(j == nt - 1)
        def _():
            final = s_scr[...].reshape(ns, 2, nh, DK_C, DV_C)
            if fill_depth:
                for k in range(fill_depth):
                    st_ref[:, k] = final if k == layer else jnp.zeros_like(final)
            else:
                st_ref[...] = final


def _gdn_scan(u, w, qg, kd, attn, eg, s0, l, seq, state_out=None):
    t = u.shape[0]
    want_state = state_out is not None
    nt = seq // TM
    nb = t // seq
    ns = next(n for n in (SCAN_SEQS, 2, 1) if nb % n == 0)
    half = N_HEADS_C * DK_C
    has_init = s0 is not None
    by_seq = lambda a: a.reshape((nb, a.shape[0] // nb) + a.shape[1:])
    in_specs, args = [], []
    for d in range(2):
        tile = (lambda b, j: j) if d == 0 else (lambda b, j: nt - 1 - j)
        row = lambda b, j, d=d, tile=tile: (b, tile(b, j), d)
        row4 = lambda b, j, tile=tile: (b, tile(b, j), 0, 0)
        in_specs += [pl.BlockSpec((ns, TM, half), row)] * 4
        in_specs += [pl.BlockSpec((ns, TM, N_HEADS_C * LANE), lambda b, j, tile=tile: (b, tile(b, j), 0)),
                     pl.BlockSpec((ns, CPT, 2 * N_HEADS_C, LANE), row4)]
        args += [by_seq(a) for a in (u, w, qg, kd, attn, eg)]
    st_tail = (2, N_HEADS_C, DK_C, DV_C)
    if has_init:
        in_specs.append(pl.BlockSpec((ns, None) + st_tail, lambda b, j: (b, l, 0, 0, 0, 0)))
        args.append(s0)
    out_specs = [pl.BlockSpec((ns, TM, half), lambda b, j: (b, j, 0)),
                 pl.BlockSpec((ns, TM, half), lambda b, j: (b, nt - 1 - j, 0))]
    out_shape = [jax.ShapeDtypeStruct((nb, seq, half), BF16)] * 2
    aliases, n_carry, fill_depth = {}, 0, 0
    if want_state:
        depth, carry = state_out
        out_shape.append(jax.ShapeDtypeStruct((nb, depth) + st_tail, F32))
        if carry is None:
            fill_depth = depth
            out_specs.append(pl.BlockSpec((ns, depth) + st_tail, lambda b, j: (b, 0, 0, 0, 0, 0)))
        else:
            out_specs.append(pl.BlockSpec((ns, None) + st_tail, lambda b, j: (b, l, 0, 0, 0, 0)))
            n_carry = 1
            aliases[len(args)] = 2
            in_specs.append(pl.BlockSpec(memory_space=pl.ANY))
            args.append(carry)
    outs = pl.pallas_call(
        functools.partial(_gdn_scan_kernel, nt=nt, ns=ns, has_init=has_init, want_state=want_state,
                          n_carry=n_carry, layer=l, fill_depth=fill_depth),
        grid=(nb // ns, nt),
        in_specs=in_specs,
        out_specs=out_specs,
        out_shape=out_shape,
        input_output_aliases=aliases,
        scratch_shapes=[pltpu.VMEM((ns * 2 * N_HEADS_C, DK_C, DV_C), F32)],
        compiler_params=pltpu.CompilerParams(dimension_semantics=("parallel", "arbitrary")),
        name="gdn_scan",
    )(*args)
    return [outs[0].reshape(t, half), outs[1].reshape(t, half)] + list(outs[2:])


def _merge_kernel(x_ref, mod_ref, oa_ref, ob_ref, cf_ref, cb_ref, zc_ref, hb_ref, wg_ref, gn_ref,
                  wa_ref, wb_ref, wc_ref, wo_ref, fg_ref, o_ref, *, last):
    oc = cf_ref[...].astype(F32) + cb_ref[...].astype(F32)
    zc = zc_ref[...].astype(F32)
    gn = gn_ref[...]
    parts = []
    for h in range(N_HEADS_C):
        hs = slice(h * DV_C, (h + 1) * DV_C)
        och = oc[:, hs]
        och = och * lax.rsqrt(jnp.mean(och * och, axis=-1, keepdims=True) + EPS) * gn
        parts.append((och * _silu(zc[:, hs])).astype(BF16))
    ocz = jnp.concatenate(parts, axis=-1)
    pa = _dot(oa_ref[...], wa_ref[...])
    pb = _dot(ob_ref[...], wb_ref[...])
    pc = _dot(ocz, wc_ref[...])
    hb = hb_ref[...]
    ga = _sigmoid(_dot_nt(hb, wg_ref[0:D_MODEL, :]))
    gb = _sigmoid(_dot_nt(hb, wg_ref[D_MODEL:2 * D_MODEL, :]))
    gc = _sigmoid(_dot_nt(hb, wg_ref[2 * D_MODEL:, :]))
    y = _dot((ga * pa + gb * pb + gc * pc).astype(BF16), wo_ref[...])
    gate = mod_ref[0][:, 2 * D_MODEL:]
    xo = x_ref[...] + gate * y
    if last:
        xo = xo * lax.rsqrt(jnp.mean(xo * xo, axis=-1, keepdims=True) + EPS) * fg_ref[...]
    o_ref[...] = xo


def _merge(x2d, l, mod, mod_row_fn, oa, ob, cf, cb, z, hb, wp, gn, wa, wb, wc, wo, fg, last):
    t = x2d.shape[0]
    row = lambda i: (i, 0)
    assert P_GATES == 0
    return pl.pallas_call(
        functools.partial(_merge_kernel, last=last),
        grid=(t // TMD,),
        in_specs=[pl.BlockSpec((TMD, D_MODEL), row),
                  pl.BlockSpec((1, 1, 3 * D_MODEL), lambda i: (mod_row_fn(i), 0, 0)),
                  pl.BlockSpec((TMD, W_A), row),
                  pl.BlockSpec((TMD, W_B), row),
                  pl.BlockSpec((TMD, W_C), row),
                  pl.BlockSpec((TMD, W_C), row),
                  pl.BlockSpec((TMD, W_C), lambda i: (i, 2)),
                  pl.BlockSpec((TMD, D_MODEL), row),
                  pl.BlockSpec((None, W_GATES, D_MODEL), lambda i: (l, 0, 0), pipeline_mode=pl.Buffered(1)),
                  _layer_spec(gn, l), _layer_spec(wa, l), _layer_spec(wb, l), _layer_spec(wc, l),
                  _layer_spec(wo, l),
                  pl.BlockSpec((1, D_MODEL), lambda i: (0, 0))],
        out_specs=pl.BlockSpec((TMD, D_MODEL), row),
        out_shape=jax.ShapeDtypeStruct((t, D_MODEL), F32),
        compiler_params=pltpu.CompilerParams(dimension_semantics=("parallel",)),
        name="merge",
    )(x2d, mod, oa, ob, cf, cb, z, hb, wp, gn, wa, wb, wc, wo, fg)


def _rope_tables(n_tokens, rot_dim):
    rows = n_tokens // GRID_W
    row = np.repeat(np.arange(rows), GRID_W).astype(np.float32)
    col = np.tile(np.arange(GRID_W), rows).astype(np.float32)
    n_pairs = rot_dim // 4
    inv = (np.float32(ROPE_BASE) ** (-np.arange(n_pairs, dtype=np.float32) / np.float32(n_pairs))).astype(np.float32)
    ang = np.concatenate([row[:, None] * inv, col[:, None] * inv], axis=-1)
    c, s = np.cos(ang), np.sin(ang)
    return np.repeat(c, 2, axis=-1), np.stack([-s, s], axis=-1).reshape(n_tokens, rot_dim)


def _in_offsets():
    o = [0]
    for n in IN_SIZES:
        o.append(o[-1] + n)
    return o


def _relayout_moves():
    o = _in_offsets()
    order = [(o[12], W_GATES), (o[0], W_A), (o[1], LANE), (o[1] + HD_A, HD_A), (o[1], HD_A), (o[2], LANE),
             (o[3], W_A), (o[7], W_B), (o[11], W_C), (o[4], Q_RANK_B), (o[5], KV_RANK_B),
             (None, S_KPE), (o[6], QK_ROPE_B), (o[9], 4 * N_HEADS_C), (None, LANE - S_B - 8), (o[8], QKV_C)]
    moves, dst = [], 0
    for src, n in order:
        moves.append((src, dst, n))
        dst += n
    assert dst == P_END
    return moves


def _relayout_kernel(w_ref, o_ref, wab_ref, wvat_ref):
    for src, dst, n in _relayout_moves():
        if src is None:
            o_ref[dst:dst + n, :] = jnp.zeros((n, o_ref.shape[1]), BF16)
        else:
            o_ref[dst:dst + n, :] = w_ref[src:src + n, :].astype(BF16)
    o = _in_offsets()
    wab_ref[...] = w_ref[o[9]:o[11], :].astype(BF16)
    wvat_ref[...] = w_ref[o[2]:o[3], :].astype(BF16)


def _relayout_w_in(w_in):
    w_t = jnp.swapaxes(w_in, 1, 2)
    depth, width, _ = w_t.shape
    cols = 128
    nab, nv = 4 * N_HEADS_C, N_KV_A * HD_A
    col = lambda l, i: (l, 0, i)
    return pl.pallas_call(
        _relayout_kernel,
        grid=(depth, D_MODEL // cols),
        in_specs=[pl.BlockSpec((None, width, cols), col)],
        out_specs=[pl.BlockSpec((None, P_END, cols), col),
                   pl.BlockSpec((None, nab, cols), col),
                   pl.BlockSpec((None, nv, cols), col)],
        out_shape=[jax.ShapeDtypeStruct((depth, P_END, D_MODEL), BF16),
                   jax.ShapeDtypeStruct((depth, nab, D_MODEL), BF16),
                   jax.ShapeDtypeStruct((depth, nv, D_MODEL), BF16)],
        name="w_in_relayout",
    )(w_t)


def _prep_weights(w_in, w_uq, w_ukv):
    depth = w_in.shape[0]
    wp, wab, wvat = _relayout_w_in(w_in)
    hd = QK_NOPE_B + QK_ROPE_B
    wuq = jnp.pad(w_uq.reshape(depth, Q_RANK_B, N_HEADS_B, hd), ((0, 0), (0, 0), (0, 0), (0, MLA_HW - hd)))
    wuq = wuq.reshape(depth, Q_RANK_B, HB_COLS).astype(BF16)
    kv = w_ukv.reshape(depth, KV_RANK_B, N_HEADS_B, QK_NOPE_B + V_HD_B)
    wk = jnp.pad(kv[..., :QK_NOPE_B], ((0, 0), (0, 0), (0, 0), (0, MLA_HW - QK_NOPE_B)))
    wukv = wk.reshape(depth, KV_RANK_B, HB_COLS).astype(BF16)
    wvbt = jnp.swapaxes(kv[..., QK_NOPE_B:].reshape(depth, KV_RANK_B, W_B), 1, 2).astype(BF16)
    return wp, wab, wvat, wuq, wukv, wvbt


def _cache_tiles_a(kx, vx):
    k0, k1 = kx[..., 0, :], kx[..., 1, :]
    z = jnp.zeros_like(k0)
    ka = jnp.concatenate([k0, z, z, k0, k1, z, z, k1], axis=-1).astype(BF16)
    vt = jnp.transpose(vx, (0, 1, 3, 4, 2))
    vt = jnp.concatenate([vt, jnp.ones(vt.shape[:3] + (VT_ONES, vt.shape[4]), vt.dtype)], axis=3)
    return ka, vt.reshape(vt.shape[:2] + (VAT_ROWS, vt.shape[4])).astype(BF16)


def kernel(x_prompt, x_sample, cache_attn_k, cache_attn_v, cache_mla_ckv, cache_mla_kpe, state_gdn, c, c_ctx,
           norm_g, w_ada, b_ada, w_in, attn_sink, mla_q_norm, mla_w_uq, mla_kv_norm, mla_w_ukv, gdn_conv,
           gdn_a_log, gdn_dt_bias, gdn_norm, w_branch_a, w_branch_b, w_branch_c, w_out, final_norm_g):
    depth = w_in.shape[0]
    nb_c, seq_c, _ = x_prompt.shape
    nb_l, seq_l, _ = x_sample.shape
    past = cache_attn_k.shape[2]
    assert P_END % LANE == 0 and seq_c % TM == 0 and seq_l % TMD == 0 and nb_l < 8 and TM == 2 * WINDOW
    assert (nb_c * seq_c) % TMD == 0 and TMD % TM == 0

    cond8 = jnp.zeros((8, D_MODEL), F32).at[:nb_l].set(c).at[nb_l].set(c_ctx)
    mod = _modulation(cond8, w_ada, b_ada).reshape(depth * 8, 1, 3 * D_MODEL)

    c_a, s_a = _rope_tables(seq_l, HD_A)
    c_b, s_b = _rope_tables(seq_l, QK_ROPE_B)
    pad_l, pad_r = S_KPE, LANE - S_KPE - QK_ROPE_B
    one, zero = np.ones((seq_l, 1), np.float32), np.zeros((seq_l, 1), np.float32)
    rope_tabs = tuple(jnp.asarray(a) for a in (
        np.tile(c_a, (1, LANE // HD_A)), np.tile(s_a, (1, LANE // HD_A)),
        np.concatenate([np.tile(one, (1, pad_l)), c_b, np.tile(one, (1, pad_r))], 1),
        np.concatenate([np.tile(zero, (1, pad_l)), s_b, np.tile(zero, (1, pad_r))], 1)))

    weights = _prep_weights(w_in, mla_w_uq, mla_w_ukv)
    wukv, wvbt = weights[4], weights[5]
    ng = norm_g.reshape(depth, 1, D_MODEL)
    qn = mla_q_norm.reshape(depth, 1, Q_RANK_B)
    kvn = mla_kv_norm.reshape(depth, 1, KV_RANK_B)
    sink = attn_sink.reshape(depth, 1, N_HEADS_A)
    prow = jnp.stack([gdn_a_log.reshape(depth, -1), gdn_dt_bias.reshape(depth, -1)], axis=1)
    pcol = jnp.swapaxes(prow, 1, 2)
    gn = gdn_norm.reshape(depth, 1, DV_C)
    wa, wb, wc, wo = (w.astype(BF16) for w in (w_branch_a, w_branch_b, w_branch_c, w_out))
    fg = final_norm_g.reshape(1, D_MODEL)
    kxa, vxa = _cache_tiles_a(cache_attn_k, cache_attn_v)
    kpex = jnp.pad(cache_mla_kpe, ((0, 0), (0, 0), (0, 0), (pad_l, pad_r)))

    tps_c, tps_l = seq_c // TM, seq_l // TM
    tpd_l = seq_l // TMD
    y_p = x_prompt.reshape(nb_c * seq_c, D_MODEL)
    y_s = x_sample.reshape(nb_l * seq_l, D_MODEL)
    new_cache = new_state = None
    for l in range(depth):
        last = l == depth - 1

        mod_row_c = lambda i, l=l: l * 8 + nb_l
        outs = _inproj(y_p, l, mod, mod_row_c, ng, weights, qn, kvn, None, tps_c, (depth, seq_c, new_cache))
        (qa, ka, vat, z, hb, qb, kb, vbt, small, cqkv, abt), new_cache = outs[:11], tuple(outs[11:])
        oa, ob = _attn_ctx(qa, ka, vat, qb, kb, vbt, z, sink, l, seq_c)
        u, w, qg, kd, attn, eg = _gdn_local(cqkv, small, abt, gdn_conv, prow, pcol, l, seq_c)
        cf, cb, new_state = _gdn_scan(u, w, qg, kd, attn, eg, None, l, seq_c, (depth, new_state))
        y_p = _merge(y_p, l, mod, mod_row_c, oa, ob, cf, cb, z, hb, weights[0], gn, wa, wb, wc, wo, fg, last)

        mod_row_l = lambda i, l=l: l * 8 + i // tpd_l
        (qa, ka, vat, z, hb, qb, kb, vbt, small, cqkv, abt, qnorm) = _inproj(
            y_s, l, mod, mod_row_l, ng, weights, qn, kvn, rope_tabs, tpd_l)
        oa = _attn_a_lat(qa, qnorm, ka, vat, z, sink, kxa, vxa, l, seq_l)
        kxb, vxb, knx = _kvup(cache_mla_ckv, kpex, wukv, wvbt, l)
        ob = _attn_b_lat(qb, qnorm, kb, vbt, z, kxb, vxb, knx, seq_l, 2 * TM, N_HEADS_B // 4)
        u, w, qg, kd, attn, eg = _gdn_local(cqkv, small, abt, gdn_conv, prow, pcol, l, seq_l)
        cf, cb = _gdn_scan(u, w, qg, kd, attn, eg, state_gdn, l, seq_l)
        y_s = _merge(y_s, l, mod, mod_row_l, oa, ob, cf, cb, z, hb, weights[0], gn, wa, wb, wc, wo, fg, last)

    new_k, new_v, new_ckv, new_kpe = new_cache
    kv_shape = (nb_c, depth, seq_c, N_KV_A, HD_A)
    return (y_p.reshape(nb_c, seq_c, D_MODEL), y_s.reshape(nb_l, seq_l, D_MODEL),
            new_k.reshape(kv_shape), new_v.reshape(kv_shape), new_ckv, new_kpe, new_state)
```
